```python
import jax, jax.numpy as jnp
from jax import lax
import numpy as np

D_MODEL = 1024
BATCH = 16
SEQ = 4096
DEPTH = 2

D_PLE = 256
D_SSM = 256
D_DN = 512
D_SG = 256
D_MIX = D_SSM + D_DN + D_SG
SSM_GROUP = 16
SSM_GROUPS = D_SSM // SSM_GROUP
SSM_STATE = 64
DN_HEADS = 4
DN_HEAD_DIM = D_DN // DN_HEADS
DN_CONV = 4
DN_CHUNK = 64
SG_HEADS = 4
SG_HEAD_DIM = D_SG // SG_HEADS
SG_CHUNK = 128
EPS = 1e-6
SPLITS = (D_SSM, D_SSM, 3 * D_DN, DN_HEADS, DN_HEADS, D_DN, D_SG, D_SG, D_SG)
D_IN = 2 * D_SSM + 4 * D_DN + 2 * DN_HEADS + 3 * D_SG

kernel_name = "hymba_style_s5_gdn_gmlp_ple"


def rms_norm(x, g):
    xf = x.astype(jnp.float32)
    y = xf * lax.rsqrt(jnp.mean(xf * xf, axis=-1, keepdims=True) + EPS)
    return (y * g.astype(jnp.float32)).astype(x.dtype)


def layer_norm(x, g, b):
    xf = x.astype(jnp.float32)
    mu = jnp.mean(xf, axis=-1, keepdims=True)
    xc = xf - mu
    y = xc * lax.rsqrt(jnp.mean(xc * xc, axis=-1, keepdims=True) + EPS)
    return (y * g.astype(jnp.float32) + b.astype(jnp.float32)).astype(x.dtype)


def l2_normalize(x):
    return x * lax.rsqrt(jnp.sum(x * x, axis=-1, keepdims=True) + EPS)


def split_cols(z):
    idx = np.cumsum(np.array(SPLITS))[:-1].tolist()
    return jnp.split(z, idx, axis=-1)


def complex_linear_combine(e1, e2):
    a1r, a1i, b1r, b1i = e1
    a2r, a2i, b2r, b2i = e2
    ar = a2r * a1r - a2i * a1i
    ai = a2r * a1i + a2i * a1r
    br = a2r * b1r - a2i * b1i + b2r
    bi = a2r * b1i + a2i * b1r + b2i
    return (ar, ai, br, bi)


def s5_branch(u, a_re, a_im, b_re, b_im, c_re, c_im, d_skip, log_step, w_glu, b_glu):
    bsz, seq, _ = u.shape
    f32 = jnp.float32
    uf = u.astype(f32).reshape(bsz, seq, SSM_GROUPS, SSM_GROUP)
    step = jnp.exp(log_step.astype(f32))[:, None]
    ar, ai = a_re.astype(f32), a_im.astype(f32)
    mag = jnp.exp(ar * step)
    lam_re = mag * jnp.cos(ai * step)
    lam_im = mag * jnp.sin(ai * step)
    den = ar * ar + ai * ai
    nr, ni = lam_re - 1.0, lam_im
    f_re = (nr * ar + ni * ai) / den
    f_im = (ni * ar - nr * ai) / den
    br, bi = b_re.astype(f32), b_im.astype(f32)
    bbar_re = f_re[..., None] * br - f_im[..., None] * bi
    bbar_im = f_re[..., None] * bi + f_im[..., None] * br
    bu_re = jnp.einsum('bsgc,gnc->bsgn', uf, bbar_re)
    bu_im = jnp.einsum('bsgc,gnc->bsgn', uf, bbar_im)
    lam_re_s = jnp.broadcast_to(lam_re, (1, seq, SSM_GROUPS, SSM_STATE))
    lam_im_s = jnp.broadcast_to(lam_im, (1, seq, SSM_GROUPS, SSM_STATE))
    _, _, h_re, h_im = lax.associative_scan(
        complex_linear_combine, (lam_re_s, lam_im_s, bu_re, bu_im), axis=1)
    y = (jnp.einsum('bsgn,gcn->bsgc', h_re, c_re.astype(f32))
         - jnp.einsum('bsgn,gcn->bsgc', h_im, c_im.astype(f32))
         + d_skip.astype(f32) * uf)
    y = jax.nn.gelu(y.reshape(bsz, seq, D_SSM))
    y = y * jax.nn.sigmoid(jnp.einsum('bse,ef->bsf', y, w_glu.astype(f32)) + b_glu.astype(f32))
    return y.astype(u.dtype)


def causal_depthwise_conv(x, w):
    k_len, ch = w.shape
    return lax.conv_general_dilated(
        x, w[:, None, :], window_strides=(1,), padding=[(k_len - 1, 0)],
        dimension_numbers=('NWC', 'WIO', 'NWC'), feature_group_count=ch)


def gated_deltanet_branch(qkv, a_in, b_in, conv_w, a_log, dt_bias, norm_g):
    bsz, seq, _ = qkv.shape
    f32 = jnp.float32
    H, Dh, C = DN_HEADS, DN_HEAD_DIM, DN_CHUNK
    nc = seq // C
    qkv = jax.nn.silu(causal_depthwise_conv(qkv.astype(f32), conv_w.astype(f32)))
    q, k, v = jnp.split(qkv, 3, axis=-1)

    def to_chunks(t):
        return t.reshape(bsz, nc, C, H, Dh).transpose(0, 3, 1, 2, 4)

    def head_scalars(t):
        return t.reshape(bsz, nc, C, H).transpose(0, 3, 1, 2)

    q = l2_normalize(to_chunks(q)) * (Dh ** -0.5)
    k = l2_normalize(to_chunks(k))
    v = to_chunks(v)
    beta = head_scalars(jax.nn.sigmoid(b_in.astype(f32)))
    g = head_scalars(-jnp.exp(a_log.astype(f32)) * jax.nn.softplus(a_in.astype(f32) + dt_bias.astype(f32)))
    gc = jnp.cumsum(g, axis=-1)
    causal = jnp.tril(jnp.ones((C, C), dtype=bool))
    strict = jnp.tril(jnp.ones((C, C), dtype=bool), k=-1)
    diff = gc[..., :, None] - gc[..., None, :]
    decay = jnp.where(causal, jnp.exp(jnp.where(causal, diff, 0.0)), 0.0)
    kb = k * beta[..., None]
    vb = v * beta[..., None]
    m = jnp.where(strict, jnp.einsum('bhnid,bhnjd->bhnij', kb, k) * decay, 0.0)
    rhs = jnp.concatenate([vb, kb * jnp.exp(gc)[..., None]], axis=-1)
    sol = lax.linalg.triangular_solve(m, rhs, left_side=True, lower=True, unit_diagonal=True)
    value, k_cd = sol[..., :Dh], sol[..., Dh:]
    attn = jnp.einsum('bhnid,bhnjd->bhnij', q, k) * decay
    q_dec = q * jnp.exp(gc)[..., None]
    k_dec = k * jnp.exp(gc[..., -1:] - gc)[..., None]
    last = jnp.exp(gc[..., -1])
    xs = (jnp.moveaxis(value, 2, 0), jnp.moveaxis(k_cd, 2, 0), jnp.moveaxis(attn, 2, 0),
          jnp.moveaxis(q_dec, 2, 0), jnp.moveaxis(k_dec, 2, 0), jnp.moveaxis(last, 2, 0))

    def chunk_step(state, inp):
        val, kcd, att, qd, kd, dl = inp
        v_new = val - jnp.einsum('bhcd,bhde->bhce', kcd, state)
        o = jnp.einsum('bhcd,bhde->bhce', qd, state) + jnp.einsum('bhij,bhje->bhie', att, v_new)
        state = state * dl[..., None, None] + jnp.einsum('bhcd,bhce->bhde', kd, v_new)
        return state, o

    s0 = jnp.zeros((bsz, H, Dh, Dh), f32)
    _, o = lax.scan(chunk_step, s0, xs)
    o = o.transpose(1, 0, 3, 2, 4).reshape(bsz, seq, H, Dh)
    o = rms_norm(o, norm_g)
    return o.reshape(bsz, seq, D_DN).astype(a_in.dtype)


def spatial_gating_branch(u, v, ln_g, ln_b, w_sp, b_sp):
    bsz, seq, _ = u.shape
    nch = seq // SG_CHUNK
    u = jax.nn.gelu(u)
    v = layer_norm(jax.nn.gelu(v), ln_g, ln_b)
    vh = v.reshape(bsz, nch, SG_CHUNK, SG_HEADS, SG_HEAD_DIM)
    causal = jnp.tril(jnp.ones((SG_CHUNK, SG_CHUNK), dtype=bool))
    w = jnp.where(causal, w_sp, 0.0)
    s = jnp.einsum('hts,bnshc->bnthc', w, vh) + jnp.transpose(b_sp)[:, :, None]
    return u * s.reshape(bsz, seq, D_SG)


def _fwd_setup_inputs(seed: int = 0) -> dict:
    key = jax.random.key(seed)
    ks = jax.random.split(key, 32)
    f32 = jnp.float32
    L, D = DEPTH, D_MODEL
    G, N, Cg = SSM_GROUPS, SSM_STATE, SSM_GROUP

    def nrm(k, shape, scale):
        return scale * jax.random.normal(k, shape, f32)

    x = jax.random.normal(ks[0], (BATCH, SEQ, D), f32)
    p = jax.random.normal(ks[1], (DEPTH, BATCH, SEQ, D_PLE), f32)
    norm_g = 1.0 + nrm(ks[2], (L, D), 0.02)
    w_in = nrm(ks[3], (L, D, D_IN), D ** -0.5)
    ssm_a_re = -0.5 + nrm(ks[4], (L, G, N), 0.01)
    ssm_a_im = jnp.pi * jnp.arange(N, dtype=f32) + nrm(ks[5], (L, G, N), 0.01)
    ssm_b_re = nrm(ks[6], (L, G, N, Cg), Cg ** -0.5)
    ssm_b_im = nrm(ks[7], (L, G, N, Cg), Cg ** -0.5)
    ssm_c_re = nrm(ks[8], (L, G, Cg, N), N ** -0.5)
    ssm_c_im = nrm(ks[9], (L, G, Cg, N), N ** -0.5)
    ssm_d = nrm(ks[10], (L, G, Cg), 1.0)
    ssm_log_step = jax.random.uniform(ks[11], (L, G), f32, np.log(1e-3), np.log(1e-1))
    ssm_w_glu = nrm(ks[12], (L, D_SSM, D_SSM), D_SSM ** -0.5)
    ssm_b_glu = nrm(ks[13], (L, D_SSM), 0.01)
    dn_conv_w = nrm(ks[14], (L, DN_CONV, 3 * D_DN), DN_CONV ** -0.5)
    dn_a_log = jnp.log(jax.random.uniform(ks[15], (L, DN_HEADS), f32, 1.0, 16.0))
    dt = jnp.exp(jax.random.uniform(ks[16], (L, DN_HEADS), f32, np.log(1e-3), np.log(1e-1)))
    dn_dt_bias = dt + jnp.log(-jnp.expm1(-dt))
    dn_norm_g = 1.0 + nrm(ks[17], (L, DN_HEAD_DIM), 0.02)
    sg_ln_g = 1.0 + nrm(ks[18], (L, D_SG), 0.02)
    sg_ln_b = nrm(ks[19], (L, D_SG), 0.01)
    sg_w = nrm(ks[20], (L, SG_HEADS, SG_CHUNK, SG_CHUNK), SG_CHUNK ** -0.5)
    sg_b = 1.0 + nrm(ks[21], (L, SG_HEADS, SG_CHUNK), 0.02)
    w_out = nrm(ks[22], (L, D_MIX, D), D_MIX ** -0.5)
    ple_norm_g = 1.0 + nrm(ks[23], (L, D), 0.02)
    w_ple_gate = nrm(ks[24], (L, D, D), D ** -0.5)
    w_ple = nrm(ks[25], (L, D_PLE, D), D_PLE ** -0.5)
    final_norm_g = 1.0 + nrm(ks[26], (D,), 0.02)
    return {"x": x, "p": p, "norm_g": norm_g, "w_in": w_in,
            "ssm_a_re": ssm_a_re, "ssm_a_im": ssm_a_im, "ssm_b_re": ssm_b_re, "ssm_b_im": ssm_b_im,
            "ssm_c_re": ssm_c_re, "ssm_c_im": ssm_c_im, "ssm_d": ssm_d, "ssm_log_step": ssm_log_step,
            "ssm_w_glu": ssm_w_glu, "ssm_b_glu": ssm_b_glu,
            "dn_conv_w": dn_conv_w, "dn_a_log": dn_a_log, "dn_dt_bias": dn_dt_bias, "dn_norm_g": dn_norm_g,
            "sg_ln_g": sg_ln_g, "sg_ln_b": sg_ln_b, "sg_w": sg_w, "sg_b": sg_b,
            "w_out": w_out, "ple_norm_g": ple_norm_g, "w_ple_gate": w_ple_gate, "w_ple": w_ple,
            "final_norm_g": final_norm_g}


def _fwd_reference(x, p, norm_g, w_in, ssm_a_re, ssm_a_im, ssm_b_re, ssm_b_im, ssm_c_re, ssm_c_im,
              ssm_d, ssm_log_step, ssm_w_glu, ssm_b_glu, dn_conv_w, dn_a_log, dn_dt_bias, dn_norm_g,
              sg_ln_g, sg_ln_b, sg_w, sg_b, w_out, ple_norm_g, w_ple_gate, w_ple, final_norm_g):
    for i in range(DEPTH):
        h = rms_norm(x, norm_g[i])
        z = jnp.einsum('bsd,de->bse', h, w_in[i])
        u_ssm, g_ssm, qkv, a_dn, b_dn, g_dn, u_sg, v_sg, g_sg = split_cols(z)
        y_ssm = s5_branch(u_ssm, ssm_a_re[i], ssm_a_im[i], ssm_b_re[i], ssm_b_im[i],
                          ssm_c_re[i], ssm_c_im[i], ssm_d[i], ssm_log_step[i],
                          ssm_w_glu[i], ssm_b_glu[i]) * jax.nn.silu(g_ssm)
        y_dn = gated_deltanet_branch(qkv, a_dn, b_dn, dn_conv_w[i], dn_a_log[i], dn_dt_bias[i],
                                     dn_norm_g[i]) * jax.nn.silu(g_dn)
        y_sg = spatial_gating_branch(u_sg, v_sg, sg_ln_g[i], sg_ln_b[i], sg_w[i], sg_b[i]) * jax.nn.silu(g_sg)
        y = jnp.concatenate([y_ssm, y_dn, y_sg], axis=-1)
        x = x + jnp.einsum('bse,ed->bsd', y, w_out[i])
        gate = jax.nn.sigmoid(jnp.einsum('bsd,de->bse', rms_norm(x, ple_norm_g[i]), w_ple_gate[i]))
        x = x + gate * jnp.einsum('bsk,kd->bsd', p[i], w_ple[i])
    return rms_norm(x, final_norm_g)


import jax as _jax
import jax.numpy as _jnp

TWIN_FORMAT = 'train_step'
FWD_PARAMS = ['x', 'p', 'norm_g', 'w_in', 'ssm_a_re', 'ssm_a_im', 'ssm_b_re', 'ssm_b_im', 'ssm_c_re', 'ssm_c_im', 'ssm_d', 'ssm_log_step', 'ssm_w_glu', 'ssm_b_glu', 'dn_conv_w', 'dn_a_log', 'dn_dt_bias', 'dn_norm_g', 'sg_ln_g', 'sg_ln_b', 'sg_w', 'sg_b', 'w_out', 'ple_norm_g', 'w_ple_gate', 'w_ple', 'final_norm_g']
TWIN_WEIGHTS = ['norm_g', 'w_in', 'ssm_a_re', 'ssm_a_im', 'ssm_b_re', 'ssm_b_im', 'ssm_c_re', 'ssm_c_im', 'ssm_d', 'ssm_log_step', 'ssm_w_glu', 'ssm_b_glu', 'dn_conv_w', 'dn_a_log', 'dn_dt_bias', 'dn_norm_g', 'sg_ln_g', 'sg_ln_b', 'sg_w', 'sg_b', 'w_out', 'ple_norm_g', 'w_ple_gate', 'w_ple', 'final_norm_g']
TWIN_DIFF_INPUT = 'x'
TWIN_INPUTS = ['x', 'p', 'norm_g', 'w_in', 'ssm_a_re', 'ssm_a_im', 'ssm_b_re', 'ssm_b_im', 'ssm_c_re', 'ssm_c_im', 'ssm_d', 'ssm_log_step', 'ssm_w_glu', 'ssm_b_glu', 'dn_conv_w', 'dn_a_log', 'dn_dt_bias', 'dn_norm_g', 'sg_ln_g', 'sg_ln_b', 'sg_w', 'sg_b', 'w_out', 'ple_norm_g', 'w_ple_gate', 'w_ple', 'final_norm_g', 'loss_target', 'm_norm_g', 'm_w_in', 'm_ssm_a_re', 'm_ssm_a_im', 'm_ssm_b_re', 'm_ssm_b_im', 'm_ssm_c_re', 'm_ssm_c_im', 'm_ssm_d', 'm_ssm_log_step', 'm_ssm_w_glu', 'm_ssm_b_glu', 'm_dn_conv_w', 'm_dn_a_log', 'm_dn_dt_bias', 'm_dn_norm_g', 'm_sg_ln_g', 'm_sg_ln_b', 'm_sg_w', 'm_sg_b', 'm_w_out', 'm_ple_norm_g', 'm_w_ple_gate', 'm_w_ple', 'm_final_norm_g', 'v_norm_g', 'v_w_in', 'v_ssm_a_re', 'v_ssm_a_im', 'v_ssm_b_re', 'v_ssm_b_im', 'v_ssm_c_re', 'v_ssm_c_im', 'v_ssm_d', 'v_ssm_log_step', 'v_ssm_w_glu', 'v_ssm_b_glu', 'v_dn_conv_w', 'v_dn_a_log', 'v_dn_dt_bias', 'v_dn_norm_g', 'v_sg_ln_g', 'v_sg_ln_b', 'v_sg_w', 'v_sg_b', 'v_w_out', 'v_ple_norm_g', 'v_w_ple_gate', 'v_w_ple', 'v_final_norm_g']
TWIN_OUTPUTS = ['loss', 'grad_x', 'grad_norm_g', 'grad_w_in', 'grad_ssm_a_re', 'grad_ssm_a_im', 'grad_ssm_b_re', 'grad_ssm_b_im', 'grad_ssm_c_re', 'grad_ssm_c_im', 'grad_ssm_d', 'grad_ssm_log_step', 'grad_ssm_w_glu', 'grad_ssm_b_glu', 'grad_dn_conv_w', 'grad_dn_a_log', 'grad_dn_dt_bias', 'grad_dn_norm_g', 'grad_sg_ln_g', 'grad_sg_ln_b', 'grad_sg_w', 'grad_sg_b', 'grad_w_out', 'grad_ple_norm_g', 'grad_w_ple_gate', 'grad_w_ple', 'grad_final_norm_g', 'delta_norm_g', 'delta_w_in', 'delta_ssm_a_re', 'delta_ssm_a_im', 'delta_ssm_b_re', 'delta_ssm_b_im', 'delta_ssm_c_re', 'delta_ssm_c_im', 'delta_ssm_d', 'delta_ssm_log_step', 'delta_ssm_w_glu', 'delta_ssm_b_glu', 'delta_dn_conv_w', 'delta_dn_a_log', 'delta_dn_dt_bias', 'delta_dn_norm_g', 'delta_sg_ln_g', 'delta_sg_ln_b', 'delta_sg_w', 'delta_sg_b', 'delta_w_out', 'delta_ple_norm_g', 'delta_w_ple_gate', 'delta_w_ple', 'delta_final_norm_g', 'new_m_norm_g', 'new_m_w_in', 'new_m_ssm_a_re', 'new_m_ssm_a_im', 'new_m_ssm_b_re', 'new_m_ssm_b_im', 'new_m_ssm_c_re', 'new_m_ssm_c_im', 'new_m_ssm_d', 'new_m_ssm_log_step', 'new_m_ssm_w_glu', 'new_m_ssm_b_glu', 'new_m_dn_conv_w', 'new_m_dn_a_log', 'new_m_dn_dt_bias', 'new_m_dn_norm_g', 'new_m_sg_ln_g', 'new_m_sg_ln_b', 'new_m_sg_w', 'new_m_sg_b', 'new_m_w_out', 'new_m_ple_norm_g', 'new_m_w_ple_gate', 'new_m_w_ple', 'new_m_final_norm_g', 'new_v_norm_g', 'new_v_w_in', 'new_v_ssm_a_re', 'new_v_ssm_a_im', 'new_v_ssm_b_re', 'new_v_ssm_b_im', 'new_v_ssm_c_re', 'new_v_ssm_c_im', 'new_v_ssm_d', 'new_v_ssm_log_step', 'new_v_ssm_w_glu', 'new_v_ssm_b_glu', 'new_v_dn_conv_w', 'new_v_dn_a_log', 'new_v_dn_dt_bias', 'new_v_dn_norm_g', 'new_v_sg_ln_g', 'new_v_sg_ln_b', 'new_v_sg_w', 'new_v_sg_b', 'new_v_w_out', 'new_v_ple_norm_g', 'new_v_w_ple_gate', 'new_v_w_ple', 'new_v_final_norm_g']
TWIN_LEAF_KINDS = {'loss': 'loss', 'grad_x': 'grad_x', 'grad_norm_g': 'grad_w', 'grad_w_in': 'grad_w', 'grad_ssm_a_re': 'grad_w', 'grad_ssm_a_im': 'grad_w', 'grad_ssm_b_re': 'grad_w', 'grad_ssm_b_im': 'grad_w', 'grad_ssm_c_re': 'grad_w', 'grad_ssm_c_im': 'grad_w', 'grad_ssm_d': 'grad_w', 'grad_ssm_log_step': 'grad_w', 'grad_ssm_w_glu': 'grad_w', 'grad_ssm_b_glu': 'grad_w', 'grad_dn_conv_w': 'grad_w', 'grad_dn_a_log': 'grad_w', 'grad_dn_dt_bias': 'grad_w', 'grad_dn_norm_g': 'grad_w', 'grad_sg_ln_g': 'grad_w', 'grad_sg_ln_b': 'grad_w', 'grad_sg_w': 'grad_w', 'grad_sg_b': 'grad_w', 'grad_w_out': 'grad_w', 'grad_ple_norm_g': 'grad_w', 'grad_w_ple_gate': 'grad_w', 'grad_w_ple': 'grad_w', 'grad_final_norm_g': 'grad_w', 'delta_norm_g': 'delta_w', 'delta_w_in': 'delta_w', 'delta_ssm_a_re': 'delta_w', 'delta_ssm_a_im': 'delta_w', 'delta_ssm_b_re': 'delta_w', 'delta_ssm_b_im': 'delta_w', 'delta_ssm_c_re': 'delta_w', 'delta_ssm_c_im': 'delta_w', 'delta_ssm_d': 'delta_w', 'delta_ssm_log_step': 'delta_w', 'delta_ssm_w_glu': 'delta_w', 'delta_ssm_b_glu': 'delta_w', 'delta_dn_conv_w': 'delta_w', 'delta_dn_a_log': 'delta_w', 'delta_dn_dt_bias': 'delta_w', 'delta_dn_norm_g': 'delta_w', 'delta_sg_ln_g': 'delta_w', 'delta_sg_ln_b': 'delta_w', 'delta_sg_w': 'delta_w', 'delta_sg_b': 'delta_w', 'delta_w_out': 'delta_w', 'delta_ple_norm_g': 'delta_w', 'delta_w_ple_gate': 'delta_w', 'delta_w_ple': 'delta_w', 'delta_final_norm_g': 'delta_w', 'new_m_norm_g': 'new_m', 'new_m_w_in': 'new_m', 'new_m_ssm_a_re': 'new_m', 'new_m_ssm_a_im': 'new_m', 'new_m_ssm_b_re': 'new_m', 'new_m_ssm_b_im': 'new_m', 'new_m_ssm_c_re': 'new_m', 'new_m_ssm_c_im': 'new_m', 'new_m_ssm_d': 'new_m', 'new_m_ssm_log_step': 'new_m', 'new_m_ssm_w_glu': 'new_m', 'new_m_ssm_b_glu': 'new_m', 'new_m_dn_conv_w': 'new_m', 'new_m_dn_a_log': 'new_m', 'new_m_dn_dt_bias': 'new_m', 'new_m_dn_norm_g': 'new_m', 'new_m_sg_ln_g': 'new_m', 'new_m_sg_ln_b': 'new_m', 'new_m_sg_w': 'new_m', 'new_m_sg_b': 'new_m', 'new_m_w_out': 'new_m', 'new_m_ple_norm_g': 'new_m', 'new_m_w_ple_gate': 'new_m', 'new_m_w_ple': 'new_m', 'new_m_final_norm_g': 'new_m', 'new_v_norm_g': 'new_v', 'new_v_w_in': 'new_v', 'new_v_ssm_a_re': 'new_v', 'new_v_ssm_a_im': 'new_v', 'new_v_ssm_b_re': 'new_v', 'new_v_ssm_b_im': 'new_v', 'new_v_ssm_c_re': 'new_v', 'new_v_ssm_c_im': 'new_v', 'new_v_ssm_d': 'new_v', 'new_v_ssm_log_step': 'new_v', 'new_v_ssm_w_glu': 'new_v', 'new_v_ssm_b_glu': 'new_v', 'new_v_dn_conv_w': 'new_v', 'new_v_dn_a_log': 'new_v', 'new_v_dn_dt_bias': 'new_v', 'new_v_dn_norm_g': 'new_v', 'new_v_sg_ln_g': 'new_v', 'new_v_sg_ln_b': 'new_v', 'new_v_sg_w': 'new_v', 'new_v_sg_b': 'new_v', 'new_v_w_out': 'new_v', 'new_v_ple_norm_g': 'new_v', 'new_v_w_ple_gate': 'new_v', 'new_v_w_ple': 'new_v', 'new_v_final_norm_g': 'new_v'}


def _forward(args):
    return _fwd_reference(*[args[k] for k in FWD_PARAMS])


def _output_shape():
    out = _jax.eval_shape(lambda: _forward(_fwd_setup_inputs(0)))
    return out.shape, out.dtype

N_MICROBATCH = 1
ADAM_LR = 0.001
ADAM_B1 = 0.9
ADAM_B2 = 0.999
ADAM_EPS = 1e-08
ADAM_WD = 0.01
ADAM_STEP = 10
PER_EXAMPLE_BATCH_AXIS = {'x': 0, 'p': 1, 'loss_target': 0}
SHARED_INPUTS = []
_WEIGHT_DTYPES = {'norm_g': _jnp.float32, 'w_in': _jnp.float32, 'ssm_a_re': _jnp.float32, 'ssm_a_im': _jnp.float32, 'ssm_b_re': _jnp.float32, 'ssm_b_im': _jnp.float32, 'ssm_c_re': _jnp.float32, 'ssm_c_im': _jnp.float32, 'ssm_d': _jnp.float32, 'ssm_log_step': _jnp.float32, 'ssm_w_glu': _jnp.float32, 'ssm_b_glu': _jnp.float32, 'dn_conv_w': _jnp.float32, 'dn_a_log': _jnp.float32, 'dn_dt_bias': _jnp.float32, 'dn_norm_g': _jnp.float32, 'sg_ln_g': _jnp.float32, 'sg_ln_b': _jnp.float32, 'sg_w': _jnp.float32, 'sg_b': _jnp.float32, 'w_out': _jnp.float32, 'ple_norm_g': _jnp.float32, 'w_ple_gate': _jnp.float32, 'w_ple': _jnp.float32, 'final_norm_g': _jnp.float32}
MOMENT_SCALE = {'norm_g': 1.737666e-01, 'w_in': 8.884642e-02, 'ssm_a_re': 5.354228e-03, 'ssm_a_im': 6.846537e-03, 'ssm_b_re': 2.227799e-03, 'ssm_b_im': 2.335433e-03, 'ssm_c_re': 4.604434e-03, 'ssm_c_im': 4.713971e-03, 'ssm_d': 5.053756e-02, 'ssm_log_step': 2.365449e+00, 'ssm_w_glu': 1.421757e-02, 'ssm_b_glu': 2.152519e-02, 'dn_conv_w': 8.979318e-02, 'dn_a_log': 4.400050e-01, 'dn_dt_bias': 4.279982e-01, 'dn_norm_g': 3.247583e-01, 'sg_ln_g': 5.584841e-02, 'sg_ln_b': 5.508123e-02, 'sg_w': 3.827593e-02, 'sg_b': 5.250367e-02, 'w_out': 9.744917e-02, 'ple_norm_g': 4.477829e-02, 'w_ple_gate': 4.135320e-02, 'w_ple': 1.054150e-01, 'final_norm_g': 6.395536e+01}


def _to_microbatches(a, axis):
    t = _jnp.moveaxis(a, axis, 0)
    t = t.reshape((N_MICROBATCH, t.shape[0] // N_MICROBATCH) + t.shape[1:])
    return _jnp.moveaxis(t, 1, axis + 1)


def setup_inputs(seed: int = 0) -> dict:
    inp = _fwd_setup_inputs(seed)
    key = _jax.random.fold_in(_jax.random.key(seed), 7919)
    shape, _ = _output_shape()
    out = dict(inp)
    out["loss_target"] = _jax.random.normal(_jax.random.fold_in(key, 0), shape, _jnp.float32)
    for i, name in enumerate(TWIN_WEIGHTS):
        w = inp[name].astype(_jnp.float32)
        if MOMENT_SCALE is None:
            s = _jnp.sqrt(_jnp.mean(_jnp.square(w)) + 1e-30)
        else:
            s = MOMENT_SCALE[name]
        km, kv = _jax.random.split(_jax.random.fold_in(key, i + 1))
        out[name] = w
        out["m_" + name] = s * _jax.random.normal(km, w.shape, _jnp.float32)
        out["v_" + name] = (s * s) * _jax.random.uniform(kv, w.shape, _jnp.float32, 0.5, 1.5)
    if N_MICROBATCH > 1:
        for name, axis in PER_EXAMPLE_BATCH_AXIS.items():
            out[name] = _to_microbatches(out[name], axis)
    return {'x': out['x'], 'p': out['p'], 'norm_g': out['norm_g'], 'w_in': out['w_in'], 'ssm_a_re': out['ssm_a_re'], 'ssm_a_im': out['ssm_a_im'], 'ssm_b_re': out['ssm_b_re'], 'ssm_b_im': out['ssm_b_im'], 'ssm_c_re': out['ssm_c_re'], 'ssm_c_im': out['ssm_c_im'], 'ssm_d': out['ssm_d'], 'ssm_log_step': out['ssm_log_step'], 'ssm_w_glu': out['ssm_w_glu'], 'ssm_b_glu': out['ssm_b_glu'], 'dn_conv_w': out['dn_conv_w'], 'dn_a_log': out['dn_a_log'], 'dn_dt_bias': out['dn_dt_bias'], 'dn_norm_g': out['dn_norm_g'], 'sg_ln_g': out['sg_ln_g'], 'sg_ln_b': out['sg_ln_b'], 'sg_w': out['sg_w'], 'sg_b': out['sg_b'], 'w_out': out['w_out'], 'ple_norm_g': out['ple_norm_g'], 'w_ple_gate': out['w_ple_gate'], 'w_ple': out['w_ple'], 'final_norm_g': out['final_norm_g'], 'loss_target': out['loss_target'], 'm_norm_g': out['m_norm_g'], 'm_w_in': out['m_w_in'], 'm_ssm_a_re': out['m_ssm_a_re'], 'm_ssm_a_im': out['m_ssm_a_im'], 'm_ssm_b_re': out['m_ssm_b_re'], 'm_ssm_b_im': out['m_ssm_b_im'], 'm_ssm_c_re': out['m_ssm_c_re'], 'm_ssm_c_im': out['m_ssm_c_im'], 'm_ssm_d': out['m_ssm_d'], 'm_ssm_log_step': out['m_ssm_log_step'], 'm_ssm_w_glu': out['m_ssm_w_glu'], 'm_ssm_b_glu': out['m_ssm_b_glu'], 'm_dn_conv_w': out['m_dn_conv_w'], 'm_dn_a_log': out['m_dn_a_log'], 'm_dn_dt_bias': out['m_dn_dt_bias'], 'm_dn_norm_g': out['m_dn_norm_g'], 'm_sg_ln_g': out['m_sg_ln_g'], 'm_sg_ln_b': out['m_sg_ln_b'], 'm_sg_w': out['m_sg_w'], 'm_sg_b': out['m_sg_b'], 'm_w_out': out['m_w_out'], 'm_ple_norm_g': out['m_ple_norm_g'], 'm_w_ple_gate': out['m_w_ple_gate'], 'm_w_ple': out['m_w_ple'], 'm_final_norm_g': out['m_final_norm_g'], 'v_norm_g': out['v_norm_g'], 'v_w_in': out['v_w_in'], 'v_ssm_a_re': out['v_ssm_a_re'], 'v_ssm_a_im': out['v_ssm_a_im'], 'v_ssm_b_re': out['v_ssm_b_re'], 'v_ssm_b_im': out['v_ssm_b_im'], 'v_ssm_c_re': out['v_ssm_c_re'], 'v_ssm_c_im': out['v_ssm_c_im'], 'v_ssm_d': out['v_ssm_d'], 'v_ssm_log_step': out['v_ssm_log_step'], 'v_ssm_w_glu': out['v_ssm_w_glu'], 'v_ssm_b_glu': out['v_ssm_b_glu'], 'v_dn_conv_w': out['v_dn_conv_w'], 'v_dn_a_log': out['v_dn_a_log'], 'v_dn_dt_bias': out['v_dn_dt_bias'], 'v_dn_norm_g': out['v_dn_norm_g'], 'v_sg_ln_g': out['v_sg_ln_g'], 'v_sg_ln_b': out['v_sg_ln_b'], 'v_sg_w': out['v_sg_w'], 'v_sg_b': out['v_sg_b'], 'v_w_out': out['v_w_out'], 'v_ple_norm_g': out['v_ple_norm_g'], 'v_w_ple_gate': out['v_w_ple_gate'], 'v_w_ple': out['v_w_ple'], 'v_final_norm_g': out['v_final_norm_g']}


def _loss(weights, diff, rest, loss_target):
    with _jax.named_scope("forward"):
        args = {**rest, TWIN_DIFF_INPUT: diff, **{k: w.astype(_WEIGHT_DTYPES[k]) for k, w in weights.items()}}
        y = _forward(args)
    with _jax.named_scope("loss_head"):
        err = _jnp.square(y.astype(_jnp.float32) - loss_target)
        return 0.5 * _jnp.sum(_jnp.mean(err, axis=-1)) if err.ndim else 0.5 * err


def _adamw(w, g, m, v):
    m = ADAM_B1 * m + (1.0 - ADAM_B1) * g
    v = ADAM_B2 * v + (1.0 - ADAM_B2) * _jnp.square(g)
    m_hat = m / (1.0 - ADAM_B1 ** ADAM_STEP)
    v_hat = v / (1.0 - ADAM_B2 ** ADAM_STEP)
    delta = -ADAM_LR * (m_hat / (_jnp.sqrt(v_hat) + ADAM_EPS) + ADAM_WD * w)
    return delta, m, v


def reference(x, p, norm_g, w_in, ssm_a_re, ssm_a_im, ssm_b_re, ssm_b_im, ssm_c_re, ssm_c_im, ssm_d, ssm_log_step, ssm_w_glu, ssm_b_glu, dn_conv_w, dn_a_log, dn_dt_bias, dn_norm_g, sg_ln_g, sg_ln_b, sg_w, sg_b, w_out, ple_norm_g, w_ple_gate, w_ple, final_norm_g, loss_target, m_norm_g, m_w_in, m_ssm_a_re, m_ssm_a_im, m_ssm_b_re, m_ssm_b_im, m_ssm_c_re, m_ssm_c_im, m_ssm_d, m_ssm_log_step, m_ssm_w_glu, m_ssm_b_glu, m_dn_conv_w, m_dn_a_log, m_dn_dt_bias, m_dn_norm_g, m_sg_ln_g, m_sg_ln_b, m_sg_w, m_sg_b, m_w_out, m_ple_norm_g, m_w_ple_gate, m_w_ple, m_final_norm_g, v_norm_g, v_w_in, v_ssm_a_re, v_ssm_a_im, v_ssm_b_re, v_ssm_b_im, v_ssm_c_re, v_ssm_c_im, v_ssm_d, v_ssm_log_step, v_ssm_w_glu, v_ssm_b_glu, v_dn_conv_w, v_dn_a_log, v_dn_dt_bias, v_dn_norm_g, v_sg_ln_g, v_sg_ln_b, v_sg_w, v_sg_b, v_w_out, v_ple_norm_g, v_w_ple_gate, v_w_ple, v_final_norm_g):
    given = dict(x=x, p=p, norm_g=norm_g, w_in=w_in, ssm_a_re=ssm_a_re, ssm_a_im=ssm_a_im, ssm_b_re=ssm_b_re, ssm_b_im=ssm_b_im, ssm_c_re=ssm_c_re, ssm_c_im=ssm_c_im, ssm_d=ssm_d, ssm_log_step=ssm_log_step, ssm_w_glu=ssm_w_glu, ssm_b_glu=ssm_b_glu, dn_conv_w=dn_conv_w, dn_a_log=dn_a_log, dn_dt_bias=dn_dt_bias, dn_norm_g=dn_norm_g, sg_ln_g=sg_ln_g, sg_ln_b=sg_ln_b, sg_w=sg_w, sg_b=sg_b, w_out=w_out, ple_norm_g=ple_norm_g, w_ple_gate=w_ple_gate, w_ple=w_ple, final_norm_g=final_norm_g, loss_target=loss_target, m_norm_g=m_norm_g, m_w_in=m_w_in, m_ssm_a_re=m_ssm_a_re, m_ssm_a_im=m_ssm_a_im, m_ssm_b_re=m_ssm_b_re, m_ssm_b_im=m_ssm_b_im, m_ssm_c_re=m_ssm_c_re, m_ssm_c_im=m_ssm_c_im, m_ssm_d=m_ssm_d, m_ssm_log_step=m_ssm_log_step, m_ssm_w_glu=m_ssm_w_glu, m_ssm_b_glu=m_ssm_b_glu, m_dn_conv_w=m_dn_conv_w, m_dn_a_log=m_dn_a_log, m_dn_dt_bias=m_dn_dt_bias, m_dn_norm_g=m_dn_norm_g, m_sg_ln_g=m_sg_ln_g, m_sg_ln_b=m_sg_ln_b, m_sg_w=m_sg_w, m_sg_b=m_sg_b, m_w_out=m_w_out, m_ple_norm_g=m_ple_norm_g, m_w_ple_gate=m_w_ple_gate, m_w_ple=m_w_ple, m_final_norm_g=m_final_norm_g, v_norm_g=v_norm_g, v_w_in=v_w_in, v_ssm_a_re=v_ssm_a_re, v_ssm_a_im=v_ssm_a_im, v_ssm_b_re=v_ssm_b_re, v_ssm_b_im=v_ssm_b_im, v_ssm_c_re=v_ssm_c_re, v_ssm_c_im=v_ssm_c_im, v_ssm_d=v_ssm_d, v_ssm_log_step=v_ssm_log_step, v_ssm_w_glu=v_ssm_w_glu, v_ssm_b_glu=v_ssm_b_glu, v_dn_conv_w=v_dn_conv_w, v_dn_a_log=v_dn_a_log, v_dn_dt_bias=v_dn_dt_bias, v_dn_norm_g=v_dn_norm_g, v_sg_ln_g=v_sg_ln_g, v_sg_ln_b=v_sg_ln_b, v_sg_w=v_sg_w, v_sg_b=v_sg_b, v_w_out=v_w_out, v_ple_norm_g=v_ple_norm_g, v_w_ple_gate=v_w_ple_gate, v_w_ple=v_w_ple, v_final_norm_g=v_final_norm_g)
    weights = {n: given[n] for n in TWIN_WEIGHTS}
    shared = {n: given[n] for n in SHARED_INPUTS}
    per_example = {n: given[n] for n in ['x', 'p']}
    grad_fn = _jax.value_and_grad(_loss, argnums=(0, 1))

    def one_microbatch(ex, loss_target):
        ex = dict(ex)
        diff = ex.pop(TWIN_DIFF_INPUT)
        return grad_fn(weights, diff, {**shared, **ex}, loss_target)

    if N_MICROBATCH == 1:
        loss, (grad_w, grad_x) = one_microbatch(per_example, given["loss_target"])
    else:
        def body(carry, xs):
            loss_sum, grad_sum = carry
            l_k, (gw_k, gx_k) = one_microbatch(xs[0], xs[1])
            with _jax.named_scope("update"):
                return (loss_sum + l_k, _jax.tree.map(_jnp.add, grad_sum, gw_k)), gx_k

        init = (_jnp.zeros((), _jnp.float32), _jax.tree.map(_jnp.zeros_like, weights))
        (loss, grad_w), grad_x = _jax.lax.scan(body, init, (per_example, given["loss_target"]))
    with _jax.named_scope("update"):
        delta_w, new_m, new_v = {}, {}, {}
        for n in TWIN_WEIGHTS:
            delta_w[n], new_m[n], new_v[n] = _adamw(weights[n], grad_w[n], given["m_" + n], given["v_" + n])
    return (loss, grad_x, *[grad_w[n] for n in TWIN_WEIGHTS], *[delta_w[n] for n in TWIN_WEIGHTS],
            *[new_m[n] for n in TWIN_WEIGHTS], *[new_v[n] for n in TWIN_WEIGHTS])
```

```python
import functools

import jax
import jax.numpy as jnp
from jax import lax
from jax.experimental import pallas as pl
from jax.experimental.pallas import tpu as pltpu

F32 = jnp.float32
BF16 = jnp.bfloat16
HI = lax.Precision.HIGHEST
EPS = 1e-6

D = 1024
D_PLE = 256
D_SSM = 256
D_DN = 512
D_SG = 256
G = 16
CG = 16
NS = 64
NRE = G * NS
H = 4
DH = 128
DN_C = 64
SG_C = 128
ZW = 3456
Z_PIECES = (512, 1536, 512, 768, 128)
N_DEV = 8
LANES = 128
PACK_ROWS = 512
VMEM_LIMIT = 56 * 1024 * 1024

ADAM_LR = 0.001
ADAM_B1 = 0.9
ADAM_B2 = 0.999
ADAM_EPS = 1e-08
ADAM_WD = 0.01
ADAM_STEP = 10

S5_L = 128
S5_SHIFTS = (1, 2, 4, 8, 16, 32, 64)

WEIGHTS = ['norm_g', 'w_in', 'ssm_a_re', 'ssm_a_im', 'ssm_b_re', 'ssm_b_im', 'ssm_c_re', 'ssm_c_im', 'ssm_d',
           'ssm_log_step', 'ssm_w_glu', 'ssm_b_glu', 'dn_conv_w', 'dn_a_log', 'dn_dt_bias', 'dn_norm_g', 'sg_ln_g',
           'sg_ln_b', 'sg_w', 'sg_b', 'w_out', 'ple_norm_g', 'w_ple_gate', 'w_ple', 'final_norm_g']
SHARDED = {'w_in': 2, 'ssm_w_glu': 1, 'dn_conv_w': 2, 'w_out': 1, 'w_ple_gate': 1, 'w_ple': 2}
SHARDED_ORDER = ['w_in', 'ssm_w_glu', 'dn_conv_w', 'w_out', 'w_ple_gate', 'w_ple']
REPLICATED_ORDER = [n for n in WEIGHTS if n not in SHARDED]


def _cp(*sem):
    return pltpu.CompilerParams(dimension_semantics=sem, vmem_limit_bytes=VMEM_LIMIT)


def _dg(a, b, ca, cb, precision=None):
    return lax.dot_general(a, b, (((ca,), (cb,)), ((), ())), precision=precision, preferred_element_type=F32)


@jax.custom_vjp
def _mm(a, b):
    return _dg(a.astype(BF16), b.astype(BF16), 1, 0)


def _mm_fwd(a, b):
    return _mm(a, b), (a, b)


def _mm_bwd(res, g):
    a, b = res
    gb = g.astype(BF16)
    return _dg(gb, b.astype(BF16), 1, 1), _dg(a.astype(BF16), gb, 0, 0)


_mm.defvjp(_mm_fwd, _mm_bwd)


@jax.custom_vjp
def _mm_nt(a, b):
    return _dg(a.astype(BF16), b.astype(BF16), 1, 1)


def _mm_nt_fwd(a, b):
    return _mm_nt(a, b), (a, b)


def _mm_nt_bwd(res, g):
    a, b = res
    gb = g.astype(BF16)
    return _dg(gb, b.astype(BF16), 1, 0), _dg(gb, a.astype(BF16), 0, 0)


_mm_nt.defvjp(_mm_nt_fwd, _mm_nt_bwd)


@jax.custom_vjp
def _mm_tn(a, b):
    return _dg(a.astype(BF16), b.astype(BF16), 0, 0)


def _mm_tn_fwd(a, b):
    return _mm_tn(a, b), (a, b)


def _mm_tn_bwd(res, g):
    a, b = res
    gb = g.astype(BF16)
    return _dg(b.astype(BF16), gb, 1, 1), _dg(a.astype(BF16), gb, 1, 0)


_mm_tn.defvjp(_mm_tn_fwd, _mm_tn_bwd)


@jax.custom_vjp
def _dotx(a, b):
    return _dg(a, b, 1, 0, HI)


def _dotx_fwd(a, b):
    return _dotx(a, b), (a, b)


def _dotx_bwd(res, g):
    a, b = res
    return _dg(g, b, 1, 1, HI), _dg(a, g, 0, 0, HI)


_dotx.defvjp(_dotx_fwd, _dotx_bwd)


def _make_shift(rows):
    @functools.partial(jax.custom_vjp, nondiff_argnums=(1,))
    def shift(x, d):
        r = lax.broadcasted_iota(jnp.int32, x.shape, 0)
        return jnp.where(r >= d, pltpu.roll(x, d, 0), 0.0)

    def fwd(x, d):
        return shift(x, d), None

    def bwd(d, _, g):
        r = lax.broadcasted_iota(jnp.int32, g.shape, 0)
        return (jnp.where(r < rows - d, pltpu.roll(g, rows - d, 0), 0.0),)

    shift.defvjp(fwd, bwd)
    return shift


def _rms(x, g):
    return x * lax.rsqrt(jnp.mean(x * x, axis=-1, keepdims=True) + EPS) * g


def _silu(x):
    return x * jax.nn.sigmoid(x)


def _in_proj_fwd(x, g, w, tb, name):
    t = x.shape[0]

    def body(x_ref, g_ref, w_ref, *z_refs):
        h = _rms(x_ref[...], g_ref[...])
        z = jnp.dot(h.astype(BF16), w_ref[...], preferred_element_type=F32)
        off = 0
        for z_ref, n in zip(z_refs, Z_PIECES):
            z_ref[...] = z[:, off:off + n]
            off += n

    row = lambda i: (i, 0)
    full = lambda i: (0, 0)
    return pl.pallas_call(
        body, grid=(t // tb,),
        in_specs=[pl.BlockSpec((tb, D), row), pl.BlockSpec((1, D), full), pl.BlockSpec((D, ZW), full)],
        out_specs=[pl.BlockSpec((tb, n), row) for n in Z_PIECES],
        out_shape=[jax.ShapeDtypeStruct((t, n), F32) for n in Z_PIECES],
        name=name, compiler_params=_cp("parallel"))(x, g, w)


def _in_proj_bwd_dx(x, g, w, dzs, dx_res, tb, name):
    t = x.shape[0]

    def body(x_ref, g_ref, w_ref, d0, d1, d2, d3, d4, dxr_ref, dx_ref, dg_ref):
        @pl.when(pl.program_id(0) == 0)
        def _():
            dg_ref[...] = jnp.zeros_like(dg_ref)

        dz = jnp.concatenate([d0[...], d1[...], d2[...], d3[...], d4[...]], axis=1).astype(BF16)
        dh = _dg(dz, w_ref[...], 1, 1)
        _, vjp = jax.vjp(_rms, x_ref[...], g_ref[...])
        dx, dg = vjp(dh)
        dx_ref[...] = dx + dxr_ref[...]
        dg_ref[...] += dg

    row = lambda i: (i, 0)
    full = lambda i: (0, 0)
    return pl.pallas_call(
        body, grid=(t // tb,),
        in_specs=[pl.BlockSpec((tb, D), row), pl.BlockSpec((1, D), full), pl.BlockSpec((D, ZW), full)]
        + [pl.BlockSpec((tb, n), row) for n in Z_PIECES] + [pl.BlockSpec((tb, D), row)],
        out_specs=[pl.BlockSpec((tb, D), row), pl.BlockSpec((1, D), full)],
        out_shape=[jax.ShapeDtypeStruct((t, D), F32), jax.ShapeDtypeStruct((1, D), F32)],
        name=name, compiler_params=_cp("arbitrary"))(x, g, w, *dzs, dx_res)


def _in_proj_bwd_dw(x, g, dzs, tb, name):
    t = x.shape[0]

    def body(x_ref, g_ref, d0, d1, d2, d3, d4, *dw_refs):
        @pl.when(pl.program_id(0) == 0)
        def _():
            for r in dw_refs:
                r[...] = jnp.zeros_like(r)

        h = _rms(x_ref[...], g_ref[...]).astype(BF16)
        for d_ref, dw_ref in zip((d0, d1, d2, d3, d4), dw_refs):
            dw_ref[...] += _dg(h, d_ref[...].astype(BF16), 0, 0)

    row = lambda i: (i, 0)
    full = lambda i: (0, 0)
    return pl.pallas_call(
        body, grid=(t // tb,),
        in_specs=[pl.BlockSpec((tb, D), row), pl.BlockSpec((1, D), full)] + [pl.BlockSpec((tb, n), row) for n in Z_PIECES],
        out_specs=[pl.BlockSpec((D, n), full) for n in Z_PIECES],
        out_shape=[jax.ShapeDtypeStruct((D, n), F32) for n in Z_PIECES],
        name=name, compiler_params=_cp("arbitrary"))(x, g, *dzs)


def _s5_tables(a_re, a_im, b_re, b_im, c_re, c_im, d_skip, log_step):
    step = jnp.exp(log_step)[:, None]

    def lam_pow(k):
        mag = jnp.exp(k * a_re * step)
        ang = k * a_im * step
        return mag * jnp.cos(ang), mag * jnp.sin(ang)

    lam_re, lam_im = lam_pow(1.0)
    den = a_re * a_re + a_im * a_im
    nr, ni = lam_re - 1.0, lam_im
    f_re = (nr * a_re + ni * a_im) / den
    f_im = (ni * a_re - nr * a_im) / den
    bbar_re = f_re[..., None] * b_re - f_im[..., None] * b_im
    bbar_im = f_re[..., None] * b_im + f_im[..., None] * b_re
    eye = jnp.eye(G, dtype=F32)

    def blk_b(bb):
        return (jnp.transpose(bb, (0, 2, 1))[:, :, None, :] * eye[:, None, :, None]).reshape(D_SSM, NRE)

    def blk_c(cc):
        return (jnp.transpose(cc, (0, 2, 1))[:, :, None, :] * eye[:, None, :, None]).reshape(NRE, D_SSM)

    b_blk = jnp.concatenate([blk_b(bbar_re), blk_b(bbar_im)], axis=1)
    c_blk = jnp.concatenate([blk_c(c_re), -blk_c(c_im)], axis=0)
    ks = jnp.asarray(S5_SHIFTS, F32)[:, None, None]
    ld_re, ld_im = lam_pow(ks)
    ld = jnp.concatenate([ld_re.reshape(-1, 1, NRE), ld_im.reshape(-1, 1, NRE)], axis=-1)
    js = jnp.arange(1, S5_L + 1, dtype=F32)[:, None, None]
    lp_re, lp_im = lam_pow(js)
    lp = jnp.concatenate([lp_re.reshape(S5_L, NRE), lp_im.reshape(S5_L, NRE)], axis=-1)
    return b_blk, c_blk, ld, lp, d_skip.reshape(1, D_SSM)


def _s5_chunk_fn():
    shift = _make_shift(S5_L)

    def f(u, gate, cr, ci, b_blk, c_blk, lds, lp, dv, wglu, bglu):
        bu = _mm(u, b_blk)
        hr, hi = bu[:, :NRE], bu[:, NRE:]
        for ld, d in zip(lds, S5_SHIFTS):
            lr, li = ld[:, :NRE], ld[:, NRE:]
            sr, si = shift(hr, d), shift(hi, d)
            hr, hi = hr + lr * sr - li * si, hi + lr * si + li * sr
        pr, pi = lp[:, :NRE], lp[:, NRE:]
        hr, hi = hr + pr * cr - pi * ci, hi + pr * ci + pi * cr
        last = lax.broadcasted_iota(jnp.int32, hr.shape, 0) == S5_L - 1
        ncr = jnp.sum(jnp.where(last, hr, 0.0), axis=0, keepdims=True)
        nci = jnp.sum(jnp.where(last, hi, 0.0), axis=0, keepdims=True)
        y = _mm(jnp.concatenate([hr, hi], axis=1), c_blk) + dv * u
        y = jax.nn.gelu(y)
        y = y * jax.nn.sigmoid(_mm(y, wglu) + bglu)
        return y * _silu(gate), ncr, nci

    return f


def _s5_specs(n_s, rev):
    nd = len(S5_SHIFTS)
    if rev:
        row = lambda b, i: (b * n_s + n_s - 1 - i, 0)
        row3 = lambda b, i: (b * n_s + n_s - 1 - i, 0, 0)
    else:
        row = lambda b, i: (b * n_s + i, 0)
        row3 = lambda b, i: (b * n_s + i, 0, 0)
    full = lambda b, i: (0, 0)
    full3 = lambda b, i: (0, 0, 0)
    par = [pl.BlockSpec((D_SSM, 2 * NRE), full), pl.BlockSpec((2 * NRE, D_SSM), full), pl.BlockSpec((nd, 1, 2 * NRE), full3),
           pl.BlockSpec((S5_L, 2 * NRE), full), pl.BlockSpec((1, D_SSM), full), pl.BlockSpec((D_SSM, D_SSM), full),
           pl.BlockSpec((1, D_SSM), full)]
    par_shapes = [(D_SSM, 2 * NRE), (2 * NRE, D_SSM), (nd, 1, 2 * NRE), (S5_L, 2 * NRE), (1, D_SSM), (D_SSM, D_SSM), (1, D_SSM)]
    return row, row3, par, par_shapes


def _s5_fwd(zs, params, bl, s, name):
    n_s = s // S5_L
    nd = len(S5_SHIFTS)
    f = _s5_chunk_fn()
    row, row3, par, _ = _s5_specs(n_s, False)

    def body(z_ref, b_ref, c_ref, ld_ref, lp_ref, dv_ref, wg_ref, bg_ref, y_ref, car_ref, cs):
        @pl.when(pl.program_id(1) == 0)
        def _():
            cs[...] = jnp.zeros_like(cs)

        c = cs[...]
        car_ref[0] = c
        z = z_ref[...]
        out, ncr, nci = f(z[:, :D_SSM], z[:, D_SSM:], c[:, :NRE], c[:, NRE:], b_ref[...], c_ref[...],
                          [ld_ref[k] for k in range(nd)], lp_ref[...], dv_ref[...], wg_ref[...], bg_ref[...])
        y_ref[...] = out
        cs[:, :NRE] = ncr
        cs[:, NRE:] = nci

    t = bl * s
    return pl.pallas_call(
        body, grid=(bl, n_s),
        in_specs=[pl.BlockSpec((S5_L, 2 * D_SSM), row)] + par,
        out_specs=[pl.BlockSpec((S5_L, D_SSM), row), pl.BlockSpec((1, 1, 2 * NRE), row3)],
        out_shape=[jax.ShapeDtypeStruct((t, D_SSM), F32), jax.ShapeDtypeStruct((bl * n_s, 1, 2 * NRE), F32)],
        scratch_shapes=[pltpu.VMEM((1, 2 * NRE), F32)],
        name=name, compiler_params=_cp("arbitrary", "arbitrary"))(zs, *params)


def _s5_bwd(zs, carries, dy, params, bl, s, name):
    n_s = s // S5_L
    nd = len(S5_SHIFTS)
    f = _s5_chunk_fn()
    row, row3, par, par_shapes = _s5_specs(n_s, True)

    def body(z_ref, car_ref, dy_ref, b_ref, c_ref, ld_ref, lp_ref, dv_ref, wg_ref, bg_ref,
             dz_ref, db_ref, dc_ref, dld_ref, dlp_ref, ddv_ref, dwg_ref, dbg_ref, dcs):
        accs = (db_ref, dc_ref, dld_ref, dlp_ref, ddv_ref, dwg_ref, dbg_ref)

        @pl.when((pl.program_id(0) == 0) & (pl.program_id(1) == 0))
        def _():
            for r in accs:
                r[...] = jnp.zeros_like(r)

        @pl.when(pl.program_id(1) == 0)
        def _():
            dcs[...] = jnp.zeros_like(dcs)

        z = z_ref[...]
        c = car_ref[0]
        _, vjp = jax.vjp(f, z[:, :D_SSM], z[:, D_SSM:], c[:, :NRE], c[:, NRE:], b_ref[...], c_ref[...],
                         [ld_ref[k] for k in range(nd)], lp_ref[...], dv_ref[...], wg_ref[...], bg_ref[...])
        dc = dcs[...]
        du, dgate, dcr, dci, dbb, dcb, dlds, dlpb, ddvb, dwgb, dbgb = vjp((dy_ref[...], dc[:, :NRE], dc[:, NRE:]))
        dz_ref[...] = jnp.concatenate([du, dgate], axis=1)
        dcs[:, :NRE] = dcr
        dcs[:, NRE:] = dci
        db_ref[...] += dbb
        dc_ref[...] += dcb
        for k in range(nd):
            dld_ref[k] += dlds[k]
        dlp_ref[...] += dlpb
        ddv_ref[...] += ddvb
        dwg_ref[...] += dwgb
        dbg_ref[...] += dbgb

    t = bl * s
    return pl.pallas_call(
        body, grid=(bl, n_s),
        in_specs=[pl.BlockSpec((S5_L, 2 * D_SSM), row), pl.BlockSpec((1, 1, 2 * NRE), row3), pl.BlockSpec((S5_L, D_SSM), row)] + par,
        out_specs=[pl.BlockSpec((S5_L, 2 * D_SSM), row)] + par,
        out_shape=[jax.ShapeDtypeStruct((t, 2 * D_SSM), F32)] + [jax.ShapeDtypeStruct(sh, F32) for sh in par_shapes],
        scratch_shapes=[pltpu.VMEM((1, 2 * NRE), F32)],
        name=name, compiler_params=_cp("arbitrary", "arbitrary"))(zs, carries, dy, *params)


def _dn_post(c):
    s = _silu(c)
    parts = []
    for j in range(12):
        xj = s[:, j * DH:(j + 1) * DH]
        if j < 8:
            xj = xj * lax.rsqrt(jnp.sum(xj * xj, axis=-1, keepdims=True) + EPS)
        if j < 4:
            xj = xj * (DH ** -0.5)
        parts.append(xj)
    return jnp.concatenate(parts, axis=1)


def _dn_prep_fwd(zq, conv_w8, bl, s, tb, name):
    n_s = s // tb
    hb = tb // 8
    w3 = 3 * D_DN

    def body(cur_ref, prev_ref, w_ref, o_ref):
        i = pl.program_id(1)
        prev = jnp.where(i > 0, prev_ref[...], 0.0)
        ext = jnp.concatenate([prev, cur_ref[...]], axis=0)
        c = jnp.zeros((tb, w3), F32)
        for k in range(4):
            sh = ext if k == 3 else pltpu.roll(ext, 3 - k, 0)
            c = c + w_ref[k:k + 1, :] * sh[8:, :]
        o_ref[...] = _dn_post(c)

    row = lambda b, i: (b * n_s + i, 0)
    prv = lambda b, i: (jnp.maximum((b * n_s + i) * hb - 1, 0), 0)
    full = lambda b, i: (0, 0)
    t = bl * s
    return pl.pallas_call(
        body, grid=(bl, n_s),
        in_specs=[pl.BlockSpec((tb, w3), row), pl.BlockSpec((8, w3), prv), pl.BlockSpec((8, w3), full)],
        out_specs=pl.BlockSpec((tb, w3), row),
        out_shape=jax.ShapeDtypeStruct((t, w3), F32),
        name=name, compiler_params=_cp("parallel", "parallel"))(zq, zq, conv_w8)


def _dn_prep_bwd(zq, dqkv, conv_w8, bl, s, tb, name):
    n_s = s // tb
    hb = tb // 8
    w3 = 3 * D_DN
    n_blk8 = bl * s // 8

    def body(cur_ref, prev_ref, next_ref, d_ref, dnext_ref, w_ref, dz_ref, dw_ref):
        b, i = pl.program_id(0), pl.program_id(1)

        @pl.when((b == 0) & (i == 0))
        def _():
            dw_ref[...] = jnp.zeros_like(dw_ref)

        prev = jnp.where(i > 0, prev_ref[...], 0.0)
        nxt = jnp.where(i < n_s - 1, next_ref[...], 0.0)
        dnxt = jnp.where(i < n_s - 1, dnext_ref[...], 0.0)
        ext = jnp.concatenate([prev, cur_ref[...], nxt], axis=0)
        shifted = [ext if k == 3 else pltpu.roll(ext, 3 - k, 0) for k in range(4)]
        c2 = jnp.zeros((tb + 8, w3), F32)
        for k in range(4):
            c2 = c2 + w_ref[k:k + 1, :] * shifted[k][8:, :]
        dpost = jnp.concatenate([d_ref[...], dnxt], axis=0)
        _, vjp = jax.vjp(_dn_post, c2)
        (dc2,) = vjp(dpost)
        dz = jnp.zeros((tb, w3), F32)
        for k in range(4):
            up = dc2 if k == 3 else pltpu.roll(dc2, tb + 8 - (3 - k), 0)
            dz = dz + w_ref[k:k + 1, :] * up[:tb, :]
            dw_ref[k:k + 1, :] += jnp.sum(dc2[:tb, :] * shifted[k][8:8 + tb, :], axis=0, keepdims=True)
        dz_ref[...] = dz

    row = lambda b, i: (b * n_s + i, 0)
    prv = lambda b, i: (jnp.maximum((b * n_s + i) * hb - 1, 0), 0)
    nxt = lambda b, i: (jnp.minimum((b * n_s + i + 1) * hb, n_blk8 - 1), 0)
    full = lambda b, i: (0, 0)
    t = bl * s
    return pl.pallas_call(
        body, grid=(bl, n_s),
        in_specs=[pl.BlockSpec((tb, w3), row), pl.BlockSpec((8, w3), prv), pl.BlockSpec((8, w3), nxt),
                  pl.BlockSpec((tb, w3), row), pl.BlockSpec((8, w3), nxt), pl.BlockSpec((8, w3), full)],
        out_specs=[pl.BlockSpec((tb, w3), row), pl.BlockSpec((8, w3), full)],
        out_shape=[jax.ShapeDtypeStruct((t, w3), F32), jax.ShapeDtypeStruct((8, w3), F32)],
        name=name, compiler_params=_cp("arbitrary", "arbitrary"))(zq, zq, zq, dqkv, dqkv, conv_w8)


def _dn_chunk_fn():
    c_len = DN_C

    def f(qkv, zab, zg, s0, s1, s2, s3, alog_e, dt_e, ng):
        states = (s0, s1, s2, s3)
        r = lax.broadcasted_iota(jnp.int32, (c_len, c_len), 0)
        c = lax.broadcasted_iota(jnp.int32, (c_len, c_len), 1)
        causal, strict, eye = r >= c, r > c, r == c
        tril = jnp.where(causal, 1.0, 0.0)
        ident = jnp.where(eye, 1.0, 0.0)
        ones = jnp.ones((c_len, c_len), F32)
        rr = lax.broadcasted_iota(jnp.int32, (LANES, D_DN), 0)
        cc = lax.broadcasted_iota(jnp.int32, (LANES, D_DN), 1)
        e_a = jnp.where((cc >= rr * DH) & (cc < rr * DH + DH) & (rr < H), 1.0, 0.0)
        e_b = jnp.where((cc >= (rr - H) * DH) & (cc < (rr - H) * DH + DH) & (rr >= H) & (rr < 2 * H), 1.0, 0.0)
        a_e = _dotx(zab, e_a)
        b_e = _dotx(zab, e_b)
        beta = jax.nn.sigmoid(b_e)
        g = -jnp.exp(alog_e) * jax.nn.softplus(a_e + dt_e)
        gc = _dotx(tril, g)
        glast = jnp.sum(g, axis=0, keepdims=True)
        eg = jnp.exp(gc)
        ekd = jnp.exp(glast - gc)
        dl = jnp.exp(glast)
        pick = jnp.where(lax.broadcasted_iota(jnp.int32, (DH, c_len), 0) == 0, 1.0, 0.0)
        ys, new_states = [], []
        for h in range(H):
            sl = slice(h * DH, (h + 1) * DH)
            q = qkv[:, h * DH:(h + 1) * DH]
            k = qkv[:, D_DN + h * DH:D_DN + (h + 1) * DH]
            v = qkv[:, 2 * D_DN + h * DH:2 * D_DN + (h + 1) * DH]
            bh = beta[:, sl]
            ccol = _dotx(gc[:, sl], pick)
            rrow = _dotx(ones, jnp.where(eye, ccol, 0.0))
            dec = jnp.where(causal, jnp.exp(jnp.where(causal, ccol - rrow, 0.0)), 0.0)
            kb = k * bh
            m = jnp.where(strict, _mm_nt(kb, k) * dec, 0.0)
            p = m
            tinv = ident - m
            for _ in range(5):
                p = _dotx(p, p)
                tinv = _dotx(tinv, ident + p)
            sol = _dotx(tinv, jnp.concatenate([v * bh, kb * eg[:, sl]], axis=1))
            val, kcd = sol[:, :DH], sol[:, DH:]
            att = _mm_nt(q, k) * dec
            qd = q * eg[:, sl]
            kd = k * ekd[:, sl]
            st = states[h]
            vnew = val - _mm(kcd, st)
            o = _mm(qd, st) + _mm(att, vnew)
            new_states.append(st * dl[:, sl] + _mm_tn(kd, vnew))
            on = _rms(o, ng)
            ys.append(on * _silu(zg[:, sl]))
        return (jnp.concatenate(ys, axis=1), *new_states)

    return f


def _dn_scan_specs(n_c, rev):
    if rev:
        row = lambda b, i: (b * n_c + n_c - 1 - i, 0)
        row4 = lambda b, i: (b * n_c + n_c - 1 - i, 0, 0, 0)
    else:
        row = lambda b, i: (b * n_c + i, 0)
        row4 = lambda b, i: (b * n_c + i, 0, 0, 0)
    full = lambda b, i: (0, 0)
    par = [pl.BlockSpec((1, D_DN), full), pl.BlockSpec((1, D_DN), full), pl.BlockSpec((1, DH), full)]
    par_shapes = [(1, D_DN), (1, D_DN), (1, DH)]
    return row, row4, par, par_shapes


def _dn_scan_fwd(qkv, zab, zg, params, bl, s, name):
    n_c = s // DN_C
    f = _dn_chunk_fn()
    row, row4, par, _ = _dn_scan_specs(n_c, False)

    def body(q_ref, ab_ref, zg_ref, al_ref, dt_ref, ng_ref, y_ref, st_ref, ssc):
        @pl.when(pl.program_id(1) == 0)
        def _():
            ssc[...] = jnp.zeros_like(ssc)

        sts = [ssc[h] for h in range(H)]
        for h in range(H):
            st_ref[0, h] = sts[h]
        outs = f(q_ref[...], ab_ref[...], zg_ref[...], *sts, al_ref[...], dt_ref[...], ng_ref[...])
        y_ref[...] = outs[0]
        for h in range(H):
            ssc[h] = outs[1 + h]

    t = bl * s
    return pl.pallas_call(
        body, grid=(bl, n_c),
        in_specs=[pl.BlockSpec((DN_C, 3 * D_DN), row), pl.BlockSpec((DN_C, LANES), row), pl.BlockSpec((DN_C, D_DN), row)] + par,
        out_specs=[pl.BlockSpec((DN_C, D_DN), row), pl.BlockSpec((1, H, DH, DH), row4)],
        out_shape=[jax.ShapeDtypeStruct((t, D_DN), F32), jax.ShapeDtypeStruct((bl * n_c, H, DH, DH), F32)],
        scratch_shapes=[pltpu.VMEM((H, DH, DH), F32)],
        name=name, compiler_params=_cp("arbitrary", "arbitrary"))(qkv, zab, zg, *params)


def _dn_scan_bwd(qkv, zab, zg, states, dy, params, bl, s, name):
    n_c = s // DN_C
    f = _dn_chunk_fn()
    row, row4, par, par_shapes = _dn_scan_specs(n_c, True)

    def body(q_ref, ab_ref, zg_ref, st_ref, dy_ref, al_ref, dt_ref, ng_ref,
             dq_ref, dab_ref, dzg_ref, dal_ref, ddt_ref, dng_ref, dsc):
        @pl.when((pl.program_id(0) == 0) & (pl.program_id(1) == 0))
        def _():
            for r in (dal_ref, ddt_ref, dng_ref):
                r[...] = jnp.zeros_like(r)

        @pl.when(pl.program_id(1) == 0)
        def _():
            dsc[...] = jnp.zeros_like(dsc)

        sts = [st_ref[0, h] for h in range(H)]
        _, vjp = jax.vjp(f, q_ref[...], ab_ref[...], zg_ref[...], *sts, al_ref[...], dt_ref[...], ng_ref[...])
        cts = vjp((dy_ref[...], *[dsc[h] for h in range(H)]))
        dq_ref[...] = cts[0]
        dab_ref[...] = cts[1]
        dzg_ref[...] = cts[2]
        for h in range(H):
            dsc[h] = cts[3 + h]
        dal_ref[...] += cts[3 + H]
        ddt_ref[...] += cts[4 + H]
        dng_ref[...] += cts[5 + H]

    t = bl * s
    return pl.pallas_call(
        body, grid=(bl, n_c),
        in_specs=[pl.BlockSpec((DN_C, 3 * D_DN), row), pl.BlockSpec((DN_C, LANES), row), pl.BlockSpec((DN_C, D_DN), row),
                  pl.BlockSpec((1, H, DH, DH), row4), pl.BlockSpec((DN_C, D_DN), row)] + par,
        out_specs=[pl.BlockSpec((DN_C, 3 * D_DN), row), pl.BlockSpec((DN_C, LANES), row), pl.BlockSpec((DN_C, D_DN), row)] + par,
        out_shape=[jax.ShapeDtypeStruct((t, 3 * D_DN), F32), jax.ShapeDtypeStruct((t, LANES), F32),
                   jax.ShapeDtypeStruct((t, D_DN), F32)] + [jax.ShapeDtypeStruct(sh, F32) for sh in par_shapes],
        scratch_shapes=[pltpu.VMEM((H, DH, DH), F32)],
        name=name, compiler_params=_cp("arbitrary", "arbitrary"))(qkv, zab, zg, states, dy, *params)


def _sg_fn(n_chunk):
    def f(z, lng, lnb, w, bsp_t):
        u = jax.nn.gelu(z[:, :D_SG])
        v = jax.nn.gelu(z[:, D_SG:2 * D_SG])
        gate = z[:, 2 * D_SG:]
        xc = v - jnp.mean(v, axis=-1, keepdims=True)
        vn = xc * lax.rsqrt(jnp.mean(xc * xc, axis=-1, keepdims=True) + EPS) * lng + lnb
        r = lax.broadcasted_iota(jnp.int32, (SG_C, SG_C), 0)
        c = lax.broadcasted_iota(jnp.int32, (SG_C, SG_C), 1)
        causal = r >= c
        first_half = c < SG_C // 2
        rr = lax.broadcasted_iota(jnp.int32, (LANES, D_SG), 0)
        cc = lax.broadcasted_iota(jnp.int32, (LANES, D_SG), 1)
        expand = jnp.where((cc >= rr * 64) & (cc < rr * 64 + 64) & (rr < 4), 1.0, 0.0)
        bias = _dotx(bsp_t, expand)
        wm = [jnp.where(causal, w[h], 0.0) for h in range(4)]
        rows = []
        for ci in range(n_chunk):
            vc = vn[ci * SG_C:(ci + 1) * SG_C]
            pairs = []
            for pr in range(2):
                vp = vc[:, pr * LANES:(pr + 1) * LANES]
                pairs.append(jnp.where(first_half, _mm(wm[2 * pr], vp), _mm(wm[2 * pr + 1], vp)))
            rows.append(jnp.concatenate(pairs, axis=1) + bias)
        sp = jnp.concatenate(rows, axis=0) if n_chunk > 1 else rows[0]
        return u * sp * _silu(gate)

    return f


def _sg_specs():
    full = lambda i: (0, 0)
    full3 = lambda i: (0, 0, 0)
    par = [pl.BlockSpec((1, D_SG), full), pl.BlockSpec((1, D_SG), full), pl.BlockSpec((4, SG_C, SG_C), full3),
           pl.BlockSpec((SG_C, LANES), full)]
    par_shapes = [(1, D_SG), (1, D_SG), (4, SG_C, SG_C), (SG_C, LANES)]
    return par, par_shapes


def _sg_fwd(zsg, params, tb, name):
    t = zsg.shape[0]
    f = _sg_fn(tb // SG_C)
    par, _ = _sg_specs()

    def body(z_ref, g_ref, b_ref, w_ref, bs_ref, y_ref):
        y_ref[...] = f(z_ref[...], g_ref[...], b_ref[...], w_ref[...], bs_ref[...])

    row = lambda i: (i, 0)
    return pl.pallas_call(
        body, grid=(t // tb,), in_specs=[pl.BlockSpec((tb, 3 * D_SG), row)] + par,
        out_specs=pl.BlockSpec((tb, D_SG), row), out_shape=jax.ShapeDtypeStruct((t, D_SG), F32),
        name=name, compiler_params=_cp("parallel"))(zsg, *params)


def _sg_bwd(zsg, dy, params, tb, name):
    t = zsg.shape[0]
    f = _sg_fn(tb // SG_C)
    par, par_shapes = _sg_specs()

    def body(z_ref, dy_ref, g_ref, b_ref, w_ref, bs_ref, dz_ref, dg_ref, db_ref, dw_ref, dbs_ref):
        accs = (dg_ref, db_ref, dw_ref, dbs_ref)

        @pl.when(pl.program_id(0) == 0)
        def _():
            for r in accs:
                r[...] = jnp.zeros_like(r)

        _, vjp = jax.vjp(f, z_ref[...], g_ref[...], b_ref[...], w_ref[...], bs_ref[...])
        cts = vjp(dy_ref[...])
        dz_ref[...] = cts[0]
        for r, ct in zip(accs, cts[1:]):
            r[...] += ct

    row = lambda i: (i, 0)
    return pl.pallas_call(
        body, grid=(t // tb,), in_specs=[pl.BlockSpec((tb, 3 * D_SG), row), pl.BlockSpec((tb, D_SG), row)] + par,
        out_specs=[pl.BlockSpec((tb, 3 * D_SG), row)] + par,
        out_shape=[jax.ShapeDtypeStruct((t, 3 * D_SG), F32)] + [jax.ShapeDtypeStruct(sh, F32) for sh in par_shapes],
        name=name, compiler_params=_cp("arbitrary"))(zsg, dy, *params)


def _out_fwd(x, ys, p, w_out, pg, w_gate, w_ple, tb, name):
    t = x.shape[0]

    def body(x_ref, y0, y1, y2, p_ref, wo_ref, pg_ref, wg_ref, wp_ref, o_ref):
        y = jnp.concatenate([y0[...], y1[...], y2[...]], axis=1).astype(BF16)
        x1 = x_ref[...] + jnp.dot(y, wo_ref[...], preferred_element_type=F32)
        hn = _rms(x1, pg_ref[...]).astype(BF16)
        gate = jax.nn.sigmoid(jnp.dot(hn, wg_ref[...], preferred_element_type=F32))
        pp = jnp.dot(p_ref[...].astype(BF16), wp_ref[...], preferred_element_type=F32)
        o_ref[...] = x1 + gate * pp

    row = lambda i: (i, 0)
    full = lambda i: (0, 0)
    return pl.pallas_call(
        body, grid=(t // tb,),
        in_specs=[pl.BlockSpec((tb, D), row), pl.BlockSpec((tb, D_SSM), row), pl.BlockSpec((tb, D_DN), row),
                  pl.BlockSpec((tb, D_SG), row), pl.BlockSpec((tb, D_PLE), row), pl.BlockSpec((D, D), full),
                  pl.BlockSpec((1, D), full), pl.BlockSpec((D, D), full), pl.BlockSpec((D_PLE, D), full)],
        out_specs=pl.BlockSpec((tb, D), row), out_shape=jax.ShapeDtypeStruct((t, D), F32),
        name=name, compiler_params=_cp("parallel"))(x, *ys, p, w_out, pg, w_gate, w_ple)


def _out_bwd(x, ys, p, dx2, w_out, pg, w_gate, w_ple, tb, name):
    t = x.shape[0]

    def body(x_ref, y0, y1, y2, p_ref, d_ref, wo_ref, pg_ref, wg_ref, wp_ref,
             dx_ref, dy0, dy1, dy2, dwo_ref, dpg_ref, dwg_ref, dwp_ref):
        accs = (dwo_ref, dpg_ref, dwg_ref, dwp_ref)

        @pl.when(pl.program_id(0) == 0)
        def _():
            for r in accs:
                r[...] = jnp.zeros_like(r)

        y = jnp.concatenate([y0[...], y1[...], y2[...]], axis=1).astype(BF16)
        x1 = x_ref[...] + jnp.dot(y, wo_ref[...], preferred_element_type=F32)
        hn, rms_vjp = jax.vjp(_rms, x1, pg_ref[...])
        hb = hn.astype(BF16)
        gate = jax.nn.sigmoid(jnp.dot(hb, wg_ref[...], preferred_element_type=F32))
        pb = p_ref[...].astype(BF16)
        pp = jnp.dot(pb, wp_ref[...], preferred_element_type=F32)
        d2 = d_ref[...]
        dpp = (d2 * gate).astype(BF16)
        dlog = (d2 * pp * gate * (1.0 - gate)).astype(BF16)
        dwp_ref[...] += _dg(pb, dpp, 0, 0)
        dwg_ref[...] += _dg(hb, dlog, 0, 0)
        dx1_n, dpg = rms_vjp(_dg(dlog, wg_ref[...], 1, 1))
        dpg_ref[...] += dpg
        dx1 = d2 + dx1_n
        dx_ref[...] = dx1
        db = dx1.astype(BF16)
        dwo_ref[...] += _dg(y, db, 0, 0)
        dy = _dg(db, wo_ref[...], 1, 1)
        dy0[...] = dy[:, :D_SSM]
        dy1[...] = dy[:, D_SSM:D_SSM + D_DN]
        dy2[...] = dy[:, D_SSM + D_DN:]

    row = lambda i: (i, 0)
    full = lambda i: (0, 0)
    acts = [pl.BlockSpec((tb, D), row), pl.BlockSpec((tb, D_SSM), row), pl.BlockSpec((tb, D_DN), row), pl.BlockSpec((tb, D_SG), row)]
    wts = [pl.BlockSpec((D, D), full), pl.BlockSpec((1, D), full), pl.BlockSpec((D, D), full), pl.BlockSpec((D_PLE, D), full)]
    return pl.pallas_call(
        body, grid=(t // tb,),
        in_specs=acts + [pl.BlockSpec((tb, D_PLE), row), pl.BlockSpec((tb, D), row)] + wts,
        out_specs=acts + wts,
        out_shape=[jax.ShapeDtypeStruct((t, n), F32) for n in (D, D_SSM, D_DN, D_SG)]
        + [jax.ShapeDtypeStruct(sh, F32) for sh in ((D, D), (1, D), (D, D), (D_PLE, D))],
        name=name, compiler_params=_cp("arbitrary"))(x, *ys, p, dx2, w_out, pg, w_gate, w_ple)


def _loss_head(x, fg, target, tb, name):
    t = x.shape[0]

    def body(x_ref, g_ref, t_ref, dx_ref, dg_ref, loss_ref):
        @pl.when(pl.program_id(0) == 0)
        def _():
            dg_ref[...] = jnp.zeros_like(dg_ref)
            loss_ref[...] = jnp.zeros_like(loss_ref)

        y, vjp = jax.vjp(_rms, x_ref[...], g_ref[...])
        err = y - t_ref[...]
        loss_ref[...] += jnp.zeros_like(loss_ref) + 0.5 * jnp.sum(err * err) / D
        dx, dg = vjp(err / D)
        dx_ref[...] = dx
        dg_ref[...] += dg

    row = lambda i: (i, 0)
    full = lambda i: (0, 0)
    return pl.pallas_call(
        body, grid=(t // tb,),
        in_specs=[pl.BlockSpec((tb, D), row), pl.BlockSpec((1, D), full), pl.BlockSpec((tb, D), row)],
        out_specs=[pl.BlockSpec((tb, D), row), pl.BlockSpec((1, D), full), pl.BlockSpec((1, LANES), full)],
        out_shape=[jax.ShapeDtypeStruct((t, D), F32), jax.ShapeDtypeStruct((1, D), F32), jax.ShapeDtypeStruct((1, LANES), F32)],
        name=name, compiler_params=_cp("arbitrary"))(x, fg, target)


def _all_gather(block, name):
    rows = block.shape[0]

    def body(x_ref, out_ref, send_sems, recv_sems, local_sem):
        x, y, c = lax.axis_index("x"), lax.axis_index("y"), lax.axis_index("c")
        me, sibling = (x, y, c), (x, y, 1 - c)
        chips = [(1 - x, y), (x, 1 - y), (1 - x, 1 - y)]

        def slot(px, py, pc):
            return out_ref.at[4 * px + 2 * py + pc]

        def copy(k, blk, to, src=None):
            return pltpu.make_async_remote_copy(
                src_ref=slot(*blk) if src is None else src, dst_ref=slot(*blk),
                send_sem=send_sems.at[k], recv_sem=recv_sems.at[k], device_id=to, device_id_type=pl.DeviceIdType.MESH)

        mine = pltpu.make_async_copy(x_ref, slot(*me), local_sem)
        mine.start()
        first = [copy(0, me, sibling, src=x_ref)]
        first += [copy(1 + j, me, (*chip, c), src=x_ref) for j, chip in enumerate(chips)]
        for cp in first:
            cp.start()
        passed = [copy(4 + j, (*chip, c), sibling) for j, chip in enumerate(chips)]
        for j, chip in enumerate(chips):
            copy(1 + j, (*chip, c), me).wait_recv()
            passed[j].start()
        copy(0, sibling, me).wait_recv()
        for j, chip in enumerate(chips):
            copy(4 + j, (*chip, 1 - c), me).wait_recv()
        for cp in first + passed:
            cp.wait_send()
        mine.wait()

    return pl.pallas_call(
        body, out_shape=jax.ShapeDtypeStruct((N_DEV, rows, LANES), block.dtype),
        in_specs=[pl.BlockSpec(memory_space=pl.ANY)], out_specs=pl.BlockSpec(memory_space=pl.ANY),
        scratch_shapes=[pltpu.SemaphoreType.DMA((7,)), pltpu.SemaphoreType.DMA((7,)), pltpu.SemaphoreType.DMA],
        name=name)(block)


def _all_to_all(blocks, name):
    rows = blocks.shape[1]

    def body(src_ref, dst_ref, send_sems, recv_sems, local_sem):
        x, y, c = lax.axis_index("x"), lax.axis_index("y"), lax.axis_index("c")
        me = 4 * x + 2 * y + c
        mine = pltpu.make_async_copy(src_ref.at[me], dst_ref.at[me], local_sem)
        mine.start()
        copies = []
        for j in range(1, N_DEV):
            px = 1 - x if j & 4 else x
            py = 1 - y if j & 2 else y
            pc = 1 - c if j & 1 else c
            cp = pltpu.make_async_remote_copy(
                src_ref=src_ref.at[4 * px + 2 * py + pc], dst_ref=dst_ref.at[me],
                send_sem=send_sems.at[j - 1], recv_sem=recv_sems.at[j - 1],
                device_id=(px, py, pc), device_id_type=pl.DeviceIdType.MESH)
            cp.start()
            copies.append(cp)
        for cp in copies:
            cp.wait_send()
            cp.wait_recv()
        mine.wait()

    return pl.pallas_call(
        body, out_shape=jax.ShapeDtypeStruct((N_DEV, rows, LANES), blocks.dtype),
        in_specs=[pl.BlockSpec(memory_space=pl.ANY)], out_specs=pl.BlockSpec(memory_space=pl.ANY),
        scratch_shapes=[pltpu.SemaphoreType.DMA((7,)), pltpu.SemaphoreType.DMA((7,)), pltpu.SemaphoreType.DMA],
        name=name)(blocks)


def _sum_adamw(g8, w, m, v, name):
    rows = w.shape[0]
    rb = PACK_ROWS

    def body(g_ref, w_ref, m_ref, v_ref, go_ref, d_ref, mo_ref, vo_ref):
        g = g_ref[0]
        for k in range(1, N_DEV):
            g = g + g_ref[k]
        mn = ADAM_B1 * m_ref[...] + (1.0 - ADAM_B1) * g
        vn = ADAM_B2 * v_ref[...] + (1.0 - ADAM_B2) * jnp.square(g)
        m_hat = mn / (1.0 - ADAM_B1 ** ADAM_STEP)
        v_hat = vn / (1.0 - ADAM_B2 ** ADAM_STEP)
        go_ref[...] = g
        d_ref[...] = -ADAM_LR * (m_hat / (jnp.sqrt(v_hat) + ADAM_EPS) + ADAM_WD * w_ref[...])
        mo_ref[...] = mn
        vo_ref[...] = vn

    row = lambda i: (i, 0)
    return pl.pallas_call(
        body, grid=(rows // rb,),
        in_specs=[pl.BlockSpec((N_DEV, rb, LANES), lambda i: (0, i, 0))] + [pl.BlockSpec((rb, LANES), row)] * 3,
        out_specs=[pl.BlockSpec((rb, LANES), row)] * 4,
        out_shape=[jax.ShapeDtypeStruct((rows, LANES), F32)] * 4,
        name=name, compiler_params=_cp("parallel"))(g8, w, m, v)


def _pack(arrs):
    flat = jnp.concatenate([a.reshape(-1).astype(F32) for a in arrs])
    n = flat.shape[0]
    gran = PACK_ROWS * LANES
    total = -(-n // gran) * gran
    return jnp.pad(flat, (0, total - n)).reshape(total // LANES, LANES)


def _unpack(pack, shapes):
    flat = pack.reshape(-1)
    out, off = [], 0
    for sh in shapes:
        n = 1
        for d in sh:
            n *= d
        out.append(flat[off:off + n].reshape(sh))
        off += n
    return out


def _to_dest_blocks(full, axis):
    sh = list(full.shape)
    sh[axis:axis + 1] = [N_DEV, sh[axis] // N_DEV]
    return jnp.moveaxis(full.reshape(sh), axis, 0)


def _from_gathered(g, axis):
    m = jnp.moveaxis(g, 0, axis)
    sh = list(m.shape)
    sh[axis:axis + 2] = [sh[axis] * sh[axis + 1]]
    return m.reshape(sh)


def _reorder_w_in(w):
    return jnp.concatenate([w[:, :2048], w[:, 2056:3336], w[:, 2048:2056], jnp.zeros((D, ZW - 3336), w.dtype)], axis=1)


def _restore_dw_in(dw):
    return jnp.concatenate([dw[:, :2048], dw[:, 3328:3336], dw[:, 2048:3328]], axis=1)


def _local_step(x, p, wts, target):
    bl, s, _ = x.shape
    t = bl * s
    depth = p.shape[0]
    tb = 256
    sg_tb = 512 if t % 512 == 0 else SG_C
    prep_tb = 256
    xs = [x.reshape(t, D)]
    saved = []
    for i in range(depth):
        li = f"l{i}"
        ng = wts['norm_g'][i].reshape(1, D)
        w_in = _reorder_w_in(wts['w_in'][i]).astype(BF16)
        s5_par_in = (wts['ssm_a_re'][i], wts['ssm_a_im'][i], wts['ssm_b_re'][i], wts['ssm_b_im'][i],
                     wts['ssm_c_re'][i], wts['ssm_c_im'][i], wts['ssm_d'][i], wts['ssm_log_step'][i])
        tabs, tab_vjp = jax.vjp(_s5_tables, *s5_par_in)
        s5_par = (*tabs, wts['ssm_w_glu'][i], wts['ssm_b_glu'][i].reshape(1, D_SSM))
        conv8 = jnp.pad(wts['dn_conv_w'][i], ((0, 4), (0, 0)))
        dn_par = (jnp.repeat(wts['dn_a_log'][i], DH).reshape(1, D_DN), jnp.repeat(wts['dn_dt_bias'][i], DH).reshape(1, D_DN),
                  wts['dn_norm_g'][i].reshape(1, DH))
        sg_par = (wts['sg_ln_g'][i].reshape(1, D_SG), wts['sg_ln_b'][i].reshape(1, D_SG), wts['sg_w'][i],
                  jnp.pad(jnp.transpose(wts['sg_b'][i]), ((0, 0), (0, LANES - 4))))
        out_par = (wts['w_out'][i].astype(BF16), wts['ple_norm_g'][i].reshape(1, D), wts['w_ple_gate'][i].astype(BF16),
                   wts['w_ple'][i].astype(BF16))
        pi = p[i].reshape(t, D_PLE)

        z_ssm, z_qkv, z_gdn, z_sg, z_ab = _in_proj_fwd(xs[i], ng, w_in, tb, f"in_proj_fwd_{li}")
        y_ssm, carries = _s5_fwd(z_ssm, s5_par, bl, s, f"s5_fwd_{li}")
        qkvn = _dn_prep_fwd(z_qkv, conv8, bl, s, prep_tb, f"dn_prep_fwd_{li}")
        y_dn, states = _dn_scan_fwd(qkvn, z_ab, z_gdn, dn_par, bl, s, f"dn_scan_fwd_{li}")
        y_sg = _sg_fwd(z_sg, sg_par, sg_tb, f"sg_fwd_{li}")
        ys = (y_ssm, y_dn, y_sg)
        xs.append(_out_fwd(xs[i], ys, pi, *out_par, tb, f"out_fwd_{li}"))
        saved.append(dict(ng=ng, w_in=w_in, tab_vjp=tab_vjp, s5_par=s5_par, conv8=conv8, dn_par=dn_par, sg_par=sg_par,
                          out_par=out_par, pi=pi, z=(z_ssm, z_qkv, z_gdn, z_sg, z_ab), carries=carries, qkvn=qkvn,
                          states=states, ys=ys))

    dx, dfg, loss_vec = _loss_head(xs[depth], wts['final_norm_g'].reshape(1, D), target.reshape(t, D), tb, "loss_head")
    grads = {n: [None] * depth for n in WEIGHTS if n != 'final_norm_g'}
    grads['final_norm_g'] = dfg.reshape(D)
    for i in reversed(range(depth)):
        li = f"l{i}"
        sv = saved[i]
        z_ssm, z_qkv, z_gdn, z_sg, z_ab = sv['z']
        dx_res, dy_ssm, dy_dn, dy_sg, dwo, dpg, dwg, dwp = _out_bwd(xs[i], sv['ys'], sv['pi'], dx, *sv['out_par'], tb, f"out_bwd_{li}")
        dz_sg, dlng, dlnb, dsgw, dbsp = _sg_bwd(z_sg, dy_sg, sv['sg_par'], sg_tb, f"sg_bwd_{li}")
        dqkvn, dz_ab, dz_gdn, dal, ddt, dng = _dn_scan_bwd(sv['qkvn'], z_ab, z_gdn, sv['states'], dy_dn, sv['dn_par'], bl, s,
                                                          f"dn_scan_bwd_{li}")
        dz_qkv, dconv = _dn_prep_bwd(z_qkv, dqkvn, sv['conv8'], bl, s, prep_tb, f"dn_prep_bwd_{li}")
        dz_ssm, dbb, dcb, dld, dlp, ddv, dwglu, dbglu = _s5_bwd(z_ssm, sv['carries'], dy_ssm, sv['s5_par'], bl, s, f"s5_bwd_{li}")
        dzs = (dz_ssm, dz_qkv, dz_gdn, dz_sg, dz_ab)
        dx, dnorm = _in_proj_bwd_dx(xs[i], sv['ng'], sv['w_in'], dzs, dx_res, tb, f"in_proj_bwd_dx_{li}")
        dws = _in_proj_bwd_dw(xs[i], sv['ng'], dzs, tb, f"in_proj_bwd_dw_{li}")
        ds5 = sv['tab_vjp']((dbb, dcb, dld, dlp, ddv))
        for n, gval in zip(('ssm_a_re', 'ssm_a_im', 'ssm_b_re', 'ssm_b_im', 'ssm_c_re', 'ssm_c_im', 'ssm_d', 'ssm_log_step'), ds5):
            grads[n][i] = gval
        grads['norm_g'][i] = dnorm.reshape(D)
        grads['w_in'][i] = _restore_dw_in(jnp.concatenate(dws, axis=1))
        grads['ssm_w_glu'][i] = dwglu
        grads['ssm_b_glu'][i] = dbglu.reshape(D_SSM)
        grads['dn_conv_w'][i] = dconv[:4]
        grads['dn_a_log'][i] = dal.reshape(H, DH).sum(axis=1)
        grads['dn_dt_bias'][i] = ddt.reshape(H, DH).sum(axis=1)
        grads['dn_norm_g'][i] = dng.reshape(DH)
        grads['sg_ln_g'][i] = dlng.reshape(D_SG)
        grads['sg_ln_b'][i] = dlnb.reshape(D_SG)
        grads['sg_w'][i] = dsgw
        grads['sg_b'][i] = jnp.transpose(dbsp[:, :4])
        grads['w_out'][i] = dwo
        grads['ple_norm_g'][i] = dpg.reshape(D)
        grads['w_ple_gate'][i] = dwg
        grads['w_ple'][i] = dwp
    grads = {n: (g if n == 'final_norm_g' else jnp.stack(g)) for n, g in grads.items()}
    return loss_vec[0, 0], dx.reshape(bl, s, D), grads


def kernel(x, p, norm_g, w_in, ssm_a_re, ssm_a_im, ssm_b_re, ssm_b_im, ssm_c_re, ssm_c_im, ssm_d, ssm_log_step, ssm_w_glu, ssm_b_glu, dn_conv_w, dn_a_log, dn_dt_bias, dn_norm_g, sg_ln_g, sg_ln_b, sg_w, sg_b, w_out, ple_norm_g, w_ple_gate, w_ple, final_norm_g, loss_target, m_norm_g, m_w_in, m_ssm_a_re, m_ssm_a_im, m_ssm_b_re, m_ssm_b_im, m_ssm_c_re, m_ssm_c_im, m_ssm_d, m_ssm_log_step, m_ssm_w_glu, m_ssm_b_glu, m_dn_conv_w, m_dn_a_log, m_dn_dt_bias, m_dn_norm_g, m_sg_ln_g, m_sg_ln_b, m_sg_w, m_sg_b, m_w_out, m_ple_norm_g, m_w_ple_gate, m_w_ple, m_final_norm_g, v_norm_g, v_w_in, v_ssm_a_re, v_ssm_a_im, v_ssm_b_re, v_ssm_b_im, v_ssm_c_re, v_ssm_c_im, v_ssm_d, v_ssm_log_step, v_ssm_w_glu, v_ssm_b_glu, v_dn_conv_w, v_dn_a_log, v_dn_dt_bias, v_dn_norm_g, v_sg_ln_g, v_sg_ln_b, v_sg_w, v_sg_b, v_w_out, v_ple_norm_g, v_w_ple_gate, v_w_ple, v_final_norm_g):
    w_loc = dict(zip(WEIGHTS, (norm_g, w_in, ssm_a_re, ssm_a_im, ssm_b_re, ssm_b_im, ssm_c_re, ssm_c_im, ssm_d, ssm_log_step,
                               ssm_w_glu, ssm_b_glu, dn_conv_w, dn_a_log, dn_dt_bias, dn_norm_g, sg_ln_g, sg_ln_b, sg_w, sg_b,
                               w_out, ple_norm_g, w_ple_gate, w_ple, final_norm_g)))
    m_loc = dict(zip(WEIGHTS, (m_norm_g, m_w_in, m_ssm_a_re, m_ssm_a_im, m_ssm_b_re, m_ssm_b_im, m_ssm_c_re, m_ssm_c_im, m_ssm_d,
                               m_ssm_log_step, m_ssm_w_glu, m_ssm_b_glu, m_dn_conv_w, m_dn_a_log, m_dn_dt_bias, m_dn_norm_g,
                               m_sg_ln_g, m_sg_ln_b, m_sg_w, m_sg_b, m_w_out, m_ple_norm_g, m_w_ple_gate, m_w_ple, m_final_norm_g)))
    v_loc = dict(zip(WEIGHTS, (v_norm_g, v_w_in, v_ssm_a_re, v_ssm_a_im, v_ssm_b_re, v_ssm_b_im, v_ssm_c_re, v_ssm_c_im, v_ssm_d,
                               v_ssm_log_step, v_ssm_w_glu, v_ssm_b_glu, v_dn_conv_w, v_dn_a_log, v_dn_dt_bias, v_dn_norm_g,
                               v_sg_ln_g, v_sg_ln_b, v_sg_w, v_sg_b, v_w_out, v_ple_norm_g, v_w_ple_gate, v_w_ple, v_final_norm_g)))

    shard_shapes = [w_loc[n].shape for n in SHARDED_ORDER]
    gathered = _all_gather(_pack([w_loc[n] for n in SHARDED_ORDER]), "gather_weights")
    per_dev = [_unpack(gathered[k], shard_shapes) for k in range(N_DEV)]
    full = dict(w_loc)
    for j, n in enumerate(SHARDED_ORDER):
        full[n] = _from_gathered(jnp.stack([per_dev[k][j] for k in range(N_DEV)]), SHARDED[n])

    loss_part, grad_x, grads = _local_step(x, p, full, loss_target)

    dest = [_to_dest_blocks(grads[n], SHARDED[n]) for n in SHARDED_ORDER]
    sh_pack8 = jnp.stack([_pack([d[k] for d in dest]) for k in range(N_DEV)])
    sh_recv = _all_to_all(sh_pack8, "scatter_grads")
    rep_pack = _pack([grads[n] for n in REPLICATED_ORDER] + [loss_part.reshape(1)])
    rep_recv = _all_gather(rep_pack, "gather_small_grads")

    sh_out = _sum_adamw(sh_recv, _pack([w_loc[n] for n in SHARDED_ORDER]), _pack([m_loc[n] for n in SHARDED_ORDER]),
                        _pack([v_loc[n] for n in SHARDED_ORDER]), "adamw_sharded")
    one = jnp.zeros((1,), F32)
    rep_out = _sum_adamw(rep_recv, _pack([w_loc[n] for n in REPLICATED_ORDER] + [one]),
                         _pack([m_loc[n] for n in REPLICATED_ORDER] + [one]),
                         _pack([v_loc[n] for n in REPLICATED_ORDER] + [one]), "adamw_replicated")
    rep_shapes = [w_loc[n].shape for n in REPLICATED_ORDER] + [(1,)]
    outs = {}
    for kind, sh_p, rep_p in zip(('g', 'd', 'm', 'v'), sh_out, rep_out):
        vals = dict(zip(SHARDED_ORDER, _unpack(sh_p, shard_shapes)))
        vals.update(zip(REPLICATED_ORDER + ['loss'], _unpack(rep_p, rep_shapes)))
        outs[kind] = vals
    loss = outs['g']['loss'].reshape(())
    return (loss, grad_x, *[outs['g'][n] for n in WEIGHTS], *[outs['d'][n] for n in WEIGHTS],
            *[outs['m'][n] for n in WEIGHTS], *[outs['v'][n] for n in WEIGHTS])
```

```python
import functools

import jax
import jax.numpy as jnp
from jax import lax
from jax.experimental import pallas as pl
from jax.experimental.pallas import tpu as pltpu

F32 = jnp.float32
BF16 = jnp.bfloat16
EPS = 1e-6

D = 1024
D_PLE = 256
D_SSM = 256
D_DN = 512
D_SG = 256
G = 16
CG = 16
NS = 64
NRE = G * NS
H = 4
DH = 128
DN_C = 128
SG_C = 128
ZW = 3456
Z_PIECES = (512, 1536, 512, 768, 128)
N_DEV = 8
LANES = 128
PACK_ROWS = 512
VMEM_LIMIT = 56 * 1024 * 1024

ADAM_LR = 0.001
ADAM_B1 = 0.9
ADAM_B2 = 0.999
ADAM_EPS = 1e-08
ADAM_WD = 0.01
ADAM_STEP = 10

S5_L = 128
S5_SHIFTS = (1, 2, 4, 8, 16, 32, 64)

WEIGHTS = ['norm_g', 'w_in', 'ssm_a_re', 'ssm_a_im', 'ssm_b_re', 'ssm_b_im', 'ssm_c_re', 'ssm_c_im', 'ssm_d',
           'ssm_log_step', 'ssm_w_glu', 'ssm_b_glu', 'dn_conv_w', 'dn_a_log', 'dn_dt_bias', 'dn_norm_g', 'sg_ln_g',
           'sg_ln_b', 'sg_w', 'sg_b', 'w_out', 'ple_norm_g', 'w_ple_gate', 'w_ple', 'final_norm_g']
SHARDED = {'w_in': 2, 'ssm_w_glu': 1, 'dn_conv_w': 2, 'w_out': 1, 'w_ple_gate': 1, 'w_ple': 2}
SHARDED_ORDER = ['w_in', 'ssm_w_glu', 'dn_conv_w', 'w_out', 'w_ple_gate', 'w_ple']
REPLICATED_ORDER = [n for n in WEIGHTS if n not in SHARDED]


def _cp(*sem):
    return pltpu.CompilerParams(dimension_semantics=sem, vmem_limit_bytes=VMEM_LIMIT)


def _dg(a, b, ca, cb, precision=None):
    return lax.dot_general(a, b, (((ca,), (cb,)), ((), ())), precision=precision, preferred_element_type=F32)


@jax.custom_vjp
def _mm(a, b):
    return _dg(a.astype(BF16), b.astype(BF16), 1, 0)


def _mm_fwd(a, b):
    return _mm(a, b), (a, b)


def _mm_bwd(res, g):
    a, b = res
    gb = g.astype(BF16)
    return _dg(gb, b.astype(BF16), 1, 1), _dg(a.astype(BF16), gb, 0, 0)


_mm.defvjp(_mm_fwd, _mm_bwd)


@jax.custom_vjp
def _mm_nt(a, b):
    return _dg(a.astype(BF16), b.astype(BF16), 1, 1)


def _mm_nt_fwd(a, b):
    return _mm_nt(a, b), (a, b)


def _mm_nt_bwd(res, g):
    a, b = res
    gb = g.astype(BF16)
    return _dg(gb, b.astype(BF16), 1, 0), _dg(gb, a.astype(BF16), 0, 0)


_mm_nt.defvjp(_mm_nt_fwd, _mm_nt_bwd)


@jax.custom_vjp
def _mm_tn(a, b):
    return _dg(a.astype(BF16), b.astype(BF16), 0, 0)


def _mm_tn_fwd(a, b):
    return _mm_tn(a, b), (a, b)


def _mm_tn_bwd(res, g):
    a, b = res
    gb = g.astype(BF16)
    return _dg(b.astype(BF16), gb, 1, 1), _dg(a.astype(BF16), gb, 1, 0)


_mm_tn.defvjp(_mm_tn_fwd, _mm_tn_bwd)


def _split(x, n):
    pieces = []
    for _ in range(n - 1):
        hi = x.astype(BF16)
        pieces.append(hi)
        x = x - hi.astype(F32)
    pieces.append(x.astype(BF16))
    return pieces


def _dg3(a, b, ca, cb):
    a_hi, a_lo = _split(a, 2)
    b_hi, b_lo = _split(b, 2)
    return _dg(a_hi, b_hi, ca, cb) + (_dg(a_hi, b_lo, ca, cb) + _dg(a_lo, b_hi, ca, cb))


@jax.custom_vjp
def _dot3(a, b):
    return _dg3(a, b, 1, 0)


def _dot3_fwd(a, b):
    return _dot3(a, b), (a, b)


def _dot3_bwd(res, g):
    a, b = res
    return _dg3(g, b, 1, 1), _dg3(a, g, 0, 0)


_dot3.defvjp(_dot3_fwd, _dot3_bwd)


def _dg_sel(x, e, cx, ce, x_first):
    eb = e.astype(BF16)
    out = None
    for piece in reversed(_split(x, 3)):
        term = _dg(piece, eb, cx, ce) if x_first else _dg(eb, piece, ce, cx)
        out = term if out is None else out + term
    return out


@jax.custom_vjp
def _sel_r(x, e):
    return _dg_sel(x, e, 1, 0, True)


def _sel_r_fwd(x, e):
    return _sel_r(x, e), e


def _sel_r_bwd(e, g):
    return _dg_sel(g, e, 1, 1, True), jnp.zeros_like(e)


_sel_r.defvjp(_sel_r_fwd, _sel_r_bwd)


@jax.custom_vjp
def _sel_l(e, x):
    return _dg_sel(x, e, 0, 1, False)


def _sel_l_fwd(e, x):
    return _sel_l(e, x), e


def _sel_l_bwd(e, g):
    return jnp.zeros_like(e), _dg_sel(g, e, 0, 0, False)


_sel_l.defvjp(_sel_l_fwd, _sel_l_bwd)


def _make_shift(rows):
    @functools.partial(jax.custom_vjp, nondiff_argnums=(1,))
    def shift(x, d):
        r = lax.broadcasted_iota(jnp.int32, x.shape, 0)
        return jnp.where(r >= d, pltpu.roll(x, d, 0), 0.0)

    def fwd(x, d):
        return shift(x, d), None

    def bwd(d, _, g):
        r = lax.broadcasted_iota(jnp.int32, g.shape, 0)
        return (jnp.where(r < rows - d, pltpu.roll(g, rows - d, 0), 0.0),)

    shift.defvjp(fwd, bwd)
    return shift


def _rms(x, g):
    return x * lax.rsqrt(jnp.mean(x * x, axis=-1, keepdims=True) + EPS) * g


def _silu(x):
    return x * jax.nn.sigmoid(x)


def _in_proj_fwd(x, g, w, tb, name):
    t = x.shape[0]

    def body(x_ref, g_ref, w_ref, *z_refs):
        h = _rms(x_ref[...], g_ref[...])
        z = jnp.dot(h.astype(BF16), w_ref[...], preferred_element_type=F32)
        off = 0
        for z_ref, n in zip(z_refs, Z_PIECES):
            z_ref[...] = z[:, off:off + n]
            off += n

    row = lambda i: (i, 0)
    full = lambda i: (0, 0)
    return pl.pallas_call(
        body, grid=(t // tb,),
        in_specs=[pl.BlockSpec((tb, D), row), pl.BlockSpec((1, D), full), pl.BlockSpec((D, ZW), full)],
        out_specs=[pl.BlockSpec((tb, n), row) for n in Z_PIECES],
        out_shape=[jax.ShapeDtypeStruct((t, n), F32) for n in Z_PIECES],
        name=name, compiler_params=_cp("parallel"))(x, g, w)


def _in_proj_bwd_dx(x, g, w, dzs, dx_res, tb, name):
    t = x.shape[0]

    def body(x_ref, g_ref, w_ref, d0, d1, d2, d3, d4, dxr_ref, dx_ref, dg_ref):
        @pl.when(pl.program_id(0) == 0)
        def _():
            dg_ref[...] = jnp.zeros_like(dg_ref)

        dz = jnp.concatenate([d0[...], d1[...], d2[...], d3[...], d4[...]], axis=1).astype(BF16)
        dh = _dg(dz, w_ref[...], 1, 1)
        _, vjp = jax.vjp(_rms, x_ref[...], g_ref[...])
        dx, dg = vjp(dh)
        dx_ref[...] = dx + dxr_ref[...]
        dg_ref[...] += dg

    row = lambda i: (i, 0)
    full = lambda i: (0, 0)
    return pl.pallas_call(
        body, grid=(t // tb,),
        in_specs=[pl.BlockSpec((tb, D), row), pl.BlockSpec((1, D), full), pl.BlockSpec((D, ZW), full)]
        + [pl.BlockSpec((tb, n), row) for n in Z_PIECES] + [pl.BlockSpec((tb, D), row)],
        out_specs=[pl.BlockSpec((tb, D), row), pl.BlockSpec((1, D), full)],
        out_shape=[jax.ShapeDtypeStruct((t, D), F32), jax.ShapeDtypeStruct((1, D), F32)],
        name=name, compiler_params=_cp("arbitrary"))(x, g, w, *dzs, dx_res)


def _in_proj_bwd_dw(x, g, dzs, tb, name):
    t = x.shape[0]

    def body(x_ref, g_ref, d0, d1, d2, d3, d4, *dw_refs):
        @pl.when(pl.program_id(0) == 0)
        def _():
            for r in dw_refs:
                r[...] = jnp.zeros_like(r)

        h = _rms(x_ref[...], g_ref[...]).astype(BF16)
        for d_ref, dw_ref in zip((d0, d1, d2, d3, d4), dw_refs):
            dw_ref[...] += _dg(h, d_ref[...].astype(BF16), 0, 0)

    row = lambda i: (i, 0)
    full = lambda i: (0, 0)
    return pl.pallas_call(
        body, grid=(t // tb,),
        in_specs=[pl.BlockSpec((tb, D), row), pl.BlockSpec((1, D), full)] + [pl.BlockSpec((tb, n), row) for n in Z_PIECES],
        out_specs=[pl.BlockSpec((D, n), full) for n in Z_PIECES],
        out_shape=[jax.ShapeDtypeStruct((D, n), F32) for n in Z_PIECES],
        name=name, compiler_params=_cp("arbitrary"))(x, g, *dzs)


def _s5_tables(a_re, a_im, b_re, b_im, c_re, c_im, d_skip, log_step):
    step = jnp.exp(log_step)[:, None]

    def lam_pow(k):
        mag = jnp.exp(k * a_re * step)
        ang = k * a_im * step
        return mag * jnp.cos(ang), mag * jnp.sin(ang)

    lam_re, lam_im = lam_pow(1.0)
    den = a_re * a_re + a_im * a_im
    nr, ni = lam_re - 1.0, lam_im
    f_re = (nr * a_re + ni * a_im) / den
    f_im = (ni * a_re - nr * a_im) / den
    bbar_re = f_re[..., None] * b_re - f_im[..., None] * b_im
    bbar_im = f_re[..., None] * b_im + f_im[..., None] * b_re
    eye = jnp.eye(G, dtype=F32)

    def blk_b(bb):
        return (jnp.transpose(bb, (0, 2, 1))[:, :, None, :] * eye[:, None, :, None]).reshape(D_SSM, NRE)

    def blk_c(cc):
        return (jnp.transpose(cc, (0, 2, 1))[:, :, None, :] * eye[:, None, :, None]).reshape(NRE, D_SSM)

    b_blk = jnp.concatenate([blk_b(bbar_re), blk_b(bbar_im)], axis=1)
    c_blk = jnp.concatenate([blk_c(c_re), -blk_c(c_im)], axis=0)
    ks = jnp.asarray(S5_SHIFTS, F32)[:, None, None]
    ld_re, ld_im = lam_pow(ks)
    ld = jnp.concatenate([ld_re.reshape(-1, 1, NRE), ld_im.reshape(-1, 1, NRE)], axis=-1)
    js = jnp.arange(1, S5_L + 1, dtype=F32)[:, None, None]
    lp_re, lp_im = lam_pow(js)
    lp = jnp.concatenate([lp_re.reshape(S5_L, NRE), lp_im.reshape(S5_L, NRE)], axis=-1)
    return b_blk, c_blk, ld, lp, d_skip.reshape(1, D_SSM)


def _s5_chunk_fn():
    shift = _make_shift(S5_L)

    def f(u, gate, cr, ci, b_blk, c_blk, lds, lp, dv, wglu, bglu):
        bu = _mm(u, b_blk)
        hr, hi = bu[:, :NRE], bu[:, NRE:]
        for ld, d in zip(lds, S5_SHIFTS):
            lr, li = ld[:, :NRE], ld[:, NRE:]
            sr, si = shift(hr, d), shift(hi, d)
            hr, hi = hr + lr * sr - li * si, hi + lr * si + li * sr
        pr, pi = lp[:, :NRE], lp[:, NRE:]
        hr, hi = hr + pr * cr - pi * ci, hi + pr * ci + pi * cr
        last = lax.broadcasted_iota(jnp.int32, hr.shape, 0) == S5_L - 1
        ncr = jnp.sum(jnp.where(last, hr, 0.0), axis=0, keepdims=True)
        nci = jnp.sum(jnp.where(last, hi, 0.0), axis=0, keepdims=True)
        y = _mm(jnp.concatenate([hr, hi], axis=1), c_blk) + dv * u
        y = jax.nn.gelu(y)
        y = y * jax.nn.sigmoid(_mm(y, wglu) + bglu)
        return y * _silu(gate), ncr, nci

    return f


def _s5_specs(n_s, rev):
    nd = len(S5_SHIFTS)
    if rev:
        row = lambda b, i: (b * n_s + n_s - 1 - i, 0)
        row3 = lambda b, i: (b * n_s + n_s - 1 - i, 0, 0)
    else:
        row = lambda b, i: (b * n_s + i, 0)
        row3 = lambda b, i: (b * n_s + i, 0, 0)
    full = lambda b, i: (0, 0)
    full3 = lambda b, i: (0, 0, 0)
    par = [pl.BlockSpec((D_SSM, 2 * NRE), full), pl.BlockSpec((2 * NRE, D_SSM), full), pl.BlockSpec((nd, 1, 2 * NRE), full3),
           pl.BlockSpec((S5_L, 2 * NRE), full), pl.BlockSpec((1, D_SSM), full), pl.BlockSpec((D_SSM, D_SSM), full),
           pl.BlockSpec((1, D_SSM), full)]
    par_shapes = [(D_SSM, 2 * NRE), (2 * NRE, D_SSM), (nd, 1, 2 * NRE), (S5_L, 2 * NRE), (1, D_SSM), (D_SSM, D_SSM), (1, D_SSM)]
    return row, row3, par, par_shapes


def _s5_fwd(zs, params, bl, s, name):
    n_s = s // S5_L
    nd = len(S5_SHIFTS)
    f = _s5_chunk_fn()
    row, row3, par, _ = _s5_specs(n_s, False)

    def body(z_ref, b_ref, c_ref, ld_ref, lp_ref, dv_ref, wg_ref, bg_ref, y_ref, car_ref, cs):
        @pl.when(pl.program_id(1) == 0)
        def _():
            cs[...] = jnp.zeros_like(cs)

        c = cs[...]
        car_ref[0] = c
        z = z_ref[...]
        out, ncr, nci = f(z[:, :D_SSM], z[:, D_SSM:], c[:, :NRE], c[:, NRE:], b_ref[...], c_ref[...],
                          [ld_ref[k] for k in range(nd)], lp_ref[...], dv_ref[...], wg_ref[...], bg_ref[...])
        y_ref[...] = out
        cs[:, :NRE] = ncr
        cs[:, NRE:] = nci

    t = bl * s
    return pl.pallas_call(
        body, grid=(bl, n_s),
        in_specs=[pl.BlockSpec((S5_L, 2 * D_SSM), row)] + par,
        out_specs=[pl.BlockSpec((S5_L, D_SSM), row), pl.BlockSpec((1, 1, 2 * NRE), row3)],
        out_shape=[jax.ShapeDtypeStruct((t, D_SSM), F32), jax.ShapeDtypeStruct((bl * n_s, 1, 2 * NRE), F32)],
        scratch_shapes=[pltpu.VMEM((1, 2 * NRE), F32)],
        name=name, compiler_params=_cp("arbitrary", "arbitrary"))(zs, *params)


def _s5_bwd(zs, carries, dy, params, bl, s, name):
    n_s = s // S5_L
    nd = len(S5_SHIFTS)
    f = _s5_chunk_fn()
    row, row3, par, par_shapes = _s5_specs(n_s, True)

    def body(z_ref, car_ref, dy_ref, b_ref, c_ref, ld_ref, lp_ref, dv_ref, wg_ref, bg_ref,
             dz_ref, db_ref, dc_ref, dld_ref, dlp_ref, ddv_ref, dwg_ref, dbg_ref, dcs):
        accs = (db_ref, dc_ref, dld_ref, dlp_ref, ddv_ref, dwg_ref, dbg_ref)

        @pl.when((pl.program_id(0) == 0) & (pl.program_id(1) == 0))
        def _():
            for r in accs:
                r[...] = jnp.zeros_like(r)

        @pl.when(pl.program_id(1) == 0)
        def _():
            dcs[...] = jnp.zeros_like(dcs)

        z = z_ref[...]
        c = car_ref[0]
        _, vjp = jax.vjp(f, z[:, :D_SSM], z[:, D_SSM:], c[:, :NRE], c[:, NRE:], b_ref[...], c_ref[...],
                         [ld_ref[k] for k in range(nd)], lp_ref[...], dv_ref[...], wg_ref[...], bg_ref[...])
        dc = dcs[...]
        du, dgate, dcr, dci, dbb, dcb, dlds, dlpb, ddvb, dwgb, dbgb = vjp((dy_ref[...], dc[:, :NRE], dc[:, NRE:]))
        dz_ref[...] = jnp.concatenate([du, dgate], axis=1)
        dcs[:, :NRE] = dcr
        dcs[:, NRE:] = dci
        db_ref[...] += dbb
        dc_ref[...] += dcb
        for k in range(nd):
            dld_ref[k] += dlds[k]
        dlp_ref[...] += dlpb
        ddv_ref[...] += ddvb
        dwg_ref[...] += dwgb
        dbg_ref[...] += dbgb

    t = bl * s
    return pl.pallas_call(
        body, grid=(bl, n_s),
        in_specs=[pl.BlockSpec((S5_L, 2 * D_SSM), row), pl.BlockSpec((1, 1, 2 * NRE), row3), pl.BlockSpec((S5_L, D_SSM), row)] + par,
        out_specs=[pl.BlockSpec((S5_L, 2 * D_SSM), row)] + par,
        out_shape=[jax.ShapeDtypeStruct((t, 2 * D_SSM), F32)] + [jax.ShapeDtypeStruct(sh, F32) for sh in par_shapes],
        scratch_shapes=[pltpu.VMEM((1, 2 * NRE), F32)],
        name=name, compiler_params=_cp("arbitrary", "arbitrary"))(zs, carries, dy, *params)


def _dn_post(c):
    s = _silu(c)
    parts = []
    for j in range(12):
        xj = s[:, j * DH:(j + 1) * DH]
        if j < 8:
            xj = xj * lax.rsqrt(jnp.sum(xj * xj, axis=-1, keepdims=True) + EPS)
        if j < 4:
            xj = xj * (DH ** -0.5)
        parts.append(xj)
    return jnp.concatenate(parts, axis=1)


def _dn_prep_fwd(zq, conv_w8, bl, s, tb, name):
    n_s = s // tb
    hb = tb // 8
    w3 = 3 * D_DN

    def body(cur_ref, prev_ref, w_ref, o_ref):
        i = pl.program_id(1)
        prev = jnp.where(i > 0, prev_ref[...], 0.0)
        ext = jnp.concatenate([prev, cur_ref[...]], axis=0)
        c = jnp.zeros((tb, w3), F32)
        for k in range(4):
            sh = ext if k == 3 else pltpu.roll(ext, 3 - k, 0)
            c = c + w_ref[k:k + 1, :] * sh[8:, :]
        o_ref[...] = _dn_post(c)

    row = lambda b, i: (b * n_s + i, 0)
    prv = lambda b, i: (jnp.maximum((b * n_s + i) * hb - 1, 0), 0)
    full = lambda b, i: (0, 0)
    t = bl * s
    return pl.pallas_call(
        body, grid=(bl, n_s),
        in_specs=[pl.BlockSpec((tb, w3), row), pl.BlockSpec((8, w3), prv), pl.BlockSpec((8, w3), full)],
        out_specs=pl.BlockSpec((tb, w3), row),
        out_shape=jax.ShapeDtypeStruct((t, w3), F32),
        name=name, compiler_params=_cp("parallel", "parallel"))(zq, zq, conv_w8)


def _dn_prep_bwd(zq, dqkv, conv_w8, bl, s, tb, name):
    n_s = s // tb
    hb = tb // 8
    w3 = 3 * D_DN
    n_blk8 = bl * s // 8

    def body(cur_ref, prev_ref, next_ref, d_ref, dnext_ref, w_ref, dz_ref, dw_ref):
        b, i = pl.program_id(0), pl.program_id(1)

        @pl.when((b == 0) & (i == 0))
        def _():
            dw_ref[...] = jnp.zeros_like(dw_ref)

        prev = jnp.where(i > 0, prev_ref[...], 0.0)
        nxt = jnp.where(i < n_s - 1, next_ref[...], 0.0)
        dnxt = jnp.where(i < n_s - 1, dnext_ref[...], 0.0)
        ext = jnp.concatenate([prev, cur_ref[...], nxt], axis=0)
        shifted = [ext if k == 3 else pltpu.roll(ext, 3 - k, 0) for k in range(4)]
        c2 = jnp.zeros((tb + 8, w3), F32)
        for k in range(4):
            c2 = c2 + w_ref[k:k + 1, :] * shifted[k][8:, :]
        dpost = jnp.concatenate([d_ref[...], dnxt], axis=0)
        _, vjp = jax.vjp(_dn_post, c2)
        (dc2,) = vjp(dpost)
        dz = jnp.zeros((tb, w3), F32)
        for k in range(4):
            up = dc2 if k == 3 else pltpu.roll(dc2, tb + 8 - (3 - k), 0)
            dz = dz + w_ref[k:k + 1, :] * up[:tb, :]
            dw_ref[k:k + 1, :] += jnp.sum(dc2[:tb, :] * shifted[k][8:8 + tb, :], axis=0, keepdims=True)
        dz_ref[...] = dz

    row = lambda b, i: (b * n_s + i, 0)
    prv = lambda b, i: (jnp.maximum((b * n_s + i) * hb - 1, 0), 0)
    nxt = lambda b, i: (jnp.minimum((b * n_s + i + 1) * hb, n_blk8 - 1), 0)
    full = lambda b, i: (0, 0)
    t = bl * s
    return pl.pallas_call(
        body, grid=(bl, n_s),
        in_specs=[pl.BlockSpec((tb, w3), row), pl.BlockSpec((8, w3), prv), pl.BlockSpec((8, w3), nxt),
                  pl.BlockSpec((tb, w3), row), pl.BlockSpec((8, w3), nxt), pl.BlockSpec((8, w3), full)],
        out_specs=[pl.BlockSpec((tb, w3), row), pl.BlockSpec((8, w3), full)],
        out_shape=[jax.ShapeDtypeStruct((t, w3), F32), jax.ShapeDtypeStruct((8, w3), F32)],
        name=name, compiler_params=_cp("arbitrary", "arbitrary"))(zq, zq, zq, dqkv, dqkv, conv_w8)


def _dn_chunk_fn():
    c_len = DN_C
    n_sq = c_len.bit_length() - 1

    def f(qkv, zab, zg, s0, s1, s2, s3, alog_e, dt_e, ng):
        states = (s0, s1, s2, s3)
        r = lax.broadcasted_iota(jnp.int32, (c_len, c_len), 0)
        c = lax.broadcasted_iota(jnp.int32, (c_len, c_len), 1)
        causal, strict, eye = r >= c, r > c, r == c
        tril = jnp.where(causal, 1.0, 0.0)
        ident = jnp.where(eye, 1.0, 0.0)
        rr = lax.broadcasted_iota(jnp.int32, (LANES, D_DN), 0)
        cc = lax.broadcasted_iota(jnp.int32, (LANES, D_DN), 1)
        e_a = jnp.where((cc >= rr * DH) & (cc < rr * DH + DH) & (rr < H), 1.0, 0.0)
        e_b = jnp.where((cc >= (rr - H) * DH) & (cc < (rr - H) * DH + DH) & (rr >= H) & (rr < 2 * H), 1.0, 0.0)
        a_e = _sel_r(zab, e_a)
        b_e = _sel_r(zab, e_b)
        beta = jax.nn.sigmoid(b_e)
        g = -jnp.exp(alog_e) * jax.nn.softplus(a_e + dt_e)
        gc = _sel_l(tril, g)
        glast = jnp.sum(g, axis=0, keepdims=True)
        eg = jnp.exp(gc)
        ekd = jnp.exp(glast - gc)
        dl = jnp.exp(glast)
        heads = range(H)
        sls = [slice(h * DH, (h + 1) * DH) for h in heads]
        qs = [qkv[:, h * DH:(h + 1) * DH] for h in heads]
        ks = [qkv[:, D_DN + h * DH:D_DN + (h + 1) * DH] for h in heads]
        vs = [qkv[:, 2 * D_DN + h * DH:2 * D_DN + (h + 1) * DH] for h in heads]
        ccols = [gc[:, sl] for sl in sls]
        decs = [jnp.where(causal, jnp.exp(jnp.where(causal, cl - jnp.transpose(cl), 0.0)), 0.0) for cl in ccols]
        kbs = [k * beta[:, sl] for k, sl in zip(ks, sls)]
        ms = [jnp.where(strict, _mm_nt(kb, k) * dec, 0.0) for kb, k, dec in zip(kbs, ks, decs)]
        ps = ms
        tinvs = [ident - m for m in ms]
        for _ in range(n_sq - 1):
            ps = [_dot3(p, p) for p in ps]
            tinvs = [t + _dot3(t, p) for t, p in zip(tinvs, ps)]
        sols = [_dot3(t, jnp.concatenate([v * beta[:, sl], kb * eg[:, sl]], axis=1))
                for t, v, kb, sl in zip(tinvs, vs, kbs, sls)]
        atts = [_mm_nt(q, k) * dec for q, k, dec in zip(qs, ks, decs)]
        vnews = [sol[:, :DH] - _mm(sol[:, DH:], st) for sol, st in zip(sols, states)]
        os_ = [_mm(q * eg[:, sl], st) + _mm(att, vn) for q, sl, st, att, vn in zip(qs, sls, states, atts, vnews)]
        new_states = [st * dl[:, sl] + _mm_tn(k * ekd[:, sl], vn) for st, sl, k, vn in zip(states, sls, ks, vnews)]
        ys = [_rms(o, ng) * _silu(zg[:, sl]) for o, sl in zip(os_, sls)]
        return (jnp.concatenate(ys, axis=1), *new_states)

    return f


def _dn_scan_specs(n_c, rev):
    if rev:
        row = lambda b, i: (b * n_c + n_c - 1 - i, 0)
        row4 = lambda b, i: (b * n_c + n_c - 1 - i, 0, 0, 0)
    else:
        row = lambda b, i: (b * n_c + i, 0)
        row4 = lambda b, i: (b * n_c + i, 0, 0, 0)
    full = lambda b, i: (0, 0)
    par = [pl.BlockSpec((1, D_DN), full), pl.BlockSpec((1, D_DN), full), pl.BlockSpec((1, DH), full)]
    par_shapes = [(1, D_DN), (1, D_DN), (1, DH)]
    return row, row4, par, par_shapes


def _dn_scan_fwd(qkv, zab, zg, params, bl, s, name):
    n_c = s // DN_C
    f = _dn_chunk_fn()
    row, row4, par, _ = _dn_scan_specs(n_c, False)

    def body(q_ref, ab_ref, zg_ref, al_ref, dt_ref, ng_ref, y_ref, st_ref, ssc):
        @pl.when(pl.program_id(1) == 0)
        def _():
            ssc[...] = jnp.zeros_like(ssc)

        sts = [ssc[h] for h in range(H)]
        for h in range(H):
            st_ref[0, h] = sts[h]
        outs = f(q_ref[...], ab_ref[...], zg_ref[...], *sts, al_ref[...], dt_ref[...], ng_ref[...])
        y_ref[...] = outs[0]
        for h in range(H):
            ssc[h] = outs[1 + h]

    t = bl * s
    return pl.pallas_call(
        body, grid=(bl, n_c),
        in_specs=[pl.BlockSpec((DN_C, 3 * D_DN), row), pl.BlockSpec((DN_C, LANES), row), pl.BlockSpec((DN_C, D_DN), row)] + par,
        out_specs=[pl.BlockSpec((DN_C, D_DN), row), pl.BlockSpec((1, H, DH, DH), row4)],
        out_shape=[jax.ShapeDtypeStruct((t, D_DN), F32), jax.ShapeDtypeStruct((bl * n_c, H, DH, DH), F32)],
        scratch_shapes=[pltpu.VMEM((H, DH, DH), F32)],
        name=name, compiler_params=_cp("arbitrary", "arbitrary"))(qkv, zab, zg, *params)


def _dn_scan_bwd(qkv, zab, zg, states, dy, params, bl, s, name):
    n_c = s // DN_C
    f = _dn_chunk_fn()
    row, row4, par, par_shapes = _dn_scan_specs(n_c, True)

    def body(q_ref, ab_ref, zg_ref, st_ref, dy_ref, al_ref, dt_ref, ng_ref,
             dq_ref, dab_ref, dzg_ref, dal_ref, ddt_ref, dng_ref, dsc):
        @pl.when((pl.program_id(0) == 0) & (pl.program_id(1) == 0))
        def _():
            for r in (dal_ref, ddt_ref, dng_ref):
                r[...] = jnp.zeros_like(r)

        @pl.when(pl.program_id(1) == 0)
        def _():
            dsc[...] = jnp.zeros_like(dsc)

        sts = [st_ref[0, h] for h in range(H)]
        _, vjp = jax.vjp(f, q_ref[...], ab_ref[...], zg_ref[...], *sts, al_ref[...], dt_ref[...], ng_ref[...])
        cts = vjp((dy_ref[...], *[dsc[h] for h in range(H)]))
        dq_ref[...] = cts[0]
        dab_ref[...] = cts[1]
        dzg_ref[...] = cts[2]
        for h in range(H):
            dsc[h] = cts[3 + h]
        dal_ref[...] += cts[3 + H]
        ddt_ref[...] += cts[4 + H]
        dng_ref[...] += cts[5 + H]

    t = bl * s
    return pl.pallas_call(
        body, grid=(bl, n_c),
        in_specs=[pl.BlockSpec((DN_C, 3 * D_DN), row), pl.BlockSpec((DN_C, LANES), row), pl.BlockSpec((DN_C, D_DN), row),
                  pl.BlockSpec((1, H, DH, DH), row4), pl.BlockSpec((DN_C, D_DN), row)] + par,
        out_specs=[pl.BlockSpec((DN_C, 3 * D_DN), row), pl.BlockSpec((DN_C, LANES), row), pl.BlockSpec((DN_C, D_DN), row)] + par,
        out_shape=[jax.ShapeDtypeStruct((t, 3 * D_DN), F32), jax.ShapeDtypeStruct((t, LANES), F32),
                   jax.ShapeDtypeStruct((t, D_DN), F32)] + [jax.ShapeDtypeStruct(sh, F32) for sh in par_shapes],
        scratch_shapes=[pltpu.VMEM((H, DH, DH), F32)],
        name=name, compiler_params=_cp("arbitrary", "arbitrary"))(qkv, zab, zg, states, dy, *params)


def _sg_fn(n_chunk):
    def f(z, lng, lnb, w, bsp_t):
        u = jax.nn.gelu(z[:, :D_SG])
        v = jax.nn.gelu(z[:, D_SG:2 * D_SG])
        gate = z[:, 2 * D_SG:]
        xc = v - jnp.mean(v, axis=-1, keepdims=True)
        vn = xc * lax.rsqrt(jnp.mean(xc * xc, axis=-1, keepdims=True) + EPS) * lng + lnb
        r = lax.broadcasted_iota(jnp.int32, (SG_C, SG_C), 0)
        c = lax.broadcasted_iota(jnp.int32, (SG_C, SG_C), 1)
        causal = r >= c
        first_half = c < SG_C // 2
        rr = lax.broadcasted_iota(jnp.int32, (LANES, D_SG), 0)
        cc = lax.broadcasted_iota(jnp.int32, (LANES, D_SG), 1)
        expand = jnp.where((cc >= rr * 64) & (cc < rr * 64 + 64) & (rr < 4), 1.0, 0.0)
        bias = _sel_r(bsp_t, expand)
        wm = [jnp.where(causal, w[h], 0.0) for h in range(4)]
        rows = []
        for ci in range(n_chunk):
            vc = vn[ci * SG_C:(ci + 1) * SG_C]
            pairs = []
            for pr in range(2):
                vp = vc[:, pr * LANES:(pr + 1) * LANES]
                pairs.append(jnp.where(first_half, _mm(wm[2 * pr], vp), _mm(wm[2 * pr + 1], vp)))
            rows.append(jnp.concatenate(pairs, axis=1) + bias)
        sp = jnp.concatenate(rows, axis=0) if n_chunk > 1 else rows[0]
        return u * sp * _silu(gate)

    return f


def _sg_specs():
    full = lambda i: (0, 0)
    full3 = lambda i: (0, 0, 0)
    par = [pl.BlockSpec((1, D_SG), full), pl.BlockSpec((1, D_SG), full), pl.BlockSpec((4, SG_C, SG_C), full3),
           pl.BlockSpec((SG_C, LANES), full)]
    par_shapes = [(1, D_SG), (1, D_SG), (4, SG_C, SG_C), (SG_C, LANES)]
    return par, par_shapes


def _sg_fwd(zsg, params, tb, name):
    t = zsg.shape[0]
    f = _sg_fn(tb // SG_C)
    par, _ = _sg_specs()

    def body(z_ref, g_ref, b_ref, w_ref, bs_ref, y_ref):
        y_ref[...] = f(z_ref[...], g_ref[...], b_ref[...], w_ref[...], bs_ref[...])

    row = lambda i: (i, 0)
    return pl.pallas_call(
        body, grid=(t // tb,), in_specs=[pl.BlockSpec((tb, 3 * D_SG), row)] + par,
        out_specs=pl.BlockSpec((tb, D_SG), row), out_shape=jax.ShapeDtypeStruct((t, D_SG), F32),
        name=name, compiler_params=_cp("parallel"))(zsg, *params)


def _sg_bwd(zsg, dy, params, tb, name):
    t = zsg.shape[0]
    f = _sg_fn(tb // SG_C)
    par, par_shapes = _sg_specs()

    def body(z_ref, dy_ref, g_ref, b_ref, w_ref, bs_ref, dz_ref, dg_ref, db_ref, dw_ref, dbs_ref):
        accs = (dg_ref, db_ref, dw_ref, dbs_ref)

        @pl.when(pl.program_id(0) == 0)
        def _():
            for r in accs:
                r[...] = jnp.zeros_like(r)

        _, vjp = jax.vjp(f, z_ref[...], g_ref[...], b_ref[...], w_ref[...], bs_ref[...])
        cts = vjp(dy_ref[...])
        dz_ref[...] = cts[0]
        for r, ct in zip(accs, cts[1:]):
            r[...] += ct

    row = lambda i: (i, 0)
    return pl.pallas_call(
        body, grid=(t // tb,), in_specs=[pl.BlockSpec((tb, 3 * D_SG), row), pl.BlockSpec((tb, D_SG), row)] + par,
        out_specs=[pl.BlockSpec((tb, 3 * D_SG), row)] + par,
        out_shape=[jax.ShapeDtypeStruct((t, 3 * D_SG), F32)] + [jax.ShapeDtypeStruct(sh, F32) for sh in par_shapes],
        name=name, compiler_params=_cp("arbitrary"))(zsg, dy, *params)


def _out_fwd(x, ys, p, w_out, pg, w_gate, w_ple, tb, name):
    t = x.shape[0]

    def body(x_ref, y0, y1, y2, p_ref, wo_ref, pg_ref, wg_ref, wp_ref, o_ref):
        y = jnp.concatenate([y0[...], y1[...], y2[...]], axis=1).astype(BF16)
        x1 = x_ref[...] + jnp.dot(y, wo_ref[...], preferred_element_type=F32)
        hn = _rms(x1, pg_ref[...]).astype(BF16)
        gate = jax.nn.sigmoid(jnp.dot(hn, wg_ref[...], preferred_element_type=F32))
        pp = jnp.dot(p_ref[...].astype(BF16), wp_ref[...], preferred_element_type=F32)
        o_ref[...] = x1 + gate * pp

    row = lambda i: (i, 0)
    full = lambda i: (0, 0)
    return pl.pallas_call(
        body, grid=(t // tb,),
        in_specs=[pl.BlockSpec((tb, D), row), pl.BlockSpec((tb, D_SSM), row), pl.BlockSpec((tb, D_DN), row),
                  pl.BlockSpec((tb, D_SG), row), pl.BlockSpec((tb, D_PLE), row), pl.BlockSpec((D, D), full),
                  pl.BlockSpec((1, D), full), pl.BlockSpec((D, D), full), pl.BlockSpec((D_PLE, D), full)],
        out_specs=pl.BlockSpec((tb, D), row), out_shape=jax.ShapeDtypeStruct((t, D), F32),
        name=name, compiler_params=_cp("parallel"))(x, *ys, p, w_out, pg, w_gate, w_ple)


def _out_bwd(x, ys, p, dx2, w_out, pg, w_gate, w_ple, tb, name):
    t = x.shape[0]

    def body(x_ref, y0, y1, y2, p_ref, d_ref, wo_ref, pg_ref, wg_ref, wp_ref,
             dx_ref, dy0, dy1, dy2, dwo_ref, dpg_ref, dwg_ref, dwp_ref):
        accs = (dwo_ref, dpg_ref, dwg_ref, dwp_ref)

        @pl.when(pl.program_id(0) == 0)
        def _():
            for r in accs:
                r[...] = jnp.zeros_like(r)

        y = jnp.concatenate([y0[...], y1[...], y2[...]], axis=1).astype(BF16)
        x1 = x_ref[...] + jnp.dot(y, wo_ref[...], preferred_element_type=F32)
        hn, rms_vjp = jax.vjp(_rms, x1, pg_ref[...])
        hb = hn.astype(BF16)
        gate = jax.nn.sigmoid(jnp.dot(hb, wg_ref[...], preferred_element_type=F32))
        pb = p_ref[...].astype(BF16)
        pp = jnp.dot(pb, wp_ref[...], preferred_element_type=F32)
        d2 = d_ref[...]
        dpp = (d2 * gate).astype(BF16)
        dlog = (d2 * pp * gate * (1.0 - gate)).astype(BF16)
        dwp_ref[...] += _dg(pb, dpp, 0, 0)
        dwg_ref[...] += _dg(hb, dlog, 0, 0)
        dx1_n, dpg = rms_vjp(_dg(dlog, wg_ref[...], 1, 1))
        dpg_ref[...] += dpg
        dx1 = d2 + dx1_n
        dx_ref[...] = dx1
        db = dx1.astype(BF16)
        dwo_ref[...] += _dg(y, db, 0, 0)
        dy = _dg(db, wo_ref[...], 1, 1)
        dy0[...] = dy[:, :D_SSM]
        dy1[...] = dy[:, D_SSM:D_SSM + D_DN]
        dy2[...] = dy[:, D_SSM + D_DN:]

    row = lambda i: (i, 0)
    full = lambda i: (0, 0)
    acts = [pl.BlockSpec((tb, D), row), pl.BlockSpec((tb, D_SSM), row), pl.BlockSpec((tb, D_DN), row), pl.BlockSpec((tb, D_SG), row)]
    wts = [pl.BlockSpec((D, D), full), pl.BlockSpec((1, D), full), pl.BlockSpec((D, D), full), pl.BlockSpec((D_PLE, D), full)]
    return pl.pallas_call(
        body, grid=(t // tb,),
        in_specs=acts + [pl.BlockSpec((tb, D_PLE), row), pl.BlockSpec((tb, D), row)] + wts,
        out_specs=acts + wts,
        out_shape=[jax.ShapeDtypeStruct((t, n), F32) for n in (D, D_SSM, D_DN, D_SG)]
        + [jax.ShapeDtypeStruct(sh, F32) for sh in ((D, D), (1, D), (D, D), (D_PLE, D))],
        name=name, compiler_params=_cp("arbitrary"))(x, *ys, p, dx2, w_out, pg, w_gate, w_ple)


def _loss_head(x, fg, target, tb, name):
    t = x.shape[0]

    def body(x_ref, g_ref, t_ref, dx_ref, dg_ref, loss_ref):
        @pl.when(pl.program_id(0) == 0)
        def _():
            dg_ref[...] = jnp.zeros_like(dg_ref)
            loss_ref[...] = jnp.zeros_like(loss_ref)

        y, vjp = jax.vjp(_rms, x_ref[...], g_ref[...])
        err = y - t_ref[...]
        loss_ref[...] += jnp.zeros_like(loss_ref) + 0.5 * jnp.sum(err * err) / D
        dx, dg = vjp(err / D)
        dx_ref[...] = dx
        dg_ref[...] += dg

    row = lambda i: (i, 0)
    full = lambda i: (0, 0)
    return pl.pallas_call(
        body, grid=(t // tb,),
        in_specs=[pl.BlockSpec((tb, D), row), pl.BlockSpec((1, D), full), pl.BlockSpec((tb, D), row)],
        out_specs=[pl.BlockSpec((tb, D), row), pl.BlockSpec((1, D), full), pl.BlockSpec((1, LANES), full)],
        out_shape=[jax.ShapeDtypeStruct((t, D), F32), jax.ShapeDtypeStruct((1, D), F32), jax.ShapeDtypeStruct((1, LANES), F32)],
        name=name, compiler_params=_cp("arbitrary"))(x, fg, target)


def _all_gather(block, name):
    rows = block.shape[0]

    def body(x_ref, out_ref, send_sems, recv_sems, local_sem):
        x, y, c = lax.axis_index("x"), lax.axis_index("y"), lax.axis_index("c")
        me, sibling = (x, y, c), (x, y, 1 - c)
        chips = [(1 - x, y), (x, 1 - y), (1 - x, 1 - y)]

        def slot(px, py, pc):
            return out_ref.at[4 * px + 2 * py + pc]

        def copy(k, blk, to, src=None):
            return pltpu.make_async_remote_copy(
                src_ref=slot(*blk) if src is None else src, dst_ref=slot(*blk),
                send_sem=send_sems.at[k], recv_sem=recv_sems.at[k], device_id=to, device_id_type=pl.DeviceIdType.MESH)

        mine = pltpu.make_async_copy(x_ref, slot(*me), local_sem)
        mine.start()
        first = [copy(0, me, sibling, src=x_ref)]
        first += [copy(1 + j, me, (*chip, c), src=x_ref) for j, chip in enumerate(chips)]
        for cp in first:
            cp.start()
        passed = [copy(4 + j, (*chip, c), sibling) for j, chip in enumerate(chips)]
        for j, chip in enumerate(chips):
            copy(1 + j, (*chip, c), me).wait_recv()
            passed[j].start()
        copy(0, sibling, me).wait_recv()
        for j, chip in enumerate(chips):
            copy(4 + j, (*chip, 1 - c), me).wait_recv()
        for cp in first + passed:
            cp.wait_send()
        mine.wait()

    return pl.pallas_call(
        body, out_shape=jax.ShapeDtypeStruct((N_DEV, rows, LANES), block.dtype),
        in_specs=[pl.BlockSpec(memory_space=pl.ANY)], out_specs=pl.BlockSpec(memory_space=pl.ANY),
        scratch_shapes=[pltpu.SemaphoreType.DMA((7,)), pltpu.SemaphoreType.DMA((7,)), pltpu.SemaphoreType.DMA],
        name=name)(block)


def _all_to_all(blocks, name):
    rows = blocks.shape[1]

    def body(src_ref, dst_ref, send_sems, recv_sems, local_sem):
        x, y, c = lax.axis_index("x"), lax.axis_index("y"), lax.axis_index("c")
        me = 4 * x + 2 * y + c
        mine = pltpu.make_async_copy(src_ref.at[me], dst_ref.at[me], local_sem)
        mine.start()
        copies = []
        for j in range(1, N_DEV):
            px = 1 - x if j & 4 else x
            py = 1 - y if j & 2 else y
            pc = 1 - c if j & 1 else c
            cp = pltpu.make_async_remote_copy(
                src_ref=src_ref.at[4 * px + 2 * py + pc], dst_ref=dst_ref.at[me],
                send_sem=send_sems.at[j - 1], recv_sem=recv_sems.at[j - 1],
                device_id=(px, py, pc), device_id_type=pl.DeviceIdType.MESH)
            cp.start()
            copies.append(cp)
        for cp in copies:
            cp.wait_send()
            cp.wait_recv()
        mine.wait()

    return pl.pallas_call(
        body, out_shape=jax.ShapeDtypeStruct((N_DEV, rows, LANES), blocks.dtype),
        in_specs=[pl.BlockSpec(memory_space=pl.ANY)], out_specs=pl.BlockSpec(memory_space=pl.ANY),
        scratch_shapes=[pltpu.SemaphoreType.DMA((7,)), pltpu.SemaphoreType.DMA((7,)), pltpu.SemaphoreType.DMA],
        name=name)(blocks)


def _sum_adamw(g8, w, m, v, name):
    rows = w.shape[0]
    rb = PACK_ROWS

    def body(g_ref, w_ref, m_ref, v_ref, go_ref, d_ref, mo_ref, vo_ref):
        g = g_ref[0]
        for k in range(1, N_DEV):
            g = g + g_ref[k]
        mn = ADAM_B1 * m_ref[...] + (1.0 - ADAM_B1) * g
        vn = ADAM_B2 * v_ref[...] + (1.0 - ADAM_B2) * jnp.square(g)
        m_hat = mn / (1.0 - ADAM_B1 ** ADAM_STEP)
        v_hat = vn / (1.0 - ADAM_B2 ** ADAM_STEP)
        go_ref[...] = g
        d_ref[...] = -ADAM_LR * (m_hat / (jnp.sqrt(v_hat) + ADAM_EPS) + ADAM_WD * w_ref[...])
        mo_ref[...] = mn
        vo_ref[...] = vn

    row = lambda i: (i, 0)
    return pl.pallas_call(
        body, grid=(rows // rb,),
        in_specs=[pl.BlockSpec((N_DEV, rb, LANES), lambda i: (0, i, 0))] + [pl.BlockSpec((rb, LANES), row)] * 3,
        out_specs=[pl.BlockSpec((rb, LANES), row)] * 4,
        out_shape=[jax.ShapeDtypeStruct((rows, LANES), F32)] * 4,
        name=name, compiler_params=_cp("parallel"))(g8, w, m, v)


def _pack(arrs):
    flat = jnp.concatenate([a.reshape(-1).astype(F32) for a in arrs])
    n = flat.shape[0]
    gran = PACK_ROWS * LANES
    total = -(-n // gran) * gran
    return jnp.pad(flat, (0, total - n)).reshape(total // LANES, LANES)


def _unpack(pack, shapes):
    flat = pack.reshape(-1)
    out, off = [], 0
    for sh in shapes:
        n = 1
        for d in sh:
            n *= d
        out.append(flat[off:off + n].reshape(sh))
        off += n
    return out


def _to_dest_blocks(full, axis):
    sh = list(full.shape)
    sh[axis:axis + 1] = [N_DEV, sh[axis] // N_DEV]
    return jnp.moveaxis(full.reshape(sh), axis, 0)


def _from_gathered(g, axis):
    m = jnp.moveaxis(g, 0, axis)
    sh = list(m.shape)
    sh[axis:axis + 2] = [sh[axis] * sh[axis + 1]]
    return m.reshape(sh)


def _reorder_w_in(w):
    return jnp.concatenate([w[:, :2048], w[:, 2056:3336], w[:, 2048:2056], jnp.zeros((D, ZW - 3336), w.dtype)], axis=1)


def _restore_dw_in(dw):
    return jnp.concatenate([dw[:, :2048], dw[:, 3328:3336], dw[:, 2048:3328]], axis=1)


def _local_step(x, p, wts, target):
    bl, s, _ = x.shape
    t = bl * s
    depth = p.shape[0]
    tb = 256
    sg_tb = 512 if t % 512 == 0 else SG_C
    prep_tb = 256
    xs = [x.reshape(t, D)]
    saved = []
    for i in range(depth):
        li = f"l{i}"
        ng = wts['norm_g'][i].reshape(1, D)
        w_in = _reorder_w_in(wts['w_in'][i]).astype(BF16)
        s5_par_in = (wts['ssm_a_re'][i], wts['ssm_a_im'][i], wts['ssm_b_re'][i], wts['ssm_b_im'][i],
                     wts['ssm_c_re'][i], wts['ssm_c_im'][i], wts['ssm_d'][i], wts['ssm_log_step'][i])
        tabs, tab_vjp = jax.vjp(_s5_tables, *s5_par_in)
        s5_par = (*tabs, wts['ssm_w_glu'][i], wts['ssm_b_glu'][i].reshape(1, D_SSM))
        conv8 = jnp.pad(wts['dn_conv_w'][i], ((0, 4), (0, 0)))
        dn_par = (jnp.repeat(wts['dn_a_log'][i], DH).reshape(1, D_DN), jnp.repeat(wts['dn_dt_bias'][i], DH).reshape(1, D_DN),
                  wts['dn_norm_g'][i].reshape(1, DH))
        sg_par = (wts['sg_ln_g'][i].reshape(1, D_SG), wts['sg_ln_b'][i].reshape(1, D_SG), wts['sg_w'][i],
                  jnp.pad(jnp.transpose(wts['sg_b'][i]), ((0, 0), (0, LANES - 4))))
        out_par = (wts['w_out'][i].astype(BF16), wts['ple_norm_g'][i].reshape(1, D), wts['w_ple_gate'][i].astype(BF16),
                   wts['w_ple'][i].astype(BF16))
        pi = p[i].reshape(t, D_PLE)

        z_ssm, z_qkv, z_gdn, z_sg, z_ab = _in_proj_fwd(xs[i], ng, w_in, tb, f"in_proj_fwd_{li}")
        y_ssm, carries = _s5_fwd(z_ssm, s5_par, bl, s, f"s5_fwd_{li}")
        qkvn = _dn_prep_fwd(z_qkv, conv8, bl, s, prep_tb, f"dn_prep_fwd_{li}")
        y_dn, states = _dn_scan_fwd(qkvn, z_ab, z_gdn, dn_par, bl, s, f"dn_scan_fwd_{li}")
        y_sg = _sg_fwd(z_sg, sg_par, sg_tb, f"sg_fwd_{li}")
        ys = (y_ssm, y_dn, y_sg)
        xs.append(_out_fwd(xs[i], ys, pi, *out_par, tb, f"out_fwd_{li}"))
        saved.append(dict(ng=ng, w_in=w_in, tab_vjp=tab_vjp, s5_par=s5_par, conv8=conv8, dn_par=dn_par, sg_par=sg_par,
                          out_par=out_par, pi=pi, z=(z_ssm, z_qkv, z_gdn, z_sg, z_ab), carries=carries, qkvn=qkvn,
                          states=states, ys=ys))

    dx, dfg, loss_vec = _loss_head(xs[depth], wts['final_norm_g'].reshape(1, D), target.reshape(t, D), tb, "loss_head")
    grads = {n: [None] * depth for n in WEIGHTS if n != 'final_norm_g'}
    grads['final_norm_g'] = dfg.reshape(D)
    for i in reversed(range(depth)):
        li = f"l{i}"
        sv = saved[i]
        z_ssm, z_qkv, z_gdn, z_sg, z_ab = sv['z']
        dx_res, dy_ssm, dy_dn, dy_sg, dwo, dpg, dwg, dwp = _out_bwd(xs[i], sv['ys'], sv['pi'], dx, *sv['out_par'], tb, f"out_bwd_{li}")
        dz_sg, dlng, dlnb, dsgw, dbsp = _sg_bwd(z_sg, dy_sg, sv['sg_par'], sg_tb, f"sg_bwd_{li}")
        dqkvn, dz_ab, dz_gdn, dal, ddt, dng = _dn_scan_bwd(sv['qkvn'], z_ab, z_gdn, sv['states'], dy_dn, sv['dn_par'], bl, s,
                                                          f"dn_scan_bwd_{li}")
        dz_qkv, dconv = _dn_prep_bwd(z_qkv, dqkvn, sv['conv8'], bl, s, prep_tb, f"dn_prep_bwd_{li}")
        dz_ssm, dbb, dcb, dld, dlp, ddv, dwglu, dbglu = _s5_bwd(z_ssm, sv['carries'], dy_ssm, sv['s5_par'], bl, s, f"s5_bwd_{li}")
        dzs = (dz_ssm, dz_qkv, dz_gdn, dz_sg, dz_ab)
        dx, dnorm = _in_proj_bwd_dx(xs[i], sv['ng'], sv['w_in'], dzs, dx_res, tb, f"in_proj_bwd_dx_{li}")
        dws = _in_proj_bwd_dw(xs[i], sv['ng'], dzs, tb, f"in_proj_bwd_dw_{li}")
        ds5 = sv['tab_vjp']((dbb, dcb, dld, dlp, ddv))
        for n, gval in zip(('ssm_a_re', 'ssm_a_im', 'ssm_b_re', 'ssm_b_im', 'ssm_c_re', 'ssm_c_im', 'ssm_d', 'ssm_log_step'), ds5):
            grads[n][i] = gval
        grads['norm_g'][i] = dnorm.reshape(D)
        grads['w_in'][i] = _restore_dw_in(jnp.concatenate(dws, axis=1))
        grads['ssm_w_glu'][i] = dwglu
        grads['ssm_b_glu'][i] = dbglu.reshape(D_SSM)
        grads['dn_conv_w'][i] = dconv[:4]
        grads['dn_a_log'][i] = dal.reshape(H, DH).sum(axis=1)
        grads['dn_dt_bias'][i] = ddt.reshape(H, DH).sum(axis=1)
        grads['dn_norm_g'][i] = dng.reshape(DH)
        grads['sg_ln_g'][i] = dlng.reshape(D_SG)
        grads['sg_ln_b'][i] = dlnb.reshape(D_SG)
        grads['sg_w'][i] = dsgw
        grads['sg_b'][i] = jnp.transpose(dbsp[:, :4])
        grads['w_out'][i] = dwo
        grads['ple_norm_g'][i] = dpg.reshape(D)
        grads['w_ple_gate'][i] = dwg
        grads['w_ple'][i] = dwp
    grads = {n: (g if n == 'final_norm_g' else jnp.stack(g)) for n, g in grads.items()}
    return loss_vec[0, 0], dx.reshape(bl, s, D), grads


def kernel(x, p, norm_g, w_in, ssm_a_re, ssm_a_im, ssm_b_re, ssm_b_im, ssm_c_re, ssm_c_im, ssm_d, ssm_log_step, ssm_w_glu, ssm_b_glu, dn_conv_w, dn_a_log, dn_dt_bias, dn_norm_g, sg_ln_g, sg_ln_b, sg_w, sg_b, w_out, ple_norm_g, w_ple_gate, w_ple, final_norm_g, loss_target, m_norm_g, m_w_in, m_ssm_a_re, m_ssm_a_im, m_ssm_b_re, m_ssm_b_im, m_ssm_c_re, m_ssm_c_im, m_ssm_d, m_ssm_log_step, m_ssm_w_glu, m_ssm_b_glu, m_dn_conv_w, m_dn_a_log, m_dn_dt_bias, m_dn_norm_g, m_sg_ln_g, m_sg_ln_b, m_sg_w, m_sg_b, m_w_out, m_ple_norm_g, m_w_ple_gate, m_w_ple, m_final_norm_g, v_norm_g, v_w_in, v_ssm_a_re, v_ssm_a_im, v_ssm_b_re, v_ssm_b_im, v_ssm_c_re, v_ssm_c_im, v_ssm_d, v_ssm_log_step, v_ssm_w_glu, v_ssm_b_glu, v_dn_conv_w, v_dn_a_log, v_dn_dt_bias, v_dn_norm_g, v_sg_ln_g, v_sg_ln_b, v_sg_w, v_sg_b, v_w_out, v_ple_norm_g, v_w_ple_gate, v_w_ple, v_final_norm_g):
    w_loc = dict(zip(WEIGHTS, (norm_g, w_in, ssm_a_re, ssm_a_im, ssm_b_re, ssm_b_im, ssm_c_re, ssm_c_im, ssm_d, ssm_log_step,
                               ssm_w_glu, ssm_b_glu, dn_conv_w, dn_a_log, dn_dt_bias, dn_norm_g, sg_ln_g, sg_ln_b, sg_w, sg_b,
                               w_out, ple_norm_g, w_ple_gate, w_ple, final_norm_g)))
    m_loc = dict(zip(WEIGHTS, (m_norm_g, m_w_in, m_ssm_a_re, m_ssm_a_im, m_ssm_b_re, m_ssm_b_im, m_ssm_c_re, m_ssm_c_im, m_ssm_d,
                               m_ssm_log_step, m_ssm_w_glu, m_ssm_b_glu, m_dn_conv_w, m_dn_a_log, m_dn_dt_bias, m_dn_norm_g,
                               m_sg_ln_g, m_sg_ln_b, m_sg_w, m_sg_b, m_w_out, m_ple_norm_g, m_w_ple_gate, m_w_ple, m_final_norm_g)))
    v_loc = dict(zip(WEIGHTS, (v_norm_g, v_w_in, v_ssm_a_re, v_ssm_a_im, v_ssm_b_re, v_ssm_b_im, v_ssm_c_re, v_ssm_c_im, v_ssm_d,
                               v_ssm_log_step, v_ssm_w_glu, v_ssm_b_glu, v_dn_conv_w, v_dn_a_log, v_dn_dt_bias, v_dn_norm_g,
                               v_sg_ln_g, v_sg_ln_b, v_sg_w, v_sg_b, v_w_out, v_ple_norm_g, v_w_ple_gate, v_w_ple, v_final_norm_g)))

    shard_shapes = [w_loc[n].shape for n in SHARDED_ORDER]
    gathered = _all_gather(_pack([w_loc[n] for n in SHARDED_ORDER]), "gather_weights")
    per_dev = [_unpack(gathered[k], shard_shapes) for k in range(N_DEV)]
    full = dict(w_loc)
    for j, n in enumerate(SHARDED_ORDER):
        full[n] = _from_gathered(jnp.stack([per_dev[k][j] for k in range(N_DEV)]), SHARDED[n])

    loss_part, grad_x, grads = _local_step(x, p, full, loss_target)

    dest = [_to_dest_blocks(grads[n], SHARDED[n]) for n in SHARDED_ORDER]
    sh_pack8 = jnp.stack([_pack([d[k] for d in dest]) for k in range(N_DEV)])
    sh_recv = _all_to_all(sh_pack8, "scatter_grads")
    rep_pack = _pack([grads[n] for n in REPLICATED_ORDER] + [loss_part.reshape(1)])
    rep_recv = _all_gather(rep_pack, "gather_small_grads")

    sh_out = _sum_adamw(sh_recv, _pack([w_loc[n] for n in SHARDED_ORDER]), _pack([m_loc[n] for n in SHARDED_ORDER]),
                        _pack([v_loc[n] for n in SHARDED_ORDER]), "adamw_sharded")
    one = jnp.zeros((1,), F32)
    rep_out = _sum_adamw(rep_recv, _pack([w_loc[n] for n in REPLICATED_ORDER] + [one]),
                         _pack([m_loc[n] for n in REPLICATED_ORDER] + [one]),
                         _pack([v_loc[n] for n in REPLICATED_ORDER] + [one]), "adamw_replicated")
    rep_shapes = [w_loc[n].shape for n in REPLICATED_ORDER] + [(1,)]
    outs = {}
    for kind, sh_p, rep_p in zip(('g', 'd', 'm', 'v'), sh_out, rep_out):
        vals = dict(zip(SHARDED_ORDER, _unpack(sh_p, shard_shapes)))
        vals.update(zip(REPLICATED_ORDER + ['loss'], _unpack(rep_p, rep_shapes)))
        outs[kind] = vals
    loss = outs['g']['loss'].reshape(())
    return (loss, grad_x, *[outs['g'][n] for n in WEIGHTS], *[outs['d'][n] for n in WEIGHTS],
            *[outs['m'][n] for n in WEIGHTS], *[outs['v'][n] for n in WEIGHTS])
```

```python
import functools

import jax
import jax.numpy as jnp
from jax import lax
from jax.experimental import pallas as pl
from jax.experimental.pallas import tpu as pltpu

F32 = jnp.float32
BF16 = jnp.bfloat16
EPS = 1e-6

D = 1024
D_PLE = 256
D_SSM = 256
D_DN = 512
D_SG = 256
G = 16
CG = 16
NS = 64
NRE = G * NS
H = 4
DH = 128
DN_C = 128
SG_C = 128
ZW = 3456
Z_PIECES = (512, 1536, 512, 768, 128)
N_DEV = 8
LANES = 128
PACK_ROWS = 256
VMEM_LIMIT = 56 * 1024 * 1024
ELEMENTWISE_STEP_BYTES = 4 * 1024 * 1024
TB = 256
SG_TB = 512
PREP_TB = 256

ADAM_LR = 0.001
ADAM_B1 = 0.9
ADAM_B2 = 0.999
ADAM_EPS = 1e-08
ADAM_WD = 0.01
ADAM_STEP = 10

S5_L = 128
S5_GROUP = 8
S5_SHIFTS = (1, 2, 4)

WEIGHTS = ['norm_g', 'w_in', 'ssm_a_re', 'ssm_a_im', 'ssm_b_re', 'ssm_b_im', 'ssm_c_re', 'ssm_c_im', 'ssm_d',
           'ssm_log_step', 'ssm_w_glu', 'ssm_b_glu', 'dn_conv_w', 'dn_a_log', 'dn_dt_bias', 'dn_norm_g', 'sg_ln_g',
           'sg_ln_b', 'sg_w', 'sg_b', 'w_out', 'ple_norm_g', 'w_ple_gate', 'w_ple', 'final_norm_g']
SHARDED = {'w_in': 2, 'ssm_w_glu': 1, 'dn_conv_w': 2, 'w_out': 1, 'w_ple_gate': 1, 'w_ple': 2}
SHARDED_ORDER = ['w_in', 'ssm_w_glu', 'dn_conv_w', 'w_out', 'w_ple_gate', 'w_ple']
WIRE = {'w_in': BF16, 'ssm_w_glu': BF16, 'dn_conv_w': F32, 'w_out': BF16, 'w_ple_gate': BF16, 'w_ple': BF16}
REPLICATED_ORDER = [n for n in WEIGHTS if n not in SHARDED]


def _cp(*sem):
    return pltpu.CompilerParams(dimension_semantics=sem, vmem_limit_bytes=VMEM_LIMIT)


def _dg(a, b, ca, cb, precision=None):
    return lax.dot_general(a, b, (((ca,), (cb,)), ((), ())), precision=precision, preferred_element_type=F32)


@jax.custom_vjp
def _mm(a, b):
    return _dg(a.astype(BF16), b.astype(BF16), 1, 0)


def _mm_fwd(a, b):
    return _mm(a, b), (a, b)


def _mm_bwd(res, g):
    a, b = res
    gb = g.astype(BF16)
    return _dg(gb, b.astype(BF16), 1, 1), _dg(a.astype(BF16), gb, 0, 0)


_mm.defvjp(_mm_fwd, _mm_bwd)


@jax.custom_vjp
def _mm_nt(a, b):
    return _dg(a.astype(BF16), b.astype(BF16), 1, 1)


def _mm_nt_fwd(a, b):
    return _mm_nt(a, b), (a, b)


def _mm_nt_bwd(res, g):
    a, b = res
    gb = g.astype(BF16)
    return _dg(gb, b.astype(BF16), 1, 0), _dg(gb, a.astype(BF16), 0, 0)


_mm_nt.defvjp(_mm_nt_fwd, _mm_nt_bwd)


@jax.custom_vjp
def _mm_tn(a, b):
    return _dg(a.astype(BF16), b.astype(BF16), 0, 0)


def _mm_tn_fwd(a, b):
    return _mm_tn(a, b), (a, b)


def _mm_tn_bwd(res, g):
    a, b = res
    gb = g.astype(BF16)
    return _dg(b.astype(BF16), gb, 1, 1), _dg(a.astype(BF16), gb, 1, 0)


_mm_tn.defvjp(_mm_tn_fwd, _mm_tn_bwd)


def _split(x, n):
    pieces = []
    for _ in range(n - 1):
        hi = x.astype(BF16)
        pieces.append(hi)
        x = x - hi.astype(F32)
    pieces.append(x.astype(BF16))
    return pieces


def _dg3(a, b, ca, cb):
    a_hi, a_lo = _split(a, 2)
    b_hi, b_lo = _split(b, 2)
    return _dg(a_hi, b_hi, ca, cb) + (_dg(a_hi, b_lo, ca, cb) + _dg(a_lo, b_hi, ca, cb))


@jax.custom_vjp
def _dot3(a, b):
    return _dg3(a, b, 1, 0)


def _dot3_fwd(a, b):
    return _dot3(a, b), (a, b)


def _dot3_bwd(res, g):
    a, b = res
    return _dg3(g, b, 1, 1), _dg3(a, g, 0, 0)


_dot3.defvjp(_dot3_fwd, _dot3_bwd)


def _dg_sel(x, e, cx, ce, x_first):
    eb = e.astype(BF16)
    out = None
    for piece in reversed(_split(x, 3)):
        term = _dg(piece, eb, cx, ce) if x_first else _dg(eb, piece, ce, cx)
        out = term if out is None else out + term
    return out


@jax.custom_vjp
def _sel_r(x, e):
    return _dg_sel(x, e, 1, 0, True)


def _sel_r_fwd(x, e):
    return _sel_r(x, e), e


def _sel_r_bwd(e, g):
    return _dg_sel(g, e, 1, 1, True), jnp.zeros_like(e)


_sel_r.defvjp(_sel_r_fwd, _sel_r_bwd)


@jax.custom_vjp
def _sel_l(e, x):
    return _dg_sel(x, e, 0, 1, False)


def _sel_l_fwd(e, x):
    return _sel_l(e, x), e


def _sel_l_bwd(e, g):
    return jnp.zeros_like(e), _dg_sel(g, e, 0, 0, False)


_sel_l.defvjp(_sel_l_fwd, _sel_l_bwd)


def _rms(x, g):
    return x * lax.rsqrt(jnp.mean(x * x, axis=-1, keepdims=True) + EPS) * g


def _silu(x):
    return x * jax.nn.sigmoid(x)


def _in_proj_fwd(x, g, w, tb, name):
    t = x.shape[0]

    def body(x_ref, g_ref, w_ref, *z_refs):
        h = _rms(x_ref[...], g_ref[...])
        z = jnp.dot(h.astype(BF16), w_ref[...], preferred_element_type=F32)
        off = 0
        for z_ref, n in zip(z_refs, Z_PIECES):
            z_ref[...] = z[:, off:off + n]
            off += n

    row = lambda i: (i, 0)
    full = lambda i: (0, 0)
    return pl.pallas_call(
        body, grid=(t // tb,),
        in_specs=[pl.BlockSpec((tb, D), row), pl.BlockSpec((1, D), full), pl.BlockSpec((D, ZW), full)],
        out_specs=[pl.BlockSpec((tb, n), row) for n in Z_PIECES],
        out_shape=[jax.ShapeDtypeStruct((t, n), F32) for n in Z_PIECES],
        name=name, compiler_params=_cp("parallel"))(x, g, w)


def _in_proj_bwd_dx(x, g, w, dzs, dx_res, tb, name):
    t = x.shape[0]

    def body(x_ref, g_ref, w_ref, d0, d1, d2, d3, d4, dxr_ref, dx_ref, dg_ref):
        @pl.when(pl.program_id(0) == 0)
        def _():
            dg_ref[...] = jnp.zeros_like(dg_ref)

        dz = jnp.concatenate([d0[...], d1[...], d2[...], d3[...], d4[...]], axis=1).astype(BF16)
        dh = _dg(dz, w_ref[...], 1, 1)
        _, vjp = jax.vjp(_rms, x_ref[...], g_ref[...])
        dx, dg = vjp(dh)
        dx_ref[...] = dx + dxr_ref[...]
        dg_ref[...] += dg

    row = lambda i: (i, 0)
    full = lambda i: (0, 0)
    return pl.pallas_call(
        body, grid=(t // tb,),
        in_specs=[pl.BlockSpec((tb, D), row), pl.BlockSpec((1, D), full), pl.BlockSpec((D, ZW), full)]
        + [pl.BlockSpec((tb, n), row) for n in Z_PIECES] + [pl.BlockSpec((tb, D), row)],
        out_specs=[pl.BlockSpec((tb, D), row), pl.BlockSpec((1, D), full)],
        out_shape=[jax.ShapeDtypeStruct((t, D), F32), jax.ShapeDtypeStruct((1, D), F32)],
        name=name, compiler_params=_cp("arbitrary"))(x, g, w, *dzs, dx_res)


def _in_proj_bwd_dw(x, g, dzs, tb, name):
    t = x.shape[0]

    def body(x_ref, g_ref, d0, d1, d2, d3, d4, *dw_refs):
        @pl.when(pl.program_id(0) == 0)
        def _():
            for r in dw_refs:
                r[...] = jnp.zeros_like(r)

        h = _rms(x_ref[...], g_ref[...]).astype(BF16)
        for d_ref, dw_ref in zip((d0, d1, d2, d3, d4), dw_refs):
            dw_ref[...] += _dg(h, d_ref[...].astype(BF16), 0, 0)

    row = lambda i: (i, 0)
    full = lambda i: (0, 0)
    return pl.pallas_call(
        body, grid=(t // tb,),
        in_specs=[pl.BlockSpec((tb, D), row), pl.BlockSpec((1, D), full)] + [pl.BlockSpec((tb, n), row) for n in Z_PIECES],
        out_specs=[pl.BlockSpec((D, n), full) for n in Z_PIECES],
        out_shape=[jax.ShapeDtypeStruct((D, n), F32) for n in Z_PIECES],
        name=name, compiler_params=_cp("arbitrary"))(x, g, *dzs)


def _s5_tables(a_re, a_im, b_re, b_im, c_re, c_im, d_skip, log_step):
    step = jnp.exp(log_step)[:, None]

    def lam_pow(k):
        mag = jnp.exp(k * a_re * step)
        ang = k * a_im * step
        return mag * jnp.cos(ang), mag * jnp.sin(ang)

    lam_re, lam_im = lam_pow(1.0)
    den = a_re * a_re + a_im * a_im
    nr, ni = lam_re - 1.0, lam_im
    f_re = (nr * a_re + ni * a_im) / den
    f_im = (ni * a_re - nr * a_im) / den
    bbar_re = f_re[..., None] * b_re - f_im[..., None] * b_im
    bbar_im = f_re[..., None] * b_im + f_im[..., None] * b_re
    eye = jnp.eye(G, dtype=F32)

    def blk_b(bb):
        return (jnp.transpose(bb, (0, 2, 1))[:, :, None, :] * eye[:, None, :, None]).reshape(D_SSM, NRE)

    def blk_c(cc):
        return (jnp.transpose(cc, (0, 2, 1))[:, :, None, :] * eye[:, None, :, None]).reshape(NRE, D_SSM)

    b_blk = jnp.concatenate([blk_b(bbar_re), blk_b(bbar_im)], axis=1)
    c_blk = jnp.concatenate([blk_c(c_re), -blk_c(c_im)], axis=0)
    ks = jnp.asarray(S5_SHIFTS, F32)[:, None, None]
    ld_re, ld_im = lam_pow(ks)
    ld = jnp.concatenate([ld_re.reshape(-1, 1, NRE), ld_im.reshape(-1, 1, NRE)], axis=-1)
    js = jnp.arange(1, S5_GROUP + 1, dtype=F32)[:, None, None]
    lp_re, lp_im = lam_pow(js)
    lp = jnp.concatenate([lp_re.reshape(S5_GROUP, NRE), lp_im.reshape(S5_GROUP, NRE)], axis=-1)
    return b_blk, c_blk, ld, lp, d_skip.reshape(1, D_SSM)


def _make_group_shift(rows):
    @functools.partial(jax.custom_vjp, nondiff_argnums=(1,))
    def shift(x, d):
        r = lax.broadcasted_iota(jnp.int32, x.shape, 0) & (S5_GROUP - 1)
        return jnp.where(r >= d, pltpu.roll(x, d, 0), 0.0)

    def fwd(x, d):
        return shift(x, d), None

    def bwd(d, _, g):
        r = lax.broadcasted_iota(jnp.int32, g.shape, 0) & (S5_GROUP - 1)
        return (jnp.where(r < S5_GROUP - d, pltpu.roll(g, rows - d, 0), 0.0),)

    shift.defvjp(fwd, bwd)
    return shift


def _s5_chunk_fn():
    shift = _make_group_shift(S5_L)

    def f(u, gate, cr, ci, b_blk, c_blk, lds, lp, dv, wglu, bglu):
        bu = _mm(u, b_blk)
        hr, hi = bu[:, :NRE], bu[:, NRE:]
        for ld, d in zip(lds, S5_SHIFTS):
            lr, li = ld[:, :NRE], ld[:, NRE:]
            sr, si = shift(hr, d), shift(hi, d)
            hr, hi = hr + lr * sr - li * si, hi + lr * si + li * sr
        pr, pi = lp[:, :NRE], lp[:, NRE:]
        rows_r, rows_i = [], []
        for r in range(S5_L // S5_GROUP):
            br, bi = hr[r * S5_GROUP:(r + 1) * S5_GROUP], hi[r * S5_GROUP:(r + 1) * S5_GROUP]
            br, bi = br + pr * cr - pi * ci, bi + pr * ci + pi * cr
            cr, ci = br[S5_GROUP - 1:S5_GROUP], bi[S5_GROUP - 1:S5_GROUP]
            rows_r.append(br)
            rows_i.append(bi)
        ncr, nci = cr, ci
        y = _mm(jnp.concatenate([jnp.concatenate(rows_r, axis=0), jnp.concatenate(rows_i, axis=0)], axis=1), c_blk) + dv * u
        y = jax.nn.gelu(y)
        y = y * jax.nn.sigmoid(_mm(y, wglu) + bglu)
        return y * _silu(gate), ncr, nci

    return f


def _s5_specs(n_s, rev):
    nd = len(S5_SHIFTS)
    if rev:
        row = lambda b, i: (b * n_s + n_s - 1 - i, 0)
        row3 = lambda b, i: (b * n_s + n_s - 1 - i, 0, 0)
    else:
        row = lambda b, i: (b * n_s + i, 0)
        row3 = lambda b, i: (b * n_s + i, 0, 0)
    full = lambda b, i: (0, 0)
    full3 = lambda b, i: (0, 0, 0)
    par = [pl.BlockSpec((D_SSM, 2 * NRE), full), pl.BlockSpec((2 * NRE, D_SSM), full), pl.BlockSpec((nd, 1, 2 * NRE), full3),
           pl.BlockSpec((S5_GROUP, 2 * NRE), full), pl.BlockSpec((1, D_SSM), full), pl.BlockSpec((D_SSM, D_SSM), full),
           pl.BlockSpec((1, D_SSM), full)]
    par_shapes = [(D_SSM, 2 * NRE), (2 * NRE, D_SSM), (nd, 1, 2 * NRE), (S5_GROUP, 2 * NRE), (1, D_SSM), (D_SSM, D_SSM), (1, D_SSM)]
    return row, row3, par, par_shapes


def _s5_fwd(zs, params, bl, s, name):
    n_s = s // S5_L
    nd = len(S5_SHIFTS)
    f = _s5_chunk_fn()
    row, row3, par, _ = _s5_specs(n_s, False)

    def body(z_ref, b_ref, c_ref, ld_ref, lp_ref, dv_ref, wg_ref, bg_ref, y_ref, car_ref, cs):
        @pl.when(pl.program_id(1) == 0)
        def _():
            cs[...] = jnp.zeros_like(cs)

        c = cs[...]
        car_ref[0] = c
        z = z_ref[...]
        out, ncr, nci = f(z[:, :D_SSM], z[:, D_SSM:], c[:, :NRE], c[:, NRE:], b_ref[...], c_ref[...],
                          [ld_ref[k] for k in range(nd)], lp_ref[...], dv_ref[...], wg_ref[...], bg_ref[...])
        y_ref[...] = out
        cs[:, :NRE] = ncr
        cs[:, NRE:] = nci

    t = bl * s
    return pl.pallas_call(
        body, grid=(bl, n_s),
        in_specs=[pl.BlockSpec((S5_L, 2 * D_SSM), row)] + par,
        out_specs=[pl.BlockSpec((S5_L, D_SSM), row), pl.BlockSpec((1, 1, 2 * NRE), row3)],
        out_shape=[jax.ShapeDtypeStruct((t, D_SSM), F32), jax.ShapeDtypeStruct((bl * n_s, 1, 2 * NRE), F32)],
        scratch_shapes=[pltpu.VMEM((1, 2 * NRE), F32)],
        name=name, compiler_params=_cp("arbitrary", "arbitrary"))(zs, *params)


def _s5_bwd(zs, carries, dy, params, bl, s, name):
    n_s = s // S5_L
    nd = len(S5_SHIFTS)
    f = _s5_chunk_fn()
    row, row3, par, par_shapes = _s5_specs(n_s, True)

    def body(z_ref, car_ref, dy_ref, b_ref, c_ref, ld_ref, lp_ref, dv_ref, wg_ref, bg_ref,
             dz_ref, db_ref, dc_ref, dld_ref, dlp_ref, ddv_ref, dwg_ref, dbg_ref, dcs):
        accs = (db_ref, dc_ref, dld_ref, dlp_ref, ddv_ref, dwg_ref, dbg_ref)

        @pl.when((pl.program_id(0) == 0) & (pl.program_id(1) == 0))
        def _():
            for r in accs:
                r[...] = jnp.zeros_like(r)

        @pl.when(pl.program_id(1) == 0)
        def _():
            dcs[...] = jnp.zeros_like(dcs)

        z = z_ref[...]
        c = car_ref[0]
        _, vjp = jax.vjp(f, z[:, :D_SSM], z[:, D_SSM:], c[:, :NRE], c[:, NRE:], b_ref[...], c_ref[...],
                         [ld_ref[k] for k in range(nd)], lp_ref[...], dv_ref[...], wg_ref[...], bg_ref[...])
        dc = dcs[...]
        du, dgate, dcr, dci, dbb, dcb, dlds, dlpb, ddvb, dwgb, dbgb = vjp((dy_ref[...], dc[:, :NRE], dc[:, NRE:]))
        dz_ref[...] = jnp.concatenate([du, dgate], axis=1)
        dcs[:, :NRE] = dcr
        dcs[:, NRE:] = dci
        db_ref[...] += dbb
        dc_ref[...] += dcb
        for k in range(nd):
            dld_ref[k] += dlds[k]
        dlp_ref[...] += dlpb
        ddv_ref[...] += ddvb
        dwg_ref[...] += dwgb
        dbg_ref[...] += dbgb

    t = bl * s
    return pl.pallas_call(
        body, grid=(bl, n_s),
        in_specs=[pl.BlockSpec((S5_L, 2 * D_SSM), row), pl.BlockSpec((1, 1, 2 * NRE), row3), pl.BlockSpec((S5_L, D_SSM), row)] + par,
        out_specs=[pl.BlockSpec((S5_L, 2 * D_SSM), row)] + par,
        out_shape=[jax.ShapeDtypeStruct((t, 2 * D_SSM), F32)] + [jax.ShapeDtypeStruct(sh, F32) for sh in par_shapes],
        scratch_shapes=[pltpu.VMEM((1, 2 * NRE), F32)],
        name=name, compiler_params=_cp("arbitrary", "arbitrary"))(zs, carries, dy, *params)


def _dn_post(c):
    s = _silu(c)
    parts = []
    for j in range(12):
        xj = s[:, j * DH:(j + 1) * DH]
        if j < 8:
            xj = xj * lax.rsqrt(jnp.sum(xj * xj, axis=-1, keepdims=True) + EPS)
        if j < 4:
            xj = xj * (DH ** -0.5)
        parts.append(xj)
    return jnp.concatenate(parts, axis=1)


def _dn_prep_fwd(zq, conv_w8, bl, s, tb, name):
    n_s = s // tb
    hb = tb // 8
    w3 = 3 * D_DN

    def body(cur_ref, prev_ref, w_ref, o_ref):
        i = pl.program_id(1)
        prev = jnp.where(i > 0, prev_ref[...], 0.0)
        ext = jnp.concatenate([prev, cur_ref[...]], axis=0)
        c = jnp.zeros((tb, w3), F32)
        for k in range(4):
            sh = ext if k == 3 else pltpu.roll(ext, 3 - k, 0)
            c = c + w_ref[k:k + 1, :] * sh[8:, :]
        o_ref[...] = _dn_post(c)

    row = lambda b, i: (b * n_s + i, 0)
    prv = lambda b, i: (jnp.maximum((b * n_s + i) * hb - 1, 0), 0)
    full = lambda b, i: (0, 0)
    t = bl * s
    return pl.pallas_call(
        body, grid=(bl, n_s),
        in_specs=[pl.BlockSpec((tb, w3), row), pl.BlockSpec((8, w3), prv), pl.BlockSpec((8, w3), full)],
        out_specs=pl.BlockSpec((tb, w3), row),
        out_shape=jax.ShapeDtypeStruct((t, w3), F32),
        name=name, compiler_params=_cp("parallel", "parallel"))(zq, zq, conv_w8)


def _dn_prep_bwd(zq, dqkv, conv_w8, bl, s, tb, name):
    n_s = s // tb
    hb = tb // 8
    w3 = 3 * D_DN
    n_blk8 = bl * s // 8

    def body(cur_ref, prev_ref, next_ref, d_ref, dnext_ref, w_ref, dz_ref, dw_ref):
        b, i = pl.program_id(0), pl.program_id(1)

        @pl.when((b == 0) & (i == 0))
        def _():
            dw_ref[...] = jnp.zeros_like(dw_ref)

        prev = jnp.where(i > 0, prev_ref[...], 0.0)
        nxt = jnp.where(i < n_s - 1, next_ref[...], 0.0)
        dnxt = jnp.where(i < n_s - 1, dnext_ref[...], 0.0)
        ext = jnp.concatenate([prev, cur_ref[...], nxt], axis=0)
        shifted = [ext if k == 3 else pltpu.roll(ext, 3 - k, 0) for k in range(4)]
        c2 = jnp.zeros((tb + 8, w3), F32)
        for k in range(4):
            c2 = c2 + w_ref[k:k + 1, :] * shifted[k][8:, :]
        dpost = jnp.concatenate([d_ref[...], dnxt], axis=0)
        _, vjp = jax.vjp(_dn_post, c2)
        (dc2,) = vjp(dpost)
        dz = jnp.zeros((tb, w3), F32)
        for k in range(4):
            up = dc2 if k == 3 else pltpu.roll(dc2, tb + 8 - (3 - k), 0)
            dz = dz + w_ref[k:k + 1, :] * up[:tb, :]
            dw_ref[k:k + 1, :] += jnp.sum(dc2[:tb, :] * shifted[k][8:8 + tb, :], axis=0, keepdims=True)
        dz_ref[...] = dz

    row = lambda b, i: (b * n_s + i, 0)
    prv = lambda b, i: (jnp.maximum((b * n_s + i) * hb - 1, 0), 0)
    nxt = lambda b, i: (jnp.minimum((b * n_s + i + 1) * hb, n_blk8 - 1), 0)
    full = lambda b, i: (0, 0)
    t = bl * s
    return pl.pallas_call(
        body, grid=(bl, n_s),
        in_specs=[pl.BlockSpec((tb, w3), row), pl.BlockSpec((8, w3), prv), pl.BlockSpec((8, w3), nxt),
                  pl.BlockSpec((tb, w3), row), pl.BlockSpec((8, w3), nxt), pl.BlockSpec((8, w3), full)],
        out_specs=[pl.BlockSpec((tb, w3), row), pl.BlockSpec((8, w3), full)],
        out_shape=[jax.ShapeDtypeStruct((t, w3), F32), jax.ShapeDtypeStruct((8, w3), F32)],
        name=name, compiler_params=_cp("arbitrary", "arbitrary"))(zq, zq, zq, dqkv, dqkv, conv_w8)


def _dn_chunk_fn():
    c_len = DN_C
    n_sq = c_len.bit_length() - 1

    def f(qkv, zab, zg, s0, s1, s2, s3, alog_e, dt_e, ng):
        states = (s0, s1, s2, s3)
        r = lax.broadcasted_iota(jnp.int32, (c_len, c_len), 0)
        c = lax.broadcasted_iota(jnp.int32, (c_len, c_len), 1)
        causal, strict, eye = r >= c, r > c, r == c
        tril = jnp.where(causal, 1.0, 0.0)
        ident = jnp.where(eye, 1.0, 0.0)
        rr = lax.broadcasted_iota(jnp.int32, (LANES, D_DN), 0)
        cc = lax.broadcasted_iota(jnp.int32, (LANES, D_DN), 1)
        e_a = jnp.where((cc >= rr * DH) & (cc < rr * DH + DH) & (rr < H), 1.0, 0.0)
        e_b = jnp.where((cc >= (rr - H) * DH) & (cc < (rr - H) * DH + DH) & (rr >= H) & (rr < 2 * H), 1.0, 0.0)
        a_e = _sel_r(zab, e_a)
        b_e = _sel_r(zab, e_b)
        beta = jax.nn.sigmoid(b_e)
        g = -jnp.exp(alog_e) * jax.nn.softplus(a_e + dt_e)
        gc = _sel_l(tril, g)
        glast = jnp.sum(g, axis=0, keepdims=True)
        eg = jnp.exp(gc)
        ekd = jnp.exp(glast - gc)
        dl = jnp.exp(glast)
        heads = range(H)
        sls = [slice(h * DH, (h + 1) * DH) for h in heads]
        qs = [qkv[:, h * DH:(h + 1) * DH] for h in heads]
        ks = [qkv[:, D_DN + h * DH:D_DN + (h + 1) * DH] for h in heads]
        vs = [qkv[:, 2 * D_DN + h * DH:2 * D_DN + (h + 1) * DH] for h in heads]
        ccols = [gc[:, sl] for sl in sls]
        decs = [jnp.where(causal, jnp.exp(jnp.where(causal, cl - jnp.transpose(cl), 0.0)), 0.0) for cl in ccols]
        kbs = [k * beta[:, sl] for k, sl in zip(ks, sls)]
        ms = [jnp.where(strict, _mm_nt(kb, k) * dec, 0.0) for kb, k, dec in zip(kbs, ks, decs)]
        ps = ms
        tinvs = [ident - m for m in ms]
        for _ in range(n_sq - 1):
            ps = [_dot3(p, p) for p in ps]
            tinvs = [t + _dot3(t, p) for t, p in zip(tinvs, ps)]
        sols = [_dot3(t, jnp.concatenate([v * beta[:, sl], kb * eg[:, sl]], axis=1))
                for t, v, kb, sl in zip(tinvs, vs, kbs, sls)]
        atts = [_mm_nt(q, k) * dec for q, k, dec in zip(qs, ks, decs)]
        vnews = [sol[:, :DH] - _mm(sol[:, DH:], st) for sol, st in zip(sols, states)]
        os_ = [_mm(q * eg[:, sl], st) + _mm(att, vn) for q, sl, st, att, vn in zip(qs, sls, states, atts, vnews)]
        new_states = [st * dl[:, sl] + _mm_tn(k * ekd[:, sl], vn) for st, sl, k, vn in zip(states, sls, ks, vnews)]
        ys = [_rms(o, ng) * _silu(zg[:, sl]) for o, sl in zip(os_, sls)]
        return (jnp.concatenate(ys, axis=1), *new_states)

    return f


def _dn_scan_specs(n_c, rev):
    if rev:
        row = lambda b, i: (b * n_c + n_c - 1 - i, 0)
        row4 = lambda b, i: (b * n_c + n_c - 1 - i, 0, 0, 0)
    else:
        row = lambda b, i: (b * n_c + i, 0)
        row4 = lambda b, i: (b * n_c + i, 0, 0, 0)
    full = lambda b, i: (0, 0)
    par = [pl.BlockSpec((1, D_DN), full), pl.BlockSpec((1, D_DN), full), pl.BlockSpec((1, DH), full)]
    par_shapes = [(1, D_DN), (1, D_DN), (1, DH)]
    return row, row4, par, par_shapes


def _dn_scan_fwd(qkv, zab, zg, params, bl, s, name):
    n_c = s // DN_C
    f = _dn_chunk_fn()
    row, row4, par, _ = _dn_scan_specs(n_c, False)

    def body(q_ref, ab_ref, zg_ref, al_ref, dt_ref, ng_ref, y_ref, st_ref, ssc):
        @pl.when(pl.program_id(1) == 0)
        def _():
            ssc[...] = jnp.zeros_like(ssc)

        sts = [ssc[h] for h in range(H)]
        for h in range(H):
            st_ref[0, h] = sts[h]
        outs = f(q_ref[...], ab_ref[...], zg_ref[...], *sts, al_ref[...], dt_ref[...], ng_ref[...])
        y_ref[...] = outs[0]
        for h in range(H):
            ssc[h] = outs[1 + h]

    t = bl * s
    return pl.pallas_call(
        body, grid=(bl, n_c),
        in_specs=[pl.BlockSpec((DN_C, 3 * D_DN), row), pl.BlockSpec((DN_C, LANES), row), pl.BlockSpec((DN_C, D_DN), row)] + par,
        out_specs=[pl.BlockSpec((DN_C, D_DN), row), pl.BlockSpec((1, H, DH, DH), row4)],
        out_shape=[jax.ShapeDtypeStruct((t, D_DN), F32), jax.ShapeDtypeStruct((bl * n_c, H, DH, DH), F32)],
        scratch_shapes=[pltpu.VMEM((H, DH, DH), F32)],
        name=name, compiler_params=_cp("arbitrary", "arbitrary"))(qkv, zab, zg, *params)


def _dn_scan_bwd(qkv, zab, zg, states, dy, params, bl, s, name):
    n_c = s // DN_C
    f = _dn_chunk_fn()
    row, row4, par, par_shapes = _dn_scan_specs(n_c, True)

    def body(q_ref, ab_ref, zg_ref, st_ref, dy_ref, al_ref, dt_ref, ng_ref,
             dq_ref, dab_ref, dzg_ref, dal_ref, ddt_ref, dng_ref, dsc):
        @pl.when((pl.program_id(0) == 0) & (pl.program_id(1) == 0))
        def _():
            for r in (dal_ref, ddt_ref, dng_ref):
                r[...] = jnp.zeros_like(r)

        @pl.when(pl.program_id(1) == 0)
        def _():
            dsc[...] = jnp.zeros_like(dsc)

        sts = [st_ref[0, h] for h in range(H)]
        _, vjp = jax.vjp(f, q_ref[...], ab_ref[...], zg_ref[...], *sts, al_ref[...], dt_ref[...], ng_ref[...])
        cts = vjp((dy_ref[...], *[dsc[h] for h in range(H)]))
        dq_ref[...] = cts[0]
        dab_ref[...] = cts[1]
        dzg_ref[...] = cts[2]
        for h in range(H):
            dsc[h] = cts[3 + h]
        dal_ref[...] += cts[3 + H]
        ddt_ref[...] += cts[4 + H]
        dng_ref[...] += cts[5 + H]

    t = bl * s
    return pl.pallas_call(
        body, grid=(bl, n_c),
        in_specs=[pl.BlockSpec((DN_C, 3 * D_DN), row), pl.BlockSpec((DN_C, LANES), row), pl.BlockSpec((DN_C, D_DN), row),
                  pl.BlockSpec((1, H, DH, DH), row4), pl.BlockSpec((DN_C, D_DN), row)] + par,
        out_specs=[pl.BlockSpec((DN_C, 3 * D_DN), row), pl.BlockSpec((DN_C, LANES), row), pl.BlockSpec((DN_C, D_DN), row)] + par,
        out_shape=[jax.ShapeDtypeStruct((t, 3 * D_DN), F32), jax.ShapeDtypeStruct((t, LANES), F32),
                   jax.ShapeDtypeStruct((t, D_DN), F32)] + [jax.ShapeDtypeStruct(sh, F32) for sh in par_shapes],
        scratch_shapes=[pltpu.VMEM((H, DH, DH), F32)],
        name=name, compiler_params=_cp("arbitrary", "arbitrary"))(qkv, zab, zg, states, dy, *params)


def _sg_fn(n_chunk):
    def f(z, lng, lnb, w, bsp_t):
        u = jax.nn.gelu(z[:, :D_SG])
        v = jax.nn.gelu(z[:, D_SG:2 * D_SG])
        gate = z[:, 2 * D_SG:]
        xc = v - jnp.mean(v, axis=-1, keepdims=True)
        vn = xc * lax.rsqrt(jnp.mean(xc * xc, axis=-1, keepdims=True) + EPS) * lng + lnb
        r = lax.broadcasted_iota(jnp.int32, (SG_C, SG_C), 0)
        c = lax.broadcasted_iota(jnp.int32, (SG_C, SG_C), 1)
        causal = r >= c
        first_half = c < SG_C // 2
        rr = lax.broadcasted_iota(jnp.int32, (LANES, D_SG), 0)
        cc = lax.broadcasted_iota(jnp.int32, (LANES, D_SG), 1)
        expand = jnp.where((cc >= rr * 64) & (cc < rr * 64 + 64) & (rr < 4), 1.0, 0.0)
        bias = _sel_r(bsp_t, expand)
        wm = [jnp.where(causal, w[h], 0.0) for h in range(4)]
        rows = []
        for ci in range(n_chunk):
            vc = vn[ci * SG_C:(ci + 1) * SG_C]
            pairs = []
            for pr in range(2):
                vp = vc[:, pr * LANES:(pr + 1) * LANES]
                pairs.append(jnp.where(first_half, _mm(wm[2 * pr], vp), _mm(wm[2 * pr + 1], vp)))
            rows.append(jnp.concatenate(pairs, axis=1) + bias)
        sp = jnp.concatenate(rows, axis=0) if n_chunk > 1 else rows[0]
        return u * sp * _silu(gate)

    return f


def _sg_specs():
    full = lambda i: (0, 0)
    full3 = lambda i: (0, 0, 0)
    par = [pl.BlockSpec((1, D_SG), full), pl.BlockSpec((1, D_SG), full), pl.BlockSpec((4, SG_C, SG_C), full3),
           pl.BlockSpec((SG_C, LANES), full)]
    par_shapes = [(1, D_SG), (1, D_SG), (4, SG_C, SG_C), (SG_C, LANES)]
    return par, par_shapes


def _sg_fwd(zsg, params, tb, name):
    t = zsg.shape[0]
    f = _sg_fn(tb // SG_C)
    par, _ = _sg_specs()

    def body(z_ref, g_ref, b_ref, w_ref, bs_ref, y_ref):
        y_ref[...] = f(z_ref[...], g_ref[...], b_ref[...], w_ref[...], bs_ref[...])

    row = lambda i: (i, 0)
    return pl.pallas_call(
        body, grid=(t // tb,), in_specs=[pl.BlockSpec((tb, 3 * D_SG), row)] + par,
        out_specs=pl.BlockSpec((tb, D_SG), row), out_shape=jax.ShapeDtypeStruct((t, D_SG), F32),
        name=name, compiler_params=_cp("parallel"))(zsg, *params)


def _sg_bwd(zsg, dy, params, tb, name):
    t = zsg.shape[0]
    f = _sg_fn(tb // SG_C)
    par, par_shapes = _sg_specs()

    def body(z_ref, dy_ref, g_ref, b_ref, w_ref, bs_ref, dz_ref, dg_ref, db_ref, dw_ref, dbs_ref):
        accs = (dg_ref, db_ref, dw_ref, dbs_ref)

        @pl.when(pl.program_id(0) == 0)
        def _():
            for r in accs:
                r[...] = jnp.zeros_like(r)

        _, vjp = jax.vjp(f, z_ref[...], g_ref[...], b_ref[...], w_ref[...], bs_ref[...])
        cts = vjp(dy_ref[...])
        dz_ref[...] = cts[0]
        for r, ct in zip(accs, cts[1:]):
            r[...] += ct

    row = lambda i: (i, 0)
    return pl.pallas_call(
        body, grid=(t // tb,), in_specs=[pl.BlockSpec((tb, 3 * D_SG), row), pl.BlockSpec((tb, D_SG), row)] + par,
        out_specs=[pl.BlockSpec((tb, 3 * D_SG), row)] + par,
        out_shape=[jax.ShapeDtypeStruct((t, 3 * D_SG), F32)] + [jax.ShapeDtypeStruct(sh, F32) for sh in par_shapes],
        name=name, compiler_params=_cp("arbitrary"))(zsg, dy, *params)


def _out_fwd(x, ys, p, w_out, pg, w_gate, w_ple, tb, name):
    t = x.shape[0]

    def body(x_ref, y0, y1, y2, p_ref, wo_ref, pg_ref, wg_ref, wp_ref, o_ref):
        y = jnp.concatenate([y0[...], y1[...], y2[...]], axis=1).astype(BF16)
        x1 = x_ref[...] + jnp.dot(y, wo_ref[...], preferred_element_type=F32)
        hn = _rms(x1, pg_ref[...]).astype(BF16)
        gate = jax.nn.sigmoid(jnp.dot(hn, wg_ref[...], preferred_element_type=F32))
        pp = jnp.dot(p_ref[...].astype(BF16), wp_ref[...], preferred_element_type=F32)
        o_ref[...] = x1 + gate * pp

    row = lambda i: (i, 0)
    full = lambda i: (0, 0)
    return pl.pallas_call(
        body, grid=(t // tb,),
        in_specs=[pl.BlockSpec((tb, D), row), pl.BlockSpec((tb, D_SSM), row), pl.BlockSpec((tb, D_DN), row),
                  pl.BlockSpec((tb, D_SG), row), pl.BlockSpec((tb, D_PLE), row), pl.BlockSpec((D, D), full),
                  pl.BlockSpec((1, D), full), pl.BlockSpec((D, D), full), pl.BlockSpec((D_PLE, D), full)],
        out_specs=pl.BlockSpec((tb, D), row), out_shape=jax.ShapeDtypeStruct((t, D), F32),
        name=name, compiler_params=_cp("parallel"))(x, *ys, p, w_out, pg, w_gate, w_ple)


def _out_bwd(x, ys, p, dx2, w_out, pg, w_gate, w_ple, tb, name):
    t = x.shape[0]

    def body(x_ref, y0, y1, y2, p_ref, d_ref, wo_ref, pg_ref, wg_ref, wp_ref,
             dx_ref, dy0, dy1, dy2, dwo_ref, dpg_ref, dwg_ref, dwp_ref):
        accs = (dwo_ref, dpg_ref, dwg_ref, dwp_ref)

        @pl.when(pl.program_id(0) == 0)
        def _():
            for r in accs:
                r[...] = jnp.zeros_like(r)

        y = jnp.concatenate([y0[...], y1[...], y2[...]], axis=1).astype(BF16)
        x1 = x_ref[...] + jnp.dot(y, wo_ref[...], preferred_element_type=F32)
        hn, rms_vjp = jax.vjp(_rms, x1, pg_ref[...])
        hb = hn.astype(BF16)
        gate = jax.nn.sigmoid(jnp.dot(hb, wg_ref[...], preferred_element_type=F32))
        pb = p_ref[...].astype(BF16)
        pp = jnp.dot(pb, wp_ref[...], preferred_element_type=F32)
        d2 = d_ref[...]
        dpp = (d2 * gate).astype(BF16)
        dlog = (d2 * pp * gate * (1.0 - gate)).astype(BF16)
        dwp_ref[...] += _dg(pb, dpp, 0, 0)
        dwg_ref[...] += _dg(hb, dlog, 0, 0)
        dx1_n, dpg = rms_vjp(_dg(dlog, wg_ref[...], 1, 1))
        dpg_ref[...] += dpg
        dx1 = d2 + dx1_n
        dx_ref[...] = dx1
        db = dx1.astype(BF16)
        dwo_ref[...] += _dg(y, db, 0, 0)
        dy = _dg(db, wo_ref[...], 1, 1)
        dy0[...] = dy[:, :D_SSM]
        dy1[...] = dy[:, D_SSM:D_SSM + D_DN]
        dy2[...] = dy[:, D_SSM + D_DN:]

    row = lambda i: (i, 0)
    full = lambda i: (0, 0)
    acts = [pl.BlockSpec((tb, D), row), pl.BlockSpec((tb, D_SSM), row), pl.BlockSpec((tb, D_DN), row), pl.BlockSpec((tb, D_SG), row)]
    wts = [pl.BlockSpec((D, D), full), pl.BlockSpec((1, D), full), pl.BlockSpec((D, D), full), pl.BlockSpec((D_PLE, D), full)]
    return pl.pallas_call(
        body, grid=(t // tb,),
        in_specs=acts + [pl.BlockSpec((tb, D_PLE), row), pl.BlockSpec((tb, D), row)] + wts,
        out_specs=acts + wts,
        out_shape=[jax.ShapeDtypeStruct((t, n), F32) for n in (D, D_SSM, D_DN, D_SG)]
        + [jax.ShapeDtypeStruct(sh, F32) for sh in ((D, D), (1, D), (D, D), (D_PLE, D))],
        name=name, compiler_params=_cp("arbitrary"))(x, *ys, p, dx2, w_out, pg, w_gate, w_ple)


def _loss_head(x, fg, target, tb, name):
    t = x.shape[0]

    def body(x_ref, g_ref, t_ref, dx_ref, dg_ref, loss_ref):
        @pl.when(pl.program_id(0) == 0)
        def _():
            dg_ref[...] = jnp.zeros_like(dg_ref)
            loss_ref[...] = jnp.zeros_like(loss_ref)

        y, vjp = jax.vjp(_rms, x_ref[...], g_ref[...])
        err = y - t_ref[...]
        loss_ref[...] += jnp.zeros_like(loss_ref) + 0.5 * jnp.sum(err * err) / D
        dx, dg = vjp(err / D)
        dx_ref[...] = dx
        dg_ref[...] += dg

    row = lambda i: (i, 0)
    full = lambda i: (0, 0)
    return pl.pallas_call(
        body, grid=(t // tb,),
        in_specs=[pl.BlockSpec((tb, D), row), pl.BlockSpec((1, D), full), pl.BlockSpec((tb, D), row)],
        out_specs=[pl.BlockSpec((tb, D), row), pl.BlockSpec((1, D), full), pl.BlockSpec((1, LANES), full)],
        out_shape=[jax.ShapeDtypeStruct((t, D), F32), jax.ShapeDtypeStruct((1, D), F32), jax.ShapeDtypeStruct((1, LANES), F32)],
        name=name, compiler_params=_cp("arbitrary"))(x, fg, target)


def _hbm_specs(n):
    return [pl.BlockSpec(memory_space=pl.ANY)] * n


def _all_gather(blocks, name):
    n = len(blocks)

    def body(*refs):
        ins, outs = refs[:n], refs[n:2 * n]
        send_sems, recv_sems, local_sems = refs[2 * n:]
        x, y, c = lax.axis_index("x"), lax.axis_index("y"), lax.axis_index("c")
        me, sibling = (x, y, c), (x, y, 1 - c)
        chips = [(1 - x, y), (x, 1 - y), (1 - x, 1 - y)]

        def slot(a, px, py, pc):
            return outs[a].at[4 * px + 2 * py + pc]

        def copy(a, k, blk, to, src=None):
            return pltpu.make_async_remote_copy(
                src_ref=slot(a, *blk) if src is None else src, dst_ref=slot(a, *blk),
                send_sem=send_sems.at[7 * a + k], recv_sem=recv_sems.at[7 * a + k],
                device_id=to, device_id_type=pl.DeviceIdType.MESH)

        mines = [pltpu.make_async_copy(ins[a], slot(a, *me), local_sems.at[a]) for a in range(n)]
        for cp in mines:
            cp.start()
        first = []
        for a in range(n):
            first.append(copy(a, 0, me, sibling, src=ins[a]))
            first += [copy(a, 1 + j, me, (*chip, c), src=ins[a]) for j, chip in enumerate(chips)]
        for cp in first:
            cp.start()
        passed = []
        for j, chip in enumerate(chips):
            for a in range(n):
                copy(a, 1 + j, (*chip, c), me).wait_recv()
                onward = copy(a, 4 + j, (*chip, c), sibling)
                onward.start()
                passed.append(onward)
        for a in range(n):
            copy(a, 0, sibling, me).wait_recv()
        for j, chip in enumerate(chips):
            for a in range(n):
                copy(a, 4 + j, (*chip, 1 - c), me).wait_recv()
        for cp in first + passed:
            cp.wait_send()
        for cp in mines:
            cp.wait()

    return pl.pallas_call(
        body, out_shape=[jax.ShapeDtypeStruct((N_DEV, *b.shape), b.dtype) for b in blocks],
        in_specs=_hbm_specs(n), out_specs=_hbm_specs(n),
        scratch_shapes=[pltpu.SemaphoreType.DMA((7 * n,)), pltpu.SemaphoreType.DMA((7 * n,)), pltpu.SemaphoreType.DMA((n,))],
        name=name)(*blocks)


def _pair_exchange(gs, name):
    n = len(gs)

    def body(*refs):
        ins, owns, recvs = refs[:n], refs[n:2 * n], refs[2 * n:3 * n]
        send_sems, recv_sems, local_sems = refs[3 * n:]
        x, y, c = lax.axis_index("x"), lax.axis_index("y"), lax.axis_index("c")
        local = [pltpu.make_async_copy(ins[a].at[c], owns[a], local_sems.at[a]) for a in range(n)]
        remote = [pltpu.make_async_remote_copy(
            src_ref=ins[a].at[1 - c], dst_ref=recvs[a], send_sem=send_sems.at[a], recv_sem=recv_sems.at[a],
            device_id=(x, y, 1 - c), device_id_type=pl.DeviceIdType.MESH) for a in range(n)]
        for cp in local + remote:
            cp.start()
        for cp in remote:
            cp.wait_send()
            cp.wait_recv()
        for cp in local:
            cp.wait()

    half = [jax.ShapeDtypeStruct(g.shape[1:], g.dtype) for g in gs]
    outs = pl.pallas_call(
        body, out_shape=half + half, in_specs=_hbm_specs(n), out_specs=_hbm_specs(2 * n),
        scratch_shapes=[pltpu.SemaphoreType.DMA((n,)), pltpu.SemaphoreType.DMA((n,)), pltpu.SemaphoreType.DMA((n,))],
        name=name)(*gs)
    return outs[:n], outs[n:]


def _chip_exchange(ps, name):
    n = len(ps)

    def body(*refs):
        ins, outs = refs[:n], refs[n:2 * n]
        send_sems, recv_sems, local_sems = refs[2 * n:]
        x, y, c = lax.axis_index("x"), lax.axis_index("y"), lax.axis_index("c")
        my_chip = 2 * x + y
        local = [pltpu.make_async_copy(ins[a].at[my_chip], outs[a].at[my_chip], local_sems.at[a]) for a in range(n)]
        remote = []
        for j in range(1, 4):
            px = 1 - x if j & 2 else x
            py = 1 - y if j & 1 else y
            for a in range(n):
                remote.append(pltpu.make_async_remote_copy(
                    src_ref=ins[a].at[2 * px + py], dst_ref=outs[a].at[my_chip],
                    send_sem=send_sems.at[3 * a + j - 1], recv_sem=recv_sems.at[3 * a + j - 1],
                    device_id=(px, py, c), device_id_type=pl.DeviceIdType.MESH))
        for cp in local + remote:
            cp.start()
        for cp in remote:
            cp.wait_send()
            cp.wait_recv()
        for cp in local:
            cp.wait()

    return pl.pallas_call(
        body, out_shape=[jax.ShapeDtypeStruct(q.shape, q.dtype) for q in ps], in_specs=_hbm_specs(n), out_specs=_hbm_specs(n),
        scratch_shapes=[pltpu.SemaphoreType.DMA((3 * n,)), pltpu.SemaphoreType.DMA((3 * n,)), pltpu.SemaphoreType.DMA((n,))],
        name=name)(*ps)


def _row_block(rows, bytes_per_row):
    best = None
    for rb in range(16, rows + 1, 16):
        if rows % rb == 0 and rb * bytes_per_row <= ELEMENTWISE_STEP_BYTES:
            best = rb
    return rows if best is None else best


def _add_pair(own, recv, name):
    shape = own.shape
    last = shape[-1]
    rows = own.size // last
    rb = _row_block(rows, 3 * 4 * (-(-last // LANES) * LANES))

    def body(a_ref, b_ref, o_ref):
        o_ref[...] = (a_ref[...].astype(F32) + b_ref[...].astype(F32)).astype(o_ref.dtype)

    row = lambda i: (i, 0)
    out = pl.pallas_call(
        body, grid=(rows // rb,), in_specs=[pl.BlockSpec((rb, last), row)] * 2, out_specs=pl.BlockSpec((rb, last), row),
        out_shape=jax.ShapeDtypeStruct((rows, last), own.dtype), name=name,
        compiler_params=_cp("parallel"))(own.reshape(rows, last), recv.reshape(rows, last))
    return out.reshape(shape)


def _sum_adamw(gk, w, m, v, name):
    shape = w.shape
    n_part = gk.shape[0]
    last = shape[-1]
    rows = w.size // last
    rb = _row_block(rows, (n_part + 7) * 4 * (-(-last // LANES) * LANES))

    def body(g_ref, w_ref, m_ref, v_ref, go_ref, d_ref, mo_ref, vo_ref):
        g = g_ref[0].astype(F32)
        for k in range(1, n_part):
            g = g + g_ref[k].astype(F32)
        mn = ADAM_B1 * m_ref[...] + (1.0 - ADAM_B1) * g
        vn = ADAM_B2 * v_ref[...] + (1.0 - ADAM_B2) * jnp.square(g)
        m_hat = mn / (1.0 - ADAM_B1 ** ADAM_STEP)
        v_hat = vn / (1.0 - ADAM_B2 ** ADAM_STEP)
        go_ref[...] = g
        d_ref[...] = -ADAM_LR * (m_hat / (jnp.sqrt(v_hat) + ADAM_EPS) + ADAM_WD * w_ref[...])
        mo_ref[...] = mn
        vo_ref[...] = vn

    row = lambda i: (i, 0)
    outs = pl.pallas_call(
        body, grid=(rows // rb,),
        in_specs=[pl.BlockSpec((n_part, rb, last), lambda i: (0, i, 0))] + [pl.BlockSpec((rb, last), row)] * 3,
        out_specs=[pl.BlockSpec((rb, last), row)] * 4,
        out_shape=[jax.ShapeDtypeStruct((rows, last), F32)] * 4,
        name=name, compiler_params=_cp("parallel"))(gk.reshape(n_part, rows, last), *[a.reshape(rows, last) for a in (w, m, v)])
    return [o.reshape(shape) for o in outs]


def _seg_rows(shape):
    n = 1
    for d in shape:
        n *= d
    return -(-n // (8 * LANES)) * 8


def _pack(arrs):
    segs = []
    for a in arrs:
        r = _seg_rows(a.shape)
        segs.append(jnp.pad(a.reshape(-1).astype(F32), (0, r * LANES - a.size)).reshape(r, LANES))
    rows = sum(s.shape[0] for s in segs)
    total = -(-rows // PACK_ROWS) * PACK_ROWS
    if total > rows:
        segs.append(jnp.zeros((total - rows, LANES), F32))
    return jnp.concatenate(segs, axis=0)


def _unpack(pack, shapes):
    out, off = [], 0
    for sh in shapes:
        r = _seg_rows(sh)
        n = 1
        for d in sh:
            n *= d
        out.append(pack[off:off + r].reshape(-1)[:n].reshape(sh))
        off += r
    return out


def _to_dest_blocks(full, axis, dtype):
    sh = list(full.shape)
    sh[axis:axis + 1] = [N_DEV // 2, 2, sh[axis] // N_DEV]
    return jnp.moveaxis(full.reshape(sh), (axis, axis + 1), (1, 0)).astype(dtype)


def _from_gathered(g, axis):
    m = jnp.moveaxis(g, 0, axis)
    sh = list(m.shape)
    sh[axis:axis + 2] = [sh[axis] * sh[axis + 1]]
    return m.reshape(sh)


def _reorder_w_in(w):
    return jnp.concatenate([w[:, :2048], w[:, 2056:3336], w[:, 2048:2056], jnp.zeros((D, ZW - 3336), w.dtype)], axis=1)


def _restore_dw_in(dw):
    return jnp.concatenate([dw[:, :2048], dw[:, 3328:3336], dw[:, 2048:3328]], axis=1)


def _local_step(x, p, wts, target):
    bl, s, _ = x.shape
    t = bl * s
    depth = p.shape[0]
    tb, sg_tb, prep_tb = TB, SG_TB, PREP_TB
    xs = [x.reshape(t, D)]
    saved = []
    for i in range(depth):
        li = f"l{i}"
        ng = wts['norm_g'][i].reshape(1, D)
        w_in = _reorder_w_in(wts['w_in'][i]).astype(BF16)
        s5_par_in = (wts['ssm_a_re'][i], wts['ssm_a_im'][i], wts['ssm_b_re'][i], wts['ssm_b_im'][i],
                     wts['ssm_c_re'][i], wts['ssm_c_im'][i], wts['ssm_d'][i], wts['ssm_log_step'][i])
        tabs, tab_vjp = jax.vjp(_s5_tables, *s5_par_in)
        s5_par = (*tabs, wts['ssm_w_glu'][i], wts['ssm_b_glu'][i].reshape(1, D_SSM))
        conv8 = jnp.pad(wts['dn_conv_w'][i], ((0, 4), (0, 0)))
        dn_par = (jnp.repeat(wts['dn_a_log'][i], DH).reshape(1, D_DN), jnp.repeat(wts['dn_dt_bias'][i], DH).reshape(1, D_DN),
                  wts['dn_norm_g'][i].reshape(1, DH))
        sg_par = (wts['sg_ln_g'][i].reshape(1, D_SG), wts['sg_ln_b'][i].reshape(1, D_SG), wts['sg_w'][i],
                  jnp.pad(jnp.transpose(wts['sg_b'][i]), ((0, 0), (0, LANES - 4))))
        out_par = (wts['w_out'][i].astype(BF16), wts['ple_norm_g'][i].reshape(1, D), wts['w_ple_gate'][i].astype(BF16),
                   wts['w_ple'][i].astype(BF16))
        pi = p[i].reshape(t, D_PLE)

        z_ssm, z_qkv, z_gdn, z_sg, z_ab = _in_proj_fwd(xs[i], ng, w_in, tb, f"in_proj_fwd_{li}")
        y_ssm, carries = _s5_fwd(z_ssm, s5_par, bl, s, f"s5_fwd_{li}")
        qkvn = _dn_prep_fwd(z_qkv, conv8, bl, s, prep_tb, f"dn_prep_fwd_{li}")
        y_dn, states = _dn_scan_fwd(qkvn, z_ab, z_gdn, dn_par, bl, s, f"dn_scan_fwd_{li}")
        y_sg = _sg_fwd(z_sg, sg_par, sg_tb, f"sg_fwd_{li}")
        ys = (y_ssm, y_dn, y_sg)
        xs.append(_out_fwd(xs[i], ys, pi, *out_par, tb, f"out_fwd_{li}"))
        saved.append(dict(ng=ng, w_in=w_in, tab_vjp=tab_vjp, s5_par=s5_par, conv8=conv8, dn_par=dn_par, sg_par=sg_par,
                          out_par=out_par, pi=pi, z=(z_ssm, z_qkv, z_gdn, z_sg, z_ab), carries=carries, qkvn=qkvn,
                          states=states, ys=ys))

    dx, dfg, loss_vec = _loss_head(xs[depth], wts['final_norm_g'].reshape(1, D), target.reshape(t, D), tb, "loss_head")
    grads = {n: [None] * depth for n in WEIGHTS if n != 'final_norm_g'}
    grads['final_norm_g'] = dfg.reshape(D)
    for i in reversed(range(depth)):
        li = f"l{i}"
        sv = saved[i]
        z_ssm, z_qkv, z_gdn, z_sg, z_ab = sv['z']
        dx_res, dy_ssm, dy_dn, dy_sg, dwo, dpg, dwg, dwp = _out_bwd(xs[i], sv['ys'], sv['pi'], dx, *sv['out_par'], tb, f"out_bwd_{li}")
        dz_sg, dlng, dlnb, dsgw, dbsp = _sg_bwd(z_sg, dy_sg, sv['sg_par'], sg_tb, f"sg_bwd_{li}")
        dqkvn, dz_ab, dz_gdn, dal, ddt, dng = _dn_scan_bwd(sv['qkvn'], z_ab, z_gdn, sv['states'], dy_dn, sv['dn_par'], bl, s,
                                                          f"dn_scan_bwd_{li}")
        dz_qkv, dconv = _dn_prep_bwd(z_qkv, dqkvn, sv['conv8'], bl, s, prep_tb, f"dn_prep_bwd_{li}")
        dz_ssm, dbb, dcb, dld, dlp, ddv, dwglu, dbglu = _s5_bwd(z_ssm, sv['carries'], dy_ssm, sv['s5_par'], bl, s, f"s5_bwd_{li}")
        dzs = (dz_ssm, dz_qkv, dz_gdn, dz_sg, dz_ab)
        dx, dnorm = _in_proj_bwd_dx(xs[i], sv['ng'], sv['w_in'], dzs, dx_res, tb, f"in_proj_bwd_dx_{li}")
        dws = _in_proj_bwd_dw(xs[i], sv['ng'], dzs, tb, f"in_proj_bwd_dw_{li}")
        ds5 = sv['tab_vjp']((dbb, dcb, dld, dlp, ddv))
        for n, gval in zip(('ssm_a_re', 'ssm_a_im', 'ssm_b_re', 'ssm_b_im', 'ssm_c_re', 'ssm_c_im', 'ssm_d', 'ssm_log_step'), ds5):
            grads[n][i] = gval
        grads['norm_g'][i] = dnorm.reshape(D)
        grads['w_in'][i] = _restore_dw_in(jnp.concatenate(dws, axis=1))
        grads['ssm_w_glu'][i] = dwglu
        grads['ssm_b_glu'][i] = dbglu.reshape(D_SSM)
        grads['dn_conv_w'][i] = dconv[:4]
        grads['dn_a_log'][i] = dal.reshape(H, DH).sum(axis=1)
        grads['dn_dt_bias'][i] = ddt.reshape(H, DH).sum(axis=1)
        grads['dn_norm_g'][i] = dng.reshape(DH)
        grads['sg_ln_g'][i] = dlng.reshape(D_SG)
        grads['sg_ln_b'][i] = dlnb.reshape(D_SG)
        grads['sg_w'][i] = dsgw
        grads['sg_b'][i] = jnp.transpose(dbsp[:, :4])
        grads['w_out'][i] = dwo
        grads['ple_norm_g'][i] = dpg.reshape(D)
        grads['w_ple_gate'][i] = dwg
        grads['w_ple'][i] = dwp
    grads = {n: (g if n == 'final_norm_g' else jnp.stack(g)) for n, g in grads.items()}
    return loss_vec[0, 0], dx.reshape(bl, s, D), grads


def kernel(x, p, norm_g, w_in, ssm_a_re, ssm_a_im, ssm_b_re, ssm_b_im, ssm_c_re, ssm_c_im, ssm_d, ssm_log_step, ssm_w_glu, ssm_b_glu, dn_conv_w, dn_a_log, dn_dt_bias, dn_norm_g, sg_ln_g, sg_ln_b, sg_w, sg_b, w_out, ple_norm_g, w_ple_gate, w_ple, final_norm_g, loss_target, m_norm_g, m_w_in, m_ssm_a_re, m_ssm_a_im, m_ssm_b_re, m_ssm_b_im, m_ssm_c_re, m_ssm_c_im, m_ssm_d, m_ssm_log_step, m_ssm_w_glu, m_ssm_b_glu, m_dn_conv_w, m_dn_a_log, m_dn_dt_bias, m_dn_norm_g, m_sg_ln_g, m_sg_ln_b, m_sg_w, m_sg_b, m_w_out, m_ple_norm_g, m_w_ple_gate, m_w_ple, m_final_norm_g, v_norm_g, v_w_in, v_ssm_a_re, v_ssm_a_im, v_ssm_b_re, v_ssm_b_im, v_ssm_c_re, v_ssm_c_im, v_ssm_d, v_ssm_log_step, v_ssm_w_glu, v_ssm_b_glu, v_dn_conv_w, v_dn_a_log, v_dn_dt_bias, v_dn_norm_g, v_sg_ln_g, v_sg_ln_b, v_sg_w, v_sg_b, v_w_out, v_ple_norm_g, v_w_ple_gate, v_w_ple, v_final_norm_g):
    w_loc = dict(zip(WEIGHTS, (norm_g, w_in, ssm_a_re, ssm_a_im, ssm_b_re, ssm_b_im, ssm_c_re, ssm_c_im, ssm_d, ssm_log_step,
                               ssm_w_glu, ssm_b_glu, dn_conv_w, dn_a_log, dn_dt_bias, dn_norm_g, sg_ln_g, sg_ln_b, sg_w, sg_b,
                               w_out, ple_norm_g, w_ple_gate, w_ple, final_norm_g)))
    m_loc = dict(zip(WEIGHTS, (m_norm_g, m_w_in, m_ssm_a_re, m_ssm_a_im, m_ssm_b_re, m_ssm_b_im, m_ssm_c_re, m_ssm_c_im, m_ssm_d,
                               m_ssm_log_step, m_ssm_w_glu, m_ssm_b_glu, m_dn_conv_w, m_dn_a_log, m_dn_dt_bias, m_dn_norm_g,
                               m_sg_ln_g, m_sg_ln_b, m_sg_w, m_sg_b, m_w_out, m_ple_norm_g, m_w_ple_gate, m_w_ple, m_final_norm_g)))
    v_loc = dict(zip(WEIGHTS, (v_norm_g, v_w_in, v_ssm_a_re, v_ssm_a_im, v_ssm_b_re, v_ssm_b_im, v_ssm_c_re, v_ssm_c_im, v_ssm_d,
                               v_ssm_log_step, v_ssm_w_glu, v_ssm_b_glu, v_dn_conv_w, v_dn_a_log, v_dn_dt_bias, v_dn_norm_g,
                               v_sg_ln_g, v_sg_ln_b, v_sg_w, v_sg_b, v_w_out, v_ple_norm_g, v_w_ple_gate, v_w_ple, v_final_norm_g)))

    gathered = _all_gather([w_loc[n].astype(WIRE[n]) for n in SHARDED_ORDER], "gather_weights")
    full = dict(w_loc)
    for n, g in zip(SHARDED_ORDER, gathered):
        full[n] = _from_gathered(g, SHARDED[n])
    full['ssm_w_glu'] = full['ssm_w_glu'].astype(F32)

    loss_part, grad_x, grads = _local_step(x, p, full, loss_target)

    dest = [_to_dest_blocks(grads[n], SHARDED[n], WIRE[n]) for n in SHARDED_ORDER]
    own, from_sibling = _pair_exchange(dest, "grads_pair_exchange")
    chip_sums = [_add_pair(a, b, f"grads_pair_sum_{n}") for n, a, b in zip(SHARDED_ORDER, own, from_sibling)]
    by_chip = _chip_exchange(chip_sums, "grads_chip_exchange")
    rep_pack = _pack([grads[n] for n in REPLICATED_ORDER] + [loss_part.reshape(1)])
    (rep_recv,) = _all_gather([rep_pack], "gather_small_grads")

    outs = {k: {} for k in 'gdmv'}
    for n, gk in zip(SHARDED_ORDER, by_chip):
        for k, o in zip('gdmv', _sum_adamw(gk, w_loc[n], m_loc[n], v_loc[n], f"adamw_{n}")):
            outs[k][n] = o
    one = jnp.zeros((1,), F32)
    rep_out = _sum_adamw(rep_recv, _pack([w_loc[n] for n in REPLICATED_ORDER] + [one]),
                         _pack([m_loc[n] for n in REPLICATED_ORDER] + [one]),
                         _pack([v_loc[n] for n in REPLICATED_ORDER] + [one]), "adamw_replicated")
    rep_shapes = [w_loc[n].shape for n in REPLICATED_ORDER] + [(1,)]
    for k, rep_p in zip('gdmv', rep_out):
        outs[k].update(zip(REPLICATED_ORDER + ['loss'], _unpack(rep_p, rep_shapes)))
    loss = outs['g']['loss'].reshape(())
    return (loss, grad_x, *[outs['g'][n] for n in WEIGHTS], *[outs['d'][n] for n in WEIGHTS],
            *[outs['m'][n] for n in WEIGHTS], *[outs['v'][n] for n in WEIGHTS])
```

```python
import functools

import jax
import jax.numpy as jnp
from jax import lax
from jax.experimental import pallas as pl
from jax.experimental.pallas import tpu as pltpu

F32 = jnp.float32
BF16 = jnp.bfloat16
EPS = 1e-6

D = 1024
D_PLE = 256
D_SSM = 256
D_DN = 512
D_SG = 256
G = 16
CG = 16
NS = 64
NRE = G * NS
H = 4
DH = 128
DN_C = 128
SG_C = 128
ZW = 3456
Z_PIECES = (512, 1536, 512, 768, 128)
N_DEV = 8
LANES = 128
PACK_ROWS = 256
VMEM_LIMIT = 56 * 1024 * 1024
ELEMENTWISE_STEP_BYTES = 4 * 1024 * 1024
TB = 256
SG_TB = 512
PREP_TB = 256

ADAM_LR = 0.001
ADAM_B1 = 0.9
ADAM_B2 = 0.999
ADAM_EPS = 1e-08
ADAM_WD = 0.01
ADAM_STEP = 10

S5_L = 128
S5_GROUP = 8
S5_SHIFTS = (1, 2, 4)

WEIGHTS = ['norm_g', 'w_in', 'ssm_a_re', 'ssm_a_im', 'ssm_b_re', 'ssm_b_im', 'ssm_c_re', 'ssm_c_im', 'ssm_d',
           'ssm_log_step', 'ssm_w_glu', 'ssm_b_glu', 'dn_conv_w', 'dn_a_log', 'dn_dt_bias', 'dn_norm_g', 'sg_ln_g',
           'sg_ln_b', 'sg_w', 'sg_b', 'w_out', 'ple_norm_g', 'w_ple_gate', 'w_ple', 'final_norm_g']
SHARDED = {'w_in': 2, 'ssm_w_glu': 1, 'dn_conv_w': 2, 'w_out': 1, 'w_ple_gate': 1, 'w_ple': 2}
SHARDED_ORDER = ['w_in', 'ssm_w_glu', 'dn_conv_w', 'w_out', 'w_ple_gate', 'w_ple']
WIRE = {'w_in': BF16, 'ssm_w_glu': BF16, 'dn_conv_w': F32, 'w_out': BF16, 'w_ple_gate': BF16, 'w_ple': BF16}
REPLICATED_ORDER = [n for n in WEIGHTS if n not in SHARDED]


def _cp(*sem):
    return pltpu.CompilerParams(dimension_semantics=sem, vmem_limit_bytes=VMEM_LIMIT)


def _dg(a, b, ca, cb, precision=None):
    return lax.dot_general(a, b, (((ca,), (cb,)), ((), ())), precision=precision, preferred_element_type=F32)


@jax.custom_vjp
def _mm(a, b):
    return _dg(a.astype(BF16), b.astype(BF16), 1, 0)


def _mm_fwd(a, b):
    return _mm(a, b), (a, b)


def _mm_bwd(res, g):
    a, b = res
    gb = g.astype(BF16)
    return _dg(gb, b.astype(BF16), 1, 1), _dg(a.astype(BF16), gb, 0, 0)


_mm.defvjp(_mm_fwd, _mm_bwd)


@jax.custom_vjp
def _mm_nt(a, b):
    return _dg(a.astype(BF16), b.astype(BF16), 1, 1)


def _mm_nt_fwd(a, b):
    return _mm_nt(a, b), (a, b)


def _mm_nt_bwd(res, g):
    a, b = res
    gb = g.astype(BF16)
    return _dg(gb, b.astype(BF16), 1, 0), _dg(gb, a.astype(BF16), 0, 0)


_mm_nt.defvjp(_mm_nt_fwd, _mm_nt_bwd)


@jax.custom_vjp
def _mm_tn(a, b):
    return _dg(a.astype(BF16), b.astype(BF16), 0, 0)


def _mm_tn_fwd(a, b):
    return _mm_tn(a, b), (a, b)


def _mm_tn_bwd(res, g):
    a, b = res
    gb = g.astype(BF16)
    return _dg(b.astype(BF16), gb, 1, 1), _dg(a.astype(BF16), gb, 1, 0)


_mm_tn.defvjp(_mm_tn_fwd, _mm_tn_bwd)


def _split(x, n):
    pieces = []
    for _ in range(n - 1):
        hi = x.astype(BF16)
        pieces.append(hi)
        x = x - hi.astype(F32)
    pieces.append(x.astype(BF16))
    return pieces


def _dg3(a, b, ca, cb):
    a_hi, a_lo = _split(a, 2)
    b_hi, b_lo = _split(b, 2)
    return _dg(a_hi, b_hi, ca, cb) + (_dg(a_hi, b_lo, ca, cb) + _dg(a_lo, b_hi, ca, cb))


@jax.custom_vjp
def _dot3(a, b):
    return _dg3(a, b, 1, 0)


def _dot3_fwd(a, b):
    return _dot3(a, b), (a, b)


def _dot3_bwd(res, g):
    a, b = res
    return _dg3(g, b, 1, 1), _dg3(a, g, 0, 0)


_dot3.defvjp(_dot3_fwd, _dot3_bwd)


def _dg_sel(x, e, cx, ce, x_first):
    eb = e.astype(BF16)
    out = None
    for piece in reversed(_split(x, 3)):
        term = _dg(piece, eb, cx, ce) if x_first else _dg(eb, piece, ce, cx)
        out = term if out is None else out + term
    return out


@jax.custom_vjp
def _sel_r(x, e):
    return _dg_sel(x, e, 1, 0, True)


def _sel_r_fwd(x, e):
    return _sel_r(x, e), e


def _sel_r_bwd(e, g):
    return _dg_sel(g, e, 1, 1, True), jnp.zeros_like(e)


_sel_r.defvjp(_sel_r_fwd, _sel_r_bwd)


@jax.custom_vjp
def _sel_l(e, x):
    return _dg_sel(x, e, 0, 1, False)


def _sel_l_fwd(e, x):
    return _sel_l(e, x), e


def _sel_l_bwd(e, g):
    return jnp.zeros_like(e), _dg_sel(g, e, 0, 0, False)


_sel_l.defvjp(_sel_l_fwd, _sel_l_bwd)


def _rms(x, g):
    return x * lax.rsqrt(jnp.mean(x * x, axis=-1, keepdims=True) + EPS) * g


def _silu(x):
    return x * jax.nn.sigmoid(x)


def _in_proj_fwd(x, g, w, tb, name):
    t = x.shape[0]

    def body(x_ref, g_ref, w_ref, *z_refs):
        h = _rms(x_ref[...], g_ref[...])
        z = jnp.dot(h.astype(BF16), w_ref[...], preferred_element_type=F32)
        off = 0
        for z_ref, n in zip(z_refs, Z_PIECES):
            z_ref[...] = z[:, off:off + n]
            off += n

    row = lambda i: (i, 0)
    full = lambda i: (0, 0)
    return pl.pallas_call(
        body, grid=(t // tb,),
        in_specs=[pl.BlockSpec((tb, D), row), pl.BlockSpec((1, D), full), pl.BlockSpec((D, ZW), full)],
        out_specs=[pl.BlockSpec((tb, n), row) for n in Z_PIECES],
        out_shape=[jax.ShapeDtypeStruct((t, n), F32) for n in Z_PIECES],
        name=name, compiler_params=_cp("parallel"))(x, g, w)


def _in_proj_bwd_dx(x, g, w, dzs, dx_res, tb, name):
    t = x.shape[0]

    def body(x_ref, g_ref, w_ref, d0, d1, d2, d3, d4, dxr_ref, dx_ref, dg_ref):
        @pl.when(pl.program_id(0) == 0)
        def _():
            dg_ref[...] = jnp.zeros_like(dg_ref)

        dz = jnp.concatenate([d0[...], d1[...], d2[...], d3[...], d4[...]], axis=1).astype(BF16)
        dh = _dg(dz, w_ref[...], 1, 1)
        _, vjp = jax.vjp(_rms, x_ref[...], g_ref[...])
        dx, dg = vjp(dh)
        dx_ref[...] = dx + dxr_ref[...]
        dg_ref[...] += dg

    row = lambda i: (i, 0)
    full = lambda i: (0, 0)
    return pl.pallas_call(
        body, grid=(t // tb,),
        in_specs=[pl.BlockSpec((tb, D), row), pl.BlockSpec((1, D), full), pl.BlockSpec((D, ZW), full)]
        + [pl.BlockSpec((tb, n), row) for n in Z_PIECES] + [pl.BlockSpec((tb, D), row)],
        out_specs=[pl.BlockSpec((tb, D), row), pl.BlockSpec((1, D), full)],
        out_shape=[jax.ShapeDtypeStruct((t, D), F32), jax.ShapeDtypeStruct((1, D), F32)],
        name=name, compiler_params=_cp("arbitrary"))(x, g, w, *dzs, dx_res)


def _in_proj_bwd_dw(x, g, dzs, tb, name):
    t = x.shape[0]

    def body(x_ref, g_ref, d0, d1, d2, d3, d4, *dw_refs):
        @pl.when(pl.program_id(0) == 0)
        def _():
            for r in dw_refs:
                r[...] = jnp.zeros_like(r)

        h = _rms(x_ref[...], g_ref[...]).astype(BF16)
        for d_ref, dw_ref in zip((d0, d1, d2, d3, d4), dw_refs):
            dw_ref[...] += _dg(h, d_ref[...].astype(BF16), 0, 0)

    row = lambda i: (i, 0)
    full = lambda i: (0, 0)
    return pl.pallas_call(
        body, grid=(t // tb,),
        in_specs=[pl.BlockSpec((tb, D), row), pl.BlockSpec((1, D), full)] + [pl.BlockSpec((tb, n), row) for n in Z_PIECES],
        out_specs=[pl.BlockSpec((D, n), full) for n in Z_PIECES],
        out_shape=[jax.ShapeDtypeStruct((D, n), F32) for n in Z_PIECES],
        name=name, compiler_params=_cp("arbitrary"))(x, g, *dzs)


def _s5_tables(a_re, a_im, b_re, b_im, c_re, c_im, d_skip, log_step):
    step = jnp.exp(log_step)[:, None]

    def lam_pow(k):
        mag = jnp.exp(k * a_re * step)
        ang = k * a_im * step
        return mag * jnp.cos(ang), mag * jnp.sin(ang)

    lam_re, lam_im = lam_pow(1.0)
    den = a_re * a_re + a_im * a_im
    nr, ni = lam_re - 1.0, lam_im
    f_re = (nr * a_re + ni * a_im) / den
    f_im = (ni * a_re - nr * a_im) / den
    bbar_re = f_re[..., None] * b_re - f_im[..., None] * b_im
    bbar_im = f_re[..., None] * b_im + f_im[..., None] * b_re
    eye = jnp.eye(G, dtype=F32)

    def blk_b(bb):
        return (jnp.transpose(bb, (0, 2, 1))[:, :, None, :] * eye[:, None, :, None]).reshape(D_SSM, NRE)

    def blk_c(cc):
        return (jnp.transpose(cc, (0, 2, 1))[:, :, None, :] * eye[:, None, :, None]).reshape(NRE, D_SSM)

    b_blk = jnp.concatenate([blk_b(bbar_re), blk_b(bbar_im)], axis=1)
    c_blk = jnp.concatenate([blk_c(c_re), -blk_c(c_im)], axis=0)
    ks = jnp.asarray(S5_SHIFTS, F32)[:, None, None]
    ld_re, ld_im = lam_pow(ks)
    ld = jnp.concatenate([ld_re.reshape(-1, 1, NRE), ld_im.reshape(-1, 1, NRE)], axis=-1)
    js = jnp.arange(1, S5_GROUP + 1, dtype=F32)[:, None, None]
    lp_re, lp_im = lam_pow(js)
    lp = jnp.concatenate([lp_re.reshape(S5_GROUP, NRE), lp_im.reshape(S5_GROUP, NRE)], axis=-1)
    return b_blk, c_blk, ld, lp, d_skip.reshape(1, D_SSM)


def _make_group_shift(rows):
    @functools.partial(jax.custom_vjp, nondiff_argnums=(1,))
    def shift(x, d):
        r = lax.broadcasted_iota(jnp.int32, x.shape, 0) & (S5_GROUP - 1)
        return jnp.where(r >= d, pltpu.roll(x, d, 0), 0.0)

    def fwd(x, d):
        return shift(x, d), None

    def bwd(d, _, g):
        r = lax.broadcasted_iota(jnp.int32, g.shape, 0) & (S5_GROUP - 1)
        return (jnp.where(r < S5_GROUP - d, pltpu.roll(g, rows - d, 0), 0.0),)

    shift.defvjp(fwd, bwd)
    return shift


def _s5_chunk_fn():
    shift = _make_group_shift(S5_L)

    def f(u, gate, cr, ci, b_blk, c_blk, lds, lp, dv, wglu, bglu):
        bu = _mm(u, b_blk)
        hr, hi = bu[:, :NRE], bu[:, NRE:]
        for ld, d in zip(lds, S5_SHIFTS):
            lr, li = ld[:, :NRE], ld[:, NRE:]
            sr, si = shift(hr, d), shift(hi, d)
            hr, hi = hr + lr * sr - li * si, hi + lr * si + li * sr
        pr, pi = lp[:, :NRE], lp[:, NRE:]
        rows_r, rows_i = [], []
        for r in range(S5_L // S5_GROUP):
            br, bi = hr[r * S5_GROUP:(r + 1) * S5_GROUP], hi[r * S5_GROUP:(r + 1) * S5_GROUP]
            br, bi = br + pr * cr - pi * ci, bi + pr * ci + pi * cr
            cr, ci = br[S5_GROUP - 1:S5_GROUP], bi[S5_GROUP - 1:S5_GROUP]
            rows_r.append(br)
            rows_i.append(bi)
        ncr, nci = cr, ci
        y = _mm(jnp.concatenate([jnp.concatenate(rows_r, axis=0), jnp.concatenate(rows_i, axis=0)], axis=1), c_blk) + dv * u
        y = jax.nn.gelu(y)
        y = y * jax.nn.sigmoid(_mm(y, wglu) + bglu)
        return y * _silu(gate), ncr, nci

    return f


def _s5_specs(n_s, rev):
    nd = len(S5_SHIFTS)
    if rev:
        row = lambda b, i: (b * n_s + n_s - 1 - i, 0)
        row3 = lambda b, i: (b * n_s + n_s - 1 - i, 0, 0)
    else:
        row = lambda b, i: (b * n_s + i, 0)
        row3 = lambda b, i: (b * n_s + i, 0, 0)
    full = lambda b, i: (0, 0)
    full3 = lambda b, i: (0, 0, 0)
    par = [pl.BlockSpec((D_SSM, 2 * NRE), full), pl.BlockSpec((2 * NRE, D_SSM), full), pl.BlockSpec((nd, 1, 2 * NRE), full3),
           pl.BlockSpec((S5_GROUP, 2 * NRE), full), pl.BlockSpec((1, D_SSM), full), pl.BlockSpec((D_SSM, D_SSM), full),
           pl.BlockSpec((1, D_SSM), full)]
    par_shapes = [(D_SSM, 2 * NRE), (2 * NRE, D_SSM), (nd, 1, 2 * NRE), (S5_GROUP, 2 * NRE), (1, D_SSM), (D_SSM, D_SSM), (1, D_SSM)]
    return row, row3, par, par_shapes


def _s5_fwd(zs, params, bl, s, name):
    n_s = s // S5_L
    nd = len(S5_SHIFTS)
    f = _s5_chunk_fn()
    row, row3, par, _ = _s5_specs(n_s, False)

    def body(z_ref, b_ref, c_ref, ld_ref, lp_ref, dv_ref, wg_ref, bg_ref, y_ref, car_ref, cs):
        @pl.when(pl.program_id(1) == 0)
        def _():
            cs[...] = jnp.zeros_like(cs)

        c = cs[...]
        car_ref[0] = c
        z = z_ref[...]
        out, ncr, nci = f(z[:, :D_SSM], z[:, D_SSM:], c[:, :NRE], c[:, NRE:], b_ref[...], c_ref[...],
                          [ld_ref[k] for k in range(nd)], lp_ref[...], dv_ref[...], wg_ref[...], bg_ref[...])
        y_ref[...] = out
        cs[:, :NRE] = ncr
        cs[:, NRE:] = nci

    t = bl * s
    return pl.pallas_call(
        body, grid=(bl, n_s),
        in_specs=[pl.BlockSpec((S5_L, 2 * D_SSM), row)] + par,
        out_specs=[pl.BlockSpec((S5_L, D_SSM), row), pl.BlockSpec((1, 1, 2 * NRE), row3)],
        out_shape=[jax.ShapeDtypeStruct((t, D_SSM), F32), jax.ShapeDtypeStruct((bl * n_s, 1, 2 * NRE), F32)],
        scratch_shapes=[pltpu.VMEM((1, 2 * NRE), F32)],
        name=name, compiler_params=_cp("arbitrary", "arbitrary"))(zs, *params)


def _s5_bwd(zs, carries, dy, params, bl, s, name):
    n_s = s // S5_L
    nd = len(S5_SHIFTS)
    f = _s5_chunk_fn()
    row, row3, par, par_shapes = _s5_specs(n_s, True)

    def body(z_ref, car_ref, dy_ref, b_ref, c_ref, ld_ref, lp_ref, dv_ref, wg_ref, bg_ref,
             dz_ref, db_ref, dc_ref, dld_ref, dlp_ref, ddv_ref, dwg_ref, dbg_ref, dcs):
        accs = (db_ref, dc_ref, dld_ref, dlp_ref, ddv_ref, dwg_ref, dbg_ref)

        @pl.when((pl.program_id(0) == 0) & (pl.program_id(1) == 0))
        def _():
            for r in accs:
                r[...] = jnp.zeros_like(r)

        @pl.when(pl.program_id(1) == 0)
        def _():
            dcs[...] = jnp.zeros_like(dcs)

        z = z_ref[...]
        c = car_ref[0]
        _, vjp = jax.vjp(f, z[:, :D_SSM], z[:, D_SSM:], c[:, :NRE], c[:, NRE:], b_ref[...], c_ref[...],
                         [ld_ref[k] for k in range(nd)], lp_ref[...], dv_ref[...], wg_ref[...], bg_ref[...])
        dc = dcs[...]
        du, dgate, dcr, dci, dbb, dcb, dlds, dlpb, ddvb, dwgb, dbgb = vjp((dy_ref[...], dc[:, :NRE], dc[:, NRE:]))
        dz_ref[...] = jnp.concatenate([du, dgate], axis=1)
        dcs[:, :NRE] = dcr
        dcs[:, NRE:] = dci
        db_ref[...] += dbb
        dc_ref[...] += dcb
        for k in range(nd):
            dld_ref[k] += dlds[k]
        dlp_ref[...] += dlpb
        ddv_ref[...] += ddvb
        dwg_ref[...] += dwgb
        dbg_ref[...] += dbgb

    t = bl * s
    return pl.pallas_call(
        body, grid=(bl, n_s),
        in_specs=[pl.BlockSpec((S5_L, 2 * D_SSM), row), pl.BlockSpec((1, 1, 2 * NRE), row3), pl.BlockSpec((S5_L, D_SSM), row)] + par,
        out_specs=[pl.BlockSpec((S5_L, 2 * D_SSM), row)] + par,
        out_shape=[jax.ShapeDtypeStruct((t, 2 * D_SSM), F32)] + [jax.ShapeDtypeStruct(sh, F32) for sh in par_shapes],
        scratch_shapes=[pltpu.VMEM((1, 2 * NRE), F32)],
        name=name, compiler_params=_cp("arbitrary", "arbitrary"))(zs, carries, dy, *params)


def _dn_post(c):
    s = _silu(c)
    parts = []
    for j in range(12):
        xj = s[:, j * DH:(j + 1) * DH]
        if j < 8:
            xj = xj * lax.rsqrt(jnp.sum(xj * xj, axis=-1, keepdims=True) + EPS)
        if j < 4:
            xj = xj * (DH ** -0.5)
        parts.append(xj)
    return jnp.concatenate(parts, axis=1)


def _dn_prep_fwd(zq, conv_w8, bl, s, tb, name):
    n_s = s // tb
    hb = tb // 8
    w3 = 3 * D_DN

    def body(cur_ref, prev_ref, w_ref, o_ref):
        i = pl.program_id(1)
        prev = jnp.where(i > 0, prev_ref[...], 0.0)
        ext = jnp.concatenate([prev, cur_ref[...]], axis=0)
        c = jnp.zeros((tb, w3), F32)
        for k in range(4):
            sh = ext if k == 3 else pltpu.roll(ext, 3 - k, 0)
            c = c + w_ref[k:k + 1, :] * sh[8:, :]
        o_ref[...] = _dn_post(c)

    row = lambda b, i: (b * n_s + i, 0)
    prv = lambda b, i: (jnp.maximum((b * n_s + i) * hb - 1, 0), 0)
    full = lambda b, i: (0, 0)
    t = bl * s
    return pl.pallas_call(
        body, grid=(bl, n_s),
        in_specs=[pl.BlockSpec((tb, w3), row), pl.BlockSpec((8, w3), prv), pl.BlockSpec((8, w3), full)],
        out_specs=pl.BlockSpec((tb, w3), row),
        out_shape=jax.ShapeDtypeStruct((t, w3), F32),
        name=name, compiler_params=_cp("parallel", "parallel"))(zq, zq, conv_w8)


def _dn_prep_bwd(zq, dqkv, conv_w8, bl, s, tb, name):
    n_s = s // tb
    hb = tb // 8
    w3 = 3 * D_DN
    n_blk8 = bl * s // 8

    def body(cur_ref, prev_ref, next_ref, d_ref, dnext_ref, w_ref, dz_ref, dw_ref):
        b, i = pl.program_id(0), pl.program_id(1)

        @pl.when((b == 0) & (i == 0))
        def _():
            dw_ref[...] = jnp.zeros_like(dw_ref)

        prev = jnp.where(i > 0, prev_ref[...], 0.0)
        nxt = jnp.where(i < n_s - 1, next_ref[...], 0.0)
        dnxt = jnp.where(i < n_s - 1, dnext_ref[...], 0.0)
        ext = jnp.concatenate([prev, cur_ref[...], nxt], axis=0)
        shifted = [ext if k == 3 else pltpu.roll(ext, 3 - k, 0) for k in range(4)]
        c2 = jnp.zeros((tb + 8, w3), F32)
        for k in range(4):
            c2 = c2 + w_ref[k:k + 1, :] * shifted[k][8:, :]
        dpost = jnp.concatenate([d_ref[...], dnxt], axis=0)
        _, vjp = jax.vjp(_dn_post, c2)
        (dc2,) = vjp(dpost)
        dz = jnp.zeros((tb, w3), F32)
        for k in range(4):
            up = dc2 if k == 3 else pltpu.roll(dc2, tb + 8 - (3 - k), 0)
            dz = dz + w_ref[k:k + 1, :] * up[:tb, :]
            dw_ref[k:k + 1, :] += jnp.sum(dc2[:tb, :] * shifted[k][8:8 + tb, :], axis=0, keepdims=True)
        dz_ref[...] = dz

    row = lambda b, i: (b * n_s + i, 0)
    prv = lambda b, i: (jnp.maximum((b * n_s + i) * hb - 1, 0), 0)
    nxt = lambda b, i: (jnp.minimum((b * n_s + i + 1) * hb, n_blk8 - 1), 0)
    full = lambda b, i: (0, 0)
    t = bl * s
    return pl.pallas_call(
        body, grid=(bl, n_s),
        in_specs=[pl.BlockSpec((tb, w3), row), pl.BlockSpec((8, w3), prv), pl.BlockSpec((8, w3), nxt),
                  pl.BlockSpec((tb, w3), row), pl.BlockSpec((8, w3), nxt), pl.BlockSpec((8, w3), full)],
        out_specs=[pl.BlockSpec((tb, w3), row), pl.BlockSpec((8, w3), full)],
        out_shape=[jax.ShapeDtypeStruct((t, w3), F32), jax.ShapeDtypeStruct((8, w3), F32)],
        name=name, compiler_params=_cp("arbitrary", "arbitrary"))(zq, zq, zq, dqkv, dqkv, conv_w8)


@jax.custom_vjp
def _unit_lower_inverses(ms):
    c_len = ms[0].shape[0]
    eye = lax.broadcasted_iota(jnp.int32, (c_len, c_len), 0) == lax.broadcasted_iota(jnp.int32, (c_len, c_len), 1)
    ident = jnp.where(eye, 1.0, 0.0)
    ps = ms
    tinvs = [ident - m for m in ms]
    for _ in range(c_len.bit_length() - 2):
        ps = [_dg3(p, p, 1, 0) for p in ps]
        tinvs = [t + _dg3(t, p, 1, 0) for t, p in zip(tinvs, ps)]
    return tinvs


def _unit_lower_inverses_fwd(ms):
    tinvs = _unit_lower_inverses(ms)
    return tinvs, tinvs


def _unit_lower_inverses_bwd(tinvs, gs):
    return ([-_dg3(_dg3(t, g, 0, 0), t, 1, 1) for t, g in zip(tinvs, gs)],)


_unit_lower_inverses.defvjp(_unit_lower_inverses_fwd, _unit_lower_inverses_bwd)


def _dn_chunk_fn():
    c_len = DN_C

    def f(qkv, zab, zg, s0, s1, s2, s3, alog_e, dt_e, ng):
        states = (s0, s1, s2, s3)
        r = lax.broadcasted_iota(jnp.int32, (c_len, c_len), 0)
        c = lax.broadcasted_iota(jnp.int32, (c_len, c_len), 1)
        causal, strict = r >= c, r > c
        tril = jnp.where(causal, 1.0, 0.0)
        rr = lax.broadcasted_iota(jnp.int32, (LANES, D_DN), 0)
        cc = lax.broadcasted_iota(jnp.int32, (LANES, D_DN), 1)
        e_a = jnp.where((cc >= rr * DH) & (cc < rr * DH + DH) & (rr < H), 1.0, 0.0)
        e_b = jnp.where((cc >= (rr - H) * DH) & (cc < (rr - H) * DH + DH) & (rr >= H) & (rr < 2 * H), 1.0, 0.0)
        a_e = _sel_r(zab, e_a)
        b_e = _sel_r(zab, e_b)
        beta = jax.nn.sigmoid(b_e)
        g = -jnp.exp(alog_e) * jax.nn.softplus(a_e + dt_e)
        gc = _sel_l(tril, g)
        glast = jnp.sum(g, axis=0, keepdims=True)
        eg = jnp.exp(gc)
        ekd = jnp.exp(glast - gc)
        dl = jnp.exp(glast)
        heads = range(H)
        sls = [slice(h * DH, (h + 1) * DH) for h in heads]
        qs = [qkv[:, h * DH:(h + 1) * DH] for h in heads]
        ks = [qkv[:, D_DN + h * DH:D_DN + (h + 1) * DH] for h in heads]
        vs = [qkv[:, 2 * D_DN + h * DH:2 * D_DN + (h + 1) * DH] for h in heads]
        ccols = [gc[:, sl] for sl in sls]
        decs = [jnp.where(causal, jnp.exp(jnp.where(causal, cl - jnp.transpose(cl), 0.0)), 0.0) for cl in ccols]
        kbs = [k * beta[:, sl] for k, sl in zip(ks, sls)]
        ms = [jnp.where(strict, _mm_nt(kb, k) * dec, 0.0) for kb, k, dec in zip(kbs, ks, decs)]
        tinvs = _unit_lower_inverses(ms)
        sols = [_dot3(t, jnp.concatenate([v * beta[:, sl], kb * eg[:, sl]], axis=1))
                for t, v, kb, sl in zip(tinvs, vs, kbs, sls)]
        atts = [_mm_nt(q, k) * dec for q, k, dec in zip(qs, ks, decs)]
        vnews = [sol[:, :DH] - _mm(sol[:, DH:], st) for sol, st in zip(sols, states)]
        os_ = [_mm(q * eg[:, sl], st) + _mm(att, vn) for q, sl, st, att, vn in zip(qs, sls, states, atts, vnews)]
        new_states = [st * dl[:, sl] + _mm_tn(k * ekd[:, sl], vn) for st, sl, k, vn in zip(states, sls, ks, vnews)]
        ys = [_rms(o, ng) * _silu(zg[:, sl]) for o, sl in zip(os_, sls)]
        return (jnp.concatenate(ys, axis=1), *new_states)

    return f


def _dn_scan_specs(n_c, rev):
    if rev:
        row = lambda b, i: (b * n_c + n_c - 1 - i, 0)
        row4 = lambda b, i: (b * n_c + n_c - 1 - i, 0, 0, 0)
    else:
        row = lambda b, i: (b * n_c + i, 0)
        row4 = lambda b, i: (b * n_c + i, 0, 0, 0)
    full = lambda b, i: (0, 0)
    par = [pl.BlockSpec((1, D_DN), full), pl.BlockSpec((1, D_DN), full), pl.BlockSpec((1, DH), full)]
    par_shapes = [(1, D_DN), (1, D_DN), (1, DH)]
    return row, row4, par, par_shapes


def _dn_scan_fwd(qkv, zab, zg, params, bl, s, name):
    n_c = s // DN_C
    f = _dn_chunk_fn()
    row, row4, par, _ = _dn_scan_specs(n_c, False)

    def body(q_ref, ab_ref, zg_ref, al_ref, dt_ref, ng_ref, y_ref, st_ref, ssc):
        @pl.when(pl.program_id(1) == 0)
        def _():
            ssc[...] = jnp.zeros_like(ssc)

        sts = [ssc[h] for h in range(H)]
        for h in range(H):
            st_ref[0, h] = sts[h]
        outs = f(q_ref[...], ab_ref[...], zg_ref[...], *sts, al_ref[...], dt_ref[...], ng_ref[...])
        y_ref[...] = outs[0]
        for h in range(H):
            ssc[h] = outs[1 + h]

    t = bl * s
    return pl.pallas_call(
        body, grid=(bl, n_c),
        in_specs=[pl.BlockSpec((DN_C, 3 * D_DN), row), pl.BlockSpec((DN_C, LANES), row), pl.BlockSpec((DN_C, D_DN), row)] + par,
        out_specs=[pl.BlockSpec((DN_C, D_DN), row), pl.BlockSpec((1, H, DH, DH), row4)],
        out_shape=[jax.ShapeDtypeStruct((t, D_DN), F32), jax.ShapeDtypeStruct((bl * n_c, H, DH, DH), F32)],
        scratch_shapes=[pltpu.VMEM((H, DH, DH), F32)],
        name=name, compiler_params=_cp("arbitrary", "arbitrary"))(qkv, zab, zg, *params)


def _dn_scan_bwd(qkv, zab, zg, states, dy, params, bl, s, name):
    n_c = s // DN_C
    f = _dn_chunk_fn()
    row, row4, par, par_shapes = _dn_scan_specs(n_c, True)

    def body(q_ref, ab_ref, zg_ref, st_ref, dy_ref, al_ref, dt_ref, ng_ref,
             dq_ref, dab_ref, dzg_ref, dal_ref, ddt_ref, dng_ref, dsc):
        @pl.when((pl.program_id(0) == 0) & (pl.program_id(1) == 0))
        def _():
            for r in (dal_ref, ddt_ref, dng_ref):
                r[...] = jnp.zeros_like(r)

        @pl.when(pl.program_id(1) == 0)
        def _():
            dsc[...] = jnp.zeros_like(dsc)

        sts = [st_ref[0, h] for h in range(H)]
        _, vjp = jax.vjp(f, q_ref[...], ab_ref[...], zg_ref[...], *sts, al_ref[...], dt_ref[...], ng_ref[...])
        cts = vjp((dy_ref[...], *[dsc[h] for h in range(H)]))
        dq_ref[...] = cts[0]
        dab_ref[...] = cts[1]
        dzg_ref[...] = cts[2]
        for h in range(H):
            dsc[h] = cts[3 + h]
        dal_ref[...] += cts[3 + H]
        ddt_ref[...] += cts[4 + H]
        dng_ref[...] += cts[5 + H]

    t = bl * s
    return pl.pallas_call(
        body, grid=(bl, n_c),
        in_specs=[pl.BlockSpec((DN_C, 3 * D_DN), row), pl.BlockSpec((DN_C, LANES), row), pl.BlockSpec((DN_C, D_DN), row),
                  pl.BlockSpec((1, H, DH, DH), row4), pl.BlockSpec((DN_C, D_DN), row)] + par,
        out_specs=[pl.BlockSpec((DN_C, 3 * D_DN), row), pl.BlockSpec((DN_C, LANES), row), pl.BlockSpec((DN_C, D_DN), row)] + par,
        out_shape=[jax.ShapeDtypeStruct((t, 3 * D_DN), F32), jax.ShapeDtypeStruct((t, LANES), F32),
                   jax.ShapeDtypeStruct((t, D_DN), F32)] + [jax.ShapeDtypeStruct(sh, F32) for sh in par_shapes],
        scratch_shapes=[pltpu.VMEM((H, DH, DH), F32)],
        name=name, compiler_params=_cp("arbitrary", "arbitrary"))(qkv, zab, zg, states, dy, *params)


def _sg_fn(n_chunk):
    def f(z, lng, lnb, w, bsp_t):
        u = jax.nn.gelu(z[:, :D_SG])
        v = jax.nn.gelu(z[:, D_SG:2 * D_SG])
        gate = z[:, 2 * D_SG:]
        xc = v - jnp.mean(v, axis=-1, keepdims=True)
        vn = xc * lax.rsqrt(jnp.mean(xc * xc, axis=-1, keepdims=True) + EPS) * lng + lnb
        r = lax.broadcasted_iota(jnp.int32, (SG_C, SG_C), 0)
        c = lax.broadcasted_iota(jnp.int32, (SG_C, SG_C), 1)
        causal = r >= c
        first_half = c < SG_C // 2
        rr = lax.broadcasted_iota(jnp.int32, (LANES, D_SG), 0)
        cc = lax.broadcasted_iota(jnp.int32, (LANES, D_SG), 1)
        expand = jnp.where((cc >= rr * 64) & (cc < rr * 64 + 64) & (rr < 4), 1.0, 0.0)
        bias = _sel_r(bsp_t, expand)
        wm = [jnp.where(causal, w[h], 0.0) for h in range(4)]
        rows = []
        for ci in range(n_chunk):
            vc = vn[ci * SG_C:(ci + 1) * SG_C]
            pairs = []
            for pr in range(2):
                vp = vc[:, pr * LANES:(pr + 1) * LANES]
                pairs.append(jnp.where(first_half, _mm(wm[2 * pr], vp), _mm(wm[2 * pr + 1], vp)))
            rows.append(jnp.concatenate(pairs, axis=1) + bias)
        sp = jnp.concatenate(rows, axis=0) if n_chunk > 1 else rows[0]
        return u * sp * _silu(gate)

    return f


def _sg_specs():
    full = lambda i: (0, 0)
    full3 = lambda i: (0, 0, 0)
    par = [pl.BlockSpec((1, D_SG), full), pl.BlockSpec((1, D_SG), full), pl.BlockSpec((4, SG_C, SG_C), full3),
           pl.BlockSpec((SG_C, LANES), full)]
    par_shapes = [(1, D_SG), (1, D_SG), (4, SG_C, SG_C), (SG_C, LANES)]
    return par, par_shapes


def _sg_fwd(zsg, params, tb, name):
    t = zsg.shape[0]
    f = _sg_fn(tb // SG_C)
    par, _ = _sg_specs()

    def body(z_ref, g_ref, b_ref, w_ref, bs_ref, y_ref):
        y_ref[...] = f(z_ref[...], g_ref[...], b_ref[...], w_ref[...], bs_ref[...])

    row = lambda i: (i, 0)
    return pl.pallas_call(
        body, grid=(t // tb,), in_specs=[pl.BlockSpec((tb, 3 * D_SG), row)] + par,
        out_specs=pl.BlockSpec((tb, D_SG), row), out_shape=jax.ShapeDtypeStruct((t, D_SG), F32),
        name=name, compiler_params=_cp("parallel"))(zsg, *params)


def _sg_bwd(zsg, dy, params, tb, name):
    t = zsg.shape[0]
    f = _sg_fn(tb // SG_C)
    par, par_shapes = _sg_specs()

    def body(z_ref, dy_ref, g_ref, b_ref, w_ref, bs_ref, dz_ref, dg_ref, db_ref, dw_ref, dbs_ref):
        accs = (dg_ref, db_ref, dw_ref, dbs_ref)

        @pl.when(pl.program_id(0) == 0)
        def _():
            for r in accs:
                r[...] = jnp.zeros_like(r)

        _, vjp = jax.vjp(f, z_ref[...], g_ref[...], b_ref[...], w_ref[...], bs_ref[...])
        cts = vjp(dy_ref[...])
        dz_ref[...] = cts[0]
        for r, ct in zip(accs, cts[1:]):
            r[...] += ct

    row = lambda i: (i, 0)
    return pl.pallas_call(
        body, grid=(t // tb,), in_specs=[pl.BlockSpec((tb, 3 * D_SG), row), pl.BlockSpec((tb, D_SG), row)] + par,
        out_specs=[pl.BlockSpec((tb, 3 * D_SG), row)] + par,
        out_shape=[jax.ShapeDtypeStruct((t, 3 * D_SG), F32)] + [jax.ShapeDtypeStruct(sh, F32) for sh in par_shapes],
        name=name, compiler_params=_cp("arbitrary"))(zsg, dy, *params)


def _out_fwd(x, ys, p, w_out, pg, w_gate, w_ple, tb, name):
    t = x.shape[0]

    def body(x_ref, y0, y1, y2, p_ref, wo_ref, pg_ref, wg_ref, wp_ref, o_ref):
        y = jnp.concatenate([y0[...], y1[...], y2[...]], axis=1).astype(BF16)
        x1 = x_ref[...] + jnp.dot(y, wo_ref[...], preferred_element_type=F32)
        hn = _rms(x1, pg_ref[...]).astype(BF16)
        gate = jax.nn.sigmoid(jnp.dot(hn, wg_ref[...], preferred_element_type=F32))
        pp = jnp.dot(p_ref[...].astype(BF16), wp_ref[...], preferred_element_type=F32)
        o_ref[...] = x1 + gate * pp

    row = lambda i: (i, 0)
    full = lambda i: (0, 0)
    return pl.pallas_call(
        body, grid=(t // tb,),
        in_specs=[pl.BlockSpec((tb, D), row), pl.BlockSpec((tb, D_SSM), row), pl.BlockSpec((tb, D_DN), row),
                  pl.BlockSpec((tb, D_SG), row), pl.BlockSpec((tb, D_PLE), row), pl.BlockSpec((D, D), full),
                  pl.BlockSpec((1, D), full), pl.BlockSpec((D, D), full), pl.BlockSpec((D_PLE, D), full)],
        out_specs=pl.BlockSpec((tb, D), row), out_shape=jax.ShapeDtypeStruct((t, D), F32),
        name=name, compiler_params=_cp("parallel"))(x, *ys, p, w_out, pg, w_gate, w_ple)


def _out_bwd(x, ys, p, dx2, w_out, pg, w_gate, w_ple, tb, name):
    t = x.shape[0]

    def body(x_ref, y0, y1, y2, p_ref, d_ref, wo_ref, pg_ref, wg_ref, wp_ref,
             dx_ref, dy0, dy1, dy2, dwo_ref, dpg_ref, dwg_ref, dwp_ref):
        accs = (dwo_ref, dpg_ref, dwg_ref, dwp_ref)

        @pl.when(pl.program_id(0) == 0)
        def _():
            for r in accs:
                r[...] = jnp.zeros_like(r)

        y = jnp.concatenate([y0[...], y1[...], y2[...]], axis=1).astype(BF16)
        x1 = x_ref[...] + jnp.dot(y, wo_ref[...], preferred_element_type=F32)
        hn, rms_vjp = jax.vjp(_rms, x1, pg_ref[...])
        hb = hn.astype(BF16)
        gate = jax.nn.sigmoid(jnp.dot(hb, wg_ref[...], preferred_element_type=F32))
        pb = p_ref[...].astype(BF16)
        pp = jnp.dot(pb, wp_ref[...], preferred_element_type=F32)
        d2 = d_ref[...]
        dpp = (d2 * gate).astype(BF16)
        dlog = (d2 * pp * gate * (1.0 - gate)).astype(BF16)
        dwp_ref[...] += _dg(pb, dpp, 0, 0)
        dwg_ref[...] += _dg(hb, dlog, 0, 0)
        dx1_n, dpg = rms_vjp(_dg(dlog, wg_ref[...], 1, 1))
        dpg_ref[...] += dpg
        dx1 = d2 + dx1_n
        dx_ref[...] = dx1
        db = dx1.astype(BF16)
        dwo_ref[...] += _dg(y, db, 0, 0)
        dy = _dg(db, wo_ref[...], 1, 1)
        dy0[...] = dy[:, :D_SSM]
        dy1[...] = dy[:, D_SSM:D_SSM + D_DN]
        dy2[...] = dy[:, D_SSM + D_DN:]

    row = lambda i: (i, 0)
    full = lambda i: (0, 0)
    acts = [pl.BlockSpec((tb, D), row), pl.BlockSpec((tb, D_SSM), row), pl.BlockSpec((tb, D_DN), row), pl.BlockSpec((tb, D_SG), row)]
    wts = [pl.BlockSpec((D, D), full), pl.BlockSpec((1, D), full), pl.BlockSpec((D, D), full), pl.BlockSpec((D_PLE, D), full)]
    return pl.pallas_call(
        body, grid=(t // tb,),
        in_specs=acts + [pl.BlockSpec((tb, D_PLE), row), pl.BlockSpec((tb, D), row)] + wts,
        out_specs=acts + wts,
        out_shape=[jax.ShapeDtypeStruct((t, n), F32) for n in (D, D_SSM, D_DN, D_SG)]
        + [jax.ShapeDtypeStruct(sh, F32) for sh in ((D, D), (1, D), (D, D), (D_PLE, D))],
        name=name, compiler_params=_cp("arbitrary"))(x, *ys, p, dx2, w_out, pg, w_gate, w_ple)


def _loss_head(x, fg, target, tb, name):
    t = x.shape[0]

    def body(x_ref, g_ref, t_ref, dx_ref, dg_ref, loss_ref):
        @pl.when(pl.program_id(0) == 0)
        def _():
            dg_ref[...] = jnp.zeros_like(dg_ref)
            loss_ref[...] = jnp.zeros_like(loss_ref)

        y, vjp = jax.vjp(_rms, x_ref[...], g_ref[...])
        err = y - t_ref[...]
        loss_ref[...] += jnp.zeros_like(loss_ref) + 0.5 * jnp.sum(err * err) / D
        dx, dg = vjp(err / D)
        dx_ref[...] = dx
        dg_ref[...] += dg

    row = lambda i: (i, 0)
    full = lambda i: (0, 0)
    return pl.pallas_call(
        body, grid=(t // tb,),
        in_specs=[pl.BlockSpec((tb, D), row), pl.BlockSpec((1, D), full), pl.BlockSpec((tb, D), row)],
        out_specs=[pl.BlockSpec((tb, D), row), pl.BlockSpec((1, D), full), pl.BlockSpec((1, LANES), full)],
        out_shape=[jax.ShapeDtypeStruct((t, D), F32), jax.ShapeDtypeStruct((1, D), F32), jax.ShapeDtypeStruct((1, LANES), F32)],
        name=name, compiler_params=_cp("arbitrary"))(x, fg, target)


def _hbm_specs(n):
    return [pl.BlockSpec(memory_space=pl.ANY)] * n


def _all_gather(blocks, name):
    n = len(blocks)

    def body(*refs):
        ins, outs = refs[:n], refs[n:2 * n]
        send_sems, recv_sems, local_sems = refs[2 * n:]
        x, y, c = lax.axis_index("x"), lax.axis_index("y"), lax.axis_index("c")
        me, sibling = (x, y, c), (x, y, 1 - c)
        chips = [(1 - x, y), (x, 1 - y), (1 - x, 1 - y)]

        def slot(a, px, py, pc):
            return outs[a].at[4 * px + 2 * py + pc]

        def copy(a, k, blk, to, src=None):
            return pltpu.make_async_remote_copy(
                src_ref=slot(a, *blk) if src is None else src, dst_ref=slot(a, *blk),
                send_sem=send_sems.at[7 * a + k], recv_sem=recv_sems.at[7 * a + k],
                device_id=to, device_id_type=pl.DeviceIdType.MESH)

        mines = [pltpu.make_async_copy(ins[a], slot(a, *me), local_sems.at[a]) for a in range(n)]
        for cp in mines:
            cp.start()
        first = []
        for a in range(n):
            first.append(copy(a, 0, me, sibling, src=ins[a]))
            first += [copy(a, 1 + j, me, (*chip, c), src=ins[a]) for j, chip in enumerate(chips)]
        for cp in first:
            cp.start()
        passed = []
        for j, chip in enumerate(chips):
            for a in range(n):
                copy(a, 1 + j, (*chip, c), me).wait_recv()
                onward = copy(a, 4 + j, (*chip, c), sibling)
                onward.start()
                passed.append(onward)
        for a in range(n):
            copy(a, 0, sibling, me).wait_recv()
        for j, chip in enumerate(chips):
            for a in range(n):
                copy(a, 4 + j, (*chip, 1 - c), me).wait_recv()
        for cp in first + passed:
            cp.wait_send()
        for cp in mines:
            cp.wait()

    return pl.pallas_call(
        body, out_shape=[jax.ShapeDtypeStruct((N_DEV, *b.shape), b.dtype) for b in blocks],
        in_specs=_hbm_specs(n), out_specs=_hbm_specs(n),
        scratch_shapes=[pltpu.SemaphoreType.DMA((7 * n,)), pltpu.SemaphoreType.DMA((7 * n,)), pltpu.SemaphoreType.DMA((n,))],
        name=name)(*blocks)


def _pair_exchange(gs, name):
    n = len(gs)

    def body(*refs):
        ins, recvs = refs[:n], refs[n:2 * n]
        send_sems, recv_sems = refs[2 * n:]
        x, y, c = lax.axis_index("x"), lax.axis_index("y"), lax.axis_index("c")
        remote = [pltpu.make_async_remote_copy(
            src_ref=ins[a], dst_ref=recvs[a], send_sem=send_sems.at[a], recv_sem=recv_sems.at[a],
            device_id=(x, y, 1 - c), device_id_type=pl.DeviceIdType.MESH) for a in range(n)]
        for cp in remote:
            cp.start()
        for cp in remote:
            cp.wait_send()
            cp.wait_recv()

    return pl.pallas_call(
        body, out_shape=[jax.ShapeDtypeStruct(g.shape, g.dtype) for g in gs], in_specs=_hbm_specs(n), out_specs=_hbm_specs(n),
        scratch_shapes=[pltpu.SemaphoreType.DMA((n,)), pltpu.SemaphoreType.DMA((n,))],
        name=name)(*gs)


def _chip_exchange(ps, name):
    n = len(ps)

    def body(*refs):
        ins, outs = refs[:n], refs[n:2 * n]
        send_sems, recv_sems, local_sems = refs[2 * n:]
        x, y, c = lax.axis_index("x"), lax.axis_index("y"), lax.axis_index("c")
        my_chip = 2 * x + y
        local = [pltpu.make_async_copy(ins[a].at[my_chip], outs[a].at[my_chip], local_sems.at[a]) for a in range(n)]
        remote = []
        for j in range(1, 4):
            px = 1 - x if j & 2 else x
            py = 1 - y if j & 1 else y
            for a in range(n):
                remote.append(pltpu.make_async_remote_copy(
                    src_ref=ins[a].at[2 * px + py], dst_ref=outs[a].at[my_chip],
                    send_sem=send_sems.at[3 * a + j - 1], recv_sem=recv_sems.at[3 * a + j - 1],
                    device_id=(px, py, c), device_id_type=pl.DeviceIdType.MESH))
        for cp in local + remote:
            cp.start()
        for cp in remote:
            cp.wait_send()
            cp.wait_recv()
        for cp in local:
            cp.wait()

    return pl.pallas_call(
        body, out_shape=[jax.ShapeDtypeStruct(q.shape, q.dtype) for q in ps], in_specs=_hbm_specs(n), out_specs=_hbm_specs(n),
        scratch_shapes=[pltpu.SemaphoreType.DMA((3 * n,)), pltpu.SemaphoreType.DMA((3 * n,)), pltpu.SemaphoreType.DMA((n,))],
        name=name)(*ps)


def _row_block(rows, bytes_per_row):
    best = None
    for rb in range(16, rows + 1, 16):
        if rows % rb == 0 and rb * bytes_per_row <= ELEMENTWISE_STEP_BYTES:
            best = rb
    return rows if best is None else best


def _add_pair(own, recv, name):
    shape = own.shape
    last = shape[-1]
    rows = own.size // last
    rb = _row_block(rows, 3 * 4 * (-(-last // LANES) * LANES))

    def body(a_ref, b_ref, o_ref):
        o_ref[...] = (a_ref[...].astype(F32) + b_ref[...].astype(F32)).astype(o_ref.dtype)

    row = lambda i: (i, 0)
    out = pl.pallas_call(
        body, grid=(rows // rb,), in_specs=[pl.BlockSpec((rb, last), row)] * 2, out_specs=pl.BlockSpec((rb, last), row),
        out_shape=jax.ShapeDtypeStruct((rows, last), own.dtype), name=name,
        compiler_params=_cp("parallel"))(own.reshape(rows, last), recv.reshape(rows, last))
    return out.reshape(shape)


def _sum_adamw(gk, w, m, v, name):
    shape = w.shape
    n_part = gk.shape[0]
    last = shape[-1]
    rows = w.size // last
    rb = _row_block(rows, (n_part + 7) * 4 * (-(-last // LANES) * LANES))

    def body(g_ref, w_ref, m_ref, v_ref, go_ref, d_ref, mo_ref, vo_ref):
        g = g_ref[0].astype(F32)
        for k in range(1, n_part):
            g = g + g_ref[k].astype(F32)
        mn = ADAM_B1 * m_ref[...] + (1.0 - ADAM_B1) * g
        vn = ADAM_B2 * v_ref[...] + (1.0 - ADAM_B2) * jnp.square(g)
        m_hat = mn / (1.0 - ADAM_B1 ** ADAM_STEP)
        v_hat = vn / (1.0 - ADAM_B2 ** ADAM_STEP)
        go_ref[...] = g
        d_ref[...] = -ADAM_LR * (m_hat / (jnp.sqrt(v_hat) + ADAM_EPS) + ADAM_WD * w_ref[...])
        mo_ref[...] = mn
        vo_ref[...] = vn

    row = lambda i: (i, 0)
    outs = pl.pallas_call(
        body, grid=(rows // rb,),
        in_specs=[pl.BlockSpec((n_part, rb, last), lambda i: (0, i, 0))] + [pl.BlockSpec((rb, last), row)] * 3,
        out_specs=[pl.BlockSpec((rb, last), row)] * 4,
        out_shape=[jax.ShapeDtypeStruct((rows, last), F32)] * 4,
        name=name, compiler_params=_cp("parallel"))(gk.reshape(n_part, rows, last), *[a.reshape(rows, last) for a in (w, m, v)])
    return [o.reshape(shape) for o in outs]


def _seg_rows(shape):
    n = 1
    for d in shape:
        n *= d
    return -(-n // (8 * LANES)) * 8


def _pack(arrs):
    segs = []
    for a in arrs:
        r = _seg_rows(a.shape)
        segs.append(jnp.pad(a.reshape(-1).astype(F32), (0, r * LANES - a.size)).reshape(r, LANES))
    rows = sum(s.shape[0] for s in segs)
    total = -(-rows // PACK_ROWS) * PACK_ROWS
    if total > rows:
        segs.append(jnp.zeros((total - rows, LANES), F32))
    return jnp.concatenate(segs, axis=0)


def _unpack(pack, shapes):
    out, off = [], 0
    for sh in shapes:
        r = _seg_rows(sh)
        n = 1
        for d in sh:
            n *= d
        out.append(pack[off:off + r].reshape(-1)[:n].reshape(sh))
        off += r
    return out


def _to_dest_blocks(full, axis, dtype):
    sh = list(full.shape)
    sh[axis:axis + 1] = [N_DEV // 2, 2, sh[axis] // N_DEV]
    return jnp.moveaxis(full.reshape(sh), (axis, axis + 1), (1, 0)).astype(dtype)


def _from_gathered(g, axis):
    m = jnp.moveaxis(g, 0, axis)
    sh = list(m.shape)
    sh[axis:axis + 2] = [sh[axis] * sh[axis + 1]]
    return m.reshape(sh)


def _reorder_w_in(w):
    return jnp.concatenate([w[:, :2048], w[:, 2056:3336], w[:, 2048:2056], jnp.zeros((D, ZW - 3336), w.dtype)], axis=1)


def _restore_dw_in(dw):
    return jnp.concatenate([dw[:, :2048], dw[:, 3328:3336], dw[:, 2048:3328]], axis=1)


def _local_step(x, p, wts, target):
    bl, s, _ = x.shape
    t = bl * s
    depth = p.shape[0]
    tb, sg_tb, prep_tb = TB, SG_TB, PREP_TB
    xs = [x.reshape(t, D)]
    saved = []
    for i in range(depth):
        li = f"l{i}"
        ng = wts['norm_g'][i].reshape(1, D)
        w_in = _reorder_w_in(wts['w_in'][i]).astype(BF16)
        s5_par_in = (wts['ssm_a_re'][i], wts['ssm_a_im'][i], wts['ssm_b_re'][i], wts['ssm_b_im'][i],
                     wts['ssm_c_re'][i], wts['ssm_c_im'][i], wts['ssm_d'][i], wts['ssm_log_step'][i])
        tabs, tab_vjp = jax.vjp(_s5_tables, *s5_par_in)
        s5_par = (*tabs, wts['ssm_w_glu'][i], wts['ssm_b_glu'][i].reshape(1, D_SSM))
        conv8 = jnp.pad(wts['dn_conv_w'][i], ((0, 4), (0, 0)))
        dn_par = (jnp.repeat(wts['dn_a_log'][i], DH).reshape(1, D_DN), jnp.repeat(wts['dn_dt_bias'][i], DH).reshape(1, D_DN),
                  wts['dn_norm_g'][i].reshape(1, DH))
        sg_par = (wts['sg_ln_g'][i].reshape(1, D_SG), wts['sg_ln_b'][i].reshape(1, D_SG), wts['sg_w'][i],
                  jnp.pad(jnp.transpose(wts['sg_b'][i]), ((0, 0), (0, LANES - 4))))
        out_par = (wts['w_out'][i].astype(BF16), wts['ple_norm_g'][i].reshape(1, D), wts['w_ple_gate'][i].astype(BF16),
                   wts['w_ple'][i].astype(BF16))
        pi = p[i].reshape(t, D_PLE)

        z_ssm, z_qkv, z_gdn, z_sg, z_ab = _in_proj_fwd(xs[i], ng, w_in, tb, f"in_proj_fwd_{li}")
        y_ssm, carries = _s5_fwd(z_ssm, s5_par, bl, s, f"s5_fwd_{li}")
        qkvn = _dn_prep_fwd(z_qkv, conv8, bl, s, prep_tb, f"dn_prep_fwd_{li}")
        y_dn, states = _dn_scan_fwd(qkvn, z_ab, z_gdn, dn_par, bl, s, f"dn_scan_fwd_{li}")
        y_sg = _sg_fwd(z_sg, sg_par, sg_tb, f"sg_fwd_{li}")
        ys = (y_ssm, y_dn, y_sg)
        xs.append(_out_fwd(xs[i], ys, pi, *out_par, tb, f"out_fwd_{li}"))
        saved.append(dict(ng=ng, w_in=w_in, tab_vjp=tab_vjp, s5_par=s5_par, conv8=conv8, dn_par=dn_par, sg_par=sg_par,
                          out_par=out_par, pi=pi, z=(z_ssm, z_qkv, z_gdn, z_sg, z_ab), carries=carries, qkvn=qkvn,
                          states=states, ys=ys))

    dx, dfg, loss_vec = _loss_head(xs[depth], wts['final_norm_g'].reshape(1, D), target.reshape(t, D), tb, "loss_head")
    grads = {n: [None] * depth for n in WEIGHTS if n != 'final_norm_g'}
    grads['final_norm_g'] = dfg.reshape(D)
    for i in reversed(range(depth)):
        li = f"l{i}"
        sv = saved[i]
        z_ssm, z_qkv, z_gdn, z_sg, z_ab = sv['z']
        dx_res, dy_ssm, dy_dn, dy_sg, dwo, dpg, dwg, dwp = _out_bwd(xs[i], sv['ys'], sv['pi'], dx, *sv['out_par'], tb, f"out_bwd_{li}")
        dz_sg, dlng, dlnb, dsgw, dbsp = _sg_bwd(z_sg, dy_sg, sv['sg_par'], sg_tb, f"sg_bwd_{li}")
        dqkvn, dz_ab, dz_gdn, dal, ddt, dng = _dn_scan_bwd(sv['qkvn'], z_ab, z_gdn, sv['states'], dy_dn, sv['dn_par'], bl, s,
                                                          f"dn_scan_bwd_{li}")
        dz_qkv, dconv = _dn_prep_bwd(z_qkv, dqkvn, sv['conv8'], bl, s, prep_tb, f"dn_prep_bwd_{li}")
        dz_ssm, dbb, dcb, dld, dlp, ddv, dwglu, dbglu = _s5_bwd(z_ssm, sv['carries'], dy_ssm, sv['s5_par'], bl, s, f"s5_bwd_{li}")
        dzs = (dz_ssm, dz_qkv, dz_gdn, dz_sg, dz_ab)
        dx, dnorm = _in_proj_bwd_dx(xs[i], sv['ng'], sv['w_in'], dzs, dx_res, tb, f"in_proj_bwd_dx_{li}")
        dws = _in_proj_bwd_dw(xs[i], sv['ng'], dzs, tb, f"in_proj_bwd_dw_{li}")
        ds5 = sv['tab_vjp']((dbb, dcb, dld, dlp, ddv))
        for n, gval in zip(('ssm_a_re', 'ssm_a_im', 'ssm_b_re', 'ssm_b_im', 'ssm_c_re', 'ssm_c_im', 'ssm_d', 'ssm_log_step'), ds5):
            grads[n][i] = gval
        grads['norm_g'][i] = dnorm.reshape(D)
        grads['w_in'][i] = _restore_dw_in(jnp.concatenate(dws, axis=1))
        grads['ssm_w_glu'][i] = dwglu
        grads['ssm_b_glu'][i] = dbglu.reshape(D_SSM)
        grads['dn_conv_w'][i] = dconv[:4]
        grads['dn_a_log'][i] = dal.reshape(H, DH).sum(axis=1)
        grads['dn_dt_bias'][i] = ddt.reshape(H, DH).sum(axis=1)
        grads['dn_norm_g'][i] = dng.reshape(DH)
        grads['sg_ln_g'][i] = dlng.reshape(D_SG)
        grads['sg_ln_b'][i] = dlnb.reshape(D_SG)
        grads['sg_w'][i] = dsgw
        grads['sg_b'][i] = jnp.transpose(dbsp[:, :4])
        grads['w_out'][i] = dwo
        grads['ple_norm_g'][i] = dpg.reshape(D)
        grads['w_ple_gate'][i] = dwg
        grads['w_ple'][i] = dwp
    grads = {n: (g if n == 'final_norm_g' else jnp.stack(g)) for n, g in grads.items()}
    return loss_vec[0, 0], dx.reshape(bl, s, D), grads


def kernel(x, p, norm_g, w_in, ssm_a_re, ssm_a_im, ssm_b_re, ssm_b_im, ssm_c_re, ssm_c_im, ssm_d, ssm_log_step, ssm_w_glu, ssm_b_glu, dn_conv_w, dn_a_log, dn_dt_bias, dn_norm_g, sg_ln_g, sg_ln_b, sg_w, sg_b, w_out, ple_norm_g, w_ple_gate, w_ple, final_norm_g, loss_target, m_norm_g, m_w_in, m_ssm_a_re, m_ssm_a_im, m_ssm_b_re, m_ssm_b_im, m_ssm_c_re, m_ssm_c_im, m_ssm_d, m_ssm_log_step, m_ssm_w_glu, m_ssm_b_glu, m_dn_conv_w, m_dn_a_log, m_dn_dt_bias, m_dn_norm_g, m_sg_ln_g, m_sg_ln_b, m_sg_w, m_sg_b, m_w_out, m_ple_norm_g, m_w_ple_gate, m_w_ple, m_final_norm_g, v_norm_g, v_w_in, v_ssm_a_re, v_ssm_a_im, v_ssm_b_re, v_ssm_b_im, v_ssm_c_re, v_ssm_c_im, v_ssm_d, v_ssm_log_step, v_ssm_w_glu, v_ssm_b_glu, v_dn_conv_w, v_dn_a_log, v_dn_dt_bias, v_dn_norm_g, v_sg_ln_g, v_sg_ln_b, v_sg_w, v_sg_b, v_w_out, v_ple_norm_g, v_w_ple_gate, v_w_ple, v_final_norm_g):
    w_loc = dict(zip(WEIGHTS, (norm_g, w_in, ssm_a_re, ssm_a_im, ssm_b_re, ssm_b_im, ssm_c_re, ssm_c_im, ssm_d, ssm_log_step,
                               ssm_w_glu, ssm_b_glu, dn_conv_w, dn_a_log, dn_dt_bias, dn_norm_g, sg_ln_g, sg_ln_b, sg_w, sg_b,
                               w_out, ple_norm_g, w_ple_gate, w_ple, final_norm_g)))
    m_loc = dict(zip(WEIGHTS, (m_norm_g, m_w_in, m_ssm_a_re, m_ssm_a_im, m_ssm_b_re, m_ssm_b_im, m_ssm_c_re, m_ssm_c_im, m_ssm_d,
                               m_ssm_log_step, m_ssm_w_glu, m_ssm_b_glu, m_dn_conv_w, m_dn_a_log, m_dn_dt_bias, m_dn_norm_g,
                               m_sg_ln_g, m_sg_ln_b, m_sg_w, m_sg_b, m_w_out, m_ple_norm_g, m_w_ple_gate, m_w_ple, m_final_norm_g)))
    v_loc = dict(zip(WEIGHTS, (v_norm_g, v_w_in, v_ssm_a_re, v_ssm_a_im, v_ssm_b_re, v_ssm_b_im, v_ssm_c_re, v_ssm_c_im, v_ssm_d,
                               v_ssm_log_step, v_ssm_w_glu, v_ssm_b_glu, v_dn_conv_w, v_dn_a_log, v_dn_dt_bias, v_dn_norm_g,
                               v_sg_ln_g, v_sg_ln_b, v_sg_w, v_sg_b, v_w_out, v_ple_norm_g, v_w_ple_gate, v_w_ple, v_final_norm_g)))

    gathered = _all_gather([w_loc[n].astype(WIRE[n]) for n in SHARDED_ORDER], "gather_weights")
    full = dict(w_loc)
    for n, g in zip(SHARDED_ORDER, gathered):
        full[n] = _from_gathered(g, SHARDED[n])
    full['ssm_w_glu'] = full['ssm_w_glu'].astype(F32)

    loss_part, grad_x, grads = _local_step(x, p, full, loss_target)

    dest = [_to_dest_blocks(grads[n], SHARDED[n], WIRE[n]) for n in SHARDED_ORDER]
    c = lax.axis_index("c")
    own = [lax.dynamic_index_in_dim(d, c, 0, keepdims=False) for d in dest]
    for_sibling = [lax.dynamic_index_in_dim(d, 1 - c, 0, keepdims=False) for d in dest]
    from_sibling = _pair_exchange(for_sibling, "grads_pair_exchange")
    chip_sums = [_add_pair(a, b, f"grads_pair_sum_{n}") for n, a, b in zip(SHARDED_ORDER, own, from_sibling)]
    by_chip = _chip_exchange(chip_sums, "grads_chip_exchange")
    rep_pack = _pack([grads[n] for n in REPLICATED_ORDER] + [loss_part.reshape(1)])
    (rep_recv,) = _all_gather([rep_pack], "gather_small_grads")

    outs = {k: {} for k in 'gdmv'}
    for n, gk in zip(SHARDED_ORDER, by_chip):
        for k, o in zip('gdmv', _sum_adamw(gk, w_loc[n], m_loc[n], v_loc[n], f"adamw_{n}")):
            outs[k][n] = o
    one = jnp.zeros((1,), F32)
    rep_out = _sum_adamw(rep_recv, _pack([w_loc[n] for n in REPLICATED_ORDER] + [one]),
                         _pack([m_loc[n] for n in REPLICATED_ORDER] + [one]),
                         _pack([v_loc[n] for n in REPLICATED_ORDER] + [one]), "adamw_replicated")
    rep_shapes = [w_loc[n].shape for n in REPLICATED_ORDER] + [(1,)]
    for k, rep_p in zip('gdmv', rep_out):
        outs[k].update(zip(REPLICATED_ORDER + ['loss'], _unpack(rep_p, rep_shapes)))
    loss = outs['g']['loss'].reshape(())
    return (loss, grad_x, *[outs['g'][n] for n in WEIGHTS], *[outs['d'][n] for n in WEIGHTS],
            *[outs['m'][n] for n in WEIGHTS], *[outs['v'][n] for n in WEIGHTS])
```

```python
import functools

import jax
import jax.numpy as jnp
from jax import lax
from jax.experimental import pallas as pl
from jax.experimental.pallas import tpu as pltpu

F32 = jnp.float32
BF16 = jnp.bfloat16
EPS = 1e-6

D = 1024
D_PLE = 256
D_SSM = 256
D_DN = 512
D_SG = 256
G = 16
CG = 16
NS = 64
NRE = G * NS
H = 4
DH = 128
DN_C = 128
SG_C = 128
ZW = 3456
Z_PIECES = (512, 1536, 512, 768, 128)
N_DEV = 8
LANES = 128
PACK_ROWS = 256
VMEM_LIMIT = 56 * 1024 * 1024
ELEMENTWISE_STEP_BYTES = 4 * 1024 * 1024
TB = 256
SG_TB = 512
PREP_TB = 256

ADAM_LR = 0.001
ADAM_B1 = 0.9
ADAM_B2 = 0.999
ADAM_EPS = 1e-08
ADAM_WD = 0.01
ADAM_STEP = 10

S5_L = 128
S5_GROUP = 8
S5_SHIFTS = (1, 2, 4)

WEIGHTS = ['norm_g', 'w_in', 'ssm_a_re', 'ssm_a_im', 'ssm_b_re', 'ssm_b_im', 'ssm_c_re', 'ssm_c_im', 'ssm_d',
           'ssm_log_step', 'ssm_w_glu', 'ssm_b_glu', 'dn_conv_w', 'dn_a_log', 'dn_dt_bias', 'dn_norm_g', 'sg_ln_g',
           'sg_ln_b', 'sg_w', 'sg_b', 'w_out', 'ple_norm_g', 'w_ple_gate', 'w_ple', 'final_norm_g']
SHARDED = {'w_in': 2, 'ssm_w_glu': 1, 'dn_conv_w': 2, 'w_out': 1, 'w_ple_gate': 1, 'w_ple': 2}
SHARDED_ORDER = ['w_in', 'ssm_w_glu', 'dn_conv_w', 'w_out', 'w_ple_gate', 'w_ple']
WIRE = {'w_in': BF16, 'ssm_w_glu': BF16, 'dn_conv_w': F32, 'w_out': BF16, 'w_ple_gate': BF16, 'w_ple': BF16}
REPLICATED_ORDER = [n for n in WEIGHTS if n not in SHARDED]


def _cp(*sem):
    return pltpu.CompilerParams(dimension_semantics=sem, vmem_limit_bytes=VMEM_LIMIT)


def _dg(a, b, ca, cb, precision=None):
    return lax.dot_general(a, b, (((ca,), (cb,)), ((), ())), precision=precision, preferred_element_type=F32)


@jax.custom_vjp
def _mm(a, b):
    return _dg(a.astype(BF16), b.astype(BF16), 1, 0)


def _mm_fwd(a, b):
    return _mm(a, b), (a, b)


def _mm_bwd(res, g):
    a, b = res
    gb = g.astype(BF16)
    return _dg(gb, b.astype(BF16), 1, 1), _dg(a.astype(BF16), gb, 0, 0)


_mm.defvjp(_mm_fwd, _mm_bwd)


@jax.custom_vjp
def _mm_nt(a, b):
    return _dg(a.astype(BF16), b.astype(BF16), 1, 1)


def _mm_nt_fwd(a, b):
    return _mm_nt(a, b), (a, b)


def _mm_nt_bwd(res, g):
    a, b = res
    gb = g.astype(BF16)
    return _dg(gb, b.astype(BF16), 1, 0), _dg(gb, a.astype(BF16), 0, 0)


_mm_nt.defvjp(_mm_nt_fwd, _mm_nt_bwd)


@jax.custom_vjp
def _mm_tn(a, b):
    return _dg(a.astype(BF16), b.astype(BF16), 0, 0)


def _mm_tn_fwd(a, b):
    return _mm_tn(a, b), (a, b)


def _mm_tn_bwd(res, g):
    a, b = res
    gb = g.astype(BF16)
    return _dg(b.astype(BF16), gb, 1, 1), _dg(a.astype(BF16), gb, 1, 0)


_mm_tn.defvjp(_mm_tn_fwd, _mm_tn_bwd)


def _split(x, n):
    pieces = []
    for _ in range(n - 1):
        hi = x.astype(BF16)
        pieces.append(hi)
        x = x - hi.astype(F32)
    pieces.append(x.astype(BF16))
    return pieces


def _dg3(a, b, ca, cb):
    a_hi, a_lo = _split(a, 2)
    b_hi, b_lo = _split(b, 2)
    return _dg(a_hi, b_hi, ca, cb) + (_dg(a_hi, b_lo, ca, cb) + _dg(a_lo, b_hi, ca, cb))


@jax.custom_vjp
def _dot3(a, b):
    return _dg3(a, b, 1, 0)


def _dot3_fwd(a, b):
    return _dot3(a, b), (a, b)


def _dot3_bwd(res, g):
    a, b = res
    return _dg3(g, b, 1, 1), _dg3(a, g, 0, 0)


_dot3.defvjp(_dot3_fwd, _dot3_bwd)


def _dg_sel(x, e, cx, ce, x_first):
    eb = e.astype(BF16)
    out = None
    for piece in reversed(_split(x, 3)):
        term = _dg(piece, eb, cx, ce) if x_first else _dg(eb, piece, ce, cx)
        out = term if out is None else out + term
    return out


@jax.custom_vjp
def _sel_r(x, e):
    return _dg_sel(x, e, 1, 0, True)


def _sel_r_fwd(x, e):
    return _sel_r(x, e), e


def _sel_r_bwd(e, g):
    return _dg_sel(g, e, 1, 1, True), jnp.zeros_like(e)


_sel_r.defvjp(_sel_r_fwd, _sel_r_bwd)


@jax.custom_vjp
def _sel_l(e, x):
    return _dg_sel(x, e, 0, 1, False)


def _sel_l_fwd(e, x):
    return _sel_l(e, x), e


def _sel_l_bwd(e, g):
    return jnp.zeros_like(e), _dg_sel(g, e, 0, 0, False)


_sel_l.defvjp(_sel_l_fwd, _sel_l_bwd)


def _rms(x, g):
    return x * lax.rsqrt(jnp.mean(x * x, axis=-1, keepdims=True) + EPS) * g


def _silu(x):
    return x * jax.nn.sigmoid(x)


def _in_proj_fwd(x, g, w, tb, name):
    t = x.shape[0]

    def body(x_ref, g_ref, w_ref, *z_refs):
        h = _rms(x_ref[...], g_ref[...])
        z = jnp.dot(h.astype(BF16), w_ref[...], preferred_element_type=F32)
        off = 0
        for z_ref, n in zip(z_refs, Z_PIECES):
            z_ref[...] = z[:, off:off + n]
            off += n

    row = lambda i: (i, 0)
    full = lambda i: (0, 0)
    return pl.pallas_call(
        body, grid=(t // tb,),
        in_specs=[pl.BlockSpec((tb, D), row), pl.BlockSpec((1, D), full), pl.BlockSpec((D, ZW), full)],
        out_specs=[pl.BlockSpec((tb, n), row) for n in Z_PIECES],
        out_shape=[jax.ShapeDtypeStruct((t, n), F32) for n in Z_PIECES],
        name=name, compiler_params=_cp("parallel"))(x, g, w)


def _in_proj_bwd_dx(x, g, w, dzs, dx_res, tb, name):
    t = x.shape[0]

    def body(x_ref, g_ref, w_ref, d0, d1, d2, d3, d4, dxr_ref, dx_ref, dg_ref):
        @pl.when(pl.program_id(0) == 0)
        def _():
            dg_ref[...] = jnp.zeros_like(dg_ref)

        dz = jnp.concatenate([d0[...], d1[...], d2[...], d3[...], d4[...]], axis=1).astype(BF16)
        dh = _dg(dz, w_ref[...], 1, 1)
        _, vjp = jax.vjp(_rms, x_ref[...], g_ref[...])
        dx, dg = vjp(dh)
        dx_ref[...] = dx + dxr_ref[...]
        dg_ref[...] += dg

    row = lambda i: (i, 0)
    full = lambda i: (0, 0)
    return pl.pallas_call(
        body, grid=(t // tb,),
        in_specs=[pl.BlockSpec((tb, D), row), pl.BlockSpec((1, D), full), pl.BlockSpec((D, ZW), full)]
        + [pl.BlockSpec((tb, n), row) for n in Z_PIECES] + [pl.BlockSpec((tb, D), row)],
        out_specs=[pl.BlockSpec((tb, D), row), pl.BlockSpec((1, D), full)],
        out_shape=[jax.ShapeDtypeStruct((t, D), F32), jax.ShapeDtypeStruct((1, D), F32)],
        name=name, compiler_params=_cp("arbitrary"))(x, g, w, *dzs, dx_res)


def _in_proj_bwd_dw(x, g, dzs, tb, name):
    t = x.shape[0]

    def body(x_ref, g_ref, d0, d1, d2, d3, d4, *dw_refs):
        @pl.when(pl.program_id(0) == 0)
        def _():
            for r in dw_refs:
                r[...] = jnp.zeros_like(r)

        h = _rms(x_ref[...], g_ref[...]).astype(BF16)
        for d_ref, dw_ref in zip((d0, d1, d2, d3, d4), dw_refs):
            dw_ref[...] += _dg(h, d_ref[...].astype(BF16), 0, 0)

    row = lambda i: (i, 0)
    full = lambda i: (0, 0)
    return pl.pallas_call(
        body, grid=(t // tb,),
        in_specs=[pl.BlockSpec((tb, D), row), pl.BlockSpec((1, D), full)] + [pl.BlockSpec((tb, n), row) for n in Z_PIECES],
        out_specs=[pl.BlockSpec((D, n), full) for n in Z_PIECES],
        out_shape=[jax.ShapeDtypeStruct((D, n), F32) for n in Z_PIECES],
        name=name, compiler_params=_cp("arbitrary"))(x, g, *dzs)


def _s5_tables(a_re, a_im, b_re, b_im, c_re, c_im, d_skip, log_step):
    step = jnp.exp(log_step)[:, None]

    def lam_pow(k):
        mag = jnp.exp(k * a_re * step)
        ang = k * a_im * step
        return mag * jnp.cos(ang), mag * jnp.sin(ang)

    lam_re, lam_im = lam_pow(1.0)
    den = a_re * a_re + a_im * a_im
    nr, ni = lam_re - 1.0, lam_im
    f_re = (nr * a_re + ni * a_im) / den
    f_im = (ni * a_re - nr * a_im) / den
    bbar_re = f_re[..., None] * b_re - f_im[..., None] * b_im
    bbar_im = f_re[..., None] * b_im + f_im[..., None] * b_re
    eye = jnp.eye(G, dtype=F32)

    def blk_b(bb):
        return (jnp.transpose(bb, (0, 2, 1))[:, :, None, :] * eye[:, None, :, None]).reshape(D_SSM, NRE)

    def blk_c(cc):
        return (jnp.transpose(cc, (0, 2, 1))[:, :, None, :] * eye[:, None, :, None]).reshape(NRE, D_SSM)

    b_blk = jnp.concatenate([blk_b(bbar_re), blk_b(bbar_im)], axis=1)
    c_blk = jnp.concatenate([blk_c(c_re), -blk_c(c_im)], axis=0)
    ks = jnp.asarray(S5_SHIFTS, F32)[:, None, None]
    ld_re, ld_im = lam_pow(ks)
    ld = jnp.concatenate([ld_re.reshape(-1, 1, NRE), ld_im.reshape(-1, 1, NRE)], axis=-1)
    js = jnp.arange(1, S5_GROUP + 1, dtype=F32)[:, None, None]
    lp_re, lp_im = lam_pow(js)
    lp = jnp.concatenate([lp_re.reshape(S5_GROUP, NRE), lp_im.reshape(S5_GROUP, NRE)], axis=-1)
    return b_blk, c_blk, ld, lp, d_skip.reshape(1, D_SSM)


def _make_group_shift(rows):
    @functools.partial(jax.custom_vjp, nondiff_argnums=(1,))
    def shift(x, d):
        r = lax.broadcasted_iota(jnp.int32, x.shape, 0) & (S5_GROUP - 1)
        return jnp.where(r >= d, pltpu.roll(x, d, 0), 0.0)

    def fwd(x, d):
        return shift(x, d), None

    def bwd(d, _, g):
        r = lax.broadcasted_iota(jnp.int32, g.shape, 0) & (S5_GROUP - 1)
        return (jnp.where(r < S5_GROUP - d, pltpu.roll(g, rows - d, 0), 0.0),)

    shift.defvjp(fwd, bwd)
    return shift


def _interleave(*gens):
    results = [None] * len(gens)
    live = list(enumerate(gens))
    while live:
        still = []
        for idx, gen in live:
            try:
                next(gen)
                still.append((idx, gen))
            except StopIteration as done:
                results[idx] = done.value
        live = still
    return results


def _s5_chunk_gen(u, gate, cr, ci, b_blk, c_blk, lds, lp, dv, wglu, bglu):
    shift = _make_group_shift(S5_L)
    bu = _mm(u, b_blk)
    hr, hi = bu[:, :NRE], bu[:, NRE:]
    yield
    for ld, d in zip(lds, S5_SHIFTS):
        lr, li = ld[:, :NRE], ld[:, NRE:]
        sr, si = shift(hr, d), shift(hi, d)
        hr, hi = hr + lr * sr - li * si, hi + lr * si + li * sr
        yield
    pr, pi = lp[:, :NRE], lp[:, NRE:]
    rows_r, rows_i = [], []
    for r in range(S5_L // S5_GROUP):
        br, bi = hr[r * S5_GROUP:(r + 1) * S5_GROUP], hi[r * S5_GROUP:(r + 1) * S5_GROUP]
        br, bi = br + pr * cr - pi * ci, bi + pr * ci + pi * cr
        cr, ci = br[S5_GROUP - 1:S5_GROUP], bi[S5_GROUP - 1:S5_GROUP]
        rows_r.append(br)
        rows_i.append(bi)
        if r % 2:
            yield
    y = _mm(jnp.concatenate([jnp.concatenate(rows_r, axis=0), jnp.concatenate(rows_i, axis=0)], axis=1), c_blk) + dv * u
    yield
    y = jax.nn.gelu(y)
    y = y * jax.nn.sigmoid(_mm(y, wglu) + bglu)
    return y * _silu(gate), cr, ci


def _s5_specs(n_s, rev):
    nd = len(S5_SHIFTS)
    if rev:
        row = lambda b, i: (b * n_s + n_s - 1 - i, 0)
        row3 = lambda b, i: (b * n_s + n_s - 1 - i, 0, 0)
    else:
        row = lambda b, i: (b * n_s + i, 0)
        row3 = lambda b, i: (b * n_s + i, 0, 0)
    full = lambda b, i: (0, 0)
    full3 = lambda b, i: (0, 0, 0)
    par = [pl.BlockSpec((D_SSM, 2 * NRE), full), pl.BlockSpec((2 * NRE, D_SSM), full), pl.BlockSpec((nd, 1, 2 * NRE), full3),
           pl.BlockSpec((S5_GROUP, 2 * NRE), full), pl.BlockSpec((1, D_SSM), full), pl.BlockSpec((D_SSM, D_SSM), full),
           pl.BlockSpec((1, D_SSM), full)]
    par_shapes = [(D_SSM, 2 * NRE), (2 * NRE, D_SSM), (nd, 1, 2 * NRE), (S5_GROUP, 2 * NRE), (1, D_SSM), (D_SSM, D_SSM), (1, D_SSM)]
    return row, row3, par, par_shapes


def _dn_post(c):
    s = _silu(c)
    parts = []
    for j in range(12):
        xj = s[:, j * DH:(j + 1) * DH]
        if j < 8:
            xj = xj * lax.rsqrt(jnp.sum(xj * xj, axis=-1, keepdims=True) + EPS)
        if j < 4:
            xj = xj * (DH ** -0.5)
        parts.append(xj)
    return jnp.concatenate(parts, axis=1)


def _dn_prep_fwd(zq, conv_w8, bl, s, tb, name):
    n_s = s // tb
    hb = tb // 8
    w3 = 3 * D_DN

    def body(cur_ref, prev_ref, w_ref, o_ref):
        i = pl.program_id(1)
        prev = jnp.where(i > 0, prev_ref[...], 0.0)
        ext = jnp.concatenate([prev, cur_ref[...]], axis=0)
        c = jnp.zeros((tb, w3), F32)
        for k in range(4):
            sh = ext if k == 3 else pltpu.roll(ext, 3 - k, 0)
            c = c + w_ref[k:k + 1, :] * sh[8:, :]
        o_ref[...] = _dn_post(c)

    row = lambda b, i: (b * n_s + i, 0)
    prv = lambda b, i: (jnp.maximum((b * n_s + i) * hb - 1, 0), 0)
    full = lambda b, i: (0, 0)
    t = bl * s
    return pl.pallas_call(
        body, grid=(bl, n_s),
        in_specs=[pl.BlockSpec((tb, w3), row), pl.BlockSpec((8, w3), prv), pl.BlockSpec((8, w3), full)],
        out_specs=pl.BlockSpec((tb, w3), row),
        out_shape=jax.ShapeDtypeStruct((t, w3), F32),
        name=name, compiler_params=_cp("parallel", "parallel"))(zq, zq, conv_w8)


def _dn_prep_bwd(zq, dqkv, conv_w8, bl, s, tb, name):
    n_s = s // tb
    hb = tb // 8
    w3 = 3 * D_DN
    n_blk8 = bl * s // 8

    def body(cur_ref, prev_ref, next_ref, d_ref, dnext_ref, w_ref, dz_ref, dw_ref):
        b, i = pl.program_id(0), pl.program_id(1)

        @pl.when((b == 0) & (i == 0))
        def _():
            dw_ref[...] = jnp.zeros_like(dw_ref)

        prev = jnp.where(i > 0, prev_ref[...], 0.0)
        nxt = jnp.where(i < n_s - 1, next_ref[...], 0.0)
        dnxt = jnp.where(i < n_s - 1, dnext_ref[...], 0.0)
        ext = jnp.concatenate([prev, cur_ref[...], nxt], axis=0)
        shifted = [ext if k == 3 else pltpu.roll(ext, 3 - k, 0) for k in range(4)]
        c2 = jnp.zeros((tb + 8, w3), F32)
        for k in range(4):
            c2 = c2 + w_ref[k:k + 1, :] * shifted[k][8:, :]
        dpost = jnp.concatenate([d_ref[...], dnxt], axis=0)
        _, vjp = jax.vjp(_dn_post, c2)
        (dc2,) = vjp(dpost)
        dz = jnp.zeros((tb, w3), F32)
        for k in range(4):
            up = dc2 if k == 3 else pltpu.roll(dc2, tb + 8 - (3 - k), 0)
            dz = dz + w_ref[k:k + 1, :] * up[:tb, :]
            dw_ref[k:k + 1, :] += jnp.sum(dc2[:tb, :] * shifted[k][8:8 + tb, :], axis=0, keepdims=True)
        dz_ref[...] = dz

    row = lambda b, i: (b * n_s + i, 0)
    prv = lambda b, i: (jnp.maximum((b * n_s + i) * hb - 1, 0), 0)
    nxt = lambda b, i: (jnp.minimum((b * n_s + i + 1) * hb, n_blk8 - 1), 0)
    full = lambda b, i: (0, 0)
    t = bl * s
    return pl.pallas_call(
        body, grid=(bl, n_s),
        in_specs=[pl.BlockSpec((tb, w3), row), pl.BlockSpec((8, w3), prv), pl.BlockSpec((8, w3), nxt),
                  pl.BlockSpec((tb, w3), row), pl.BlockSpec((8, w3), nxt), pl.BlockSpec((8, w3), full)],
        out_specs=[pl.BlockSpec((tb, w3), row), pl.BlockSpec((8, w3), full)],
        out_shape=[jax.ShapeDtypeStruct((t, w3), F32), jax.ShapeDtypeStruct((8, w3), F32)],
        name=name, compiler_params=_cp("arbitrary", "arbitrary"))(zq, zq, zq, dqkv, dqkv, conv_w8)


def _unit_lower_inverse_steps(ms):
    c_len = ms[0].shape[0]
    eye = lax.broadcasted_iota(jnp.int32, (c_len, c_len), 0) == lax.broadcasted_iota(jnp.int32, (c_len, c_len), 1)
    ident = jnp.where(eye, 1.0, 0.0)
    ps = ms
    tinvs = [ident - m for m in ms]
    for _ in range(c_len.bit_length() - 2):
        ps = [_dg3(p, p, 1, 0) for p in ps]
        yield
        tinvs = [t + _dg3(t, p, 1, 0) for t, p in zip(tinvs, ps)]
        yield
    return tinvs


@jax.custom_vjp
def _known_inverses(ms, tinvs):
    return tinvs


def _known_inverses_fwd(ms, tinvs):
    return tinvs, tinvs


def _known_inverses_bwd(tinvs, gs):
    return [-_dg3(_dg3(t, g, 0, 0), t, 1, 1) for t, g in zip(tinvs, gs)], [jnp.zeros_like(t) for t in tinvs]


_known_inverses.defvjp(_known_inverses_fwd, _known_inverses_bwd)


def _dn_chunk_gen(qkv, zab, zg, states, alog_e, dt_e, ng, known_tinvs=None):
    c_len = DN_C
    r = lax.broadcasted_iota(jnp.int32, (c_len, c_len), 0)
    c = lax.broadcasted_iota(jnp.int32, (c_len, c_len), 1)
    causal, strict = r >= c, r > c
    tril = jnp.where(causal, 1.0, 0.0)
    rr = lax.broadcasted_iota(jnp.int32, (LANES, D_DN), 0)
    cc = lax.broadcasted_iota(jnp.int32, (LANES, D_DN), 1)
    e_a = jnp.where((cc >= rr * DH) & (cc < rr * DH + DH) & (rr < H), 1.0, 0.0)
    e_b = jnp.where((cc >= (rr - H) * DH) & (cc < (rr - H) * DH + DH) & (rr >= H) & (rr < 2 * H), 1.0, 0.0)
    a_e = _sel_r(zab, e_a)
    b_e = _sel_r(zab, e_b)
    beta = jax.nn.sigmoid(b_e)
    g = -jnp.exp(alog_e) * jax.nn.softplus(a_e + dt_e)
    yield
    gc = _sel_l(tril, g)
    glast = jnp.sum(g, axis=0, keepdims=True)
    eg = jnp.exp(gc)
    ekd = jnp.exp(glast - gc)
    dl = jnp.exp(glast)
    yield
    heads = range(H)
    sls = [slice(h * DH, (h + 1) * DH) for h in heads]
    qs = [qkv[:, h * DH:(h + 1) * DH] for h in heads]
    ks = [qkv[:, D_DN + h * DH:D_DN + (h + 1) * DH] for h in heads]
    vs = [qkv[:, 2 * D_DN + h * DH:2 * D_DN + (h + 1) * DH] for h in heads]
    ccols = [gc[:, sl] for sl in sls]
    decs = [jnp.where(causal, jnp.exp(jnp.where(causal, cl - jnp.transpose(cl), 0.0)), 0.0) for cl in ccols]
    kbs = [k * beta[:, sl] for k, sl in zip(ks, sls)]
    ms = [jnp.where(strict, _mm_nt(kb, k) * dec, 0.0) for kb, k, dec in zip(kbs, ks, decs)]
    yield
    if known_tinvs is None:
        tinvs = yield from _unit_lower_inverse_steps(ms)
    else:
        tinvs = _known_inverses(ms, list(known_tinvs))
    sols = [_dot3(t, jnp.concatenate([v * beta[:, sl], kb * eg[:, sl]], axis=1))
            for t, v, kb, sl in zip(tinvs, vs, kbs, sls)]
    yield
    atts = [_mm_nt(q, k) * dec for q, k, dec in zip(qs, ks, decs)]
    vnews = [sol[:, :DH] - _mm(sol[:, DH:], st) for sol, st in zip(sols, states)]
    yield
    os_ = [_mm(q * eg[:, sl], st) + _mm(att, vn) for q, sl, st, att, vn in zip(qs, sls, states, atts, vnews)]
    yield
    new_states = [st * dl[:, sl] + _mm_tn(k * ekd[:, sl], vn) for st, sl, k, vn in zip(states, sls, ks, vnews)]
    yield
    ys = [_rms(o, ng) * _silu(zg[:, sl]) for o, sl in zip(os_, sls)]
    return jnp.concatenate(ys, axis=1), new_states, tinvs


def _dn_scan_specs(n_c, rev):
    if rev:
        row = lambda b, i: (b * n_c + n_c - 1 - i, 0)
        row4 = lambda b, i: (b * n_c + n_c - 1 - i, 0, 0, 0)
    else:
        row = lambda b, i: (b * n_c + i, 0)
        row4 = lambda b, i: (b * n_c + i, 0, 0, 0)
    full = lambda b, i: (0, 0)
    par = [pl.BlockSpec((1, D_DN), full), pl.BlockSpec((1, D_DN), full), pl.BlockSpec((1, DH), full)]
    par_shapes = [(1, D_DN), (1, D_DN), (1, DH)]
    return row, row4, par, par_shapes


def _mix_fwd(zs, s5_par, qkv, zab, zg, dn_par, bl, s, name):
    assert S5_L == DN_C
    n_c = s // DN_C
    nd = len(S5_SHIFTS)
    row, row3, s5_specs, _ = _s5_specs(n_c, False)
    _, row4, dn_specs, _ = _dn_scan_specs(n_c, False)

    def body(z_ref, b_ref, c_ref, ld_ref, lp_ref, dv_ref, wg_ref, bg_ref, q_ref, ab_ref, zg_ref, al_ref, dt_ref, ng_ref,
             ys_ref, car_ref, yd_ref, st_ref, ti_ref, cs, ssc):
        @pl.when(pl.program_id(1) == 0)
        def _():
            cs[...] = jnp.zeros_like(cs)
            ssc[...] = jnp.zeros_like(ssc)

        c = cs[...]
        car_ref[0] = c
        sts = [ssc[h] for h in range(H)]
        for h in range(H):
            st_ref[0, h] = sts[h]
        z = z_ref[...]
        (y_s, ncr, nci), (y_d, new_sts, tinvs) = _interleave(
            _s5_chunk_gen(z[:, :D_SSM], z[:, D_SSM:], c[:, :NRE], c[:, NRE:], b_ref[...], c_ref[...],
                          [ld_ref[k] for k in range(nd)], lp_ref[...], dv_ref[...], wg_ref[...], bg_ref[...]),
            _dn_chunk_gen(q_ref[...], ab_ref[...], zg_ref[...], sts, al_ref[...], dt_ref[...], ng_ref[...]))
        ys_ref[...] = y_s
        cs[:, :NRE] = ncr
        cs[:, NRE:] = nci
        yd_ref[...] = y_d
        for h in range(H):
            ssc[h] = new_sts[h]
            ti_ref[0, h] = tinvs[h]

    t = bl * s
    per_chunk = jax.ShapeDtypeStruct((bl * n_c, H, DH, DH), F32)
    return pl.pallas_call(
        body, grid=(bl, n_c),
        in_specs=[pl.BlockSpec((S5_L, 2 * D_SSM), row)] + s5_specs
        + [pl.BlockSpec((DN_C, 3 * D_DN), row), pl.BlockSpec((DN_C, LANES), row), pl.BlockSpec((DN_C, D_DN), row)] + dn_specs,
        out_specs=[pl.BlockSpec((S5_L, D_SSM), row), pl.BlockSpec((1, 1, 2 * NRE), row3), pl.BlockSpec((DN_C, D_DN), row),
                   pl.BlockSpec((1, H, DH, DH), row4), pl.BlockSpec((1, H, DH, DH), row4)],
        out_shape=[jax.ShapeDtypeStruct((t, D_SSM), F32), jax.ShapeDtypeStruct((bl * n_c, 1, 2 * NRE), F32),
                   jax.ShapeDtypeStruct((t, D_DN), F32), per_chunk, per_chunk],
        scratch_shapes=[pltpu.VMEM((1, 2 * NRE), F32), pltpu.VMEM((H, DH, DH), F32)],
        name=name, compiler_params=_cp("arbitrary", "arbitrary"))(zs, *s5_par, qkv, zab, zg, *dn_par)


def _mix_bwd(zs, carries, dy_s, s5_par, qkv, zab, zg, states, tinvs, dy_d, dn_par, bl, s, name):
    n_c = s // DN_C
    nd = len(S5_SHIFTS)
    row, row3, s5_specs, s5_shapes = _s5_specs(n_c, True)
    _, row4, dn_specs, dn_shapes = _dn_scan_specs(n_c, True)

    def both(u, gate, cr, ci, b_blk, c_blk, lds, lp, dv, wglu, bglu, q, ab, zgate, sts, known, alog_e, dt_e, ng):
        (y_s, ncr, nci), (y_d, new_sts, _) = _interleave(
            _s5_chunk_gen(u, gate, cr, ci, b_blk, c_blk, lds, lp, dv, wglu, bglu),
            _dn_chunk_gen(q, ab, zgate, sts, alog_e, dt_e, ng, known_tinvs=known))
        return y_s, ncr, nci, y_d, new_sts

    def body(z_ref, car_ref, dys_ref, b_ref, c_ref, ld_ref, lp_ref, dv_ref, wg_ref, bg_ref,
             q_ref, ab_ref, zg_ref, st_ref, ti_ref, dyd_ref, al_ref, dt_ref, ng_ref,
             dz_ref, db_ref, dc_ref, dld_ref, dlp_ref, ddv_ref, dwg_ref, dbg_ref,
             dq_ref, dab_ref, dzg_ref, dal_ref, ddt_ref, dng_ref, dcs, dsc):
        accs = (db_ref, dc_ref, dld_ref, dlp_ref, ddv_ref, dwg_ref, dbg_ref, dal_ref, ddt_ref, dng_ref)

        @pl.when((pl.program_id(0) == 0) & (pl.program_id(1) == 0))
        def _():
            for r in accs:
                r[...] = jnp.zeros_like(r)

        @pl.when(pl.program_id(1) == 0)
        def _():
            dcs[...] = jnp.zeros_like(dcs)
            dsc[...] = jnp.zeros_like(dsc)

        z = z_ref[...]
        c = car_ref[0]
        _, vjp = jax.vjp(both, z[:, :D_SSM], z[:, D_SSM:], c[:, :NRE], c[:, NRE:], b_ref[...], c_ref[...],
                         [ld_ref[k] for k in range(nd)], lp_ref[...], dv_ref[...], wg_ref[...], bg_ref[...],
                         q_ref[...], ab_ref[...], zg_ref[...], [st_ref[0, h] for h in range(H)],
                         [ti_ref[0, h] for h in range(H)], al_ref[...], dt_ref[...], ng_ref[...])
        dc = dcs[...]
        (du, dgate, dcr, dci, dbb, dcb, dlds, dlpb, ddvb, dwgb, dbgb,
         dq, dab, dzg, dsts, _, dal, ddt, dng) = vjp((dys_ref[...], dc[:, :NRE], dc[:, NRE:], dyd_ref[...],
                                                     [dsc[h] for h in range(H)]))
        dz_ref[...] = jnp.concatenate([du, dgate], axis=1)
        dcs[:, :NRE] = dcr
        dcs[:, NRE:] = dci
        dq_ref[...] = dq
        dab_ref[...] = dab
        dzg_ref[...] = dzg
        for h in range(H):
            dsc[h] = dsts[h]
        for k in range(nd):
            dld_ref[k] += dlds[k]
        for r, ct in ((db_ref, dbb), (dc_ref, dcb), (dlp_ref, dlpb), (ddv_ref, ddvb), (dwg_ref, dwgb), (dbg_ref, dbgb),
                      (dal_ref, dal), (ddt_ref, ddt), (dng_ref, dng)):
            r[...] += ct

    t = bl * s
    chunk4 = pl.BlockSpec((1, H, DH, DH), row4)
    outs = pl.pallas_call(
        body, grid=(bl, n_c),
        in_specs=[pl.BlockSpec((S5_L, 2 * D_SSM), row), pl.BlockSpec((1, 1, 2 * NRE), row3), pl.BlockSpec((S5_L, D_SSM), row)]
        + s5_specs + [pl.BlockSpec((DN_C, 3 * D_DN), row), pl.BlockSpec((DN_C, LANES), row), pl.BlockSpec((DN_C, D_DN), row),
                      chunk4, chunk4, pl.BlockSpec((DN_C, D_DN), row)] + dn_specs,
        out_specs=[pl.BlockSpec((S5_L, 2 * D_SSM), row)] + s5_specs
        + [pl.BlockSpec((DN_C, 3 * D_DN), row), pl.BlockSpec((DN_C, LANES), row), pl.BlockSpec((DN_C, D_DN), row)] + dn_specs,
        out_shape=[jax.ShapeDtypeStruct((t, 2 * D_SSM), F32)] + [jax.ShapeDtypeStruct(sh, F32) for sh in s5_shapes]
        + [jax.ShapeDtypeStruct((t, 3 * D_DN), F32), jax.ShapeDtypeStruct((t, LANES), F32), jax.ShapeDtypeStruct((t, D_DN), F32)]
        + [jax.ShapeDtypeStruct(sh, F32) for sh in dn_shapes],
        scratch_shapes=[pltpu.VMEM((1, 2 * NRE), F32), pltpu.VMEM((H, DH, DH), F32)],
        name=name, compiler_params=_cp("arbitrary", "arbitrary"))(
            zs, carries, dy_s, *s5_par, qkv, zab, zg, states, tinvs, dy_d, *dn_par)
    return outs[:8], outs[8:]


def _sg_fn(n_chunk):
    def f(z, lng, lnb, w, bsp_t):
        u = jax.nn.gelu(z[:, :D_SG])
        v = jax.nn.gelu(z[:, D_SG:2 * D_SG])
        gate = z[:, 2 * D_SG:]
        xc = v - jnp.mean(v, axis=-1, keepdims=True)
        vn = xc * lax.rsqrt(jnp.mean(xc * xc, axis=-1, keepdims=True) + EPS) * lng + lnb
        r = lax.broadcasted_iota(jnp.int32, (SG_C, SG_C), 0)
        c = lax.broadcasted_iota(jnp.int32, (SG_C, SG_C), 1)
        causal = r >= c
        first_half = c < SG_C // 2
        rr = lax.broadcasted_iota(jnp.int32, (LANES, D_SG), 0)
        cc = lax.broadcasted_iota(jnp.int32, (LANES, D_SG), 1)
        expand = jnp.where((cc >= rr * 64) & (cc < rr * 64 + 64) & (rr < 4), 1.0, 0.0)
        bias = _sel_r(bsp_t, expand)
        wm = [jnp.where(causal, w[h], 0.0) for h in range(4)]
        rows = []
        for ci in range(n_chunk):
            vc = vn[ci * SG_C:(ci + 1) * SG_C]
            pairs = []
            for pr in range(2):
                vp = vc[:, pr * LANES:(pr + 1) * LANES]
                pairs.append(jnp.where(first_half, _mm(wm[2 * pr], vp), _mm(wm[2 * pr + 1], vp)))
            rows.append(jnp.concatenate(pairs, axis=1) + bias)
        sp = jnp.concatenate(rows, axis=0) if n_chunk > 1 else rows[0]
        return u * sp * _silu(gate)

    return f


def _sg_specs():
    full = lambda i: (0, 0)
    full3 = lambda i: (0, 0, 0)
    par = [pl.BlockSpec((1, D_SG), full), pl.BlockSpec((1, D_SG), full), pl.BlockSpec((4, SG_C, SG_C), full3),
           pl.BlockSpec((SG_C, LANES), full)]
    par_shapes = [(1, D_SG), (1, D_SG), (4, SG_C, SG_C), (SG_C, LANES)]
    return par, par_shapes


def _sg_fwd(zsg, params, tb, name):
    t = zsg.shape[0]
    f = _sg_fn(tb // SG_C)
    par, _ = _sg_specs()

    def body(z_ref, g_ref, b_ref, w_ref, bs_ref, y_ref):
        y_ref[...] = f(z_ref[...], g_ref[...], b_ref[...], w_ref[...], bs_ref[...])

    row = lambda i: (i, 0)
    return pl.pallas_call(
        body, grid=(t // tb,), in_specs=[pl.BlockSpec((tb, 3 * D_SG), row)] + par,
        out_specs=pl.BlockSpec((tb, D_SG), row), out_shape=jax.ShapeDtypeStruct((t, D_SG), F32),
        name=name, compiler_params=_cp("parallel"))(zsg, *params)


def _sg_bwd(zsg, dy, params, tb, name):
    t = zsg.shape[0]
    f = _sg_fn(tb // SG_C)
    par, par_shapes = _sg_specs()

    def body(z_ref, dy_ref, g_ref, b_ref, w_ref, bs_ref, dz_ref, dg_ref, db_ref, dw_ref, dbs_ref):
        accs = (dg_ref, db_ref, dw_ref, dbs_ref)

        @pl.when(pl.program_id(0) == 0)
        def _():
            for r in accs:
                r[...] = jnp.zeros_like(r)

        _, vjp = jax.vjp(f, z_ref[...], g_ref[...], b_ref[...], w_ref[...], bs_ref[...])
        cts = vjp(dy_ref[...])
        dz_ref[...] = cts[0]
        for r, ct in zip(accs, cts[1:]):
            r[...] += ct

    row = lambda i: (i, 0)
    return pl.pallas_call(
        body, grid=(t // tb,), in_specs=[pl.BlockSpec((tb, 3 * D_SG), row), pl.BlockSpec((tb, D_SG), row)] + par,
        out_specs=[pl.BlockSpec((tb, 3 * D_SG), row)] + par,
        out_shape=[jax.ShapeDtypeStruct((t, 3 * D_SG), F32)] + [jax.ShapeDtypeStruct(sh, F32) for sh in par_shapes],
        name=name, compiler_params=_cp("arbitrary"))(zsg, dy, *params)


def _out_fwd(x, ys, p, w_out, pg, w_gate, w_ple, tb, name):
    t = x.shape[0]

    def body(x_ref, y0, y1, y2, p_ref, wo_ref, pg_ref, wg_ref, wp_ref, o_ref):
        y = jnp.concatenate([y0[...], y1[...], y2[...]], axis=1).astype(BF16)
        x1 = x_ref[...] + jnp.dot(y, wo_ref[...], preferred_element_type=F32)
        hn = _rms(x1, pg_ref[...]).astype(BF16)
        gate = jax.nn.sigmoid(jnp.dot(hn, wg_ref[...], preferred_element_type=F32))
        pp = jnp.dot(p_ref[...].astype(BF16), wp_ref[...], preferred_element_type=F32)
        o_ref[...] = x1 + gate * pp

    row = lambda i: (i, 0)
    full = lambda i: (0, 0)
    return pl.pallas_call(
        body, grid=(t // tb,),
        in_specs=[pl.BlockSpec((tb, D), row), pl.BlockSpec((tb, D_SSM), row), pl.BlockSpec((tb, D_DN), row),
                  pl.BlockSpec((tb, D_SG), row), pl.BlockSpec((tb, D_PLE), row), pl.BlockSpec((D, D), full),
                  pl.BlockSpec((1, D), full), pl.BlockSpec((D, D), full), pl.BlockSpec((D_PLE, D), full)],
        out_specs=pl.BlockSpec((tb, D), row), out_shape=jax.ShapeDtypeStruct((t, D), F32),
        name=name, compiler_params=_cp("parallel"))(x, *ys, p, w_out, pg, w_gate, w_ple)


def _out_bwd(x, ys, p, dx2, w_out, pg, w_gate, w_ple, tb, name):
    t = x.shape[0]

    def body(x_ref, y0, y1, y2, p_ref, d_ref, wo_ref, pg_ref, wg_ref, wp_ref,
             dx_ref, dy0, dy1, dy2, dwo_ref, dpg_ref, dwg_ref, dwp_ref):
        accs = (dwo_ref, dpg_ref, dwg_ref, dwp_ref)

        @pl.when(pl.program_id(0) == 0)
        def _():
            for r in accs:
                r[...] = jnp.zeros_like(r)

        y = jnp.concatenate([y0[...], y1[...], y2[...]], axis=1).astype(BF16)
        x1 = x_ref[...] + jnp.dot(y, wo_ref[...], preferred_element_type=F32)
        hn, rms_vjp = jax.vjp(_rms, x1, pg_ref[...])
        hb = hn.astype(BF16)
        gate = jax.nn.sigmoid(jnp.dot(hb, wg_ref[...], preferred_element_type=F32))
        pb = p_ref[...].astype(BF16)
        pp = jnp.dot(pb, wp_ref[...], preferred_element_type=F32)
        d2 = d_ref[...]
        dpp = (d2 * gate).astype(BF16)
        dlog = (d2 * pp * gate * (1.0 - gate)).astype(BF16)
        dwp_ref[...] += _dg(pb, dpp, 0, 0)
        dwg_ref[...] += _dg(hb, dlog, 0, 0)
        dx1_n, dpg = rms_vjp(_dg(dlog, wg_ref[...], 1, 1))
        dpg_ref[...] += dpg
        dx1 = d2 + dx1_n
        dx_ref[...] = dx1
        db = dx1.astype(BF16)
        dwo_ref[...] += _dg(y, db, 0, 0)
        dy = _dg(db, wo_ref[...], 1, 1)
        dy0[...] = dy[:, :D_SSM]
        dy1[...] = dy[:, D_SSM:D_SSM + D_DN]
        dy2[...] = dy[:, D_SSM + D_DN:]

    row = lambda i: (i, 0)
    full = lambda i: (0, 0)
    acts = [pl.BlockSpec((tb, D), row), pl.BlockSpec((tb, D_SSM), row), pl.BlockSpec((tb, D_DN), row), pl.BlockSpec((tb, D_SG), row)]
    wts = [pl.BlockSpec((D, D), full), pl.BlockSpec((1, D), full), pl.BlockSpec((D, D), full), pl.BlockSpec((D_PLE, D), full)]
    return pl.pallas_call(
        body, grid=(t // tb,),
        in_specs=acts + [pl.BlockSpec((tb, D_PLE), row), pl.BlockSpec((tb, D), row)] + wts,
        out_specs=acts + wts,
        out_shape=[jax.ShapeDtypeStruct((t, n), F32) for n in (D, D_SSM, D_DN, D_SG)]
        + [jax.ShapeDtypeStruct(sh, F32) for sh in ((D, D), (1, D), (D, D), (D_PLE, D))],
        name=name, compiler_params=_cp("arbitrary"))(x, *ys, p, dx2, w_out, pg, w_gate, w_ple)


def _loss_head(x, fg, target, tb, name):
    t = x.shape[0]

    def body(x_ref, g_ref, t_ref, dx_ref, dg_ref, loss_ref):
        @pl.when(pl.program_id(0) == 0)
        def _():
            dg_ref[...] = jnp.zeros_like(dg_ref)
            loss_ref[...] = jnp.zeros_like(loss_ref)

        y, vjp = jax.vjp(_rms, x_ref[...], g_ref[...])
        err = y - t_ref[...]
        loss_ref[...] += jnp.zeros_like(loss_ref) + 0.5 * jnp.sum(err * err) / D
        dx, dg = vjp(err / D)
        dx_ref[...] = dx
        dg_ref[...] += dg

    row = lambda i: (i, 0)
    full = lambda i: (0, 0)
    return pl.pallas_call(
        body, grid=(t // tb,),
        in_specs=[pl.BlockSpec((tb, D), row), pl.BlockSpec((1, D), full), pl.BlockSpec((tb, D), row)],
        out_specs=[pl.BlockSpec((tb, D), row), pl.BlockSpec((1, D), full), pl.BlockSpec((1, LANES), full)],
        out_shape=[jax.ShapeDtypeStruct((t, D), F32), jax.ShapeDtypeStruct((1, D), F32), jax.ShapeDtypeStruct((1, LANES), F32)],
        name=name, compiler_params=_cp("arbitrary"))(x, fg, target)


def _hbm_specs(n):
    return [pl.BlockSpec(memory_space=pl.ANY)] * n


def _all_gather(blocks, name):
    n = len(blocks)

    def body(*refs):
        ins, outs = refs[:n], refs[n:2 * n]
        send_sems, recv_sems, local_sems = refs[2 * n:]
        x, y, c = lax.axis_index("x"), lax.axis_index("y"), lax.axis_index("c")
        me, sibling = (x, y, c), (x, y, 1 - c)
        chips = [(1 - x, y), (x, 1 - y), (1 - x, 1 - y)]

        def slot(a, px, py, pc):
            return outs[a].at[4 * px + 2 * py + pc]

        def copy(a, k, blk, to, src=None):
            return pltpu.make_async_remote_copy(
                src_ref=slot(a, *blk) if src is None else src, dst_ref=slot(a, *blk),
                send_sem=send_sems.at[7 * a + k], recv_sem=recv_sems.at[7 * a + k],
                device_id=to, device_id_type=pl.DeviceIdType.MESH)

        mines = [pltpu.make_async_copy(ins[a], slot(a, *me), local_sems.at[a]) for a in range(n)]
        for cp in mines:
            cp.start()
        first = []
        for a in range(n):
            first.append(copy(a, 0, me, sibling, src=ins[a]))
            first += [copy(a, 1 + j, me, (*chip, c), src=ins[a]) for j, chip in enumerate(chips)]
        for cp in first:
            cp.start()
        passed = []
        for j, chip in enumerate(chips):
            for a in range(n):
                copy(a, 1 + j, (*chip, c), me).wait_recv()
                onward = copy(a, 4 + j, (*chip, c), sibling)
                onward.start()
                passed.append(onward)
        for a in range(n):
            copy(a, 0, sibling, me).wait_recv()
        for j, chip in enumerate(chips):
            for a in range(n):
                copy(a, 4 + j, (*chip, 1 - c), me).wait_recv()
        for cp in first + passed:
            cp.wait_send()
        for cp in mines:
            cp.wait()

    return pl.pallas_call(
        body, out_shape=[jax.ShapeDtypeStruct((N_DEV, *b.shape), b.dtype) for b in blocks],
        in_specs=_hbm_specs(n), out_specs=_hbm_specs(n),
        scratch_shapes=[pltpu.SemaphoreType.DMA((7 * n,)), pltpu.SemaphoreType.DMA((7 * n,)), pltpu.SemaphoreType.DMA((n,))],
        name=name)(*blocks)


def _pair_exchange(gs, name):
    n = len(gs)

    def body(*refs):
        ins, recvs = refs[:n], refs[n:2 * n]
        send_sems, recv_sems = refs[2 * n:]
        x, y, c = lax.axis_index("x"), lax.axis_index("y"), lax.axis_index("c")
        remote = [pltpu.make_async_remote_copy(
            src_ref=ins[a], dst_ref=recvs[a], send_sem=send_sems.at[a], recv_sem=recv_sems.at[a],
            device_id=(x, y, 1 - c), device_id_type=pl.DeviceIdType.MESH) for a in range(n)]
        for cp in remote:
            cp.start()
        for cp in remote:
            cp.wait_send()
            cp.wait_recv()

    return pl.pallas_call(
        body, out_shape=[jax.ShapeDtypeStruct(g.shape, g.dtype) for g in gs], in_specs=_hbm_specs(n), out_specs=_hbm_specs(n),
        scratch_shapes=[pltpu.SemaphoreType.DMA((n,)), pltpu.SemaphoreType.DMA((n,))],
        name=name)(*gs)


def _chip_exchange(ps, name):
    n = len(ps)

    def body(*refs):
        ins, outs = refs[:n], refs[n:2 * n]
        send_sems, recv_sems, local_sems = refs[2 * n:]
        x, y, c = lax.axis_index("x"), lax.axis_index("y"), lax.axis_index("c")
        my_chip = 2 * x + y
        local = [pltpu.make_async_copy(ins[a].at[my_chip], outs[a].at[my_chip], local_sems.at[a]) for a in range(n)]
        remote = []
        for j in range(1, 4):
            px = 1 - x if j & 2 else x
            py = 1 - y if j & 1 else y
            for a in range(n):
                remote.append(pltpu.make_async_remote_copy(
                    src_ref=ins[a].at[2 * px + py], dst_ref=outs[a].at[my_chip],
                    send_sem=send_sems.at[3 * a + j - 1], recv_sem=recv_sems.at[3 * a + j - 1],
                    device_id=(px, py, c), device_id_type=pl.DeviceIdType.MESH))
        for cp in local + remote:
            cp.start()
        for cp in remote:
            cp.wait_send()
            cp.wait_recv()
        for cp in local:
            cp.wait()

    return pl.pallas_call(
        body, out_shape=[jax.ShapeDtypeStruct(q.shape, q.dtype) for q in ps], in_specs=_hbm_specs(n), out_specs=_hbm_specs(n),
        scratch_shapes=[pltpu.SemaphoreType.DMA((3 * n,)), pltpu.SemaphoreType.DMA((3 * n,)), pltpu.SemaphoreType.DMA((n,))],
        name=name)(*ps)


def _row_block(rows, bytes_per_row):
    best = None
    for rb in range(16, rows + 1, 16):
        if rows % rb == 0 and rb * bytes_per_row <= ELEMENTWISE_STEP_BYTES:
            best = rb
    return rows if best is None else best


def _add_pair(own, recv, name):
    shape = own.shape
    last = shape[-1]
    rows = own.size // last
    rb = _row_block(rows, 3 * 4 * (-(-last // LANES) * LANES))

    def body(a_ref, b_ref, o_ref):
        o_ref[...] = (a_ref[...].astype(F32) + b_ref[...].astype(F32)).astype(o_ref.dtype)

    row = lambda i: (i, 0)
    out = pl.pallas_call(
        body, grid=(rows // rb,), in_specs=[pl.BlockSpec((rb, last), row)] * 2, out_specs=pl.BlockSpec((rb, last), row),
        out_shape=jax.ShapeDtypeStruct((rows, last), own.dtype), name=name,
        compiler_params=_cp("parallel"))(own.reshape(rows, last), recv.reshape(rows, last))
    return out.reshape(shape)


def _sum_adamw(gk, w, m, v, name):
    shape = w.shape
    n_part = gk.shape[0]
    last = shape[-1]
    rows = w.size // last
    rb = _row_block(rows, (n_part + 7) * 4 * (-(-last // LANES) * LANES))

    def body(g_ref, w_ref, m_ref, v_ref, go_ref, d_ref, mo_ref, vo_ref):
        g = g_ref[0].astype(F32)
        for k in range(1, n_part):
            g = g + g_ref[k].astype(F32)
        mn = ADAM_B1 * m_ref[...] + (1.0 - ADAM_B1) * g
        vn = ADAM_B2 * v_ref[...] + (1.0 - ADAM_B2) * jnp.square(g)
        m_hat = mn / (1.0 - ADAM_B1 ** ADAM_STEP)
        v_hat = vn / (1.0 - ADAM_B2 ** ADAM_STEP)
        go_ref[...] = g
        d_ref[...] = -ADAM_LR * (m_hat / (jnp.sqrt(v_hat) + ADAM_EPS) + ADAM_WD * w_ref[...])
        mo_ref[...] = mn
        vo_ref[...] = vn

    row = lambda i: (i, 0)
    outs = pl.pallas_call(
        body, grid=(rows // rb,),
        in_specs=[pl.BlockSpec((n_part, rb, last), lambda i: (0, i, 0))] + [pl.BlockSpec((rb, last), row)] * 3,
        out_specs=[pl.BlockSpec((rb, last), row)] * 4,
        out_shape=[jax.ShapeDtypeStruct((rows, last), F32)] * 4,
        name=name, compiler_params=_cp("parallel"))(gk.reshape(n_part, rows, last), *[a.reshape(rows, last) for a in (w, m, v)])
    return [o.reshape(shape) for o in outs]


def _seg_rows(shape):
    n = 1
    for d in shape:
        n *= d
    return -(-n // (8 * LANES)) * 8


def _pack(arrs):
    segs = []
    for a in arrs:
        r = _seg_rows(a.shape)
        segs.append(jnp.pad(a.reshape(-1).astype(F32), (0, r * LANES - a.size)).reshape(r, LANES))
    rows = sum(s.shape[0] for s in segs)
    total = -(-rows // PACK_ROWS) * PACK_ROWS
    if total > rows:
        segs.append(jnp.zeros((total - rows, LANES), F32))
    return jnp.concatenate(segs, axis=0)


def _unpack(pack, shapes):
    out, off = [], 0
    for sh in shapes:
        r = _seg_rows(sh)
        n = 1
        for d in sh:
            n *= d
        out.append(pack[off:off + r].reshape(-1)[:n].reshape(sh))
        off += r
    return out


def _to_dest_blocks(full, axis, dtype):
    sh = list(full.shape)
    sh[axis:axis + 1] = [N_DEV // 2, 2, sh[axis] // N_DEV]
    return jnp.moveaxis(full.reshape(sh), (axis, axis + 1), (1, 0)).astype(dtype)


def _from_gathered(g, axis):
    m = jnp.moveaxis(g, 0, axis)
    sh = list(m.shape)
    sh[axis:axis + 2] = [sh[axis] * sh[axis + 1]]
    return m.reshape(sh)


def _reorder_w_in(w):
    return jnp.concatenate([w[:, :2048], w[:, 2056:3336], w[:, 2048:2056], jnp.zeros((D, ZW - 3336), w.dtype)], axis=1)


def _restore_dw_in(dw):
    return jnp.concatenate([dw[:, :2048], dw[:, 3328:3336], dw[:, 2048:3328]], axis=1)


def _local_step(x, p, wts, target):
    bl, s, _ = x.shape
    t = bl * s
    depth = p.shape[0]
    tb, sg_tb, prep_tb = TB, SG_TB, PREP_TB
    xs = [x.reshape(t, D)]
    saved = []
    for i in range(depth):
        li = f"l{i}"
        ng = wts['norm_g'][i].reshape(1, D)
        w_in = _reorder_w_in(wts['w_in'][i]).astype(BF16)
        s5_par_in = (wts['ssm_a_re'][i], wts['ssm_a_im'][i], wts['ssm_b_re'][i], wts['ssm_b_im'][i],
                     wts['ssm_c_re'][i], wts['ssm_c_im'][i], wts['ssm_d'][i], wts['ssm_log_step'][i])
        tabs, tab_vjp = jax.vjp(_s5_tables, *s5_par_in)
        s5_par = (*tabs, wts['ssm_w_glu'][i], wts['ssm_b_glu'][i].reshape(1, D_SSM))
        conv8 = jnp.pad(wts['dn_conv_w'][i], ((0, 4), (0, 0)))
        dn_par = (jnp.repeat(wts['dn_a_log'][i], DH).reshape(1, D_DN), jnp.repeat(wts['dn_dt_bias'][i], DH).reshape(1, D_DN),
                  wts['dn_norm_g'][i].reshape(1, DH))
        sg_par = (wts['sg_ln_g'][i].reshape(1, D_SG), wts['sg_ln_b'][i].reshape(1, D_SG), wts['sg_w'][i],
                  jnp.pad(jnp.transpose(wts['sg_b'][i]), ((0, 0), (0, LANES - 4))))
        out_par = (wts['w_out'][i].astype(BF16), wts['ple_norm_g'][i].reshape(1, D), wts['w_ple_gate'][i].astype(BF16),
                   wts['w_ple'][i].astype(BF16))
        pi = p[i].reshape(t, D_PLE)

        z_ssm, z_qkv, z_gdn, z_sg, z_ab = _in_proj_fwd(xs[i], ng, w_in, tb, f"in_proj_fwd_{li}")
        qkvn = _dn_prep_fwd(z_qkv, conv8, bl, s, prep_tb, f"dn_prep_fwd_{li}")
        y_ssm, carries, y_dn, states, tinvs = _mix_fwd(z_ssm, s5_par, qkvn, z_ab, z_gdn, dn_par, bl, s, f"mix_fwd_{li}")
        y_sg = _sg_fwd(z_sg, sg_par, sg_tb, f"sg_fwd_{li}")
        ys = (y_ssm, y_dn, y_sg)
        xs.append(_out_fwd(xs[i], ys, pi, *out_par, tb, f"out_fwd_{li}"))
        saved.append(dict(ng=ng, w_in=w_in, tab_vjp=tab_vjp, s5_par=s5_par, conv8=conv8, dn_par=dn_par, sg_par=sg_par,
                          out_par=out_par, pi=pi, z=(z_ssm, z_qkv, z_gdn, z_sg, z_ab), carries=carries, qkvn=qkvn,
                          states=states, tinvs=tinvs, ys=ys))

    dx, dfg, loss_vec = _loss_head(xs[depth], wts['final_norm_g'].reshape(1, D), target.reshape(t, D), tb, "loss_head")
    grads = {n: [None] * depth for n in WEIGHTS if n != 'final_norm_g'}
    grads['final_norm_g'] = dfg.reshape(D)
    for i in reversed(range(depth)):
        li = f"l{i}"
        sv = saved[i]
        z_ssm, z_qkv, z_gdn, z_sg, z_ab = sv['z']
        dx_res, dy_ssm, dy_dn, dy_sg, dwo, dpg, dwg, dwp = _out_bwd(xs[i], sv['ys'], sv['pi'], dx, *sv['out_par'], tb, f"out_bwd_{li}")
        dz_sg, dlng, dlnb, dsgw, dbsp = _sg_bwd(z_sg, dy_sg, sv['sg_par'], sg_tb, f"sg_bwd_{li}")
        (dz_ssm, dbb, dcb, dld, dlp, ddv, dwglu, dbglu), (dqkvn, dz_ab, dz_gdn, dal, ddt, dng) = _mix_bwd(
            z_ssm, sv['carries'], dy_ssm, sv['s5_par'], sv['qkvn'], z_ab, z_gdn, sv['states'], sv['tinvs'], dy_dn, sv['dn_par'],
            bl, s, f"mix_bwd_{li}")
        dz_qkv, dconv = _dn_prep_bwd(z_qkv, dqkvn, sv['conv8'], bl, s, prep_tb, f"dn_prep_bwd_{li}")
        dzs = (dz_ssm, dz_qkv, dz_gdn, dz_sg, dz_ab)
        dx, dnorm = _in_proj_bwd_dx(xs[i], sv['ng'], sv['w_in'], dzs, dx_res, tb, f"in_proj_bwd_dx_{li}")
        dws = _in_proj_bwd_dw(xs[i], sv['ng'], dzs, tb, f"in_proj_bwd_dw_{li}")
        ds5 = sv['tab_vjp']((dbb, dcb, dld, dlp, ddv))
        for n, gval in zip(('ssm_a_re', 'ssm_a_im', 'ssm_b_re', 'ssm_b_im', 'ssm_c_re', 'ssm_c_im', 'ssm_d', 'ssm_log_step'), ds5):
            grads[n][i] = gval
        grads['norm_g'][i] = dnorm.reshape(D)
        grads['w_in'][i] = _restore_dw_in(jnp.concatenate(dws, axis=1))
        grads['ssm_w_glu'][i] = dwglu
        grads['ssm_b_glu'][i] = dbglu.reshape(D_SSM)
        grads['dn_conv_w'][i] = dconv[:4]
        grads['dn_a_log'][i] = dal.reshape(H, DH).sum(axis=1)
        grads['dn_dt_bias'][i] = ddt.reshape(H, DH).sum(axis=1)
        grads['dn_norm_g'][i] = dng.reshape(DH)
        grads['sg_ln_g'][i] = dlng.reshape(D_SG)
        grads['sg_ln_b'][i] = dlnb.reshape(D_SG)
        grads['sg_w'][i] = dsgw
        grads['sg_b'][i] = jnp.transpose(dbsp[:, :4])
        grads['w_out'][i] = dwo
        grads['ple_norm_g'][i] = dpg.reshape(D)
        grads['w_ple_gate'][i] = dwg
        grads['w_ple'][i] = dwp
    grads = {n: (g if n == 'final_norm_g' else jnp.stack(g)) for n, g in grads.items()}
    return loss_vec[0, 0], dx.reshape(bl, s, D), grads


def kernel(x, p, norm_g, w_in, ssm_a_re, ssm_a_im, ssm_b_re, ssm_b_im, ssm_c_re, ssm_c_im, ssm_d, ssm_log_step, ssm_w_glu, ssm_b_glu, dn_conv_w, dn_a_log, dn_dt_bias, dn_norm_g, sg_ln_g, sg_ln_b, sg_w, sg_b, w_out, ple_norm_g, w_ple_gate, w_ple, final_norm_g, loss_target, m_norm_g, m_w_in, m_ssm_a_re, m_ssm_a_im, m_ssm_b_re, m_ssm_b_im, m_ssm_c_re, m_ssm_c_im, m_ssm_d, m_ssm_log_step, m_ssm_w_glu, m_ssm_b_glu, m_dn_conv_w, m_dn_a_log, m_dn_dt_bias, m_dn_norm_g, m_sg_ln_g, m_sg_ln_b, m_sg_w, m_sg_b, m_w_out, m_ple_norm_g, m_w_ple_gate, m_w_ple, m_final_norm_g, v_norm_g, v_w_in, v_ssm_a_re, v_ssm_a_im, v_ssm_b_re, v_ssm_b_im, v_ssm_c_re, v_ssm_c_im, v_ssm_d, v_ssm_log_step, v_ssm_w_glu, v_ssm_b_glu, v_dn_conv_w, v_dn_a_log, v_dn_dt_bias, v_dn_norm_g, v_sg_ln_g, v_sg_ln_b, v_sg_w, v_sg_b, v_w_out, v_ple_norm_g, v_w_ple_gate, v_w_ple, v_final_norm_g):
    w_loc = dict(zip(WEIGHTS, (norm_g, w_in, ssm_a_re, ssm_a_im, ssm_b_re, ssm_b_im, ssm_c_re, ssm_c_im, ssm_d, ssm_log_step,
                               ssm_w_glu, ssm_b_glu, dn_conv_w, dn_a_log, dn_dt_bias, dn_norm_g, sg_ln_g, sg_ln_b, sg_w, sg_b,
                               w_out, ple_norm_g, w_ple_gate, w_ple, final_norm_g)))
    m_loc = dict(zip(WEIGHTS, (m_norm_g, m_w_in, m_ssm_a_re, m_ssm_a_im, m_ssm_b_re, m_ssm_b_im, m_ssm_c_re, m_ssm_c_im, m_ssm_d,
                               m_ssm_log_step, m_ssm_w_glu, m_ssm_b_glu, m_dn_conv_w, m_dn_a_log, m_dn_dt_bias, m_dn_norm_g,
                               m_sg_ln_g, m_sg_ln_b, m_sg_w, m_sg_b, m_w_out, m_ple_norm_g, m_w_ple_gate, m_w_ple, m_final_norm_g)))
    v_loc = dict(zip(WEIGHTS, (v_norm_g, v_w_in, v_ssm_a_re, v_ssm_a_im, v_ssm_b_re, v_ssm_b_im, v_ssm_c_re, v_ssm_c_im, v_ssm_d,
                               v_ssm_log_step, v_ssm_w_glu, v_ssm_b_glu, v_dn_conv_w, v_dn_a_log, v_dn_dt_bias, v_dn_norm_g,
                               v_sg_ln_g, v_sg_ln_b, v_sg_w, v_sg_b, v_w_out, v_ple_norm_g, v_w_ple_gate, v_w_ple, v_final_norm_g)))

    gathered = _all_gather([w_loc[n].astype(WIRE[n]) for n in SHARDED_ORDER], "gather_weights")
    full = dict(w_loc)
    for n, g in zip(SHARDED_ORDER, gathered):
        full[n] = _from_gathered(g, SHARDED[n])
    full['ssm_w_glu'] = full['ssm_w_glu'].astype(F32)

    loss_part, grad_x, grads = _local_step(x, p, full, loss_target)

    dest = [_to_dest_blocks(grads[n], SHARDED[n], WIRE[n]) for n in SHARDED_ORDER]
    c = lax.axis_index("c")
    own = [lax.dynamic_index_in_dim(d, c, 0, keepdims=False) for d in dest]
    for_sibling = [lax.dynamic_index_in_dim(d, 1 - c, 0, keepdims=False) for d in dest]
    from_sibling = _pair_exchange(for_sibling, "grads_pair_exchange")
    chip_sums = [_add_pair(a, b, f"grads_pair_sum_{n}") for n, a, b in zip(SHARDED_ORDER, own, from_sibling)]
    by_chip = _chip_exchange(chip_sums, "grads_chip_exchange")
    rep_pack = _pack([grads[n] for n in REPLICATED_ORDER] + [loss_part.reshape(1)])
    (rep_recv,) = _all_gather([rep_pack], "gather_small_grads")

    outs = {k: {} for k in 'gdmv'}
    for n, gk in zip(SHARDED_ORDER, by_chip):
        for k, o in zip('gdmv', _sum_adamw(gk, w_loc[n], m_loc[n], v_loc[n], f"adamw_{n}")):
            outs[k][n] = o
    one = jnp.zeros((1,), F32)
    rep_out = _sum_adamw(rep_recv, _pack([w_loc[n] for n in REPLICATED_ORDER] + [one]),
                         _pack([m_loc[n] for n in REPLICATED_ORDER] + [one]),
                         _pack([v_loc[n] for n in REPLICATED_ORDER] + [one]), "adamw_replicated")
    rep_shapes = [w_loc[n].shape for n in REPLICATED_ORDER] + [(1,)]
    for k, rep_p in zip('gdmv', rep_out):
        outs[k].update(zip(REPLICATED_ORDER + ['loss'], _unpack(rep_p, rep_shapes)))
    loss = outs['g']['loss'].reshape(())
    return (loss, grad_x, *[outs['g'][n] for n in WEIGHTS], *[outs['d'][n] for n in WEIGHTS],
            *[outs['m'][n] for n in WEIGHTS], *[outs['v'][n] for n in WEIGHTS])
```

```python
import functools

import jax
import jax.numpy as jnp
from jax import lax
from jax.experimental import pallas as pl
from jax.experimental.pallas import tpu as pltpu

F32 = jnp.float32
BF16 = jnp.bfloat16
EPS = 1e-6

D = 1024
D_PLE = 256
D_SSM = 256
D_DN = 512
D_SG = 256
G = 16
CG = 16
NS = 64
NRE = G * NS
H = 4
DH = 128
DN_C = 128
SG_C = 128
ZW = 3456
Z_PIECES = (512, 1536, 512, 768, 128)
N_DEV = 8
LANES = 128
PACK_ROWS = 256
VMEM_LIMIT = 56 * 1024 * 1024
ELEMENTWISE_STEP_BYTES = 4 * 1024 * 1024
TB = 256
SG_TB = 512
PREP_TB = 256

ADAM_LR = 0.001
ADAM_B1 = 0.9
ADAM_B2 = 0.999
ADAM_EPS = 1e-08
ADAM_WD = 0.01
ADAM_STEP = 10

MIX_HEAD_START = 3
S5_L = 128
S5_GROUP = 8
S5_SHIFTS = (1, 2, 4)

WEIGHTS = ['norm_g', 'w_in', 'ssm_a_re', 'ssm_a_im', 'ssm_b_re', 'ssm_b_im', 'ssm_c_re', 'ssm_c_im', 'ssm_d',
           'ssm_log_step', 'ssm_w_glu', 'ssm_b_glu', 'dn_conv_w', 'dn_a_log', 'dn_dt_bias', 'dn_norm_g', 'sg_ln_g',
           'sg_ln_b', 'sg_w', 'sg_b', 'w_out', 'ple_norm_g', 'w_ple_gate', 'w_ple', 'final_norm_g']
SHARDED = {'w_in': 2, 'ssm_w_glu': 1, 'dn_conv_w': 2, 'w_out': 1, 'w_ple_gate': 1, 'w_ple': 2}
SHARDED_ORDER = ['w_in', 'ssm_w_glu', 'dn_conv_w', 'w_out', 'w_ple_gate', 'w_ple']
WIRE = {'w_in': BF16, 'ssm_w_glu': BF16, 'dn_conv_w': F32, 'w_out': BF16, 'w_ple_gate': BF16, 'w_ple': BF16}
REPLICATED_ORDER = [n for n in WEIGHTS if n not in SHARDED]


def _cp(*sem):
    return pltpu.CompilerParams(dimension_semantics=sem, vmem_limit_bytes=VMEM_LIMIT)


def _dg(a, b, ca, cb, precision=None):
    return lax.dot_general(a, b, (((ca,), (cb,)), ((), ())), precision=precision, preferred_element_type=F32)


@jax.custom_vjp
def _mm(a, b):
    return _dg(a.astype(BF16), b.astype(BF16), 1, 0)


def _mm_fwd(a, b):
    return _mm(a, b), (a, b)


def _mm_bwd(res, g):
    a, b = res
    gb = g.astype(BF16)
    return _dg(gb, b.astype(BF16), 1, 1), _dg(a.astype(BF16), gb, 0, 0)


_mm.defvjp(_mm_fwd, _mm_bwd)


@jax.custom_vjp
def _mm_nt(a, b):
    return _dg(a.astype(BF16), b.astype(BF16), 1, 1)


def _mm_nt_fwd(a, b):
    return _mm_nt(a, b), (a, b)


def _mm_nt_bwd(res, g):
    a, b = res
    gb = g.astype(BF16)
    return _dg(gb, b.astype(BF16), 1, 0), _dg(gb, a.astype(BF16), 0, 0)


_mm_nt.defvjp(_mm_nt_fwd, _mm_nt_bwd)


@jax.custom_vjp
def _mm_tn(a, b):
    return _dg(a.astype(BF16), b.astype(BF16), 0, 0)


def _mm_tn_fwd(a, b):
    return _mm_tn(a, b), (a, b)


def _mm_tn_bwd(res, g):
    a, b = res
    gb = g.astype(BF16)
    return _dg(b.astype(BF16), gb, 1, 1), _dg(a.astype(BF16), gb, 1, 0)


_mm_tn.defvjp(_mm_tn_fwd, _mm_tn_bwd)


def _split(x, n):
    pieces = []
    for _ in range(n - 1):
        hi = x.astype(BF16)
        pieces.append(hi)
        x = x - hi.astype(F32)
    pieces.append(x.astype(BF16))
    return pieces


def _dg3(a, b, ca, cb):
    a_hi, a_lo = _split(a, 2)
    b_hi, b_lo = _split(b, 2)
    return _dg(a_hi, b_hi, ca, cb) + (_dg(a_hi, b_lo, ca, cb) + _dg(a_lo, b_hi, ca, cb))


@jax.custom_vjp
def _dot3(a, b):
    return _dg3(a, b, 1, 0)


def _dot3_fwd(a, b):
    return _dot3(a, b), (a, b)


def _dot3_bwd(res, g):
    a, b = res
    return _dg3(g, b, 1, 1), _dg3(a, g, 0, 0)


_dot3.defvjp(_dot3_fwd, _dot3_bwd)


def _dg_sel(x, e, cx, ce, x_first):
    eb = e.astype(BF16)
    out = None
    for piece in reversed(_split(x, 3)):
        term = _dg(piece, eb, cx, ce) if x_first else _dg(eb, piece, ce, cx)
        out = term if out is None else out + term
    return out


@jax.custom_vjp
def _sel_r(x, e):
    return _dg_sel(x, e, 1, 0, True)


def _sel_r_fwd(x, e):
    return _sel_r(x, e), e


def _sel_r_bwd(e, g):
    return _dg_sel(g, e, 1, 1, True), jnp.zeros_like(e)


_sel_r.defvjp(_sel_r_fwd, _sel_r_bwd)


@jax.custom_vjp
def _sel_l(e, x):
    return _dg_sel(x, e, 0, 1, False)


def _sel_l_fwd(e, x):
    return _sel_l(e, x), e


def _sel_l_bwd(e, g):
    return jnp.zeros_like(e), _dg_sel(g, e, 0, 0, False)


_sel_l.defvjp(_sel_l_fwd, _sel_l_bwd)


def _rms(x, g):
    return x * lax.rsqrt(jnp.mean(x * x, axis=-1, keepdims=True) + EPS) * g


def _silu(x):
    return x * jax.nn.sigmoid(x)


def _in_proj_fwd(x, g, w, tb, name):
    t = x.shape[0]

    def body(x_ref, g_ref, w_ref, *z_refs):
        h = _rms(x_ref[...], g_ref[...])
        z = jnp.dot(h.astype(BF16), w_ref[...], preferred_element_type=F32)
        off = 0
        for z_ref, n in zip(z_refs, Z_PIECES):
            z_ref[...] = z[:, off:off + n]
            off += n

    row = lambda i: (i, 0)
    full = lambda i: (0, 0)
    return pl.pallas_call(
        body, grid=(t // tb,),
        in_specs=[pl.BlockSpec((tb, D), row), pl.BlockSpec((1, D), full), pl.BlockSpec((D, ZW), full)],
        out_specs=[pl.BlockSpec((tb, n), row) for n in Z_PIECES],
        out_shape=[jax.ShapeDtypeStruct((t, n), F32) for n in Z_PIECES],
        name=name, compiler_params=_cp("parallel"))(x, g, w)


def _in_proj_bwd_dx(x, g, w, dzs, dx_res, tb, name):
    t = x.shape[0]

    def body(x_ref, g_ref, w_ref, d0, d1, d2, d3, d4, dxr_ref, dx_ref, dg_ref):
        @pl.when(pl.program_id(0) == 0)
        def _():
            dg_ref[...] = jnp.zeros_like(dg_ref)

        dz = jnp.concatenate([d0[...], d1[...], d2[...], d3[...], d4[...]], axis=1).astype(BF16)
        dh = _dg(dz, w_ref[...], 1, 1)
        _, vjp = jax.vjp(_rms, x_ref[...], g_ref[...])
        dx, dg = vjp(dh)
        dx_ref[...] = dx + dxr_ref[...]
        dg_ref[...] += dg

    row = lambda i: (i, 0)
    full = lambda i: (0, 0)
    return pl.pallas_call(
        body, grid=(t // tb,),
        in_specs=[pl.BlockSpec((tb, D), row), pl.BlockSpec((1, D), full), pl.BlockSpec((D, ZW), full)]
        + [pl.BlockSpec((tb, n), row) for n in Z_PIECES] + [pl.BlockSpec((tb, D), row)],
        out_specs=[pl.BlockSpec((tb, D), row), pl.BlockSpec((1, D), full)],
        out_shape=[jax.ShapeDtypeStruct((t, D), F32), jax.ShapeDtypeStruct((1, D), F32)],
        name=name, compiler_params=_cp("arbitrary"))(x, g, w, *dzs, dx_res)


def _in_proj_bwd_dw(x, g, dzs, tb, name):
    t = x.shape[0]

    def body(x_ref, g_ref, d0, d1, d2, d3, d4, *dw_refs):
        @pl.when(pl.program_id(0) == 0)
        def _():
            for r in dw_refs:
                r[...] = jnp.zeros_like(r)

        h = _rms(x_ref[...], g_ref[...]).astype(BF16)
        for d_ref, dw_ref in zip((d0, d1, d2, d3, d4), dw_refs):
            dw_ref[...] += _dg(h, d_ref[...].astype(BF16), 0, 0)

    row = lambda i: (i, 0)
    full = lambda i: (0, 0)
    return pl.pallas_call(
        body, grid=(t // tb,),
        in_specs=[pl.BlockSpec((tb, D), row), pl.BlockSpec((1, D), full)] + [pl.BlockSpec((tb, n), row) for n in Z_PIECES],
        out_specs=[pl.BlockSpec((D, n), full) for n in Z_PIECES],
        out_shape=[jax.ShapeDtypeStruct((D, n), F32) for n in Z_PIECES],
        name=name, compiler_params=_cp("arbitrary"))(x, g, *dzs)


def _lam_pow(a_re, a_im, log_step, k):
    step = jnp.exp(log_step)[:, None]
    mag = jnp.exp(k * a_re * step)
    ang = k * a_im * step
    return mag * jnp.cos(ang), mag * jnp.sin(ang)


def _s5_powers(a_re, a_im, log_step):
    def table(ks):
        re, im = _lam_pow(a_re, a_im, log_step, jnp.asarray(ks, F32)[:, None, None])
        return jnp.concatenate([re.reshape(len(ks), NRE), im.reshape(len(ks), NRE)], axis=-1)

    ld = table(S5_SHIFTS).reshape(len(S5_SHIFTS), 1, 2 * NRE)
    return ld, table(range(1, S5_GROUP + 1)), table(range(S5_GROUP, 0, -1))


def _s5_tables(a_re, a_im, b_re, b_im, c_re, c_im, d_skip, log_step):
    lam_re, lam_im = _lam_pow(a_re, a_im, log_step, 1.0)
    den = a_re * a_re + a_im * a_im
    nr, ni = lam_re - 1.0, lam_im
    f_re = (nr * a_re + ni * a_im) / den
    f_im = (ni * a_re - nr * a_im) / den
    bbar_re = f_re[..., None] * b_re - f_im[..., None] * b_im
    bbar_im = f_re[..., None] * b_im + f_im[..., None] * b_re
    eye = jnp.eye(G, dtype=F32)

    def blk_b(bb):
        return (jnp.transpose(bb, (0, 2, 1))[:, :, None, :] * eye[:, None, :, None]).reshape(D_SSM, NRE)

    def blk_c(cc):
        return (jnp.transpose(cc, (0, 2, 1))[:, :, None, :] * eye[:, None, :, None]).reshape(NRE, D_SSM)

    b_blk = jnp.concatenate([blk_b(bbar_re), blk_b(bbar_im)], axis=1)
    c_blk = jnp.concatenate([blk_c(c_re), -blk_c(c_im)], axis=0)
    lam = jnp.concatenate([lam_re.reshape(1, NRE), lam_im.reshape(1, NRE)], axis=-1)
    return b_blk, c_blk, lam, d_skip.reshape(1, D_SSM)


def _group_shift(x, d, up=False):
    r = lax.broadcasted_iota(jnp.int32, x.shape, 0) & (S5_GROUP - 1)
    if up:
        return jnp.where(r < S5_GROUP - d, pltpu.roll(x, x.shape[0] - d, 0), 0.0)
    return jnp.where(r >= d, pltpu.roll(x, d, 0), 0.0)


def _s5_scan_steps(hr, hi, cr, ci, lds, lp):
    for ld, d in zip(lds, S5_SHIFTS):
        lr, li = ld[:, :NRE], ld[:, NRE:]
        sr, si = _group_shift(hr, d), _group_shift(hi, d)
        hr, hi = hr + lr * sr - li * si, hi + lr * si + li * sr
        yield
    pr, pi = lp[:, :NRE], lp[:, NRE:]
    rows_r, rows_i = [], []
    for r in range(S5_L // S5_GROUP):
        br, bi = hr[r * S5_GROUP:(r + 1) * S5_GROUP], hi[r * S5_GROUP:(r + 1) * S5_GROUP]
        br, bi = br + pr * cr - pi * ci, bi + pr * ci + pi * cr
        cr, ci = br[S5_GROUP - 1:S5_GROUP], bi[S5_GROUP - 1:S5_GROUP]
        rows_r.append(br)
        rows_i.append(bi)
        if r % 2:
            yield
    return jnp.concatenate(rows_r, axis=0), jnp.concatenate(rows_i, axis=0)


@jax.custom_vjp
def _known_scan(xr, xi, cr, ci, lam, lds, lp_rev, hr, hi):
    return hr, hi


def _known_scan_fwd(xr, xi, cr, ci, lam, lds, lp_rev, hr, hi):
    return (hr, hi), (cr, ci, lam, lds, lp_rev, hr, hi)


def _known_scan_bwd(res, cts):
    cr, ci, lam, lds, lp_rev, hr, hi = res
    ar, ai = cts
    for ld, d in zip(lds, S5_SHIFTS):
        lr, li = ld[:, :NRE], ld[:, NRE:]
        sr, si = _group_shift(ar, d, up=True), _group_shift(ai, d, up=True)
        ar, ai = ar + lr * sr + li * si, ai + lr * si - li * sr
    qr, qi = lp_rev[:, :NRE], lp_rev[:, NRE:]
    nr, ni = jnp.zeros_like(cr), jnp.zeros_like(ci)
    rows_r, rows_i = [], []
    for r in reversed(range(S5_L // S5_GROUP)):
        br, bi = ar[r * S5_GROUP:(r + 1) * S5_GROUP], ai[r * S5_GROUP:(r + 1) * S5_GROUP]
        br, bi = br + qr * nr + qi * ni, bi + qr * ni - qi * nr
        nr, ni = br[0:1], bi[0:1]
        rows_r.insert(0, br)
        rows_i.insert(0, bi)
    ar, ai = jnp.concatenate(rows_r, axis=0), jnp.concatenate(rows_i, axis=0)
    lr, li = lam[:, :NRE], lam[:, NRE:]
    dcr, dci = lr * nr + li * ni, lr * ni - li * nr
    first = lax.broadcasted_iota(jnp.int32, hr.shape, 0) == 0
    pr = jnp.where(first, cr, pltpu.roll(hr, 1, 0))
    pi = jnp.where(first, ci, pltpu.roll(hi, 1, 0))
    dlam = jnp.concatenate([jnp.sum(ar * pr + ai * pi, axis=0, keepdims=True),
                            jnp.sum(ai * pr - ar * pi, axis=0, keepdims=True)], axis=1)
    return (ar, ai, dcr, dci, dlam, [jnp.zeros_like(ld) for ld in lds], jnp.zeros_like(lp_rev),
            jnp.zeros_like(hr), jnp.zeros_like(hi))


_known_scan.defvjp(_known_scan_fwd, _known_scan_bwd)


def _interleave(*gens, head_start=0):
    results = [None] * len(gens)
    for _ in range(head_start):
        next(gens[-1])
    live = list(enumerate(gens))
    while live:
        still = []
        for idx, gen in live:
            try:
                next(gen)
                still.append((idx, gen))
            except StopIteration as done:
                results[idx] = done.value
        live = still
    return results


def _s5_chunk_gen(u, gate, cr, ci, b_blk, c_blk, lam, dv, wglu, bglu, lds, lp, lp_rev, known_h=None):
    bu = _mm(u, b_blk)
    xr, xi = bu[:, :NRE], bu[:, NRE:]
    yield
    if known_h is None:
        hr, hi = yield from _s5_scan_steps(xr, xi, cr, ci, lds, lp)
    else:
        hr, hi = _known_scan(xr, xi, cr, ci, lam, lds, lp_rev, *known_h)
    y = _mm(jnp.concatenate([hr, hi], axis=1), c_blk) + dv * u
    yield
    y = jax.nn.gelu(y)
    y = y * jax.nn.sigmoid(_mm(y, wglu) + bglu)
    return y * _silu(gate), hr, hi


def _s5_specs(n_s, rev):
    nd = len(S5_SHIFTS)
    if rev:
        row = lambda b, i: (b * n_s + n_s - 1 - i, 0)
        row3 = lambda b, i: (b * n_s + n_s - 1 - i, 0, 0)
    else:
        row = lambda b, i: (b * n_s + i, 0)
        row3 = lambda b, i: (b * n_s + i, 0, 0)
    full = lambda b, i: (0, 0)
    full3 = lambda b, i: (0, 0, 0)
    par_shapes = [(D_SSM, 2 * NRE), (2 * NRE, D_SSM), (1, 2 * NRE), (1, D_SSM), (D_SSM, D_SSM), (1, D_SSM)]
    par = [pl.BlockSpec(sh, full) for sh in par_shapes]
    const = [pl.BlockSpec((nd, 1, 2 * NRE), full3), pl.BlockSpec((S5_GROUP, 2 * NRE), full), pl.BlockSpec((S5_GROUP, 2 * NRE), full)]
    return row, row3, par, par_shapes, const


def _dn_post(c):
    s = _silu(c)
    parts = []
    for j in range(12):
        xj = s[:, j * DH:(j + 1) * DH]
        if j < 8:
            xj = xj * lax.rsqrt(jnp.sum(xj * xj, axis=-1, keepdims=True) + EPS)
        if j < 4:
            xj = xj * (DH ** -0.5)
        parts.append(xj)
    return jnp.concatenate(parts, axis=1)


def _dn_prep_fwd(zq, conv_w8, bl, s, tb, name):
    n_s = s // tb
    hb = tb // 8
    w3 = 3 * D_DN

    def body(cur_ref, prev_ref, w_ref, o_ref):
        i = pl.program_id(1)
        prev = jnp.where(i > 0, prev_ref[...], 0.0)
        ext = jnp.concatenate([prev, cur_ref[...]], axis=0)
        c = jnp.zeros((tb, w3), F32)
        for k in range(4):
            sh = ext if k == 3 else pltpu.roll(ext, 3 - k, 0)
            c = c + w_ref[k:k + 1, :] * sh[8:, :]
        o_ref[...] = _dn_post(c)

    row = lambda b, i: (b * n_s + i, 0)
    prv = lambda b, i: (jnp.maximum((b * n_s + i) * hb - 1, 0), 0)
    full = lambda b, i: (0, 0)
    t = bl * s
    return pl.pallas_call(
        body, grid=(bl, n_s),
        in_specs=[pl.BlockSpec((tb, w3), row), pl.BlockSpec((8, w3), prv), pl.BlockSpec((8, w3), full)],
        out_specs=pl.BlockSpec((tb, w3), row),
        out_shape=jax.ShapeDtypeStruct((t, w3), F32),
        name=name, compiler_params=_cp("parallel", "parallel"))(zq, zq, conv_w8)


def _dn_prep_bwd(zq, dqkv, conv_w8, bl, s, tb, name):
    n_s = s // tb
    hb = tb // 8
    w3 = 3 * D_DN
    n_blk8 = bl * s // 8

    def body(cur_ref, prev_ref, next_ref, d_ref, dnext_ref, w_ref, dz_ref, dw_ref):
        b, i = pl.program_id(0), pl.program_id(1)

        @pl.when((b == 0) & (i == 0))
        def _():
            dw_ref[...] = jnp.zeros_like(dw_ref)

        prev = jnp.where(i > 0, prev_ref[...], 0.0)
        nxt = jnp.where(i < n_s - 1, next_ref[...], 0.0)
        dnxt = jnp.where(i < n_s - 1, dnext_ref[...], 0.0)
        ext = jnp.concatenate([prev, cur_ref[...], nxt], axis=0)
        shifted = [ext if k == 3 else pltpu.roll(ext, 3 - k, 0) for k in range(4)]
        c2 = jnp.zeros((tb + 8, w3), F32)
        for k in range(4):
            c2 = c2 + w_ref[k:k + 1, :] * shifted[k][8:, :]
        dpost = jnp.concatenate([d_ref[...], dnxt], axis=0)
        _, vjp = jax.vjp(_dn_post, c2)
        (dc2,) = vjp(dpost)
        dz = jnp.zeros((tb, w3), F32)
        for k in range(4):
            up = dc2 if k == 3 else pltpu.roll(dc2, tb + 8 - (3 - k), 0)
            dz = dz + w_ref[k:k + 1, :] * up[:tb, :]
            dw_ref[k:k + 1, :] += jnp.sum(dc2[:tb, :] * shifted[k][8:8 + tb, :], axis=0, keepdims=True)
        dz_ref[...] = dz

    row = lambda b, i: (b * n_s + i, 0)
    prv = lambda b, i: (jnp.maximum((b * n_s + i) * hb - 1, 0), 0)
    nxt = lambda b, i: (jnp.minimum((b * n_s + i + 1) * hb, n_blk8 - 1), 0)
    full = lambda b, i: (0, 0)
    t = bl * s
    return pl.pallas_call(
        body, grid=(bl, n_s),
        in_specs=[pl.BlockSpec((tb, w3), row), pl.BlockSpec((8, w3), prv), pl.BlockSpec((8, w3), nxt),
                  pl.BlockSpec((tb, w3), row), pl.BlockSpec((8, w3), nxt), pl.BlockSpec((8, w3), full)],
        out_specs=[pl.BlockSpec((tb, w3), row), pl.BlockSpec((8, w3), full)],
        out_shape=[jax.ShapeDtypeStruct((t, w3), F32), jax.ShapeDtypeStruct((8, w3), F32)],
        name=name, compiler_params=_cp("arbitrary", "arbitrary"))(zq, zq, zq, dqkv, dqkv, conv_w8)


def _unit_lower_inverse_steps(ms):
    c_len = ms[0].shape[0]
    eye = lax.broadcasted_iota(jnp.int32, (c_len, c_len), 0) == lax.broadcasted_iota(jnp.int32, (c_len, c_len), 1)
    ident = jnp.where(eye, 1.0, 0.0)
    ps = ms
    tinvs = [ident - m for m in ms]
    for _ in range(c_len.bit_length() - 2):
        ps = [_dg3(p, p, 1, 0) for p in ps]
        yield
        tinvs = [t + _dg3(t, p, 1, 0) for t, p in zip(tinvs, ps)]
        yield
    return tinvs


@jax.custom_vjp
def _known_inverses(ms, tinvs):
    return tinvs


def _known_inverses_fwd(ms, tinvs):
    return tinvs, tinvs


def _known_inverses_bwd(tinvs, gs):
    return [-_dg3(_dg3(t, g, 0, 0), t, 1, 1) for t, g in zip(tinvs, gs)], [jnp.zeros_like(t) for t in tinvs]


_known_inverses.defvjp(_known_inverses_fwd, _known_inverses_bwd)


def _dn_chunk_gen(qkv, zab, zg, states, alog_e, dt_e, ng, known_tinvs=None):
    c_len = DN_C
    r = lax.broadcasted_iota(jnp.int32, (c_len, c_len), 0)
    c = lax.broadcasted_iota(jnp.int32, (c_len, c_len), 1)
    causal, strict = r >= c, r > c
    tril = jnp.where(causal, 1.0, 0.0)
    rr = lax.broadcasted_iota(jnp.int32, (LANES, D_DN), 0)
    cc = lax.broadcasted_iota(jnp.int32, (LANES, D_DN), 1)
    e_a = jnp.where((cc >= rr * DH) & (cc < rr * DH + DH) & (rr < H), 1.0, 0.0)
    e_b = jnp.where((cc >= (rr - H) * DH) & (cc < (rr - H) * DH + DH) & (rr >= H) & (rr < 2 * H), 1.0, 0.0)
    a_e = _sel_r(zab, e_a)
    b_e = _sel_r(zab, e_b)
    beta = jax.nn.sigmoid(b_e)
    g = -jnp.exp(alog_e) * jax.nn.softplus(a_e + dt_e)
    yield
    gc = _sel_l(tril, g)
    glast = jnp.sum(g, axis=0, keepdims=True)
    eg = jnp.exp(gc)
    ekd = jnp.exp(glast - gc)
    dl = jnp.exp(glast)
    yield
    heads = range(H)
    sls = [slice(h * DH, (h + 1) * DH) for h in heads]
    qs = [qkv[:, h * DH:(h + 1) * DH] for h in heads]
    ks = [qkv[:, D_DN + h * DH:D_DN + (h + 1) * DH] for h in heads]
    vs = [qkv[:, 2 * D_DN + h * DH:2 * D_DN + (h + 1) * DH] for h in heads]
    ccols = [gc[:, sl] for sl in sls]
    decs = [jnp.where(causal, jnp.exp(jnp.where(causal, cl - jnp.transpose(cl), 0.0)), 0.0) for cl in ccols]
    kbs = [k * beta[:, sl] for k, sl in zip(ks, sls)]
    ms = [jnp.where(strict, _mm_nt(kb, k) * dec, 0.0) for kb, k, dec in zip(kbs, ks, decs)]
    yield
    if known_tinvs is None:
        tinvs = yield from _unit_lower_inverse_steps(ms)
    else:
        tinvs = _known_inverses(ms, list(known_tinvs))
    sols = [_dot3(t, jnp.concatenate([v * beta[:, sl], kb * eg[:, sl]], axis=1))
            for t, v, kb, sl in zip(tinvs, vs, kbs, sls)]
    yield
    atts = [_mm_nt(q, k) * dec for q, k, dec in zip(qs, ks, decs)]
    vnews = [sol[:, :DH] - _mm(sol[:, DH:], st) for sol, st in zip(sols, states)]
    yield
    os_ = [_mm(q * eg[:, sl], st) + _mm(att, vn) for q, sl, st, att, vn in zip(qs, sls, states, atts, vnews)]
    yield
    new_states = [st * dl[:, sl] + _mm_tn(k * ekd[:, sl], vn) for st, sl, k, vn in zip(states, sls, ks, vnews)]
    yield
    ys = [_rms(o, ng) * _silu(zg[:, sl]) for o, sl in zip(os_, sls)]
    return jnp.concatenate(ys, axis=1), new_states, tinvs


def _dn_scan_specs(n_c, rev):
    if rev:
        row = lambda b, i: (b * n_c + n_c - 1 - i, 0)
        row4 = lambda b, i: (b * n_c + n_c - 1 - i, 0, 0, 0)
    else:
        row = lambda b, i: (b * n_c + i, 0)
        row4 = lambda b, i: (b * n_c + i, 0, 0, 0)
    full = lambda b, i: (0, 0)
    par = [pl.BlockSpec((1, D_DN), full), pl.BlockSpec((1, D_DN), full), pl.BlockSpec((1, DH), full)]
    par_shapes = [(1, D_DN), (1, D_DN), (1, DH)]
    return row, row4, par, par_shapes


def _mix_fwd(zs, s5_par, s5_const, qkv, zab, zg, dn_par, bl, s, name):
    assert S5_L == DN_C
    n_c = s // DN_C
    nd = len(S5_SHIFTS)
    row, row3, s5_specs, _, s5_const_specs = _s5_specs(n_c, False)
    _, row4, dn_specs, _ = _dn_scan_specs(n_c, False)

    def body(z_ref, b_ref, c_ref, lam_ref, dv_ref, wg_ref, bg_ref, ld_ref, lp_ref, lpr_ref,
             q_ref, ab_ref, zg_ref, al_ref, dt_ref, ng_ref,
             ys_ref, car_ref, h_ref, yd_ref, st_ref, ti_ref, cs, ssc):
        @pl.when(pl.program_id(1) == 0)
        def _():
            cs[...] = jnp.zeros_like(cs)
            ssc[...] = jnp.zeros_like(ssc)

        c = cs[...]
        car_ref[0] = c
        sts = [ssc[h] for h in range(H)]
        for h in range(H):
            st_ref[0, h] = sts[h]
        z = z_ref[...]
        (y_s, hr, hi), (y_d, new_sts, tinvs) = _interleave(
            _s5_chunk_gen(z[:, :D_SSM], z[:, D_SSM:], c[:, :NRE], c[:, NRE:], b_ref[...], c_ref[...], lam_ref[...],
                          dv_ref[...], wg_ref[...], bg_ref[...], [ld_ref[k] for k in range(nd)], lp_ref[...], lpr_ref[...]),
            _dn_chunk_gen(q_ref[...], ab_ref[...], zg_ref[...], sts, al_ref[...], dt_ref[...], ng_ref[...]),
            head_start=MIX_HEAD_START)
        ys_ref[...] = y_s
        h_ref[:, :NRE] = hr
        h_ref[:, NRE:] = hi
        cs[:, :NRE] = hr[S5_L - 1:S5_L]
        cs[:, NRE:] = hi[S5_L - 1:S5_L]
        yd_ref[...] = y_d
        for h in range(H):
            ssc[h] = new_sts[h]
            ti_ref[0, h] = tinvs[h]

    t = bl * s
    per_chunk = jax.ShapeDtypeStruct((bl * n_c, H, DH, DH), F32)
    return pl.pallas_call(
        body, grid=(bl, n_c),
        in_specs=[pl.BlockSpec((S5_L, 2 * D_SSM), row)] + s5_specs + s5_const_specs
        + [pl.BlockSpec((DN_C, 3 * D_DN), row), pl.BlockSpec((DN_C, LANES), row), pl.BlockSpec((DN_C, D_DN), row)] + dn_specs,
        out_specs=[pl.BlockSpec((S5_L, D_SSM), row), pl.BlockSpec((1, 1, 2 * NRE), row3), pl.BlockSpec((S5_L, 2 * NRE), row),
                   pl.BlockSpec((DN_C, D_DN), row), pl.BlockSpec((1, H, DH, DH), row4), pl.BlockSpec((1, H, DH, DH), row4)],
        out_shape=[jax.ShapeDtypeStruct((t, D_SSM), F32), jax.ShapeDtypeStruct((bl * n_c, 1, 2 * NRE), F32),
                   jax.ShapeDtypeStruct((t, 2 * NRE), F32), jax.ShapeDtypeStruct((t, D_DN), F32), per_chunk, per_chunk],
        scratch_shapes=[pltpu.VMEM((1, 2 * NRE), F32), pltpu.VMEM((H, DH, DH), F32)],
        name=name, compiler_params=_cp("arbitrary", "arbitrary"))(zs, *s5_par, *s5_const, qkv, zab, zg, *dn_par)


def _mix_bwd(zs, carries, h_all, dy_s, s5_par, s5_const, qkv, zab, zg, states, tinvs, dy_d, dn_par, bl, s, name):
    n_c = s // DN_C
    nd = len(S5_SHIFTS)
    row, row3, s5_specs, s5_shapes, s5_const_specs = _s5_specs(n_c, True)
    _, row4, dn_specs, dn_shapes = _dn_scan_specs(n_c, True)

    def both(u, gate, cr, ci, b_blk, c_blk, lam, dv, wglu, bglu, lds, lp, lp_rev, hr, hi, q, ab, zgate, sts, known,
             alog_e, dt_e, ng):
        (y_s, hr, hi), (y_d, new_sts, _) = _interleave(
            _s5_chunk_gen(u, gate, cr, ci, b_blk, c_blk, lam, dv, wglu, bglu, lds, lp, lp_rev, known_h=(hr, hi)),
            _dn_chunk_gen(q, ab, zgate, sts, alog_e, dt_e, ng, known_tinvs=known), head_start=MIX_HEAD_START)
        return y_s, hr[S5_L - 1:S5_L], hi[S5_L - 1:S5_L], y_d, new_sts

    def body(z_ref, car_ref, h_ref, dys_ref, b_ref, c_ref, lam_ref, dv_ref, wg_ref, bg_ref, ld_ref, lp_ref, lpr_ref,
             q_ref, ab_ref, zg_ref, st_ref, ti_ref, dyd_ref, al_ref, dt_ref, ng_ref,
             dz_ref, db_ref, dc_ref, dlam_ref, ddv_ref, dwg_ref, dbg_ref,
             dq_ref, dab_ref, dzg_ref, dal_ref, ddt_ref, dng_ref, dcs, dsc):
        accs = (db_ref, dc_ref, dlam_ref, ddv_ref, dwg_ref, dbg_ref, dal_ref, ddt_ref, dng_ref)

        @pl.when((pl.program_id(0) == 0) & (pl.program_id(1) == 0))
        def _():
            for r in accs:
                r[...] = jnp.zeros_like(r)

        @pl.when(pl.program_id(1) == 0)
        def _():
            dcs[...] = jnp.zeros_like(dcs)
            dsc[...] = jnp.zeros_like(dsc)

        z = z_ref[...]
        c = car_ref[0]
        _, vjp = jax.vjp(both, z[:, :D_SSM], z[:, D_SSM:], c[:, :NRE], c[:, NRE:], b_ref[...], c_ref[...], lam_ref[...],
                         dv_ref[...], wg_ref[...], bg_ref[...], [ld_ref[k] for k in range(nd)], lp_ref[...], lpr_ref[...],
                         h_ref[:, :NRE], h_ref[:, NRE:],
                         q_ref[...], ab_ref[...], zg_ref[...], [st_ref[0, h] for h in range(H)],
                         [ti_ref[0, h] for h in range(H)], al_ref[...], dt_ref[...], ng_ref[...])
        dc = dcs[...]
        (du, dgate, dcr, dci, dbb, dcb, dlam, ddvb, dwgb, dbgb, _, _, _, _, _,
         dq, dab, dzg, dsts, _, dal, ddt, dng) = vjp((dys_ref[...], dc[:, :NRE], dc[:, NRE:], dyd_ref[...],
                                                     [dsc[h] for h in range(H)]))
        dz_ref[...] = jnp.concatenate([du, dgate], axis=1)
        dcs[:, :NRE] = dcr
        dcs[:, NRE:] = dci
        dq_ref[...] = dq
        dab_ref[...] = dab
        dzg_ref[...] = dzg
        for h in range(H):
            dsc[h] = dsts[h]
        for r, ct in zip(accs, (dbb, dcb, dlam, ddvb, dwgb, dbgb, dal, ddt, dng)):
            r[...] += ct

    t = bl * s
    chunk4 = pl.BlockSpec((1, H, DH, DH), row4)
    outs = pl.pallas_call(
        body, grid=(bl, n_c),
        in_specs=[pl.BlockSpec((S5_L, 2 * D_SSM), row), pl.BlockSpec((1, 1, 2 * NRE), row3), pl.BlockSpec((S5_L, 2 * NRE), row),
                  pl.BlockSpec((S5_L, D_SSM), row)] + s5_specs + s5_const_specs
        + [pl.BlockSpec((DN_C, 3 * D_DN), row), pl.BlockSpec((DN_C, LANES), row), pl.BlockSpec((DN_C, D_DN), row),
           chunk4, chunk4, pl.BlockSpec((DN_C, D_DN), row)] + dn_specs,
        out_specs=[pl.BlockSpec((S5_L, 2 * D_SSM), row)] + s5_specs
        + [pl.BlockSpec((DN_C, 3 * D_DN), row), pl.BlockSpec((DN_C, LANES), row), pl.BlockSpec((DN_C, D_DN), row)] + dn_specs,
        out_shape=[jax.ShapeDtypeStruct((t, 2 * D_SSM), F32)] + [jax.ShapeDtypeStruct(sh, F32) for sh in s5_shapes]
        + [jax.ShapeDtypeStruct((t, 3 * D_DN), F32), jax.ShapeDtypeStruct((t, LANES), F32), jax.ShapeDtypeStruct((t, D_DN), F32)]
        + [jax.ShapeDtypeStruct(sh, F32) for sh in dn_shapes],
        scratch_shapes=[pltpu.VMEM((1, 2 * NRE), F32), pltpu.VMEM((H, DH, DH), F32)],
        name=name, compiler_params=_cp("arbitrary", "arbitrary"))(
            zs, carries, h_all, dy_s, *s5_par, *s5_const, qkv, zab, zg, states, tinvs, dy_d, *dn_par)
    return outs[:7], outs[7:]


def _sg_fn(n_chunk):
    def f(z, lng, lnb, w, bsp_t):
        u = jax.nn.gelu(z[:, :D_SG])
        v = jax.nn.gelu(z[:, D_SG:2 * D_SG])
        gate = z[:, 2 * D_SG:]
        xc = v - jnp.mean(v, axis=-1, keepdims=True)
        vn = xc * lax.rsqrt(jnp.mean(xc * xc, axis=-1, keepdims=True) + EPS) * lng + lnb
        r = lax.broadcasted_iota(jnp.int32, (SG_C, SG_C), 0)
        c = lax.broadcasted_iota(jnp.int32, (SG_C, SG_C), 1)
        causal = r >= c
        first_half = c < SG_C // 2
        rr = lax.broadcasted_iota(jnp.int32, (LANES, D_SG), 0)
        cc = lax.broadcasted_iota(jnp.int32, (LANES, D_SG), 1)
        expand = jnp.where((cc >= rr * 64) & (cc < rr * 64 + 64) & (rr < 4), 1.0, 0.0)
        bias = _sel_r(bsp_t, expand)
        wm = [jnp.where(causal, w[h], 0.0) for h in range(4)]
        rows = []
        for ci in range(n_chunk):
            vc = vn[ci * SG_C:(ci + 1) * SG_C]
            pairs = []
            for pr in range(2):
                vp = vc[:, pr * LANES:(pr + 1) * LANES]
                pairs.append(jnp.where(first_half, _mm(wm[2 * pr], vp), _mm(wm[2 * pr + 1], vp)))
            rows.append(jnp.concatenate(pairs, axis=1) + bias)
        sp = jnp.concatenate(rows, axis=0) if n_chunk > 1 else rows[0]
        return u * sp * _silu(gate)

    return f


def _sg_specs():
    full = lambda i: (0, 0)
    full3 = lambda i: (0, 0, 0)
    par = [pl.BlockSpec((1, D_SG), full), pl.BlockSpec((1, D_SG), full), pl.BlockSpec((4, SG_C, SG_C), full3),
           pl.BlockSpec((SG_C, LANES), full)]
    par_shapes = [(1, D_SG), (1, D_SG), (4, SG_C, SG_C), (SG_C, LANES)]
    return par, par_shapes


def _sg_fwd(zsg, params, tb, name):
    t = zsg.shape[0]
    f = _sg_fn(tb // SG_C)
    par, _ = _sg_specs()

    def body(z_ref, g_ref, b_ref, w_ref, bs_ref, y_ref):
        y_ref[...] = f(z_ref[...], g_ref[...], b_ref[...], w_ref[...], bs_ref[...])

    row = lambda i: (i, 0)
    return pl.pallas_call(
        body, grid=(t // tb,), in_specs=[pl.BlockSpec((tb, 3 * D_SG), row)] + par,
        out_specs=pl.BlockSpec((tb, D_SG), row), out_shape=jax.ShapeDtypeStruct((t, D_SG), F32),
        name=name, compiler_params=_cp("parallel"))(zsg, *params)


def _sg_bwd(zsg, dy, params, tb, name):
    t = zsg.shape[0]
    f = _sg_fn(tb // SG_C)
    par, par_shapes = _sg_specs()

    def body(z_ref, dy_ref, g_ref, b_ref, w_ref, bs_ref, dz_ref, dg_ref, db_ref, dw_ref, dbs_ref):
        accs = (dg_ref, db_ref, dw_ref, dbs_ref)

        @pl.when(pl.program_id(0) == 0)
        def _():
            for r in accs:
                r[...] = jnp.zeros_like(r)

        _, vjp = jax.vjp(f, z_ref[...], g_ref[...], b_ref[...], w_ref[...], bs_ref[...])
        cts = vjp(dy_ref[...])
        dz_ref[...] = cts[0]
        for r, ct in zip(accs, cts[1:]):
            r[...] += ct

    row = lambda i: (i, 0)
    return pl.pallas_call(
        body, grid=(t // tb,), in_specs=[pl.BlockSpec((tb, 3 * D_SG), row), pl.BlockSpec((tb, D_SG), row)] + par,
        out_specs=[pl.BlockSpec((tb, 3 * D_SG), row)] + par,
        out_shape=[jax.ShapeDtypeStruct((t, 3 * D_SG), F32)] + [jax.ShapeDtypeStruct(sh, F32) for sh in par_shapes],
        name=name, compiler_params=_cp("arbitrary"))(zsg, dy, *params)


def _out_fwd(x, ys, p, w_out, pg, w_gate, w_ple, tb, name):
    t = x.shape[0]

    def body(x_ref, y0, y1, y2, p_ref, wo_ref, pg_ref, wg_ref, wp_ref, o_ref):
        y = jnp.concatenate([y0[...], y1[...], y2[...]], axis=1).astype(BF16)
        x1 = x_ref[...] + jnp.dot(y, wo_ref[...], preferred_element_type=F32)
        hn = _rms(x1, pg_ref[...]).astype(BF16)
        gate = jax.nn.sigmoid(jnp.dot(hn, wg_ref[...], preferred_element_type=F32))
        pp = jnp.dot(p_ref[...].astype(BF16), wp_ref[...], preferred_element_type=F32)
        o_ref[...] = x1 + gate * pp

    row = lambda i: (i, 0)
    full = lambda i: (0, 0)
    return pl.pallas_call(
        body, grid=(t // tb,),
        in_specs=[pl.BlockSpec((tb, D), row), pl.BlockSpec((tb, D_SSM), row), pl.BlockSpec((tb, D_DN), row),
                  pl.BlockSpec((tb, D_SG), row), pl.BlockSpec((tb, D_PLE), row), pl.BlockSpec((D, D), full),
                  pl.BlockSpec((1, D), full), pl.BlockSpec((D, D), full), pl.BlockSpec((D_PLE, D), full)],
        out_specs=pl.BlockSpec((tb, D), row), out_shape=jax.ShapeDtypeStruct((t, D), F32),
        name=name, compiler_params=_cp("parallel"))(x, *ys, p, w_out, pg, w_gate, w_ple)


def _out_bwd(x, ys, p, dx2, w_out, pg, w_gate, w_ple, tb, name):
    t = x.shape[0]

    def body(x_ref, y0, y1, y2, p_ref, d_ref, wo_ref, pg_ref, wg_ref, wp_ref,
             dx_ref, dy0, dy1, dy2, dwo_ref, dpg_ref, dwg_ref, dwp_ref):
        accs = (dwo_ref, dpg_ref, dwg_ref, dwp_ref)

        @pl.when(pl.program_id(0) == 0)
        def _():
            for r in accs:
                r[...] = jnp.zeros_like(r)

        y = jnp.concatenate([y0[...], y1[...], y2[...]], axis=1).astype(BF16)
        x1 = x_ref[...] + jnp.dot(y, wo_ref[...], preferred_element_type=F32)
        hn, rms_vjp = jax.vjp(_rms, x1, pg_ref[...])
        hb = hn.astype(BF16)
        gate = jax.nn.sigmoid(jnp.dot(hb, wg_ref[...], preferred_element_type=F32))
        pb = p_ref[...].astype(BF16)
        pp = jnp.dot(pb, wp_ref[...], preferred_element_type=F32)
        d2 = d_ref[...]
        dpp = (d2 * gate).astype(BF16)
        dlog = (d2 * pp * gate * (1.0 - gate)).astype(BF16)
        dwp_ref[...] += _dg(pb, dpp, 0, 0)
        dwg_ref[...] += _dg(hb, dlog, 0, 0)
        dx1_n, dpg = rms_vjp(_dg(dlog, wg_ref[...], 1, 1))
        dpg_ref[...] += dpg
        dx1 = d2 + dx1_n
        dx_ref[...] = dx1
        db = dx1.astype(BF16)
        dwo_ref[...] += _dg(y, db, 0, 0)
        dy = _dg(db, wo_ref[...], 1, 1)
        dy0[...] = dy[:, :D_SSM]
        dy1[...] = dy[:, D_SSM:D_SSM + D_DN]
        dy2[...] = dy[:, D_SSM + D_DN:]

    row = lambda i: (i, 0)
    full = lambda i: (0, 0)
    acts = [pl.BlockSpec((tb, D), row), pl.BlockSpec((tb, D_SSM), row), pl.BlockSpec((tb, D_DN), row), pl.BlockSpec((tb, D_SG), row)]
    wts = [pl.BlockSpec((D, D), full), pl.BlockSpec((1, D), full), pl.BlockSpec((D, D), full), pl.BlockSpec((D_PLE, D), full)]
    return pl.pallas_call(
        body, grid=(t // tb,),
        in_specs=acts + [pl.BlockSpec((tb, D_PLE), row), pl.BlockSpec((tb, D), row)] + wts,
        out_specs=acts + wts,
        out_shape=[jax.ShapeDtypeStruct((t, n), F32) for n in (D, D_SSM, D_DN, D_SG)]
        + [jax.ShapeDtypeStruct(sh, F32) for sh in ((D, D), (1, D), (D, D), (D_PLE, D))],
        name=name, compiler_params=_cp("arbitrary"))(x, *ys, p, dx2, w_out, pg, w_gate, w_ple)


def _loss_head(x, fg, target, tb, name):
    t = x.shape[0]

    def body(x_ref, g_ref, t_ref, dx_ref, dg_ref, loss_ref):
        @pl.when(pl.program_id(0) == 0)
        def _():
            dg_ref[...] = jnp.zeros_like(dg_ref)
            loss_ref[...] = jnp.zeros_like(loss_ref)

        y, vjp = jax.vjp(_rms, x_ref[...], g_ref[...])
        err = y - t_ref[...]
        loss_ref[...] += jnp.zeros_like(loss_ref) + 0.5 * jnp.sum(err * err) / D
        dx, dg = vjp(err / D)
        dx_ref[...] = dx
        dg_ref[...] += dg

    row = lambda i: (i, 0)
    full = lambda i: (0, 0)
    return pl.pallas_call(
        body, grid=(t // tb,),
        in_specs=[pl.BlockSpec((tb, D), row), pl.BlockSpec((1, D), full), pl.BlockSpec((tb, D), row)],
        out_specs=[pl.BlockSpec((tb, D), row), pl.BlockSpec((1, D), full), pl.BlockSpec((1, LANES), full)],
        out_shape=[jax.ShapeDtypeStruct((t, D), F32), jax.ShapeDtypeStruct((1, D), F32), jax.ShapeDtypeStruct((1, LANES), F32)],
        name=name, compiler_params=_cp("arbitrary"))(x, fg, target)


def _hbm_specs(n):
    return [pl.BlockSpec(memory_space=pl.ANY)] * n


def _all_gather(blocks, name):
    n = len(blocks)

    def body(*refs):
        ins, outs = refs[:n], refs[n:2 * n]
        send_sems, recv_sems, local_sems = refs[2 * n:]
        x, y, c = lax.axis_index("x"), lax.axis_index("y"), lax.axis_index("c")
        me, sibling = (x, y, c), (x, y, 1 - c)
        chips = [(1 - x, y), (x, 1 - y), (1 - x, 1 - y)]

        def slot(a, px, py, pc):
            return outs[a].at[4 * px + 2 * py + pc]

        def copy(a, k, blk, to, src=None):
            return pltpu.make_async_remote_copy(
                src_ref=slot(a, *blk) if src is None else src, dst_ref=slot(a, *blk),
                send_sem=send_sems.at[7 * a + k], recv_sem=recv_sems.at[7 * a + k],
                device_id=to, device_id_type=pl.DeviceIdType.MESH)

        mines = [pltpu.make_async_copy(ins[a], slot(a, *me), local_sems.at[a]) for a in range(n)]
        for cp in mines:
            cp.start()
        first = []
        for a in range(n):
            first.append(copy(a, 0, me, sibling, src=ins[a]))
            first += [copy(a, 1 + j, me, (*chip, c), src=ins[a]) for j, chip in enumerate(chips)]
        for cp in first:
            cp.start()
        passed = []
        for j, chip in enumerate(chips):
            for a in range(n):
                copy(a, 1 + j, (*chip, c), me).wait_recv()
                onward = copy(a, 4 + j, (*chip, c), sibling)
                onward.start()
                passed.append(onward)
        for a in range(n):
            copy(a, 0, sibling, me).wait_recv()
        for j, chip in enumerate(chips):
            for a in range(n):
                copy(a, 4 + j, (*chip, 1 - c), me).wait_recv()
        for cp in first + passed:
            cp.wait_send()
        for cp in mines:
            cp.wait()

    return pl.pallas_call(
        body, out_shape=[jax.ShapeDtypeStruct((N_DEV, *b.shape), b.dtype) for b in blocks],
        in_specs=_hbm_specs(n), out_specs=_hbm_specs(n),
        scratch_shapes=[pltpu.SemaphoreType.DMA((7 * n,)), pltpu.SemaphoreType.DMA((7 * n,)), pltpu.SemaphoreType.DMA((n,))],
        name=name)(*blocks)


def _pair_exchange(gs, name):
    n = len(gs)

    def body(*refs):
        ins, recvs = refs[:n], refs[n:2 * n]
        send_sems, recv_sems = refs[2 * n:]
        x, y, c = lax.axis_index("x"), lax.axis_index("y"), lax.axis_index("c")
        remote = [pltpu.make_async_remote_copy(
            src_ref=ins[a], dst_ref=recvs[a], send_sem=send_sems.at[a], recv_sem=recv_sems.at[a],
            device_id=(x, y, 1 - c), device_id_type=pl.DeviceIdType.MESH) for a in range(n)]
        for cp in remote:
            cp.start()
        for cp in remote:
            cp.wait_send()
            cp.wait_recv()

    return pl.pallas_call(
        body, out_shape=[jax.ShapeDtypeStruct(g.shape, g.dtype) for g in gs], in_specs=_hbm_specs(n), out_specs=_hbm_specs(n),
        scratch_shapes=[pltpu.SemaphoreType.DMA((n,)), pltpu.SemaphoreType.DMA((n,))],
        name=name)(*gs)


def _chip_exchange(ps, name):
    n = len(ps)

    def body(*refs):
        ins, outs = refs[:n], refs[n:2 * n]
        send_sems, recv_sems, local_sems = refs[2 * n:]
        x, y, c = lax.axis_index("x"), lax.axis_index("y"), lax.axis_index("c")
        my_chip = 2 * x + y
        local = [pltpu.make_async_copy(ins[a].at[my_chip], outs[a].at[my_chip], local_sems.at[a]) for a in range(n)]
        remote = []
        for j in range(1, 4):
            px = 1 - x if j & 2 else x
            py = 1 - y if j & 1 else y
            for a in range(n):
                remote.append(pltpu.make_async_remote_copy(
                    src_ref=ins[a].at[2 * px + py], dst_ref=outs[a].at[my_chip],
                    send_sem=send_sems.at[3 * a + j - 1], recv_sem=recv_sems.at[3 * a + j - 1],
                    device_id=(px, py, c), device_id_type=pl.DeviceIdType.MESH))
        for cp in local + remote:
            cp.start()
        for cp in remote:
            cp.wait_send()
            cp.wait_recv()
        for cp in local:
            cp.wait()

    return pl.pallas_call(
        body, out_shape=[jax.ShapeDtypeStruct(q.shape, q.dtype) for q in ps], in_specs=_hbm_specs(n), out_specs=_hbm_specs(n),
        scratch_shapes=[pltpu.SemaphoreType.DMA((3 * n,)), pltpu.SemaphoreType.DMA((3 * n,)), pltpu.SemaphoreType.DMA((n,))],
        name=name)(*ps)


def _row_block(rows, bytes_per_row):
    best = None
    for rb in range(16, rows + 1, 16):
        if rows % rb == 0 and rb * bytes_per_row <= ELEMENTWISE_STEP_BYTES:
            best = rb
    return rows if best is None else best


def _add_pair(own, recv, name):
    shape = own.shape
    last = shape[-1]
    rows = own.size // last
    rb = _row_block(rows, 3 * 4 * (-(-last // LANES) * LANES))

    def body(a_ref, b_ref, o_ref):
        o_ref[...] = (a_ref[...].astype(F32) + b_ref[...].astype(F32)).astype(o_ref.dtype)

    row = lambda i: (i, 0)
    out = pl.pallas_call(
        body, grid=(rows // rb,), in_specs=[pl.BlockSpec((rb, last), row)] * 2, out_specs=pl.BlockSpec((rb, last), row),
        out_shape=jax.ShapeDtypeStruct((rows, last), own.dtype), name=name,
        compiler_params=_cp("parallel"))(own.reshape(rows, last), recv.reshape(rows, last))
    return out.reshape(shape)


def _sum_adamw(gk, w, m, v, name):
    shape = w.shape
    n_part = gk.shape[0]
    last = shape[-1]
    rows = w.size // last
    rb = _row_block(rows, (n_part + 7) * 4 * (-(-last // LANES) * LANES))

    def body(g_ref, w_ref, m_ref, v_ref, go_ref, d_ref, mo_ref, vo_ref):
        g = g_ref[0].astype(F32)
        for k in range(1, n_part):
            g = g + g_ref[k].astype(F32)
        mn = ADAM_B1 * m_ref[...] + (1.0 - ADAM_B1) * g
        vn = ADAM_B2 * v_ref[...] + (1.0 - ADAM_B2) * jnp.square(g)
        m_hat = mn / (1.0 - ADAM_B1 ** ADAM_STEP)
        v_hat = vn / (1.0 - ADAM_B2 ** ADAM_STEP)
        go_ref[...] = g
        d_ref[...] = -ADAM_LR * (m_hat / (jnp.sqrt(v_hat) + ADAM_EPS) + ADAM_WD * w_ref[...])
        mo_ref[...] = mn
        vo_ref[...] = vn

    row = lambda i: (i, 0)
    outs = pl.pallas_call(
        body, grid=(rows // rb,),
        in_specs=[pl.BlockSpec((n_part, rb, last), lambda i: (0, i, 0))] + [pl.BlockSpec((rb, last), row)] * 3,
        out_specs=[pl.BlockSpec((rb, last), row)] * 4,
        out_shape=[jax.ShapeDtypeStruct((rows, last), F32)] * 4,
        name=name, compiler_params=_cp("parallel"))(gk.reshape(n_part, rows, last), *[a.reshape(rows, last) for a in (w, m, v)])
    return [o.reshape(shape) for o in outs]


def _seg_rows(shape):
    n = 1
    for d in shape:
        n *= d
    return -(-n // (8 * LANES)) * 8


def _pack(arrs):
    segs = []
    for a in arrs:
        r = _seg_rows(a.shape)
        segs.append(jnp.pad(a.reshape(-1).astype(F32), (0, r * LANES - a.size)).reshape(r, LANES))
    rows = sum(s.shape[0] for s in segs)
    total = -(-rows // PACK_ROWS) * PACK_ROWS
    if total > rows:
        segs.append(jnp.zeros((total - rows, LANES), F32))
    return jnp.concatenate(segs, axis=0)


def _unpack(pack, shapes):
    out, off = [], 0
    for sh in shapes:
        r = _seg_rows(sh)
        n = 1
        for d in sh:
            n *= d
        out.append(pack[off:off + r].reshape(-1)[:n].reshape(sh))
        off += r
    return out


def _to_dest_blocks(full, axis, dtype):
    sh = list(full.shape)
    sh[axis:axis + 1] = [N_DEV // 2, 2, sh[axis] // N_DEV]
    return jnp.moveaxis(full.reshape(sh), (axis, axis + 1), (1, 0)).astype(dtype)


def _from_gathered(g, axis):
    m = jnp.moveaxis(g, 0, axis)
    sh = list(m.shape)
    sh[axis:axis + 2] = [sh[axis] * sh[axis + 1]]
    return m.reshape(sh)


def _reorder_w_in(w):
    return jnp.concatenate([w[:, :2048], w[:, 2056:3336], w[:, 2048:2056], jnp.zeros((D, ZW - 3336), w.dtype)], axis=1)


def _restore_dw_in(dw):
    return jnp.concatenate([dw[:, :2048], dw[:, 3328:3336], dw[:, 2048:3328]], axis=1)


def _local_step(x, p, wts, target):
    bl, s, _ = x.shape
    t = bl * s
    depth = p.shape[0]
    tb, sg_tb, prep_tb = TB, SG_TB, PREP_TB
    xs = [x.reshape(t, D)]
    saved = []
    for i in range(depth):
        li = f"l{i}"
        ng = wts['norm_g'][i].reshape(1, D)
        w_in = _reorder_w_in(wts['w_in'][i]).astype(BF16)
        s5_par_in = (wts['ssm_a_re'][i], wts['ssm_a_im'][i], wts['ssm_b_re'][i], wts['ssm_b_im'][i],
                     wts['ssm_c_re'][i], wts['ssm_c_im'][i], wts['ssm_d'][i], wts['ssm_log_step'][i])
        tabs, tab_vjp = jax.vjp(_s5_tables, *s5_par_in)
        s5_par = (*tabs, wts['ssm_w_glu'][i], wts['ssm_b_glu'][i].reshape(1, D_SSM))
        s5_const = _s5_powers(wts['ssm_a_re'][i], wts['ssm_a_im'][i], wts['ssm_log_step'][i])
        conv8 = jnp.pad(wts['dn_conv_w'][i], ((0, 4), (0, 0)))
        dn_par = (jnp.repeat(wts['dn_a_log'][i], DH).reshape(1, D_DN), jnp.repeat(wts['dn_dt_bias'][i], DH).reshape(1, D_DN),
                  wts['dn_norm_g'][i].reshape(1, DH))
        sg_par = (wts['sg_ln_g'][i].reshape(1, D_SG), wts['sg_ln_b'][i].reshape(1, D_SG), wts['sg_w'][i],
                  jnp.pad(jnp.transpose(wts['sg_b'][i]), ((0, 0), (0, LANES - 4))))
        out_par = (wts['w_out'][i].astype(BF16), wts['ple_norm_g'][i].reshape(1, D), wts['w_ple_gate'][i].astype(BF16),
                   wts['w_ple'][i].astype(BF16))
        pi = p[i].reshape(t, D_PLE)

        z_ssm, z_qkv, z_gdn, z_sg, z_ab = _in_proj_fwd(xs[i], ng, w_in, tb, f"in_proj_fwd_{li}")
        qkvn = _dn_prep_fwd(z_qkv, conv8, bl, s, prep_tb, f"dn_prep_fwd_{li}")
        y_ssm, carries, h_all, y_dn, states, tinvs = _mix_fwd(z_ssm, s5_par, s5_const, qkvn, z_ab, z_gdn, dn_par, bl, s,
                                                              f"mix_fwd_{li}")
        y_sg = _sg_fwd(z_sg, sg_par, sg_tb, f"sg_fwd_{li}")
        ys = (y_ssm, y_dn, y_sg)
        xs.append(_out_fwd(xs[i], ys, pi, *out_par, tb, f"out_fwd_{li}"))
        saved.append(dict(ng=ng, w_in=w_in, tab_vjp=tab_vjp, s5_par=s5_par, s5_const=s5_const, conv8=conv8, dn_par=dn_par,
                          sg_par=sg_par, out_par=out_par, pi=pi, z=(z_ssm, z_qkv, z_gdn, z_sg, z_ab), carries=carries,
                          h_all=h_all, qkvn=qkvn,
                          states=states, tinvs=tinvs, ys=ys))

    dx, dfg, loss_vec = _loss_head(xs[depth], wts['final_norm_g'].reshape(1, D), target.reshape(t, D), tb, "loss_head")
    grads = {n: [None] * depth for n in WEIGHTS if n != 'final_norm_g'}
    grads['final_norm_g'] = dfg.reshape(D)
    for i in reversed(range(depth)):
        li = f"l{i}"
        sv = saved[i]
        z_ssm, z_qkv, z_gdn, z_sg, z_ab = sv['z']
        dx_res, dy_ssm, dy_dn, dy_sg, dwo, dpg, dwg, dwp = _out_bwd(xs[i], sv['ys'], sv['pi'], dx, *sv['out_par'], tb, f"out_bwd_{li}")
        dz_sg, dlng, dlnb, dsgw, dbsp = _sg_bwd(z_sg, dy_sg, sv['sg_par'], sg_tb, f"sg_bwd_{li}")
        (dz_ssm, dbb, dcb, dlam, ddv, dwglu, dbglu), (dqkvn, dz_ab, dz_gdn, dal, ddt, dng) = _mix_bwd(
            z_ssm, sv['carries'], sv['h_all'], dy_ssm, sv['s5_par'], sv['s5_const'], sv['qkvn'], z_ab, z_gdn, sv['states'],
            sv['tinvs'], dy_dn, sv['dn_par'], bl, s, f"mix_bwd_{li}")
        dz_qkv, dconv = _dn_prep_bwd(z_qkv, dqkvn, sv['conv8'], bl, s, prep_tb, f"dn_prep_bwd_{li}")
        dzs = (dz_ssm, dz_qkv, dz_gdn, dz_sg, dz_ab)
        dx, dnorm = _in_proj_bwd_dx(xs[i], sv['ng'], sv['w_in'], dzs, dx_res, tb, f"in_proj_bwd_dx_{li}")
        dws = _in_proj_bwd_dw(xs[i], sv['ng'], dzs, tb, f"in_proj_bwd_dw_{li}")
        ds5 = sv['tab_vjp']((dbb, dcb, dlam, ddv))
        for n, gval in zip(('ssm_a_re', 'ssm_a_im', 'ssm_b_re', 'ssm_b_im', 'ssm_c_re', 'ssm_c_im', 'ssm_d', 'ssm_log_step'), ds5):
            grads[n][i] = gval
        grads['norm_g'][i] = dnorm.reshape(D)
        grads['w_in'][i] = _restore_dw_in(jnp.concatenate(dws, axis=1))
        grads['ssm_w_glu'][i] = dwglu
        grads['ssm_b_glu'][i] = dbglu.reshape(D_SSM)
        grads['dn_conv_w'][i] = dconv[:4]
        grads['dn_a_log'][i] = dal.reshape(H, DH).sum(axis=1)
        grads['dn_dt_bias'][i] = ddt.reshape(H, DH).sum(axis=1)
        grads['dn_norm_g'][i] = dng.reshape(DH)
        grads['sg_ln_g'][i] = dlng.reshape(D_SG)
        grads['sg_ln_b'][i] = dlnb.reshape(D_SG)
        grads['sg_w'][i] = dsgw
        grads['sg_b'][i] = jnp.transpose(dbsp[:, :4])
        grads['w_out'][i] = dwo
        grads['ple_norm_g'][i] = dpg.reshape(D)
        grads['w_ple_gate'][i] = dwg
        grads['w_ple'][i] = dwp
    grads = {n: (g if n == 'final_norm_g' else jnp.stack(g)) for n, g in grads.items()}
    return loss_vec[0, 0], dx.reshape(bl, s, D), grads


def kernel(x, p, norm_g, w_in, ssm_a_re, ssm_a_im, ssm_b_re, ssm_b_im, ssm_c_re, ssm_c_im, ssm_d, ssm_log_step, ssm_w_glu, ssm_b_glu, dn_conv_w, dn_a_log, dn_dt_bias, dn_norm_g, sg_ln_g, sg_ln_b, sg_w, sg_b, w_out, ple_norm_g, w_ple_gate, w_ple, final_norm_g, loss_target, m_norm_g, m_w_in, m_ssm_a_re, m_ssm_a_im, m_ssm_b_re, m_ssm_b_im, m_ssm_c_re, m_ssm_c_im, m_ssm_d, m_ssm_log_step, m_ssm_w_glu, m_ssm_b_glu, m_dn_conv_w, m_dn_a_log, m_dn_dt_bias, m_dn_norm_g, m_sg_ln_g, m_sg_ln_b, m_sg_w, m_sg_b, m_w_out, m_ple_norm_g, m_w_ple_gate, m_w_ple, m_final_norm_g, v_norm_g, v_w_in, v_ssm_a_re, v_ssm_a_im, v_ssm_b_re, v_ssm_b_im, v_ssm_c_re, v_ssm_c_im, v_ssm_d, v_ssm_log_step, v_ssm_w_glu, v_ssm_b_glu, v_dn_conv_w, v_dn_a_log, v_dn_dt_bias, v_dn_norm_g, v_sg_ln_g, v_sg_ln_b, v_sg_w, v_sg_b, v_w_out, v_ple_norm_g, v_w_ple_gate, v_w_ple, v_final_norm_g):
    w_loc = dict(zip(WEIGHTS, (norm_g, w_in, ssm_a_re, ssm_a_im, ssm_b_re, ssm_b_im, ssm_c_re, ssm_c_im, ssm_d, ssm_log_step,
                               ssm_w_glu, ssm_b_glu, dn_conv_w, dn_a_log, dn_dt_bias, dn_norm_g, sg_ln_g, sg_ln_b, sg_w, sg_b,
                               w_out, ple_norm_g, w_ple_gate, w_ple, final_norm_g)))
    m_loc = dict(zip(WEIGHTS, (m_norm_g, m_w_in, m_ssm_a_re, m_ssm_a_im, m_ssm_b_re, m_ssm_b_im, m_ssm_c_re, m_ssm_c_im, m_ssm_d,
                               m_ssm_log_step, m_ssm_w_glu, m_ssm_b_glu, m_dn_conv_w, m_dn_a_log, m_dn_dt_bias, m_dn_norm_g,
                               m_sg_ln_g, m_sg_ln_b, m_sg_w, m_sg_b, m_w_out, m_ple_norm_g, m_w_ple_gate, m_w_ple, m_final_norm_g)))
    v_loc = dict(zip(WEIGHTS, (v_norm_g, v_w_in, v_ssm_a_re, v_ssm_a_im, v_ssm_b_re, v_ssm_b_im, v_ssm_c_re, v_ssm_c_im, v_ssm_d,
                               v_ssm_log_step, v_ssm_w_glu, v_ssm_b_glu, v_dn_conv_w, v_dn_a_log, v_dn_dt_bias, v_dn_norm_g,
                               v_sg_ln_g, v_sg_ln_b, v_sg_w, v_sg_b, v_w_out, v_ple_norm_g, v_w_ple_gate, v_w_ple, v_final_norm_g)))

    gathered = _all_gather([w_loc[n].astype(WIRE[n]) for n in SHARDED_ORDER], "gather_weights")
    full = dict(w_loc)
    for n, g in zip(SHARDED_ORDER, gathered):
        full[n] = _from_gathered(g, SHARDED[n])
    full['ssm_w_glu'] = full['ssm_w_glu'].astype(F32)

    loss_part, grad_x, grads = _local_step(x, p, full, loss_target)

    dest = [_to_dest_blocks(grads[n], SHARDED[n], WIRE[n]) for n in SHARDED_ORDER]
    c = lax.axis_index("c")
    own = [lax.dynamic_index_in_dim(d, c, 0, keepdims=False) for d in dest]
    for_sibling = [lax.dynamic_index_in_dim(d, 1 - c, 0, keepdims=False) for d in dest]
    from_sibling = _pair_exchange(for_sibling, "grads_pair_exchange")
    chip_sums = [_add_pair(a, b, f"grads_pair_sum_{n}") for n, a, b in zip(SHARDED_ORDER, own, from_sibling)]
    by_chip = _chip_exchange(chip_sums, "grads_chip_exchange")
    rep_pack = _pack([grads[n] for n in REPLICATED_ORDER] + [loss_part.reshape(1)])
    (rep_recv,) = _all_gather([rep_pack], "gather_small_grads")

    outs = {k: {} for k in 'gdmv'}
    for n, gk in zip(SHARDED_ORDER, by_chip):
        for k, o in zip('gdmv', _sum_adamw(gk, w_loc[n], m_loc[n], v_loc[n], f"adamw_{n}")):
            outs[k][n] = o
    one = jnp.zeros((1,), F32)
    rep_out = _sum_adamw(rep_recv, _pack([w_loc[n] for n in REPLICATED_ORDER] + [one]),
                         _pack([m_loc[n] for n in REPLICATED_ORDER] + [one]),
                         _pack([v_loc[n] for n in REPLICATED_ORDER] + [one]), "adamw_replicated")
    rep_shapes = [w_loc[n].shape for n in REPLICATED_ORDER] + [(1,)]
    for k, rep_p in zip('gdmv', rep_out):
        outs[k].update(zip(REPLICATED_ORDER + ['loss'], _unpack(rep_p, rep_shapes)))
    loss = outs['g']['loss'].reshape(())
    return (loss, grad_x, *[outs['g'][n] for n in WEIGHTS], *[outs['d'][n] for n in WEIGHTS],
            *[outs['m'][n] for n in WEIGHTS], *[outs['v'][n] for n in WEIGHTS])
```

```python
import functools

import jax
import jax.numpy as jnp
from jax import lax
from jax.experimental import pallas as pl
from jax.experimental.pallas import tpu as pltpu

F32 = jnp.float32
BF16 = jnp.bfloat16
EPS = 1e-6

D = 1024
D_PLE = 256
D_SSM = 256
D_DN = 512
D_SG = 256
G = 16
CG = 16
NS = 64
NRE = G * NS
H = 4
DH = 128
DN_C = 128
SG_C = 128
ZW = 3456
Z_PIECES = (512, 1536, 512, 768, 128)
N_DEV = 8
LANES = 128
PACK_ROWS = 256
VMEM_LIMIT = 56 * 1024 * 1024
ELEMENTWISE_STEP_BYTES = 4 * 1024 * 1024
TB = 256
SG_TB = 512
PREP_TB = 256

ADAM_LR = 0.001
ADAM_B1 = 0.9
ADAM_B2 = 0.999
ADAM_EPS = 1e-08
ADAM_WD = 0.01
ADAM_STEP = 10

MIX_HEAD_START = 3
S5_L = 128
S5_GROUP = 8
S5_SHIFTS = (1, 2, 4)

WEIGHTS = ['norm_g', 'w_in', 'ssm_a_re', 'ssm_a_im', 'ssm_b_re', 'ssm_b_im', 'ssm_c_re', 'ssm_c_im', 'ssm_d',
           'ssm_log_step', 'ssm_w_glu', 'ssm_b_glu', 'dn_conv_w', 'dn_a_log', 'dn_dt_bias', 'dn_norm_g', 'sg_ln_g',
           'sg_ln_b', 'sg_w', 'sg_b', 'w_out', 'ple_norm_g', 'w_ple_gate', 'w_ple', 'final_norm_g']
SHARDED = {'w_in': 2, 'ssm_w_glu': 1, 'dn_conv_w': 2, 'w_out': 1, 'w_ple_gate': 1, 'w_ple': 2}
SHARDED_ORDER = ['w_in', 'ssm_w_glu', 'dn_conv_w', 'w_out', 'w_ple_gate', 'w_ple']
WIRE = {'w_in': BF16, 'ssm_w_glu': BF16, 'dn_conv_w': F32, 'w_out': BF16, 'w_ple_gate': BF16, 'w_ple': BF16}
REPLICATED_ORDER = [n for n in WEIGHTS if n not in SHARDED]


def _cp(*sem):
    return pltpu.CompilerParams(dimension_semantics=sem, vmem_limit_bytes=VMEM_LIMIT)


def _dg(a, b, ca, cb, precision=None):
    return lax.dot_general(a, b, (((ca,), (cb,)), ((), ())), precision=precision, preferred_element_type=F32)


@jax.custom_vjp
def _mm(a, b):
    return _dg(a.astype(BF16), b.astype(BF16), 1, 0)


def _mm_fwd(a, b):
    return _mm(a, b), (a, b)


def _mm_bwd(res, g):
    a, b = res
    gb = g.astype(BF16)
    return _dg(gb, b.astype(BF16), 1, 1), _dg(a.astype(BF16), gb, 0, 0)


_mm.defvjp(_mm_fwd, _mm_bwd)


@jax.custom_vjp
def _mm_nt(a, b):
    return _dg(a.astype(BF16), b.astype(BF16), 1, 1)


def _mm_nt_fwd(a, b):
    return _mm_nt(a, b), (a, b)


def _mm_nt_bwd(res, g):
    a, b = res
    gb = g.astype(BF16)
    return _dg(gb, b.astype(BF16), 1, 0), _dg(gb, a.astype(BF16), 0, 0)


_mm_nt.defvjp(_mm_nt_fwd, _mm_nt_bwd)


@jax.custom_vjp
def _mm_tn(a, b):
    return _dg(a.astype(BF16), b.astype(BF16), 0, 0)


def _mm_tn_fwd(a, b):
    return _mm_tn(a, b), (a, b)


def _mm_tn_bwd(res, g):
    a, b = res
    gb = g.astype(BF16)
    return _dg(b.astype(BF16), gb, 1, 1), _dg(a.astype(BF16), gb, 1, 0)


_mm_tn.defvjp(_mm_tn_fwd, _mm_tn_bwd)


def _split(x, n):
    pieces = []
    for _ in range(n - 1):
        hi = x.astype(BF16)
        pieces.append(hi)
        x = x - hi.astype(F32)
    pieces.append(x.astype(BF16))
    return pieces


def _dg3(a, b, ca, cb):
    a_hi, a_lo = _split(a, 2)
    b_hi, b_lo = _split(b, 2)
    return _dg(a_hi, b_hi, ca, cb) + (_dg(a_hi, b_lo, ca, cb) + _dg(a_lo, b_hi, ca, cb))


@jax.custom_vjp
def _dot3(a, b):
    return _dg3(a, b, 1, 0)


def _dot3_fwd(a, b):
    return _dot3(a, b), (a, b)


def _dot3_bwd(res, g):
    a, b = res
    return _dg3(g, b, 1, 1), _dg3(a, g, 0, 0)


_dot3.defvjp(_dot3_fwd, _dot3_bwd)


def _dg_sel(x, e, cx, ce, x_first):
    eb = e.astype(BF16)
    out = None
    for piece in reversed(_split(x, 3)):
        term = _dg(piece, eb, cx, ce) if x_first else _dg(eb, piece, ce, cx)
        out = term if out is None else out + term
    return out


@jax.custom_vjp
def _sel_r(x, e):
    return _dg_sel(x, e, 1, 0, True)


def _sel_r_fwd(x, e):
    return _sel_r(x, e), e


def _sel_r_bwd(e, g):
    return _dg_sel(g, e, 1, 1, True), jnp.zeros_like(e)


_sel_r.defvjp(_sel_r_fwd, _sel_r_bwd)


@jax.custom_vjp
def _sel_l(e, x):
    return _dg_sel(x, e, 0, 1, False)


def _sel_l_fwd(e, x):
    return _sel_l(e, x), e


def _sel_l_bwd(e, g):
    return jnp.zeros_like(e), _dg_sel(g, e, 0, 0, False)


_sel_l.defvjp(_sel_l_fwd, _sel_l_bwd)


def _rms(x, g):
    return x * lax.rsqrt(jnp.mean(x * x, axis=-1, keepdims=True) + EPS) * g


def _silu(x):
    return x * jax.nn.sigmoid(x)


def _in_proj_fwd(x, g, w, tb, name):
    t = x.shape[0]

    def body(x_ref, g_ref, w_ref, *z_refs):
        h = _rms(x_ref[...], g_ref[...])
        z = jnp.dot(h.astype(BF16), w_ref[...], preferred_element_type=F32)
        off = 0
        for z_ref, n in zip(z_refs, Z_PIECES):
            z_ref[...] = z[:, off:off + n]
            off += n

    row = lambda i: (i, 0)
    full = lambda i: (0, 0)
    return pl.pallas_call(
        body, grid=(t // tb,),
        in_specs=[pl.BlockSpec((tb, D), row), pl.BlockSpec((1, D), full), pl.BlockSpec((D, ZW), full)],
        out_specs=[pl.BlockSpec((tb, n), row) for n in Z_PIECES],
        out_shape=[jax.ShapeDtypeStruct((t, n), F32) for n in Z_PIECES],
        name=name, compiler_params=_cp("parallel"))(x, g, w)


def _in_proj_bwd_dx(x, g, w, dzs, dx_res, tb, name):
    t = x.shape[0]

    def body(x_ref, g_ref, w_ref, d0, d1, d2, d3, d4, dxr_ref, dx_ref, dg_ref):
        @pl.when(pl.program_id(0) == 0)
        def _():
            dg_ref[...] = jnp.zeros_like(dg_ref)

        dz = jnp.concatenate([d0[...], d1[...], d2[...], d3[...], d4[...]], axis=1).astype(BF16)
        dh = _dg(dz, w_ref[...], 1, 1)
        _, vjp = jax.vjp(_rms, x_ref[...], g_ref[...])
        dx, dg = vjp(dh)
        dx_ref[...] = dx + dxr_ref[...]
        dg_ref[...] += dg

    row = lambda i: (i, 0)
    full = lambda i: (0, 0)
    return pl.pallas_call(
        body, grid=(t // tb,),
        in_specs=[pl.BlockSpec((tb, D), row), pl.BlockSpec((1, D), full), pl.BlockSpec((D, ZW), full)]
        + [pl.BlockSpec((tb, n), row) for n in Z_PIECES] + [pl.BlockSpec((tb, D), row)],
        out_specs=[pl.BlockSpec((tb, D), row), pl.BlockSpec((1, D), full)],
        out_shape=[jax.ShapeDtypeStruct((t, D), F32), jax.ShapeDtypeStruct((1, D), F32)],
        name=name, compiler_params=_cp("arbitrary"))(x, g, w, *dzs, dx_res)


def _in_proj_bwd_dw(x, g, dzs, tb, name):
    t = x.shape[0]

    def body(x_ref, g_ref, d0, d1, d2, d3, d4, *dw_refs):
        @pl.when(pl.program_id(0) == 0)
        def _():
            for r in dw_refs:
                r[...] = jnp.zeros_like(r)

        h = _rms(x_ref[...], g_ref[...]).astype(BF16)
        for d_ref, dw_ref in zip((d0, d1, d2, d3, d4), dw_refs):
            dw_ref[...] += _dg(h, d_ref[...].astype(BF16), 0, 0)

    row = lambda i: (i, 0)
    full = lambda i: (0, 0)
    return pl.pallas_call(
        body, grid=(t // tb,),
        in_specs=[pl.BlockSpec((tb, D), row), pl.BlockSpec((1, D), full)] + [pl.BlockSpec((tb, n), row) for n in Z_PIECES],
        out_specs=[pl.BlockSpec((D, n), full) for n in Z_PIECES],
        out_shape=[jax.ShapeDtypeStruct((D, n), F32) for n in Z_PIECES],
        name=name, compiler_params=_cp("arbitrary"))(x, g, *dzs)


def _lam_pow(a_re, a_im, log_step, k):
    step = jnp.exp(log_step)[:, None]
    mag = jnp.exp(k * a_re * step)
    ang = k * a_im * step
    return mag * jnp.cos(ang), mag * jnp.sin(ang)


def _s5_powers(a_re, a_im, log_step):
    def table(ks):
        re, im = _lam_pow(a_re, a_im, log_step, jnp.asarray(ks, F32)[:, None, None])
        return jnp.concatenate([re.reshape(len(ks), NRE), im.reshape(len(ks), NRE)], axis=-1)

    ld = table(S5_SHIFTS).reshape(len(S5_SHIFTS), 1, 2 * NRE)
    return ld, table(range(1, S5_GROUP + 1)), table(range(S5_GROUP, 0, -1))


def _s5_tables(a_re, a_im, b_re, b_im, c_re, c_im, d_skip, log_step):
    lam_re, lam_im = _lam_pow(a_re, a_im, log_step, 1.0)
    den = a_re * a_re + a_im * a_im
    nr, ni = lam_re - 1.0, lam_im
    f_re = (nr * a_re + ni * a_im) / den
    f_im = (ni * a_re - nr * a_im) / den
    bbar_re = f_re[..., None] * b_re - f_im[..., None] * b_im
    bbar_im = f_re[..., None] * b_im + f_im[..., None] * b_re
    eye = jnp.eye(G, dtype=F32)

    def blk_b(bb):
        return (jnp.transpose(bb, (0, 2, 1))[:, :, None, :] * eye[:, None, :, None]).reshape(D_SSM, NRE)

    def blk_c(cc):
        return (jnp.transpose(cc, (0, 2, 1))[:, :, None, :] * eye[:, None, :, None]).reshape(NRE, D_SSM)

    b_blk = jnp.concatenate([blk_b(bbar_re), blk_b(bbar_im)], axis=1)
    c_blk = jnp.concatenate([blk_c(c_re), -blk_c(c_im)], axis=0)
    lam = jnp.concatenate([lam_re.reshape(1, NRE), lam_im.reshape(1, NRE)], axis=-1)
    return b_blk, c_blk, lam, d_skip.reshape(1, D_SSM)


def _group_shift(x, d, up=False):
    r = lax.broadcasted_iota(jnp.int32, x.shape, 0) & (S5_GROUP - 1)
    if up:
        return jnp.where(r < S5_GROUP - d, pltpu.roll(x, x.shape[0] - d, 0), 0.0)
    return jnp.where(r >= d, pltpu.roll(x, d, 0), 0.0)


def _s5_scan_steps(hr, hi, cr, ci, lds, lp):
    for ld, d in zip(lds, S5_SHIFTS):
        lr, li = ld[:, :NRE], ld[:, NRE:]
        sr, si = _group_shift(hr, d), _group_shift(hi, d)
        hr, hi = hr + lr * sr - li * si, hi + lr * si + li * sr
        yield
    pr, pi = lp[:, :NRE], lp[:, NRE:]
    rows_r, rows_i = [], []
    for r in range(hr.shape[0] // S5_GROUP):
        br, bi = hr[r * S5_GROUP:(r + 1) * S5_GROUP], hi[r * S5_GROUP:(r + 1) * S5_GROUP]
        br, bi = br + pr * cr - pi * ci, bi + pr * ci + pi * cr
        cr, ci = br[S5_GROUP - 1:S5_GROUP], bi[S5_GROUP - 1:S5_GROUP]
        rows_r.append(br)
        rows_i.append(bi)
        if r % 2:
            yield
    return jnp.concatenate(rows_r, axis=0), jnp.concatenate(rows_i, axis=0)


@jax.custom_vjp
def _known_scan(xr, xi, cr, ci, lam, lds, lp_rev, hr, hi):
    return hr, hi


def _known_scan_fwd(xr, xi, cr, ci, lam, lds, lp_rev, hr, hi):
    return (hr, hi), (cr, ci, lam, lds, lp_rev, hr, hi)


def _known_scan_bwd(res, cts):
    cr, ci, lam, lds, lp_rev, hr, hi = res
    ar, ai = cts
    for ld, d in zip(lds, S5_SHIFTS):
        lr, li = ld[:, :NRE], ld[:, NRE:]
        sr, si = _group_shift(ar, d, up=True), _group_shift(ai, d, up=True)
        ar, ai = ar + lr * sr + li * si, ai + lr * si - li * sr
    qr, qi = lp_rev[:, :NRE], lp_rev[:, NRE:]
    nr, ni = jnp.zeros_like(cr), jnp.zeros_like(ci)
    rows_r, rows_i = [], []
    for r in reversed(range(hr.shape[0] // S5_GROUP)):
        br, bi = ar[r * S5_GROUP:(r + 1) * S5_GROUP], ai[r * S5_GROUP:(r + 1) * S5_GROUP]
        br, bi = br + qr * nr + qi * ni, bi + qr * ni - qi * nr
        nr, ni = br[0:1], bi[0:1]
        rows_r.insert(0, br)
        rows_i.insert(0, bi)
    ar, ai = jnp.concatenate(rows_r, axis=0), jnp.concatenate(rows_i, axis=0)
    lr, li = lam[:, :NRE], lam[:, NRE:]
    dcr, dci = lr * nr + li * ni, lr * ni - li * nr
    first = lax.broadcasted_iota(jnp.int32, hr.shape, 0) == 0
    pr = jnp.where(first, cr, pltpu.roll(hr, 1, 0))
    pi = jnp.where(first, ci, pltpu.roll(hi, 1, 0))
    dlam = jnp.concatenate([jnp.sum(ar * pr + ai * pi, axis=0, keepdims=True),
                            jnp.sum(ai * pr - ar * pi, axis=0, keepdims=True)], axis=1)
    return (ar, ai, dcr, dci, dlam, [jnp.zeros_like(ld) for ld in lds], jnp.zeros_like(lp_rev),
            jnp.zeros_like(hr), jnp.zeros_like(hi))


_known_scan.defvjp(_known_scan_fwd, _known_scan_bwd)


def _interleave(short, long, head_start=0):
    gens = list(short) + list(long)
    results = [None] * len(gens)

    def advance(live):
        still = []
        for idx, gen in live:
            try:
                next(gen)
                still.append((idx, gen))
            except StopIteration as done:
                results[idx] = done.value
        return still

    live_short = advance(list(enumerate(gens))[:len(short)])
    live_long = list(enumerate(gens))[len(short):]
    for _ in range(head_start):
        live_long = advance(live_long)
    live = live_short + live_long
    while live:
        live = advance(live)
    return results[:len(short)], results[len(short):]


def _s5_chunk_gen(u, gate, cr, ci, b_blk, c_blk, lam, dv, wglu, bglu, lds, lp, lp_rev, known_h=None):
    bu = _mm(u, b_blk)
    xr, xi = bu[:, :NRE], bu[:, NRE:]
    yield
    if known_h is None:
        hr, hi = yield from _s5_scan_steps(xr, xi, cr, ci, lds, lp)
    else:
        hr, hi = _known_scan(xr, xi, cr, ci, lam, lds, lp_rev, *known_h)
    y = _mm(jnp.concatenate([hr, hi], axis=1), c_blk) + dv * u
    yield
    y = jax.nn.gelu(y)
    y = y * jax.nn.sigmoid(_mm(y, wglu) + bglu)
    return y * _silu(gate), hr, hi


S5_PAR_SHAPES = [(D_SSM, 2 * NRE), (2 * NRE, D_SSM), (1, 2 * NRE), (1, D_SSM), (D_SSM, D_SSM), (1, D_SSM)]
S5_CONST_SHAPES = [(len(S5_SHIFTS), 1, 2 * NRE), (S5_GROUP, 2 * NRE), (S5_GROUP, 2 * NRE)]
DN_PAR_SHAPES = [(1, D_DN), (1, D_DN), (1, DH)]


def _mix_specs(bl, n_c, rev):
    def chunk(i):
        return n_c - 1 - i if rev else i

    def tok(n):
        return pl.BlockSpec((bl, DN_C, n), lambda i: (0, chunk(i), 0))

    def per_chunk(shape):
        return pl.BlockSpec((bl, 1, *shape), lambda i: (0, chunk(i)) + (0,) * len(shape))

    def whole(shape):
        return pl.BlockSpec(shape, lambda i: (0,) * len(shape))

    return tok, per_chunk, whole


def _dn_post(c):
    s = _silu(c)
    parts = []
    for j in range(12):
        xj = s[:, j * DH:(j + 1) * DH]
        if j < 8:
            xj = xj * lax.rsqrt(jnp.sum(xj * xj, axis=-1, keepdims=True) + EPS)
        if j < 4:
            xj = xj * (DH ** -0.5)
        parts.append(xj)
    return jnp.concatenate(parts, axis=1)


def _dn_prep_fwd(zq, conv_w8, bl, s, tb, name):
    n_s = s // tb
    hb = tb // 8
    w3 = 3 * D_DN

    def body(cur_ref, prev_ref, w_ref, o_ref):
        i = pl.program_id(1)
        prev = jnp.where(i > 0, prev_ref[...], 0.0)
        ext = jnp.concatenate([prev, cur_ref[...]], axis=0)
        c = jnp.zeros((tb, w3), F32)
        for k in range(4):
            sh = ext if k == 3 else pltpu.roll(ext, 3 - k, 0)
            c = c + w_ref[k:k + 1, :] * sh[8:, :]
        o_ref[...] = _dn_post(c)

    row = lambda b, i: (b * n_s + i, 0)
    prv = lambda b, i: (jnp.maximum((b * n_s + i) * hb - 1, 0), 0)
    full = lambda b, i: (0, 0)
    t = bl * s
    return pl.pallas_call(
        body, grid=(bl, n_s),
        in_specs=[pl.BlockSpec((tb, w3), row), pl.BlockSpec((8, w3), prv), pl.BlockSpec((8, w3), full)],
        out_specs=pl.BlockSpec((tb, w3), row),
        out_shape=jax.ShapeDtypeStruct((t, w3), F32),
        name=name, compiler_params=_cp("parallel", "parallel"))(zq, zq, conv_w8)


def _dn_prep_bwd(zq, dqkv, conv_w8, bl, s, tb, name):
    n_s = s // tb
    hb = tb // 8
    w3 = 3 * D_DN
    n_blk8 = bl * s // 8

    def body(cur_ref, prev_ref, next_ref, d_ref, dnext_ref, w_ref, dz_ref, dw_ref):
        b, i = pl.program_id(0), pl.program_id(1)

        @pl.when((b == 0) & (i == 0))
        def _():
            dw_ref[...] = jnp.zeros_like(dw_ref)

        prev = jnp.where(i > 0, prev_ref[...], 0.0)
        nxt = jnp.where(i < n_s - 1, next_ref[...], 0.0)
        dnxt = jnp.where(i < n_s - 1, dnext_ref[...], 0.0)
        ext = jnp.concatenate([prev, cur_ref[...], nxt], axis=0)
        shifted = [ext if k == 3 else pltpu.roll(ext, 3 - k, 0) for k in range(4)]
        c2 = jnp.zeros((tb + 8, w3), F32)
        for k in range(4):
            c2 = c2 + w_ref[k:k + 1, :] * shifted[k][8:, :]
        dpost = jnp.concatenate([d_ref[...], dnxt], axis=0)
        _, vjp = jax.vjp(_dn_post, c2)
        (dc2,) = vjp(dpost)
        dz = jnp.zeros((tb, w3), F32)
        for k in range(4):
            up = dc2 if k == 3 else pltpu.roll(dc2, tb + 8 - (3 - k), 0)
            dz = dz + w_ref[k:k + 1, :] * up[:tb, :]
            dw_ref[k:k + 1, :] += jnp.sum(dc2[:tb, :] * shifted[k][8:8 + tb, :], axis=0, keepdims=True)
        dz_ref[...] = dz

    row = lambda b, i: (b * n_s + i, 0)
    prv = lambda b, i: (jnp.maximum((b * n_s + i) * hb - 1, 0), 0)
    nxt = lambda b, i: (jnp.minimum((b * n_s + i + 1) * hb, n_blk8 - 1), 0)
    full = lambda b, i: (0, 0)
    t = bl * s
    return pl.pallas_call(
        body, grid=(bl, n_s),
        in_specs=[pl.BlockSpec((tb, w3), row), pl.BlockSpec((8, w3), prv), pl.BlockSpec((8, w3), nxt),
                  pl.BlockSpec((tb, w3), row), pl.BlockSpec((8, w3), nxt), pl.BlockSpec((8, w3), full)],
        out_specs=[pl.BlockSpec((tb, w3), row), pl.BlockSpec((8, w3), full)],
        out_shape=[jax.ShapeDtypeStruct((t, w3), F32), jax.ShapeDtypeStruct((8, w3), F32)],
        name=name, compiler_params=_cp("arbitrary", "arbitrary"))(zq, zq, zq, dqkv, dqkv, conv_w8)


def _unit_lower_inverse_steps(ms):
    c_len = ms[0].shape[0]
    eye = lax.broadcasted_iota(jnp.int32, (c_len, c_len), 0) == lax.broadcasted_iota(jnp.int32, (c_len, c_len), 1)
    ident = jnp.where(eye, 1.0, 0.0)
    ps = ms
    tinvs = [ident - m for m in ms]
    for _ in range(c_len.bit_length() - 2):
        ps = [_dg3(p, p, 1, 0) for p in ps]
        yield
        tinvs = [t + _dg3(t, p, 1, 0) for t, p in zip(tinvs, ps)]
        yield
    return tinvs


@jax.custom_vjp
def _known_inverses(ms, tinvs):
    return tinvs


def _known_inverses_fwd(ms, tinvs):
    return tinvs, tinvs


def _known_inverses_bwd(tinvs, gs):
    return [-_dg3(_dg3(t, g, 0, 0), t, 1, 1) for t, g in zip(tinvs, gs)], [jnp.zeros_like(t) for t in tinvs]


_known_inverses.defvjp(_known_inverses_fwd, _known_inverses_bwd)


def _dn_chunk_gen(qkv, zab, zg, states, alog_e, dt_e, ng, known_tinvs=None):
    c_len = DN_C
    r = lax.broadcasted_iota(jnp.int32, (c_len, c_len), 0)
    c = lax.broadcasted_iota(jnp.int32, (c_len, c_len), 1)
    causal, strict = r >= c, r > c
    tril = jnp.where(causal, 1.0, 0.0)
    rr = lax.broadcasted_iota(jnp.int32, (LANES, D_DN), 0)
    cc = lax.broadcasted_iota(jnp.int32, (LANES, D_DN), 1)
    e_a = jnp.where((cc >= rr * DH) & (cc < rr * DH + DH) & (rr < H), 1.0, 0.0)
    e_b = jnp.where((cc >= (rr - H) * DH) & (cc < (rr - H) * DH + DH) & (rr >= H) & (rr < 2 * H), 1.0, 0.0)
    a_e = _sel_r(zab, e_a)
    b_e = _sel_r(zab, e_b)
    beta = jax.nn.sigmoid(b_e)
    g = -jnp.exp(alog_e) * jax.nn.softplus(a_e + dt_e)
    yield
    gc = _sel_l(tril, g)
    glast = jnp.sum(g, axis=0, keepdims=True)
    eg = jnp.exp(gc)
    ekd = jnp.exp(glast - gc)
    dl = jnp.exp(glast)
    yield
    heads = range(H)
    sls = [slice(h * DH, (h + 1) * DH) for h in heads]
    qs = [qkv[:, h * DH:(h + 1) * DH] for h in heads]
    ks = [qkv[:, D_DN + h * DH:D_DN + (h + 1) * DH] for h in heads]
    vs = [qkv[:, 2 * D_DN + h * DH:2 * D_DN + (h + 1) * DH] for h in heads]
    ccols = [gc[:, sl] for sl in sls]
    decs = [jnp.where(causal, jnp.exp(jnp.where(causal, cl - jnp.transpose(cl), 0.0)), 0.0) for cl in ccols]
    kbs = [k * beta[:, sl] for k, sl in zip(ks, sls)]
    ms = [jnp.where(strict, _mm_nt(kb, k) * dec, 0.0) for kb, k, dec in zip(kbs, ks, decs)]
    yield
    if known_tinvs is None:
        tinvs = yield from _unit_lower_inverse_steps(ms)
    else:
        tinvs = _known_inverses(ms, list(known_tinvs))
    sols = [_dot3(t, jnp.concatenate([v * beta[:, sl], kb * eg[:, sl]], axis=1))
            for t, v, kb, sl in zip(tinvs, vs, kbs, sls)]
    yield
    atts = [_mm_nt(q, k) * dec for q, k, dec in zip(qs, ks, decs)]
    vnews = [sol[:, :DH] - _mm(sol[:, DH:], st) for sol, st in zip(sols, states)]
    yield
    os_ = [_mm(q * eg[:, sl], st) + _mm(att, vn) for q, sl, st, att, vn in zip(qs, sls, states, atts, vnews)]
    yield
    new_states = [st * dl[:, sl] + _mm_tn(k * ekd[:, sl], vn) for st, sl, k, vn in zip(states, sls, ks, vnews)]
    yield
    ys = [_rms(o, ng) * _silu(zg[:, sl]) for o, sl in zip(os_, sls)]
    return jnp.concatenate(ys, axis=1), new_states, tinvs


def _mix_fwd(zs, s5_par, s5_const, qkv, zab, zg, dn_par, bl, s, name):
    assert S5_L == DN_C
    n_c = s // DN_C
    nd = len(S5_SHIFTS)
    tok, per_chunk, whole = _mix_specs(bl, n_c, False)

    def body(z_ref, b_ref, c_ref, lam_ref, dv_ref, wg_ref, bg_ref, ld_ref, lp_ref, lpr_ref,
             q_ref, ab_ref, zg_ref, al_ref, dt_ref, ng_ref,
             ys_ref, car_ref, h_ref, yd_ref, st_ref, ti_ref, cs, ssc):
        @pl.when(pl.program_id(0) == 0)
        def _():
            cs[...] = jnp.zeros_like(cs)
            ssc[...] = jnp.zeros_like(ssc)

        lds = [ld_ref[k] for k in range(nd)]
        s5_gens, dn_gens = [], []
        for e in range(bl):
            c = cs[e]
            car_ref[e, 0] = c
            sts = [ssc[e, h] for h in range(H)]
            for h in range(H):
                st_ref[e, 0, h] = sts[h]
            z = z_ref[e]
            s5_gens.append(_s5_chunk_gen(z[:, :D_SSM], z[:, D_SSM:], c[:, :NRE], c[:, NRE:], b_ref[...], c_ref[...], lam_ref[...],
                                         dv_ref[...], wg_ref[...], bg_ref[...], lds, lp_ref[...], lpr_ref[...]))
            dn_gens.append(_dn_chunk_gen(q_ref[e], ab_ref[e], zg_ref[e], sts, al_ref[...], dt_ref[...], ng_ref[...]))
        s5_outs, dn_outs = _interleave(s5_gens, dn_gens, head_start=MIX_HEAD_START)
        for e in range(bl):
            y_s, hr, hi = s5_outs[e]
            y_d, new_sts, tinvs = dn_outs[e]
            ys_ref[e] = y_s
            h_ref[e, :, :NRE] = hr
            h_ref[e, :, NRE:] = hi
            cs[e, :, :NRE] = hr[S5_L - 1:S5_L]
            cs[e, :, NRE:] = hi[S5_L - 1:S5_L]
            yd_ref[e] = y_d
            for h in range(H):
                ssc[e, h] = new_sts[h]
                ti_ref[e, 0, h] = tinvs[h]

    head_mats = jax.ShapeDtypeStruct((bl, n_c, H, DH, DH), F32)
    return pl.pallas_call(
        body, grid=(n_c,),
        in_specs=[tok(2 * D_SSM)] + [whole(sh) for sh in S5_PAR_SHAPES + S5_CONST_SHAPES]
        + [tok(3 * D_DN), tok(LANES), tok(D_DN)] + [whole(sh) for sh in DN_PAR_SHAPES],
        out_specs=[tok(D_SSM), per_chunk((1, 2 * NRE)), tok(2 * NRE), tok(D_DN), per_chunk((H, DH, DH)), per_chunk((H, DH, DH))],
        out_shape=[jax.ShapeDtypeStruct((bl, s, D_SSM), F32), jax.ShapeDtypeStruct((bl, n_c, 1, 2 * NRE), F32),
                   jax.ShapeDtypeStruct((bl, s, 2 * NRE), F32), jax.ShapeDtypeStruct((bl, s, D_DN), F32), head_mats, head_mats],
        scratch_shapes=[pltpu.VMEM((bl, 1, 2 * NRE), F32), pltpu.VMEM((bl, H, DH, DH), F32)],
        name=name, compiler_params=_cp("arbitrary"))(zs, *s5_par, *s5_const, qkv, zab, zg, *dn_par)


def _mix_bwd(zs, carries, h_all, dy_s, s5_par, s5_const, qkv, zab, zg, states, tinvs, dy_d, dn_par, bl, s, name):
    n_c = s // DN_C
    nd = len(S5_SHIFTS)
    tok, per_chunk, whole = _mix_specs(bl, n_c, True)

    def both(examples, s5_tabs, s5_consts, dn_tabs):
        s5_gens = [_s5_chunk_gen(u, gate, cr, ci, *s5_tabs, *s5_consts, known_h=(hr, hi))
                   for u, gate, cr, ci, hr, hi, _, _, _, _, _ in examples]
        dn_gens = [_dn_chunk_gen(q, ab, zgate, sts, *dn_tabs, known_tinvs=known)
                   for _, _, _, _, _, _, q, ab, zgate, sts, known in examples]
        s5_outs, dn_outs = _interleave(s5_gens, dn_gens, head_start=MIX_HEAD_START)
        return [(y_s, hr[S5_L - 1:S5_L], hi[S5_L - 1:S5_L], y_d, new_sts)
                for (y_s, hr, hi), (y_d, new_sts, _) in zip(s5_outs, dn_outs)]

    def body(z_ref, car_ref, h_ref, dys_ref, b_ref, c_ref, lam_ref, dv_ref, wg_ref, bg_ref, ld_ref, lp_ref, lpr_ref,
             q_ref, ab_ref, zg_ref, st_ref, ti_ref, dyd_ref, al_ref, dt_ref, ng_ref,
             dz_ref, db_ref, dc_ref, dlam_ref, ddv_ref, dwg_ref, dbg_ref,
             dq_ref, dab_ref, dzg_ref, dal_ref, ddt_ref, dng_ref, dcs, dsc):
        accs = (db_ref, dc_ref, dlam_ref, ddv_ref, dwg_ref, dbg_ref, dal_ref, ddt_ref, dng_ref)

        @pl.when(pl.program_id(0) == 0)
        def _():
            for r in accs + (dcs, dsc):
                r[...] = jnp.zeros_like(r)

        examples = []
        for e in range(bl):
            z = z_ref[e]
            c = car_ref[e, 0]
            examples.append((z[:, :D_SSM], z[:, D_SSM:], c[:, :NRE], c[:, NRE:], h_ref[e, :, :NRE], h_ref[e, :, NRE:],
                             q_ref[e], ab_ref[e], zg_ref[e], [st_ref[e, 0, h] for h in range(H)],
                             [ti_ref[e, 0, h] for h in range(H)]))
        _, vjp = jax.vjp(both, examples,
                         (b_ref[...], c_ref[...], lam_ref[...], dv_ref[...], wg_ref[...], bg_ref[...]),
                         ([ld_ref[k] for k in range(nd)], lp_ref[...], lpr_ref[...]),
                         (al_ref[...], dt_ref[...], ng_ref[...]))
        cts = []
        for e in range(bl):
            dc = dcs[e]
            cts.append((dys_ref[e], dc[:, :NRE], dc[:, NRE:], dyd_ref[e], [dsc[e, h] for h in range(H)]))
        d_examples, d_s5, _, d_dn = vjp(cts)
        for e in range(bl):
            du, dgate, dcr, dci, _, _, dq, dab, dzg, dsts, _ = d_examples[e]
            dz_ref[e] = jnp.concatenate([du, dgate], axis=1)
            dcs[e, :, :NRE] = dcr
            dcs[e, :, NRE:] = dci
            dq_ref[e] = dq
            dab_ref[e] = dab
            dzg_ref[e] = dzg
            for h in range(H):
                dsc[e, h] = dsts[h]
        for r, ct in zip(accs, (*d_s5, *d_dn)):
            r[...] += ct

    head_mats = per_chunk((H, DH, DH))
    outs = pl.pallas_call(
        body, grid=(n_c,),
        in_specs=[tok(2 * D_SSM), per_chunk((1, 2 * NRE)), tok(2 * NRE), tok(D_SSM)]
        + [whole(sh) for sh in S5_PAR_SHAPES + S5_CONST_SHAPES]
        + [tok(3 * D_DN), tok(LANES), tok(D_DN), head_mats, head_mats, tok(D_DN)] + [whole(sh) for sh in DN_PAR_SHAPES],
        out_specs=[tok(2 * D_SSM)] + [whole(sh) for sh in S5_PAR_SHAPES]
        + [tok(3 * D_DN), tok(LANES), tok(D_DN)] + [whole(sh) for sh in DN_PAR_SHAPES],
        out_shape=[jax.ShapeDtypeStruct((bl, s, 2 * D_SSM), F32)] + [jax.ShapeDtypeStruct(sh, F32) for sh in S5_PAR_SHAPES]
        + [jax.ShapeDtypeStruct((bl, s, n), F32) for n in (3 * D_DN, LANES, D_DN)]
        + [jax.ShapeDtypeStruct(sh, F32) for sh in DN_PAR_SHAPES],
        scratch_shapes=[pltpu.VMEM((bl, 1, 2 * NRE), F32), pltpu.VMEM((bl, H, DH, DH), F32)],
        name=name, compiler_params=_cp("arbitrary"))(
            zs, carries, h_all, dy_s, *s5_par, *s5_const, qkv, zab, zg, states, tinvs, dy_d, *dn_par)
    return outs[:7], outs[7:]


def _sg_fn(n_chunk):
    def f(z, lng, lnb, w, bsp_t):
        u = jax.nn.gelu(z[:, :D_SG])
        v = jax.nn.gelu(z[:, D_SG:2 * D_SG])
        gate = z[:, 2 * D_SG:]
        xc = v - jnp.mean(v, axis=-1, keepdims=True)
        vn = xc * lax.rsqrt(jnp.mean(xc * xc, axis=-1, keepdims=True) + EPS) * lng + lnb
        r = lax.broadcasted_iota(jnp.int32, (SG_C, SG_C), 0)
        c = lax.broadcasted_iota(jnp.int32, (SG_C, SG_C), 1)
        causal = r >= c
        first_half = c < SG_C // 2
        rr = lax.broadcasted_iota(jnp.int32, (LANES, D_SG), 0)
        cc = lax.broadcasted_iota(jnp.int32, (LANES, D_SG), 1)
        expand = jnp.where((cc >= rr * 64) & (cc < rr * 64 + 64) & (rr < 4), 1.0, 0.0)
        bias = _sel_r(bsp_t, expand)
        wm = [jnp.where(causal, w[h], 0.0) for h in range(4)]
        rows = []
        for ci in range(n_chunk):
            vc = vn[ci * SG_C:(ci + 1) * SG_C]
            pairs = []
            for pr in range(2):
                vp = vc[:, pr * LANES:(pr + 1) * LANES]
                pairs.append(jnp.where(first_half, _mm(wm[2 * pr], vp), _mm(wm[2 * pr + 1], vp)))
            rows.append(jnp.concatenate(pairs, axis=1) + bias)
        sp = jnp.concatenate(rows, axis=0) if n_chunk > 1 else rows[0]
        return u * sp * _silu(gate)

    return f


def _sg_specs():
    full = lambda i: (0, 0)
    full3 = lambda i: (0, 0, 0)
    par = [pl.BlockSpec((1, D_SG), full), pl.BlockSpec((1, D_SG), full), pl.BlockSpec((4, SG_C, SG_C), full3),
           pl.BlockSpec((SG_C, LANES), full)]
    par_shapes = [(1, D_SG), (1, D_SG), (4, SG_C, SG_C), (SG_C, LANES)]
    return par, par_shapes


def _sg_fwd(zsg, params, tb, name):
    t = zsg.shape[0]
    f = _sg_fn(tb // SG_C)
    par, _ = _sg_specs()

    def body(z_ref, g_ref, b_ref, w_ref, bs_ref, y_ref):
        y_ref[...] = f(z_ref[...], g_ref[...], b_ref[...], w_ref[...], bs_ref[...])

    row = lambda i: (i, 0)
    return pl.pallas_call(
        body, grid=(t // tb,), in_specs=[pl.BlockSpec((tb, 3 * D_SG), row)] + par,
        out_specs=pl.BlockSpec((tb, D_SG), row), out_shape=jax.ShapeDtypeStruct((t, D_SG), F32),
        name=name, compiler_params=_cp("parallel"))(zsg, *params)


def _sg_bwd(zsg, dy, params, tb, name):
    t = zsg.shape[0]
    f = _sg_fn(tb // SG_C)
    par, par_shapes = _sg_specs()

    def body(z_ref, dy_ref, g_ref, b_ref, w_ref, bs_ref, dz_ref, dg_ref, db_ref, dw_ref, dbs_ref):
        accs = (dg_ref, db_ref, dw_ref, dbs_ref)

        @pl.when(pl.program_id(0) == 0)
        def _():
            for r in accs:
                r[...] = jnp.zeros_like(r)

        _, vjp = jax.vjp(f, z_ref[...], g_ref[...], b_ref[...], w_ref[...], bs_ref[...])
        cts = vjp(dy_ref[...])
        dz_ref[...] = cts[0]
        for r, ct in zip(accs, cts[1:]):
            r[...] += ct

    row = lambda i: (i, 0)
    return pl.pallas_call(
        body, grid=(t // tb,), in_specs=[pl.BlockSpec((tb, 3 * D_SG), row), pl.BlockSpec((tb, D_SG), row)] + par,
        out_specs=[pl.BlockSpec((tb, 3 * D_SG), row)] + par,
        out_shape=[jax.ShapeDtypeStruct((t, 3 * D_SG), F32)] + [jax.ShapeDtypeStruct(sh, F32) for sh in par_shapes],
        name=name, compiler_params=_cp("arbitrary"))(zsg, dy, *params)


def _out_fwd(x, ys, p, w_out, pg, w_gate, w_ple, tb, name):
    t = x.shape[0]

    def body(x_ref, y0, y1, y2, p_ref, wo_ref, pg_ref, wg_ref, wp_ref, o_ref):
        y = jnp.concatenate([y0[...], y1[...], y2[...]], axis=1).astype(BF16)
        x1 = x_ref[...] + jnp.dot(y, wo_ref[...], preferred_element_type=F32)
        hn = _rms(x1, pg_ref[...]).astype(BF16)
        gate = jax.nn.sigmoid(jnp.dot(hn, wg_ref[...], preferred_element_type=F32))
        pp = jnp.dot(p_ref[...].astype(BF16), wp_ref[...], preferred_element_type=F32)
        o_ref[...] = x1 + gate * pp

    row = lambda i: (i, 0)
    full = lambda i: (0, 0)
    return pl.pallas_call(
        body, grid=(t // tb,),
        in_specs=[pl.BlockSpec((tb, D), row), pl.BlockSpec((tb, D_SSM), row), pl.BlockSpec((tb, D_DN), row),
                  pl.BlockSpec((tb, D_SG), row), pl.BlockSpec((tb, D_PLE), row), pl.BlockSpec((D, D), full),
                  pl.BlockSpec((1, D), full), pl.BlockSpec((D, D), full), pl.BlockSpec((D_PLE, D), full)],
        out_specs=pl.BlockSpec((tb, D), row), out_shape=jax.ShapeDtypeStruct((t, D), F32),
        name=name, compiler_params=_cp("parallel"))(x, *ys, p, w_out, pg, w_gate, w_ple)


def _out_bwd(x, ys, p, dx2, w_out, pg, w_gate, w_ple, tb, name):
    t = x.shape[0]

    def body(x_ref, y0, y1, y2, p_ref, d_ref, wo_ref, pg_ref, wg_ref, wp_ref,
             dx_ref, dy0, dy1, dy2, dwo_ref, dpg_ref, dwg_ref, dwp_ref):
        accs = (dwo_ref, dpg_ref, dwg_ref, dwp_ref)

        @pl.when(pl.program_id(0) == 0)
        def _():
            for r in accs:
                r[...] = jnp.zeros_like(r)

        y = jnp.concatenate([y0[...], y1[...], y2[...]], axis=1).astype(BF16)
        x1 = x_ref[...] + jnp.dot(y, wo_ref[...], preferred_element_type=F32)
        hn, rms_vjp = jax.vjp(_rms, x1, pg_ref[...])
        hb = hn.astype(BF16)
        gate = jax.nn.sigmoid(jnp.dot(hb, wg_ref[...], preferred_element_type=F32))
        pb = p_ref[...].astype(BF16)
        pp = jnp.dot(pb, wp_ref[...], preferred_element_type=F32)
        d2 = d_ref[...]
        dpp = (d2 * gate).astype(BF16)
        dlog = (d2 * pp * gate * (1.0 - gate)).astype(BF16)
        dwp_ref[...] += _dg(pb, dpp, 0, 0)
        dwg_ref[...] += _dg(hb, dlog, 0, 0)
        dx1_n, dpg = rms_vjp(_dg(dlog, wg_ref[...], 1, 1))
        dpg_ref[...] += dpg
        dx1 = d2 + dx1_n
        dx_ref[...] = dx1
        db = dx1.astype(BF16)
        dwo_ref[...] += _dg(y, db, 0, 0)
        dy = _dg(db, wo_ref[...], 1, 1)
        dy0[...] = dy[:, :D_SSM]
        dy1[...] = dy[:, D_SSM:D_SSM + D_DN]
        dy2[...] = dy[:, D_SSM + D_DN:]

    row = lambda i: (i, 0)
    full = lambda i: (0, 0)
    acts = [pl.BlockSpec((tb, D), row), pl.BlockSpec((tb, D_SSM), row), pl.BlockSpec((tb, D_DN), row), pl.BlockSpec((tb, D_SG), row)]
    wts = [pl.BlockSpec((D, D), full), pl.BlockSpec((1, D), full), pl.BlockSpec((D, D), full), pl.BlockSpec((D_PLE, D), full)]
    return pl.pallas_call(
        body, grid=(t // tb,),
        in_specs=acts + [pl.BlockSpec((tb, D_PLE), row), pl.BlockSpec((tb, D), row)] + wts,
        out_specs=acts + wts,
        out_shape=[jax.ShapeDtypeStruct((t, n), F32) for n in (D, D_SSM, D_DN, D_SG)]
        + [jax.ShapeDtypeStruct(sh, F32) for sh in ((D, D), (1, D), (D, D), (D_PLE, D))],
        name=name, compiler_params=_cp("arbitrary"))(x, *ys, p, dx2, w_out, pg, w_gate, w_ple)


def _loss_head(x, fg, target, tb, name):
    t = x.shape[0]

    def body(x_ref, g_ref, t_ref, dx_ref, dg_ref, loss_ref):
        @pl.when(pl.program_id(0) == 0)
        def _():
            dg_ref[...] = jnp.zeros_like(dg_ref)
            loss_ref[...] = jnp.zeros_like(loss_ref)

        y, vjp = jax.vjp(_rms, x_ref[...], g_ref[...])
        err = y - t_ref[...]
        loss_ref[...] += jnp.zeros_like(loss_ref) + 0.5 * jnp.sum(err * err) / D
        dx, dg = vjp(err / D)
        dx_ref[...] = dx
        dg_ref[...] += dg

    row = lambda i: (i, 0)
    full = lambda i: (0, 0)
    return pl.pallas_call(
        body, grid=(t // tb,),
        in_specs=[pl.BlockSpec((tb, D), row), pl.BlockSpec((1, D), full), pl.BlockSpec((tb, D), row)],
        out_specs=[pl.BlockSpec((tb, D), row), pl.BlockSpec((1, D), full), pl.BlockSpec((1, LANES), full)],
        out_shape=[jax.ShapeDtypeStruct((t, D), F32), jax.ShapeDtypeStruct((1, D), F32), jax.ShapeDtypeStruct((1, LANES), F32)],
        name=name, compiler_params=_cp("arbitrary"))(x, fg, target)


def _hbm_specs(n):
    return [pl.BlockSpec(memory_space=pl.ANY)] * n


def _all_gather(blocks, name):
    n = len(blocks)

    def body(*refs):
        ins, outs = refs[:n], refs[n:2 * n]
        send_sems, recv_sems, local_sems = refs[2 * n:]
        x, y, c = lax.axis_index("x"), lax.axis_index("y"), lax.axis_index("c")
        me, sibling = (x, y, c), (x, y, 1 - c)
        chips = [(1 - x, y), (x, 1 - y), (1 - x, 1 - y)]

        def slot(a, px, py, pc):
            return outs[a].at[4 * px + 2 * py + pc]

        def copy(a, k, blk, to, src=None):
            return pltpu.make_async_remote_copy(
                src_ref=slot(a, *blk) if src is None else src, dst_ref=slot(a, *blk),
                send_sem=send_sems.at[7 * a + k], recv_sem=recv_sems.at[7 * a + k],
                device_id=to, device_id_type=pl.DeviceIdType.MESH)

        mines = [pltpu.make_async_copy(ins[a], slot(a, *me), local_sems.at[a]) for a in range(n)]
        for cp in mines:
            cp.start()
        first = []
        for a in range(n):
            first.append(copy(a, 0, me, sibling, src=ins[a]))
            first += [copy(a, 1 + j, me, (*chip, c), src=ins[a]) for j, chip in enumerate(chips)]
        for cp in first:
            cp.start()
        passed = []
        for j, chip in enumerate(chips):
            for a in range(n):
                copy(a, 1 + j, (*chip, c), me).wait_recv()
                onward = copy(a, 4 + j, (*chip, c), sibling)
                onward.start()
                passed.append(onward)
        for a in range(n):
            copy(a, 0, sibling, me).wait_recv()
        for j, chip in enumerate(chips):
            for a in range(n):
                copy(a, 4 + j, (*chip, 1 - c), me).wait_recv()
        for cp in first + passed:
            cp.wait_send()
        for cp in mines:
            cp.wait()

    return pl.pallas_call(
        body, out_shape=[jax.ShapeDtypeStruct((N_DEV, *b.shape), b.dtype) for b in blocks],
        in_specs=_hbm_specs(n), out_specs=_hbm_specs(n),
        scratch_shapes=[pltpu.SemaphoreType.DMA((7 * n,)), pltpu.SemaphoreType.DMA((7 * n,)), pltpu.SemaphoreType.DMA((n,))],
        name=name)(*blocks)


def _pair_exchange(gs, name):
    n = len(gs)

    def body(*refs):
        ins, recvs = refs[:n], refs[n:2 * n]
        send_sems, recv_sems = refs[2 * n:]
        x, y, c = lax.axis_index("x"), lax.axis_index("y"), lax.axis_index("c")
        remote = [pltpu.make_async_remote_copy(
            src_ref=ins[a], dst_ref=recvs[a], send_sem=send_sems.at[a], recv_sem=recv_sems.at[a],
            device_id=(x, y, 1 - c), device_id_type=pl.DeviceIdType.MESH) for a in range(n)]
        for cp in remote:
            cp.start()
        for cp in remote:
            cp.wait_send()
            cp.wait_recv()

    return pl.pallas_call(
        body, out_shape=[jax.ShapeDtypeStruct(g.shape, g.dtype) for g in gs], in_specs=_hbm_specs(n), out_specs=_hbm_specs(n),
        scratch_shapes=[pltpu.SemaphoreType.DMA((n,)), pltpu.SemaphoreType.DMA((n,))],
        name=name)(*gs)


def _chip_exchange(ps, name):
    n = len(ps)

    def body(*refs):
        ins, outs = refs[:n], refs[n:2 * n]
        send_sems, recv_sems, local_sems = refs[2 * n:]
        x, y, c = lax.axis_index("x"), lax.axis_index("y"), lax.axis_index("c")
        my_chip = 2 * x + y
        local = [pltpu.make_async_copy(ins[a].at[my_chip], outs[a].at[my_chip], local_sems.at[a]) for a in range(n)]
        remote = []
        for j in range(1, 4):
            px = 1 - x if j & 2 else x
            py = 1 - y if j & 1 else y
            for a in range(n):
                remote.append(pltpu.make_async_remote_copy(
                    src_ref=ins[a].at[2 * px + py], dst_ref=outs[a].at[my_chip],
                    send_sem=send_sems.at[3 * a + j - 1], recv_sem=recv_sems.at[3 * a + j - 1],
                    device_id=(px, py, c), device_id_type=pl.DeviceIdType.MESH))
        for cp in local + remote:
            cp.start()
        for cp in remote:
            cp.wait_send()
            cp.wait_recv()
        for cp in local:
            cp.wait()

    return pl.pallas_call(
        body, out_shape=[jax.ShapeDtypeStruct(q.shape, q.dtype) for q in ps], in_specs=_hbm_specs(n), out_specs=_hbm_specs(n),
        scratch_shapes=[pltpu.SemaphoreType.DMA((3 * n,)), pltpu.SemaphoreType.DMA((3 * n,)), pltpu.SemaphoreType.DMA((n,))],
        name=name)(*ps)


def _row_block(rows, bytes_per_row):
    best = None
    for rb in range(16, rows + 1, 16):
        if rows % rb == 0 and rb * bytes_per_row <= ELEMENTWISE_STEP_BYTES:
            best = rb
    return rows if best is None else best


def _add_pair(own, recv, name):
    shape = own.shape
    last = shape[-1]
    rows = own.size // last
    rb = _row_block(rows, 3 * 4 * (-(-last // LANES) * LANES))

    def body(a_ref, b_ref, o_ref):
        o_ref[...] = (a_ref[...].astype(F32) + b_ref[...].astype(F32)).astype(o_ref.dtype)

    row = lambda i: (i, 0)
    out = pl.pallas_call(
        body, grid=(rows // rb,), in_specs=[pl.BlockSpec((rb, last), row)] * 2, out_specs=pl.BlockSpec((rb, last), row),
        out_shape=jax.ShapeDtypeStruct((rows, last), own.dtype), name=name,
        compiler_params=_cp("parallel"))(own.reshape(rows, last), recv.reshape(rows, last))
    return out.reshape(shape)


def _sum_adamw(gk, w, m, v, name):
    shape = w.shape
    n_part = gk.shape[0]
    last = shape[-1]
    rows = w.size // last
    rb = _row_block(rows, (n_part + 7) * 4 * (-(-last // LANES) * LANES))

    def body(g_ref, w_ref, m_ref, v_ref, go_ref, d_ref, mo_ref, vo_ref):
        g = g_ref[0].astype(F32)
        for k in range(1, n_part):
            g = g + g_ref[k].astype(F32)
        mn = ADAM_B1 * m_ref[...] + (1.0 - ADAM_B1) * g
        vn = ADAM_B2 * v_ref[...] + (1.0 - ADAM_B2) * jnp.square(g)
        m_hat = mn / (1.0 - ADAM_B1 ** ADAM_STEP)
        v_hat = vn / (1.0 - ADAM_B2 ** ADAM_STEP)
        go_ref[...] = g
        d_ref[...] = -ADAM_LR * (m_hat / (jnp.sqrt(v_hat) + ADAM_EPS) + ADAM_WD * w_ref[...])
        mo_ref[...] = mn
        vo_ref[...] = vn

    row = lambda i: (i, 0)
    outs = pl.pallas_call(
        body, grid=(rows // rb,),
        in_specs=[pl.BlockSpec((n_part, rb, last), lambda i: (0, i, 0))] + [pl.BlockSpec((rb, last), row)] * 3,
        out_specs=[pl.BlockSpec((rb, last), row)] * 4,
        out_shape=[jax.ShapeDtypeStruct((rows, last), F32)] * 4,
        name=name, compiler_params=_cp("parallel"))(gk.reshape(n_part, rows, last), *[a.reshape(rows, last) for a in (w, m, v)])
    return [o.reshape(shape) for o in outs]


def _seg_rows(shape):
    n = 1
    for d in shape:
        n *= d
    return -(-n // (8 * LANES)) * 8


def _pack(arrs):
    segs = []
    for a in arrs:
        r = _seg_rows(a.shape)
        segs.append(jnp.pad(a.reshape(-1).astype(F32), (0, r * LANES - a.size)).reshape(r, LANES))
    rows = sum(s.shape[0] for s in segs)
    total = -(-rows // PACK_ROWS) * PACK_ROWS
    if total > rows:
        segs.append(jnp.zeros((total - rows, LANES), F32))
    return jnp.concatenate(segs, axis=0)


def _unpack(pack, shapes):
    out, off = [], 0
    for sh in shapes:
        r = _seg_rows(sh)
        n = 1
        for d in sh:
            n *= d
        out.append(pack[off:off + r].reshape(-1)[:n].reshape(sh))
        off += r
    return out


def _to_dest_blocks(full, axis, dtype):
    sh = list(full.shape)
    sh[axis:axis + 1] = [N_DEV // 2, 2, sh[axis] // N_DEV]
    return jnp.moveaxis(full.reshape(sh), (axis, axis + 1), (1, 0)).astype(dtype)


def _from_gathered(g, axis):
    m = jnp.moveaxis(g, 0, axis)
    sh = list(m.shape)
    sh[axis:axis + 2] = [sh[axis] * sh[axis + 1]]
    return m.reshape(sh)


def _reorder_w_in(w):
    return jnp.concatenate([w[:, :2048], w[:, 2056:3336], w[:, 2048:2056], jnp.zeros((D, ZW - 3336), w.dtype)], axis=1)


def _restore_dw_in(dw):
    return jnp.concatenate([dw[:, :2048], dw[:, 3328:3336], dw[:, 2048:3328]], axis=1)


def _local_step(x, p, wts, target):
    bl, s, _ = x.shape
    t = bl * s
    depth = p.shape[0]
    tb, sg_tb, prep_tb = TB, SG_TB, PREP_TB

    def by_example(a):
        return a.reshape(bl, s, a.shape[-1])

    def flat(a):
        return a.reshape(t, a.shape[-1])

    xs = [x.reshape(t, D)]
    saved = []
    for i in range(depth):
        li = f"l{i}"
        ng = wts['norm_g'][i].reshape(1, D)
        w_in = _reorder_w_in(wts['w_in'][i]).astype(BF16)
        s5_par_in = (wts['ssm_a_re'][i], wts['ssm_a_im'][i], wts['ssm_b_re'][i], wts['ssm_b_im'][i],
                     wts['ssm_c_re'][i], wts['ssm_c_im'][i], wts['ssm_d'][i], wts['ssm_log_step'][i])
        tabs, tab_vjp = jax.vjp(_s5_tables, *s5_par_in)
        s5_par = (*tabs, wts['ssm_w_glu'][i], wts['ssm_b_glu'][i].reshape(1, D_SSM))
        s5_const = _s5_powers(wts['ssm_a_re'][i], wts['ssm_a_im'][i], wts['ssm_log_step'][i])
        conv8 = jnp.pad(wts['dn_conv_w'][i], ((0, 4), (0, 0)))
        dn_par = (jnp.repeat(wts['dn_a_log'][i], DH).reshape(1, D_DN), jnp.repeat(wts['dn_dt_bias'][i], DH).reshape(1, D_DN),
                  wts['dn_norm_g'][i].reshape(1, DH))
        sg_par = (wts['sg_ln_g'][i].reshape(1, D_SG), wts['sg_ln_b'][i].reshape(1, D_SG), wts['sg_w'][i],
                  jnp.pad(jnp.transpose(wts['sg_b'][i]), ((0, 0), (0, LANES - 4))))
        out_par = (wts['w_out'][i].astype(BF16), wts['ple_norm_g'][i].reshape(1, D), wts['w_ple_gate'][i].astype(BF16),
                   wts['w_ple'][i].astype(BF16))
        pi = p[i].reshape(t, D_PLE)

        z_ssm, z_qkv, z_gdn, z_sg, z_ab = _in_proj_fwd(xs[i], ng, w_in, tb, f"in_proj_fwd_{li}")
        qkvn = _dn_prep_fwd(z_qkv, conv8, bl, s, prep_tb, f"dn_prep_fwd_{li}")
        y_ssm, carries, h_all, y_dn, states, tinvs = _mix_fwd(by_example(z_ssm), s5_par, s5_const, by_example(qkvn),
                                                              by_example(z_ab), by_example(z_gdn), dn_par, bl, s, f"mix_fwd_{li}")
        y_sg = _sg_fwd(z_sg, sg_par, sg_tb, f"sg_fwd_{li}")
        ys = (flat(y_ssm), flat(y_dn), y_sg)
        xs.append(_out_fwd(xs[i], ys, pi, *out_par, tb, f"out_fwd_{li}"))
        saved.append(dict(ng=ng, w_in=w_in, tab_vjp=tab_vjp, s5_par=s5_par, s5_const=s5_const, conv8=conv8, dn_par=dn_par,
                          sg_par=sg_par, out_par=out_par, pi=pi, z=(z_ssm, z_qkv, z_gdn, z_sg, z_ab), carries=carries,
                          h_all=h_all, qkvn=qkvn,
                          states=states, tinvs=tinvs, ys=ys))

    dx, dfg, loss_vec = _loss_head(xs[depth], wts['final_norm_g'].reshape(1, D), target.reshape(t, D), tb, "loss_head")
    grads = {n: [None] * depth for n in WEIGHTS if n != 'final_norm_g'}
    grads['final_norm_g'] = dfg.reshape(D)
    for i in reversed(range(depth)):
        li = f"l{i}"
        sv = saved[i]
        z_ssm, z_qkv, z_gdn, z_sg, z_ab = sv['z']
        dx_res, dy_ssm, dy_dn, dy_sg, dwo, dpg, dwg, dwp = _out_bwd(xs[i], sv['ys'], sv['pi'], dx, *sv['out_par'], tb, f"out_bwd_{li}")
        dz_sg, dlng, dlnb, dsgw, dbsp = _sg_bwd(z_sg, dy_sg, sv['sg_par'], sg_tb, f"sg_bwd_{li}")
        (dz_ssm, dbb, dcb, dlam, ddv, dwglu, dbglu), (dqkvn, dz_ab, dz_gdn, dal, ddt, dng) = _mix_bwd(
            by_example(z_ssm), sv['carries'], sv['h_all'], by_example(dy_ssm), sv['s5_par'], sv['s5_const'],
            by_example(sv['qkvn']), by_example(z_ab), by_example(z_gdn), sv['states'], sv['tinvs'], by_example(dy_dn),
            sv['dn_par'], bl, s, f"mix_bwd_{li}")
        dz_qkv, dconv = _dn_prep_bwd(z_qkv, flat(dqkvn), sv['conv8'], bl, s, prep_tb, f"dn_prep_bwd_{li}")
        dzs = (flat(dz_ssm), dz_qkv, flat(dz_gdn), dz_sg, flat(dz_ab))
        dx, dnorm = _in_proj_bwd_dx(xs[i], sv['ng'], sv['w_in'], dzs, dx_res, tb, f"in_proj_bwd_dx_{li}")
        dws = _in_proj_bwd_dw(xs[i], sv['ng'], dzs, tb, f"in_proj_bwd_dw_{li}")
        ds5 = sv['tab_vjp']((dbb, dcb, dlam, ddv))
        for n, gval in zip(('ssm_a_re', 'ssm_a_im', 'ssm_b_re', 'ssm_b_im', 'ssm_c_re', 'ssm_c_im', 'ssm_d', 'ssm_log_step'), ds5):
            grads[n][i] = gval
        grads['norm_g'][i] = dnorm.reshape(D)
        grads['w_in'][i] = _restore_dw_in(jnp.concatenate(dws, axis=1))
        grads['ssm_w_glu'][i] = dwglu
        grads['ssm_b_glu'][i] = dbglu.reshape(D_SSM)
        grads['dn_conv_w'][i] = dconv[:4]
        grads['dn_a_log'][i] = dal.reshape(H, DH).sum(axis=1)
        grads['dn_dt_bias'][i] = ddt.reshape(H, DH).sum(axis=1)
        grads['dn_norm_g'][i] = dng.reshape(DH)
        grads['sg_ln_g'][i] = dlng.reshape(D_SG)
        grads['sg_ln_b'][i] = dlnb.reshape(D_SG)
        grads['sg_w'][i] = dsgw
        grads['sg_b'][i] = jnp.transpose(dbsp[:, :4])
        grads['w_out'][i] = dwo
        grads['ple_norm_g'][i] = dpg.reshape(D)
        grads['w_ple_gate'][i] = dwg
        grads['w_ple'][i] = dwp
    grads = {n: (g if n == 'final_norm_g' else jnp.stack(g)) for n, g in grads.items()}
    return loss_vec[0, 0], dx.reshape(bl, s, D), grads


def kernel(x, p, norm_g, w_in, ssm_a_re, ssm_a_im, ssm_b_re, ssm_b_im, ssm_c_re, ssm_c_im, ssm_d, ssm_log_step, ssm_w_glu, ssm_b_glu, dn_conv_w, dn_a_log, dn_dt_bias, dn_norm_g, sg_ln_g, sg_ln_b, sg_w, sg_b, w_out, ple_norm_g, w_ple_gate, w_ple, final_norm_g, loss_target, m_norm_g, m_w_in, m_ssm_a_re, m_ssm_a_im, m_ssm_b_re, m_ssm_b_im, m_ssm_c_re, m_ssm_c_im, m_ssm_d, m_ssm_log_step, m_ssm_w_glu, m_ssm_b_glu, m_dn_conv_w, m_dn_a_log, m_dn_dt_bias, m_dn_norm_g, m_sg_ln_g, m_sg_ln_b, m_sg_w, m_sg_b, m_w_out, m_ple_norm_g, m_w_ple_gate, m_w_ple, m_final_norm_g, v_norm_g, v_w_in, v_ssm_a_re, v_ssm_a_im, v_ssm_b_re, v_ssm_b_im, v_ssm_c_re, v_ssm_c_im, v_ssm_d, v_ssm_log_step, v_ssm_w_glu, v_ssm_b_glu, v_dn_conv_w, v_dn_a_log, v_dn_dt_bias, v_dn_norm_g, v_sg_ln_g, v_sg_ln_b, v_sg_w, v_sg_b, v_w_out, v_ple_norm_g, v_w_ple_gate, v_w_ple, v_final_norm_g):
    w_loc = dict(zip(WEIGHTS, (norm_g, w_in, ssm_a_re, ssm_a_im, ssm_b_re, ssm_b_im, ssm_c_re, ssm_c_im, ssm_d, ssm_log_step,
                               ssm_w_glu, ssm_b_glu, dn_conv_w, dn_a_log, dn_dt_bias, dn_norm_g, sg_ln_g, sg_ln_b, sg_w, sg_b,
                               w_out, ple_norm_g, w_ple_gate, w_ple, final_norm_g)))
    m_loc = dict(zip(WEIGHTS, (m_norm_g, m_w_in, m_ssm_a_re, m_ssm_a_im, m_ssm_b_re, m_ssm_b_im, m_ssm_c_re, m_ssm_c_im, m_ssm_d,
                               m_ssm_log_step, m_ssm_w_glu, m_ssm_b_glu, m_dn_conv_w, m_dn_a_log, m_dn_dt_bias, m_dn_norm_g,
                               m_sg_ln_g, m_sg_ln_b, m_sg_w, m_sg_b, m_w_out, m_ple_norm_g, m_w_ple_gate, m_w_ple, m_final_norm_g)))
    v_loc = dict(zip(WEIGHTS, (v_norm_g, v_w_in, v_ssm_a_re, v_ssm_a_im, v_ssm_b_re, v_ssm_b_im, v_ssm_c_re, v_ssm_c_im, v_ssm_d,
                               v_ssm_log_step, v_ssm_w_glu, v_ssm_b_glu, v_dn_conv_w, v_dn_a_log, v_dn_dt_bias, v_dn_norm_g,
                               v_sg_ln_g, v_sg_ln_b, v_sg_w, v_sg_b, v_w_out, v_ple_norm_g, v_w_ple_gate, v_w_ple, v_final_norm_g)))

    gathered = _all_gather([w_loc[n].astype(WIRE[n]) for n in SHARDED_ORDER], "gather_weights")
    full = dict(w_loc)
    for n, g in zip(SHARDED_ORDER, gathered):
        full[n] = _from_gathered(g, SHARDED[n])
    full['ssm_w_glu'] = full['ssm_w_glu'].astype(F32)

    loss_part, grad_x, grads = _local_step(x, p, full, loss_target)

    dest = [_to_dest_blocks(grads[n], SHARDED[n], WIRE[n]) for n in SHARDED_ORDER]
    c = lax.axis_index("c")
    own = [lax.dynamic_index_in_dim(d, c, 0, keepdims=False) for d in dest]
    for_sibling = [lax.dynamic_index_in_dim(d, 1 - c, 0, keepdims=False) for d in dest]
    from_sibling = _pair_exchange(for_sibling, "grads_pair_exchange")
    chip_sums = [_add_pair(a, b, f"grads_pair_sum_{n}") for n, a, b in zip(SHARDED_ORDER, own, from_sibling)]
    by_chip = _chip_exchange(chip_sums, "grads_chip_exchange")
    rep_pack = _pack([grads[n] for n in REPLICATED_ORDER] + [loss_part.reshape(1)])
    (rep_recv,) = _all_gather([rep_pack], "gather_small_grads")

    outs = {k: {} for k in 'gdmv'}
    for n, gk in zip(SHARDED_ORDER, by_chip):
        for k, o in zip('gdmv', _sum_adamw(gk, w_loc[n], m_loc[n], v_loc[n], f"adamw_{n}")):
            outs[k][n] = o
    one = jnp.zeros((1,), F32)
    rep_out = _sum_adamw(rep_recv, _pack([w_loc[n] for n in REPLICATED_ORDER] + [one]),
                         _pack([m_loc[n] for n in REPLICATED_ORDER] + [one]),
                         _pack([v_loc[n] for n in REPLICATED_ORDER] + [one]), "adamw_replicated")
    rep_shapes = [w_loc[n].shape for n in REPLICATED_ORDER] + [(1,)]
    for k, rep_p in zip('gdmv', rep_out):
        outs[k].update(zip(REPLICATED_ORDER + ['loss'], _unpack(rep_p, rep_shapes)))
    loss = outs['g']['loss'].reshape(())
    return (loss, grad_x, *[outs['g'][n] for n in WEIGHTS], *[outs['d'][n] for n in WEIGHTS],
            *[outs['m'][n] for n in WEIGHTS], *[outs['v'][n] for n in WEIGHTS])
```

```python
import functools

import jax
import jax.numpy as jnp
from jax import lax
from jax.experimental import pallas as pl
from jax.experimental.pallas import tpu as pltpu

F32 = jnp.float32
BF16 = jnp.bfloat16
EPS = 1e-6

D = 1024
D_PLE = 256
D_SSM = 256
D_DN = 512
D_SG = 256
G = 16
CG = 16
NS = 64
NRE = G * NS
H = 4
DH = 128
DN_C = 128
SG_C = 128
ZW = 3456
Z_PIECES = (512, 1536, 512, 768, 128)
N_DEV = 8
LANES = 128
PACK_ROWS = 256
VMEM_LIMIT = 56 * 1024 * 1024
ELEMENTWISE_STEP_BYTES = 4 * 1024 * 1024
TB = 256
TB_DW = 512
SG_TB = 512
PREP_TB = 256

ADAM_LR = 0.001
ADAM_B1 = 0.9
ADAM_B2 = 0.999
ADAM_EPS = 1e-08
ADAM_WD = 0.01
ADAM_STEP = 10

MIX_HEAD_START = 3
S5_L = 128
S5_GROUP = 8
S5_SHIFTS = (1, 2, 4)

WEIGHTS = ['norm_g', 'w_in', 'ssm_a_re', 'ssm_a_im', 'ssm_b_re', 'ssm_b_im', 'ssm_c_re', 'ssm_c_im', 'ssm_d',
           'ssm_log_step', 'ssm_w_glu', 'ssm_b_glu', 'dn_conv_w', 'dn_a_log', 'dn_dt_bias', 'dn_norm_g', 'sg_ln_g',
           'sg_ln_b', 'sg_w', 'sg_b', 'w_out', 'ple_norm_g', 'w_ple_gate', 'w_ple', 'final_norm_g']
SHARDED = {'w_in': 2, 'ssm_w_glu': 1, 'dn_conv_w': 2, 'w_out': 1, 'w_ple_gate': 1, 'w_ple': 2}
SHARDED_ORDER = ['w_in', 'ssm_w_glu', 'dn_conv_w', 'w_out', 'w_ple_gate', 'w_ple']
WIRE = {'w_in': BF16, 'ssm_w_glu': BF16, 'dn_conv_w': F32, 'w_out': BF16, 'w_ple_gate': BF16, 'w_ple': BF16}
REPLICATED_ORDER = [n for n in WEIGHTS if n not in SHARDED]


def _cp(*sem):
    return pltpu.CompilerParams(dimension_semantics=sem, vmem_limit_bytes=VMEM_LIMIT)


def _dg(a, b, ca, cb, precision=None):
    return lax.dot_general(a, b, (((ca,), (cb,)), ((), ())), precision=precision, preferred_element_type=F32)


@jax.custom_vjp
def _mm(a, b):
    return _dg(a.astype(BF16), b.astype(BF16), 1, 0)


def _mm_fwd(a, b):
    return _mm(a, b), (a, b)


def _mm_bwd(res, g):
    a, b = res
    gb = g.astype(BF16)
    return _dg(gb, b.astype(BF16), 1, 1), _dg(a.astype(BF16), gb, 0, 0)


_mm.defvjp(_mm_fwd, _mm_bwd)


@jax.custom_vjp
def _mm_nt(a, b):
    return _dg(a.astype(BF16), b.astype(BF16), 1, 1)


def _mm_nt_fwd(a, b):
    return _mm_nt(a, b), (a, b)


def _mm_nt_bwd(res, g):
    a, b = res
    gb = g.astype(BF16)
    return _dg(gb, b.astype(BF16), 1, 0), _dg(gb, a.astype(BF16), 0, 0)


_mm_nt.defvjp(_mm_nt_fwd, _mm_nt_bwd)


@jax.custom_vjp
def _mm_tn(a, b):
    return _dg(a.astype(BF16), b.astype(BF16), 0, 0)


def _mm_tn_fwd(a, b):
    return _mm_tn(a, b), (a, b)


def _mm_tn_bwd(res, g):
    a, b = res
    gb = g.astype(BF16)
    return _dg(b.astype(BF16), gb, 1, 1), _dg(a.astype(BF16), gb, 1, 0)


_mm_tn.defvjp(_mm_tn_fwd, _mm_tn_bwd)


def _split(x, n):
    pieces = []
    for _ in range(n - 1):
        hi = x.astype(BF16)
        pieces.append(hi)
        x = x - hi.astype(F32)
    pieces.append(x.astype(BF16))
    return pieces


def _dg3(a, b, ca, cb):
    a_hi, a_lo = _split(a, 2)
    b_hi, b_lo = _split(b, 2)
    return _dg(a_hi, b_hi, ca, cb) + (_dg(a_hi, b_lo, ca, cb) + _dg(a_lo, b_hi, ca, cb))


@jax.custom_vjp
def _dot3(a, b):
    return _dg3(a, b, 1, 0)


def _dot3_fwd(a, b):
    return _dot3(a, b), (a, b)


def _dot3_bwd(res, g):
    a, b = res
    return _dg3(g, b, 1, 1), _dg3(a, g, 0, 0)


_dot3.defvjp(_dot3_fwd, _dot3_bwd)


def _dg_sel(x, e, cx, ce, x_first):
    eb = e.astype(BF16)
    out = None
    for piece in reversed(_split(x, 3)):
        term = _dg(piece, eb, cx, ce) if x_first else _dg(eb, piece, ce, cx)
        out = term if out is None else out + term
    return out


@jax.custom_vjp
def _sel_r(x, e):
    return _dg_sel(x, e, 1, 0, True)


def _sel_r_fwd(x, e):
    return _sel_r(x, e), e


def _sel_r_bwd(e, g):
    return _dg_sel(g, e, 1, 1, True), jnp.zeros_like(e)


_sel_r.defvjp(_sel_r_fwd, _sel_r_bwd)


@jax.custom_vjp
def _sel_l(e, x):
    return _dg_sel(x, e, 0, 1, False)


def _sel_l_fwd(e, x):
    return _sel_l(e, x), e


def _sel_l_bwd(e, g):
    return jnp.zeros_like(e), _dg_sel(g, e, 0, 0, False)


_sel_l.defvjp(_sel_l_fwd, _sel_l_bwd)


def _rms(x, g):
    return x * lax.rsqrt(jnp.mean(x * x, axis=-1, keepdims=True) + EPS) * g


def _silu(x):
    return x * jax.nn.sigmoid(x)


def _in_proj_fwd(x, g, w, tb, name):
    t = x.shape[0]

    def body(x_ref, g_ref, w_ref, *z_refs):
        h = _rms(x_ref[...], g_ref[...])
        z = jnp.dot(h.astype(BF16), w_ref[...], preferred_element_type=F32)
        off = 0
        for z_ref, n in zip(z_refs, Z_PIECES):
            z_ref[...] = z[:, off:off + n]
            off += n

    row = lambda i: (i, 0)
    full = lambda i: (0, 0)
    return pl.pallas_call(
        body, grid=(t // tb,),
        in_specs=[pl.BlockSpec((tb, D), row), pl.BlockSpec((1, D), full), pl.BlockSpec((D, ZW), full)],
        out_specs=[pl.BlockSpec((tb, n), row) for n in Z_PIECES],
        out_shape=[jax.ShapeDtypeStruct((t, n), F32) for n in Z_PIECES],
        name=name, compiler_params=_cp("parallel"))(x, g, w)


def _in_proj_bwd_dx(x, g, w, dzs, dx_res, tb, name):
    t = x.shape[0]

    def body(x_ref, g_ref, w_ref, d0, d1, d2, d3, d4, dxr_ref, dx_ref, dg_ref):
        @pl.when(pl.program_id(0) == 0)
        def _():
            dg_ref[...] = jnp.zeros_like(dg_ref)

        dz = jnp.concatenate([d0[...], d1[...], d2[...], d3[...], d4[...]], axis=1).astype(BF16)
        dh = _dg(dz, w_ref[...], 1, 1)
        _, vjp = jax.vjp(_rms, x_ref[...], g_ref[...])
        dx, dg = vjp(dh)
        dx_ref[...] = dx + dxr_ref[...]
        dg_ref[...] += dg

    row = lambda i: (i, 0)
    full = lambda i: (0, 0)
    return pl.pallas_call(
        body, grid=(t // tb,),
        in_specs=[pl.BlockSpec((tb, D), row), pl.BlockSpec((1, D), full), pl.BlockSpec((D, ZW), full)]
        + [pl.BlockSpec((tb, n), row) for n in Z_PIECES] + [pl.BlockSpec((tb, D), row)],
        out_specs=[pl.BlockSpec((tb, D), row), pl.BlockSpec((1, D), full)],
        out_shape=[jax.ShapeDtypeStruct((t, D), F32), jax.ShapeDtypeStruct((1, D), F32)],
        name=name, compiler_params=_cp("arbitrary"))(x, g, w, *dzs, dx_res)


def _in_proj_bwd_dw(x, g, dzs, tb, name):
    t = x.shape[0]

    def body(x_ref, g_ref, d0, d1, d2, d3, d4, *dw_refs):
        @pl.when(pl.program_id(0) == 0)
        def _():
            for r in dw_refs:
                r[...] = jnp.zeros_like(r)

        h = _rms(x_ref[...], g_ref[...]).astype(BF16)
        for d_ref, dw_ref in zip((d0, d1, d2, d3, d4), dw_refs):
            dw_ref[...] += _dg(h, d_ref[...].astype(BF16), 0, 0)

    row = lambda i: (i, 0)
    full = lambda i: (0, 0)
    return pl.pallas_call(
        body, grid=(t // tb,),
        in_specs=[pl.BlockSpec((tb, D), row), pl.BlockSpec((1, D), full)] + [pl.BlockSpec((tb, n), row) for n in Z_PIECES],
        out_specs=[pl.BlockSpec((D, n), full) for n in Z_PIECES],
        out_shape=[jax.ShapeDtypeStruct((D, n), F32) for n in Z_PIECES],
        name=name, compiler_params=_cp("arbitrary"))(x, g, *dzs)


def _lam_pow(a_re, a_im, log_step, k):
    step = jnp.exp(log_step)[:, None]
    mag = jnp.exp(k * a_re * step)
    ang = k * a_im * step
    return mag * jnp.cos(ang), mag * jnp.sin(ang)


def _s5_powers(a_re, a_im, log_step):
    def table(ks):
        re, im = _lam_pow(a_re, a_im, log_step, jnp.asarray(ks, F32)[:, None, None])
        return jnp.concatenate([re.reshape(len(ks), NRE), im.reshape(len(ks), NRE)], axis=-1)

    ld = table(S5_SHIFTS).reshape(len(S5_SHIFTS), 1, 2 * NRE)
    return ld, table(range(1, S5_GROUP + 1)), table(range(S5_GROUP, 0, -1))


def _s5_tables(a_re, a_im, b_re, b_im, c_re, c_im, d_skip, log_step):
    lam_re, lam_im = _lam_pow(a_re, a_im, log_step, 1.0)
    den = a_re * a_re + a_im * a_im
    nr, ni = lam_re - 1.0, lam_im
    f_re = (nr * a_re + ni * a_im) / den
    f_im = (ni * a_re - nr * a_im) / den
    bbar_re = f_re[..., None] * b_re - f_im[..., None] * b_im
    bbar_im = f_re[..., None] * b_im + f_im[..., None] * b_re
    eye = jnp.eye(G, dtype=F32)

    def blk_b(bb):
        return (jnp.transpose(bb, (0, 2, 1))[:, :, None, :] * eye[:, None, :, None]).reshape(D_SSM, NRE)

    def blk_c(cc):
        return (jnp.transpose(cc, (0, 2, 1))[:, :, None, :] * eye[:, None, :, None]).reshape(NRE, D_SSM)

    b_blk = jnp.concatenate([blk_b(bbar_re), blk_b(bbar_im)], axis=1)
    c_blk = jnp.concatenate([blk_c(c_re), -blk_c(c_im)], axis=0)
    lam = jnp.concatenate([lam_re.reshape(1, NRE), lam_im.reshape(1, NRE)], axis=-1)
    return b_blk, c_blk, lam, d_skip.reshape(1, D_SSM)


def _group_shift(x, d, up=False):
    r = lax.broadcasted_iota(jnp.int32, x.shape, 0) & (S5_GROUP - 1)
    if up:
        return jnp.where(r < S5_GROUP - d, pltpu.roll(x, x.shape[0] - d, 0), 0.0)
    return jnp.where(r >= d, pltpu.roll(x, d, 0), 0.0)


def _s5_scan_steps(hr, hi, cr, ci, lds, lp):
    for ld, d in zip(lds, S5_SHIFTS):
        lr, li = ld[:, :NRE], ld[:, NRE:]
        sr, si = _group_shift(hr, d), _group_shift(hi, d)
        hr, hi = hr + lr * sr - li * si, hi + lr * si + li * sr
        yield
    pr, pi = lp[:, :NRE], lp[:, NRE:]
    rows_r, rows_i = [], []
    for r in range(hr.shape[0] // S5_GROUP):
        br, bi = hr[r * S5_GROUP:(r + 1) * S5_GROUP], hi[r * S5_GROUP:(r + 1) * S5_GROUP]
        br, bi = br + pr * cr - pi * ci, bi + pr * ci + pi * cr
        cr, ci = br[S5_GROUP - 1:S5_GROUP], bi[S5_GROUP - 1:S5_GROUP]
        rows_r.append(br)
        rows_i.append(bi)
        if r % 2:
            yield
    return jnp.concatenate(rows_r, axis=0), jnp.concatenate(rows_i, axis=0)


@jax.custom_vjp
def _known_scan(xr, xi, cr, ci, lam, lds, lp_rev, hr, hi):
    return hr, hi


def _known_scan_fwd(xr, xi, cr, ci, lam, lds, lp_rev, hr, hi):
    return (hr, hi), (cr, ci, lam, lds, lp_rev, hr, hi)


def _known_scan_bwd(res, cts):
    cr, ci, lam, lds, lp_rev, hr, hi = res
    ar, ai = cts
    for ld, d in zip(lds, S5_SHIFTS):
        lr, li = ld[:, :NRE], ld[:, NRE:]
        sr, si = _group_shift(ar, d, up=True), _group_shift(ai, d, up=True)
        ar, ai = ar + lr * sr + li * si, ai + lr * si - li * sr
    qr, qi = lp_rev[:, :NRE], lp_rev[:, NRE:]
    nr, ni = jnp.zeros_like(cr), jnp.zeros_like(ci)
    rows_r, rows_i = [], []
    for r in reversed(range(hr.shape[0] // S5_GROUP)):
        br, bi = ar[r * S5_GROUP:(r + 1) * S5_GROUP], ai[r * S5_GROUP:(r + 1) * S5_GROUP]
        br, bi = br + qr * nr + qi * ni, bi + qr * ni - qi * nr
        nr, ni = br[0:1], bi[0:1]
        rows_r.insert(0, br)
        rows_i.insert(0, bi)
    ar, ai = jnp.concatenate(rows_r, axis=0), jnp.concatenate(rows_i, axis=0)
    lr, li = lam[:, :NRE], lam[:, NRE:]
    dcr, dci = lr * nr + li * ni, lr * ni - li * nr
    first = lax.broadcasted_iota(jnp.int32, hr.shape, 0) == 0
    pr = jnp.where(first, cr, pltpu.roll(hr, 1, 0))
    pi = jnp.where(first, ci, pltpu.roll(hi, 1, 0))
    dlam = jnp.concatenate([jnp.sum(ar * pr + ai * pi, axis=0, keepdims=True),
                            jnp.sum(ai * pr - ar * pi, axis=0, keepdims=True)], axis=1)
    return (ar, ai, dcr, dci, dlam, [jnp.zeros_like(ld) for ld in lds], jnp.zeros_like(lp_rev),
            jnp.zeros_like(hr), jnp.zeros_like(hi))


_known_scan.defvjp(_known_scan_fwd, _known_scan_bwd)


def _interleave(short, long, head_start=0):
    gens = list(short) + list(long)
    results = [None] * len(gens)

    def advance(live):
        still = []
        for idx, gen in live:
            try:
                next(gen)
                still.append((idx, gen))
            except StopIteration as done:
                results[idx] = done.value
        return still

    live_short = advance(list(enumerate(gens))[:len(short)])
    live_long = list(enumerate(gens))[len(short):]
    for _ in range(head_start):
        live_long = advance(live_long)
    live = live_short + live_long
    while live:
        live = advance(live)
    return results[:len(short)], results[len(short):]


def _s5_chunk_gen(u, gate, cr, ci, b_blk, c_blk, lam, dv, wglu, bglu, lds, lp, lp_rev, known_h=None):
    bu = _mm(u, b_blk)
    xr, xi = bu[:, :NRE], bu[:, NRE:]
    yield
    if known_h is None:
        hr, hi = yield from _s5_scan_steps(xr, xi, cr, ci, lds, lp)
    else:
        hr, hi = _known_scan(xr, xi, cr, ci, lam, lds, lp_rev, *known_h)
    y = _mm(jnp.concatenate([hr, hi], axis=1), c_blk) + dv * u
    yield
    y = jax.nn.gelu(y)
    y = y * jax.nn.sigmoid(_mm(y, wglu) + bglu)
    return y * _silu(gate), hr, hi


S5_PAR_SHAPES = [(D_SSM, 2 * NRE), (2 * NRE, D_SSM), (1, 2 * NRE), (1, D_SSM), (D_SSM, D_SSM), (1, D_SSM)]
S5_CONST_SHAPES = [(len(S5_SHIFTS), 1, 2 * NRE), (S5_GROUP, 2 * NRE), (S5_GROUP, 2 * NRE)]
DN_PAR_SHAPES = [(1, D_DN), (1, D_DN), (1, DH)]


def _mix_specs(bl, n_c, rev):
    def chunk(i):
        return n_c - 1 - i if rev else i

    def tok(n):
        return pl.BlockSpec((bl, DN_C, n), lambda i: (0, chunk(i), 0))

    def per_chunk(shape):
        return pl.BlockSpec((bl, 1, *shape), lambda i: (0, chunk(i)) + (0,) * len(shape))

    def whole(shape):
        return pl.BlockSpec(shape, lambda i: (0,) * len(shape))

    return tok, per_chunk, whole


def _dn_post(c):
    s = _silu(c)
    parts = []
    for j in range(12):
        xj = s[:, j * DH:(j + 1) * DH]
        if j < 8:
            xj = xj * lax.rsqrt(jnp.sum(xj * xj, axis=-1, keepdims=True) + EPS)
        if j < 4:
            xj = xj * (DH ** -0.5)
        parts.append(xj)
    return jnp.concatenate(parts, axis=1)


def _dn_prep_fwd(zq, conv_w8, bl, s, tb, name):
    n_s = s // tb
    hb = tb // 8
    w3 = 3 * D_DN

    def body(cur_ref, prev_ref, w_ref, o_ref):
        i = pl.program_id(1)
        prev = jnp.where(i > 0, prev_ref[...], 0.0)
        ext = jnp.concatenate([prev, cur_ref[...]], axis=0)
        c = jnp.zeros((tb, w3), F32)
        for k in range(4):
            sh = ext if k == 3 else pltpu.roll(ext, 3 - k, 0)
            c = c + w_ref[k:k + 1, :] * sh[8:, :]
        o_ref[...] = _dn_post(c)

    row = lambda b, i: (b * n_s + i, 0)
    prv = lambda b, i: (jnp.maximum((b * n_s + i) * hb - 1, 0), 0)
    full = lambda b, i: (0, 0)
    t = bl * s
    return pl.pallas_call(
        body, grid=(bl, n_s),
        in_specs=[pl.BlockSpec((tb, w3), row), pl.BlockSpec((8, w3), prv), pl.BlockSpec((8, w3), full)],
        out_specs=pl.BlockSpec((tb, w3), row),
        out_shape=jax.ShapeDtypeStruct((t, w3), F32),
        name=name, compiler_params=_cp("parallel", "parallel"))(zq, zq, conv_w8)


def _dn_prep_bwd(zq, dqkv, conv_w8, bl, s, tb, name):
    n_s = s // tb
    hb = tb // 8
    w3 = 3 * D_DN
    n_blk8 = bl * s // 8

    def body(cur_ref, prev_ref, next_ref, d_ref, dnext_ref, w_ref, dz_ref, dw_ref):
        b, i = pl.program_id(0), pl.program_id(1)

        @pl.when((b == 0) & (i == 0))
        def _():
            dw_ref[...] = jnp.zeros_like(dw_ref)

        prev = jnp.where(i > 0, prev_ref[...], 0.0)
        nxt = jnp.where(i < n_s - 1, next_ref[...], 0.0)
        dnxt = jnp.where(i < n_s - 1, dnext_ref[...], 0.0)
        ext = jnp.concatenate([prev, cur_ref[...], nxt], axis=0)
        shifted = [ext if k == 3 else pltpu.roll(ext, 3 - k, 0) for k in range(4)]
        c2 = jnp.zeros((tb + 8, w3), F32)
        for k in range(4):
            c2 = c2 + w_ref[k:k + 1, :] * shifted[k][8:, :]
        dpost = jnp.concatenate([d_ref[...], dnxt], axis=0)
        _, vjp = jax.vjp(_dn_post, c2)
        (dc2,) = vjp(dpost)
        dz = jnp.zeros((tb, w3), F32)
        for k in range(4):
            up = dc2 if k == 3 else pltpu.roll(dc2, tb + 8 - (3 - k), 0)
            dz = dz + w_ref[k:k + 1, :] * up[:tb, :]
            dw_ref[k:k + 1, :] += jnp.sum(dc2[:tb, :] * shifted[k][8:8 + tb, :], axis=0, keepdims=True)
        dz_ref[...] = dz.astype(BF16)

    row = lambda b, i: (b * n_s + i, 0)
    prv = lambda b, i: (jnp.maximum((b * n_s + i) * hb - 1, 0), 0)
    nxt = lambda b, i: (jnp.minimum((b * n_s + i + 1) * hb, n_blk8 - 1), 0)
    full = lambda b, i: (0, 0)
    t = bl * s
    return pl.pallas_call(
        body, grid=(bl, n_s),
        in_specs=[pl.BlockSpec((tb, w3), row), pl.BlockSpec((8, w3), prv), pl.BlockSpec((8, w3), nxt),
                  pl.BlockSpec((tb, w3), row), pl.BlockSpec((8, w3), nxt), pl.BlockSpec((8, w3), full)],
        out_specs=[pl.BlockSpec((tb, w3), row), pl.BlockSpec((8, w3), full)],
        out_shape=[jax.ShapeDtypeStruct((t, w3), BF16), jax.ShapeDtypeStruct((8, w3), F32)],
        name=name, compiler_params=_cp("arbitrary", "arbitrary"))(zq, zq, zq, dqkv, dqkv, conv_w8)


def _unit_lower_inverse_steps(ms):
    c_len = ms[0].shape[0]
    eye = lax.broadcasted_iota(jnp.int32, (c_len, c_len), 0) == lax.broadcasted_iota(jnp.int32, (c_len, c_len), 1)
    ident = jnp.where(eye, 1.0, 0.0)
    ps = ms
    tinvs = [ident - m for m in ms]
    for _ in range(c_len.bit_length() - 2):
        ps = [_dg3(p, p, 1, 0) for p in ps]
        yield
        tinvs = [t + _dg3(t, p, 1, 0) for t, p in zip(tinvs, ps)]
        yield
    return tinvs


@jax.custom_vjp
def _known_inverses(ms, tinvs):
    return tinvs


def _known_inverses_fwd(ms, tinvs):
    return tinvs, tinvs


def _known_inverses_bwd(tinvs, gs):
    return [-_dg3(_dg3(t, g, 0, 0), t, 1, 1) for t, g in zip(tinvs, gs)], [jnp.zeros_like(t) for t in tinvs]


_known_inverses.defvjp(_known_inverses_fwd, _known_inverses_bwd)


def _dn_chunk_gen(qkv, zab, zg, states, alog_e, dt_e, ng, known_tinvs=None):
    c_len = DN_C
    r = lax.broadcasted_iota(jnp.int32, (c_len, c_len), 0)
    c = lax.broadcasted_iota(jnp.int32, (c_len, c_len), 1)
    causal, strict = r >= c, r > c
    tril = jnp.where(causal, 1.0, 0.0)
    rr = lax.broadcasted_iota(jnp.int32, (LANES, D_DN), 0)
    cc = lax.broadcasted_iota(jnp.int32, (LANES, D_DN), 1)
    e_a = jnp.where((cc >= rr * DH) & (cc < rr * DH + DH) & (rr < H), 1.0, 0.0)
    e_b = jnp.where((cc >= (rr - H) * DH) & (cc < (rr - H) * DH + DH) & (rr >= H) & (rr < 2 * H), 1.0, 0.0)
    a_e = _sel_r(zab, e_a)
    b_e = _sel_r(zab, e_b)
    beta = jax.nn.sigmoid(b_e)
    g = -jnp.exp(alog_e) * jax.nn.softplus(a_e + dt_e)
    yield
    gc = _sel_l(tril, g)
    glast = jnp.sum(g, axis=0, keepdims=True)
    eg = jnp.exp(gc)
    ekd = jnp.exp(glast - gc)
    dl = jnp.exp(glast)
    yield
    heads = range(H)
    sls = [slice(h * DH, (h + 1) * DH) for h in heads]
    qs = [qkv[:, h * DH:(h + 1) * DH] for h in heads]
    ks = [qkv[:, D_DN + h * DH:D_DN + (h + 1) * DH] for h in heads]
    vs = [qkv[:, 2 * D_DN + h * DH:2 * D_DN + (h + 1) * DH] for h in heads]
    ccols = [gc[:, sl] for sl in sls]
    decs = [jnp.where(causal, jnp.exp(jnp.where(causal, cl - jnp.transpose(cl), 0.0)), 0.0) for cl in ccols]
    kbs = [k * beta[:, sl] for k, sl in zip(ks, sls)]
    ms = [jnp.where(strict, _mm_nt(kb, k) * dec, 0.0) for kb, k, dec in zip(kbs, ks, decs)]
    yield
    if known_tinvs is None:
        tinvs = yield from _unit_lower_inverse_steps(ms)
    else:
        tinvs = _known_inverses(ms, list(known_tinvs))
    sols = [_dot3(t, jnp.concatenate([v * beta[:, sl], kb * eg[:, sl]], axis=1))
            for t, v, kb, sl in zip(tinvs, vs, kbs, sls)]
    yield
    atts = [_mm_nt(q, k) * dec for q, k, dec in zip(qs, ks, decs)]
    vnews = [sol[:, :DH] - _mm(sol[:, DH:], st) for sol, st in zip(sols, states)]
    yield
    os_ = [_mm(q * eg[:, sl], st) + _mm(att, vn) for q, sl, st, att, vn in zip(qs, sls, states, atts, vnews)]
    yield
    new_states = [st * dl[:, sl] + _mm_tn(k * ekd[:, sl], vn) for st, sl, k, vn in zip(states, sls, ks, vnews)]
    yield
    ys = [_rms(o, ng) * _silu(zg[:, sl]) for o, sl in zip(os_, sls)]
    return jnp.concatenate(ys, axis=1), new_states, tinvs


def _mix_fwd(zs, s5_par, s5_const, qkv, zab, zg, dn_par, bl, s, name):
    assert S5_L == DN_C
    n_c = s // DN_C
    nd = len(S5_SHIFTS)
    tok, per_chunk, whole = _mix_specs(bl, n_c, False)

    def body(z_ref, b_ref, c_ref, lam_ref, dv_ref, wg_ref, bg_ref, ld_ref, lp_ref, lpr_ref,
             q_ref, ab_ref, zg_ref, al_ref, dt_ref, ng_ref,
             ys_ref, car_ref, h_ref, yd_ref, st_ref, ti_ref, cs, ssc):
        @pl.when(pl.program_id(0) == 0)
        def _():
            cs[...] = jnp.zeros_like(cs)
            ssc[...] = jnp.zeros_like(ssc)

        lds = [ld_ref[k] for k in range(nd)]
        s5_gens, dn_gens = [], []
        for e in range(bl):
            c = cs[e]
            car_ref[e, 0] = c
            sts = [ssc[e, h] for h in range(H)]
            for h in range(H):
                st_ref[e, 0, h] = sts[h]
            z = z_ref[e]
            s5_gens.append(_s5_chunk_gen(z[:, :D_SSM], z[:, D_SSM:], c[:, :NRE], c[:, NRE:], b_ref[...], c_ref[...], lam_ref[...],
                                         dv_ref[...], wg_ref[...], bg_ref[...], lds, lp_ref[...], lpr_ref[...]))
            dn_gens.append(_dn_chunk_gen(q_ref[e], ab_ref[e], zg_ref[e], sts, al_ref[...], dt_ref[...], ng_ref[...]))
        s5_outs, dn_outs = _interleave(s5_gens, dn_gens, head_start=MIX_HEAD_START)
        for e in range(bl):
            y_s, hr, hi = s5_outs[e]
            y_d, new_sts, tinvs = dn_outs[e]
            ys_ref[e] = y_s
            h_ref[e, :, :NRE] = hr
            h_ref[e, :, NRE:] = hi
            cs[e, :, :NRE] = hr[S5_L - 1:S5_L]
            cs[e, :, NRE:] = hi[S5_L - 1:S5_L]
            yd_ref[e] = y_d
            for h in range(H):
                ssc[e, h] = new_sts[h]
                ti_ref[e, 0, h] = tinvs[h]

    head_mats = jax.ShapeDtypeStruct((bl, n_c, H, DH, DH), F32)
    return pl.pallas_call(
        body, grid=(n_c,),
        in_specs=[tok(2 * D_SSM)] + [whole(sh) for sh in S5_PAR_SHAPES + S5_CONST_SHAPES]
        + [tok(3 * D_DN), tok(LANES), tok(D_DN)] + [whole(sh) for sh in DN_PAR_SHAPES],
        out_specs=[tok(D_SSM), per_chunk((1, 2 * NRE)), tok(2 * NRE), tok(D_DN), per_chunk((H, DH, DH)), per_chunk((H, DH, DH))],
        out_shape=[jax.ShapeDtypeStruct((bl, s, D_SSM), F32), jax.ShapeDtypeStruct((bl, n_c, 1, 2 * NRE), F32),
                   jax.ShapeDtypeStruct((bl, s, 2 * NRE), F32), jax.ShapeDtypeStruct((bl, s, D_DN), F32), head_mats, head_mats],
        scratch_shapes=[pltpu.VMEM((bl, 1, 2 * NRE), F32), pltpu.VMEM((bl, H, DH, DH), F32)],
        name=name, compiler_params=_cp("arbitrary"))(zs, *s5_par, *s5_const, qkv, zab, zg, *dn_par)


def _mix_bwd(zs, carries, h_all, dy_s, s5_par, s5_const, qkv, zab, zg, states, tinvs, dy_d, dn_par, bl, s, name):
    n_c = s // DN_C
    nd = len(S5_SHIFTS)
    tok, per_chunk, whole = _mix_specs(bl, n_c, True)

    def both(examples, s5_tabs, s5_consts, dn_tabs):
        s5_gens = [_s5_chunk_gen(u, gate, cr, ci, *s5_tabs, *s5_consts, known_h=(hr, hi))
                   for u, gate, cr, ci, hr, hi, _, _, _, _, _ in examples]
        dn_gens = [_dn_chunk_gen(q, ab, zgate, sts, *dn_tabs, known_tinvs=known)
                   for _, _, _, _, _, _, q, ab, zgate, sts, known in examples]
        s5_outs, dn_outs = _interleave(s5_gens, dn_gens, head_start=MIX_HEAD_START)
        return [(y_s, hr[S5_L - 1:S5_L], hi[S5_L - 1:S5_L], y_d, new_sts)
                for (y_s, hr, hi), (y_d, new_sts, _) in zip(s5_outs, dn_outs)]

    def body(z_ref, car_ref, h_ref, dys_ref, b_ref, c_ref, lam_ref, dv_ref, wg_ref, bg_ref, ld_ref, lp_ref, lpr_ref,
             q_ref, ab_ref, zg_ref, st_ref, ti_ref, dyd_ref, al_ref, dt_ref, ng_ref,
             dz_ref, db_ref, dc_ref, dlam_ref, ddv_ref, dwg_ref, dbg_ref,
             dq_ref, dab_ref, dzg_ref, dal_ref, ddt_ref, dng_ref, dcs, dsc):
        accs = (db_ref, dc_ref, dlam_ref, ddv_ref, dwg_ref, dbg_ref, dal_ref, ddt_ref, dng_ref)

        @pl.when(pl.program_id(0) == 0)
        def _():
            for r in accs + (dcs, dsc):
                r[...] = jnp.zeros_like(r)

        examples = []
        for e in range(bl):
            z = z_ref[e]
            c = car_ref[e, 0]
            examples.append((z[:, :D_SSM], z[:, D_SSM:], c[:, :NRE], c[:, NRE:], h_ref[e, :, :NRE], h_ref[e, :, NRE:],
                             q_ref[e], ab_ref[e], zg_ref[e], [st_ref[e, 0, h] for h in range(H)],
                             [ti_ref[e, 0, h] for h in range(H)]))
        _, vjp = jax.vjp(both, examples,
                         (b_ref[...], c_ref[...], lam_ref[...], dv_ref[...], wg_ref[...], bg_ref[...]),
                         ([ld_ref[k] for k in range(nd)], lp_ref[...], lpr_ref[...]),
                         (al_ref[...], dt_ref[...], ng_ref[...]))
        cts = []
        for e in range(bl):
            dc = dcs[e]
            cts.append((dys_ref[e], dc[:, :NRE], dc[:, NRE:], dyd_ref[e], [dsc[e, h] for h in range(H)]))
        d_examples, d_s5, _, d_dn = vjp(cts)
        for e in range(bl):
            du, dgate, dcr, dci, _, _, dq, dab, dzg, dsts, _ = d_examples[e]
            dz_ref[e] = jnp.concatenate([du, dgate], axis=1).astype(BF16)
            dcs[e, :, :NRE] = dcr
            dcs[e, :, NRE:] = dci
            dq_ref[e] = dq
            dab_ref[e] = dab.astype(BF16)
            dzg_ref[e] = dzg.astype(BF16)
            for h in range(H):
                dsc[e, h] = dsts[h]
        for r, ct in zip(accs, (*d_s5, *d_dn)):
            r[...] += ct

    head_mats = per_chunk((H, DH, DH))
    outs = pl.pallas_call(
        body, grid=(n_c,),
        in_specs=[tok(2 * D_SSM), per_chunk((1, 2 * NRE)), tok(2 * NRE), tok(D_SSM)]
        + [whole(sh) for sh in S5_PAR_SHAPES + S5_CONST_SHAPES]
        + [tok(3 * D_DN), tok(LANES), tok(D_DN), head_mats, head_mats, tok(D_DN)] + [whole(sh) for sh in DN_PAR_SHAPES],
        out_specs=[tok(2 * D_SSM)] + [whole(sh) for sh in S5_PAR_SHAPES]
        + [tok(3 * D_DN), tok(LANES), tok(D_DN)] + [whole(sh) for sh in DN_PAR_SHAPES],
        out_shape=[jax.ShapeDtypeStruct((bl, s, 2 * D_SSM), BF16)] + [jax.ShapeDtypeStruct(sh, F32) for sh in S5_PAR_SHAPES]
        + [jax.ShapeDtypeStruct((bl, s, 3 * D_DN), F32), jax.ShapeDtypeStruct((bl, s, LANES), BF16),
           jax.ShapeDtypeStruct((bl, s, D_DN), BF16)]
        + [jax.ShapeDtypeStruct(sh, F32) for sh in DN_PAR_SHAPES],
        scratch_shapes=[pltpu.VMEM((bl, 1, 2 * NRE), F32), pltpu.VMEM((bl, H, DH, DH), F32)],
        name=name, compiler_params=_cp("arbitrary"))(
            zs, carries, h_all, dy_s, *s5_par, *s5_const, qkv, zab, zg, states, tinvs, dy_d, *dn_par)
    return outs[:7], outs[7:]


def _sg_fn(n_chunk):
    def f(z, lng, lnb, w, bsp_t):
        u = jax.nn.gelu(z[:, :D_SG])
        v = jax.nn.gelu(z[:, D_SG:2 * D_SG])
        gate = z[:, 2 * D_SG:]
        xc = v - jnp.mean(v, axis=-1, keepdims=True)
        vn = xc * lax.rsqrt(jnp.mean(xc * xc, axis=-1, keepdims=True) + EPS) * lng + lnb
        r = lax.broadcasted_iota(jnp.int32, (SG_C, SG_C), 0)
        c = lax.broadcasted_iota(jnp.int32, (SG_C, SG_C), 1)
        causal = r >= c
        first_half = c < SG_C // 2
        rr = lax.broadcasted_iota(jnp.int32, (LANES, D_SG), 0)
        cc = lax.broadcasted_iota(jnp.int32, (LANES, D_SG), 1)
        expand = jnp.where((cc >= rr * 64) & (cc < rr * 64 + 64) & (rr < 4), 1.0, 0.0)
        bias = _sel_r(bsp_t, expand)
        wm = [jnp.where(causal, w[h], 0.0) for h in range(4)]
        rows = []
        for ci in range(n_chunk):
            vc = vn[ci * SG_C:(ci + 1) * SG_C]
            pairs = []
            for pr in range(2):
                vp = vc[:, pr * LANES:(pr + 1) * LANES]
                pairs.append(jnp.where(first_half, _mm(wm[2 * pr], vp), _mm(wm[2 * pr + 1], vp)))
            rows.append(jnp.concatenate(pairs, axis=1) + bias)
        sp = jnp.concatenate(rows, axis=0) if n_chunk > 1 else rows[0]
        return u * sp * _silu(gate)

    return f


def _sg_specs():
    full = lambda i: (0, 0)
    full3 = lambda i: (0, 0, 0)
    par = [pl.BlockSpec((1, D_SG), full), pl.BlockSpec((1, D_SG), full), pl.BlockSpec((4, SG_C, SG_C), full3),
           pl.BlockSpec((SG_C, LANES), full)]
    par_shapes = [(1, D_SG), (1, D_SG), (4, SG_C, SG_C), (SG_C, LANES)]
    return par, par_shapes


def _sg_fwd(zsg, params, tb, name):
    t = zsg.shape[0]
    f = _sg_fn(tb // SG_C)
    par, _ = _sg_specs()

    def body(z_ref, g_ref, b_ref, w_ref, bs_ref, y_ref):
        y_ref[...] = f(z_ref[...], g_ref[...], b_ref[...], w_ref[...], bs_ref[...])

    row = lambda i: (i, 0)
    return pl.pallas_call(
        body, grid=(t // tb,), in_specs=[pl.BlockSpec((tb, 3 * D_SG), row)] + par,
        out_specs=pl.BlockSpec((tb, D_SG), row), out_shape=jax.ShapeDtypeStruct((t, D_SG), F32),
        name=name, compiler_params=_cp("parallel"))(zsg, *params)


def _sg_bwd(zsg, dy, params, tb, name):
    t = zsg.shape[0]
    f = _sg_fn(tb // SG_C)
    par, par_shapes = _sg_specs()

    def body(z_ref, dy_ref, g_ref, b_ref, w_ref, bs_ref, dz_ref, dg_ref, db_ref, dw_ref, dbs_ref):
        accs = (dg_ref, db_ref, dw_ref, dbs_ref)

        @pl.when(pl.program_id(0) == 0)
        def _():
            for r in accs:
                r[...] = jnp.zeros_like(r)

        _, vjp = jax.vjp(f, z_ref[...], g_ref[...], b_ref[...], w_ref[...], bs_ref[...])
        cts = vjp(dy_ref[...])
        dz_ref[...] = cts[0].astype(BF16)
        for r, ct in zip(accs, cts[1:]):
            r[...] += ct

    row = lambda i: (i, 0)
    return pl.pallas_call(
        body, grid=(t // tb,), in_specs=[pl.BlockSpec((tb, 3 * D_SG), row), pl.BlockSpec((tb, D_SG), row)] + par,
        out_specs=[pl.BlockSpec((tb, 3 * D_SG), row)] + par,
        out_shape=[jax.ShapeDtypeStruct((t, 3 * D_SG), BF16)] + [jax.ShapeDtypeStruct(sh, F32) for sh in par_shapes],
        name=name, compiler_params=_cp("arbitrary"))(zsg, dy, *params)


def _out_fwd(x, ys, p, w_out, pg, w_gate, w_ple, tb, name):
    t = x.shape[0]

    def body(x_ref, y0, y1, y2, p_ref, wo_ref, pg_ref, wg_ref, wp_ref, o_ref, x1_ref, gate_ref):
        y = jnp.concatenate([y0[...], y1[...], y2[...]], axis=1).astype(BF16)
        x1 = x_ref[...] + jnp.dot(y, wo_ref[...], preferred_element_type=F32)
        hn = _rms(x1, pg_ref[...]).astype(BF16)
        gate = jax.nn.sigmoid(jnp.dot(hn, wg_ref[...], preferred_element_type=F32))
        pp = jnp.dot(p_ref[...].astype(BF16), wp_ref[...], preferred_element_type=F32)
        o_ref[...] = x1 + gate * pp
        x1_ref[...] = x1
        gate_ref[...] = gate

    row = lambda i: (i, 0)
    full = lambda i: (0, 0)
    return pl.pallas_call(
        body, grid=(t // tb,),
        in_specs=[pl.BlockSpec((tb, D), row), pl.BlockSpec((tb, D_SSM), row), pl.BlockSpec((tb, D_DN), row),
                  pl.BlockSpec((tb, D_SG), row), pl.BlockSpec((tb, D_PLE), row), pl.BlockSpec((D, D), full),
                  pl.BlockSpec((1, D), full), pl.BlockSpec((D, D), full), pl.BlockSpec((D_PLE, D), full)],
        out_specs=[pl.BlockSpec((tb, D), row)] * 3, out_shape=[jax.ShapeDtypeStruct((t, D), F32)] * 3,
        name=name, compiler_params=_cp("parallel"))(x, *ys, p, w_out, pg, w_gate, w_ple)


def _out_bwd(x1, gate, ys, p, dx2, w_out, pg, w_gate, w_ple, tb, name):
    t = x1.shape[0]

    def body(x1_ref, gate_ref, y0, y1, y2, p_ref, d_ref, wo_ref, pg_ref, wg_ref, wp_ref,
             dx_ref, dy0, dy1, dy2, dwo_ref, dpg_ref, dwg_ref, dwp_ref):
        accs = (dwo_ref, dpg_ref, dwg_ref, dwp_ref)

        @pl.when(pl.program_id(0) == 0)
        def _():
            for r in accs:
                r[...] = jnp.zeros_like(r)

        y = jnp.concatenate([y0[...], y1[...], y2[...]], axis=1).astype(BF16)
        hn, rms_vjp = jax.vjp(_rms, x1_ref[...], pg_ref[...])
        hb = hn.astype(BF16)
        gate = gate_ref[...]
        pb = p_ref[...].astype(BF16)
        pp = jnp.dot(pb, wp_ref[...], preferred_element_type=F32)
        d2 = d_ref[...]
        dpp = (d2 * gate).astype(BF16)
        dlog = (d2 * pp * gate * (1.0 - gate)).astype(BF16)
        dwp_ref[...] += _dg(pb, dpp, 0, 0)
        dwg_ref[...] += _dg(hb, dlog, 0, 0)
        dx1_n, dpg = rms_vjp(_dg(dlog, wg_ref[...], 1, 1))
        dpg_ref[...] += dpg
        dx1 = d2 + dx1_n
        dx_ref[...] = dx1
        db = dx1.astype(BF16)
        dwo_ref[...] += _dg(y, db, 0, 0)
        dy = _dg(db, wo_ref[...], 1, 1)
        dy0[...] = dy[:, :D_SSM]
        dy1[...] = dy[:, D_SSM:D_SSM + D_DN]
        dy2[...] = dy[:, D_SSM + D_DN:]

    row = lambda i: (i, 0)
    full = lambda i: (0, 0)
    acts = [pl.BlockSpec((tb, D), row), pl.BlockSpec((tb, D_SSM), row), pl.BlockSpec((tb, D_DN), row), pl.BlockSpec((tb, D_SG), row)]
    wts = [pl.BlockSpec((D, D), full), pl.BlockSpec((1, D), full), pl.BlockSpec((D, D), full), pl.BlockSpec((D_PLE, D), full)]
    return pl.pallas_call(
        body, grid=(t // tb,),
        in_specs=[pl.BlockSpec((tb, D), row)] + acts + [pl.BlockSpec((tb, D_PLE), row), pl.BlockSpec((tb, D), row)] + wts,
        out_specs=acts + wts,
        out_shape=[jax.ShapeDtypeStruct((t, n), F32) for n in (D, D_SSM, D_DN, D_SG)]
        + [jax.ShapeDtypeStruct(sh, F32) for sh in ((D, D), (1, D), (D, D), (D_PLE, D))],
        name=name, compiler_params=_cp("arbitrary"))(x1, gate, *ys, p, dx2, w_out, pg, w_gate, w_ple)


def _loss_head(x, fg, target, tb, name):
    t = x.shape[0]

    def body(x_ref, g_ref, t_ref, dx_ref, dg_ref, loss_ref):
        @pl.when(pl.program_id(0) == 0)
        def _():
            dg_ref[...] = jnp.zeros_like(dg_ref)
            loss_ref[...] = jnp.zeros_like(loss_ref)

        y, vjp = jax.vjp(_rms, x_ref[...], g_ref[...])
        err = y - t_ref[...]
        loss_ref[...] += jnp.zeros_like(loss_ref) + 0.5 * jnp.sum(err * err) / D
        dx, dg = vjp(err / D)
        dx_ref[...] = dx
        dg_ref[...] += dg

    row = lambda i: (i, 0)
    full = lambda i: (0, 0)
    return pl.pallas_call(
        body, grid=(t // tb,),
        in_specs=[pl.BlockSpec((tb, D), row), pl.BlockSpec((1, D), full), pl.BlockSpec((tb, D), row)],
        out_specs=[pl.BlockSpec((tb, D), row), pl.BlockSpec((1, D), full), pl.BlockSpec((1, LANES), full)],
        out_shape=[jax.ShapeDtypeStruct((t, D), F32), jax.ShapeDtypeStruct((1, D), F32), jax.ShapeDtypeStruct((1, LANES), F32)],
        name=name, compiler_params=_cp("arbitrary"))(x, fg, target)


def _hbm_specs(n):
    return [pl.BlockSpec(memory_space=pl.ANY)] * n


def _all_gather(blocks, name):
    n = len(blocks)

    def body(*refs):
        ins, outs = refs[:n], refs[n:2 * n]
        send_sems, recv_sems, local_sems = refs[2 * n:]
        x, y, c = lax.axis_index("x"), lax.axis_index("y"), lax.axis_index("c")
        me, sibling = (x, y, c), (x, y, 1 - c)
        chips = [(1 - x, y), (x, 1 - y), (1 - x, 1 - y)]

        def slot(a, px, py, pc):
            return outs[a].at[4 * px + 2 * py + pc]

        def copy(a, k, blk, to, src=None):
            return pltpu.make_async_remote_copy(
                src_ref=slot(a, *blk) if src is None else src, dst_ref=slot(a, *blk),
                send_sem=send_sems.at[7 * a + k], recv_sem=recv_sems.at[7 * a + k],
                device_id=to, device_id_type=pl.DeviceIdType.MESH)

        mines = [pltpu.make_async_copy(ins[a], slot(a, *me), local_sems.at[a]) for a in range(n)]
        for cp in mines:
            cp.start()
        first = []
        for a in range(n):
            first.append(copy(a, 0, me, sibling, src=ins[a]))
            first += [copy(a, 1 + j, me, (*chip, c), src=ins[a]) for j, chip in enumerate(chips)]
        for cp in first:
            cp.start()
        passed = []
        for j, chip in enumerate(chips):
            for a in range(n):
                copy(a, 1 + j, (*chip, c), me).wait_recv()
                onward = copy(a, 4 + j, (*chip, c), sibling)
                onward.start()
                passed.append(onward)
        for a in range(n):
            copy(a, 0, sibling, me).wait_recv()
        for j, chip in enumerate(chips):
            for a in range(n):
                copy(a, 4 + j, (*chip, 1 - c), me).wait_recv()
        for cp in first + passed:
            cp.wait_send()
        for cp in mines:
            cp.wait()

    return pl.pallas_call(
        body, out_shape=[jax.ShapeDtypeStruct((N_DEV, *b.shape), b.dtype) for b in blocks],
        in_specs=_hbm_specs(n), out_specs=_hbm_specs(n),
        scratch_shapes=[pltpu.SemaphoreType.DMA((7 * n,)), pltpu.SemaphoreType.DMA((7 * n,)), pltpu.SemaphoreType.DMA((n,))],
        name=name)(*blocks)


def _pair_exchange(gs, name):
    n = len(gs)

    def body(*refs):
        ins, recvs = refs[:n], refs[n:2 * n]
        send_sems, recv_sems = refs[2 * n:]
        x, y, c = lax.axis_index("x"), lax.axis_index("y"), lax.axis_index("c")
        remote = [pltpu.make_async_remote_copy(
            src_ref=ins[a], dst_ref=recvs[a], send_sem=send_sems.at[a], recv_sem=recv_sems.at[a],
            device_id=(x, y, 1 - c), device_id_type=pl.DeviceIdType.MESH) for a in range(n)]
        for cp in remote:
            cp.start()
        for cp in remote:
            cp.wait_send()
            cp.wait_recv()

    return pl.pallas_call(
        body, out_shape=[jax.ShapeDtypeStruct(g.shape, g.dtype) for g in gs], in_specs=_hbm_specs(n), out_specs=_hbm_specs(n),
        scratch_shapes=[pltpu.SemaphoreType.DMA((n,)), pltpu.SemaphoreType.DMA((n,))],
        name=name)(*gs)


def _chip_exchange(ps, name):
    n = len(ps)

    def body(*refs):
        ins, outs = refs[:n], refs[n:2 * n]
        send_sems, recv_sems, local_sems = refs[2 * n:]
        x, y, c = lax.axis_index("x"), lax.axis_index("y"), lax.axis_index("c")
        my_chip = 2 * x + y
        local = [pltpu.make_async_copy(ins[a].at[my_chip], outs[a].at[my_chip], local_sems.at[a]) for a in range(n)]
        remote = []
        for j in range(1, 4):
            px = 1 - x if j & 2 else x
            py = 1 - y if j & 1 else y
            for a in range(n):
                remote.append(pltpu.make_async_remote_copy(
                    src_ref=ins[a].at[2 * px + py], dst_ref=outs[a].at[my_chip],
                    send_sem=send_sems.at[3 * a + j - 1], recv_sem=recv_sems.at[3 * a + j - 1],
                    device_id=(px, py, c), device_id_type=pl.DeviceIdType.MESH))
        for cp in local + remote:
            cp.start()
        for cp in remote:
            cp.wait_send()
            cp.wait_recv()
        for cp in local:
            cp.wait()

    return pl.pallas_call(
        body, out_shape=[jax.ShapeDtypeStruct(q.shape, q.dtype) for q in ps], in_specs=_hbm_specs(n), out_specs=_hbm_specs(n),
        scratch_shapes=[pltpu.SemaphoreType.DMA((3 * n,)), pltpu.SemaphoreType.DMA((3 * n,)), pltpu.SemaphoreType.DMA((n,))],
        name=name)(*ps)


def _row_block(rows, bytes_per_row):
    best = None
    for rb in range(16, rows + 1, 16):
        if rows % rb == 0 and rb * bytes_per_row <= ELEMENTWISE_STEP_BYTES:
            best = rb
    return rows if best is None else best


def _add_pair(own, recv, name):
    shape = own.shape
    last = shape[-1]
    rows = own.size // last
    rb = _row_block(rows, 3 * 4 * (-(-last // LANES) * LANES))

    def body(a_ref, b_ref, o_ref):
        o_ref[...] = (a_ref[...].astype(F32) + b_ref[...].astype(F32)).astype(o_ref.dtype)

    row = lambda i: (i, 0)
    out = pl.pallas_call(
        body, grid=(rows // rb,), in_specs=[pl.BlockSpec((rb, last), row)] * 2, out_specs=pl.BlockSpec((rb, last), row),
        out_shape=jax.ShapeDtypeStruct((rows, last), own.dtype), name=name,
        compiler_params=_cp("parallel"))(own.reshape(rows, last), recv.reshape(rows, last))
    return out.reshape(shape)


def _sum_adamw(gk, w, m, v, name):
    shape = w.shape
    n_part = gk.shape[0]
    last = shape[-1]
    rows = w.size // last
    rb = _row_block(rows, (n_part + 7) * 4 * (-(-last // LANES) * LANES))

    def body(g_ref, w_ref, m_ref, v_ref, go_ref, d_ref, mo_ref, vo_ref):
        g = g_ref[0].astype(F32)
        for k in range(1, n_part):
            g = g + g_ref[k].astype(F32)
        mn = ADAM_B1 * m_ref[...] + (1.0 - ADAM_B1) * g
        vn = ADAM_B2 * v_ref[...] + (1.0 - ADAM_B2) * jnp.square(g)
        m_hat = mn / (1.0 - ADAM_B1 ** ADAM_STEP)
        v_hat = vn / (1.0 - ADAM_B2 ** ADAM_STEP)
        go_ref[...] = g
        d_ref[...] = -ADAM_LR * (m_hat / (jnp.sqrt(v_hat) + ADAM_EPS) + ADAM_WD * w_ref[...])
        mo_ref[...] = mn
        vo_ref[...] = vn

    row = lambda i: (i, 0)
    outs = pl.pallas_call(
        body, grid=(rows // rb,),
        in_specs=[pl.BlockSpec((n_part, rb, last), lambda i: (0, i, 0))] + [pl.BlockSpec((rb, last), row)] * 3,
        out_specs=[pl.BlockSpec((rb, last), row)] * 4,
        out_shape=[jax.ShapeDtypeStruct((rows, last), F32)] * 4,
        name=name, compiler_params=_cp("parallel"))(gk.reshape(n_part, rows, last), *[a.reshape(rows, last) for a in (w, m, v)])
    return [o.reshape(shape) for o in outs]


def _seg_rows(shape):
    n = 1
    for d in shape:
        n *= d
    return -(-n // (8 * LANES)) * 8


def _pack(arrs):
    segs = []
    for a in arrs:
        r = _seg_rows(a.shape)
        segs.append(jnp.pad(a.reshape(-1).astype(F32), (0, r * LANES - a.size)).reshape(r, LANES))
    rows = sum(s.shape[0] for s in segs)
    total = -(-rows // PACK_ROWS) * PACK_ROWS
    if total > rows:
        segs.append(jnp.zeros((total - rows, LANES), F32))
    return jnp.concatenate(segs, axis=0)


def _unpack(pack, shapes):
    out, off = [], 0
    for sh in shapes:
        r = _seg_rows(sh)
        n = 1
        for d in sh:
            n *= d
        out.append(pack[off:off + r].reshape(-1)[:n].reshape(sh))
        off += r
    return out


def _to_dest_blocks(full, axis, dtype):
    sh = list(full.shape)
    sh[axis:axis + 1] = [N_DEV // 2, 2, sh[axis] // N_DEV]
    return jnp.moveaxis(full.reshape(sh), (axis, axis + 1), (1, 0)).astype(dtype)


def _from_gathered(g, axis):
    m = jnp.moveaxis(g, 0, axis)
    sh = list(m.shape)
    sh[axis:axis + 2] = [sh[axis] * sh[axis + 1]]
    return m.reshape(sh)


def _reorder_w_in(w):
    return jnp.concatenate([w[:, :2048], w[:, 2056:3336], w[:, 2048:2056], jnp.zeros((D, ZW - 3336), w.dtype)], axis=1)


def _restore_dw_in(dw):
    return jnp.concatenate([dw[:, :2048], dw[:, 3328:3336], dw[:, 2048:3328]], axis=1)


def _local_step(x, p, wts, target):
    bl, s, _ = x.shape
    t = bl * s
    depth = p.shape[0]
    tb, sg_tb, prep_tb = TB, SG_TB, PREP_TB

    def by_example(a):
        return a.reshape(bl, s, a.shape[-1])

    def flat(a):
        return a.reshape(t, a.shape[-1])

    xs = [x.reshape(t, D)]
    saved = []
    for i in range(depth):
        li = f"l{i}"
        ng = wts['norm_g'][i].reshape(1, D)
        w_in = _reorder_w_in(wts['w_in'][i]).astype(BF16)
        s5_par_in = (wts['ssm_a_re'][i], wts['ssm_a_im'][i], wts['ssm_b_re'][i], wts['ssm_b_im'][i],
                     wts['ssm_c_re'][i], wts['ssm_c_im'][i], wts['ssm_d'][i], wts['ssm_log_step'][i])
        tabs, tab_vjp = jax.vjp(_s5_tables, *s5_par_in)
        s5_par = (*tabs, wts['ssm_w_glu'][i], wts['ssm_b_glu'][i].reshape(1, D_SSM))
        s5_const = _s5_powers(wts['ssm_a_re'][i], wts['ssm_a_im'][i], wts['ssm_log_step'][i])
        conv8 = jnp.pad(wts['dn_conv_w'][i], ((0, 4), (0, 0)))
        dn_par = (jnp.repeat(wts['dn_a_log'][i], DH).reshape(1, D_DN), jnp.repeat(wts['dn_dt_bias'][i], DH).reshape(1, D_DN),
                  wts['dn_norm_g'][i].reshape(1, DH))
        sg_par = (wts['sg_ln_g'][i].reshape(1, D_SG), wts['sg_ln_b'][i].reshape(1, D_SG), wts['sg_w'][i],
                  jnp.pad(jnp.transpose(wts['sg_b'][i]), ((0, 0), (0, LANES - 4))))
        out_par = (wts['w_out'][i].astype(BF16), wts['ple_norm_g'][i].reshape(1, D), wts['w_ple_gate'][i].astype(BF16),
                   wts['w_ple'][i].astype(BF16))
        pi = p[i].reshape(t, D_PLE)

        z_ssm, z_qkv, z_gdn, z_sg, z_ab = _in_proj_fwd(xs[i], ng, w_in, tb, f"in_proj_fwd_{li}")
        qkvn = _dn_prep_fwd(z_qkv, conv8, bl, s, prep_tb, f"dn_prep_fwd_{li}")
        y_ssm, carries, h_all, y_dn, states, tinvs = _mix_fwd(by_example(z_ssm), s5_par, s5_const, by_example(qkvn),
                                                              by_example(z_ab), by_example(z_gdn), dn_par, bl, s, f"mix_fwd_{li}")
        y_sg = _sg_fwd(z_sg, sg_par, sg_tb, f"sg_fwd_{li}")
        ys = (flat(y_ssm), flat(y_dn), y_sg)
        x_next, x1, gate = _out_fwd(xs[i], ys, pi, *out_par, tb, f"out_fwd_{li}")
        xs.append(x_next)
        saved.append(dict(ng=ng, w_in=w_in, tab_vjp=tab_vjp, s5_par=s5_par, s5_const=s5_const, conv8=conv8, dn_par=dn_par,
                          sg_par=sg_par, out_par=out_par, pi=pi, z=(z_ssm, z_qkv, z_gdn, z_sg, z_ab), carries=carries,
                          h_all=h_all, qkvn=qkvn, x1=x1, gate=gate,
                          states=states, tinvs=tinvs, ys=ys))

    dx, dfg, loss_vec = _loss_head(xs[depth], wts['final_norm_g'].reshape(1, D), target.reshape(t, D), tb, "loss_head")
    grads = {n: [None] * depth for n in WEIGHTS if n != 'final_norm_g'}
    grads['final_norm_g'] = dfg.reshape(D)
    for i in reversed(range(depth)):
        li = f"l{i}"
        sv = saved[i]
        z_ssm, z_qkv, z_gdn, z_sg, z_ab = sv['z']
        dx_res, dy_ssm, dy_dn, dy_sg, dwo, dpg, dwg, dwp = _out_bwd(sv['x1'], sv['gate'], sv['ys'], sv['pi'], dx, *sv['out_par'], tb,
                                                                    f"out_bwd_{li}")
        dz_sg, dlng, dlnb, dsgw, dbsp = _sg_bwd(z_sg, dy_sg, sv['sg_par'], sg_tb, f"sg_bwd_{li}")
        (dz_ssm, dbb, dcb, dlam, ddv, dwglu, dbglu), (dqkvn, dz_ab, dz_gdn, dal, ddt, dng) = _mix_bwd(
            by_example(z_ssm), sv['carries'], sv['h_all'], by_example(dy_ssm), sv['s5_par'], sv['s5_const'],
            by_example(sv['qkvn']), by_example(z_ab), by_example(z_gdn), sv['states'], sv['tinvs'], by_example(dy_dn),
            sv['dn_par'], bl, s, f"mix_bwd_{li}")
        dz_qkv, dconv = _dn_prep_bwd(z_qkv, flat(dqkvn), sv['conv8'], bl, s, prep_tb, f"dn_prep_bwd_{li}")
        dzs = (flat(dz_ssm), dz_qkv, flat(dz_gdn), dz_sg, flat(dz_ab))
        dx, dnorm = _in_proj_bwd_dx(xs[i], sv['ng'], sv['w_in'], dzs, dx_res, tb, f"in_proj_bwd_dx_{li}")
        dws = _in_proj_bwd_dw(xs[i], sv['ng'], dzs, min(TB_DW, t), f"in_proj_bwd_dw_{li}")
        ds5 = sv['tab_vjp']((dbb, dcb, dlam, ddv))
        for n, gval in zip(('ssm_a_re', 'ssm_a_im', 'ssm_b_re', 'ssm_b_im', 'ssm_c_re', 'ssm_c_im', 'ssm_d', 'ssm_log_step'), ds5):
            grads[n][i] = gval
        grads['norm_g'][i] = dnorm.reshape(D)
        grads['w_in'][i] = _restore_dw_in(jnp.concatenate(dws, axis=1))
        grads['ssm_w_glu'][i] = dwglu
        grads['ssm_b_glu'][i] = dbglu.reshape(D_SSM)
        grads['dn_conv_w'][i] = dconv[:4]
        grads['dn_a_log'][i] = dal.reshape(H, DH).sum(axis=1)
        grads['dn_dt_bias'][i] = ddt.reshape(H, DH).sum(axis=1)
        grads['dn_norm_g'][i] = dng.reshape(DH)
        grads['sg_ln_g'][i] = dlng.reshape(D_SG)
        grads['sg_ln_b'][i] = dlnb.reshape(D_SG)
        grads['sg_w'][i] = dsgw
        grads['sg_b'][i] = jnp.transpose(dbsp[:, :4])
        grads['w_out'][i] = dwo
        grads['ple_norm_g'][i] = dpg.reshape(D)
        grads['w_ple_gate'][i] = dwg
        grads['w_ple'][i] = dwp
    grads = {n: (g if n == 'final_norm_g' else jnp.stack(g)) for n, g in grads.items()}
    return loss_vec[0, 0], dx.reshape(bl, s, D), grads


def kernel(x, p, norm_g, w_in, ssm_a_re, ssm_a_im, ssm_b_re, ssm_b_im, ssm_c_re, ssm_c_im, ssm_d, ssm_log_step, ssm_w_glu, ssm_b_glu, dn_conv_w, dn_a_log, dn_dt_bias, dn_norm_g, sg_ln_g, sg_ln_b, sg_w, sg_b, w_out, ple_norm_g, w_ple_gate, w_ple, final_norm_g, loss_target, m_norm_g, m_w_in, m_ssm_a_re, m_ssm_a_im, m_ssm_b_re, m_ssm_b_im, m_ssm_c_re, m_ssm_c_im, m_ssm_d, m_ssm_log_step, m_ssm_w_glu, m_ssm_b_glu, m_dn_conv_w, m_dn_a_log, m_dn_dt_bias, m_dn_norm_g, m_sg_ln_g, m_sg_ln_b, m_sg_w, m_sg_b, m_w_out, m_ple_norm_g, m_w_ple_gate, m_w_ple, m_final_norm_g, v_norm_g, v_w_in, v_ssm_a_re, v_ssm_a_im, v_ssm_b_re, v_ssm_b_im, v_ssm_c_re, v_ssm_c_im, v_ssm_d, v_ssm_log_step, v_ssm_w_glu, v_ssm_b_glu, v_dn_conv_w, v_dn_a_log, v_dn_dt_bias, v_dn_norm_g, v_sg_ln_g, v_sg_ln_b, v_sg_w, v_sg_b, v_w_out, v_ple_norm_g, v_w_ple_gate, v_w_ple, v_final_norm_g):
    w_loc = dict(zip(WEIGHTS, (norm_g, w_in, ssm_a_re, ssm_a_im, ssm_b_re, ssm_b_im, ssm_c_re, ssm_c_im, ssm_d, ssm_log_step,
                               ssm_w_glu, ssm_b_glu, dn_conv_w, dn_a_log, dn_dt_bias, dn_norm_g, sg_ln_g, sg_ln_b, sg_w, sg_b,
                               w_out, ple_norm_g, w_ple_gate, w_ple, final_norm_g)))
    m_loc = dict(zip(WEIGHTS, (m_norm_g, m_w_in, m_ssm_a_re, m_ssm_a_im, m_ssm_b_re, m_ssm_b_im, m_ssm_c_re, m_ssm_c_im, m_ssm_d,
                               m_ssm_log_step, m_ssm_w_glu, m_ssm_b_glu, m_dn_conv_w, m_dn_a_log, m_dn_dt_bias, m_dn_norm_g,
                               m_sg_ln_g, m_sg_ln_b, m_sg_w, m_sg_b, m_w_out, m_ple_norm_g, m_w_ple_gate, m_w_ple, m_final_norm_g)))
    v_loc = dict(zip(WEIGHTS, (v_norm_g, v_w_in, v_ssm_a_re, v_ssm_a_im, v_ssm_b_re, v_ssm_b_im, v_ssm_c_re, v_ssm_c_im, v_ssm_d,
                               v_ssm_log_step, v_ssm_w_glu, v_ssm_b_glu, v_dn_conv_w, v_dn_a_log, v_dn_dt_bias, v_dn_norm_g,
                               v_sg_ln_g, v_sg_ln_b, v_sg_w, v_sg_b, v_w_out, v_ple_norm_g, v_w_ple_gate, v_w_ple, v_final_norm_g)))

    gathered = _all_gather([w_loc[n].astype(WIRE[n]) for n in SHARDED_ORDER], "gather_weights")
    full = dict(w_loc)
    for n, g in zip(SHARDED_ORDER, gathered):
        full[n] = _from_gathered(g, SHARDED[n])
    full['ssm_w_glu'] = full['ssm_w_glu'].astype(F32)

    loss_part, grad_x, grads = _local_step(x, p, full, loss_target)

    dest = [_to_dest_blocks(grads[n], SHARDED[n], WIRE[n]) for n in SHARDED_ORDER]
    c = lax.axis_index("c")
    own = [lax.dynamic_index_in_dim(d, c, 0, keepdims=False) for d in dest]
    for_sibling = [lax.dynamic_index_in_dim(d, 1 - c, 0, keepdims=False) for d in dest]
    from_sibling = _pair_exchange(for_sibling, "grads_pair_exchange")
    chip_sums = [_add_pair(a, b, f"grads_pair_sum_{n}") for n, a, b in zip(SHARDED_ORDER, own, from_sibling)]
    by_chip = _chip_exchange(chip_sums, "grads_chip_exchange")
    rep_pack = _pack([grads[n] for n in REPLICATED_ORDER] + [loss_part.reshape(1)])
    (rep_recv,) = _all_gather([rep_pack], "gather_small_grads")

    outs = {k: {} for k in 'gdmv'}
    for n, gk in zip(SHARDED_ORDER, by_chip):
        for k, o in zip('gdmv', _sum_adamw(gk, w_loc[n], m_loc[n], v_loc[n], f"adamw_{n}")):
            outs[k][n] = o
    one = jnp.zeros((1,), F32)
    rep_out = _sum_adamw(rep_recv, _pack([w_loc[n] for n in REPLICATED_ORDER] + [one]),
                         _pack([m_loc[n] for n in REPLICATED_ORDER] + [one]),
                         _pack([v_loc[n] for n in REPLICATED_ORDER] + [one]), "adamw_replicated")
    rep_shapes = [w_loc[n].shape for n in REPLICATED_ORDER] + [(1,)]
    for k, rep_p in zip('gdmv', rep_out):
        outs[k].update(zip(REPLICATED_ORDER + ['loss'], _unpack(rep_p, rep_shapes)))
    loss = outs['g']['loss'].reshape(())
    return (loss, grad_x, *[outs['g'][n] for n in WEIGHTS], *[outs['d'][n] for n in WEIGHTS],
            *[outs['m'][n] for n in WEIGHTS], *[outs['v'][n] for n in WEIGHTS])
```

```python
import functools

import jax
import jax.numpy as jnp
from jax import lax
from jax.experimental import pallas as pl
from jax.experimental.pallas import tpu as pltpu

F32 = jnp.float32
BF16 = jnp.bfloat16
EPS = 1e-6

D = 1024
D_PLE = 256
D_SSM = 256
D_DN = 512
D_SG = 256
G = 16
CG = 16
NS = 64
NRE = G * NS
H = 4
DH = 128
DN_C = 128
SG_C = 128
ZW = 3456
Z_PIECES = (512, 1536, 512, 768, 128)
N_DEV = 8
LANES = 128
PACK_ROWS = 256
VMEM_LIMIT = 56 * 1024 * 1024
ELEMENTWISE_STEP_BYTES = 4 * 1024 * 1024
TB = 256
TB_DW = 512
SG_TB = 512
PREP_TB = 256

ADAM_LR = 0.001
ADAM_B1 = 0.9
ADAM_B2 = 0.999
ADAM_EPS = 1e-08
ADAM_WD = 0.01
ADAM_STEP = 10

MIX_HEAD_START = 3
S5_L = 128
S5_GROUP = 8
S5_SHIFTS = (1, 2, 4)

WEIGHTS = ['norm_g', 'w_in', 'ssm_a_re', 'ssm_a_im', 'ssm_b_re', 'ssm_b_im', 'ssm_c_re', 'ssm_c_im', 'ssm_d',
           'ssm_log_step', 'ssm_w_glu', 'ssm_b_glu', 'dn_conv_w', 'dn_a_log', 'dn_dt_bias', 'dn_norm_g', 'sg_ln_g',
           'sg_ln_b', 'sg_w', 'sg_b', 'w_out', 'ple_norm_g', 'w_ple_gate', 'w_ple', 'final_norm_g']
SHARDED = {'w_in': 2, 'ssm_w_glu': 1, 'dn_conv_w': 2, 'w_out': 1, 'w_ple_gate': 1, 'w_ple': 2}
SHARDED_ORDER = ['w_in', 'ssm_w_glu', 'dn_conv_w', 'w_out', 'w_ple_gate', 'w_ple']
WIRE = {'w_in': BF16, 'ssm_w_glu': BF16, 'dn_conv_w': F32, 'w_out': BF16, 'w_ple_gate': BF16, 'w_ple': BF16}
REPLICATED_ORDER = [n for n in WEIGHTS if n not in SHARDED]


def _cp(*sem):
    return pltpu.CompilerParams(dimension_semantics=sem, vmem_limit_bytes=VMEM_LIMIT)


def _dg(a, b, ca, cb, precision=None):
    return lax.dot_general(a, b, (((ca,), (cb,)), ((), ())), precision=precision, preferred_element_type=F32)


@jax.custom_vjp
def _mm(a, b):
    return _dg(a.astype(BF16), b.astype(BF16), 1, 0)


def _mm_fwd(a, b):
    return _mm(a, b), (a, b)


def _mm_bwd(res, g):
    a, b = res
    gb = g.astype(BF16)
    return _dg(gb, b.astype(BF16), 1, 1), _dg(a.astype(BF16), gb, 0, 0)


_mm.defvjp(_mm_fwd, _mm_bwd)


@jax.custom_vjp
def _mm_nt(a, b):
    return _dg(a.astype(BF16), b.astype(BF16), 1, 1)


def _mm_nt_fwd(a, b):
    return _mm_nt(a, b), (a, b)


def _mm_nt_bwd(res, g):
    a, b = res
    gb = g.astype(BF16)
    return _dg(gb, b.astype(BF16), 1, 0), _dg(gb, a.astype(BF16), 0, 0)


_mm_nt.defvjp(_mm_nt_fwd, _mm_nt_bwd)


@jax.custom_vjp
def _mm_tn(a, b):
    return _dg(a.astype(BF16), b.astype(BF16), 0, 0)


def _mm_tn_fwd(a, b):
    return _mm_tn(a, b), (a, b)


def _mm_tn_bwd(res, g):
    a, b = res
    gb = g.astype(BF16)
    return _dg(b.astype(BF16), gb, 1, 1), _dg(a.astype(BF16), gb, 1, 0)


_mm_tn.defvjp(_mm_tn_fwd, _mm_tn_bwd)


def _split(x, n):
    pieces = []
    for _ in range(n - 1):
        hi = x.astype(BF16)
        pieces.append(hi)
        x = x - hi.astype(F32)
    pieces.append(x.astype(BF16))
    return pieces


def _dg3(a, b, ca, cb):
    a_hi, a_lo = _split(a, 2)
    b_hi, b_lo = _split(b, 2)
    return _dg(a_hi, b_hi, ca, cb) + (_dg(a_hi, b_lo, ca, cb) + _dg(a_lo, b_hi, ca, cb))


@jax.custom_vjp
def _dot3(a, b):
    return _dg3(a, b, 1, 0)


def _dot3_fwd(a, b):
    return _dot3(a, b), (a, b)


def _dot3_bwd(res, g):
    a, b = res
    return _dg3(g, b, 1, 1), _dg3(a, g, 0, 0)


_dot3.defvjp(_dot3_fwd, _dot3_bwd)


def _dg_sel(x, e, cx, ce, x_first):
    eb = e.astype(BF16)
    out = None
    for piece in reversed(_split(x, 3)):
        term = _dg(piece, eb, cx, ce) if x_first else _dg(eb, piece, ce, cx)
        out = term if out is None else out + term
    return out


@jax.custom_vjp
def _sel_r(x, e):
    return _dg_sel(x, e, 1, 0, True)


def _sel_r_fwd(x, e):
    return _sel_r(x, e), e


def _sel_r_bwd(e, g):
    return _dg_sel(g, e, 1, 1, True), jnp.zeros_like(e)


_sel_r.defvjp(_sel_r_fwd, _sel_r_bwd)


@jax.custom_vjp
def _sel_l(e, x):
    return _dg_sel(x, e, 0, 1, False)


def _sel_l_fwd(e, x):
    return _sel_l(e, x), e


def _sel_l_bwd(e, g):
    return jnp.zeros_like(e), _dg_sel(g, e, 0, 0, False)


_sel_l.defvjp(_sel_l_fwd, _sel_l_bwd)


def _rms(x, g):
    return x * lax.rsqrt(jnp.mean(x * x, axis=-1, keepdims=True) + EPS) * g


def _silu(x):
    return x * jax.nn.sigmoid(x)


def _in_proj_fwd(x, g, w, tb, name):
    t = x.shape[0]

    def body(x_ref, g_ref, w_ref, *z_refs):
        h = _rms(x_ref[...], g_ref[...])
        z = jnp.dot(h.astype(BF16), w_ref[...], preferred_element_type=F32)
        off = 0
        for z_ref, n in zip(z_refs, Z_PIECES):
            z_ref[...] = z[:, off:off + n]
            off += n

    row = lambda i: (i, 0)
    full = lambda i: (0, 0)
    return pl.pallas_call(
        body, grid=(t // tb,),
        in_specs=[pl.BlockSpec((tb, D), row), pl.BlockSpec((1, D), full), pl.BlockSpec((D, ZW), full)],
        out_specs=[pl.BlockSpec((tb, n), row) for n in Z_PIECES],
        out_shape=[jax.ShapeDtypeStruct((t, n), F32) for n in Z_PIECES],
        name=name, compiler_params=_cp("parallel"))(x, g, w)


def _in_proj_bwd_dx(x, g, w, dzs, dx_res, tb, name):
    t = x.shape[0]

    def body(x_ref, g_ref, w_ref, d0, d1, d2, d3, d4, dxr_ref, dx_ref, dg_ref):
        @pl.when(pl.program_id(0) == 0)
        def _():
            dg_ref[...] = jnp.zeros_like(dg_ref)

        dz = jnp.concatenate([d0[...], d1[...], d2[...], d3[...], d4[...]], axis=1).astype(BF16)
        dh = _dg(dz, w_ref[...], 1, 1)
        _, vjp = jax.vjp(_rms, x_ref[...], g_ref[...])
        dx, dg = vjp(dh)
        dx_ref[...] = dx + dxr_ref[...]
        dg_ref[...] += dg

    row = lambda i: (i, 0)
    full = lambda i: (0, 0)
    return pl.pallas_call(
        body, grid=(t // tb,),
        in_specs=[pl.BlockSpec((tb, D), row), pl.BlockSpec((1, D), full), pl.BlockSpec((D, ZW), full)]
        + [pl.BlockSpec((tb, n), row) for n in Z_PIECES] + [pl.BlockSpec((tb, D), row)],
        out_specs=[pl.BlockSpec((tb, D), row), pl.BlockSpec((1, D), full)],
        out_shape=[jax.ShapeDtypeStruct((t, D), F32), jax.ShapeDtypeStruct((1, D), F32)],
        name=name, compiler_params=_cp("arbitrary"))(x, g, w, *dzs, dx_res)


def _in_proj_bwd_dw(x, g, dzs, tb, name):
    t = x.shape[0]

    def body(x_ref, g_ref, d0, d1, d2, d3, d4, *dw_refs):
        @pl.when(pl.program_id(0) == 0)
        def _():
            for r in dw_refs:
                r[...] = jnp.zeros_like(r)

        h = _rms(x_ref[...], g_ref[...]).astype(BF16)
        for d_ref, dw_ref in zip((d0, d1, d2, d3, d4), dw_refs):
            dw_ref[...] += _dg(h, d_ref[...].astype(BF16), 0, 0)

    row = lambda i: (i, 0)
    full = lambda i: (0, 0)
    return pl.pallas_call(
        body, grid=(t // tb,),
        in_specs=[pl.BlockSpec((tb, D), row), pl.BlockSpec((1, D), full)] + [pl.BlockSpec((tb, n), row) for n in Z_PIECES],
        out_specs=[pl.BlockSpec((D, n), full) for n in Z_PIECES],
        out_shape=[jax.ShapeDtypeStruct((D, n), F32) for n in Z_PIECES],
        name=name, compiler_params=_cp("arbitrary"))(x, g, *dzs)


def _lam_pow(a_re, a_im, log_step, k):
    step = jnp.exp(log_step)[:, None]
    mag = jnp.exp(k * a_re * step)
    ang = k * a_im * step
    return mag * jnp.cos(ang), mag * jnp.sin(ang)


def _s5_powers(a_re, a_im, log_step):
    def table(ks):
        re, im = _lam_pow(a_re, a_im, log_step, jnp.asarray(ks, F32)[:, None, None])
        return jnp.concatenate([re.reshape(len(ks), NRE), im.reshape(len(ks), NRE)], axis=-1)

    ld = table(S5_SHIFTS).reshape(len(S5_SHIFTS), 1, 2 * NRE)
    return ld, table(range(1, S5_GROUP + 1)), table(range(S5_GROUP, 0, -1))


def _s5_tables(a_re, a_im, b_re, b_im, c_re, c_im, d_skip, log_step):
    lam_re, lam_im = _lam_pow(a_re, a_im, log_step, 1.0)
    den = a_re * a_re + a_im * a_im
    nr, ni = lam_re - 1.0, lam_im
    f_re = (nr * a_re + ni * a_im) / den
    f_im = (ni * a_re - nr * a_im) / den
    bbar_re = f_re[..., None] * b_re - f_im[..., None] * b_im
    bbar_im = f_re[..., None] * b_im + f_im[..., None] * b_re
    eye = jnp.eye(G, dtype=F32)

    def blk_b(bb):
        return (jnp.transpose(bb, (0, 2, 1))[:, :, None, :] * eye[:, None, :, None]).reshape(D_SSM, NRE)

    def blk_c(cc):
        return (jnp.transpose(cc, (0, 2, 1))[:, :, None, :] * eye[:, None, :, None]).reshape(NRE, D_SSM)

    b_blk = jnp.concatenate([blk_b(bbar_re), blk_b(bbar_im)], axis=1)
    c_blk = jnp.concatenate([blk_c(c_re), -blk_c(c_im)], axis=0)
    lam = jnp.concatenate([lam_re.reshape(1, NRE), lam_im.reshape(1, NRE)], axis=-1)
    return b_blk, c_blk, lam, d_skip.reshape(1, D_SSM)


def _group_shift(x, d, up=False):
    r = lax.broadcasted_iota(jnp.int32, x.shape, 0) & (S5_GROUP - 1)
    if up:
        return jnp.where(r < S5_GROUP - d, pltpu.roll(x, x.shape[0] - d, 0), 0.0)
    return jnp.where(r >= d, pltpu.roll(x, d, 0), 0.0)


def _s5_scan_steps(hr, hi, cr, ci, lds, lp):
    for ld, d in zip(lds, S5_SHIFTS):
        lr, li = ld[:, :NRE], ld[:, NRE:]
        sr, si = _group_shift(hr, d), _group_shift(hi, d)
        hr, hi = hr + lr * sr - li * si, hi + lr * si + li * sr
        yield
    pr, pi = lp[:, :NRE], lp[:, NRE:]
    rows_r, rows_i = [], []
    for r in range(hr.shape[0] // S5_GROUP):
        br, bi = hr[r * S5_GROUP:(r + 1) * S5_GROUP], hi[r * S5_GROUP:(r + 1) * S5_GROUP]
        br, bi = br + pr * cr - pi * ci, bi + pr * ci + pi * cr
        cr, ci = br[S5_GROUP - 1:S5_GROUP], bi[S5_GROUP - 1:S5_GROUP]
        rows_r.append(br)
        rows_i.append(bi)
        if r % 2:
            yield
    return jnp.concatenate(rows_r, axis=0), jnp.concatenate(rows_i, axis=0)


@jax.custom_vjp
def _known_scan(xr, xi, cr, ci, lam, lds, lp_rev, hr, hi):
    return hr, hi


def _known_scan_fwd(xr, xi, cr, ci, lam, lds, lp_rev, hr, hi):
    return (hr, hi), (cr, ci, lam, lds, lp_rev, hr, hi)


def _known_scan_bwd(res, cts):
    cr, ci, lam, lds, lp_rev, hr, hi = res
    ar, ai = cts
    for ld, d in zip(lds, S5_SHIFTS):
        lr, li = ld[:, :NRE], ld[:, NRE:]
        sr, si = _group_shift(ar, d, up=True), _group_shift(ai, d, up=True)
        ar, ai = ar + lr * sr + li * si, ai + lr * si - li * sr
    qr, qi = lp_rev[:, :NRE], lp_rev[:, NRE:]
    nr, ni = jnp.zeros_like(cr), jnp.zeros_like(ci)
    rows_r, rows_i = [], []
    for r in reversed(range(hr.shape[0] // S5_GROUP)):
        br, bi = ar[r * S5_GROUP:(r + 1) * S5_GROUP], ai[r * S5_GROUP:(r + 1) * S5_GROUP]
        br, bi = br + qr * nr + qi * ni, bi + qr * ni - qi * nr
        nr, ni = br[0:1], bi[0:1]
        rows_r.insert(0, br)
        rows_i.insert(0, bi)
    ar, ai = jnp.concatenate(rows_r, axis=0), jnp.concatenate(rows_i, axis=0)
    lr, li = lam[:, :NRE], lam[:, NRE:]
    dcr, dci = lr * nr + li * ni, lr * ni - li * nr
    first = lax.broadcasted_iota(jnp.int32, hr.shape, 0) == 0
    pr = jnp.where(first, cr, pltpu.roll(hr, 1, 0))
    pi = jnp.where(first, ci, pltpu.roll(hi, 1, 0))
    dlam = jnp.concatenate([jnp.sum(ar * pr + ai * pi, axis=0, keepdims=True),
                            jnp.sum(ai * pr - ar * pi, axis=0, keepdims=True)], axis=1)
    return (ar, ai, dcr, dci, dlam, [jnp.zeros_like(ld) for ld in lds], jnp.zeros_like(lp_rev),
            jnp.zeros_like(hr), jnp.zeros_like(hi))


_known_scan.defvjp(_known_scan_fwd, _known_scan_bwd)


def _interleave(short, long, head_start=0):
    gens = list(short) + list(long)
    results = [None] * len(gens)

    def advance(live):
        still = []
        for idx, gen in live:
            try:
                next(gen)
                still.append((idx, gen))
            except StopIteration as done:
                results[idx] = done.value
        return still

    live_short = advance(list(enumerate(gens))[:len(short)])
    live_long = list(enumerate(gens))[len(short):]
    for _ in range(head_start):
        live_long = advance(live_long)
    live = live_short + live_long
    while live:
        live = advance(live)
    return results[:len(short)], results[len(short):]


def _s5_chunk_gen(u, gate, cr, ci, b_blk, c_blk, lam, dv, wglu, bglu, lds, lp, lp_rev, known_h=None):
    bu = _mm(u, b_blk)
    xr, xi = bu[:, :NRE], bu[:, NRE:]
    yield
    if known_h is None:
        hr, hi = yield from _s5_scan_steps(xr, xi, cr, ci, lds, lp)
    else:
        hr, hi = _known_scan(xr, xi, cr, ci, lam, lds, lp_rev, *known_h)
    y = _mm(jnp.concatenate([hr, hi], axis=1), c_blk) + dv * u
    yield
    y = jax.nn.gelu(y)
    y = y * jax.nn.sigmoid(_mm(y, wglu) + bglu)
    return y * _silu(gate), hr, hi


S5_PAR_SHAPES = [(D_SSM, 2 * NRE), (2 * NRE, D_SSM), (1, 2 * NRE), (1, D_SSM), (D_SSM, D_SSM), (1, D_SSM)]
S5_CONST_SHAPES = [(len(S5_SHIFTS), 1, 2 * NRE), (S5_GROUP, 2 * NRE), (S5_GROUP, 2 * NRE)]
DN_PAR_SHAPES = [(1, D_DN), (1, D_DN), (1, DH)]


def _mix_specs(bl, n_c, rev):
    def chunk(i):
        return n_c - 1 - i if rev else i

    def tok(n):
        return pl.BlockSpec((bl, DN_C, n), lambda i: (0, chunk(i), 0))

    def per_chunk(shape):
        return pl.BlockSpec((bl, 1, *shape), lambda i: (0, chunk(i)) + (0,) * len(shape))

    def whole(shape):
        return pl.BlockSpec(shape, lambda i: (0,) * len(shape))

    return tok, per_chunk, whole


def _dn_post(c):
    s = _silu(c)
    parts = []
    for j in range(12):
        xj = s[:, j * DH:(j + 1) * DH]
        if j < 8:
            xj = xj * lax.rsqrt(jnp.sum(xj * xj, axis=-1, keepdims=True) + EPS)
        if j < 4:
            xj = xj * (DH ** -0.5)
        parts.append(xj)
    return jnp.concatenate(parts, axis=1)


def _dn_prep_fwd(zq, conv_w8, bl, s, tb, name):
    n_s = s // tb
    hb = tb // 8
    w3 = 3 * D_DN

    def body(cur_ref, prev_ref, w_ref, o_ref):
        i = pl.program_id(1)
        prev = jnp.where(i > 0, prev_ref[...], 0.0)
        ext = jnp.concatenate([prev, cur_ref[...]], axis=0)
        c = jnp.zeros((tb, w3), F32)
        for k in range(4):
            sh = ext if k == 3 else pltpu.roll(ext, 3 - k, 0)
            c = c + w_ref[k:k + 1, :] * sh[8:, :]
        o_ref[...] = _dn_post(c)

    row = lambda b, i: (b * n_s + i, 0)
    prv = lambda b, i: (jnp.maximum((b * n_s + i) * hb - 1, 0), 0)
    full = lambda b, i: (0, 0)
    t = bl * s
    return pl.pallas_call(
        body, grid=(bl, n_s),
        in_specs=[pl.BlockSpec((tb, w3), row), pl.BlockSpec((8, w3), prv), pl.BlockSpec((8, w3), full)],
        out_specs=pl.BlockSpec((tb, w3), row),
        out_shape=jax.ShapeDtypeStruct((t, w3), F32),
        name=name, compiler_params=_cp("parallel", "parallel"))(zq, zq, conv_w8)


def _dn_prep_bwd(zq, dqkv, conv_w8, bl, s, tb, name):
    n_s = s // tb
    hb = tb // 8
    w3 = 3 * D_DN
    n_blk8 = bl * s // 8

    def body(cur_ref, prev_ref, next_ref, d_ref, dnext_ref, w_ref, dz_ref, dw_ref):
        b, i = pl.program_id(0), pl.program_id(1)

        @pl.when((b == 0) & (i == 0))
        def _():
            dw_ref[...] = jnp.zeros_like(dw_ref)

        prev = jnp.where(i > 0, prev_ref[...], 0.0)
        nxt = jnp.where(i < n_s - 1, next_ref[...], 0.0)
        dnxt = jnp.where(i < n_s - 1, dnext_ref[...], 0.0)
        ext = jnp.concatenate([prev, cur_ref[...], nxt], axis=0)
        shifted = [ext if k == 3 else pltpu.roll(ext, 3 - k, 0) for k in range(4)]
        c2 = jnp.zeros((tb + 8, w3), F32)
        for k in range(4):
            c2 = c2 + w_ref[k:k + 1, :] * shifted[k][8:, :]
        dpost = jnp.concatenate([d_ref[...], dnxt], axis=0)
        _, vjp = jax.vjp(_dn_post, c2)
        (dc2,) = vjp(dpost)
        dz = jnp.zeros((tb, w3), F32)
        for k in range(4):
            up = dc2 if k == 3 else pltpu.roll(dc2, tb + 8 - (3 - k), 0)
            dz = dz + w_ref[k:k + 1, :] * up[:tb, :]
            dw_ref[k:k + 1, :] += jnp.sum(dc2[:tb, :] * shifted[k][8:8 + tb, :], axis=0, keepdims=True)
        dz_ref[...] = dz.astype(BF16)

    row = lambda b, i: (b * n_s + i, 0)
    prv = lambda b, i: (jnp.maximum((b * n_s + i) * hb - 1, 0), 0)
    nxt = lambda b, i: (jnp.minimum((b * n_s + i + 1) * hb, n_blk8 - 1), 0)
    full = lambda b, i: (0, 0)
    t = bl * s
    return pl.pallas_call(
        body, grid=(bl, n_s),
        in_specs=[pl.BlockSpec((tb, w3), row), pl.BlockSpec((8, w3), prv), pl.BlockSpec((8, w3), nxt),
                  pl.BlockSpec((tb, w3), row), pl.BlockSpec((8, w3), nxt), pl.BlockSpec((8, w3), full)],
        out_specs=[pl.BlockSpec((tb, w3), row), pl.BlockSpec((8, w3), full)],
        out_shape=[jax.ShapeDtypeStruct((t, w3), BF16), jax.ShapeDtypeStruct((8, w3), F32)],
        name=name, compiler_params=_cp("arbitrary", "arbitrary"))(zq, zq, zq, dqkv, dqkv, conv_w8)


def _unit_lower_inverse_steps(ms):
    c_len = ms[0].shape[0]
    eye = lax.broadcasted_iota(jnp.int32, (c_len, c_len), 0) == lax.broadcasted_iota(jnp.int32, (c_len, c_len), 1)
    ident = jnp.where(eye, 1.0, 0.0)
    ps = ms
    tinvs = [ident - m for m in ms]
    for _ in range(c_len.bit_length() - 2):
        ps = [_dg3(p, p, 1, 0) for p in ps]
        yield
        tinvs = [t + _dg3(t, p, 1, 0) for t, p in zip(tinvs, ps)]
        yield
    return tinvs


@jax.custom_vjp
def _known_inverses(ms, tinvs):
    return tinvs


def _known_inverses_fwd(ms, tinvs):
    return tinvs, tinvs


def _known_inverses_bwd(tinvs, gs):
    return [-_dg3(_dg3(t, g, 0, 0), t, 1, 1) for t, g in zip(tinvs, gs)], [jnp.zeros_like(t) for t in tinvs]


_known_inverses.defvjp(_known_inverses_fwd, _known_inverses_bwd)


def _dn_chunk_gen(qkv, zab, zg, states, alog_e, dt_e, ng, known_tinvs=None):
    c_len = DN_C
    r = lax.broadcasted_iota(jnp.int32, (c_len, c_len), 0)
    c = lax.broadcasted_iota(jnp.int32, (c_len, c_len), 1)
    causal, strict = r >= c, r > c
    tril = jnp.where(causal, 1.0, 0.0)
    rr = lax.broadcasted_iota(jnp.int32, (LANES, D_DN), 0)
    cc = lax.broadcasted_iota(jnp.int32, (LANES, D_DN), 1)
    e_a = jnp.where((cc >= rr * DH) & (cc < rr * DH + DH) & (rr < H), 1.0, 0.0)
    e_b = jnp.where((cc >= (rr - H) * DH) & (cc < (rr - H) * DH + DH) & (rr >= H) & (rr < 2 * H), 1.0, 0.0)
    a_e = _sel_r(zab, e_a)
    b_e = _sel_r(zab, e_b)
    beta = jax.nn.sigmoid(b_e)
    g = -jnp.exp(alog_e) * jax.nn.softplus(a_e + dt_e)
    yield
    gc = _sel_l(tril, g)
    glast = jnp.sum(g, axis=0, keepdims=True)
    eg = jnp.exp(gc)
    ekd = jnp.exp(glast - gc)
    dl = jnp.exp(glast)
    yield
    heads = range(H)
    sls = [slice(h * DH, (h + 1) * DH) for h in heads]
    qs = [qkv[:, h * DH:(h + 1) * DH] for h in heads]
    ks = [qkv[:, D_DN + h * DH:D_DN + (h + 1) * DH] for h in heads]
    vs = [qkv[:, 2 * D_DN + h * DH:2 * D_DN + (h + 1) * DH] for h in heads]
    ccols = [gc[:, sl] for sl in sls]
    decs = [jnp.where(causal, jnp.exp(jnp.where(causal, cl - jnp.transpose(cl), 0.0)), 0.0) for cl in ccols]
    kbs = [k * beta[:, sl] for k, sl in zip(ks, sls)]
    ms = [jnp.where(strict, _mm_nt(kb, k) * dec, 0.0) for kb, k, dec in zip(kbs, ks, decs)]
    yield
    if known_tinvs is None:
        tinvs = yield from _unit_lower_inverse_steps(ms)
    else:
        tinvs = _known_inverses(ms, list(known_tinvs))
    sols = [_dot3(t, jnp.concatenate([v * beta[:, sl], kb * eg[:, sl]], axis=1))
            for t, v, kb, sl in zip(tinvs, vs, kbs, sls)]
    yield
    atts = [_mm_nt(q, k) * dec for q, k, dec in zip(qs, ks, decs)]
    vnews = [sol[:, :DH] - _mm(sol[:, DH:], st) for sol, st in zip(sols, states)]
    yield
    os_ = [_mm(q * eg[:, sl], st) + _mm(att, vn) for q, sl, st, att, vn in zip(qs, sls, states, atts, vnews)]
    yield
    new_states = [st * dl[:, sl] + _mm_tn(k * ekd[:, sl], vn) for st, sl, k, vn in zip(states, sls, ks, vnews)]
    yield
    ys = [_rms(o, ng) * _silu(zg[:, sl]) for o, sl in zip(os_, sls)]
    return jnp.concatenate(ys, axis=1), new_states, tinvs


def _mix_fwd(zs, s5_par, s5_const, qkv, zab, zg, dn_par, bl, s, name):
    assert S5_L == DN_C
    n_c = s // DN_C
    nd = len(S5_SHIFTS)
    tok, per_chunk, whole = _mix_specs(bl, n_c, False)

    def body(z_ref, b_ref, c_ref, lam_ref, dv_ref, wg_ref, bg_ref, ld_ref, lp_ref, lpr_ref,
             q_ref, ab_ref, zg_ref, al_ref, dt_ref, ng_ref,
             ys_ref, car_ref, h_ref, yd_ref, st_ref, ti_ref, cs, ssc):
        @pl.when(pl.program_id(0) == 0)
        def _():
            cs[...] = jnp.zeros_like(cs)
            ssc[...] = jnp.zeros_like(ssc)

        lds = [ld_ref[k] for k in range(nd)]
        s5_gens, dn_gens = [], []
        for e in range(bl):
            c = cs[e]
            car_ref[e, 0] = c
            sts = [ssc[e, h] for h in range(H)]
            for h in range(H):
                st_ref[e, 0, h] = sts[h]
            z = z_ref[e]
            s5_gens.append(_s5_chunk_gen(z[:, :D_SSM], z[:, D_SSM:], c[:, :NRE], c[:, NRE:], b_ref[...], c_ref[...], lam_ref[...],
                                         dv_ref[...], wg_ref[...], bg_ref[...], lds, lp_ref[...], lpr_ref[...]))
            dn_gens.append(_dn_chunk_gen(q_ref[e], ab_ref[e], zg_ref[e], sts, al_ref[...], dt_ref[...], ng_ref[...]))
        s5_outs, dn_outs = _interleave(s5_gens, dn_gens, head_start=MIX_HEAD_START)
        for e in range(bl):
            y_s, hr, hi = s5_outs[e]
            y_d, new_sts, tinvs = dn_outs[e]
            ys_ref[e] = y_s
            h_ref[e, :, :NRE] = hr
            h_ref[e, :, NRE:] = hi
            cs[e, :, :NRE] = hr[S5_L - 1:S5_L]
            cs[e, :, NRE:] = hi[S5_L - 1:S5_L]
            yd_ref[e] = y_d
            for h in range(H):
                ssc[e, h] = new_sts[h]
                ti_ref[e, 0, h] = tinvs[h]

    head_mats = jax.ShapeDtypeStruct((bl, n_c, H, DH, DH), F32)
    return pl.pallas_call(
        body, grid=(n_c,),
        in_specs=[tok(2 * D_SSM)] + [whole(sh) for sh in S5_PAR_SHAPES + S5_CONST_SHAPES]
        + [tok(3 * D_DN), tok(LANES), tok(D_DN)] + [whole(sh) for sh in DN_PAR_SHAPES],
        out_specs=[tok(D_SSM), per_chunk((1, 2 * NRE)), tok(2 * NRE), tok(D_DN), per_chunk((H, DH, DH)), per_chunk((H, DH, DH))],
        out_shape=[jax.ShapeDtypeStruct((bl, s, D_SSM), F32), jax.ShapeDtypeStruct((bl, n_c, 1, 2 * NRE), F32),
                   jax.ShapeDtypeStruct((bl, s, 2 * NRE), F32), jax.ShapeDtypeStruct((bl, s, D_DN), F32), head_mats, head_mats],
        scratch_shapes=[pltpu.VMEM((bl, 1, 2 * NRE), F32), pltpu.VMEM((bl, H, DH, DH), F32)],
        name=name, compiler_params=_cp("arbitrary"))(zs, *s5_par, *s5_const, qkv, zab, zg, *dn_par)


def _mix_bwd(zs, carries, h_all, dy_s, s5_par, s5_const, qkv, zab, zg, states, tinvs, dy_d, dn_par, bl, s, name):
    n_c = s // DN_C
    nd = len(S5_SHIFTS)
    tok, per_chunk, whole = _mix_specs(bl, n_c, True)

    def both(examples, s5_tabs, s5_consts, dn_tabs):
        s5_gens = [_s5_chunk_gen(u, gate, cr, ci, *s5_tabs, *s5_consts, known_h=(hr, hi))
                   for u, gate, cr, ci, hr, hi, _, _, _, _, _ in examples]
        dn_gens = [_dn_chunk_gen(q, ab, zgate, sts, *dn_tabs, known_tinvs=known)
                   for _, _, _, _, _, _, q, ab, zgate, sts, known in examples]
        s5_outs, dn_outs = _interleave(s5_gens, dn_gens, head_start=MIX_HEAD_START)
        return [(y_s, hr[S5_L - 1:S5_L], hi[S5_L - 1:S5_L], y_d, new_sts)
                for (y_s, hr, hi), (y_d, new_sts, _) in zip(s5_outs, dn_outs)]

    def body(z_ref, car_ref, h_ref, dys_ref, b_ref, c_ref, lam_ref, dv_ref, wg_ref, bg_ref, ld_ref, lp_ref, lpr_ref,
             q_ref, ab_ref, zg_ref, st_ref, ti_ref, dyd_ref, al_ref, dt_ref, ng_ref,
             dz_ref, db_ref, dc_ref, dlam_ref, ddv_ref, dwg_ref, dbg_ref,
             dq_ref, dab_ref, dzg_ref, dal_ref, ddt_ref, dng_ref, dcs, dsc):
        accs = (db_ref, dc_ref, dlam_ref, ddv_ref, dwg_ref, dbg_ref, dal_ref, ddt_ref, dng_ref)

        @pl.when(pl.program_id(0) == 0)
        def _():
            for r in accs + (dcs, dsc):
                r[...] = jnp.zeros_like(r)

        examples = []
        for e in range(bl):
            z = z_ref[e]
            c = car_ref[e, 0]
            examples.append((z[:, :D_SSM], z[:, D_SSM:], c[:, :NRE], c[:, NRE:], h_ref[e, :, :NRE], h_ref[e, :, NRE:],
                             q_ref[e], ab_ref[e], zg_ref[e], [st_ref[e, 0, h] for h in range(H)],
                             [ti_ref[e, 0, h] for h in range(H)]))
        _, vjp = jax.vjp(both, examples,
                         (b_ref[...], c_ref[...], lam_ref[...], dv_ref[...], wg_ref[...], bg_ref[...]),
                         ([ld_ref[k] for k in range(nd)], lp_ref[...], lpr_ref[...]),
                         (al_ref[...], dt_ref[...], ng_ref[...]))
        cts = []
        for e in range(bl):
            dc = dcs[e]
            cts.append((dys_ref[e], dc[:, :NRE], dc[:, NRE:], dyd_ref[e], [dsc[e, h] for h in range(H)]))
        d_examples, d_s5, _, d_dn = vjp(cts)
        for e in range(bl):
            du, dgate, dcr, dci, _, _, dq, dab, dzg, dsts, _ = d_examples[e]
            dz_ref[e] = jnp.concatenate([du, dgate], axis=1).astype(BF16)
            dcs[e, :, :NRE] = dcr
            dcs[e, :, NRE:] = dci
            dq_ref[e] = dq
            dab_ref[e] = dab.astype(BF16)
            dzg_ref[e] = dzg.astype(BF16)
            for h in range(H):
                dsc[e, h] = dsts[h]
        for r, ct in zip(accs, (*d_s5, *d_dn)):
            r[...] += ct

    head_mats = per_chunk((H, DH, DH))
    outs = pl.pallas_call(
        body, grid=(n_c,),
        in_specs=[tok(2 * D_SSM), per_chunk((1, 2 * NRE)), tok(2 * NRE), tok(D_SSM)]
        + [whole(sh) for sh in S5_PAR_SHAPES + S5_CONST_SHAPES]
        + [tok(3 * D_DN), tok(LANES), tok(D_DN), head_mats, head_mats, tok(D_DN)] + [whole(sh) for sh in DN_PAR_SHAPES],
        out_specs=[tok(2 * D_SSM)] + [whole(sh) for sh in S5_PAR_SHAPES]
        + [tok(3 * D_DN), tok(LANES), tok(D_DN)] + [whole(sh) for sh in DN_PAR_SHAPES],
        out_shape=[jax.ShapeDtypeStruct((bl, s, 2 * D_SSM), BF16)] + [jax.ShapeDtypeStruct(sh, F32) for sh in S5_PAR_SHAPES]
        + [jax.ShapeDtypeStruct((bl, s, 3 * D_DN), F32), jax.ShapeDtypeStruct((bl, s, LANES), BF16),
           jax.ShapeDtypeStruct((bl, s, D_DN), BF16)]
        + [jax.ShapeDtypeStruct(sh, F32) for sh in DN_PAR_SHAPES],
        scratch_shapes=[pltpu.VMEM((bl, 1, 2 * NRE), F32), pltpu.VMEM((bl, H, DH, DH), F32)],
        name=name, compiler_params=_cp("arbitrary"))(
            zs, carries, h_all, dy_s, *s5_par, *s5_const, qkv, zab, zg, states, tinvs, dy_d, *dn_par)
    return outs[:7], outs[7:]


def _sg_fn(n_chunk):
    def f(z, lng, lnb, w, bsp_t):
        u = jax.nn.gelu(z[:, :D_SG])
        v = jax.nn.gelu(z[:, D_SG:2 * D_SG])
        gate = z[:, 2 * D_SG:]
        xc = v - jnp.mean(v, axis=-1, keepdims=True)
        vn = xc * lax.rsqrt(jnp.mean(xc * xc, axis=-1, keepdims=True) + EPS) * lng + lnb
        r = lax.broadcasted_iota(jnp.int32, (SG_C, SG_C), 0)
        c = lax.broadcasted_iota(jnp.int32, (SG_C, SG_C), 1)
        causal = r >= c
        first_half = c < SG_C // 2
        rr = lax.broadcasted_iota(jnp.int32, (LANES, D_SG), 0)
        cc = lax.broadcasted_iota(jnp.int32, (LANES, D_SG), 1)
        expand = jnp.where((cc >= rr * 64) & (cc < rr * 64 + 64) & (rr < 4), 1.0, 0.0)
        bias = _sel_r(bsp_t, expand)
        wm = [jnp.where(causal, w[h], 0.0) for h in range(4)]
        rows = []
        for ci in range(n_chunk):
            vc = vn[ci * SG_C:(ci + 1) * SG_C]
            pairs = []
            for pr in range(2):
                vp = vc[:, pr * LANES:(pr + 1) * LANES]
                pairs.append(jnp.where(first_half, _mm(wm[2 * pr], vp), _mm(wm[2 * pr + 1], vp)))
            rows.append(jnp.concatenate(pairs, axis=1) + bias)
        sp = jnp.concatenate(rows, axis=0) if n_chunk > 1 else rows[0]
        return u * sp * _silu(gate)

    return f


def _sg_specs():
    full = lambda i: (0, 0)
    full3 = lambda i: (0, 0, 0)
    par = [pl.BlockSpec((1, D_SG), full), pl.BlockSpec((1, D_SG), full), pl.BlockSpec((4, SG_C, SG_C), full3),
           pl.BlockSpec((SG_C, LANES), full)]
    par_shapes = [(1, D_SG), (1, D_SG), (4, SG_C, SG_C), (SG_C, LANES)]
    return par, par_shapes


def _sg_fwd(zsg, params, tb, name):
    t = zsg.shape[0]
    f = _sg_fn(tb // SG_C)
    par, _ = _sg_specs()

    def body(z_ref, g_ref, b_ref, w_ref, bs_ref, y_ref):
        y_ref[...] = f(z_ref[...], g_ref[...], b_ref[...], w_ref[...], bs_ref[...])

    row = lambda i: (i, 0)
    return pl.pallas_call(
        body, grid=(t // tb,), in_specs=[pl.BlockSpec((tb, 3 * D_SG), row)] + par,
        out_specs=pl.BlockSpec((tb, D_SG), row), out_shape=jax.ShapeDtypeStruct((t, D_SG), F32),
        name=name, compiler_params=_cp("parallel"))(zsg, *params)


def _sg_bwd(zsg, dy, params, tb, name):
    t = zsg.shape[0]
    f = _sg_fn(tb // SG_C)
    par, par_shapes = _sg_specs()

    def body(z_ref, dy_ref, g_ref, b_ref, w_ref, bs_ref, dz_ref, dg_ref, db_ref, dw_ref, dbs_ref):
        accs = (dg_ref, db_ref, dw_ref, dbs_ref)

        @pl.when(pl.program_id(0) == 0)
        def _():
            for r in accs:
                r[...] = jnp.zeros_like(r)

        _, vjp = jax.vjp(f, z_ref[...], g_ref[...], b_ref[...], w_ref[...], bs_ref[...])
        cts = vjp(dy_ref[...])
        dz_ref[...] = cts[0].astype(BF16)
        for r, ct in zip(accs, cts[1:]):
            r[...] += ct

    row = lambda i: (i, 0)
    return pl.pallas_call(
        body, grid=(t // tb,), in_specs=[pl.BlockSpec((tb, 3 * D_SG), row), pl.BlockSpec((tb, D_SG), row)] + par,
        out_specs=[pl.BlockSpec((tb, 3 * D_SG), row)] + par,
        out_shape=[jax.ShapeDtypeStruct((t, 3 * D_SG), BF16)] + [jax.ShapeDtypeStruct(sh, F32) for sh in par_shapes],
        name=name, compiler_params=_cp("arbitrary"))(zsg, dy, *params)


def _out_fwd(x, ys, p, w_out, pg, w_gate, w_ple, tb, name):
    t = x.shape[0]

    def body(x_ref, y0, y1, y2, p_ref, wo_ref, pg_ref, wg_ref, wp_ref, o_ref, x1_ref, gate_ref):
        y = jnp.concatenate([y0[...], y1[...], y2[...]], axis=1).astype(BF16)
        x1 = x_ref[...] + jnp.dot(y, wo_ref[...], preferred_element_type=F32)
        hn = _rms(x1, pg_ref[...]).astype(BF16)
        gate = jax.nn.sigmoid(jnp.dot(hn, wg_ref[...], preferred_element_type=F32))
        pp = jnp.dot(p_ref[...].astype(BF16), wp_ref[...], preferred_element_type=F32)
        o_ref[...] = x1 + gate * pp
        x1_ref[...] = x1
        gate_ref[...] = gate

    row = lambda i: (i, 0)
    full = lambda i: (0, 0)
    return pl.pallas_call(
        body, grid=(t // tb,),
        in_specs=[pl.BlockSpec((tb, D), row), pl.BlockSpec((tb, D_SSM), row), pl.BlockSpec((tb, D_DN), row),
                  pl.BlockSpec((tb, D_SG), row), pl.BlockSpec((tb, D_PLE), row), pl.BlockSpec((D, D), full),
                  pl.BlockSpec((1, D), full), pl.BlockSpec((D, D), full), pl.BlockSpec((D_PLE, D), full)],
        out_specs=[pl.BlockSpec((tb, D), row)] * 3, out_shape=[jax.ShapeDtypeStruct((t, D), F32)] * 3,
        name=name, compiler_params=_cp("parallel"))(x, *ys, p, w_out, pg, w_gate, w_ple)


def _out_bwd(x1, gate, ys, p, dx2, w_out, pg, w_gate, w_ple, tb, name):
    t = x1.shape[0]

    def body(x1_ref, gate_ref, y0, y1, y2, p_ref, d_ref, wo_ref, pg_ref, wg_ref, wp_ref,
             dx_ref, dy0, dy1, dy2, dwo_ref, dpg_ref, dwg_ref, dwp_ref):
        accs = (dwo_ref, dpg_ref, dwg_ref, dwp_ref)

        @pl.when(pl.program_id(0) == 0)
        def _():
            for r in accs:
                r[...] = jnp.zeros_like(r)

        y = jnp.concatenate([y0[...], y1[...], y2[...]], axis=1).astype(BF16)
        hn, rms_vjp = jax.vjp(_rms, x1_ref[...], pg_ref[...])
        hb = hn.astype(BF16)
        gate = gate_ref[...]
        pb = p_ref[...].astype(BF16)
        pp = jnp.dot(pb, wp_ref[...], preferred_element_type=F32)
        d2 = d_ref[...]
        dpp = (d2 * gate).astype(BF16)
        dlog = (d2 * pp * gate * (1.0 - gate)).astype(BF16)
        dwp_ref[...] += _dg(pb, dpp, 0, 0)
        dwg_ref[...] += _dg(hb, dlog, 0, 0)
        dx1_n, dpg = rms_vjp(_dg(dlog, wg_ref[...], 1, 1))
        dpg_ref[...] += dpg
        dx1 = d2 + dx1_n
        dx_ref[...] = dx1
        db = dx1.astype(BF16)
        dwo_ref[...] += _dg(y, db, 0, 0)
        dy = _dg(db, wo_ref[...], 1, 1)
        dy0[...] = dy[:, :D_SSM]
        dy1[...] = dy[:, D_SSM:D_SSM + D_DN]
        dy2[...] = dy[:, D_SSM + D_DN:]

    row = lambda i: (i, 0)
    full = lambda i: (0, 0)
    acts = [pl.BlockSpec((tb, D), row), pl.BlockSpec((tb, D_SSM), row), pl.BlockSpec((tb, D_DN), row), pl.BlockSpec((tb, D_SG), row)]
    wts = [pl.BlockSpec((D, D), full), pl.BlockSpec((1, D), full), pl.BlockSpec((D, D), full), pl.BlockSpec((D_PLE, D), full)]
    return pl.pallas_call(
        body, grid=(t // tb,),
        in_specs=[pl.BlockSpec((tb, D), row)] + acts + [pl.BlockSpec((tb, D_PLE), row), pl.BlockSpec((tb, D), row)] + wts,
        out_specs=acts + wts,
        out_shape=[jax.ShapeDtypeStruct((t, n), F32) for n in (D, D_SSM, D_DN, D_SG)]
        + [jax.ShapeDtypeStruct(sh, F32) for sh in ((D, D), (1, D), (D, D), (D_PLE, D))],
        name=name, compiler_params=_cp("arbitrary"))(x1, gate, *ys, p, dx2, w_out, pg, w_gate, w_ple)


def _loss_head(x, fg, target, tb, name):
    t = x.shape[0]

    def body(x_ref, g_ref, t_ref, dx_ref, dg_ref, loss_ref):
        @pl.when(pl.program_id(0) == 0)
        def _():
            dg_ref[...] = jnp.zeros_like(dg_ref)
            loss_ref[...] = jnp.zeros_like(loss_ref)

        y, vjp = jax.vjp(_rms, x_ref[...], g_ref[...])
        err = y - t_ref[...]
        loss_ref[...] += jnp.zeros_like(loss_ref) + 0.5 * jnp.sum(err * err) / D
        dx, dg = vjp(err / D)
        dx_ref[...] = dx
        dg_ref[...] += dg

    row = lambda i: (i, 0)
    full = lambda i: (0, 0)
    return pl.pallas_call(
        body, grid=(t // tb,),
        in_specs=[pl.BlockSpec((tb, D), row), pl.BlockSpec((1, D), full), pl.BlockSpec((tb, D), row)],
        out_specs=[pl.BlockSpec((tb, D), row), pl.BlockSpec((1, D), full), pl.BlockSpec((1, LANES), full)],
        out_shape=[jax.ShapeDtypeStruct((t, D), F32), jax.ShapeDtypeStruct((1, D), F32), jax.ShapeDtypeStruct((1, LANES), F32)],
        name=name, compiler_params=_cp("arbitrary"))(x, fg, target)


def _hbm_specs(n):
    return [pl.BlockSpec(memory_space=pl.ANY)] * n


def _all_gather(blocks, name):
    n = len(blocks)

    def body(*refs):
        ins, outs = refs[:n], refs[n:2 * n]
        send_sems, recv_sems, local_sems = refs[2 * n:]
        x, y, c = lax.axis_index("x"), lax.axis_index("y"), lax.axis_index("c")
        me, sibling = (x, y, c), (x, y, 1 - c)
        chips = [(1 - x, y), (x, 1 - y), (1 - x, 1 - y)]

        def slot(a, px, py, pc):
            return outs[a].at[4 * px + 2 * py + pc]

        def copy(a, k, blk, to, src=None):
            return pltpu.make_async_remote_copy(
                src_ref=slot(a, *blk) if src is None else src, dst_ref=slot(a, *blk),
                send_sem=send_sems.at[7 * a + k], recv_sem=recv_sems.at[7 * a + k],
                device_id=to, device_id_type=pl.DeviceIdType.MESH)

        mines = [pltpu.make_async_copy(ins[a], slot(a, *me), local_sems.at[a]) for a in range(n)]
        for cp in mines:
            cp.start()
        first = []
        for a in range(n):
            first.append(copy(a, 0, me, sibling, src=ins[a]))
            first += [copy(a, 1 + j, me, (*chip, c), src=ins[a]) for j, chip in enumerate(chips)]
        for cp in first:
            cp.start()
        passed = []
        for j, chip in enumerate(chips):
            for a in range(n):
                copy(a, 1 + j, (*chip, c), me).wait_recv()
                onward = copy(a, 4 + j, (*chip, c), sibling)
                onward.start()
                passed.append(onward)
        for a in range(n):
            copy(a, 0, sibling, me).wait_recv()
        for j, chip in enumerate(chips):
            for a in range(n):
                copy(a, 4 + j, (*chip, 1 - c), me).wait_recv()
        for cp in first + passed:
            cp.wait_send()
        for cp in mines:
            cp.wait()

    return pl.pallas_call(
        body, out_shape=[jax.ShapeDtypeStruct((N_DEV, *b.shape), b.dtype) for b in blocks],
        in_specs=_hbm_specs(n), out_specs=_hbm_specs(n),
        scratch_shapes=[pltpu.SemaphoreType.DMA((7 * n,)), pltpu.SemaphoreType.DMA((7 * n,)), pltpu.SemaphoreType.DMA((n,))],
        name=name)(*blocks)


def _pair_exchange(gs, name):
    n = len(gs)

    def body(*refs):
        ins, recvs = refs[:n], refs[n:2 * n]
        send_sems, recv_sems = refs[2 * n:]
        x, y, c = lax.axis_index("x"), lax.axis_index("y"), lax.axis_index("c")
        remote = [pltpu.make_async_remote_copy(
            src_ref=ins[a], dst_ref=recvs[a], send_sem=send_sems.at[a], recv_sem=recv_sems.at[a],
            device_id=(x, y, 1 - c), device_id_type=pl.DeviceIdType.MESH) for a in range(n)]
        for cp in remote:
            cp.start()
        for cp in remote:
            cp.wait_send()
            cp.wait_recv()

    return pl.pallas_call(
        body, out_shape=[jax.ShapeDtypeStruct(g.shape, g.dtype) for g in gs], in_specs=_hbm_specs(n), out_specs=_hbm_specs(n),
        scratch_shapes=[pltpu.SemaphoreType.DMA((n,)), pltpu.SemaphoreType.DMA((n,))],
        name=name)(*gs)


def _chip_exchange(ps, name):
    n = len(ps)

    def body(*refs):
        ins, outs = refs[:n], refs[n:2 * n]
        send_sems, recv_sems, local_sems = refs[2 * n:]
        x, y, c = lax.axis_index("x"), lax.axis_index("y"), lax.axis_index("c")
        my_chip = 2 * x + y
        local = [pltpu.make_async_copy(ins[a].at[my_chip], outs[a].at[my_chip], local_sems.at[a]) for a in range(n)]
        remote = []
        for j in range(1, 4):
            px = 1 - x if j & 2 else x
            py = 1 - y if j & 1 else y
            for a in range(n):
                remote.append(pltpu.make_async_remote_copy(
                    src_ref=ins[a].at[2 * px + py], dst_ref=outs[a].at[my_chip],
                    send_sem=send_sems.at[3 * a + j - 1], recv_sem=recv_sems.at[3 * a + j - 1],
                    device_id=(px, py, c), device_id_type=pl.DeviceIdType.MESH))
        for cp in local + remote:
            cp.start()
        for cp in remote:
            cp.wait_send()
            cp.wait_recv()
        for cp in local:
            cp.wait()

    return pl.pallas_call(
        body, out_shape=[jax.ShapeDtypeStruct(q.shape, q.dtype) for q in ps], in_specs=_hbm_specs(n), out_specs=_hbm_specs(n),
        scratch_shapes=[pltpu.SemaphoreType.DMA((3 * n,)), pltpu.SemaphoreType.DMA((3 * n,)), pltpu.SemaphoreType.DMA((n,))],
        name=name)(*ps)


def _row_block(rows, bytes_per_row):
    best = None
    for rb in range(16, rows + 1, 16):
        if rows % rb == 0 and rb * bytes_per_row <= ELEMENTWISE_STEP_BYTES:
            best = rb
    return rows if best is None else best


def _add_pair(own, recv, name):
    shape = own.shape
    last = shape[-1]
    rows = own.size // last
    rb = _row_block(rows, 3 * 4 * (-(-last // LANES) * LANES))

    def body(a_ref, b_ref, o_ref):
        o_ref[...] = (a_ref[...].astype(F32) + b_ref[...].astype(F32)).astype(o_ref.dtype)

    row = lambda i: (i, 0)
    out = pl.pallas_call(
        body, grid=(rows // rb,), in_specs=[pl.BlockSpec((rb, last), row)] * 2, out_specs=pl.BlockSpec((rb, last), row),
        out_shape=jax.ShapeDtypeStruct((rows, last), own.dtype), name=name,
        compiler_params=_cp("parallel"))(own.reshape(rows, last), recv.reshape(rows, last))
    return out.reshape(shape)


def _sum_adamw(gk, w, m, v, name):
    shape = w.shape
    n_part = gk.shape[0]
    last = shape[-1]
    rows = w.size // last
    rb = _row_block(rows, (n_part + 7) * 4 * (-(-last // LANES) * LANES))

    def body(g_ref, w_ref, m_ref, v_ref, go_ref, d_ref, mo_ref, vo_ref):
        g = g_ref[0].astype(F32)
        for k in range(1, n_part):
            g = g + g_ref[k].astype(F32)
        mn = ADAM_B1 * m_ref[...] + (1.0 - ADAM_B1) * g
        vn = ADAM_B2 * v_ref[...] + (1.0 - ADAM_B2) * jnp.square(g)
        m_hat = mn / (1.0 - ADAM_B1 ** ADAM_STEP)
        v_hat = vn / (1.0 - ADAM_B2 ** ADAM_STEP)
        go_ref[...] = g
        d_ref[...] = -ADAM_LR * (m_hat / (jnp.sqrt(v_hat) + ADAM_EPS) + ADAM_WD * w_ref[...])
        mo_ref[...] = mn
        vo_ref[...] = vn

    row = lambda i: (i, 0)
    outs = pl.pallas_call(
        body, grid=(rows // rb,),
        in_specs=[pl.BlockSpec((n_part, rb, last), lambda i: (0, i, 0))] + [pl.BlockSpec((rb, last), row)] * 3,
        out_specs=[pl.BlockSpec((rb, last), row)] * 4,
        out_shape=[jax.ShapeDtypeStruct((rows, last), F32)] * 4,
        name=name, compiler_params=_cp("parallel"))(gk.reshape(n_part, rows, last), *[a.reshape(rows, last) for a in (w, m, v)])
    return [o.reshape(shape) for o in outs]


def _seg_rows(shape):
    n = 1
    for d in shape:
        n *= d
    return -(-n // (8 * LANES)) * 8


def _pack(arrs):
    segs = []
    for a in arrs:
        r = _seg_rows(a.shape)
        segs.append(jnp.pad(a.reshape(-1).astype(F32), (0, r * LANES - a.size)).reshape(r, LANES))
    rows = sum(s.shape[0] for s in segs)
    total = -(-rows // PACK_ROWS) * PACK_ROWS
    if total > rows:
        segs.append(jnp.zeros((total - rows, LANES), F32))
    return jnp.concatenate(segs, axis=0)


def _unpack(pack, shapes):
    out, off = [], 0
    for sh in shapes:
        r = _seg_rows(sh)
        n = 1
        for d in sh:
            n *= d
        out.append(pack[off:off + r].reshape(-1)[:n].reshape(sh))
        off += r
    return out


def _to_dest_blocks(full, axis, dtype):
    sh = list(full.shape)
    sh[axis:axis + 1] = [N_DEV // 2, 2, sh[axis] // N_DEV]
    return jnp.moveaxis(full.reshape(sh), (axis, axis + 1), (1, 0)).astype(dtype)


def _from_gathered(g, axis):
    m = jnp.moveaxis(g, 0, axis)
    sh = list(m.shape)
    sh[axis:axis + 2] = [sh[axis] * sh[axis + 1]]
    return m.reshape(sh)


D_IN = 3336
W_IN_SHARD = D_IN // N_DEV
W_IN_MOVES = ((0, 2048, 0), (2048, 2056, 3328), (2056, D_IN, 2048))
Z_OFFSETS = (0, 512, 2048, 2560, 3328)


def _w_in_windows(k):
    lo, hi = k * W_IN_SHARD, (k + 1) * W_IN_SHARD
    out = []
    for a, b, mine in W_IN_MOVES:
        a2, b2 = max(a, lo), min(b, hi)
        if b2 > a2:
            out.append((a2 - lo, b2 - a2, mine + a2 - a))
    return out


def _assemble_w_in(gathered, name):
    depth = gathered.shape[1]
    rb = 256

    def body(g_ref, o_ref):
        o_ref[0, :, D_IN:] = jnp.zeros((rb, ZW - D_IN), o_ref.dtype)
        for k in range(N_DEV):
            for off, width, mine in _w_in_windows(k):
                o_ref[0, :, mine:mine + width] = g_ref[k, 0, :, off:off + width]

    return pl.pallas_call(
        body, grid=(depth, D // rb),
        in_specs=[pl.BlockSpec((N_DEV, 1, rb, W_IN_SHARD), lambda l, i: (0, l, i, 0))],
        out_specs=pl.BlockSpec((1, rb, ZW), lambda l, i: (l, i, 0)),
        out_shape=jax.ShapeDtypeStruct((depth, D, ZW), gathered.dtype),
        name=name, compiler_params=_cp("parallel", "parallel"))(gathered)


def _split_dw_in(dws, name):
    depth = len(dws)
    rb = 128

    def body(*refs):
        o_ref = refs[-1]
        for l in range(depth):
            pieces = refs[5 * l:5 * l + 5]
            for k in range(N_DEV):
                for off, width, mine in _w_in_windows(k):
                    for p_ref, start, n in zip(pieces, Z_OFFSETS, Z_PIECES):
                        a, b = max(mine, start), min(mine + width, start + n)
                        if b > a:
                            o_ref[k % 2, k // 2, l, :, off + a - mine:off + b - mine] = (
                                p_ref[:, a - start:b - start].astype(o_ref.dtype))

    row = lambda i: (i, 0)
    flat = [piece for layer in dws for piece in layer]
    return pl.pallas_call(
        body, grid=(D // rb,),
        in_specs=[pl.BlockSpec((rb, n), row) for _ in range(depth) for n in Z_PIECES],
        out_specs=pl.BlockSpec((2, N_DEV // 2, depth, rb, W_IN_SHARD), lambda i: (0, 0, 0, i, 0)),
        out_shape=jax.ShapeDtypeStruct((2, N_DEV // 2, depth, D, W_IN_SHARD), WIRE['w_in']),
        name=name, compiler_params=_cp("parallel"))(*flat)


def _local_step(x, p, wts, target):
    bl, s, _ = x.shape
    t = bl * s
    depth = p.shape[0]
    tb, sg_tb, prep_tb = TB, SG_TB, PREP_TB

    def by_example(a):
        return a.reshape(bl, s, a.shape[-1])

    def flat(a):
        return a.reshape(t, a.shape[-1])

    xs = [x.reshape(t, D)]
    saved = []
    for i in range(depth):
        li = f"l{i}"
        ng = wts['norm_g'][i].reshape(1, D)
        w_in = wts['w_in'][i]
        s5_par_in = (wts['ssm_a_re'][i], wts['ssm_a_im'][i], wts['ssm_b_re'][i], wts['ssm_b_im'][i],
                     wts['ssm_c_re'][i], wts['ssm_c_im'][i], wts['ssm_d'][i], wts['ssm_log_step'][i])
        tabs, tab_vjp = jax.vjp(_s5_tables, *s5_par_in)
        s5_par = (*tabs, wts['ssm_w_glu'][i], wts['ssm_b_glu'][i].reshape(1, D_SSM))
        s5_const = _s5_powers(wts['ssm_a_re'][i], wts['ssm_a_im'][i], wts['ssm_log_step'][i])
        conv8 = jnp.pad(wts['dn_conv_w'][i], ((0, 4), (0, 0)))
        dn_par = (jnp.repeat(wts['dn_a_log'][i], DH).reshape(1, D_DN), jnp.repeat(wts['dn_dt_bias'][i], DH).reshape(1, D_DN),
                  wts['dn_norm_g'][i].reshape(1, DH))
        sg_par = (wts['sg_ln_g'][i].reshape(1, D_SG), wts['sg_ln_b'][i].reshape(1, D_SG), wts['sg_w'][i],
                  jnp.pad(jnp.transpose(wts['sg_b'][i]), ((0, 0), (0, LANES - 4))))
        out_par = (wts['w_out'][i].astype(BF16), wts['ple_norm_g'][i].reshape(1, D), wts['w_ple_gate'][i].astype(BF16),
                   wts['w_ple'][i].astype(BF16))
        pi = p[i].reshape(t, D_PLE)

        z_ssm, z_qkv, z_gdn, z_sg, z_ab = _in_proj_fwd(xs[i], ng, w_in, tb, f"in_proj_fwd_{li}")
        qkvn = _dn_prep_fwd(z_qkv, conv8, bl, s, prep_tb, f"dn_prep_fwd_{li}")
        y_ssm, carries, h_all, y_dn, states, tinvs = _mix_fwd(by_example(z_ssm), s5_par, s5_const, by_example(qkvn),
                                                              by_example(z_ab), by_example(z_gdn), dn_par, bl, s, f"mix_fwd_{li}")
        y_sg = _sg_fwd(z_sg, sg_par, sg_tb, f"sg_fwd_{li}")
        ys = (flat(y_ssm), flat(y_dn), y_sg)
        x_next, x1, gate = _out_fwd(xs[i], ys, pi, *out_par, tb, f"out_fwd_{li}")
        xs.append(x_next)
        saved.append(dict(ng=ng, w_in=w_in, tab_vjp=tab_vjp, s5_par=s5_par, s5_const=s5_const, conv8=conv8, dn_par=dn_par,
                          sg_par=sg_par, out_par=out_par, pi=pi, z=(z_ssm, z_qkv, z_gdn, z_sg, z_ab), carries=carries,
                          h_all=h_all, qkvn=qkvn, x1=x1, gate=gate,
                          states=states, tinvs=tinvs, ys=ys))

    dx, dfg, loss_vec = _loss_head(xs[depth], wts['final_norm_g'].reshape(1, D), target.reshape(t, D), tb, "loss_head")
    grads = {n: [None] * depth for n in WEIGHTS if n != 'final_norm_g'}
    grads['final_norm_g'] = dfg.reshape(D)
    for i in reversed(range(depth)):
        li = f"l{i}"
        sv = saved[i]
        z_ssm, z_qkv, z_gdn, z_sg, z_ab = sv['z']
        dx_res, dy_ssm, dy_dn, dy_sg, dwo, dpg, dwg, dwp = _out_bwd(sv['x1'], sv['gate'], sv['ys'], sv['pi'], dx, *sv['out_par'], tb,
                                                                    f"out_bwd_{li}")
        dz_sg, dlng, dlnb, dsgw, dbsp = _sg_bwd(z_sg, dy_sg, sv['sg_par'], sg_tb, f"sg_bwd_{li}")
        (dz_ssm, dbb, dcb, dlam, ddv, dwglu, dbglu), (dqkvn, dz_ab, dz_gdn, dal, ddt, dng) = _mix_bwd(
            by_example(z_ssm), sv['carries'], sv['h_all'], by_example(dy_ssm), sv['s5_par'], sv['s5_const'],
            by_example(sv['qkvn']), by_example(z_ab), by_example(z_gdn), sv['states'], sv['tinvs'], by_example(dy_dn),
            sv['dn_par'], bl, s, f"mix_bwd_{li}")
        dz_qkv, dconv = _dn_prep_bwd(z_qkv, flat(dqkvn), sv['conv8'], bl, s, prep_tb, f"dn_prep_bwd_{li}")
        dzs = (flat(dz_ssm), dz_qkv, flat(dz_gdn), dz_sg, flat(dz_ab))
        dx, dnorm = _in_proj_bwd_dx(xs[i], sv['ng'], sv['w_in'], dzs, dx_res, tb, f"in_proj_bwd_dx_{li}")
        dws = _in_proj_bwd_dw(xs[i], sv['ng'], dzs, min(TB_DW, t), f"in_proj_bwd_dw_{li}")
        ds5 = sv['tab_vjp']((dbb, dcb, dlam, ddv))
        for n, gval in zip(('ssm_a_re', 'ssm_a_im', 'ssm_b_re', 'ssm_b_im', 'ssm_c_re', 'ssm_c_im', 'ssm_d', 'ssm_log_step'), ds5):
            grads[n][i] = gval
        grads['norm_g'][i] = dnorm.reshape(D)
        grads['w_in'][i] = dws
        grads['ssm_w_glu'][i] = dwglu
        grads['ssm_b_glu'][i] = dbglu.reshape(D_SSM)
        grads['dn_conv_w'][i] = dconv[:4]
        grads['dn_a_log'][i] = dal.reshape(H, DH).sum(axis=1)
        grads['dn_dt_bias'][i] = ddt.reshape(H, DH).sum(axis=1)
        grads['dn_norm_g'][i] = dng.reshape(DH)
        grads['sg_ln_g'][i] = dlng.reshape(D_SG)
        grads['sg_ln_b'][i] = dlnb.reshape(D_SG)
        grads['sg_w'][i] = dsgw
        grads['sg_b'][i] = jnp.transpose(dbsp[:, :4])
        grads['w_out'][i] = dwo
        grads['ple_norm_g'][i] = dpg.reshape(D)
        grads['w_ple_gate'][i] = dwg
        grads['w_ple'][i] = dwp
    grads = {n: (g if n in ('final_norm_g', 'w_in') else jnp.stack(g)) for n, g in grads.items()}
    return loss_vec[0, 0], dx.reshape(bl, s, D), grads


def kernel(x, p, norm_g, w_in, ssm_a_re, ssm_a_im, ssm_b_re, ssm_b_im, ssm_c_re, ssm_c_im, ssm_d, ssm_log_step, ssm_w_glu, ssm_b_glu, dn_conv_w, dn_a_log, dn_dt_bias, dn_norm_g, sg_ln_g, sg_ln_b, sg_w, sg_b, w_out, ple_norm_g, w_ple_gate, w_ple, final_norm_g, loss_target, m_norm_g, m_w_in, m_ssm_a_re, m_ssm_a_im, m_ssm_b_re, m_ssm_b_im, m_ssm_c_re, m_ssm_c_im, m_ssm_d, m_ssm_log_step, m_ssm_w_glu, m_ssm_b_glu, m_dn_conv_w, m_dn_a_log, m_dn_dt_bias, m_dn_norm_g, m_sg_ln_g, m_sg_ln_b, m_sg_w, m_sg_b, m_w_out, m_ple_norm_g, m_w_ple_gate, m_w_ple, m_final_norm_g, v_norm_g, v_w_in, v_ssm_a_re, v_ssm_a_im, v_ssm_b_re, v_ssm_b_im, v_ssm_c_re, v_ssm_c_im, v_ssm_d, v_ssm_log_step, v_ssm_w_glu, v_ssm_b_glu, v_dn_conv_w, v_dn_a_log, v_dn_dt_bias, v_dn_norm_g, v_sg_ln_g, v_sg_ln_b, v_sg_w, v_sg_b, v_w_out, v_ple_norm_g, v_w_ple_gate, v_w_ple, v_final_norm_g):
    w_loc = dict(zip(WEIGHTS, (norm_g, w_in, ssm_a_re, ssm_a_im, ssm_b_re, ssm_b_im, ssm_c_re, ssm_c_im, ssm_d, ssm_log_step,
                               ssm_w_glu, ssm_b_glu, dn_conv_w, dn_a_log, dn_dt_bias, dn_norm_g, sg_ln_g, sg_ln_b, sg_w, sg_b,
                               w_out, ple_norm_g, w_ple_gate, w_ple, final_norm_g)))
    m_loc = dict(zip(WEIGHTS, (m_norm_g, m_w_in, m_ssm_a_re, m_ssm_a_im, m_ssm_b_re, m_ssm_b_im, m_ssm_c_re, m_ssm_c_im, m_ssm_d,
                               m_ssm_log_step, m_ssm_w_glu, m_ssm_b_glu, m_dn_conv_w, m_dn_a_log, m_dn_dt_bias, m_dn_norm_g,
                               m_sg_ln_g, m_sg_ln_b, m_sg_w, m_sg_b, m_w_out, m_ple_norm_g, m_w_ple_gate, m_w_ple, m_final_norm_g)))
    v_loc = dict(zip(WEIGHTS, (v_norm_g, v_w_in, v_ssm_a_re, v_ssm_a_im, v_ssm_b_re, v_ssm_b_im, v_ssm_c_re, v_ssm_c_im, v_ssm_d,
                               v_ssm_log_step, v_ssm_w_glu, v_ssm_b_glu, v_dn_conv_w, v_dn_a_log, v_dn_dt_bias, v_dn_norm_g,
                               v_sg_ln_g, v_sg_ln_b, v_sg_w, v_sg_b, v_w_out, v_ple_norm_g, v_w_ple_gate, v_w_ple, v_final_norm_g)))

    gathered = _all_gather([w_loc[n].astype(WIRE[n]) for n in SHARDED_ORDER], "gather_weights")
    full = dict(w_loc)
    for n, g in zip(SHARDED_ORDER, gathered):
        full[n] = _assemble_w_in(g, "assemble_w_in") if n == 'w_in' else _from_gathered(g, SHARDED[n])
    full['ssm_w_glu'] = full['ssm_w_glu'].astype(F32)

    loss_part, grad_x, grads = _local_step(x, p, full, loss_target)

    dest = [_split_dw_in(grads[n], "split_dw_in") if n == 'w_in' else _to_dest_blocks(grads[n], SHARDED[n], WIRE[n])
            for n in SHARDED_ORDER]
    c = lax.axis_index("c")
    own = [lax.dynamic_index_in_dim(d, c, 0, keepdims=False) for d in dest]
    for_sibling = [lax.dynamic_index_in_dim(d, 1 - c, 0, keepdims=False) for d in dest]
    from_sibling = _pair_exchange(for_sibling, "grads_pair_exchange")
    chip_sums = [_add_pair(a, b, f"grads_pair_sum_{n}") for n, a, b in zip(SHARDED_ORDER, own, from_sibling)]
    by_chip = _chip_exchange(chip_sums, "grads_chip_exchange")
    rep_pack = _pack([grads[n] for n in REPLICATED_ORDER] + [loss_part.reshape(1)])
    (rep_recv,) = _all_gather([rep_pack], "gather_small_grads")

    outs = {k: {} for k in 'gdmv'}
    for n, gk in zip(SHARDED_ORDER, by_chip):
        for k, o in zip('gdmv', _sum_adamw(gk, w_loc[n], m_loc[n], v_loc[n], f"adamw_{n}")):
            outs[k][n] = o
    one = jnp.zeros((1,), F32)
    rep_out = _sum_adamw(rep_recv, _pack([w_loc[n] for n in REPLICATED_ORDER] + [one]),
                         _pack([m_loc[n] for n in REPLICATED_ORDER] + [one]),
                         _pack([v_loc[n] for n in REPLICATED_ORDER] + [one]), "adamw_replicated")
    rep_shapes = [w_loc[n].shape for n in REPLICATED_ORDER] + [(1,)]
    for k, rep_p in zip('gdmv', rep_out):
        outs[k].update(zip(REPLICATED_ORDER + ['loss'], _unpack(rep_p, rep_shapes)))
    loss = outs['g']['loss'].reshape(())
    return (loss, grad_x, *[outs['g'][n] for n in WEIGHTS], *[outs['d'][n] for n in WEIGHTS],
            *[outs['m'][n] for n in WEIGHTS], *[outs['v'][n] for n in WEIGHTS])
```

```python
import functools

import jax
import jax.numpy as jnp
from jax import lax
from jax.experimental import pallas as pl
from jax.experimental.pallas import tpu as pltpu

F32 = jnp.float32
BF16 = jnp.bfloat16
EPS = 1e-6

D = 1024
D_PLE = 256
D_SSM = 256
D_DN = 512
D_SG = 256
G = 16
CG = 16
NS = 64
NRE = G * NS
H = 4
DH = 128
DN_C = 128
SG_C = 128
ZW = 3456
Z_PIECES = (512, 1536, 512, 768, 128)
N_DEV = 8
LANES = 128
PACK_ROWS = 256
VMEM_LIMIT = 56 * 1024 * 1024
ELEMENTWISE_STEP_BYTES = 4 * 1024 * 1024
TB = 256
TB_DW = 512
SG_TB = 512

ADAM_LR = 0.001
ADAM_B1 = 0.9
ADAM_B2 = 0.999
ADAM_EPS = 1e-08
ADAM_WD = 0.01
ADAM_STEP = 10

MIX_HEAD_START = 3
S5_L = 128
S5_GROUP = 8
S5_SHIFTS = (1, 2, 4)

WEIGHTS = ['norm_g', 'w_in', 'ssm_a_re', 'ssm_a_im', 'ssm_b_re', 'ssm_b_im', 'ssm_c_re', 'ssm_c_im', 'ssm_d',
           'ssm_log_step', 'ssm_w_glu', 'ssm_b_glu', 'dn_conv_w', 'dn_a_log', 'dn_dt_bias', 'dn_norm_g', 'sg_ln_g',
           'sg_ln_b', 'sg_w', 'sg_b', 'w_out', 'ple_norm_g', 'w_ple_gate', 'w_ple', 'final_norm_g']
SHARDED = {'w_in': 2, 'ssm_w_glu': 1, 'dn_conv_w': 2, 'w_out': 1, 'w_ple_gate': 1, 'w_ple': 2}
SHARDED_ORDER = ['w_in', 'ssm_w_glu', 'dn_conv_w', 'w_out', 'w_ple_gate', 'w_ple']
WIRE = {'w_in': BF16, 'ssm_w_glu': BF16, 'dn_conv_w': F32, 'w_out': BF16, 'w_ple_gate': BF16, 'w_ple': BF16}
REPLICATED_ORDER = [n for n in WEIGHTS if n not in SHARDED]


def _cp(*sem):
    return pltpu.CompilerParams(dimension_semantics=sem, vmem_limit_bytes=VMEM_LIMIT)


def _dg(a, b, ca, cb, precision=None):
    return lax.dot_general(a, b, (((ca,), (cb,)), ((), ())), precision=precision, preferred_element_type=F32)


@jax.custom_vjp
def _mm(a, b):
    return _dg(a.astype(BF16), b.astype(BF16), 1, 0)


def _mm_fwd(a, b):
    return _mm(a, b), (a, b)


def _mm_bwd(res, g):
    a, b = res
    gb = g.astype(BF16)
    return _dg(gb, b.astype(BF16), 1, 1), _dg(a.astype(BF16), gb, 0, 0)


_mm.defvjp(_mm_fwd, _mm_bwd)


@jax.custom_vjp
def _mm_nt(a, b):
    return _dg(a.astype(BF16), b.astype(BF16), 1, 1)


def _mm_nt_fwd(a, b):
    return _mm_nt(a, b), (a, b)


def _mm_nt_bwd(res, g):
    a, b = res
    gb = g.astype(BF16)
    return _dg(gb, b.astype(BF16), 1, 0), _dg(gb, a.astype(BF16), 0, 0)


_mm_nt.defvjp(_mm_nt_fwd, _mm_nt_bwd)


@jax.custom_vjp
def _mm_tn(a, b):
    return _dg(a.astype(BF16), b.astype(BF16), 0, 0)


def _mm_tn_fwd(a, b):
    return _mm_tn(a, b), (a, b)


def _mm_tn_bwd(res, g):
    a, b = res
    gb = g.astype(BF16)
    return _dg(b.astype(BF16), gb, 1, 1), _dg(a.astype(BF16), gb, 1, 0)


_mm_tn.defvjp(_mm_tn_fwd, _mm_tn_bwd)


def _split(x, n):
    pieces = []
    for _ in range(n - 1):
        hi = x.astype(BF16)
        pieces.append(hi)
        x = x - hi.astype(F32)
    pieces.append(x.astype(BF16))
    return pieces


def _dg3(a, b, ca, cb):
    a_hi, a_lo = _split(a, 2)
    b_hi, b_lo = _split(b, 2)
    return _dg(a_hi, b_hi, ca, cb) + (_dg(a_hi, b_lo, ca, cb) + _dg(a_lo, b_hi, ca, cb))


@jax.custom_vjp
def _dot3(a, b):
    return _dg3(a, b, 1, 0)


def _dot3_fwd(a, b):
    return _dot3(a, b), (a, b)


def _dot3_bwd(res, g):
    a, b = res
    return _dg3(g, b, 1, 1), _dg3(a, g, 0, 0)


_dot3.defvjp(_dot3_fwd, _dot3_bwd)


def _dg_sel(x, e, cx, ce, x_first):
    eb = e.astype(BF16)
    out = None
    for piece in reversed(_split(x, 3)):
        term = _dg(piece, eb, cx, ce) if x_first else _dg(eb, piece, ce, cx)
        out = term if out is None else out + term
    return out


@jax.custom_vjp
def _sel_r(x, e):
    return _dg_sel(x, e, 1, 0, True)


def _sel_r_fwd(x, e):
    return _sel_r(x, e), e


def _sel_r_bwd(e, g):
    return _dg_sel(g, e, 1, 1, True), jnp.zeros_like(e)


_sel_r.defvjp(_sel_r_fwd, _sel_r_bwd)


@jax.custom_vjp
def _sel_l(e, x):
    return _dg_sel(x, e, 0, 1, False)


def _sel_l_fwd(e, x):
    return _sel_l(e, x), e


def _sel_l_bwd(e, g):
    return jnp.zeros_like(e), _dg_sel(g, e, 0, 0, False)


_sel_l.defvjp(_sel_l_fwd, _sel_l_bwd)


def _rms(x, g):
    return x * lax.rsqrt(jnp.mean(x * x, axis=-1, keepdims=True) + EPS) * g


def _silu(x):
    return x * jax.nn.sigmoid(x)


Z_OFFSETS = (0, 512, 2048, 2560, 3328)


def _dn_post(c):
    s = _silu(c)
    parts = []
    for j in range(12):
        xj = s[:, j * DH:(j + 1) * DH]
        if j < 8:
            xj = xj * lax.rsqrt(jnp.sum(xj * xj, axis=-1, keepdims=True) + EPS)
        if j < 4:
            xj = xj * (DH ** -0.5)
        parts.append(xj)
    return jnp.concatenate(parts, axis=1)


def _dn_conv(ext, cw_ref, rows):
    c = None
    for k in range(4):
        sh = ext if k == 3 else pltpu.roll(ext, 3 - k, 0)
        term = cw_ref[k:k + 1, :] * sh[ext.shape[0] - rows:, :]
        c = term if c is None else c + term
    return c


def _dn_prep_vjp(prev, cur, nxt, d_cur, d_nxt, cw_ref, tb):
    ext = jnp.concatenate([prev, cur, nxt], axis=0)
    shifted = [ext if k == 3 else pltpu.roll(ext, 3 - k, 0) for k in range(4)]
    c2 = None
    for k in range(4):
        term = cw_ref[k:k + 1, :] * shifted[k][8:, :]
        c2 = term if c2 is None else c2 + term
    _, vjp = jax.vjp(_dn_post, c2)
    (dc2,) = vjp(jnp.concatenate([d_cur, d_nxt], axis=0))
    dz, dcw = None, []
    for k in range(4):
        up = dc2 if k == 3 else pltpu.roll(dc2, tb + 8 - (3 - k), 0)
        term = cw_ref[k:k + 1, :] * up[:tb, :]
        dz = term if dz is None else dz + term
        dcw.append(jnp.sum(dc2[:tb, :] * shifted[k][8:8 + tb, :], axis=0, keepdims=True))
    return dz, dcw


def _in_proj_fwd(x, g, w, conv_w8, s, tb, name):
    t = x.shape[0]
    n_s = s // tb
    w3 = 3 * D_DN
    q0, q1 = Z_OFFSETS[1], Z_OFFSETS[2]

    def body(x_ref, g_ref, w_ref, cw_ref, zs_ref, zq_ref, zg_ref, zsg_ref, zab_ref, qkvn_ref, halo):
        h = _rms(x_ref[...], g_ref[...]).astype(BF16)
        zq = jnp.dot(h, w_ref[:, q0:q1], preferred_element_type=F32)
        zq_ref[...] = zq
        prev = jnp.where(pl.program_id(0) % n_s == 0, 0.0, halo[...])
        qkvn_ref[...] = _dn_post(_dn_conv(jnp.concatenate([prev, zq], axis=0), cw_ref, tb))
        halo[...] = zq[tb - 8:, :]
        zs_ref[...] = jnp.dot(h, w_ref[:, :q0], preferred_element_type=F32)
        rest = jnp.dot(h, w_ref[:, q1:], preferred_element_type=F32)
        zg_ref[...] = rest[:, :Z_PIECES[2]]
        zsg_ref[...] = rest[:, Z_PIECES[2]:Z_PIECES[2] + Z_PIECES[3]]
        zab_ref[...] = rest[:, Z_PIECES[2] + Z_PIECES[3]:]

    row = lambda i: (i, 0)
    full = lambda i: (0, 0)
    widths = Z_PIECES + (w3,)
    return pl.pallas_call(
        body, grid=(t // tb,),
        in_specs=[pl.BlockSpec((tb, D), row), pl.BlockSpec((1, D), full), pl.BlockSpec((D, ZW), full), pl.BlockSpec((8, w3), full)],
        out_specs=[pl.BlockSpec((tb, n), row) for n in widths],
        out_shape=[jax.ShapeDtypeStruct((t, n), F32) for n in widths],
        scratch_shapes=[pltpu.VMEM((8, w3), F32)],
        name=name, compiler_params=_cp("arbitrary"))(x, g, w, conv_w8)


def _in_proj_bwd_dx(x, g, w, conv_w8, dz_ssm, dz_gdn, dz_sg, dz_ab, zq, dqkvn, dx_res, s, tb, name):
    t = x.shape[0]
    n_s = s // tb
    hb = tb // 8
    w3 = 3 * D_DN
    q0, q1 = Z_OFFSETS[1], Z_OFFSETS[2]

    def body(x_ref, g_ref, w_ref, cw_ref, ds_ref, dgd_ref, dsg_ref, dab_ref, cur_ref, prev_ref, next_ref, dq_ref, dqn_ref,
             dxr_ref, dx_ref, dg_ref, dzq_ref, dcw_ref):
        i = pl.program_id(0)

        @pl.when(i == 0)
        def _():
            dg_ref[...] = jnp.zeros_like(dg_ref)
            dcw_ref[...] = jnp.zeros_like(dcw_ref)

        rest = jnp.concatenate([dgd_ref[...], dsg_ref[...], dab_ref[...]], axis=1)
        dh = _dg(ds_ref[...], w_ref[:, :q0], 1, 1) + _dg(rest, w_ref[:, q1:], 1, 1)
        first, last = i % n_s == 0, i % n_s == n_s - 1
        dzq, dcw = _dn_prep_vjp(jnp.where(first, 0.0, prev_ref[...]), cur_ref[...], jnp.where(last, 0.0, next_ref[...]),
                                dq_ref[...], jnp.where(last, 0.0, dqn_ref[...]), cw_ref, tb)
        for k in range(4):
            dcw_ref[k:k + 1, :] += dcw[k]
        dzq = dzq.astype(BF16)
        dzq_ref[...] = dzq
        dh = dh + _dg(dzq, w_ref[:, q0:q1], 1, 1)
        _, vjp = jax.vjp(_rms, x_ref[...], g_ref[...])
        dx, dg = vjp(dh)
        dx_ref[...] = dx + dxr_ref[...]
        dg_ref[...] += dg

    n_blk8 = t // 8
    row = lambda i: (i, 0)
    prv = lambda i: (jnp.maximum(i * hb - 1, 0), 0)
    nxt = lambda i: (jnp.minimum((i + 1) * hb, n_blk8 - 1), 0)
    full = lambda i: (0, 0)
    return pl.pallas_call(
        body, grid=(t // tb,),
        in_specs=[pl.BlockSpec((tb, D), row), pl.BlockSpec((1, D), full), pl.BlockSpec((D, ZW), full), pl.BlockSpec((8, w3), full)]
        + [pl.BlockSpec((tb, n), row) for n in (Z_PIECES[0], Z_PIECES[2], Z_PIECES[3], Z_PIECES[4])]
        + [pl.BlockSpec((tb, w3), row), pl.BlockSpec((8, w3), prv), pl.BlockSpec((8, w3), nxt),
           pl.BlockSpec((tb, w3), row), pl.BlockSpec((8, w3), nxt), pl.BlockSpec((tb, D), row)],
        out_specs=[pl.BlockSpec((tb, D), row), pl.BlockSpec((1, D), full), pl.BlockSpec((tb, w3), row), pl.BlockSpec((8, w3), full)],
        out_shape=[jax.ShapeDtypeStruct((t, D), F32), jax.ShapeDtypeStruct((1, D), F32), jax.ShapeDtypeStruct((t, w3), BF16),
                   jax.ShapeDtypeStruct((8, w3), F32)],
        name=name, compiler_params=_cp("arbitrary"))(x, g, w, conv_w8, dz_ssm, dz_gdn, dz_sg, dz_ab, zq, zq, zq, dqkvn, dqkvn, dx_res)


def _in_proj_bwd_dw(x, g, dzs, tb, name):
    t = x.shape[0]

    def body(x_ref, g_ref, d0, d1, d2, d3, d4, *dw_refs):
        @pl.when(pl.program_id(0) == 0)
        def _():
            for r in dw_refs:
                r[...] = jnp.zeros_like(r)

        h = _rms(x_ref[...], g_ref[...]).astype(BF16)
        for d_ref, dw_ref in zip((d0, d1, d2, d3, d4), dw_refs):
            dw_ref[...] += _dg(h, d_ref[...].astype(BF16), 0, 0)

    row = lambda i: (i, 0)
    full = lambda i: (0, 0)
    return pl.pallas_call(
        body, grid=(t // tb,),
        in_specs=[pl.BlockSpec((tb, D), row), pl.BlockSpec((1, D), full)] + [pl.BlockSpec((tb, n), row) for n in Z_PIECES],
        out_specs=[pl.BlockSpec((D, n), full) for n in Z_PIECES],
        out_shape=[jax.ShapeDtypeStruct((D, n), F32) for n in Z_PIECES],
        name=name, compiler_params=_cp("arbitrary"))(x, g, *dzs)


def _lam_pow(a_re, a_im, log_step, k):
    step = jnp.exp(log_step)[:, None]
    mag = jnp.exp(k * a_re * step)
    ang = k * a_im * step
    return mag * jnp.cos(ang), mag * jnp.sin(ang)


def _s5_powers(a_re, a_im, log_step):
    def table(ks):
        re, im = _lam_pow(a_re, a_im, log_step, jnp.asarray(ks, F32)[:, None, None])
        return jnp.concatenate([re.reshape(len(ks), NRE), im.reshape(len(ks), NRE)], axis=-1)

    ld = table(S5_SHIFTS).reshape(len(S5_SHIFTS), 1, 2 * NRE)
    return ld, table(range(1, S5_GROUP + 1)), table(range(S5_GROUP, 0, -1))


def _s5_tables(a_re, a_im, b_re, b_im, c_re, c_im, d_skip, log_step):
    lam_re, lam_im = _lam_pow(a_re, a_im, log_step, 1.0)
    den = a_re * a_re + a_im * a_im
    nr, ni = lam_re - 1.0, lam_im
    f_re = (nr * a_re + ni * a_im) / den
    f_im = (ni * a_re - nr * a_im) / den
    bbar_re = f_re[..., None] * b_re - f_im[..., None] * b_im
    bbar_im = f_re[..., None] * b_im + f_im[..., None] * b_re
    eye = jnp.eye(G, dtype=F32)

    def blk_b(bb):
        return (jnp.transpose(bb, (0, 2, 1))[:, :, None, :] * eye[:, None, :, None]).reshape(D_SSM, NRE)

    def blk_c(cc):
        return (jnp.transpose(cc, (0, 2, 1))[:, :, None, :] * eye[:, None, :, None]).reshape(NRE, D_SSM)

    b_blk = jnp.concatenate([blk_b(bbar_re), blk_b(bbar_im)], axis=1)
    c_blk = jnp.concatenate([blk_c(c_re), -blk_c(c_im)], axis=0)
    lam = jnp.concatenate([lam_re.reshape(1, NRE), lam_im.reshape(1, NRE)], axis=-1)
    return b_blk, c_blk, lam, d_skip.reshape(1, D_SSM)


def _group_shift(x, d, up=False):
    r = lax.broadcasted_iota(jnp.int32, x.shape, 0) & (S5_GROUP - 1)
    if up:
        return jnp.where(r < S5_GROUP - d, pltpu.roll(x, x.shape[0] - d, 0), 0.0)
    return jnp.where(r >= d, pltpu.roll(x, d, 0), 0.0)


def _s5_scan_steps(hr, hi, cr, ci, lds, lp):
    for ld, d in zip(lds, S5_SHIFTS):
        lr, li = ld[:, :NRE], ld[:, NRE:]
        sr, si = _group_shift(hr, d), _group_shift(hi, d)
        hr, hi = hr + lr * sr - li * si, hi + lr * si + li * sr
        yield
    pr, pi = lp[:, :NRE], lp[:, NRE:]
    rows_r, rows_i = [], []
    for r in range(hr.shape[0] // S5_GROUP):
        br, bi = hr[r * S5_GROUP:(r + 1) * S5_GROUP], hi[r * S5_GROUP:(r + 1) * S5_GROUP]
        br, bi = br + pr * cr - pi * ci, bi + pr * ci + pi * cr
        cr, ci = br[S5_GROUP - 1:S5_GROUP], bi[S5_GROUP - 1:S5_GROUP]
        rows_r.append(br)
        rows_i.append(bi)
        if r % 2:
            yield
    return jnp.concatenate(rows_r, axis=0), jnp.concatenate(rows_i, axis=0)


@jax.custom_vjp
def _known_scan(xr, xi, cr, ci, lam, lds, lp_rev, hr, hi):
    return hr, hi


def _known_scan_fwd(xr, xi, cr, ci, lam, lds, lp_rev, hr, hi):
    return (hr, hi), (cr, ci, lam, lds, lp_rev, hr, hi)


def _known_scan_bwd(res, cts):
    cr, ci, lam, lds, lp_rev, hr, hi = res
    ar, ai = cts
    for ld, d in zip(lds, S5_SHIFTS):
        lr, li = ld[:, :NRE], ld[:, NRE:]
        sr, si = _group_shift(ar, d, up=True), _group_shift(ai, d, up=True)
        ar, ai = ar + lr * sr + li * si, ai + lr * si - li * sr
    qr, qi = lp_rev[:, :NRE], lp_rev[:, NRE:]
    nr, ni = jnp.zeros_like(cr), jnp.zeros_like(ci)
    rows_r, rows_i = [], []
    for r in reversed(range(hr.shape[0] // S5_GROUP)):
        br, bi = ar[r * S5_GROUP:(r + 1) * S5_GROUP], ai[r * S5_GROUP:(r + 1) * S5_GROUP]
        br, bi = br + qr * nr + qi * ni, bi + qr * ni - qi * nr
        nr, ni = br[0:1], bi[0:1]
        rows_r.insert(0, br)
        rows_i.insert(0, bi)
    ar, ai = jnp.concatenate(rows_r, axis=0), jnp.concatenate(rows_i, axis=0)
    lr, li = lam[:, :NRE], lam[:, NRE:]
    dcr, dci = lr * nr + li * ni, lr * ni - li * nr
    first = lax.broadcasted_iota(jnp.int32, hr.shape, 0) == 0
    pr = jnp.where(first, cr, pltpu.roll(hr, 1, 0))
    pi = jnp.where(first, ci, pltpu.roll(hi, 1, 0))
    dlam = jnp.concatenate([jnp.sum(ar * pr + ai * pi, axis=0, keepdims=True),
                            jnp.sum(ai * pr - ar * pi, axis=0, keepdims=True)], axis=1)
    return (ar, ai, dcr, dci, dlam, [jnp.zeros_like(ld) for ld in lds], jnp.zeros_like(lp_rev),
            jnp.zeros_like(hr), jnp.zeros_like(hi))


_known_scan.defvjp(_known_scan_fwd, _known_scan_bwd)


def _interleave(short, long, head_start=0):
    gens = list(short) + list(long)
    results = [None] * len(gens)

    def advance(live):
        still = []
        for idx, gen in live:
            try:
                next(gen)
                still.append((idx, gen))
            except StopIteration as done:
                results[idx] = done.value
        return still

    live_short = advance(list(enumerate(gens))[:len(short)])
    live_long = list(enumerate(gens))[len(short):]
    for _ in range(head_start):
        live_long = advance(live_long)
    live = live_short + live_long
    while live:
        live = advance(live)
    return results[:len(short)], results[len(short):]


def _s5_chunk_gen(u, gate, cr, ci, b_blk, c_blk, lam, dv, wglu, bglu, lds, lp, lp_rev, known_h=None):
    bu = _mm(u, b_blk)
    xr, xi = bu[:, :NRE], bu[:, NRE:]
    yield
    if known_h is None:
        hr, hi = yield from _s5_scan_steps(xr, xi, cr, ci, lds, lp)
    else:
        hr, hi = _known_scan(xr, xi, cr, ci, lam, lds, lp_rev, *known_h)
    y = _mm(jnp.concatenate([hr, hi], axis=1), c_blk) + dv * u
    yield
    y = jax.nn.gelu(y)
    y = y * jax.nn.sigmoid(_mm(y, wglu) + bglu)
    return y * _silu(gate), hr, hi


S5_PAR_SHAPES = [(D_SSM, 2 * NRE), (2 * NRE, D_SSM), (1, 2 * NRE), (1, D_SSM), (D_SSM, D_SSM), (1, D_SSM)]
S5_CONST_SHAPES = [(len(S5_SHIFTS), 1, 2 * NRE), (S5_GROUP, 2 * NRE), (S5_GROUP, 2 * NRE)]
DN_PAR_SHAPES = [(1, D_DN), (1, D_DN), (1, DH)]


def _mix_specs(bl, n_c, rev):
    def chunk(i):
        return n_c - 1 - i if rev else i

    def tok(n):
        return pl.BlockSpec((bl, DN_C, n), lambda i: (0, chunk(i), 0))

    def per_chunk(shape):
        return pl.BlockSpec((bl, 1, *shape), lambda i: (0, chunk(i)) + (0,) * len(shape))

    def whole(shape):
        return pl.BlockSpec(shape, lambda i: (0,) * len(shape))

    return tok, per_chunk, whole


def _unit_lower_inverse_steps(ms):
    c_len = ms[0].shape[0]
    eye = lax.broadcasted_iota(jnp.int32, (c_len, c_len), 0) == lax.broadcasted_iota(jnp.int32, (c_len, c_len), 1)
    ident = jnp.where(eye, 1.0, 0.0)
    ps = ms
    tinvs = [ident - m for m in ms]
    for _ in range(c_len.bit_length() - 2):
        ps = [_dg3(p, p, 1, 0) for p in ps]
        yield
        tinvs = [t + _dg3(t, p, 1, 0) for t, p in zip(tinvs, ps)]
        yield
    return tinvs


@jax.custom_vjp
def _known_inverses(ms, tinvs):
    return tinvs


def _known_inverses_fwd(ms, tinvs):
    return tinvs, tinvs


def _known_inverses_bwd(tinvs, gs):
    return [-_dg3(_dg3(t, g, 0, 0), t, 1, 1) for t, g in zip(tinvs, gs)], [jnp.zeros_like(t) for t in tinvs]


_known_inverses.defvjp(_known_inverses_fwd, _known_inverses_bwd)


def _dn_chunk_gen(qkv, zab, zg, states, alog_e, dt_e, ng, known_tinvs=None):
    c_len = DN_C
    r = lax.broadcasted_iota(jnp.int32, (c_len, c_len), 0)
    c = lax.broadcasted_iota(jnp.int32, (c_len, c_len), 1)
    causal, strict = r >= c, r > c
    tril = jnp.where(causal, 1.0, 0.0)
    rr = lax.broadcasted_iota(jnp.int32, (LANES, D_DN), 0)
    cc = lax.broadcasted_iota(jnp.int32, (LANES, D_DN), 1)
    e_a = jnp.where((cc >= rr * DH) & (cc < rr * DH + DH) & (rr < H), 1.0, 0.0)
    e_b = jnp.where((cc >= (rr - H) * DH) & (cc < (rr - H) * DH + DH) & (rr >= H) & (rr < 2 * H), 1.0, 0.0)
    a_e = _sel_r(zab, e_a)
    b_e = _sel_r(zab, e_b)
    beta = jax.nn.sigmoid(b_e)
    g = -jnp.exp(alog_e) * jax.nn.softplus(a_e + dt_e)
    yield
    gc = _sel_l(tril, g)
    glast = jnp.sum(g, axis=0, keepdims=True)
    eg = jnp.exp(gc)
    ekd = jnp.exp(glast - gc)
    dl = jnp.exp(glast)
    yield
    heads = range(H)
    sls = [slice(h * DH, (h + 1) * DH) for h in heads]
    qs = [qkv[:, h * DH:(h + 1) * DH] for h in heads]
    ks = [qkv[:, D_DN + h * DH:D_DN + (h + 1) * DH] for h in heads]
    vs = [qkv[:, 2 * D_DN + h * DH:2 * D_DN + (h + 1) * DH] for h in heads]
    ccols = [gc[:, sl] for sl in sls]
    decs = [jnp.where(causal, jnp.exp(jnp.where(causal, cl - jnp.transpose(cl), 0.0)), 0.0) for cl in ccols]
    kbs = [k * beta[:, sl] for k, sl in zip(ks, sls)]
    ms = [jnp.where(strict, _mm_nt(kb, k) * dec, 0.0) for kb, k, dec in zip(kbs, ks, decs)]
    yield
    if known_tinvs is None:
        tinvs = yield from _unit_lower_inverse_steps(ms)
    else:
        tinvs = _known_inverses(ms, list(known_tinvs))
    sols = [_dot3(t, jnp.concatenate([v * beta[:, sl], kb * eg[:, sl]], axis=1))
            for t, v, kb, sl in zip(tinvs, vs, kbs, sls)]
    yield
    atts = [_mm_nt(q, k) * dec for q, k, dec in zip(qs, ks, decs)]
    vnews = [sol[:, :DH] - _mm(sol[:, DH:], st) for sol, st in zip(sols, states)]
    yield
    os_ = [_mm(q * eg[:, sl], st) + _mm(att, vn) for q, sl, st, att, vn in zip(qs, sls, states, atts, vnews)]
    yield
    new_states = [st * dl[:, sl] + _mm_tn(k * ekd[:, sl], vn) for st, sl, k, vn in zip(states, sls, ks, vnews)]
    yield
    ys = [_rms(o, ng) * _silu(zg[:, sl]) for o, sl in zip(os_, sls)]
    return jnp.concatenate(ys, axis=1), new_states, tinvs


def _mix_fwd(zs, s5_par, s5_const, qkv, zab, zg, dn_par, bl, s, name):
    assert S5_L == DN_C
    n_c = s // DN_C
    nd = len(S5_SHIFTS)
    tok, per_chunk, whole = _mix_specs(bl, n_c, False)

    def body(z_ref, b_ref, c_ref, lam_ref, dv_ref, wg_ref, bg_ref, ld_ref, lp_ref, lpr_ref,
             q_ref, ab_ref, zg_ref, al_ref, dt_ref, ng_ref,
             ys_ref, car_ref, h_ref, yd_ref, st_ref, ti_ref, cs, ssc):
        @pl.when(pl.program_id(0) == 0)
        def _():
            cs[...] = jnp.zeros_like(cs)
            ssc[...] = jnp.zeros_like(ssc)

        lds = [ld_ref[k] for k in range(nd)]
        s5_gens, dn_gens = [], []
        for e in range(bl):
            c = cs[e]
            car_ref[e, 0] = c
            sts = [ssc[e, h] for h in range(H)]
            for h in range(H):
                st_ref[e, 0, h] = sts[h]
            z = z_ref[e]
            s5_gens.append(_s5_chunk_gen(z[:, :D_SSM], z[:, D_SSM:], c[:, :NRE], c[:, NRE:], b_ref[...], c_ref[...], lam_ref[...],
                                         dv_ref[...], wg_ref[...], bg_ref[...], lds, lp_ref[...], lpr_ref[...]))
            dn_gens.append(_dn_chunk_gen(q_ref[e], ab_ref[e], zg_ref[e], sts, al_ref[...], dt_ref[...], ng_ref[...]))
        s5_outs, dn_outs = _interleave(s5_gens, dn_gens, head_start=MIX_HEAD_START)
        for e in range(bl):
            y_s, hr, hi = s5_outs[e]
            y_d, new_sts, tinvs = dn_outs[e]
            ys_ref[e] = y_s
            h_ref[e, :, :NRE] = hr
            h_ref[e, :, NRE:] = hi
            cs[e, :, :NRE] = hr[S5_L - 1:S5_L]
            cs[e, :, NRE:] = hi[S5_L - 1:S5_L]
            yd_ref[e] = y_d
            for h in range(H):
                ssc[e, h] = new_sts[h]
                ti_ref[e, 0, h] = tinvs[h]

    head_mats = jax.ShapeDtypeStruct((bl, n_c, H, DH, DH), F32)
    return pl.pallas_call(
        body, grid=(n_c,),
        in_specs=[tok(2 * D_SSM)] + [whole(sh) for sh in S5_PAR_SHAPES + S5_CONST_SHAPES]
        + [tok(3 * D_DN), tok(LANES), tok(D_DN)] + [whole(sh) for sh in DN_PAR_SHAPES],
        out_specs=[tok(D_SSM), per_chunk((1, 2 * NRE)), tok(2 * NRE), tok(D_DN), per_chunk((H, DH, DH)), per_chunk((H, DH, DH))],
        out_shape=[jax.ShapeDtypeStruct((bl, s, D_SSM), F32), jax.ShapeDtypeStruct((bl, n_c, 1, 2 * NRE), F32),
                   jax.ShapeDtypeStruct((bl, s, 2 * NRE), F32), jax.ShapeDtypeStruct((bl, s, D_DN), F32), head_mats, head_mats],
        scratch_shapes=[pltpu.VMEM((bl, 1, 2 * NRE), F32), pltpu.VMEM((bl, H, DH, DH), F32)],
        name=name, compiler_params=_cp("arbitrary"))(zs, *s5_par, *s5_const, qkv, zab, zg, *dn_par)


def _mix_bwd(zs, carries, h_all, dy_s, s5_par, s5_const, qkv, zab, zg, states, tinvs, dy_d, dn_par, bl, s, name):
    n_c = s // DN_C
    nd = len(S5_SHIFTS)
    tok, per_chunk, whole = _mix_specs(bl, n_c, True)

    def both(examples, s5_tabs, s5_consts, dn_tabs):
        s5_gens = [_s5_chunk_gen(u, gate, cr, ci, *s5_tabs, *s5_consts, known_h=(hr, hi))
                   for u, gate, cr, ci, hr, hi, _, _, _, _, _ in examples]
        dn_gens = [_dn_chunk_gen(q, ab, zgate, sts, *dn_tabs, known_tinvs=known)
                   for _, _, _, _, _, _, q, ab, zgate, sts, known in examples]
        s5_outs, dn_outs = _interleave(s5_gens, dn_gens, head_start=MIX_HEAD_START)
        return [(y_s, hr[S5_L - 1:S5_L], hi[S5_L - 1:S5_L], y_d, new_sts)
                for (y_s, hr, hi), (y_d, new_sts, _) in zip(s5_outs, dn_outs)]

    def body(z_ref, car_ref, h_ref, dys_ref, b_ref, c_ref, lam_ref, dv_ref, wg_ref, bg_ref, ld_ref, lp_ref, lpr_ref,
             q_ref, ab_ref, zg_ref, st_ref, ti_ref, dyd_ref, al_ref, dt_ref, ng_ref,
             dz_ref, db_ref, dc_ref, dlam_ref, ddv_ref, dwg_ref, dbg_ref,
             dq_ref, dab_ref, dzg_ref, dal_ref, ddt_ref, dng_ref, dcs, dsc):
        accs = (db_ref, dc_ref, dlam_ref, ddv_ref, dwg_ref, dbg_ref, dal_ref, ddt_ref, dng_ref)

        @pl.when(pl.program_id(0) == 0)
        def _():
            for r in accs + (dcs, dsc):
                r[...] = jnp.zeros_like(r)

        examples = []
        for e in range(bl):
            z = z_ref[e]
            c = car_ref[e, 0]
            examples.append((z[:, :D_SSM], z[:, D_SSM:], c[:, :NRE], c[:, NRE:], h_ref[e, :, :NRE], h_ref[e, :, NRE:],
                             q_ref[e], ab_ref[e], zg_ref[e], [st_ref[e, 0, h] for h in range(H)],
                             [ti_ref[e, 0, h] for h in range(H)]))
        _, vjp = jax.vjp(both, examples,
                         (b_ref[...], c_ref[...], lam_ref[...], dv_ref[...], wg_ref[...], bg_ref[...]),
                         ([ld_ref[k] for k in range(nd)], lp_ref[...], lpr_ref[...]),
                         (al_ref[...], dt_ref[...], ng_ref[...]))
        cts = []
        for e in range(bl):
            dc = dcs[e]
            cts.append((dys_ref[e], dc[:, :NRE], dc[:, NRE:], dyd_ref[e], [dsc[e, h] for h in range(H)]))
        d_examples, d_s5, _, d_dn = vjp(cts)
        for e in range(bl):
            du, dgate, dcr, dci, _, _, dq, dab, dzg, dsts, _ = d_examples[e]
            dz_ref[e] = jnp.concatenate([du, dgate], axis=1).astype(BF16)
            dcs[e, :, :NRE] = dcr
            dcs[e, :, NRE:] = dci
            dq_ref[e] = dq
            dab_ref[e] = dab.astype(BF16)
            dzg_ref[e] = dzg.astype(BF16)
            for h in range(H):
                dsc[e, h] = dsts[h]
        for r, ct in zip(accs, (*d_s5, *d_dn)):
            r[...] += ct

    head_mats = per_chunk((H, DH, DH))
    outs = pl.pallas_call(
        body, grid=(n_c,),
        in_specs=[tok(2 * D_SSM), per_chunk((1, 2 * NRE)), tok(2 * NRE), tok(D_SSM)]
        + [whole(sh) for sh in S5_PAR_SHAPES + S5_CONST_SHAPES]
        + [tok(3 * D_DN), tok(LANES), tok(D_DN), head_mats, head_mats, tok(D_DN)] + [whole(sh) for sh in DN_PAR_SHAPES],
        out_specs=[tok(2 * D_SSM)] + [whole(sh) for sh in S5_PAR_SHAPES]
        + [tok(3 * D_DN), tok(LANES), tok(D_DN)] + [whole(sh) for sh in DN_PAR_SHAPES],
        out_shape=[jax.ShapeDtypeStruct((bl, s, 2 * D_SSM), BF16)] + [jax.ShapeDtypeStruct(sh, F32) for sh in S5_PAR_SHAPES]
        + [jax.ShapeDtypeStruct((bl, s, 3 * D_DN), F32), jax.ShapeDtypeStruct((bl, s, LANES), BF16),
           jax.ShapeDtypeStruct((bl, s, D_DN), BF16)]
        + [jax.ShapeDtypeStruct(sh, F32) for sh in DN_PAR_SHAPES],
        scratch_shapes=[pltpu.VMEM((bl, 1, 2 * NRE), F32), pltpu.VMEM((bl, H, DH, DH), F32)],
        name=name, compiler_params=_cp("arbitrary"))(
            zs, carries, h_all, dy_s, *s5_par, *s5_const, qkv, zab, zg, states, tinvs, dy_d, *dn_par)
    return outs[:7], outs[7:]


def _sg_fn(n_chunk):
    def f(z, lng, lnb, w, bsp_t):
        u = jax.nn.gelu(z[:, :D_SG])
        v = jax.nn.gelu(z[:, D_SG:2 * D_SG])
        gate = z[:, 2 * D_SG:]
        xc = v - jnp.mean(v, axis=-1, keepdims=True)
        vn = xc * lax.rsqrt(jnp.mean(xc * xc, axis=-1, keepdims=True) + EPS) * lng + lnb
        r = lax.broadcasted_iota(jnp.int32, (SG_C, SG_C), 0)
        c = lax.broadcasted_iota(jnp.int32, (SG_C, SG_C), 1)
        causal = r >= c
        first_half = c < SG_C // 2
        rr = lax.broadcasted_iota(jnp.int32, (LANES, D_SG), 0)
        cc = lax.broadcasted_iota(jnp.int32, (LANES, D_SG), 1)
        expand = jnp.where((cc >= rr * 64) & (cc < rr * 64 + 64) & (rr < 4), 1.0, 0.0)
        bias = _sel_r(bsp_t, expand)
        wm = [jnp.where(causal, w[h], 0.0) for h in range(4)]
        rows = []
        for ci in range(n_chunk):
            vc = vn[ci * SG_C:(ci + 1) * SG_C]
            pairs = []
            for pr in range(2):
                vp = vc[:, pr * LANES:(pr + 1) * LANES]
                pairs.append(jnp.where(first_half, _mm(wm[2 * pr], vp), _mm(wm[2 * pr + 1], vp)))
            rows.append(jnp.concatenate(pairs, axis=1) + bias)
        sp = jnp.concatenate(rows, axis=0) if n_chunk > 1 else rows[0]
        return u * sp * _silu(gate)

    return f


def _sg_specs():
    full = lambda i: (0, 0)
    full3 = lambda i: (0, 0, 0)
    par = [pl.BlockSpec((1, D_SG), full), pl.BlockSpec((1, D_SG), full), pl.BlockSpec((4, SG_C, SG_C), full3),
           pl.BlockSpec((SG_C, LANES), full)]
    par_shapes = [(1, D_SG), (1, D_SG), (4, SG_C, SG_C), (SG_C, LANES)]
    return par, par_shapes


def _sg_fwd(zsg, params, tb, name):
    t = zsg.shape[0]
    f = _sg_fn(tb // SG_C)
    par, _ = _sg_specs()

    def body(z_ref, g_ref, b_ref, w_ref, bs_ref, y_ref):
        y_ref[...] = f(z_ref[...], g_ref[...], b_ref[...], w_ref[...], bs_ref[...])

    row = lambda i: (i, 0)
    return pl.pallas_call(
        body, grid=(t // tb,), in_specs=[pl.BlockSpec((tb, 3 * D_SG), row)] + par,
        out_specs=pl.BlockSpec((tb, D_SG), row), out_shape=jax.ShapeDtypeStruct((t, D_SG), F32),
        name=name, compiler_params=_cp("parallel"))(zsg, *params)


def _sg_bwd(zsg, dy, params, tb, name):
    t = zsg.shape[0]
    f = _sg_fn(tb // SG_C)
    par, par_shapes = _sg_specs()

    def body(z_ref, dy_ref, g_ref, b_ref, w_ref, bs_ref, dz_ref, dg_ref, db_ref, dw_ref, dbs_ref):
        accs = (dg_ref, db_ref, dw_ref, dbs_ref)

        @pl.when(pl.program_id(0) == 0)
        def _():
            for r in accs:
                r[...] = jnp.zeros_like(r)

        _, vjp = jax.vjp(f, z_ref[...], g_ref[...], b_ref[...], w_ref[...], bs_ref[...])
        cts = vjp(dy_ref[...])
        dz_ref[...] = cts[0].astype(BF16)
        for r, ct in zip(accs, cts[1:]):
            r[...] += ct

    row = lambda i: (i, 0)
    return pl.pallas_call(
        body, grid=(t // tb,), in_specs=[pl.BlockSpec((tb, 3 * D_SG), row), pl.BlockSpec((tb, D_SG), row)] + par,
        out_specs=[pl.BlockSpec((tb, 3 * D_SG), row)] + par,
        out_shape=[jax.ShapeDtypeStruct((t, 3 * D_SG), BF16)] + [jax.ShapeDtypeStruct(sh, F32) for sh in par_shapes],
        name=name, compiler_params=_cp("arbitrary"))(zsg, dy, *params)


def _out_fwd(x, ys, p, w_out, pg, w_gate, w_ple, tb, name):
    t = x.shape[0]

    def body(x_ref, y0, y1, y2, p_ref, wo_ref, pg_ref, wg_ref, wp_ref, o_ref, x1_ref, gate_ref):
        y = jnp.concatenate([y0[...], y1[...], y2[...]], axis=1).astype(BF16)
        x1 = x_ref[...] + jnp.dot(y, wo_ref[...], preferred_element_type=F32)
        hn = _rms(x1, pg_ref[...]).astype(BF16)
        gate = jax.nn.sigmoid(jnp.dot(hn, wg_ref[...], preferred_element_type=F32))
        pp = jnp.dot(p_ref[...].astype(BF16), wp_ref[...], preferred_element_type=F32)
        o_ref[...] = x1 + gate * pp
        x1_ref[...] = x1
        gate_ref[...] = gate

    row = lambda i: (i, 0)
    full = lambda i: (0, 0)
    return pl.pallas_call(
        body, grid=(t // tb,),
        in_specs=[pl.BlockSpec((tb, D), row), pl.BlockSpec((tb, D_SSM), row), pl.BlockSpec((tb, D_DN), row),
                  pl.BlockSpec((tb, D_SG), row), pl.BlockSpec((tb, D_PLE), row), pl.BlockSpec((D, D), full),
                  pl.BlockSpec((1, D), full), pl.BlockSpec((D, D), full), pl.BlockSpec((D_PLE, D), full)],
        out_specs=[pl.BlockSpec((tb, D), row)] * 3, out_shape=[jax.ShapeDtypeStruct((t, D), F32)] * 3,
        name=name, compiler_params=_cp("parallel"))(x, *ys, p, w_out, pg, w_gate, w_ple)


def _out_bwd(x1, gate, ys, p, dx2, w_out, pg, w_gate, w_ple, tb, name):
    t = x1.shape[0]

    def body(x1_ref, gate_ref, y0, y1, y2, p_ref, d_ref, wo_ref, pg_ref, wg_ref, wp_ref,
             dx_ref, dy0, dy1, dy2, dwo_ref, dpg_ref, dwg_ref, dwp_ref):
        accs = (dwo_ref, dpg_ref, dwg_ref, dwp_ref)

        @pl.when(pl.program_id(0) == 0)
        def _():
            for r in accs:
                r[...] = jnp.zeros_like(r)

        y = jnp.concatenate([y0[...], y1[...], y2[...]], axis=1).astype(BF16)
        hn, rms_vjp = jax.vjp(_rms, x1_ref[...], pg_ref[...])
        hb = hn.astype(BF16)
        gate = gate_ref[...]
        pb = p_ref[...].astype(BF16)
        pp = jnp.dot(pb, wp_ref[...], preferred_element_type=F32)
        d2 = d_ref[...]
        dpp = (d2 * gate).astype(BF16)
        dlog = (d2 * pp * gate * (1.0 - gate)).astype(BF16)
        dwp_ref[...] += _dg(pb, dpp, 0, 0)
        dwg_ref[...] += _dg(hb, dlog, 0, 0)
        dx1_n, dpg = rms_vjp(_dg(dlog, wg_ref[...], 1, 1))
        dpg_ref[...] += dpg
        dx1 = d2 + dx1_n
        dx_ref[...] = dx1
        db = dx1.astype(BF16)
        dwo_ref[...] += _dg(y, db, 0, 0)
        dy = _dg(db, wo_ref[...], 1, 1)
        dy0[...] = dy[:, :D_SSM]
        dy1[...] = dy[:, D_SSM:D_SSM + D_DN]
        dy2[...] = dy[:, D_SSM + D_DN:]

    row = lambda i: (i, 0)
    full = lambda i: (0, 0)
    acts = [pl.BlockSpec((tb, D), row), pl.BlockSpec((tb, D_SSM), row), pl.BlockSpec((tb, D_DN), row), pl.BlockSpec((tb, D_SG), row)]
    wts = [pl.BlockSpec((D, D), full), pl.BlockSpec((1, D), full), pl.BlockSpec((D, D), full), pl.BlockSpec((D_PLE, D), full)]
    return pl.pallas_call(
        body, grid=(t // tb,),
        in_specs=[pl.BlockSpec((tb, D), row)] + acts + [pl.BlockSpec((tb, D_PLE), row), pl.BlockSpec((tb, D), row)] + wts,
        out_specs=acts + wts,
        out_shape=[jax.ShapeDtypeStruct((t, n), F32) for n in (D, D_SSM, D_DN, D_SG)]
        + [jax.ShapeDtypeStruct(sh, F32) for sh in ((D, D), (1, D), (D, D), (D_PLE, D))],
        name=name, compiler_params=_cp("arbitrary"))(x1, gate, *ys, p, dx2, w_out, pg, w_gate, w_ple)


def _loss_head(x, fg, target, tb, name):
    t = x.shape[0]

    def body(x_ref, g_ref, t_ref, dx_ref, dg_ref, loss_ref):
        @pl.when(pl.program_id(0) == 0)
        def _():
            dg_ref[...] = jnp.zeros_like(dg_ref)
            loss_ref[...] = jnp.zeros_like(loss_ref)

        y, vjp = jax.vjp(_rms, x_ref[...], g_ref[...])
        err = y - t_ref[...]
        loss_ref[...] += jnp.zeros_like(loss_ref) + 0.5 * jnp.sum(err * err) / D
        dx, dg = vjp(err / D)
        dx_ref[...] = dx
        dg_ref[...] += dg

    row = lambda i: (i, 0)
    full = lambda i: (0, 0)
    return pl.pallas_call(
        body, grid=(t // tb,),
        in_specs=[pl.BlockSpec((tb, D), row), pl.BlockSpec((1, D), full), pl.BlockSpec((tb, D), row)],
        out_specs=[pl.BlockSpec((tb, D), row), pl.BlockSpec((1, D), full), pl.BlockSpec((1, LANES), full)],
        out_shape=[jax.ShapeDtypeStruct((t, D), F32), jax.ShapeDtypeStruct((1, D), F32), jax.ShapeDtypeStruct((1, LANES), F32)],
        name=name, compiler_params=_cp("arbitrary"))(x, fg, target)


def _hbm_specs(n):
    return [pl.BlockSpec(memory_space=pl.ANY)] * n


def _all_gather(blocks, name):
    n = len(blocks)

    def body(*refs):
        ins, outs = refs[:n], refs[n:2 * n]
        send_sems, recv_sems, local_sems = refs[2 * n:]
        x, y, c = lax.axis_index("x"), lax.axis_index("y"), lax.axis_index("c")
        me, sibling = (x, y, c), (x, y, 1 - c)
        chips = [(1 - x, y), (x, 1 - y), (1 - x, 1 - y)]

        def slot(a, px, py, pc):
            return outs[a].at[4 * px + 2 * py + pc]

        def copy(a, k, blk, to, src=None):
            return pltpu.make_async_remote_copy(
                src_ref=slot(a, *blk) if src is None else src, dst_ref=slot(a, *blk),
                send_sem=send_sems.at[7 * a + k], recv_sem=recv_sems.at[7 * a + k],
                device_id=to, device_id_type=pl.DeviceIdType.MESH)

        mines = [pltpu.make_async_copy(ins[a], slot(a, *me), local_sems.at[a]) for a in range(n)]
        for cp in mines:
            cp.start()
        first = []
        for a in range(n):
            first.append(copy(a, 0, me, sibling, src=ins[a]))
            first += [copy(a, 1 + j, me, (*chip, c), src=ins[a]) for j, chip in enumerate(chips)]
        for cp in first:
            cp.start()
        passed = []
        for j, chip in enumerate(chips):
            for a in range(n):
                copy(a, 1 + j, (*chip, c), me).wait_recv()
                onward = copy(a, 4 + j, (*chip, c), sibling)
                onward.start()
                passed.append(onward)
        for a in range(n):
            copy(a, 0, sibling, me).wait_recv()
        for j, chip in enumerate(chips):
            for a in range(n):
                copy(a, 4 + j, (*chip, 1 - c), me).wait_recv()
        for cp in first + passed:
            cp.wait_send()
        for cp in mines:
            cp.wait()

    return pl.pallas_call(
        body, out_shape=[jax.ShapeDtypeStruct((N_DEV, *b.shape), b.dtype) for b in blocks],
        in_specs=_hbm_specs(n), out_specs=_hbm_specs(n),
        scratch_shapes=[pltpu.SemaphoreType.DMA((7 * n,)), pltpu.SemaphoreType.DMA((7 * n,)), pltpu.SemaphoreType.DMA((n,))],
        name=name)(*blocks)


def _pair_exchange(gs, name):
    n = len(gs)

    def body(*refs):
        ins, recvs = refs[:n], refs[n:2 * n]
        send_sems, recv_sems = refs[2 * n:]
        x, y, c = lax.axis_index("x"), lax.axis_index("y"), lax.axis_index("c")
        remote = [pltpu.make_async_remote_copy(
            src_ref=ins[a], dst_ref=recvs[a], send_sem=send_sems.at[a], recv_sem=recv_sems.at[a],
            device_id=(x, y, 1 - c), device_id_type=pl.DeviceIdType.MESH) for a in range(n)]
        for cp in remote:
            cp.start()
        for cp in remote:
            cp.wait_send()
            cp.wait_recv()

    return pl.pallas_call(
        body, out_shape=[jax.ShapeDtypeStruct(g.shape, g.dtype) for g in gs], in_specs=_hbm_specs(n), out_specs=_hbm_specs(n),
        scratch_shapes=[pltpu.SemaphoreType.DMA((n,)), pltpu.SemaphoreType.DMA((n,))],
        name=name)(*gs)


def _chip_exchange(ps, name):
    n = len(ps)

    def body(*refs):
        ins, outs = refs[:n], refs[n:2 * n]
        send_sems, recv_sems, local_sems = refs[2 * n:]
        x, y, c = lax.axis_index("x"), lax.axis_index("y"), lax.axis_index("c")
        my_chip = 2 * x + y
        local = [pltpu.make_async_copy(ins[a].at[my_chip], outs[a].at[my_chip], local_sems.at[a]) for a in range(n)]
        remote = []
        for j in range(1, 4):
            px = 1 - x if j & 2 else x
            py = 1 - y if j & 1 else y
            for a in range(n):
                remote.append(pltpu.make_async_remote_copy(
                    src_ref=ins[a].at[2 * px + py], dst_ref=outs[a].at[my_chip],
                    send_sem=send_sems.at[3 * a + j - 1], recv_sem=recv_sems.at[3 * a + j - 1],
                    device_id=(px, py, c), device_id_type=pl.DeviceIdType.MESH))
        for cp in local + remote:
            cp.start()
        for cp in remote:
            cp.wait_send()
            cp.wait_recv()
        for cp in local:
            cp.wait()

    return pl.pallas_call(
        body, out_shape=[jax.ShapeDtypeStruct(q.shape, q.dtype) for q in ps], in_specs=_hbm_specs(n), out_specs=_hbm_specs(n),
        scratch_shapes=[pltpu.SemaphoreType.DMA((3 * n,)), pltpu.SemaphoreType.DMA((3 * n,)), pltpu.SemaphoreType.DMA((n,))],
        name=name)(*ps)


def _row_block(rows, bytes_per_row):
    best = None
    for rb in range(16, rows + 1, 16):
        if rows % rb == 0 and rb * bytes_per_row <= ELEMENTWISE_STEP_BYTES:
            best = rb
    return rows if best is None else best


def _add_pair(own, recv, name):
    shape = own.shape
    last = shape[-1]
    rows = own.size // last
    rb = _row_block(rows, 3 * 4 * (-(-last // LANES) * LANES))

    def body(a_ref, b_ref, o_ref):
        o_ref[...] = (a_ref[...].astype(F32) + b_ref[...].astype(F32)).astype(o_ref.dtype)

    row = lambda i: (i, 0)
    out = pl.pallas_call(
        body, grid=(rows // rb,), in_specs=[pl.BlockSpec((rb, last), row)] * 2, out_specs=pl.BlockSpec((rb, last), row),
        out_shape=jax.ShapeDtypeStruct((rows, last), own.dtype), name=name,
        compiler_params=_cp("parallel"))(own.reshape(rows, last), recv.reshape(rows, last))
    return out.reshape(shape)


def _sum_adamw(gk, w, m, v, name):
    shape = w.shape
    n_part = gk.shape[0]
    last = shape[-1]
    rows = w.size // last
    rb = _row_block(rows, (n_part + 7) * 4 * (-(-last // LANES) * LANES))

    def body(g_ref, w_ref, m_ref, v_ref, go_ref, d_ref, mo_ref, vo_ref):
        g = g_ref[0].astype(F32)
        for k in range(1, n_part):
            g = g + g_ref[k].astype(F32)
        mn = ADAM_B1 * m_ref[...] + (1.0 - ADAM_B1) * g
        vn = ADAM_B2 * v_ref[...] + (1.0 - ADAM_B2) * jnp.square(g)
        m_hat = mn / (1.0 - ADAM_B1 ** ADAM_STEP)
        v_hat = vn / (1.0 - ADAM_B2 ** ADAM_STEP)
        go_ref[...] = g
        d_ref[...] = -ADAM_LR * (m_hat / (jnp.sqrt(v_hat) + ADAM_EPS) + ADAM_WD * w_ref[...])
        mo_ref[...] = mn
        vo_ref[...] = vn

    row = lambda i: (i, 0)
    outs = pl.pallas_call(
        body, grid=(rows // rb,),
        in_specs=[pl.BlockSpec((n_part, rb, last), lambda i: (0, i, 0))] + [pl.BlockSpec((rb, last), row)] * 3,
        out_specs=[pl.BlockSpec((rb, last), row)] * 4,
        out_shape=[jax.ShapeDtypeStruct((rows, last), F32)] * 4,
        name=name, compiler_params=_cp("parallel"))(gk.reshape(n_part, rows, last), *[a.reshape(rows, last) for a in (w, m, v)])
    return [o.reshape(shape) for o in outs]


def _seg_rows(shape):
    n = 1
    for d in shape:
        n *= d
    return -(-n // (8 * LANES)) * 8


def _pack(arrs):
    segs = []
    for a in arrs:
        r = _seg_rows(a.shape)
        segs.append(jnp.pad(a.reshape(-1).astype(F32), (0, r * LANES - a.size)).reshape(r, LANES))
    rows = sum(s.shape[0] for s in segs)
    total = -(-rows // PACK_ROWS) * PACK_ROWS
    if total > rows:
        segs.append(jnp.zeros((total - rows, LANES), F32))
    return jnp.concatenate(segs, axis=0)


def _unpack(pack, shapes):
    out, off = [], 0
    for sh in shapes:
        r = _seg_rows(sh)
        n = 1
        for d in sh:
            n *= d
        out.append(pack[off:off + r].reshape(-1)[:n].reshape(sh))
        off += r
    return out


def _to_dest_blocks(full, axis, dtype):
    sh = list(full.shape)
    sh[axis:axis + 1] = [N_DEV // 2, 2, sh[axis] // N_DEV]
    return jnp.moveaxis(full.reshape(sh), (axis, axis + 1), (1, 0)).astype(dtype)


def _from_gathered(g, axis):
    m = jnp.moveaxis(g, 0, axis)
    sh = list(m.shape)
    sh[axis:axis + 2] = [sh[axis] * sh[axis + 1]]
    return m.reshape(sh)


D_IN = 3336
W_IN_SHARD = D_IN // N_DEV
W_IN_MOVES = ((0, 2048, 0), (2048, 2056, 3328), (2056, D_IN, 2048))


def _w_in_windows(k):
    lo, hi = k * W_IN_SHARD, (k + 1) * W_IN_SHARD
    out = []
    for a, b, mine in W_IN_MOVES:
        a2, b2 = max(a, lo), min(b, hi)
        if b2 > a2:
            out.append((a2 - lo, b2 - a2, mine + a2 - a))
    return out


def _assemble_w_in(gathered, name):
    depth = gathered.shape[1]
    rb = 256

    def body(g_ref, o_ref):
        o_ref[0, :, D_IN:] = jnp.zeros((rb, ZW - D_IN), o_ref.dtype)
        for k in range(N_DEV):
            for off, width, mine in _w_in_windows(k):
                o_ref[0, :, mine:mine + width] = g_ref[k, 0, :, off:off + width]

    return pl.pallas_call(
        body, grid=(depth, D // rb),
        in_specs=[pl.BlockSpec((N_DEV, 1, rb, W_IN_SHARD), lambda l, i: (0, l, i, 0))],
        out_specs=pl.BlockSpec((1, rb, ZW), lambda l, i: (l, i, 0)),
        out_shape=jax.ShapeDtypeStruct((depth, D, ZW), gathered.dtype),
        name=name, compiler_params=_cp("parallel", "parallel"))(gathered)


def _split_dw_in(dws, name):
    depth = len(dws)
    rb = 128

    def body(*refs):
        o_ref = refs[-1]
        for l in range(depth):
            pieces = refs[5 * l:5 * l + 5]
            for k in range(N_DEV):
                for off, width, mine in _w_in_windows(k):
                    for p_ref, start, n in zip(pieces, Z_OFFSETS, Z_PIECES):
                        a, b = max(mine, start), min(mine + width, start + n)
                        if b > a:
                            o_ref[k % 2, k // 2, l, :, off + a - mine:off + b - mine] = (
                                p_ref[:, a - start:b - start].astype(o_ref.dtype))

    row = lambda i: (i, 0)
    flat = [piece for layer in dws for piece in layer]
    return pl.pallas_call(
        body, grid=(D // rb,),
        in_specs=[pl.BlockSpec((rb, n), row) for _ in range(depth) for n in Z_PIECES],
        out_specs=pl.BlockSpec((2, N_DEV // 2, depth, rb, W_IN_SHARD), lambda i: (0, 0, 0, i, 0)),
        out_shape=jax.ShapeDtypeStruct((2, N_DEV // 2, depth, D, W_IN_SHARD), WIRE['w_in']),
        name=name, compiler_params=_cp("parallel"))(*flat)


def _local_step(x, p, wts, target):
    bl, s, _ = x.shape
    t = bl * s
    depth = p.shape[0]
    tb, sg_tb = TB, SG_TB

    def by_example(a):
        return a.reshape(bl, s, a.shape[-1])

    def flat(a):
        return a.reshape(t, a.shape[-1])

    xs = [x.reshape(t, D)]
    saved = []
    for i in range(depth):
        li = f"l{i}"
        ng = wts['norm_g'][i].reshape(1, D)
        w_in = wts['w_in'][i]
        s5_par_in = (wts['ssm_a_re'][i], wts['ssm_a_im'][i], wts['ssm_b_re'][i], wts['ssm_b_im'][i],
                     wts['ssm_c_re'][i], wts['ssm_c_im'][i], wts['ssm_d'][i], wts['ssm_log_step'][i])
        tabs, tab_vjp = jax.vjp(_s5_tables, *s5_par_in)
        s5_par = (*tabs, wts['ssm_w_glu'][i], wts['ssm_b_glu'][i].reshape(1, D_SSM))
        s5_const = _s5_powers(wts['ssm_a_re'][i], wts['ssm_a_im'][i], wts['ssm_log_step'][i])
        conv8 = jnp.pad(wts['dn_conv_w'][i], ((0, 4), (0, 0)))
        dn_par = (jnp.repeat(wts['dn_a_log'][i], DH).reshape(1, D_DN), jnp.repeat(wts['dn_dt_bias'][i], DH).reshape(1, D_DN),
                  wts['dn_norm_g'][i].reshape(1, DH))
        sg_par = (wts['sg_ln_g'][i].reshape(1, D_SG), wts['sg_ln_b'][i].reshape(1, D_SG), wts['sg_w'][i],
                  jnp.pad(jnp.transpose(wts['sg_b'][i]), ((0, 0), (0, LANES - 4))))
        out_par = (wts['w_out'][i].astype(BF16), wts['ple_norm_g'][i].reshape(1, D), wts['w_ple_gate'][i].astype(BF16),
                   wts['w_ple'][i].astype(BF16))
        pi = p[i].reshape(t, D_PLE)

        z_ssm, z_qkv, z_gdn, z_sg, z_ab, qkvn = _in_proj_fwd(xs[i], ng, w_in, conv8, s, tb, f"in_proj_fwd_{li}")
        y_ssm, carries, h_all, y_dn, states, tinvs = _mix_fwd(by_example(z_ssm), s5_par, s5_const, by_example(qkvn),
                                                              by_example(z_ab), by_example(z_gdn), dn_par, bl, s, f"mix_fwd_{li}")
        y_sg = _sg_fwd(z_sg, sg_par, sg_tb, f"sg_fwd_{li}")
        ys = (flat(y_ssm), flat(y_dn), y_sg)
        x_next, x1, gate = _out_fwd(xs[i], ys, pi, *out_par, tb, f"out_fwd_{li}")
        xs.append(x_next)
        saved.append(dict(ng=ng, w_in=w_in, tab_vjp=tab_vjp, s5_par=s5_par, s5_const=s5_const, conv8=conv8, dn_par=dn_par,
                          sg_par=sg_par, out_par=out_par, pi=pi, z=(z_ssm, z_qkv, z_gdn, z_sg, z_ab), carries=carries,
                          h_all=h_all, qkvn=qkvn, x1=x1, gate=gate,
                          states=states, tinvs=tinvs, ys=ys))

    dx, dfg, loss_vec = _loss_head(xs[depth], wts['final_norm_g'].reshape(1, D), target.reshape(t, D), tb, "loss_head")
    grads = {n: [None] * depth for n in WEIGHTS if n != 'final_norm_g'}
    grads['final_norm_g'] = dfg.reshape(D)
    for i in reversed(range(depth)):
        li = f"l{i}"
        sv = saved[i]
        z_ssm, z_qkv, z_gdn, z_sg, z_ab = sv['z']
        dx_res, dy_ssm, dy_dn, dy_sg, dwo, dpg, dwg, dwp = _out_bwd(sv['x1'], sv['gate'], sv['ys'], sv['pi'], dx, *sv['out_par'], tb,
                                                                    f"out_bwd_{li}")
        dz_sg, dlng, dlnb, dsgw, dbsp = _sg_bwd(z_sg, dy_sg, sv['sg_par'], sg_tb, f"sg_bwd_{li}")
        (dz_ssm, dbb, dcb, dlam, ddv, dwglu, dbglu), (dqkvn, dz_ab, dz_gdn, dal, ddt, dng) = _mix_bwd(
            by_example(z_ssm), sv['carries'], sv['h_all'], by_example(dy_ssm), sv['s5_par'], sv['s5_const'],
            by_example(sv['qkvn']), by_example(z_ab), by_example(z_gdn), sv['states'], sv['tinvs'], by_example(dy_dn),
            sv['dn_par'], bl, s, f"mix_bwd_{li}")
        dx, dnorm, dz_qkv, dconv = _in_proj_bwd_dx(xs[i], sv['ng'], sv['w_in'], sv['conv8'], flat(dz_ssm), flat(dz_gdn), dz_sg,
                                                   flat(dz_ab), z_qkv, flat(dqkvn), dx_res, s, tb, f"in_proj_bwd_dx_{li}")
        dzs = (flat(dz_ssm), dz_qkv, flat(dz_gdn), dz_sg, flat(dz_ab))
        dws = _in_proj_bwd_dw(xs[i], sv['ng'], dzs, min(TB_DW, t), f"in_proj_bwd_dw_{li}")
        ds5 = sv['tab_vjp']((dbb, dcb, dlam, ddv))
        for n, gval in zip(('ssm_a_re', 'ssm_a_im', 'ssm_b_re', 'ssm_b_im', 'ssm_c_re', 'ssm_c_im', 'ssm_d', 'ssm_log_step'), ds5):
            grads[n][i] = gval
        grads['norm_g'][i] = dnorm.reshape(D)
        grads['w_in'][i] = dws
        grads['ssm_w_glu'][i] = dwglu
        grads['ssm_b_glu'][i] = dbglu.reshape(D_SSM)
        grads['dn_conv_w'][i] = dconv[:4]
        grads['dn_a_log'][i] = dal.reshape(H, DH).sum(axis=1)
        grads['dn_dt_bias'][i] = ddt.reshape(H, DH).sum(axis=1)
        grads['dn_norm_g'][i] = dng.reshape(DH)
        grads['sg_ln_g'][i] = dlng.reshape(D_SG)
        grads['sg_ln_b'][i] = dlnb.reshape(D_SG)
        grads['sg_w'][i] = dsgw
        grads['sg_b'][i] = jnp.transpose(dbsp[:, :4])
        grads['w_out'][i] = dwo
        grads['ple_norm_g'][i] = dpg.reshape(D)
        grads['w_ple_gate'][i] = dwg
        grads['w_ple'][i] = dwp
    grads = {n: (g if n in ('final_norm_g', 'w_in') else jnp.stack(g)) for n, g in grads.items()}
    return loss_vec[0, 0], dx.reshape(bl, s, D), grads


def kernel(x, p, norm_g, w_in, ssm_a_re, ssm_a_im, ssm_b_re, ssm_b_im, ssm_c_re, ssm_c_im, ssm_d, ssm_log_step, ssm_w_glu, ssm_b_glu, dn_conv_w, dn_a_log, dn_dt_bias, dn_norm_g, sg_ln_g, sg_ln_b, sg_w, sg_b, w_out, ple_norm_g, w_ple_gate, w_ple, final_norm_g, loss_target, m_norm_g, m_w_in, m_ssm_a_re, m_ssm_a_im, m_ssm_b_re, m_ssm_b_im, m_ssm_c_re, m_ssm_c_im, m_ssm_d, m_ssm_log_step, m_ssm_w_glu, m_ssm_b_glu, m_dn_conv_w, m_dn_a_log, m_dn_dt_bias, m_dn_norm_g, m_sg_ln_g, m_sg_ln_b, m_sg_w, m_sg_b, m_w_out, m_ple_norm_g, m_w_ple_gate, m_w_ple, m_final_norm_g, v_norm_g, v_w_in, v_ssm_a_re, v_ssm_a_im, v_ssm_b_re, v_ssm_b_im, v_ssm_c_re, v_ssm_c_im, v_ssm_d, v_ssm_log_step, v_ssm_w_glu, v_ssm_b_glu, v_dn_conv_w, v_dn_a_log, v_dn_dt_bias, v_dn_norm_g, v_sg_ln_g, v_sg_ln_b, v_sg_w, v_sg_b, v_w_out, v_ple_norm_g, v_w_ple_gate, v_w_ple, v_final_norm_g):
    w_loc = dict(zip(WEIGHTS, (norm_g, w_in, ssm_a_re, ssm_a_im, ssm_b_re, ssm_b_im, ssm_c_re, ssm_c_im, ssm_d, ssm_log_step,
                               ssm_w_glu, ssm_b_glu, dn_conv_w, dn_a_log, dn_dt_bias, dn_norm_g, sg_ln_g, sg_ln_b, sg_w, sg_b,
                               w_out, ple_norm_g, w_ple_gate, w_ple, final_norm_g)))
    m_loc = dict(zip(WEIGHTS, (m_norm_g, m_w_in, m_ssm_a_re, m_ssm_a_im, m_ssm_b_re, m_ssm_b_im, m_ssm_c_re, m_ssm_c_im, m_ssm_d,
                               m_ssm_log_step, m_ssm_w_glu, m_ssm_b_glu, m_dn_conv_w, m_dn_a_log, m_dn_dt_bias, m_dn_norm_g,
                               m_sg_ln_g, m_sg_ln_b, m_sg_w, m_sg_b, m_w_out, m_ple_norm_g, m_w_ple_gate, m_w_ple, m_final_norm_g)))
    v_loc = dict(zip(WEIGHTS, (v_norm_g, v_w_in, v_ssm_a_re, v_ssm_a_im, v_ssm_b_re, v_ssm_b_im, v_ssm_c_re, v_ssm_c_im, v_ssm_d,
                               v_ssm_log_step, v_ssm_w_glu, v_ssm_b_glu, v_dn_conv_w, v_dn_a_log, v_dn_dt_bias, v_dn_norm_g,
                               v_sg_ln_g, v_sg_ln_b, v_sg_w, v_sg_b, v_w_out, v_ple_norm_g, v_w_ple_gate, v_w_ple, v_final_norm_g)))

    gathered = _all_gather([w_loc[n].astype(WIRE[n]) for n in SHARDED_ORDER], "gather_weights")
    full = dict(w_loc)
    for n, g in zip(SHARDED_ORDER, gathered):
        full[n] = _assemble_w_in(g, "assemble_w_in") if n == 'w_in' else _from_gathered(g, SHARDED[n])
    full['ssm_w_glu'] = full['ssm_w_glu'].astype(F32)

    loss_part, grad_x, grads = _local_step(x, p, full, loss_target)

    dest = [_split_dw_in(grads[n], "split_dw_in") if n == 'w_in' else _to_dest_blocks(grads[n], SHARDED[n], WIRE[n])
            for n in SHARDED_ORDER]
    c = lax.axis_index("c")
    own = [lax.dynamic_index_in_dim(d, c, 0, keepdims=False) for d in dest]
    for_sibling = [lax.dynamic_index_in_dim(d, 1 - c, 0, keepdims=False) for d in dest]
    from_sibling = _pair_exchange(for_sibling, "grads_pair_exchange")
    chip_sums = [_add_pair(a, b, f"grads_pair_sum_{n}") for n, a, b in zip(SHARDED_ORDER, own, from_sibling)]
    by_chip = _chip_exchange(chip_sums, "grads_chip_exchange")
    rep_pack = _pack([grads[n] for n in REPLICATED_ORDER] + [loss_part.reshape(1)])
    (rep_recv,) = _all_gather([rep_pack], "gather_small_grads")

    outs = {k: {} for k in 'gdmv'}
    for n, gk in zip(SHARDED_ORDER, by_chip):
        for k, o in zip('gdmv', _sum_adamw(gk, w_loc[n], m_loc[n], v_loc[n], f"adamw_{n}")):
            outs[k][n] = o
    one = jnp.zeros((1,), F32)
    rep_out = _sum_adamw(rep_recv, _pack([w_loc[n] for n in REPLICATED_ORDER] + [one]),
                         _pack([m_loc[n] for n in REPLICATED_ORDER] + [one]),
                         _pack([v_loc[n] for n in REPLICATED_ORDER] + [one]), "adamw_replicated")
    rep_shapes = [w_loc[n].shape for n in REPLICATED_ORDER] + [(1,)]
    for k, rep_p in zip('gdmv', rep_out):
        outs[k].update(zip(REPLICATED_ORDER + ['loss'], _unpack(rep_p, rep_shapes)))
    loss = outs['g']['loss'].reshape(())
    return (loss, grad_x, *[outs['g'][n] for n in WEIGHTS], *[outs['d'][n] for n in WEIGHTS],
            *[outs['m'][n] for n in WEIGHTS], *[outs['v'][n] for n in WEIGHTS])
```

```python
import functools

import jax
import jax.numpy as jnp
from jax import lax
from jax.experimental import pallas as pl
from jax.experimental.pallas import tpu as pltpu

F32 = jnp.float32
BF16 = jnp.bfloat16
EPS = 1e-6

D = 1024
D_PLE = 256
D_SSM = 256
D_DN = 512
D_SG = 256
G = 16
CG = 16
NS = 64
NRE = G * NS
H = 4
DH = 128
DN_C = 128
SG_C = 128
ZW = 3456
Z_PIECES = (512, 1536, 512, 768, 128)
N_DEV = 8
LANES = 128
PACK_ROWS = 256
VMEM_LIMIT = 56 * 1024 * 1024
ELEMENTWISE_STEP_BYTES = 4 * 1024 * 1024
TB = 256
SG_TB = 512
TB_DW = 512

ADAM_LR = 0.001
ADAM_B1 = 0.9
ADAM_B2 = 0.999
ADAM_EPS = 1e-08
ADAM_WD = 0.01
ADAM_STEP = 10

MIX_HEAD_START = 3
S5_L = 128
S5_GROUP = 8
S5_SHIFTS = (1, 2, 4)

WEIGHTS = ['norm_g', 'w_in', 'ssm_a_re', 'ssm_a_im', 'ssm_b_re', 'ssm_b_im', 'ssm_c_re', 'ssm_c_im', 'ssm_d',
           'ssm_log_step', 'ssm_w_glu', 'ssm_b_glu', 'dn_conv_w', 'dn_a_log', 'dn_dt_bias', 'dn_norm_g', 'sg_ln_g',
           'sg_ln_b', 'sg_w', 'sg_b', 'w_out', 'ple_norm_g', 'w_ple_gate', 'w_ple', 'final_norm_g']
SHARDED = {'w_in': 2, 'ssm_w_glu': 1, 'dn_conv_w': 2, 'w_out': 1, 'w_ple_gate': 1, 'w_ple': 2}
SHARDED_ORDER = ['w_in', 'ssm_w_glu', 'dn_conv_w', 'w_out', 'w_ple_gate', 'w_ple']
WIRE = {'w_in': BF16, 'ssm_w_glu': BF16, 'dn_conv_w': F32, 'w_out': BF16, 'w_ple_gate': BF16, 'w_ple': BF16}
REPLICATED_ORDER = [n for n in WEIGHTS if n not in SHARDED]


def _cp(*sem):
    return pltpu.CompilerParams(dimension_semantics=sem, vmem_limit_bytes=VMEM_LIMIT)


def _dg(a, b, ca, cb, precision=None):
    return lax.dot_general(a, b, (((ca,), (cb,)), ((), ())), precision=precision, preferred_element_type=F32)


@jax.custom_vjp
def _mm(a, b):
    return _dg(a.astype(BF16), b.astype(BF16), 1, 0)


def _mm_fwd(a, b):
    return _mm(a, b), (a, b)


def _mm_bwd(res, g):
    a, b = res
    gb = g.astype(BF16)
    return _dg(gb, b.astype(BF16), 1, 1), _dg(a.astype(BF16), gb, 0, 0)


_mm.defvjp(_mm_fwd, _mm_bwd)


@jax.custom_vjp
def _mm_nt(a, b):
    return _dg(a.astype(BF16), b.astype(BF16), 1, 1)


def _mm_nt_fwd(a, b):
    return _mm_nt(a, b), (a, b)


def _mm_nt_bwd(res, g):
    a, b = res
    gb = g.astype(BF16)
    return _dg(gb, b.astype(BF16), 1, 0), _dg(gb, a.astype(BF16), 0, 0)


_mm_nt.defvjp(_mm_nt_fwd, _mm_nt_bwd)


@jax.custom_vjp
def _mm_tn(a, b):
    return _dg(a.astype(BF16), b.astype(BF16), 0, 0)


def _mm_tn_fwd(a, b):
    return _mm_tn(a, b), (a, b)


def _mm_tn_bwd(res, g):
    a, b = res
    gb = g.astype(BF16)
    return _dg(b.astype(BF16), gb, 1, 1), _dg(a.astype(BF16), gb, 1, 0)


_mm_tn.defvjp(_mm_tn_fwd, _mm_tn_bwd)


def _split(x, n):
    pieces = []
    for _ in range(n - 1):
        hi = x.astype(BF16)
        pieces.append(hi)
        x = x - hi.astype(F32)
    pieces.append(x.astype(BF16))
    return pieces


def _dg3(a, b, ca, cb):
    a_hi, a_lo = _split(a, 2)
    b_hi, b_lo = _split(b, 2)
    return _dg(a_hi, b_hi, ca, cb) + (_dg(a_hi, b_lo, ca, cb) + _dg(a_lo, b_hi, ca, cb))


@jax.custom_vjp
def _dot3(a, b):
    return _dg3(a, b, 1, 0)


def _dot3_fwd(a, b):
    return _dot3(a, b), (a, b)


def _dot3_bwd(res, g):
    a, b = res
    return _dg3(g, b, 1, 1), _dg3(a, g, 0, 0)


_dot3.defvjp(_dot3_fwd, _dot3_bwd)


def _dg_sel(x, e, cx, ce, x_first):
    eb = e.astype(BF16)
    out = None
    for piece in reversed(_split(x, 3)):
        term = _dg(piece, eb, cx, ce) if x_first else _dg(eb, piece, ce, cx)
        out = term if out is None else out + term
    return out


@jax.custom_vjp
def _sel_r(x, e):
    return _dg_sel(x, e, 1, 0, True)


def _sel_r_fwd(x, e):
    return _sel_r(x, e), e


def _sel_r_bwd(e, g):
    return _dg_sel(g, e, 1, 1, True), jnp.zeros_like(e)


_sel_r.defvjp(_sel_r_fwd, _sel_r_bwd)


@jax.custom_vjp
def _sel_l(e, x):
    return _dg_sel(x, e, 0, 1, False)


def _sel_l_fwd(e, x):
    return _sel_l(e, x), e


def _sel_l_bwd(e, g):
    return jnp.zeros_like(e), _dg_sel(g, e, 0, 0, False)


_sel_l.defvjp(_sel_l_fwd, _sel_l_bwd)


def _rms(x, g):
    return x * lax.rsqrt(jnp.mean(x * x, axis=-1, keepdims=True) + EPS) * g


def _silu(x):
    return x * jax.nn.sigmoid(x)


Z_OFFSETS = (0, 512, 2048, 2560, 3328)


def _dn_post(c):
    s = _silu(c)
    parts = []
    for j in range(12):
        xj = s[:, j * DH:(j + 1) * DH]
        if j < 8:
            xj = xj * lax.rsqrt(jnp.sum(xj * xj, axis=-1, keepdims=True) + EPS)
        if j < 4:
            xj = xj * (DH ** -0.5)
        parts.append(xj)
    return jnp.concatenate(parts, axis=1)


def _dn_conv(ext, cw_ref, rows):
    c = None
    for k in range(4):
        sh = ext if k == 3 else pltpu.roll(ext, 3 - k, 0)
        term = cw_ref[k:k + 1, :] * sh[ext.shape[0] - rows:, :]
        c = term if c is None else c + term
    return c


def _dn_prep_vjp(prev, cur, nxt, d_cur, d_nxt, cw_ref, tb):
    ext = jnp.concatenate([prev, cur, nxt], axis=0)
    shifted = [ext if k == 3 else pltpu.roll(ext, 3 - k, 0) for k in range(4)]
    c2 = None
    for k in range(4):
        term = cw_ref[k:k + 1, :] * shifted[k][8:, :]
        c2 = term if c2 is None else c2 + term
    _, vjp = jax.vjp(_dn_post, c2)
    (dc2,) = vjp(jnp.concatenate([d_cur, d_nxt], axis=0))
    dz, dcw = None, []
    for k in range(4):
        up = dc2 if k == 3 else pltpu.roll(dc2, tb + 8 - (3 - k), 0)
        term = cw_ref[k:k + 1, :] * up[:tb, :]
        dz = term if dz is None else dz + term
        dcw.append(jnp.sum(dc2[:tb, :] * shifted[k][8:8 + tb, :], axis=0, keepdims=True))
    return dz, dcw


def _in_proj_fwd(x, g, w, conv_w8, s, tb, name):
    t = x.shape[0]
    n_s = s // tb
    w3 = 3 * D_DN
    q0, q1 = Z_OFFSETS[1], Z_OFFSETS[2]

    def body(x_ref, g_ref, w_ref, cw_ref, zs_ref, zq_ref, zg_ref, zsg_ref, zab_ref, qkvn_ref, halo):
        h = _rms(x_ref[...], g_ref[...]).astype(BF16)
        zq = jnp.dot(h, w_ref[:, q0:q1], preferred_element_type=F32)
        zq_ref[...] = zq
        prev = jnp.where(pl.program_id(0) % n_s == 0, 0.0, halo[...])
        qkvn_ref[...] = _dn_post(_dn_conv(jnp.concatenate([prev, zq], axis=0), cw_ref, tb))
        halo[...] = zq[tb - 8:, :]
        zs_ref[...] = jnp.dot(h, w_ref[:, :q0], preferred_element_type=F32)
        rest = jnp.dot(h, w_ref[:, q1:], preferred_element_type=F32)
        zg_ref[...] = rest[:, :Z_PIECES[2]]
        zsg_ref[...] = rest[:, Z_PIECES[2]:Z_PIECES[2] + Z_PIECES[3]]
        zab_ref[...] = rest[:, Z_PIECES[2] + Z_PIECES[3]:]

    row = lambda i: (i, 0)
    full = lambda i: (0, 0)
    widths = Z_PIECES + (w3,)
    return pl.pallas_call(
        body, grid=(t // tb,),
        in_specs=[pl.BlockSpec((tb, D), row), pl.BlockSpec((1, D), full), pl.BlockSpec((D, ZW), full), pl.BlockSpec((8, w3), full)],
        out_specs=[pl.BlockSpec((tb, n), row) for n in widths],
        out_shape=[jax.ShapeDtypeStruct((t, n), F32) for n in widths],
        scratch_shapes=[pltpu.VMEM((8, w3), F32)],
        name=name, compiler_params=_cp("arbitrary"))(x, g, w, conv_w8)


def _in_proj_bwd_dx(x, g, w, conv_w8, dz_ssm, dz_gdn, dz_sg, dz_ab, zq, dqkvn, dx_res, s, tb, name):
    t = x.shape[0]
    n_s = s // tb
    hb = tb // 8
    w3 = 3 * D_DN
    q0, q1 = Z_OFFSETS[1], Z_OFFSETS[2]

    def body(x_ref, g_ref, w_ref, cw_ref, ds_ref, dgd_ref, dsg_ref, dab_ref, cur_ref, prev_ref, next_ref, dq_ref, dqn_ref,
             dxr_ref, dx_ref, dg_ref, dzq_ref, dcw_ref):
        i = pl.program_id(0)

        @pl.when(i == 0)
        def _():
            dg_ref[...] = jnp.zeros_like(dg_ref)
            dcw_ref[...] = jnp.zeros_like(dcw_ref)

        rest = jnp.concatenate([dgd_ref[...], dsg_ref[...], dab_ref[...]], axis=1)
        dh = _dg(ds_ref[...], w_ref[:, :q0], 1, 1) + _dg(rest, w_ref[:, q1:], 1, 1)
        first, last = i % n_s == 0, i % n_s == n_s - 1
        dzq, dcw = _dn_prep_vjp(jnp.where(first, 0.0, prev_ref[...]), cur_ref[...], jnp.where(last, 0.0, next_ref[...]),
                                dq_ref[...], jnp.where(last, 0.0, dqn_ref[...]), cw_ref, tb)
        for k in range(4):
            dcw_ref[k:k + 1, :] += dcw[k]
        dzq = dzq.astype(BF16)
        dzq_ref[...] = dzq
        dh = dh + _dg(dzq, w_ref[:, q0:q1], 1, 1)
        _, vjp = jax.vjp(_rms, x_ref[...], g_ref[...])
        dx, dg = vjp(dh)
        dx_ref[...] = dx + dxr_ref[...]
        dg_ref[...] += dg

    n_blk8 = t // 8
    row = lambda i: (i, 0)
    prv = lambda i: (jnp.maximum(i * hb - 1, 0), 0)
    nxt = lambda i: (jnp.minimum((i + 1) * hb, n_blk8 - 1), 0)
    full = lambda i: (0, 0)
    return pl.pallas_call(
        body, grid=(t // tb,),
        in_specs=[pl.BlockSpec((tb, D), row), pl.BlockSpec((1, D), full), pl.BlockSpec((D, ZW), full), pl.BlockSpec((8, w3), full)]
        + [pl.BlockSpec((tb, n), row) for n in (Z_PIECES[0], Z_PIECES[2], Z_PIECES[3], Z_PIECES[4])]
        + [pl.BlockSpec((tb, w3), row), pl.BlockSpec((8, w3), prv), pl.BlockSpec((8, w3), nxt),
           pl.BlockSpec((tb, w3), row), pl.BlockSpec((8, w3), nxt), pl.BlockSpec((tb, D), row)],
        out_specs=[pl.BlockSpec((tb, D), row), pl.BlockSpec((1, D), full), pl.BlockSpec((tb, w3), row), pl.BlockSpec((8, w3), full)],
        out_shape=[jax.ShapeDtypeStruct((t, D), F32), jax.ShapeDtypeStruct((1, D), F32), jax.ShapeDtypeStruct((t, w3), BF16),
                   jax.ShapeDtypeStruct((8, w3), F32)],
        name=name, compiler_params=_cp("arbitrary"))(x, g, w, conv_w8, dz_ssm, dz_gdn, dz_sg, dz_ab, zq, zq, zq, dqkvn, dqkvn, dx_res)


def _in_proj_bwd_dw(x, g, dzs, tb, name):
    t = x.shape[0]

    def body(x_ref, g_ref, d0, d1, d2, d3, d4, *dw_refs):
        @pl.when(pl.program_id(0) == 0)
        def _():
            for r in dw_refs:
                r[...] = jnp.zeros_like(r)

        h = _rms(x_ref[...], g_ref[...]).astype(BF16)
        for d_ref, dw_ref in zip((d0, d1, d2, d3, d4), dw_refs):
            dw_ref[...] += _dg(h, d_ref[...].astype(BF16), 0, 0)

    row = lambda i: (i, 0)
    full = lambda i: (0, 0)
    return pl.pallas_call(
        body, grid=(t // tb,),
        in_specs=[pl.BlockSpec((tb, D), row), pl.BlockSpec((1, D), full)] + [pl.BlockSpec((tb, n), row) for n in Z_PIECES],
        out_specs=[pl.BlockSpec((D, n), full) for n in Z_PIECES],
        out_shape=[jax.ShapeDtypeStruct((D, n), F32) for n in Z_PIECES],
        name=name, compiler_params=_cp("arbitrary"))(x, g, *dzs)


def _lam_pow(a_re, a_im, log_step, k):
    step = jnp.exp(log_step)[:, None]
    mag = jnp.exp(k * a_re * step)
    ang = k * a_im * step
    return mag * jnp.cos(ang), mag * jnp.sin(ang)


def _s5_powers(a_re, a_im, log_step):
    def table(ks):
        re, im = _lam_pow(a_re, a_im, log_step, jnp.asarray(ks, F32)[:, None, None])
        return jnp.concatenate([re.reshape(len(ks), NRE), im.reshape(len(ks), NRE)], axis=-1)

    ld = table(S5_SHIFTS).reshape(len(S5_SHIFTS), 1, 2 * NRE)
    return ld, table(range(1, S5_GROUP + 1)), table(range(S5_GROUP, 0, -1))


def _s5_tables(a_re, a_im, b_re, b_im, c_re, c_im, d_skip, log_step):
    lam_re, lam_im = _lam_pow(a_re, a_im, log_step, 1.0)
    den = a_re * a_re + a_im * a_im
    nr, ni = lam_re - 1.0, lam_im
    f_re = (nr * a_re + ni * a_im) / den
    f_im = (ni * a_re - nr * a_im) / den
    bbar_re = f_re[..., None] * b_re - f_im[..., None] * b_im
    bbar_im = f_re[..., None] * b_im + f_im[..., None] * b_re
    eye = jnp.eye(G, dtype=F32)

    def blk_b(bb):
        return (jnp.transpose(bb, (0, 2, 1))[:, :, None, :] * eye[:, None, :, None]).reshape(D_SSM, NRE)

    def blk_c(cc):
        return (jnp.transpose(cc, (0, 2, 1))[:, :, None, :] * eye[:, None, :, None]).reshape(NRE, D_SSM)

    b_blk = jnp.concatenate([blk_b(bbar_re), blk_b(bbar_im)], axis=1)
    c_blk = jnp.concatenate([blk_c(c_re), -blk_c(c_im)], axis=0)
    lam = jnp.concatenate([lam_re.reshape(1, NRE), lam_im.reshape(1, NRE)], axis=-1)
    return b_blk, c_blk, lam, d_skip.reshape(1, D_SSM)


def _group_shift(x, d, up=False):
    r = lax.broadcasted_iota(jnp.int32, x.shape, 0) & (S5_GROUP - 1)
    if up:
        return jnp.where(r < S5_GROUP - d, pltpu.roll(x, x.shape[0] - d, 0), 0.0)
    return jnp.where(r >= d, pltpu.roll(x, d, 0), 0.0)


def _s5_scan_steps(hr, hi, cr, ci, lds, lp):
    for ld, d in zip(lds, S5_SHIFTS):
        lr, li = ld[:, :NRE], ld[:, NRE:]
        sr, si = _group_shift(hr, d), _group_shift(hi, d)
        hr, hi = hr + lr * sr - li * si, hi + lr * si + li * sr
        yield
    pr, pi = lp[:, :NRE], lp[:, NRE:]
    rows_r, rows_i = [], []
    for r in range(hr.shape[0] // S5_GROUP):
        br, bi = hr[r * S5_GROUP:(r + 1) * S5_GROUP], hi[r * S5_GROUP:(r + 1) * S5_GROUP]
        br, bi = br + pr * cr - pi * ci, bi + pr * ci + pi * cr
        cr, ci = br[S5_GROUP - 1:S5_GROUP], bi[S5_GROUP - 1:S5_GROUP]
        rows_r.append(br)
        rows_i.append(bi)
        if r % 2:
            yield
    return jnp.concatenate(rows_r, axis=0), jnp.concatenate(rows_i, axis=0)


@jax.custom_vjp
def _known_scan(xr, xi, cr, ci, lam, lds, lp_rev, hr, hi):
    return hr, hi


def _known_scan_fwd(xr, xi, cr, ci, lam, lds, lp_rev, hr, hi):
    return (hr, hi), (cr, ci, lam, lds, lp_rev, hr, hi)


def _known_scan_bwd(res, cts):
    cr, ci, lam, lds, lp_rev, hr, hi = res
    ar, ai = cts
    for ld, d in zip(lds, S5_SHIFTS):
        lr, li = ld[:, :NRE], ld[:, NRE:]
        sr, si = _group_shift(ar, d, up=True), _group_shift(ai, d, up=True)
        ar, ai = ar + lr * sr + li * si, ai + lr * si - li * sr
    qr, qi = lp_rev[:, :NRE], lp_rev[:, NRE:]
    nr, ni = jnp.zeros_like(cr), jnp.zeros_like(ci)
    rows_r, rows_i = [], []
    for r in reversed(range(hr.shape[0] // S5_GROUP)):
        br, bi = ar[r * S5_GROUP:(r + 1) * S5_GROUP], ai[r * S5_GROUP:(r + 1) * S5_GROUP]
        br, bi = br + qr * nr + qi * ni, bi + qr * ni - qi * nr
        nr, ni = br[0:1], bi[0:1]
        rows_r.insert(0, br)
        rows_i.insert(0, bi)
    ar, ai = jnp.concatenate(rows_r, axis=0), jnp.concatenate(rows_i, axis=0)
    lr, li = lam[:, :NRE], lam[:, NRE:]
    dcr, dci = lr * nr + li * ni, lr * ni - li * nr
    first = lax.broadcasted_iota(jnp.int32, hr.shape, 0) == 0
    pr = jnp.where(first, cr, pltpu.roll(hr, 1, 0))
    pi = jnp.where(first, ci, pltpu.roll(hi, 1, 0))
    dlam = jnp.concatenate([jnp.sum(ar * pr + ai * pi, axis=0, keepdims=True),
                            jnp.sum(ai * pr - ar * pi, axis=0, keepdims=True)], axis=1)
    return (ar, ai, dcr, dci, dlam, [jnp.zeros_like(ld) for ld in lds], jnp.zeros_like(lp_rev),
            jnp.zeros_like(hr), jnp.zeros_like(hi))


_known_scan.defvjp(_known_scan_fwd, _known_scan_bwd)


def _interleave(short, long, head_start=0):
    gens = list(short) + list(long)
    results = [None] * len(gens)

    def advance(live):
        still = []
        for idx, gen in live:
            try:
                next(gen)
                still.append((idx, gen))
            except StopIteration as done:
                results[idx] = done.value
        return still

    live_short = advance(list(enumerate(gens))[:len(short)])
    live_long = list(enumerate(gens))[len(short):]
    for _ in range(head_start):
        live_long = advance(live_long)
    live = live_short + live_long
    while live:
        live = advance(live)
    return results[:len(short)], results[len(short):]


def _s5_chunk_gen(u, gate, cr, ci, b_blk, c_blk, lam, dv, wglu, bglu, lds, lp, lp_rev, known_h=None):
    bu = _mm(u, b_blk)
    xr, xi = bu[:, :NRE], bu[:, NRE:]
    yield
    if known_h is None:
        hr, hi = yield from _s5_scan_steps(xr, xi, cr, ci, lds, lp)
    else:
        hr, hi = _known_scan(xr, xi, cr, ci, lam, lds, lp_rev, *known_h)
    y = _mm(jnp.concatenate([hr, hi], axis=1), c_blk) + dv * u
    yield
    y = jax.nn.gelu(y)
    y = y * jax.nn.sigmoid(_mm(y, wglu) + bglu)
    return y * _silu(gate), hr, hi


S5_PAR_SHAPES = [(D_SSM, 2 * NRE), (2 * NRE, D_SSM), (1, 2 * NRE), (1, D_SSM), (D_SSM, D_SSM), (1, D_SSM)]
S5_CONST_SHAPES = [(len(S5_SHIFTS), 1, 2 * NRE), (S5_GROUP, 2 * NRE), (S5_GROUP, 2 * NRE)]
DN_PAR_SHAPES = [(1, D_DN), (1, D_DN), (1, DH)]


def _mix_specs(bl, n_c, rev):
    def chunk(i):
        return n_c - 1 - i if rev else i

    def tok(n):
        return pl.BlockSpec((bl, DN_C, n), lambda i: (0, chunk(i), 0))

    def per_chunk(shape):
        return pl.BlockSpec((bl, 1, *shape), lambda i: (0, chunk(i)) + (0,) * len(shape))

    def whole(shape):
        return pl.BlockSpec(shape, lambda i: (0,) * len(shape))

    return tok, per_chunk, whole


def _unit_lower_inverse_steps(ms):
    c_len = ms[0].shape[0]
    eye = lax.broadcasted_iota(jnp.int32, (c_len, c_len), 0) == lax.broadcasted_iota(jnp.int32, (c_len, c_len), 1)
    ident = jnp.where(eye, 1.0, 0.0)
    ps = ms
    tinvs = [ident - m for m in ms]
    for _ in range(c_len.bit_length() - 2):
        ps = [_dg3(p, p, 1, 0) for p in ps]
        yield
        tinvs = [t + _dg3(t, p, 1, 0) for t, p in zip(tinvs, ps)]
        yield
    return tinvs


@jax.custom_vjp
def _known_inverses(ms, tinvs):
    return tinvs


def _known_inverses_fwd(ms, tinvs):
    return tinvs, tinvs


def _known_inverses_bwd(tinvs, gs):
    return [-_dg3(_dg3(t, g, 0, 0), t, 1, 1) for t, g in zip(tinvs, gs)], [jnp.zeros_like(t) for t in tinvs]


_known_inverses.defvjp(_known_inverses_fwd, _known_inverses_bwd)


def _dn_chunk_gen(qkv, zab, zg, states, alog_e, dt_e, ng, known_tinvs=None):
    c_len = DN_C
    r = lax.broadcasted_iota(jnp.int32, (c_len, c_len), 0)
    c = lax.broadcasted_iota(jnp.int32, (c_len, c_len), 1)
    causal, strict = r >= c, r > c
    tril = jnp.where(causal, 1.0, 0.0)
    rr = lax.broadcasted_iota(jnp.int32, (LANES, D_DN), 0)
    cc = lax.broadcasted_iota(jnp.int32, (LANES, D_DN), 1)
    e_a = jnp.where((cc >= rr * DH) & (cc < rr * DH + DH) & (rr < H), 1.0, 0.0)
    e_b = jnp.where((cc >= (rr - H) * DH) & (cc < (rr - H) * DH + DH) & (rr >= H) & (rr < 2 * H), 1.0, 0.0)
    a_e = _sel_r(zab, e_a)
    b_e = _sel_r(zab, e_b)
    beta = jax.nn.sigmoid(b_e)
    g = -jnp.exp(alog_e) * jax.nn.softplus(a_e + dt_e)
    yield
    gc = _sel_l(tril, g)
    glast = jnp.sum(g, axis=0, keepdims=True)
    eg = jnp.exp(gc)
    ekd = jnp.exp(glast - gc)
    dl = jnp.exp(glast)
    yield
    heads = range(H)
    sls = [slice(h * DH, (h + 1) * DH) for h in heads]
    qs = [qkv[:, h * DH:(h + 1) * DH] for h in heads]
    ks = [qkv[:, D_DN + h * DH:D_DN + (h + 1) * DH] for h in heads]
    vs = [qkv[:, 2 * D_DN + h * DH:2 * D_DN + (h + 1) * DH] for h in heads]
    ccols = [gc[:, sl] for sl in sls]
    decs = [jnp.where(causal, jnp.exp(jnp.where(causal, cl - jnp.transpose(cl), 0.0)), 0.0) for cl in ccols]
    kbs = [k * beta[:, sl] for k, sl in zip(ks, sls)]
    ms = [jnp.where(strict, _mm_nt(kb, k) * dec, 0.0) for kb, k, dec in zip(kbs, ks, decs)]
    yield
    if known_tinvs is None:
        tinvs = yield from _unit_lower_inverse_steps(ms)
    else:
        tinvs = _known_inverses(ms, list(known_tinvs))
    sols = [_dot3(t, jnp.concatenate([v * beta[:, sl], kb * eg[:, sl]], axis=1))
            for t, v, kb, sl in zip(tinvs, vs, kbs, sls)]
    yield
    atts = [_mm_nt(q, k) * dec for q, k, dec in zip(qs, ks, decs)]
    vnews = [sol[:, :DH] - _mm(sol[:, DH:], st) for sol, st in zip(sols, states)]
    yield
    os_ = [_mm(q * eg[:, sl], st) + _mm(att, vn) for q, sl, st, att, vn in zip(qs, sls, states, atts, vnews)]
    yield
    new_states = [st * dl[:, sl] + _mm_tn(k * ekd[:, sl], vn) for st, sl, k, vn in zip(states, sls, ks, vnews)]
    yield
    ys = [_rms(o, ng) * _silu(zg[:, sl]) for o, sl in zip(os_, sls)]
    return jnp.concatenate(ys, axis=1), new_states, tinvs


def _mix_fwd(zs, s5_par, s5_const, qkv, zab, zg, dn_par, bl, s, name):
    assert S5_L == DN_C
    n_c = s // DN_C
    nd = len(S5_SHIFTS)
    tok, per_chunk, whole = _mix_specs(bl, n_c, False)

    def body(z_ref, b_ref, c_ref, lam_ref, dv_ref, wg_ref, bg_ref, ld_ref, lp_ref, lpr_ref,
             q_ref, ab_ref, zg_ref, al_ref, dt_ref, ng_ref,
             ys_ref, car_ref, h_ref, yd_ref, st_ref, ti_ref, cs, ssc):
        @pl.when(pl.program_id(0) == 0)
        def _():
            cs[...] = jnp.zeros_like(cs)
            ssc[...] = jnp.zeros_like(ssc)

        lds = [ld_ref[k] for k in range(nd)]
        s5_gens, dn_gens = [], []
        for e in range(bl):
            c = cs[e]
            car_ref[e, 0] = c
            sts = [ssc[e, h] for h in range(H)]
            for h in range(H):
                st_ref[e, 0, h] = sts[h]
            z = z_ref[e]
            s5_gens.append(_s5_chunk_gen(z[:, :D_SSM], z[:, D_SSM:], c[:, :NRE], c[:, NRE:], b_ref[...], c_ref[...], lam_ref[...],
                                         dv_ref[...], wg_ref[...], bg_ref[...], lds, lp_ref[...], lpr_ref[...]))
            dn_gens.append(_dn_chunk_gen(q_ref[e], ab_ref[e], zg_ref[e], sts, al_ref[...], dt_ref[...], ng_ref[...]))
        s5_outs, dn_outs = _interleave(s5_gens, dn_gens, head_start=MIX_HEAD_START)
        for e in range(bl):
            y_s, hr, hi = s5_outs[e]
            y_d, new_sts, tinvs = dn_outs[e]
            ys_ref[e] = y_s
            h_ref[e, :, :NRE] = hr
            h_ref[e, :, NRE:] = hi
            cs[e, :, :NRE] = hr[S5_L - 1:S5_L]
            cs[e, :, NRE:] = hi[S5_L - 1:S5_L]
            yd_ref[e] = y_d
            for h in range(H):
                ssc[e, h] = new_sts[h]
                ti_ref[e, 0, h] = tinvs[h]

    head_mats = jax.ShapeDtypeStruct((bl, n_c, H, DH, DH), F32)
    return pl.pallas_call(
        body, grid=(n_c,),
        in_specs=[tok(2 * D_SSM)] + [whole(sh) for sh in S5_PAR_SHAPES + S5_CONST_SHAPES]
        + [tok(3 * D_DN), tok(LANES), tok(D_DN)] + [whole(sh) for sh in DN_PAR_SHAPES],
        out_specs=[tok(D_SSM), per_chunk((1, 2 * NRE)), tok(2 * NRE), tok(D_DN), per_chunk((H, DH, DH)), per_chunk((H, DH, DH))],
        out_shape=[jax.ShapeDtypeStruct((bl, s, D_SSM), F32), jax.ShapeDtypeStruct((bl, n_c, 1, 2 * NRE), F32),
                   jax.ShapeDtypeStruct((bl, s, 2 * NRE), F32), jax.ShapeDtypeStruct((bl, s, D_DN), F32), head_mats, head_mats],
        scratch_shapes=[pltpu.VMEM((bl, 1, 2 * NRE), F32), pltpu.VMEM((bl, H, DH, DH), F32)],
        name=name, compiler_params=_cp("arbitrary"))(zs, *s5_par, *s5_const, qkv, zab, zg, *dn_par)


def _mix_bwd(zs, carries, h_all, dy_s, s5_par, s5_const, qkv, zab, zg, states, tinvs, dy_d, dn_par, bl, s, name):
    n_c = s // DN_C
    nd = len(S5_SHIFTS)
    tok, per_chunk, whole = _mix_specs(bl, n_c, True)

    def both(examples, s5_tabs, s5_consts, dn_tabs):
        s5_gens = [_s5_chunk_gen(u, gate, cr, ci, *s5_tabs, *s5_consts, known_h=(hr, hi))
                   for u, gate, cr, ci, hr, hi, _, _, _, _, _ in examples]
        dn_gens = [_dn_chunk_gen(q, ab, zgate, sts, *dn_tabs, known_tinvs=known)
                   for _, _, _, _, _, _, q, ab, zgate, sts, known in examples]
        s5_outs, dn_outs = _interleave(s5_gens, dn_gens, head_start=MIX_HEAD_START)
        return [(y_s, hr[S5_L - 1:S5_L], hi[S5_L - 1:S5_L], y_d, new_sts)
                for (y_s, hr, hi), (y_d, new_sts, _) in zip(s5_outs, dn_outs)]

    def body(z_ref, car_ref, h_ref, dys_ref, b_ref, c_ref, lam_ref, dv_ref, wg_ref, bg_ref, ld_ref, lp_ref, lpr_ref,
             q_ref, ab_ref, zg_ref, st_ref, ti_ref, dyd_ref, al_ref, dt_ref, ng_ref,
             dz_ref, db_ref, dc_ref, dlam_ref, ddv_ref, dwg_ref, dbg_ref,
             dq_ref, dab_ref, dzg_ref, dal_ref, ddt_ref, dng_ref, dcs, dsc):
        accs = (db_ref, dc_ref, dlam_ref, ddv_ref, dwg_ref, dbg_ref, dal_ref, ddt_ref, dng_ref)

        @pl.when(pl.program_id(0) == 0)
        def _():
            for r in accs + (dcs, dsc):
                r[...] = jnp.zeros_like(r)

        examples = []
        for e in range(bl):
            z = z_ref[e]
            c = car_ref[e, 0]
            examples.append((z[:, :D_SSM], z[:, D_SSM:], c[:, :NRE], c[:, NRE:], h_ref[e, :, :NRE], h_ref[e, :, NRE:],
                             q_ref[e], ab_ref[e], zg_ref[e], [st_ref[e, 0, h] for h in range(H)],
                             [ti_ref[e, 0, h] for h in range(H)]))
        _, vjp = jax.vjp(both, examples,
                         (b_ref[...], c_ref[...], lam_ref[...], dv_ref[...], wg_ref[...], bg_ref[...]),
                         ([ld_ref[k] for k in range(nd)], lp_ref[...], lpr_ref[...]),
                         (al_ref[...], dt_ref[...], ng_ref[...]))
        cts = []
        for e in range(bl):
            dc = dcs[e]
            cts.append((dys_ref[e], dc[:, :NRE], dc[:, NRE:], dyd_ref[e], [dsc[e, h] for h in range(H)]))
        d_examples, d_s5, _, d_dn = vjp(cts)
        for e in range(bl):
            du, dgate, dcr, dci, _, _, dq, dab, dzg, dsts, _ = d_examples[e]
            dz_ref[e] = jnp.concatenate([du, dgate], axis=1).astype(BF16)
            dcs[e, :, :NRE] = dcr
            dcs[e, :, NRE:] = dci
            dq_ref[e] = dq
            dab_ref[e] = dab.astype(BF16)
            dzg_ref[e] = dzg.astype(BF16)
            for h in range(H):
                dsc[e, h] = dsts[h]
        for r, ct in zip(accs, (*d_s5, *d_dn)):
            r[...] += ct

    head_mats = per_chunk((H, DH, DH))
    outs = pl.pallas_call(
        body, grid=(n_c,),
        in_specs=[tok(2 * D_SSM), per_chunk((1, 2 * NRE)), tok(2 * NRE), tok(D_SSM)]
        + [whole(sh) for sh in S5_PAR_SHAPES + S5_CONST_SHAPES]
        + [tok(3 * D_DN), tok(LANES), tok(D_DN), head_mats, head_mats, tok(D_DN)] + [whole(sh) for sh in DN_PAR_SHAPES],
        out_specs=[tok(2 * D_SSM)] + [whole(sh) for sh in S5_PAR_SHAPES]
        + [tok(3 * D_DN), tok(LANES), tok(D_DN)] + [whole(sh) for sh in DN_PAR_SHAPES],
        out_shape=[jax.ShapeDtypeStruct((bl, s, 2 * D_SSM), BF16)] + [jax.ShapeDtypeStruct(sh, F32) for sh in S5_PAR_SHAPES]
        + [jax.ShapeDtypeStruct((bl, s, 3 * D_DN), F32), jax.ShapeDtypeStruct((bl, s, LANES), BF16),
           jax.ShapeDtypeStruct((bl, s, D_DN), BF16)]
        + [jax.ShapeDtypeStruct(sh, F32) for sh in DN_PAR_SHAPES],
        scratch_shapes=[pltpu.VMEM((bl, 1, 2 * NRE), F32), pltpu.VMEM((bl, H, DH, DH), F32)],
        name=name, compiler_params=_cp("arbitrary"))(
            zs, carries, h_all, dy_s, *s5_par, *s5_const, qkv, zab, zg, states, tinvs, dy_d, *dn_par)
    return outs[:7], outs[7:]


def _sg_fn(n_chunk):
    def f(z, lng, lnb, w, bsp_t):
        u = jax.nn.gelu(z[:, :D_SG])
        v = jax.nn.gelu(z[:, D_SG:2 * D_SG])
        gate = z[:, 2 * D_SG:]
        xc = v - jnp.mean(v, axis=-1, keepdims=True)
        vn = xc * lax.rsqrt(jnp.mean(xc * xc, axis=-1, keepdims=True) + EPS) * lng + lnb
        r = lax.broadcasted_iota(jnp.int32, (SG_C, SG_C), 0)
        c = lax.broadcasted_iota(jnp.int32, (SG_C, SG_C), 1)
        causal = r >= c
        first_half = c < SG_C // 2
        rr = lax.broadcasted_iota(jnp.int32, (LANES, D_SG), 0)
        cc = lax.broadcasted_iota(jnp.int32, (LANES, D_SG), 1)
        expand = jnp.where((cc >= rr * 64) & (cc < rr * 64 + 64) & (rr < 4), 1.0, 0.0)
        bias = _sel_r(bsp_t, expand)
        wm = [jnp.where(causal, w[h], 0.0) for h in range(4)]
        rows = []
        for ci in range(n_chunk):
            vc = vn[ci * SG_C:(ci + 1) * SG_C]
            pairs = []
            for pr in range(2):
                vp = vc[:, pr * LANES:(pr + 1) * LANES]
                pairs.append(jnp.where(first_half, _mm(wm[2 * pr], vp), _mm(wm[2 * pr + 1], vp)))
            rows.append(jnp.concatenate(pairs, axis=1) + bias)
        sp = jnp.concatenate(rows, axis=0) if n_chunk > 1 else rows[0]
        return u * sp * _silu(gate)

    return f


def _sg_specs():
    full = lambda i: (0, 0)
    full3 = lambda i: (0, 0, 0)
    par = [pl.BlockSpec((1, D_SG), full), pl.BlockSpec((1, D_SG), full), pl.BlockSpec((4, SG_C, SG_C), full3),
           pl.BlockSpec((SG_C, LANES), full)]
    par_shapes = [(1, D_SG), (1, D_SG), (4, SG_C, SG_C), (SG_C, LANES)]
    return par, par_shapes


def _sg_fwd(zsg, params, tb, name):
    t = zsg.shape[0]
    f = _sg_fn(tb // SG_C)
    par, _ = _sg_specs()

    def body(z_ref, g_ref, b_ref, w_ref, bs_ref, y_ref):
        y_ref[...] = f(z_ref[...], g_ref[...], b_ref[...], w_ref[...], bs_ref[...])

    row = lambda i: (i, 0)
    return pl.pallas_call(
        body, grid=(t // tb,), in_specs=[pl.BlockSpec((tb, 3 * D_SG), row)] + par,
        out_specs=pl.BlockSpec((tb, D_SG), row), out_shape=jax.ShapeDtypeStruct((t, D_SG), F32),
        name=name, compiler_params=_cp("parallel"))(zsg, *params)


def _sg_bwd(zsg, dy, params, tb, name):
    t = zsg.shape[0]
    f = _sg_fn(tb // SG_C)
    par, par_shapes = _sg_specs()

    def body(z_ref, dy_ref, g_ref, b_ref, w_ref, bs_ref, dz_ref, dg_ref, db_ref, dw_ref, dbs_ref):
        accs = (dg_ref, db_ref, dw_ref, dbs_ref)

        @pl.when(pl.program_id(0) == 0)
        def _():
            for r in accs:
                r[...] = jnp.zeros_like(r)

        _, vjp = jax.vjp(f, z_ref[...], g_ref[...], b_ref[...], w_ref[...], bs_ref[...])
        cts = vjp(dy_ref[...])
        dz_ref[...] = cts[0].astype(BF16)
        for r, ct in zip(accs, cts[1:]):
            r[...] += ct

    row = lambda i: (i, 0)
    return pl.pallas_call(
        body, grid=(t // tb,), in_specs=[pl.BlockSpec((tb, 3 * D_SG), row), pl.BlockSpec((tb, D_SG), row)] + par,
        out_specs=[pl.BlockSpec((tb, 3 * D_SG), row)] + par,
        out_shape=[jax.ShapeDtypeStruct((t, 3 * D_SG), BF16)] + [jax.ShapeDtypeStruct(sh, F32) for sh in par_shapes],
        name=name, compiler_params=_cp("arbitrary"))(zsg, dy, *params)


def _out_fwd(x, ys, p, w_out, pg, w_gate, w_ple, tb, name):
    t = x.shape[0]

    def body(x_ref, y0, y1, y2, p_ref, wo_ref, pg_ref, wg_ref, wp_ref, o_ref, x1_ref, gate_ref):
        y = jnp.concatenate([y0[...], y1[...], y2[...]], axis=1).astype(BF16)
        x1 = x_ref[...] + jnp.dot(y, wo_ref[...], preferred_element_type=F32)
        hn = _rms(x1, pg_ref[...]).astype(BF16)
        gate = jax.nn.sigmoid(jnp.dot(hn, wg_ref[...], preferred_element_type=F32))
        pp = jnp.dot(p_ref[...].astype(BF16), wp_ref[...], preferred_element_type=F32)
        o_ref[...] = x1 + gate * pp
        x1_ref[...] = x1
        gate_ref[...] = gate

    row = lambda i: (i, 0)
    full = lambda i: (0, 0)
    return pl.pallas_call(
        body, grid=(t // tb,),
        in_specs=[pl.BlockSpec((tb, D), row), pl.BlockSpec((tb, D_SSM), row), pl.BlockSpec((tb, D_DN), row),
                  pl.BlockSpec((tb, D_SG), row), pl.BlockSpec((tb, D_PLE), row), pl.BlockSpec((D, D), full),
                  pl.BlockSpec((1, D), full), pl.BlockSpec((D, D), full), pl.BlockSpec((D_PLE, D), full)],
        out_specs=[pl.BlockSpec((tb, D), row)] * 3, out_shape=[jax.ShapeDtypeStruct((t, D), F32)] * 3,
        name=name, compiler_params=_cp("parallel"))(x, *ys, p, w_out, pg, w_gate, w_ple)


def _out_bwd(x1, gate, ys, p, dx2, w_out, pg, w_gate, w_ple, tb, name):
    t = x1.shape[0]

    def body(x1_ref, gate_ref, y0, y1, y2, p_ref, d_ref, wo_ref, pg_ref, wg_ref, wp_ref,
             dx_ref, dy0, dy1, dy2, dwo_ref, dpg_ref, dwg_ref, dwp_ref):
        accs = (dwo_ref, dpg_ref, dwg_ref, dwp_ref)

        @pl.when(pl.program_id(0) == 0)
        def _():
            for r in accs:
                r[...] = jnp.zeros_like(r)

        y = jnp.concatenate([y0[...], y1[...], y2[...]], axis=1).astype(BF16)
        hn, rms_vjp = jax.vjp(_rms, x1_ref[...], pg_ref[...])
        hb = hn.astype(BF16)
        gate = gate_ref[...]
        pb = p_ref[...].astype(BF16)
        pp = jnp.dot(pb, wp_ref[...], preferred_element_type=F32)
        d2 = d_ref[...]
        dpp = (d2 * gate).astype(BF16)
        dlog = (d2 * pp * gate * (1.0 - gate)).astype(BF16)
        dwp_ref[...] += _dg(pb, dpp, 0, 0)
        dwg_ref[...] += _dg(hb, dlog, 0, 0)
        dx1_n, dpg = rms_vjp(_dg(dlog, wg_ref[...], 1, 1))
        dpg_ref[...] += dpg
        dx1 = d2 + dx1_n
        dx_ref[...] = dx1
        db = dx1.astype(BF16)
        dwo_ref[...] += _dg(y, db, 0, 0)
        dy = _dg(db, wo_ref[...], 1, 1)
        dy0[...] = dy[:, :D_SSM]
        dy1[...] = dy[:, D_SSM:D_SSM + D_DN]
        dy2[...] = dy[:, D_SSM + D_DN:]

    row = lambda i: (i, 0)
    full = lambda i: (0, 0)
    acts = [pl.BlockSpec((tb, D), row), pl.BlockSpec((tb, D_SSM), row), pl.BlockSpec((tb, D_DN), row), pl.BlockSpec((tb, D_SG), row)]
    wts = [pl.BlockSpec((D, D), full), pl.BlockSpec((1, D), full), pl.BlockSpec((D, D), full), pl.BlockSpec((D_PLE, D), full)]
    return pl.pallas_call(
        body, grid=(t // tb,),
        in_specs=[pl.BlockSpec((tb, D), row)] + acts + [pl.BlockSpec((tb, D_PLE), row), pl.BlockSpec((tb, D), row)] + wts,
        out_specs=acts + wts,
        out_shape=[jax.ShapeDtypeStruct((t, n), F32) for n in (D, D_SSM, D_DN, D_SG)]
        + [jax.ShapeDtypeStruct(sh, F32) for sh in ((D, D), (1, D), (D, D), (D_PLE, D))],
        name=name, compiler_params=_cp("arbitrary"))(x1, gate, *ys, p, dx2, w_out, pg, w_gate, w_ple)


def _loss_head(x, fg, target, tb, name):
    t = x.shape[0]

    def body(x_ref, g_ref, t_ref, dx_ref, dg_ref, loss_ref):
        @pl.when(pl.program_id(0) == 0)
        def _():
            dg_ref[...] = jnp.zeros_like(dg_ref)
            loss_ref[...] = jnp.zeros_like(loss_ref)

        y, vjp = jax.vjp(_rms, x_ref[...], g_ref[...])
        err = y - t_ref[...]
        loss_ref[...] += jnp.zeros_like(loss_ref) + 0.5 * jnp.sum(err * err) / D
        dx, dg = vjp(err / D)
        dx_ref[...] = dx
        dg_ref[...] += dg

    row = lambda i: (i, 0)
    full = lambda i: (0, 0)
    return pl.pallas_call(
        body, grid=(t // tb,),
        in_specs=[pl.BlockSpec((tb, D), row), pl.BlockSpec((1, D), full), pl.BlockSpec((tb, D), row)],
        out_specs=[pl.BlockSpec((tb, D), row), pl.BlockSpec((1, D), full), pl.BlockSpec((1, LANES), full)],
        out_shape=[jax.ShapeDtypeStruct((t, D), F32), jax.ShapeDtypeStruct((1, D), F32), jax.ShapeDtypeStruct((1, LANES), F32)],
        name=name, compiler_params=_cp("arbitrary"))(x, fg, target)


def _hbm_specs(n):
    return [pl.BlockSpec(memory_space=pl.ANY)] * n


def _gather_protocol(ins, outs, send_sems, recv_sems, local_sems):
    n = len(ins)
    x, y, c = lax.axis_index("x"), lax.axis_index("y"), lax.axis_index("c")
    me, sibling = (x, y, c), (x, y, 1 - c)
    chips = [(1 - x, y), (x, 1 - y), (1 - x, 1 - y)]

    def slot(a, px, py, pc):
        return outs[a].at[4 * px + 2 * py + pc]

    def copy(a, k, blk, to, src=None):
        return pltpu.make_async_remote_copy(
            src_ref=slot(a, *blk) if src is None else src, dst_ref=slot(a, *blk),
            send_sem=send_sems.at[7 * a + k], recv_sem=recv_sems.at[7 * a + k],
            device_id=to, device_id_type=pl.DeviceIdType.MESH)

    mines = [pltpu.make_async_copy(ins[a], slot(a, *me), local_sems.at[a]) for a in range(n)]
    for cp in mines:
        cp.start()
    first = []
    for a in range(n):
        first.append(copy(a, 0, me, sibling, src=ins[a]))
        first += [copy(a, 1 + j, me, (*chip, c), src=ins[a]) for j, chip in enumerate(chips)]
    for cp in first:
        cp.start()
    yield
    passed = []
    for j, chip in enumerate(chips):
        for a in range(n):
            copy(a, 1 + j, (*chip, c), me).wait_recv()
            onward = copy(a, 4 + j, (*chip, c), sibling)
            onward.start()
            passed.append(onward)
    for a in range(n):
        copy(a, 0, sibling, me).wait_recv()
    for j, chip in enumerate(chips):
        for a in range(n):
            copy(a, 4 + j, (*chip, 1 - c), me).wait_recv()
    for cp in first + passed:
        cp.wait_send()
    for cp in mines:
        cp.wait()


def _gather_sems(n):
    return [pltpu.SemaphoreType.DMA((7 * n,)), pltpu.SemaphoreType.DMA((7 * n,)), pltpu.SemaphoreType.DMA((n,))]


def _all_gather(blocks, name):
    n = len(blocks)

    def body(*refs):
        for _ in _gather_protocol(refs[:n], refs[n:2 * n], *refs[2 * n:]):
            pass

    return pl.pallas_call(
        body, out_shape=[jax.ShapeDtypeStruct((N_DEV, *b.shape), b.dtype) for b in blocks],
        in_specs=_hbm_specs(n), out_specs=_hbm_specs(n), scratch_shapes=_gather_sems(n), name=name)(*blocks)


def _pair_exchange(gs, name):
    n = len(gs)

    def body(*refs):
        ins, recvs = refs[:n], refs[n:2 * n]
        send_sems, recv_sems = refs[2 * n:]
        x, y, c = lax.axis_index("x"), lax.axis_index("y"), lax.axis_index("c")
        remote = [pltpu.make_async_remote_copy(
            src_ref=ins[a], dst_ref=recvs[a], send_sem=send_sems.at[a], recv_sem=recv_sems.at[a],
            device_id=(x, y, 1 - c), device_id_type=pl.DeviceIdType.MESH) for a in range(n)]
        for cp in remote:
            cp.start()
        for cp in remote:
            cp.wait_send()
            cp.wait_recv()

    return pl.pallas_call(
        body, out_shape=[jax.ShapeDtypeStruct(g.shape, g.dtype) for g in gs], in_specs=_hbm_specs(n), out_specs=_hbm_specs(n),
        scratch_shapes=[pltpu.SemaphoreType.DMA((n,)), pltpu.SemaphoreType.DMA((n,))],
        name=name)(*gs)


def _chip_exchange(ps, gather, name):
    n, m = len(ps), len(gather)

    def body(*refs):
        ins, g_ins, outs, g_outs = refs[:n], refs[n:n + m], refs[n + m:2 * n + m], refs[2 * n + m:2 * (n + m)]
        send_sems, recv_sems, local_sems = refs[2 * (n + m):2 * (n + m) + 3]
        gathering = _gather_protocol(g_ins, g_outs, *refs[2 * (n + m) + 3:])
        x, y, c = lax.axis_index("x"), lax.axis_index("y"), lax.axis_index("c")
        my_chip = 2 * x + y
        local = [pltpu.make_async_copy(ins[a].at[my_chip], outs[a].at[my_chip], local_sems.at[a]) for a in range(n)]
        remote = []
        for j in range(1, 4):
            px = 1 - x if j & 2 else x
            py = 1 - y if j & 1 else y
            for a in range(n):
                remote.append(pltpu.make_async_remote_copy(
                    src_ref=ins[a].at[2 * px + py], dst_ref=outs[a].at[my_chip],
                    send_sem=send_sems.at[3 * a + j - 1], recv_sem=recv_sems.at[3 * a + j - 1],
                    device_id=(px, py, c), device_id_type=pl.DeviceIdType.MESH))
        for cp in local + remote:
            cp.start()
        for _ in gathering:
            pass
        for cp in remote:
            cp.wait_send()
            cp.wait_recv()
        for cp in local:
            cp.wait()

    outs = pl.pallas_call(
        body, out_shape=[jax.ShapeDtypeStruct(q.shape, q.dtype) for q in ps]
        + [jax.ShapeDtypeStruct((N_DEV, *b.shape), b.dtype) for b in gather],
        in_specs=_hbm_specs(n + m), out_specs=_hbm_specs(n + m),
        scratch_shapes=[pltpu.SemaphoreType.DMA((3 * n,)), pltpu.SemaphoreType.DMA((3 * n,)), pltpu.SemaphoreType.DMA((n,))]
        + _gather_sems(m), name=name)(*ps, *gather)
    return outs[:n], outs[n:]


def _row_block(rows, bytes_per_row):
    best = None
    for rb in range(16, rows + 1, 16):
        if rows % rb == 0 and rb * bytes_per_row <= ELEMENTWISE_STEP_BYTES:
            best = rb
    return rows if best is None else best


def _add_pair(own, recv, name):
    shape = own.shape
    last = shape[-1]
    rows = own.size // last
    rb = _row_block(rows, 3 * 4 * (-(-last // LANES) * LANES))

    def body(a_ref, b_ref, o_ref):
        o_ref[...] = (a_ref[...].astype(F32) + b_ref[...].astype(F32)).astype(o_ref.dtype)

    row = lambda i: (i, 0)
    out = pl.pallas_call(
        body, grid=(rows // rb,), in_specs=[pl.BlockSpec((rb, last), row)] * 2, out_specs=pl.BlockSpec((rb, last), row),
        out_shape=jax.ShapeDtypeStruct((rows, last), own.dtype), name=name,
        compiler_params=_cp("parallel"))(own.reshape(rows, last), recv.reshape(rows, last))
    return out.reshape(shape)


def _sum_adamw(gk, w, m, v, name):
    shape = w.shape
    n_part = gk.shape[0]
    last = shape[-1]
    rows = w.size // last
    rb = _row_block(rows, (n_part + 7) * 4 * (-(-last // LANES) * LANES))

    def body(g_ref, w_ref, m_ref, v_ref, go_ref, d_ref, mo_ref, vo_ref):
        g = g_ref[0].astype(F32)
        for k in range(1, n_part):
            g = g + g_ref[k].astype(F32)
        mn = ADAM_B1 * m_ref[...] + (1.0 - ADAM_B1) * g
        vn = ADAM_B2 * v_ref[...] + (1.0 - ADAM_B2) * jnp.square(g)
        m_hat = mn / (1.0 - ADAM_B1 ** ADAM_STEP)
        v_hat = vn / (1.0 - ADAM_B2 ** ADAM_STEP)
        go_ref[...] = g
        d_ref[...] = -ADAM_LR * (m_hat / (jnp.sqrt(v_hat) + ADAM_EPS) + ADAM_WD * w_ref[...])
        mo_ref[...] = mn
        vo_ref[...] = vn

    row = lambda i: (i, 0)
    outs = pl.pallas_call(
        body, grid=(rows // rb,),
        in_specs=[pl.BlockSpec((n_part, rb, last), lambda i: (0, i, 0))] + [pl.BlockSpec((rb, last), row)] * 3,
        out_specs=[pl.BlockSpec((rb, last), row)] * 4,
        out_shape=[jax.ShapeDtypeStruct((rows, last), F32)] * 4,
        name=name, compiler_params=_cp("parallel"))(gk.reshape(n_part, rows, last), *[a.reshape(rows, last) for a in (w, m, v)])
    return [o.reshape(shape) for o in outs]


def _seg_rows(shape):
    n = 1
    for d in shape:
        n *= d
    return -(-n // (8 * LANES)) * 8


def _pack(arrs):
    segs = []
    for a in arrs:
        r = _seg_rows(a.shape)
        segs.append(jnp.pad(a.reshape(-1).astype(F32), (0, r * LANES - a.size)).reshape(r, LANES))
    rows = sum(s.shape[0] for s in segs)
    total = -(-rows // PACK_ROWS) * PACK_ROWS
    if total > rows:
        segs.append(jnp.zeros((total - rows, LANES), F32))
    return jnp.concatenate(segs, axis=0)


def _unpack(pack, shapes):
    out, off = [], 0
    for sh in shapes:
        r = _seg_rows(sh)
        n = 1
        for d in sh:
            n *= d
        out.append(pack[off:off + r].reshape(-1)[:n].reshape(sh))
        off += r
    return out


def _to_dest_blocks(full, axis, dtype):
    sh = list(full.shape)
    sh[axis:axis + 1] = [N_DEV // 2, 2, sh[axis] // N_DEV]
    return jnp.moveaxis(full.reshape(sh), (axis, axis + 1), (1, 0)).astype(dtype)


def _from_gathered(g, axis):
    m = jnp.moveaxis(g, 0, axis)
    sh = list(m.shape)
    sh[axis:axis + 2] = [sh[axis] * sh[axis + 1]]
    return m.reshape(sh)


D_IN = 3336
W_IN_SHARD = D_IN // N_DEV
W_IN_MOVES = ((0, 2048, 0), (2048, 2056, 3328), (2056, D_IN, 2048))


def _w_in_windows(k):
    lo, hi = k * W_IN_SHARD, (k + 1) * W_IN_SHARD
    out = []
    for a, b, mine in W_IN_MOVES:
        a2, b2 = max(a, lo), min(b, hi)
        if b2 > a2:
            out.append((a2 - lo, b2 - a2, mine + a2 - a))
    return out


def _assemble_w_in(gathered, name):
    depth = gathered.shape[1]
    rb = 256

    def body(g_ref, o_ref):
        o_ref[0, :, D_IN:] = jnp.zeros((rb, ZW - D_IN), o_ref.dtype)
        for k in range(N_DEV):
            for off, width, mine in _w_in_windows(k):
                o_ref[0, :, mine:mine + width] = g_ref[k, 0, :, off:off + width]

    return pl.pallas_call(
        body, grid=(depth, D // rb),
        in_specs=[pl.BlockSpec((N_DEV, 1, rb, W_IN_SHARD), lambda l, i: (0, l, i, 0))],
        out_specs=pl.BlockSpec((1, rb, ZW), lambda l, i: (l, i, 0)),
        out_shape=jax.ShapeDtypeStruct((depth, D, ZW), gathered.dtype),
        name=name, compiler_params=_cp("parallel", "parallel"))(gathered)


def _split_dw_in(dws, name):
    depth = len(dws)
    rb = 128

    def body(*refs):
        o_ref = refs[-1]
        for l in range(depth):
            pieces = refs[5 * l:5 * l + 5]
            for k in range(N_DEV):
                for off, width, mine in _w_in_windows(k):
                    for p_ref, start, n in zip(pieces, Z_OFFSETS, Z_PIECES):
                        a, b = max(mine, start), min(mine + width, start + n)
                        if b > a:
                            o_ref[k % 2, k // 2, l, :, off + a - mine:off + b - mine] = (
                                p_ref[:, a - start:b - start].astype(o_ref.dtype))

    row = lambda i: (i, 0)
    flat = [piece for layer in dws for piece in layer]
    return pl.pallas_call(
        body, grid=(D // rb,),
        in_specs=[pl.BlockSpec((rb, n), row) for _ in range(depth) for n in Z_PIECES],
        out_specs=pl.BlockSpec((2, N_DEV // 2, depth, rb, W_IN_SHARD), lambda i: (0, 0, 0, i, 0)),
        out_shape=jax.ShapeDtypeStruct((2, N_DEV // 2, depth, D, W_IN_SHARD), WIRE['w_in']),
        name=name, compiler_params=_cp("parallel"))(*flat)


def _local_step(x, p, wts, target):
    bl, s, _ = x.shape
    t = bl * s
    depth = p.shape[0]
    tb, sg_tb = TB, SG_TB

    def by_example(a):
        return a.reshape(bl, s, a.shape[-1])

    def flat(a):
        return a.reshape(t, a.shape[-1])

    xs = [x.reshape(t, D)]
    saved = []
    for i in range(depth):
        li = f"l{i}"
        ng = wts['norm_g'][i].reshape(1, D)
        w_in = wts['w_in'][i]
        s5_par_in = (wts['ssm_a_re'][i], wts['ssm_a_im'][i], wts['ssm_b_re'][i], wts['ssm_b_im'][i],
                     wts['ssm_c_re'][i], wts['ssm_c_im'][i], wts['ssm_d'][i], wts['ssm_log_step'][i])
        tabs, tab_vjp = jax.vjp(_s5_tables, *s5_par_in)
        s5_par = (*tabs, wts['ssm_w_glu'][i], wts['ssm_b_glu'][i].reshape(1, D_SSM))
        s5_const = _s5_powers(wts['ssm_a_re'][i], wts['ssm_a_im'][i], wts['ssm_log_step'][i])
        conv8 = jnp.pad(wts['dn_conv_w'][i], ((0, 4), (0, 0)))
        dn_par = (jnp.repeat(wts['dn_a_log'][i], DH).reshape(1, D_DN), jnp.repeat(wts['dn_dt_bias'][i], DH).reshape(1, D_DN),
                  wts['dn_norm_g'][i].reshape(1, DH))
        sg_par = (wts['sg_ln_g'][i].reshape(1, D_SG), wts['sg_ln_b'][i].reshape(1, D_SG), wts['sg_w'][i],
                  jnp.pad(jnp.transpose(wts['sg_b'][i]), ((0, 0), (0, LANES - 4))))
        out_par = (wts['w_out'][i].astype(BF16), wts['ple_norm_g'][i].reshape(1, D), wts['w_ple_gate'][i].astype(BF16),
                   wts['w_ple'][i].astype(BF16))
        pi = p[i].reshape(t, D_PLE)

        z_ssm, z_qkv, z_gdn, z_sg, z_ab, qkvn = _in_proj_fwd(xs[i], ng, w_in, conv8, s, tb, f"in_proj_fwd_{li}")
        y_ssm, carries, h_all, y_dn, states, tinvs = _mix_fwd(by_example(z_ssm), s5_par, s5_const, by_example(qkvn),
                                                              by_example(z_ab), by_example(z_gdn), dn_par, bl, s, f"mix_fwd_{li}")
        y_sg = _sg_fwd(z_sg, sg_par, sg_tb, f"sg_fwd_{li}")
        ys = (flat(y_ssm), flat(y_dn), y_sg)
        x_next, x1, gate = _out_fwd(xs[i], ys, pi, *out_par, tb, f"out_fwd_{li}")
        xs.append(x_next)
        saved.append(dict(ng=ng, w_in=w_in, tab_vjp=tab_vjp, s5_par=s5_par, s5_const=s5_const, conv8=conv8, dn_par=dn_par,
                          sg_par=sg_par, out_par=out_par, pi=pi, z=(z_ssm, z_qkv, z_gdn, z_sg, z_ab), carries=carries,
                          h_all=h_all, qkvn=qkvn, x1=x1, gate=gate,
                          states=states, tinvs=tinvs, ys=ys))

    dx, dfg, loss_vec = _loss_head(xs[depth], wts['final_norm_g'].reshape(1, D), target.reshape(t, D), tb, "loss_head")
    grads = {n: [None] * depth for n in WEIGHTS if n != 'final_norm_g'}
    grads['final_norm_g'] = dfg.reshape(D)
    for i in reversed(range(depth)):
        li = f"l{i}"
        sv = saved[i]
        z_ssm, z_qkv, z_gdn, z_sg, z_ab = sv['z']
        dx_res, dy_ssm, dy_dn, dy_sg, dwo, dpg, dwg, dwp = _out_bwd(sv['x1'], sv['gate'], sv['ys'], sv['pi'], dx, *sv['out_par'], tb,
                                                                    f"out_bwd_{li}")
        dz_sg, dlng, dlnb, dsgw, dbsp = _sg_bwd(z_sg, dy_sg, sv['sg_par'], sg_tb, f"sg_bwd_{li}")
        (dz_ssm, dbb, dcb, dlam, ddv, dwglu, dbglu), (dqkvn, dz_ab, dz_gdn, dal, ddt, dng) = _mix_bwd(
            by_example(z_ssm), sv['carries'], sv['h_all'], by_example(dy_ssm), sv['s5_par'], sv['s5_const'],
            by_example(sv['qkvn']), by_example(z_ab), by_example(z_gdn), sv['states'], sv['tinvs'], by_example(dy_dn),
            sv['dn_par'], bl, s, f"mix_bwd_{li}")
        dx, dnorm, dz_qkv, dconv = _in_proj_bwd_dx(xs[i], sv['ng'], sv['w_in'], sv['conv8'], flat(dz_ssm), flat(dz_gdn), dz_sg,
                                                   flat(dz_ab), z_qkv, flat(dqkvn), dx_res, s, tb, f"in_proj_bwd_dx_{li}")
        dzs = (flat(dz_ssm), dz_qkv, flat(dz_gdn), dz_sg, flat(dz_ab))
        dws = _in_proj_bwd_dw(xs[i], sv['ng'], dzs, min(TB_DW, t), f"in_proj_bwd_dw_{li}")
        ds5 = sv['tab_vjp']((dbb, dcb, dlam, ddv))
        for n, gval in zip(('ssm_a_re', 'ssm_a_im', 'ssm_b_re', 'ssm_b_im', 'ssm_c_re', 'ssm_c_im', 'ssm_d', 'ssm_log_step'), ds5):
            grads[n][i] = gval
        grads['norm_g'][i] = dnorm.reshape(D)
        grads['w_in'][i] = dws
        grads['ssm_w_glu'][i] = dwglu
        grads['ssm_b_glu'][i] = dbglu.reshape(D_SSM)
        grads['dn_conv_w'][i] = dconv[:4]
        grads['dn_a_log'][i] = dal.reshape(H, DH).sum(axis=1)
        grads['dn_dt_bias'][i] = ddt.reshape(H, DH).sum(axis=1)
        grads['dn_norm_g'][i] = dng.reshape(DH)
        grads['sg_ln_g'][i] = dlng.reshape(D_SG)
        grads['sg_ln_b'][i] = dlnb.reshape(D_SG)
        grads['sg_w'][i] = dsgw
        grads['sg_b'][i] = jnp.transpose(dbsp[:, :4])
        grads['w_out'][i] = dwo
        grads['ple_norm_g'][i] = dpg.reshape(D)
        grads['w_ple_gate'][i] = dwg
        grads['w_ple'][i] = dwp
    grads = {n: (g if n in ('final_norm_g', 'w_in') else jnp.stack(g)) for n, g in grads.items()}
    return loss_vec[0, 0], dx.reshape(bl, s, D), grads


def kernel(x, p, norm_g, w_in, ssm_a_re, ssm_a_im, ssm_b_re, ssm_b_im, ssm_c_re, ssm_c_im, ssm_d, ssm_log_step, ssm_w_glu, ssm_b_glu, dn_conv_w, dn_a_log, dn_dt_bias, dn_norm_g, sg_ln_g, sg_ln_b, sg_w, sg_b, w_out, ple_norm_g, w_ple_gate, w_ple, final_norm_g, loss_target, m_norm_g, m_w_in, m_ssm_a_re, m_ssm_a_im, m_ssm_b_re, m_ssm_b_im, m_ssm_c_re, m_ssm_c_im, m_ssm_d, m_ssm_log_step, m_ssm_w_glu, m_ssm_b_glu, m_dn_conv_w, m_dn_a_log, m_dn_dt_bias, m_dn_norm_g, m_sg_ln_g, m_sg_ln_b, m_sg_w, m_sg_b, m_w_out, m_ple_norm_g, m_w_ple_gate, m_w_ple, m_final_norm_g, v_norm_g, v_w_in, v_ssm_a_re, v_ssm_a_im, v_ssm_b_re, v_ssm_b_im, v_ssm_c_re, v_ssm_c_im, v_ssm_d, v_ssm_log_step, v_ssm_w_glu, v_ssm_b_glu, v_dn_conv_w, v_dn_a_log, v_dn_dt_bias, v_dn_norm_g, v_sg_ln_g, v_sg_ln_b, v_sg_w, v_sg_b, v_w_out, v_ple_norm_g, v_w_ple_gate, v_w_ple, v_final_norm_g):
    w_loc = dict(zip(WEIGHTS, (norm_g, w_in, ssm_a_re, ssm_a_im, ssm_b_re, ssm_b_im, ssm_c_re, ssm_c_im, ssm_d, ssm_log_step,
                               ssm_w_glu, ssm_b_glu, dn_conv_w, dn_a_log, dn_dt_bias, dn_norm_g, sg_ln_g, sg_ln_b, sg_w, sg_b,
                               w_out, ple_norm_g, w_ple_gate, w_ple, final_norm_g)))
    m_loc = dict(zip(WEIGHTS, (m_norm_g, m_w_in, m_ssm_a_re, m_ssm_a_im, m_ssm_b_re, m_ssm_b_im, m_ssm_c_re, m_ssm_c_im, m_ssm_d,
                               m_ssm_log_step, m_ssm_w_glu, m_ssm_b_glu, m_dn_conv_w, m_dn_a_log, m_dn_dt_bias, m_dn_norm_g,
                               m_sg_ln_g, m_sg_ln_b, m_sg_w, m_sg_b, m_w_out, m_ple_norm_g, m_w_ple_gate, m_w_ple, m_final_norm_g)))
    v_loc = dict(zip(WEIGHTS, (v_norm_g, v_w_in, v_ssm_a_re, v_ssm_a_im, v_ssm_b_re, v_ssm_b_im, v_ssm_c_re, v_ssm_c_im, v_ssm_d,
                               v_ssm_log_step, v_ssm_w_glu, v_ssm_b_glu, v_dn_conv_w, v_dn_a_log, v_dn_dt_bias, v_dn_norm_g,
                               v_sg_ln_g, v_sg_ln_b, v_sg_w, v_sg_b, v_w_out, v_ple_norm_g, v_w_ple_gate, v_w_ple, v_final_norm_g)))

    gathered = _all_gather([w_loc[n].astype(WIRE[n]) for n in SHARDED_ORDER], "gather_weights")
    full = dict(w_loc)
    for n, g in zip(SHARDED_ORDER, gathered):
        full[n] = _assemble_w_in(g, "assemble_w_in") if n == 'w_in' else _from_gathered(g, SHARDED[n])
    full['ssm_w_glu'] = full['ssm_w_glu'].astype(F32)

    loss_part, grad_x, grads = _local_step(x, p, full, loss_target)

    dest = [_split_dw_in(grads[n], "split_dw_in") if n == 'w_in' else _to_dest_blocks(grads[n], SHARDED[n], WIRE[n])
            for n in SHARDED_ORDER]
    c = lax.axis_index("c")
    own = [lax.dynamic_index_in_dim(d, c, 0, keepdims=False) for d in dest]
    for_sibling = [lax.dynamic_index_in_dim(d, 1 - c, 0, keepdims=False) for d in dest]
    from_sibling = _pair_exchange(for_sibling, "grads_pair_exchange")
    chip_sums = [_add_pair(a, b, f"grads_pair_sum_{n}") for n, a, b in zip(SHARDED_ORDER, own, from_sibling)]
    rep_pack = _pack([grads[n] for n in REPLICATED_ORDER] + [loss_part.reshape(1)])
    by_chip, (rep_recv,) = _chip_exchange(chip_sums, [rep_pack], "grads_chip_exchange")

    outs = {k: {} for k in 'gdmv'}
    for n, gk in zip(SHARDED_ORDER, by_chip):
        for k, o in zip('gdmv', _sum_adamw(gk, w_loc[n], m_loc[n], v_loc[n], f"adamw_{n}")):
            outs[k][n] = o
    one = jnp.zeros((1,), F32)
    rep_out = _sum_adamw(rep_recv, _pack([w_loc[n] for n in REPLICATED_ORDER] + [one]),
                         _pack([m_loc[n] for n in REPLICATED_ORDER] + [one]),
                         _pack([v_loc[n] for n in REPLICATED_ORDER] + [one]), "adamw_replicated")
    rep_shapes = [w_loc[n].shape for n in REPLICATED_ORDER] + [(1,)]
    for k, rep_p in zip('gdmv', rep_out):
        outs[k].update(zip(REPLICATED_ORDER + ['loss'], _unpack(rep_p, rep_shapes)))
    loss = outs['g']['loss'].reshape(())
    return (loss, grad_x, *[outs['g'][n] for n in WEIGHTS], *[outs['d'][n] for n in WEIGHTS],
            *[outs['m'][n] for n in WEIGHTS], *[outs['v'][n] for n in WEIGHTS])
```

```python
import functools

import jax
import jax.numpy as jnp
from jax import lax
from jax.experimental import pallas as pl
from jax.experimental.pallas import tpu as pltpu

F32 = jnp.float32
BF16 = jnp.bfloat16
EPS = 1e-6

D = 1024
D_PLE = 256
D_SSM = 256
D_DN = 512
D_SG = 256
G = 16
CG = 16
NS = 64
NRE = G * NS
H = 4
DH = 128
DN_C = 128
SG_C = 128
ZW = 3456
Z_PIECES = (512, 1536, 512, 768, 128)
N_DEV = 8
LANES = 128
PACK_ROWS = 256
VMEM_LIMIT = 56 * 1024 * 1024
ELEMENTWISE_STEP_BYTES = 4 * 1024 * 1024
TB = 256
SG_TB = 512
TB_DW = 512

ADAM_LR = 0.001
ADAM_B1 = 0.9
ADAM_B2 = 0.999
ADAM_EPS = 1e-08
ADAM_WD = 0.01
ADAM_STEP = 10

MIX_HEAD_START = 3
S5_L = 128
S5_GROUP = 8
S5_SHIFTS = (1, 2, 4)

WEIGHTS = ['norm_g', 'w_in', 'ssm_a_re', 'ssm_a_im', 'ssm_b_re', 'ssm_b_im', 'ssm_c_re', 'ssm_c_im', 'ssm_d',
           'ssm_log_step', 'ssm_w_glu', 'ssm_b_glu', 'dn_conv_w', 'dn_a_log', 'dn_dt_bias', 'dn_norm_g', 'sg_ln_g',
           'sg_ln_b', 'sg_w', 'sg_b', 'w_out', 'ple_norm_g', 'w_ple_gate', 'w_ple', 'final_norm_g']
SHARDED = {'w_in': 2, 'ssm_w_glu': 1, 'dn_conv_w': 2, 'w_out': 1, 'w_ple_gate': 1, 'w_ple': 2}
SHARDED_ORDER = ['w_in', 'ssm_w_glu', 'dn_conv_w', 'w_out', 'w_ple_gate', 'w_ple']
WIRE = {'w_in': BF16, 'ssm_w_glu': BF16, 'dn_conv_w': F32, 'w_out': BF16, 'w_ple_gate': BF16, 'w_ple': BF16}
REPLICATED_ORDER = [n for n in WEIGHTS if n not in SHARDED]


def _cp(*sem):
    return pltpu.CompilerParams(dimension_semantics=sem, vmem_limit_bytes=VMEM_LIMIT)


def _dg(a, b, ca, cb, precision=None):
    return lax.dot_general(a, b, (((ca,), (cb,)), ((), ())), precision=precision, preferred_element_type=F32)


@jax.custom_vjp
def _mm(a, b):
    return _dg(a.astype(BF16), b.astype(BF16), 1, 0)


def _mm_fwd(a, b):
    return _mm(a, b), (a, b)


def _mm_bwd(res, g):
    a, b = res
    gb = g.astype(BF16)
    return _dg(gb, b.astype(BF16), 1, 1), _dg(a.astype(BF16), gb, 0, 0)


_mm.defvjp(_mm_fwd, _mm_bwd)


@jax.custom_vjp
def _mm_nt(a, b):
    return _dg(a.astype(BF16), b.astype(BF16), 1, 1)


def _mm_nt_fwd(a, b):
    return _mm_nt(a, b), (a, b)


def _mm_nt_bwd(res, g):
    a, b = res
    gb = g.astype(BF16)
    return _dg(gb, b.astype(BF16), 1, 0), _dg(gb, a.astype(BF16), 0, 0)


_mm_nt.defvjp(_mm_nt_fwd, _mm_nt_bwd)


@jax.custom_vjp
def _mm_tn(a, b):
    return _dg(a.astype(BF16), b.astype(BF16), 0, 0)


def _mm_tn_fwd(a, b):
    return _mm_tn(a, b), (a, b)


def _mm_tn_bwd(res, g):
    a, b = res
    gb = g.astype(BF16)
    return _dg(b.astype(BF16), gb, 1, 1), _dg(a.astype(BF16), gb, 1, 0)


_mm_tn.defvjp(_mm_tn_fwd, _mm_tn_bwd)


def _split(x, n):
    pieces = []
    for _ in range(n - 1):
        hi = x.astype(BF16)
        pieces.append(hi)
        x = x - hi.astype(F32)
    pieces.append(x.astype(BF16))
    return pieces


def _dg3(a, b, ca, cb):
    a_hi, a_lo = _split(a, 2)
    b_hi, b_lo = _split(b, 2)
    return _dg(a_hi, b_hi, ca, cb) + (_dg(a_hi, b_lo, ca, cb) + _dg(a_lo, b_hi, ca, cb))


@jax.custom_vjp
def _dot3(a, b):
    return _dg3(a, b, 1, 0)


def _dot3_fwd(a, b):
    return _dot3(a, b), (a, b)


def _dot3_bwd(res, g):
    a, b = res
    return _dg3(g, b, 1, 1), _dg3(a, g, 0, 0)


_dot3.defvjp(_dot3_fwd, _dot3_bwd)


def _dg_sel(x, e, cx, ce, x_first):
    eb = e.astype(BF16)
    out = None
    for piece in reversed(_split(x, 3)):
        term = _dg(piece, eb, cx, ce) if x_first else _dg(eb, piece, ce, cx)
        out = term if out is None else out + term
    return out


@jax.custom_vjp
def _sel_r(x, e):
    return _dg_sel(x, e, 1, 0, True)


def _sel_r_fwd(x, e):
    return _sel_r(x, e), e


def _sel_r_bwd(e, g):
    return _dg_sel(g, e, 1, 1, True), jnp.zeros_like(e)


_sel_r.defvjp(_sel_r_fwd, _sel_r_bwd)


@jax.custom_vjp
def _sel_l(e, x):
    return _dg_sel(x, e, 0, 1, False)


def _sel_l_fwd(e, x):
    return _sel_l(e, x), e


def _sel_l_bwd(e, g):
    return jnp.zeros_like(e), _dg_sel(g, e, 0, 0, False)


_sel_l.defvjp(_sel_l_fwd, _sel_l_bwd)


def _rms(x, g):
    return x * lax.rsqrt(jnp.mean(x * x, axis=-1, keepdims=True) + EPS) * g


def _silu(x):
    return x * jax.nn.sigmoid(x)


Z_OFFSETS = (0, 512, 2048, 2560, 3328)


def _dn_post(c):
    s = _silu(c)
    parts = []
    for j in range(12):
        xj = s[:, j * DH:(j + 1) * DH]
        if j < 8:
            xj = xj * lax.rsqrt(jnp.sum(xj * xj, axis=-1, keepdims=True) + EPS)
        if j < 4:
            xj = xj * (DH ** -0.5)
        parts.append(xj)
    return jnp.concatenate(parts, axis=1)


def _dn_conv(ext, cw_ref, rows):
    c = None
    for k in range(4):
        sh = ext if k == 3 else pltpu.roll(ext, 3 - k, 0)
        term = cw_ref[k:k + 1, :] * sh[ext.shape[0] - rows:, :]
        c = term if c is None else c + term
    return c


def _dn_prep_vjp(prev, cur, nxt, d_cur, d_nxt, cw_ref, tb):
    ext = jnp.concatenate([prev, cur, nxt], axis=0)
    shifted = [ext if k == 3 else pltpu.roll(ext, 3 - k, 0) for k in range(4)]
    c2 = None
    for k in range(4):
        term = cw_ref[k:k + 1, :] * shifted[k][8:, :]
        c2 = term if c2 is None else c2 + term
    _, vjp = jax.vjp(_dn_post, c2)
    (dc2,) = vjp(jnp.concatenate([d_cur, d_nxt], axis=0))
    dz, dcw = None, []
    for k in range(4):
        up = dc2 if k == 3 else pltpu.roll(dc2, tb + 8 - (3 - k), 0)
        term = cw_ref[k:k + 1, :] * up[:tb, :]
        dz = term if dz is None else dz + term
        dcw.append(jnp.sum(dc2[:tb, :] * shifted[k][8:8 + tb, :], axis=0, keepdims=True))
    return dz, dcw


def _in_proj_fwd(x, g, w, conv_w8, s, tb, name):
    t = x.shape[0]
    n_s = s // tb
    w3 = 3 * D_DN
    q0, q1 = Z_OFFSETS[1], Z_OFFSETS[2]

    def body(x_ref, g_ref, w_ref, cw_ref, zs_ref, zq_ref, zg_ref, zsg_ref, zab_ref, qkvn_ref, halo):
        h = _rms(x_ref[...], g_ref[...]).astype(BF16)
        zq = jnp.dot(h, w_ref[:, q0:q1], preferred_element_type=F32)
        zq_ref[...] = zq
        prev = jnp.where(pl.program_id(0) % n_s == 0, 0.0, halo[...])
        qkvn_ref[...] = _dn_post(_dn_conv(jnp.concatenate([prev, zq], axis=0), cw_ref, tb))
        halo[...] = zq[tb - 8:, :]
        zs_ref[...] = jnp.dot(h, w_ref[:, :q0], preferred_element_type=F32)
        rest = jnp.dot(h, w_ref[:, q1:], preferred_element_type=F32)
        zg_ref[...] = rest[:, :Z_PIECES[2]]
        zsg_ref[...] = rest[:, Z_PIECES[2]:Z_PIECES[2] + Z_PIECES[3]]
        zab_ref[...] = rest[:, Z_PIECES[2] + Z_PIECES[3]:]

    row = lambda i: (i, 0)
    full = lambda i: (0, 0)
    widths = Z_PIECES + (w3,)
    return pl.pallas_call(
        body, grid=(t // tb,),
        in_specs=[pl.BlockSpec((tb, D), row), pl.BlockSpec((1, D), full), pl.BlockSpec((D, ZW), full), pl.BlockSpec((8, w3), full)],
        out_specs=[pl.BlockSpec((tb, n), row) for n in widths],
        out_shape=[jax.ShapeDtypeStruct((t, n), F32) for n in widths],
        scratch_shapes=[pltpu.VMEM((8, w3), F32)],
        name=name, compiler_params=_cp("arbitrary"))(x, g, w, conv_w8)


def _in_proj_bwd_dx(x, g, w, conv_w8, dz_ssm, dz_gdn, dz_sg, dz_ab, zq, dqkvn, dx_res, s, tb, name):
    t = x.shape[0]
    n_s = s // tb
    hb = tb // 8
    w3 = 3 * D_DN
    q0, q1 = Z_OFFSETS[1], Z_OFFSETS[2]

    def body(x_ref, g_ref, w_ref, cw_ref, ds_ref, dgd_ref, dsg_ref, dab_ref, cur_ref, prev_ref, next_ref, dq_ref, dqn_ref,
             dxr_ref, dx_ref, dg_ref, dzq_ref, dcw_ref):
        i = pl.program_id(0)

        @pl.when(i == 0)
        def _():
            dg_ref[...] = jnp.zeros_like(dg_ref)
            dcw_ref[...] = jnp.zeros_like(dcw_ref)

        rest = jnp.concatenate([dgd_ref[...], dsg_ref[...], dab_ref[...]], axis=1)
        dh = _dg(ds_ref[...], w_ref[:, :q0], 1, 1) + _dg(rest, w_ref[:, q1:], 1, 1)
        first, last = i % n_s == 0, i % n_s == n_s - 1
        dzq, dcw = _dn_prep_vjp(jnp.where(first, 0.0, prev_ref[...]), cur_ref[...], jnp.where(last, 0.0, next_ref[...]),
                                dq_ref[...], jnp.where(last, 0.0, dqn_ref[...]), cw_ref, tb)
        for k in range(4):
            dcw_ref[k:k + 1, :] += dcw[k]
        dzq = dzq.astype(BF16)
        dzq_ref[...] = dzq
        dh = dh + _dg(dzq, w_ref[:, q0:q1], 1, 1)
        _, vjp = jax.vjp(_rms, x_ref[...], g_ref[...])
        dx, dg = vjp(dh)
        dx_ref[...] = dx + dxr_ref[...]
        dg_ref[...] += dg

    n_blk8 = t // 8
    row = lambda i: (i, 0)
    prv = lambda i: (jnp.maximum(i * hb - 1, 0), 0)
    nxt = lambda i: (jnp.minimum((i + 1) * hb, n_blk8 - 1), 0)
    full = lambda i: (0, 0)
    return pl.pallas_call(
        body, grid=(t // tb,),
        in_specs=[pl.BlockSpec((tb, D), row), pl.BlockSpec((1, D), full), pl.BlockSpec((D, ZW), full), pl.BlockSpec((8, w3), full)]
        + [pl.BlockSpec((tb, n), row) for n in (Z_PIECES[0], Z_PIECES[2], Z_PIECES[3], Z_PIECES[4])]
        + [pl.BlockSpec((tb, w3), row), pl.BlockSpec((8, w3), prv), pl.BlockSpec((8, w3), nxt),
           pl.BlockSpec((tb, w3), row), pl.BlockSpec((8, w3), nxt), pl.BlockSpec((tb, D), row)],
        out_specs=[pl.BlockSpec((tb, D), row), pl.BlockSpec((1, D), full), pl.BlockSpec((tb, w3), row), pl.BlockSpec((8, w3), full)],
        out_shape=[jax.ShapeDtypeStruct((t, D), F32), jax.ShapeDtypeStruct((1, D), F32), jax.ShapeDtypeStruct((t, w3), BF16),
                   jax.ShapeDtypeStruct((8, w3), F32)],
        name=name, compiler_params=_cp("arbitrary"))(x, g, w, conv_w8, dz_ssm, dz_gdn, dz_sg, dz_ab, zq, zq, zq, dqkvn, dqkvn, dx_res)


def _in_proj_bwd_dw(x, g, dzs, tb, name):
    t = x.shape[0]

    def body(x_ref, g_ref, d0, d1, d2, d3, d4, *dw_refs):
        @pl.when(pl.program_id(0) == 0)
        def _():
            for r in dw_refs:
                r[...] = jnp.zeros_like(r)

        h = _rms(x_ref[...], g_ref[...]).astype(BF16)
        for d_ref, dw_ref in zip((d0, d1, d2, d3, d4), dw_refs):
            dw_ref[...] += _dg(h, d_ref[...].astype(BF16), 0, 0)

    row = lambda i: (i, 0)
    full = lambda i: (0, 0)
    return pl.pallas_call(
        body, grid=(t // tb,),
        in_specs=[pl.BlockSpec((tb, D), row), pl.BlockSpec((1, D), full)] + [pl.BlockSpec((tb, n), row) for n in Z_PIECES],
        out_specs=[pl.BlockSpec((D, n), full) for n in Z_PIECES],
        out_shape=[jax.ShapeDtypeStruct((D, n), F32) for n in Z_PIECES],
        name=name, compiler_params=_cp("arbitrary"))(x, g, *dzs)


def _lam_pow(a_re, a_im, log_step, k):
    step = jnp.exp(log_step)[:, None]
    mag = jnp.exp(k * a_re * step)
    ang = k * a_im * step
    return mag * jnp.cos(ang), mag * jnp.sin(ang)


def _s5_powers(a_re, a_im, log_step):
    def table(ks):
        re, im = _lam_pow(a_re, a_im, log_step, jnp.asarray(ks, F32)[:, None, None])
        return jnp.concatenate([re.reshape(len(ks), NRE), im.reshape(len(ks), NRE)], axis=-1)

    ld = table(S5_SHIFTS).reshape(len(S5_SHIFTS), 1, 2 * NRE)
    return ld, table(range(1, S5_GROUP + 1)), table(range(S5_GROUP, 0, -1))


def _s5_tables(a_re, a_im, b_re, b_im, c_re, c_im, d_skip, log_step):
    lam_re, lam_im = _lam_pow(a_re, a_im, log_step, 1.0)
    den = a_re * a_re + a_im * a_im
    nr, ni = lam_re - 1.0, lam_im
    f_re = (nr * a_re + ni * a_im) / den
    f_im = (ni * a_re - nr * a_im) / den
    bbar_re = f_re[..., None] * b_re - f_im[..., None] * b_im
    bbar_im = f_re[..., None] * b_im + f_im[..., None] * b_re
    eye = jnp.eye(G, dtype=F32)

    def blk_b(bb):
        return (jnp.transpose(bb, (0, 2, 1))[:, :, None, :] * eye[:, None, :, None]).reshape(D_SSM, NRE)

    def blk_c(cc):
        return (jnp.transpose(cc, (0, 2, 1))[:, :, None, :] * eye[:, None, :, None]).reshape(NRE, D_SSM)

    b_blk = jnp.concatenate([blk_b(bbar_re), blk_b(bbar_im)], axis=1)
    c_blk = jnp.concatenate([blk_c(c_re), -blk_c(c_im)], axis=0)
    lam = jnp.concatenate([lam_re.reshape(1, NRE), lam_im.reshape(1, NRE)], axis=-1)
    return b_blk, c_blk, lam, d_skip.reshape(1, D_SSM)


def _group_shift(x, d, up=False):
    r = lax.broadcasted_iota(jnp.int32, x.shape, 0) & (S5_GROUP - 1)
    if up:
        return jnp.where(r < S5_GROUP - d, pltpu.roll(x, x.shape[0] - d, 0), 0.0)
    return jnp.where(r >= d, pltpu.roll(x, d, 0), 0.0)


def _s5_scan_steps(hr, hi, cr, ci, lds, lp):
    for ld, d in zip(lds, S5_SHIFTS):
        lr, li = ld[:, :NRE], ld[:, NRE:]
        sr, si = _group_shift(hr, d), _group_shift(hi, d)
        hr, hi = hr + lr * sr - li * si, hi + lr * si + li * sr
        yield
    pr, pi = lp[:, :NRE], lp[:, NRE:]
    rows_r, rows_i = [], []
    for r in range(hr.shape[0] // S5_GROUP):
        br, bi = hr[r * S5_GROUP:(r + 1) * S5_GROUP], hi[r * S5_GROUP:(r + 1) * S5_GROUP]
        br, bi = br + pr * cr - pi * ci, bi + pr * ci + pi * cr
        cr, ci = br[S5_GROUP - 1:S5_GROUP], bi[S5_GROUP - 1:S5_GROUP]
        rows_r.append(br)
        rows_i.append(bi)
        if r % 2:
            yield
    return jnp.concatenate(rows_r, axis=0), jnp.concatenate(rows_i, axis=0)


@jax.custom_vjp
def _known_scan(xr, xi, cr, ci, lam, lds, lp_rev, hr, hi):
    return hr, hi


def _known_scan_fwd(xr, xi, cr, ci, lam, lds, lp_rev, hr, hi):
    return (hr, hi), (cr, ci, lam, lds, lp_rev, hr, hi)


def _known_scan_bwd(res, cts):
    cr, ci, lam, lds, lp_rev, hr, hi = res
    ar, ai = cts
    for ld, d in zip(lds, S5_SHIFTS):
        lr, li = ld[:, :NRE], ld[:, NRE:]
        sr, si = _group_shift(ar, d, up=True), _group_shift(ai, d, up=True)
        ar, ai = ar + lr * sr + li * si, ai + lr * si - li * sr
    qr, qi = lp_rev[:, :NRE], lp_rev[:, NRE:]
    nr, ni = jnp.zeros_like(cr), jnp.zeros_like(ci)
    rows_r, rows_i = [], []
    for r in reversed(range(hr.shape[0] // S5_GROUP)):
        br, bi = ar[r * S5_GROUP:(r + 1) * S5_GROUP], ai[r * S5_GROUP:(r + 1) * S5_GROUP]
        br, bi = br + qr * nr + qi * ni, bi + qr * ni - qi * nr
        nr, ni = br[0:1], bi[0:1]
        rows_r.insert(0, br)
        rows_i.insert(0, bi)
    ar, ai = jnp.concatenate(rows_r, axis=0), jnp.concatenate(rows_i, axis=0)
    lr, li = lam[:, :NRE], lam[:, NRE:]
    dcr, dci = lr * nr + li * ni, lr * ni - li * nr
    first = lax.broadcasted_iota(jnp.int32, hr.shape, 0) == 0
    pr = jnp.where(first, cr, pltpu.roll(hr, 1, 0))
    pi = jnp.where(first, ci, pltpu.roll(hi, 1, 0))
    dlam = jnp.concatenate([jnp.sum(ar * pr + ai * pi, axis=0, keepdims=True),
                            jnp.sum(ai * pr - ar * pi, axis=0, keepdims=True)], axis=1)
    return (ar, ai, dcr, dci, dlam, [jnp.zeros_like(ld) for ld in lds], jnp.zeros_like(lp_rev),
            jnp.zeros_like(hr), jnp.zeros_like(hi))


_known_scan.defvjp(_known_scan_fwd, _known_scan_bwd)


def _interleave(short, long, head_start=0):
    gens = list(short) + list(long)
    results = [None] * len(gens)

    def advance(live):
        still = []
        for idx, gen in live:
            try:
                next(gen)
                still.append((idx, gen))
            except StopIteration as done:
                results[idx] = done.value
        return still

    live_short = advance(list(enumerate(gens))[:len(short)])
    live_long = list(enumerate(gens))[len(short):]
    for _ in range(head_start):
        live_long = advance(live_long)
    live = live_short + live_long
    while live:
        live = advance(live)
    return results[:len(short)], results[len(short):]


def _s5_chunk_gen(u, gate, cr, ci, b_blk, c_blk, lam, dv, wglu, bglu, lds, lp, lp_rev, known_h=None):
    bu = _mm(u, b_blk)
    xr, xi = bu[:, :NRE], bu[:, NRE:]
    yield
    if known_h is None:
        hr, hi = yield from _s5_scan_steps(xr, xi, cr, ci, lds, lp)
    else:
        hr, hi = _known_scan(xr, xi, cr, ci, lam, lds, lp_rev, *known_h)
    y = _mm(jnp.concatenate([hr, hi], axis=1), c_blk) + dv * u
    yield
    y = jax.nn.gelu(y)
    y = y * jax.nn.sigmoid(_mm(y, wglu) + bglu)
    return y * _silu(gate), hr, hi


S5_PAR_SHAPES = [(D_SSM, 2 * NRE), (2 * NRE, D_SSM), (1, 2 * NRE), (1, D_SSM), (D_SSM, D_SSM), (1, D_SSM)]
S5_CONST_SHAPES = [(len(S5_SHIFTS), 1, 2 * NRE), (S5_GROUP, 2 * NRE), (S5_GROUP, 2 * NRE)]
DN_PAR_SHAPES = [(1, D_DN), (1, D_DN), (1, DH)]


def _mix_specs(bl, n_c, rev):
    def chunk(i):
        return n_c - 1 - i if rev else i

    def tok(n):
        return pl.BlockSpec((bl, DN_C, n), lambda i: (0, chunk(i), 0))

    def per_chunk(shape):
        return pl.BlockSpec((bl, 1, *shape), lambda i: (0, chunk(i)) + (0,) * len(shape))

    def whole(shape):
        return pl.BlockSpec(shape, lambda i: (0,) * len(shape))

    return tok, per_chunk, whole


def _unit_lower_inverse_steps(ms):
    c_len = ms[0].shape[0]
    eye = lax.broadcasted_iota(jnp.int32, (c_len, c_len), 0) == lax.broadcasted_iota(jnp.int32, (c_len, c_len), 1)
    ident = jnp.where(eye, 1.0, 0.0)
    ps = ms
    tinvs = [ident - m for m in ms]
    for _ in range(c_len.bit_length() - 2):
        ps = [_dg3(p, p, 1, 0) for p in ps]
        yield
        tinvs = [t + _dg3(t, p, 1, 0) for t, p in zip(tinvs, ps)]
        yield
    return tinvs


@jax.custom_vjp
def _known_inverses(ms, tinvs):
    return tinvs


def _known_inverses_fwd(ms, tinvs):
    return tinvs, tinvs


def _known_inverses_bwd(tinvs, gs):
    return [-_dg3(_dg3(t, g, 0, 0), t, 1, 1) for t, g in zip(tinvs, gs)], [jnp.zeros_like(t) for t in tinvs]


_known_inverses.defvjp(_known_inverses_fwd, _known_inverses_bwd)


def _dn_chunk_gen(qkv, zab, zg, states, alog_e, dt_e, ng, known_tinvs=None):
    c_len = DN_C
    r = lax.broadcasted_iota(jnp.int32, (c_len, c_len), 0)
    c = lax.broadcasted_iota(jnp.int32, (c_len, c_len), 1)
    causal, strict = r >= c, r > c
    tril = jnp.where(causal, 1.0, 0.0)
    rr = lax.broadcasted_iota(jnp.int32, (LANES, D_DN), 0)
    cc = lax.broadcasted_iota(jnp.int32, (LANES, D_DN), 1)
    e_a = jnp.where((cc >= rr * DH) & (cc < rr * DH + DH) & (rr < H), 1.0, 0.0)
    e_b = jnp.where((cc >= (rr - H) * DH) & (cc < (rr - H) * DH + DH) & (rr >= H) & (rr < 2 * H), 1.0, 0.0)
    a_e = _sel_r(zab, e_a)
    b_e = _sel_r(zab, e_b)
    beta = jax.nn.sigmoid(b_e)
    g = -jnp.exp(alog_e) * jax.nn.softplus(a_e + dt_e)
    yield
    gc = _sel_l(tril, g)
    glast = jnp.sum(g, axis=0, keepdims=True)
    eg = jnp.exp(gc)
    ekd = jnp.exp(glast - gc)
    dl = jnp.exp(glast)
    yield
    heads = range(H)
    sls = [slice(h * DH, (h + 1) * DH) for h in heads]
    qs = [qkv[:, h * DH:(h + 1) * DH] for h in heads]
    ks = [qkv[:, D_DN + h * DH:D_DN + (h + 1) * DH] for h in heads]
    vs = [qkv[:, 2 * D_DN + h * DH:2 * D_DN + (h + 1) * DH] for h in heads]
    ccols = [gc[:, sl] for sl in sls]
    decs = [jnp.where(causal, jnp.exp(jnp.where(causal, cl - jnp.transpose(cl), 0.0)), 0.0) for cl in ccols]
    kbs = [k * beta[:, sl] for k, sl in zip(ks, sls)]
    ms = [jnp.where(strict, _mm_nt(kb, k) * dec, 0.0) for kb, k, dec in zip(kbs, ks, decs)]
    yield
    if known_tinvs is None:
        tinvs = yield from _unit_lower_inverse_steps(ms)
    else:
        tinvs = _known_inverses(ms, list(known_tinvs))
    sols = [_dot3(t, jnp.concatenate([v * beta[:, sl], kb * eg[:, sl]], axis=1))
            for t, v, kb, sl in zip(tinvs, vs, kbs, sls)]
    yield
    atts = [_mm_nt(q, k) * dec for q, k, dec in zip(qs, ks, decs)]
    vnews = [sol[:, :DH] - _mm(sol[:, DH:], st) for sol, st in zip(sols, states)]
    yield
    os_ = [_mm(q * eg[:, sl], st) + _mm(att, vn) for q, sl, st, att, vn in zip(qs, sls, states, atts, vnews)]
    yield
    new_states = [st * dl[:, sl] + _mm_tn(k * ekd[:, sl], vn) for st, sl, k, vn in zip(states, sls, ks, vnews)]
    yield
    ys = [_rms(o, ng) * _silu(zg[:, sl]) for o, sl in zip(os_, sls)]
    return jnp.concatenate(ys, axis=1), new_states, tinvs


def _mix_fwd(zs, s5_par, s5_const, qkv, zab, zg, dn_par, bl, s, name, gather=()):
    assert S5_L == DN_C
    n_c = s // DN_C
    nd = len(S5_SHIFTS)
    m = len(gather)
    tok, per_chunk, whole = _mix_specs(bl, n_c, False)

    def body(*refs):
        (z_ref, b_ref, c_ref, lam_ref, dv_ref, wg_ref, bg_ref, ld_ref, lp_ref, lpr_ref,
         q_ref, ab_ref, zg_ref, al_ref, dt_ref, ng_ref) = refs[:16]
        ys_ref, car_ref, h_ref, yd_ref, st_ref, ti_ref = refs[16 + m:22 + m]
        cs, ssc = refs[22 + 2 * m:24 + 2 * m]
        gathering = (_gather_protocol(_mesh_place(), refs[16:16 + m], refs[22 + m:22 + 2 * m], *refs[24 + 2 * m:])
                     if m else None)

        @pl.when(pl.program_id(0) == 0)
        def _():
            cs[...] = jnp.zeros_like(cs)
            ssc[...] = jnp.zeros_like(ssc)
            if m:
                next(gathering)

        lds = [ld_ref[k] for k in range(nd)]
        s5_gens, dn_gens = [], []
        for e in range(bl):
            c = cs[e]
            car_ref[e, 0] = c
            sts = [ssc[e, h] for h in range(H)]
            for h in range(H):
                st_ref[e, 0, h] = sts[h]
            z = z_ref[e]
            s5_gens.append(_s5_chunk_gen(z[:, :D_SSM], z[:, D_SSM:], c[:, :NRE], c[:, NRE:], b_ref[...], c_ref[...], lam_ref[...],
                                         dv_ref[...], wg_ref[...], bg_ref[...], lds, lp_ref[...], lpr_ref[...]))
            dn_gens.append(_dn_chunk_gen(q_ref[e], ab_ref[e], zg_ref[e], sts, al_ref[...], dt_ref[...], ng_ref[...]))
        s5_outs, dn_outs = _interleave(s5_gens, dn_gens, head_start=MIX_HEAD_START)
        for e in range(bl):
            y_s, hr, hi = s5_outs[e]
            y_d, new_sts, tinvs = dn_outs[e]
            ys_ref[e] = y_s
            h_ref[e, :, :NRE] = hr
            h_ref[e, :, NRE:] = hi
            cs[e, :, :NRE] = hr[S5_L - 1:S5_L]
            cs[e, :, NRE:] = hi[S5_L - 1:S5_L]
            yd_ref[e] = y_d
            for h in range(H):
                ssc[e, h] = new_sts[h]
                ti_ref[e, 0, h] = tinvs[h]

        if m:
            @pl.when(pl.program_id(0) == n_c - 1)
            def _():
                for _ in gathering:
                    pass

    head_mats = jax.ShapeDtypeStruct((bl, n_c, H, DH, DH), F32)
    outs = pl.pallas_call(
        body, grid=(n_c,),
        in_specs=[tok(2 * D_SSM)] + [whole(sh) for sh in S5_PAR_SHAPES + S5_CONST_SHAPES]
        + [tok(3 * D_DN), tok(LANES), tok(D_DN)] + [whole(sh) for sh in DN_PAR_SHAPES] + _hbm_specs(m),
        out_specs=[tok(D_SSM), per_chunk((1, 2 * NRE)), tok(2 * NRE), tok(D_DN), per_chunk((H, DH, DH)), per_chunk((H, DH, DH))]
        + _hbm_specs(m),
        out_shape=[jax.ShapeDtypeStruct((bl, s, D_SSM), F32), jax.ShapeDtypeStruct((bl, n_c, 1, 2 * NRE), F32),
                   jax.ShapeDtypeStruct((bl, s, 2 * NRE), F32), jax.ShapeDtypeStruct((bl, s, D_DN), F32), head_mats, head_mats]
        + [jax.ShapeDtypeStruct((N_DEV, *b.shape), b.dtype) for b in gather],
        scratch_shapes=[pltpu.VMEM((bl, 1, 2 * NRE), F32), pltpu.VMEM((bl, H, DH, DH), F32)] + (_gather_sems(m) if m else []),
        name=name, compiler_params=_cp("arbitrary"))(zs, *s5_par, *s5_const, qkv, zab, zg, *dn_par, *gather)
    return outs[:6], outs[6:]


def _mix_bwd(zs, carries, h_all, dy_s, s5_par, s5_const, qkv, zab, zg, states, tinvs, dy_d, dn_par, bl, s, name):
    n_c = s // DN_C
    nd = len(S5_SHIFTS)
    tok, per_chunk, whole = _mix_specs(bl, n_c, True)

    def both(examples, s5_tabs, s5_consts, dn_tabs):
        s5_gens = [_s5_chunk_gen(u, gate, cr, ci, *s5_tabs, *s5_consts, known_h=(hr, hi))
                   for u, gate, cr, ci, hr, hi, _, _, _, _, _ in examples]
        dn_gens = [_dn_chunk_gen(q, ab, zgate, sts, *dn_tabs, known_tinvs=known)
                   for _, _, _, _, _, _, q, ab, zgate, sts, known in examples]
        s5_outs, dn_outs = _interleave(s5_gens, dn_gens, head_start=MIX_HEAD_START)
        return [(y_s, hr[S5_L - 1:S5_L], hi[S5_L - 1:S5_L], y_d, new_sts)
                for (y_s, hr, hi), (y_d, new_sts, _) in zip(s5_outs, dn_outs)]

    def body(z_ref, car_ref, h_ref, dys_ref, b_ref, c_ref, lam_ref, dv_ref, wg_ref, bg_ref, ld_ref, lp_ref, lpr_ref,
             q_ref, ab_ref, zg_ref, st_ref, ti_ref, dyd_ref, al_ref, dt_ref, ng_ref,
             dz_ref, db_ref, dc_ref, dlam_ref, ddv_ref, dwg_ref, dbg_ref,
             dq_ref, dab_ref, dzg_ref, dal_ref, ddt_ref, dng_ref, dcs, dsc):
        accs = (db_ref, dc_ref, dlam_ref, ddv_ref, dwg_ref, dbg_ref, dal_ref, ddt_ref, dng_ref)

        @pl.when(pl.program_id(0) == 0)
        def _():
            for r in accs + (dcs, dsc):
                r[...] = jnp.zeros_like(r)

        examples = []
        for e in range(bl):
            z = z_ref[e]
            c = car_ref[e, 0]
            examples.append((z[:, :D_SSM], z[:, D_SSM:], c[:, :NRE], c[:, NRE:], h_ref[e, :, :NRE], h_ref[e, :, NRE:],
                             q_ref[e], ab_ref[e], zg_ref[e], [st_ref[e, 0, h] for h in range(H)],
                             [ti_ref[e, 0, h] for h in range(H)]))
        _, vjp = jax.vjp(both, examples,
                         (b_ref[...], c_ref[...], lam_ref[...], dv_ref[...], wg_ref[...], bg_ref[...]),
                         ([ld_ref[k] for k in range(nd)], lp_ref[...], lpr_ref[...]),
                         (al_ref[...], dt_ref[...], ng_ref[...]))
        cts = []
        for e in range(bl):
            dc = dcs[e]
            cts.append((dys_ref[e], dc[:, :NRE], dc[:, NRE:], dyd_ref[e], [dsc[e, h] for h in range(H)]))
        d_examples, d_s5, _, d_dn = vjp(cts)
        for e in range(bl):
            du, dgate, dcr, dci, _, _, dq, dab, dzg, dsts, _ = d_examples[e]
            dz_ref[e] = jnp.concatenate([du, dgate], axis=1).astype(BF16)
            dcs[e, :, :NRE] = dcr
            dcs[e, :, NRE:] = dci
            dq_ref[e] = dq
            dab_ref[e] = dab.astype(BF16)
            dzg_ref[e] = dzg.astype(BF16)
            for h in range(H):
                dsc[e, h] = dsts[h]
        for r, ct in zip(accs, (*d_s5, *d_dn)):
            r[...] += ct

    head_mats = per_chunk((H, DH, DH))
    outs = pl.pallas_call(
        body, grid=(n_c,),
        in_specs=[tok(2 * D_SSM), per_chunk((1, 2 * NRE)), tok(2 * NRE), tok(D_SSM)]
        + [whole(sh) for sh in S5_PAR_SHAPES + S5_CONST_SHAPES]
        + [tok(3 * D_DN), tok(LANES), tok(D_DN), head_mats, head_mats, tok(D_DN)] + [whole(sh) for sh in DN_PAR_SHAPES],
        out_specs=[tok(2 * D_SSM)] + [whole(sh) for sh in S5_PAR_SHAPES]
        + [tok(3 * D_DN), tok(LANES), tok(D_DN)] + [whole(sh) for sh in DN_PAR_SHAPES],
        out_shape=[jax.ShapeDtypeStruct((bl, s, 2 * D_SSM), BF16)] + [jax.ShapeDtypeStruct(sh, F32) for sh in S5_PAR_SHAPES]
        + [jax.ShapeDtypeStruct((bl, s, 3 * D_DN), F32), jax.ShapeDtypeStruct((bl, s, LANES), BF16),
           jax.ShapeDtypeStruct((bl, s, D_DN), BF16)]
        + [jax.ShapeDtypeStruct(sh, F32) for sh in DN_PAR_SHAPES],
        scratch_shapes=[pltpu.VMEM((bl, 1, 2 * NRE), F32), pltpu.VMEM((bl, H, DH, DH), F32)],
        name=name, compiler_params=_cp("arbitrary"))(
            zs, carries, h_all, dy_s, *s5_par, *s5_const, qkv, zab, zg, states, tinvs, dy_d, *dn_par)
    return outs[:7], outs[7:]


def _sg_fn(n_chunk):
    def f(z, lng, lnb, w, bsp_t):
        u = jax.nn.gelu(z[:, :D_SG])
        v = jax.nn.gelu(z[:, D_SG:2 * D_SG])
        gate = z[:, 2 * D_SG:]
        xc = v - jnp.mean(v, axis=-1, keepdims=True)
        vn = xc * lax.rsqrt(jnp.mean(xc * xc, axis=-1, keepdims=True) + EPS) * lng + lnb
        r = lax.broadcasted_iota(jnp.int32, (SG_C, SG_C), 0)
        c = lax.broadcasted_iota(jnp.int32, (SG_C, SG_C), 1)
        causal = r >= c
        first_half = c < SG_C // 2
        rr = lax.broadcasted_iota(jnp.int32, (LANES, D_SG), 0)
        cc = lax.broadcasted_iota(jnp.int32, (LANES, D_SG), 1)
        expand = jnp.where((cc >= rr * 64) & (cc < rr * 64 + 64) & (rr < 4), 1.0, 0.0)
        bias = _sel_r(bsp_t, expand)
        wm = [jnp.where(causal, w[h], 0.0) for h in range(4)]
        rows = []
        for ci in range(n_chunk):
            vc = vn[ci * SG_C:(ci + 1) * SG_C]
            pairs = []
            for pr in range(2):
                vp = vc[:, pr * LANES:(pr + 1) * LANES]
                pairs.append(jnp.where(first_half, _mm(wm[2 * pr], vp), _mm(wm[2 * pr + 1], vp)))
            rows.append(jnp.concatenate(pairs, axis=1) + bias)
        sp = jnp.concatenate(rows, axis=0) if n_chunk > 1 else rows[0]
        return u * sp * _silu(gate)

    return f


def _sg_specs():
    full = lambda i: (0, 0)
    full3 = lambda i: (0, 0, 0)
    par = [pl.BlockSpec((1, D_SG), full), pl.BlockSpec((1, D_SG), full), pl.BlockSpec((4, SG_C, SG_C), full3),
           pl.BlockSpec((SG_C, LANES), full)]
    par_shapes = [(1, D_SG), (1, D_SG), (4, SG_C, SG_C), (SG_C, LANES)]
    return par, par_shapes


def _sg_fwd(zsg, params, tb, name):
    t = zsg.shape[0]
    f = _sg_fn(tb // SG_C)
    par, _ = _sg_specs()

    def body(z_ref, g_ref, b_ref, w_ref, bs_ref, y_ref):
        y_ref[...] = f(z_ref[...], g_ref[...], b_ref[...], w_ref[...], bs_ref[...])

    row = lambda i: (i, 0)
    return pl.pallas_call(
        body, grid=(t // tb,), in_specs=[pl.BlockSpec((tb, 3 * D_SG), row)] + par,
        out_specs=pl.BlockSpec((tb, D_SG), row), out_shape=jax.ShapeDtypeStruct((t, D_SG), F32),
        name=name, compiler_params=_cp("parallel"))(zsg, *params)


def _sg_bwd(zsg, dy, params, tb, name):
    t = zsg.shape[0]
    f = _sg_fn(tb // SG_C)
    par, par_shapes = _sg_specs()

    def body(z_ref, dy_ref, g_ref, b_ref, w_ref, bs_ref, dz_ref, dg_ref, db_ref, dw_ref, dbs_ref):
        accs = (dg_ref, db_ref, dw_ref, dbs_ref)

        @pl.when(pl.program_id(0) == 0)
        def _():
            for r in accs:
                r[...] = jnp.zeros_like(r)

        _, vjp = jax.vjp(f, z_ref[...], g_ref[...], b_ref[...], w_ref[...], bs_ref[...])
        cts = vjp(dy_ref[...])
        dz_ref[...] = cts[0].astype(BF16)
        for r, ct in zip(accs, cts[1:]):
            r[...] += ct

    row = lambda i: (i, 0)
    return pl.pallas_call(
        body, grid=(t // tb,), in_specs=[pl.BlockSpec((tb, 3 * D_SG), row), pl.BlockSpec((tb, D_SG), row)] + par,
        out_specs=[pl.BlockSpec((tb, 3 * D_SG), row)] + par,
        out_shape=[jax.ShapeDtypeStruct((t, 3 * D_SG), BF16)] + [jax.ShapeDtypeStruct(sh, F32) for sh in par_shapes],
        name=name, compiler_params=_cp("arbitrary"))(zsg, dy, *params)


def _out_fwd(x, ys, p, w_out, pg, w_gate, w_ple, tb, name):
    t = x.shape[0]

    def body(x_ref, y0, y1, y2, p_ref, wo_ref, pg_ref, wg_ref, wp_ref, o_ref, x1_ref, gate_ref):
        y = jnp.concatenate([y0[...], y1[...], y2[...]], axis=1).astype(BF16)
        x1 = x_ref[...] + jnp.dot(y, wo_ref[...], preferred_element_type=F32)
        hn = _rms(x1, pg_ref[...]).astype(BF16)
        gate = jax.nn.sigmoid(jnp.dot(hn, wg_ref[...], preferred_element_type=F32))
        pp = jnp.dot(p_ref[...].astype(BF16), wp_ref[...], preferred_element_type=F32)
        o_ref[...] = x1 + gate * pp
        x1_ref[...] = x1
        gate_ref[...] = gate

    row = lambda i: (i, 0)
    full = lambda i: (0, 0)
    return pl.pallas_call(
        body, grid=(t // tb,),
        in_specs=[pl.BlockSpec((tb, D), row), pl.BlockSpec((tb, D_SSM), row), pl.BlockSpec((tb, D_DN), row),
                  pl.BlockSpec((tb, D_SG), row), pl.BlockSpec((tb, D_PLE), row), pl.BlockSpec((D, D), full),
                  pl.BlockSpec((1, D), full), pl.BlockSpec((D, D), full), pl.BlockSpec((D_PLE, D), full)],
        out_specs=[pl.BlockSpec((tb, D), row)] * 3, out_shape=[jax.ShapeDtypeStruct((t, D), F32)] * 3,
        name=name, compiler_params=_cp("parallel"))(x, *ys, p, w_out, pg, w_gate, w_ple)


def _out_bwd(x1, gate, ys, p, dx2, w_out, pg, w_gate, w_ple, tb, name):
    t = x1.shape[0]

    def body(x1_ref, gate_ref, y0, y1, y2, p_ref, d_ref, wo_ref, pg_ref, wg_ref, wp_ref,
             dx_ref, dy0, dy1, dy2, dwo_ref, dpg_ref, dwg_ref, dwp_ref):
        accs = (dwo_ref, dpg_ref, dwg_ref, dwp_ref)

        @pl.when(pl.program_id(0) == 0)
        def _():
            for r in accs:
                r[...] = jnp.zeros_like(r)

        y = jnp.concatenate([y0[...], y1[...], y2[...]], axis=1).astype(BF16)
        hn, rms_vjp = jax.vjp(_rms, x1_ref[...], pg_ref[...])
        hb = hn.astype(BF16)
        gate = gate_ref[...]
        pb = p_ref[...].astype(BF16)
        pp = jnp.dot(pb, wp_ref[...], preferred_element_type=F32)
        d2 = d_ref[...]
        dpp = (d2 * gate).astype(BF16)
        dlog = (d2 * pp * gate * (1.0 - gate)).astype(BF16)
        dwp_ref[...] += _dg(pb, dpp, 0, 0)
        dwg_ref[...] += _dg(hb, dlog, 0, 0)
        dx1_n, dpg = rms_vjp(_dg(dlog, wg_ref[...], 1, 1))
        dpg_ref[...] += dpg
        dx1 = d2 + dx1_n
        dx_ref[...] = dx1
        db = dx1.astype(BF16)
        dwo_ref[...] += _dg(y, db, 0, 0)
        dy = _dg(db, wo_ref[...], 1, 1)
        dy0[...] = dy[:, :D_SSM]
        dy1[...] = dy[:, D_SSM:D_SSM + D_DN]
        dy2[...] = dy[:, D_SSM + D_DN:]

    row = lambda i: (i, 0)
    full = lambda i: (0, 0)
    acts = [pl.BlockSpec((tb, D), row), pl.BlockSpec((tb, D_SSM), row), pl.BlockSpec((tb, D_DN), row), pl.BlockSpec((tb, D_SG), row)]
    wts = [pl.BlockSpec((D, D), full), pl.BlockSpec((1, D), full), pl.BlockSpec((D, D), full), pl.BlockSpec((D_PLE, D), full)]
    return pl.pallas_call(
        body, grid=(t // tb,),
        in_specs=[pl.BlockSpec((tb, D), row)] + acts + [pl.BlockSpec((tb, D_PLE), row), pl.BlockSpec((tb, D), row)] + wts,
        out_specs=acts + wts,
        out_shape=[jax.ShapeDtypeStruct((t, n), F32) for n in (D, D_SSM, D_DN, D_SG)]
        + [jax.ShapeDtypeStruct(sh, F32) for sh in ((D, D), (1, D), (D, D), (D_PLE, D))],
        name=name, compiler_params=_cp("arbitrary"))(x1, gate, *ys, p, dx2, w_out, pg, w_gate, w_ple)


def _loss_head(x, fg, target, tb, name):
    t = x.shape[0]

    def body(x_ref, g_ref, t_ref, dx_ref, dg_ref, loss_ref):
        @pl.when(pl.program_id(0) == 0)
        def _():
            dg_ref[...] = jnp.zeros_like(dg_ref)
            loss_ref[...] = jnp.zeros_like(loss_ref)

        y, vjp = jax.vjp(_rms, x_ref[...], g_ref[...])
        err = y - t_ref[...]
        loss_ref[...] += jnp.zeros_like(loss_ref) + 0.5 * jnp.sum(err * err) / D
        dx, dg = vjp(err / D)
        dx_ref[...] = dx
        dg_ref[...] += dg

    row = lambda i: (i, 0)
    full = lambda i: (0, 0)
    return pl.pallas_call(
        body, grid=(t // tb,),
        in_specs=[pl.BlockSpec((tb, D), row), pl.BlockSpec((1, D), full), pl.BlockSpec((tb, D), row)],
        out_specs=[pl.BlockSpec((tb, D), row), pl.BlockSpec((1, D), full), pl.BlockSpec((1, LANES), full)],
        out_shape=[jax.ShapeDtypeStruct((t, D), F32), jax.ShapeDtypeStruct((1, D), F32), jax.ShapeDtypeStruct((1, LANES), F32)],
        name=name, compiler_params=_cp("arbitrary"))(x, fg, target)


def _hbm_specs(n):
    return [pl.BlockSpec(memory_space=pl.ANY)] * n


def _gather_protocol(place, ins, outs, send_sems, recv_sems, local_sems):
    n = len(ins)
    x, y, c, other_x, other_y, other_c = place
    me, sibling = (x, y, c), (x, y, other_c)
    chips = [(other_x, y), (x, other_y), (other_x, other_y)]

    def slot(a, px, py, pc):
        return outs[a].at[4 * px + 2 * py + pc]

    def copy(a, k, blk, to, src=None):
        return pltpu.make_async_remote_copy(
            src_ref=slot(a, *blk) if src is None else src, dst_ref=slot(a, *blk),
            send_sem=send_sems.at[7 * a + k], recv_sem=recv_sems.at[7 * a + k],
            device_id=to, device_id_type=pl.DeviceIdType.MESH)

    def own_copies():
        mines = [pltpu.make_async_copy(ins[a], slot(a, *me), local_sems.at[a]) for a in range(n)]
        first = []
        for a in range(n):
            first.append(copy(a, 0, me, sibling, src=ins[a]))
            first += [copy(a, 1 + j, me, (*chip, c), src=ins[a]) for j, chip in enumerate(chips)]
        return mines, first

    mines, first = own_copies()
    for cp in mines + first:
        cp.start()
    yield
    mines, first = own_copies()
    passed = []
    for j, chip in enumerate(chips):
        for a in range(n):
            copy(a, 1 + j, (*chip, c), me).wait_recv()
            onward = copy(a, 4 + j, (*chip, c), sibling)
            onward.start()
            passed.append(onward)
    for a in range(n):
        copy(a, 0, sibling, me).wait_recv()
    for j, chip in enumerate(chips):
        for a in range(n):
            copy(a, 4 + j, (*chip, other_c), me).wait_recv()
    for cp in first + passed:
        cp.wait_send()
    for cp in mines:
        cp.wait()


def _mesh_place():
    x, y, c = lax.axis_index("x"), lax.axis_index("y"), lax.axis_index("c")
    return x, y, c, 1 - x, 1 - y, 1 - c


def _gather_sems(n):
    return [pltpu.SemaphoreType.DMA((7 * n,)), pltpu.SemaphoreType.DMA((7 * n,)), pltpu.SemaphoreType.DMA((n,))]


def _all_gather(blocks, name):
    n = len(blocks)

    def body(*refs):
        for _ in _gather_protocol(_mesh_place(), refs[:n], refs[n:2 * n], *refs[2 * n:]):
            pass

    return pl.pallas_call(
        body, out_shape=[jax.ShapeDtypeStruct((N_DEV, *b.shape), b.dtype) for b in blocks],
        in_specs=_hbm_specs(n), out_specs=_hbm_specs(n), scratch_shapes=_gather_sems(n), name=name)(*blocks)


def _pair_exchange(gs, name):
    n = len(gs)

    def body(*refs):
        ins, recvs = refs[:n], refs[n:2 * n]
        send_sems, recv_sems = refs[2 * n:]
        x, y, c = lax.axis_index("x"), lax.axis_index("y"), lax.axis_index("c")
        remote = [pltpu.make_async_remote_copy(
            src_ref=ins[a], dst_ref=recvs[a], send_sem=send_sems.at[a], recv_sem=recv_sems.at[a],
            device_id=(x, y, 1 - c), device_id_type=pl.DeviceIdType.MESH) for a in range(n)]
        for cp in remote:
            cp.start()
        for cp in remote:
            cp.wait_send()
            cp.wait_recv()

    return pl.pallas_call(
        body, out_shape=[jax.ShapeDtypeStruct(g.shape, g.dtype) for g in gs], in_specs=_hbm_specs(n), out_specs=_hbm_specs(n),
        scratch_shapes=[pltpu.SemaphoreType.DMA((n,)), pltpu.SemaphoreType.DMA((n,))],
        name=name)(*gs)


def _chip_exchange(ps, gather, name):
    n, m = len(ps), len(gather)

    def body(*refs):
        ins, g_ins, outs, g_outs = refs[:n], refs[n:n + m], refs[n + m:2 * n + m], refs[2 * n + m:2 * (n + m)]
        send_sems, recv_sems, local_sems = refs[2 * (n + m):2 * (n + m) + 3]
        gathering = _gather_protocol(_mesh_place(), g_ins, g_outs, *refs[2 * (n + m) + 3:])
        x, y, c = lax.axis_index("x"), lax.axis_index("y"), lax.axis_index("c")
        my_chip = 2 * x + y
        local = [pltpu.make_async_copy(ins[a].at[my_chip], outs[a].at[my_chip], local_sems.at[a]) for a in range(n)]
        remote = []
        for j in range(1, 4):
            px = 1 - x if j & 2 else x
            py = 1 - y if j & 1 else y
            for a in range(n):
                remote.append(pltpu.make_async_remote_copy(
                    src_ref=ins[a].at[2 * px + py], dst_ref=outs[a].at[my_chip],
                    send_sem=send_sems.at[3 * a + j - 1], recv_sem=recv_sems.at[3 * a + j - 1],
                    device_id=(px, py, c), device_id_type=pl.DeviceIdType.MESH))
        for cp in local + remote:
            cp.start()
        for _ in gathering:
            pass
        for cp in remote:
            cp.wait_send()
            cp.wait_recv()
        for cp in local:
            cp.wait()

    outs = pl.pallas_call(
        body, out_shape=[jax.ShapeDtypeStruct(q.shape, q.dtype) for q in ps]
        + [jax.ShapeDtypeStruct((N_DEV, *b.shape), b.dtype) for b in gather],
        in_specs=_hbm_specs(n + m), out_specs=_hbm_specs(n + m),
        scratch_shapes=[pltpu.SemaphoreType.DMA((3 * n,)), pltpu.SemaphoreType.DMA((3 * n,)), pltpu.SemaphoreType.DMA((n,))]
        + _gather_sems(m), name=name)(*ps, *gather)
    return outs[:n], outs[n:]


def _row_block(rows, bytes_per_row):
    best = None
    for rb in range(16, rows + 1, 16):
        if rows % rb == 0 and rb * bytes_per_row <= ELEMENTWISE_STEP_BYTES:
            best = rb
    return rows if best is None else best


def _add_pair(own, recv, name):
    shape = own.shape
    last = shape[-1]
    rows = own.size // last
    rb = _row_block(rows, 3 * 4 * (-(-last // LANES) * LANES))

    def body(a_ref, b_ref, o_ref):
        o_ref[...] = (a_ref[...].astype(F32) + b_ref[...].astype(F32)).astype(o_ref.dtype)

    row = lambda i: (i, 0)
    out = pl.pallas_call(
        body, grid=(rows // rb,), in_specs=[pl.BlockSpec((rb, last), row)] * 2, out_specs=pl.BlockSpec((rb, last), row),
        out_shape=jax.ShapeDtypeStruct((rows, last), own.dtype), name=name,
        compiler_params=_cp("parallel"))(own.reshape(rows, last), recv.reshape(rows, last))
    return out.reshape(shape)


def _sum_adamw(gk, w, m, v, name):
    shape = w.shape
    n_part = gk.shape[0]
    last = shape[-1]
    rows = w.size // last
    rb = _row_block(rows, (n_part + 7) * 4 * (-(-last // LANES) * LANES))

    def body(g_ref, w_ref, m_ref, v_ref, go_ref, d_ref, mo_ref, vo_ref):
        g = g_ref[0].astype(F32)
        for k in range(1, n_part):
            g = g + g_ref[k].astype(F32)
        mn = ADAM_B1 * m_ref[...] + (1.0 - ADAM_B1) * g
        vn = ADAM_B2 * v_ref[...] + (1.0 - ADAM_B2) * jnp.square(g)
        m_hat = mn / (1.0 - ADAM_B1 ** ADAM_STEP)
        v_hat = vn / (1.0 - ADAM_B2 ** ADAM_STEP)
        go_ref[...] = g
        d_ref[...] = -ADAM_LR * (m_hat / (jnp.sqrt(v_hat) + ADAM_EPS) + ADAM_WD * w_ref[...])
        mo_ref[...] = mn
        vo_ref[...] = vn

    row = lambda i: (i, 0)
    outs = pl.pallas_call(
        body, grid=(rows // rb,),
        in_specs=[pl.BlockSpec((n_part, rb, last), lambda i: (0, i, 0))] + [pl.BlockSpec((rb, last), row)] * 3,
        out_specs=[pl.BlockSpec((rb, last), row)] * 4,
        out_shape=[jax.ShapeDtypeStruct((rows, last), F32)] * 4,
        name=name, compiler_params=_cp("parallel"))(gk.reshape(n_part, rows, last), *[a.reshape(rows, last) for a in (w, m, v)])
    return [o.reshape(shape) for o in outs]


def _seg_rows(shape):
    n = 1
    for d in shape:
        n *= d
    return -(-n // (8 * LANES)) * 8


def _pack(arrs):
    segs = []
    for a in arrs:
        r = _seg_rows(a.shape)
        segs.append(jnp.pad(a.reshape(-1).astype(F32), (0, r * LANES - a.size)).reshape(r, LANES))
    rows = sum(s.shape[0] for s in segs)
    total = -(-rows // PACK_ROWS) * PACK_ROWS
    if total > rows:
        segs.append(jnp.zeros((total - rows, LANES), F32))
    return jnp.concatenate(segs, axis=0)


def _unpack(pack, shapes):
    out, off = [], 0
    for sh in shapes:
        r = _seg_rows(sh)
        n = 1
        for d in sh:
            n *= d
        out.append(pack[off:off + r].reshape(-1)[:n].reshape(sh))
        off += r
    return out


def _to_dest_blocks(full, axis, dtype):
    sh = list(full.shape)
    sh[axis:axis + 1] = [N_DEV // 2, 2, sh[axis] // N_DEV]
    return jnp.moveaxis(full.reshape(sh), (axis, axis + 1), (1, 0)).astype(dtype)


def _from_gathered(g, axis):
    m = jnp.moveaxis(g, 0, axis)
    sh = list(m.shape)
    sh[axis:axis + 2] = [sh[axis] * sh[axis + 1]]
    return m.reshape(sh)


D_IN = 3336
W_IN_SHARD = D_IN // N_DEV
W_IN_MOVES = ((0, 2048, 0), (2048, 2056, 3328), (2056, D_IN, 2048))


def _w_in_windows(k):
    lo, hi = k * W_IN_SHARD, (k + 1) * W_IN_SHARD
    out = []
    for a, b, mine in W_IN_MOVES:
        a2, b2 = max(a, lo), min(b, hi)
        if b2 > a2:
            out.append((a2 - lo, b2 - a2, mine + a2 - a))
    return out


def _assemble_w_in(gathered, name):
    depth = gathered.shape[1]
    rb = 256

    def body(g_ref, o_ref):
        o_ref[0, :, D_IN:] = jnp.zeros((rb, ZW - D_IN), o_ref.dtype)
        for k in range(N_DEV):
            for off, width, mine in _w_in_windows(k):
                o_ref[0, :, mine:mine + width] = g_ref[k, 0, :, off:off + width]

    return pl.pallas_call(
        body, grid=(depth, D // rb),
        in_specs=[pl.BlockSpec((N_DEV, 1, rb, W_IN_SHARD), lambda l, i: (0, l, i, 0))],
        out_specs=pl.BlockSpec((1, rb, ZW), lambda l, i: (l, i, 0)),
        out_shape=jax.ShapeDtypeStruct((depth, D, ZW), gathered.dtype),
        name=name, compiler_params=_cp("parallel", "parallel"))(gathered)


def _split_dw_in(dws, name):
    depth = len(dws)
    rb = 128

    def body(*refs):
        o_ref = refs[-1]
        for l in range(depth):
            pieces = refs[5 * l:5 * l + 5]
            for k in range(N_DEV):
                for off, width, mine in _w_in_windows(k):
                    for p_ref, start, n in zip(pieces, Z_OFFSETS, Z_PIECES):
                        a, b = max(mine, start), min(mine + width, start + n)
                        if b > a:
                            o_ref[k % 2, k // 2, l, :, off + a - mine:off + b - mine] = (
                                p_ref[:, a - start:b - start].astype(o_ref.dtype))

    row = lambda i: (i, 0)
    flat = [piece for layer in dws for piece in layer]
    return pl.pallas_call(
        body, grid=(D // rb,),
        in_specs=[pl.BlockSpec((rb, n), row) for _ in range(depth) for n in Z_PIECES],
        out_specs=pl.BlockSpec((2, N_DEV // 2, depth, rb, W_IN_SHARD), lambda i: (0, 0, 0, i, 0)),
        out_shape=jax.ShapeDtypeStruct((2, N_DEV // 2, depth, D, W_IN_SHARD), WIRE['w_in']),
        name=name, compiler_params=_cp("parallel"))(*flat)


def _layer_weights(gathered, layer):
    full = {}
    for n, g in zip(SHARDED_ORDER, gathered):
        full[n] = _assemble_w_in(g, f"assemble_w_in_l{layer}") if n == 'w_in' else _from_gathered(g, SHARDED[n])
    full['ssm_w_glu'] = full['ssm_w_glu'].astype(F32)
    return full


def _local_step(x, p, wts, sharded, target, shards=None):
    sharded = list(sharded)
    bl, s, _ = x.shape
    t = bl * s
    depth = p.shape[0]
    tb, sg_tb = TB, SG_TB

    def by_example(a):
        return a.reshape(bl, s, a.shape[-1])

    def flat(a):
        return a.reshape(t, a.shape[-1])

    xs = [x.reshape(t, D)]
    saved = []
    for i in range(depth):
        li = f"l{i}"
        ng = wts['norm_g'][i].reshape(1, D)
        lw = sharded[i]
        w_in = lw['w_in'][0]
        s5_par_in = (wts['ssm_a_re'][i], wts['ssm_a_im'][i], wts['ssm_b_re'][i], wts['ssm_b_im'][i],
                     wts['ssm_c_re'][i], wts['ssm_c_im'][i], wts['ssm_d'][i], wts['ssm_log_step'][i])
        tabs, tab_vjp = jax.vjp(_s5_tables, *s5_par_in)
        s5_par = (*tabs, lw['ssm_w_glu'][0], wts['ssm_b_glu'][i].reshape(1, D_SSM))
        s5_const = _s5_powers(wts['ssm_a_re'][i], wts['ssm_a_im'][i], wts['ssm_log_step'][i])
        conv8 = jnp.pad(lw['dn_conv_w'][0], ((0, 4), (0, 0)))
        dn_par = (jnp.repeat(wts['dn_a_log'][i], DH).reshape(1, D_DN), jnp.repeat(wts['dn_dt_bias'][i], DH).reshape(1, D_DN),
                  wts['dn_norm_g'][i].reshape(1, DH))
        sg_par = (wts['sg_ln_g'][i].reshape(1, D_SG), wts['sg_ln_b'][i].reshape(1, D_SG), wts['sg_w'][i],
                  jnp.pad(jnp.transpose(wts['sg_b'][i]), ((0, 0), (0, LANES - 4))))
        out_par = (lw['w_out'][0].astype(BF16), wts['ple_norm_g'][i].reshape(1, D), lw['w_ple_gate'][0].astype(BF16),
                   lw['w_ple'][0].astype(BF16))
        pi = p[i].reshape(t, D_PLE)

        z_ssm, z_qkv, z_gdn, z_sg, z_ab, qkvn = _in_proj_fwd(xs[i], ng, w_in, conv8, s, tb, f"in_proj_fwd_{li}")
        wanted = shards[i + 1] if i + 1 < depth and sharded[i + 1] is None else ()
        (y_ssm, carries, h_all, y_dn, states, tinvs), gathered = _mix_fwd(
            by_example(z_ssm), s5_par, s5_const, by_example(qkvn), by_example(z_ab), by_example(z_gdn), dn_par, bl, s,
            f"mix_fwd_{li}", gather=wanted)
        if wanted:
            sharded[i + 1] = _layer_weights(gathered, i + 1)
        y_sg = _sg_fwd(z_sg, sg_par, sg_tb, f"sg_fwd_{li}")
        ys = (flat(y_ssm), flat(y_dn), y_sg)
        x_next, x1, gate = _out_fwd(xs[i], ys, pi, *out_par, tb, f"out_fwd_{li}")
        xs.append(x_next)
        saved.append(dict(ng=ng, w_in=w_in, tab_vjp=tab_vjp, s5_par=s5_par, s5_const=s5_const, conv8=conv8, dn_par=dn_par,
                          sg_par=sg_par, out_par=out_par, pi=pi, z=(z_ssm, z_qkv, z_gdn, z_sg, z_ab), carries=carries,
                          h_all=h_all, qkvn=qkvn, x1=x1, gate=gate,
                          states=states, tinvs=tinvs, ys=ys))

    dx, dfg, loss_vec = _loss_head(xs[depth], wts['final_norm_g'].reshape(1, D), target.reshape(t, D), tb, "loss_head")
    grads = {n: [None] * depth for n in WEIGHTS if n != 'final_norm_g'}
    grads['final_norm_g'] = dfg.reshape(D)
    for i in reversed(range(depth)):
        li = f"l{i}"
        sv = saved[i]
        z_ssm, z_qkv, z_gdn, z_sg, z_ab = sv['z']
        dx_res, dy_ssm, dy_dn, dy_sg, dwo, dpg, dwg, dwp = _out_bwd(sv['x1'], sv['gate'], sv['ys'], sv['pi'], dx, *sv['out_par'], tb,
                                                                    f"out_bwd_{li}")
        dz_sg, dlng, dlnb, dsgw, dbsp = _sg_bwd(z_sg, dy_sg, sv['sg_par'], sg_tb, f"sg_bwd_{li}")
        (dz_ssm, dbb, dcb, dlam, ddv, dwglu, dbglu), (dqkvn, dz_ab, dz_gdn, dal, ddt, dng) = _mix_bwd(
            by_example(z_ssm), sv['carries'], sv['h_all'], by_example(dy_ssm), sv['s5_par'], sv['s5_const'],
            by_example(sv['qkvn']), by_example(z_ab), by_example(z_gdn), sv['states'], sv['tinvs'], by_example(dy_dn),
            sv['dn_par'], bl, s, f"mix_bwd_{li}")
        dx, dnorm, dz_qkv, dconv = _in_proj_bwd_dx(xs[i], sv['ng'], sv['w_in'], sv['conv8'], flat(dz_ssm), flat(dz_gdn), dz_sg,
                                                   flat(dz_ab), z_qkv, flat(dqkvn), dx_res, s, tb, f"in_proj_bwd_dx_{li}")
        dzs = (flat(dz_ssm), dz_qkv, flat(dz_gdn), dz_sg, flat(dz_ab))
        dws = _in_proj_bwd_dw(xs[i], sv['ng'], dzs, min(TB_DW, t), f"in_proj_bwd_dw_{li}")
        ds5 = sv['tab_vjp']((dbb, dcb, dlam, ddv))
        for n, gval in zip(('ssm_a_re', 'ssm_a_im', 'ssm_b_re', 'ssm_b_im', 'ssm_c_re', 'ssm_c_im', 'ssm_d', 'ssm_log_step'), ds5):
            grads[n][i] = gval
        grads['norm_g'][i] = dnorm.reshape(D)
        grads['w_in'][i] = dws
        grads['ssm_w_glu'][i] = dwglu
        grads['ssm_b_glu'][i] = dbglu.reshape(D_SSM)
        grads['dn_conv_w'][i] = dconv[:4]
        grads['dn_a_log'][i] = dal.reshape(H, DH).sum(axis=1)
        grads['dn_dt_bias'][i] = ddt.reshape(H, DH).sum(axis=1)
        grads['dn_norm_g'][i] = dng.reshape(DH)
        grads['sg_ln_g'][i] = dlng.reshape(D_SG)
        grads['sg_ln_b'][i] = dlnb.reshape(D_SG)
        grads['sg_w'][i] = dsgw
        grads['sg_b'][i] = jnp.transpose(dbsp[:, :4])
        grads['w_out'][i] = dwo
        grads['ple_norm_g'][i] = dpg.reshape(D)
        grads['w_ple_gate'][i] = dwg
        grads['w_ple'][i] = dwp
    grads = {n: (g if n in ('final_norm_g', 'w_in') else jnp.stack(g)) for n, g in grads.items()}
    return loss_vec[0, 0], dx.reshape(bl, s, D), grads


def kernel(x, p, norm_g, w_in, ssm_a_re, ssm_a_im, ssm_b_re, ssm_b_im, ssm_c_re, ssm_c_im, ssm_d, ssm_log_step, ssm_w_glu, ssm_b_glu, dn_conv_w, dn_a_log, dn_dt_bias, dn_norm_g, sg_ln_g, sg_ln_b, sg_w, sg_b, w_out, ple_norm_g, w_ple_gate, w_ple, final_norm_g, loss_target, m_norm_g, m_w_in, m_ssm_a_re, m_ssm_a_im, m_ssm_b_re, m_ssm_b_im, m_ssm_c_re, m_ssm_c_im, m_ssm_d, m_ssm_log_step, m_ssm_w_glu, m_ssm_b_glu, m_dn_conv_w, m_dn_a_log, m_dn_dt_bias, m_dn_norm_g, m_sg_ln_g, m_sg_ln_b, m_sg_w, m_sg_b, m_w_out, m_ple_norm_g, m_w_ple_gate, m_w_ple, m_final_norm_g, v_norm_g, v_w_in, v_ssm_a_re, v_ssm_a_im, v_ssm_b_re, v_ssm_b_im, v_ssm_c_re, v_ssm_c_im, v_ssm_d, v_ssm_log_step, v_ssm_w_glu, v_ssm_b_glu, v_dn_conv_w, v_dn_a_log, v_dn_dt_bias, v_dn_norm_g, v_sg_ln_g, v_sg_ln_b, v_sg_w, v_sg_b, v_w_out, v_ple_norm_g, v_w_ple_gate, v_w_ple, v_final_norm_g):
    w_loc = dict(zip(WEIGHTS, (norm_g, w_in, ssm_a_re, ssm_a_im, ssm_b_re, ssm_b_im, ssm_c_re, ssm_c_im, ssm_d, ssm_log_step,
                               ssm_w_glu, ssm_b_glu, dn_conv_w, dn_a_log, dn_dt_bias, dn_norm_g, sg_ln_g, sg_ln_b, sg_w, sg_b,
                               w_out, ple_norm_g, w_ple_gate, w_ple, final_norm_g)))
    m_loc = dict(zip(WEIGHTS, (m_norm_g, m_w_in, m_ssm_a_re, m_ssm_a_im, m_ssm_b_re, m_ssm_b_im, m_ssm_c_re, m_ssm_c_im, m_ssm_d,
                               m_ssm_log_step, m_ssm_w_glu, m_ssm_b_glu, m_dn_conv_w, m_dn_a_log, m_dn_dt_bias, m_dn_norm_g,
                               m_sg_ln_g, m_sg_ln_b, m_sg_w, m_sg_b, m_w_out, m_ple_norm_g, m_w_ple_gate, m_w_ple, m_final_norm_g)))
    v_loc = dict(zip(WEIGHTS, (v_norm_g, v_w_in, v_ssm_a_re, v_ssm_a_im, v_ssm_b_re, v_ssm_b_im, v_ssm_c_re, v_ssm_c_im, v_ssm_d,
                               v_ssm_log_step, v_ssm_w_glu, v_ssm_b_glu, v_dn_conv_w, v_dn_a_log, v_dn_dt_bias, v_dn_norm_g,
                               v_sg_ln_g, v_sg_ln_b, v_sg_w, v_sg_b, v_w_out, v_ple_norm_g, v_w_ple_gate, v_w_ple, v_final_norm_g)))

    depth = p.shape[0]
    shards = [[w_loc[n][l:l + 1].astype(WIRE[n]) for n in SHARDED_ORDER] for l in range(depth)]
    first_layer = _layer_weights(_all_gather(shards[0], "gather_weights_l0"), 0)

    loss_part, grad_x, grads = _local_step(x, p, w_loc, [first_layer] + [None] * (depth - 1), loss_target, shards)

    dest = [_split_dw_in(grads[n], "split_dw_in") if n == 'w_in' else _to_dest_blocks(grads[n], SHARDED[n], WIRE[n])
            for n in SHARDED_ORDER]
    c = lax.axis_index("c")
    own = [lax.dynamic_index_in_dim(d, c, 0, keepdims=False) for d in dest]
    for_sibling = [lax.dynamic_index_in_dim(d, 1 - c, 0, keepdims=False) for d in dest]
    from_sibling = _pair_exchange(for_sibling, "grads_pair_exchange")
    chip_sums = [_add_pair(a, b, f"grads_pair_sum_{n}") for n, a, b in zip(SHARDED_ORDER, own, from_sibling)]
    rep_pack = _pack([grads[n] for n in REPLICATED_ORDER] + [loss_part.reshape(1)])
    by_chip, (rep_recv,) = _chip_exchange(chip_sums, [rep_pack], "grads_chip_exchange")

    outs = {k: {} for k in 'gdmv'}
    for n, gk in zip(SHARDED_ORDER, by_chip):
        for k, o in zip('gdmv', _sum_adamw(gk, w_loc[n], m_loc[n], v_loc[n], f"adamw_{n}")):
            outs[k][n] = o
    one = jnp.zeros((1,), F32)
    rep_out = _sum_adamw(rep_recv, _pack([w_loc[n] for n in REPLICATED_ORDER] + [one]),
                         _pack([m_loc[n] for n in REPLICATED_ORDER] + [one]),
                         _pack([v_loc[n] for n in REPLICATED_ORDER] + [one]), "adamw_replicated")
    rep_shapes = [w_loc[n].shape for n in REPLICATED_ORDER] + [(1,)]
    for k, rep_p in zip('gdmv', rep_out):
        outs[k].update(zip(REPLICATED_ORDER + ['loss'], _unpack(rep_p, rep_shapes)))
    loss = outs['g']['loss'].reshape(())
    return (loss, grad_x, *[outs['g'][n] for n in WEIGHTS], *[outs['d'][n] for n in WEIGHTS],
            *[outs['m'][n] for n in WEIGHTS], *[outs['v'][n] for n in WEIGHTS])
```

```python
import functools

import jax
import jax.numpy as jnp
from jax import lax
from jax.experimental import pallas as pl
from jax.experimental.pallas import tpu as pltpu

F32 = jnp.float32
BF16 = jnp.bfloat16
EPS = 1e-6

D = 1024
D_PLE = 256
D_SSM = 256
D_DN = 512
D_SG = 256
G = 16
CG = 16
NS = 64
NRE = G * NS
H = 4
DH = 128
DN_C = 128
SG_C = 128
ZW = 3456
Z_PIECES = (512, 1536, 512, 768, 128)
N_DEV = 8
LANES = 128
PACK_ROWS = 256
VMEM_LIMIT = 56 * 1024 * 1024
ELEMENTWISE_STEP_BYTES = 4 * 1024 * 1024
TB = 256
SG_TB = 512
TB_DW = 512

ADAM_LR = 0.001
ADAM_B1 = 0.9
ADAM_B2 = 0.999
ADAM_EPS = 1e-08
ADAM_WD = 0.01
ADAM_STEP = 10

MIX_HEAD_START = 3
S5_L = 128
S5_GROUP = 8
S5_SHIFTS = (1, 2, 4)

WEIGHTS = ['norm_g', 'w_in', 'ssm_a_re', 'ssm_a_im', 'ssm_b_re', 'ssm_b_im', 'ssm_c_re', 'ssm_c_im', 'ssm_d',
           'ssm_log_step', 'ssm_w_glu', 'ssm_b_glu', 'dn_conv_w', 'dn_a_log', 'dn_dt_bias', 'dn_norm_g', 'sg_ln_g',
           'sg_ln_b', 'sg_w', 'sg_b', 'w_out', 'ple_norm_g', 'w_ple_gate', 'w_ple', 'final_norm_g']
SHARDED = {'w_in': 2, 'ssm_w_glu': 1, 'dn_conv_w': 2, 'w_out': 1, 'w_ple_gate': 1, 'w_ple': 2}
SHARDED_ORDER = ['w_in', 'ssm_w_glu', 'dn_conv_w', 'w_out', 'w_ple_gate', 'w_ple']
WIRE = {'w_in': BF16, 'ssm_w_glu': BF16, 'dn_conv_w': F32, 'w_out': BF16, 'w_ple_gate': BF16, 'w_ple': BF16}
REPLICATED_ORDER = [n for n in WEIGHTS if n not in SHARDED]


def _cp(*sem):
    return pltpu.CompilerParams(dimension_semantics=sem, vmem_limit_bytes=VMEM_LIMIT)


def _dg(a, b, ca, cb, precision=None):
    return lax.dot_general(a, b, (((ca,), (cb,)), ((), ())), precision=precision, preferred_element_type=F32)


@jax.custom_vjp
def _mm(a, b):
    return _dg(a.astype(BF16), b.astype(BF16), 1, 0)


def _mm_fwd(a, b):
    return _mm(a, b), (a, b)


def _mm_bwd(res, g):
    a, b = res
    gb = g.astype(BF16)
    return _dg(gb, b.astype(BF16), 1, 1), _dg(a.astype(BF16), gb, 0, 0)


_mm.defvjp(_mm_fwd, _mm_bwd)


@jax.custom_vjp
def _mm_nt(a, b):
    return _dg(a.astype(BF16), b.astype(BF16), 1, 1)


def _mm_nt_fwd(a, b):
    return _mm_nt(a, b), (a, b)


def _mm_nt_bwd(res, g):
    a, b = res
    gb = g.astype(BF16)
    return _dg(gb, b.astype(BF16), 1, 0), _dg(gb, a.astype(BF16), 0, 0)


_mm_nt.defvjp(_mm_nt_fwd, _mm_nt_bwd)


@jax.custom_vjp
def _mm_tn(a, b):
    return _dg(a.astype(BF16), b.astype(BF16), 0, 0)


def _mm_tn_fwd(a, b):
    return _mm_tn(a, b), (a, b)


def _mm_tn_bwd(res, g):
    a, b = res
    gb = g.astype(BF16)
    return _dg(b.astype(BF16), gb, 1, 1), _dg(a.astype(BF16), gb, 1, 0)


_mm_tn.defvjp(_mm_tn_fwd, _mm_tn_bwd)


def _split(x, n):
    pieces = []
    for _ in range(n - 1):
        hi = x.astype(BF16)
        pieces.append(hi)
        x = x - hi.astype(F32)
    pieces.append(x.astype(BF16))
    return pieces


def _dg3(a, b, ca, cb):
    a_hi, a_lo = _split(a, 2)
    b_hi, b_lo = _split(b, 2)
    return _dg(a_hi, b_hi, ca, cb) + (_dg(a_hi, b_lo, ca, cb) + _dg(a_lo, b_hi, ca, cb))


@jax.custom_vjp
def _dot3(a, b):
    return _dg3(a, b, 1, 0)


def _dot3_fwd(a, b):
    return _dot3(a, b), (a, b)


def _dot3_bwd(res, g):
    a, b = res
    return _dg3(g, b, 1, 1), _dg3(a, g, 0, 0)


_dot3.defvjp(_dot3_fwd, _dot3_bwd)


def _dg_sel(x, e, cx, ce, x_first):
    eb = e.astype(BF16)
    out = None
    for piece in reversed(_split(x, 3)):
        term = _dg(piece, eb, cx, ce) if x_first else _dg(eb, piece, ce, cx)
        out = term if out is None else out + term
    return out


@jax.custom_vjp
def _sel_r(x, e):
    return _dg_sel(x, e, 1, 0, True)


def _sel_r_fwd(x, e):
    return _sel_r(x, e), e


def _sel_r_bwd(e, g):
    return _dg_sel(g, e, 1, 1, True), jnp.zeros_like(e)


_sel_r.defvjp(_sel_r_fwd, _sel_r_bwd)


@jax.custom_vjp
def _sel_l(e, x):
    return _dg_sel(x, e, 0, 1, False)


def _sel_l_fwd(e, x):
    return _sel_l(e, x), e


def _sel_l_bwd(e, g):
    return jnp.zeros_like(e), _dg_sel(g, e, 0, 0, False)


_sel_l.defvjp(_sel_l_fwd, _sel_l_bwd)


def _rms(x, g):
    return x * lax.rsqrt(jnp.mean(x * x, axis=-1, keepdims=True) + EPS) * g


def _silu(x):
    return x * jax.nn.sigmoid(x)


Z_OFFSETS = (0, 512, 2048, 2560, 3328)


def _dn_post(c):
    s = _silu(c)
    parts = []
    for j in range(12):
        xj = s[:, j * DH:(j + 1) * DH]
        if j < 8:
            xj = xj * lax.rsqrt(jnp.sum(xj * xj, axis=-1, keepdims=True) + EPS)
        if j < 4:
            xj = xj * (DH ** -0.5)
        parts.append(xj)
    return jnp.concatenate(parts, axis=1)


def _dn_conv(ext, cw_ref, rows):
    c = None
    for k in range(4):
        sh = ext if k == 3 else pltpu.roll(ext, 3 - k, 0)
        term = cw_ref[k:k + 1, :] * sh[ext.shape[0] - rows:, :]
        c = term if c is None else c + term
    return c


def _dn_prep_vjp(prev, cur, nxt, d_cur, d_nxt, cw_ref, tb):
    ext = jnp.concatenate([prev, cur, nxt], axis=0)
    shifted = [ext if k == 3 else pltpu.roll(ext, 3 - k, 0) for k in range(4)]
    c2 = None
    for k in range(4):
        term = cw_ref[k:k + 1, :] * shifted[k][8:, :]
        c2 = term if c2 is None else c2 + term
    _, vjp = jax.vjp(_dn_post, c2)
    (dc2,) = vjp(jnp.concatenate([d_cur, d_nxt], axis=0))
    dz, dcw = None, []
    for k in range(4):
        up = dc2 if k == 3 else pltpu.roll(dc2, tb + 8 - (3 - k), 0)
        term = cw_ref[k:k + 1, :] * up[:tb, :]
        dz = term if dz is None else dz + term
        dcw.append(jnp.sum(dc2[:tb, :] * shifted[k][8:8 + tb, :], axis=0, keepdims=True))
    return dz, dcw


def _in_proj_fwd(x, g, w, conv_w8, s, tb, name):
    t = x.shape[0]
    n_s = s // tb
    w3 = 3 * D_DN
    q0, q1 = Z_OFFSETS[1], Z_OFFSETS[2]

    def body(x_ref, g_ref, w_ref, cw_ref, zs_ref, zq_ref, zg_ref, zsg_ref, zab_ref, qkvn_ref, halo):
        h = _rms(x_ref[...], g_ref[...]).astype(BF16)
        zq = jnp.dot(h, w_ref[:, q0:q1], preferred_element_type=F32)
        zq_ref[...] = zq
        prev = jnp.where(pl.program_id(0) % n_s == 0, 0.0, halo[...])
        qkvn_ref[...] = _dn_post(_dn_conv(jnp.concatenate([prev, zq], axis=0), cw_ref, tb))
        halo[...] = zq[tb - 8:, :]
        zs_ref[...] = jnp.dot(h, w_ref[:, :q0], preferred_element_type=F32)
        rest = jnp.dot(h, w_ref[:, q1:], preferred_element_type=F32)
        zg_ref[...] = rest[:, :Z_PIECES[2]]
        zsg_ref[...] = rest[:, Z_PIECES[2]:Z_PIECES[2] + Z_PIECES[3]]
        zab_ref[...] = rest[:, Z_PIECES[2] + Z_PIECES[3]:]

    row = lambda i: (i, 0)
    full = lambda i: (0, 0)
    widths = Z_PIECES + (w3,)
    return pl.pallas_call(
        body, grid=(t // tb,),
        in_specs=[pl.BlockSpec((tb, D), row), pl.BlockSpec((1, D), full), pl.BlockSpec((D, ZW), full), pl.BlockSpec((8, w3), full)],
        out_specs=[pl.BlockSpec((tb, n), row) for n in widths],
        out_shape=[jax.ShapeDtypeStruct((t, n), F32) for n in widths],
        scratch_shapes=[pltpu.VMEM((8, w3), F32)],
        name=name, compiler_params=_cp("arbitrary"))(x, g, w, conv_w8)


def _in_proj_bwd_dx(x, g, w, conv_w8, dz_ssm, dz_gdn, dz_sg, dz_ab, zq, dqkvn, dx_res, s, tb, name):
    t = x.shape[0]
    n_s = s // tb
    hb = tb // 8
    w3 = 3 * D_DN
    q0, q1 = Z_OFFSETS[1], Z_OFFSETS[2]

    def body(x_ref, g_ref, w_ref, cw_ref, ds_ref, dgd_ref, dsg_ref, dab_ref, cur_ref, prev_ref, next_ref, dq_ref, dqn_ref,
             dxr_ref, dx_ref, dg_ref, dzq_ref, dcw_ref):
        i = pl.program_id(0)

        @pl.when(i == 0)
        def _():
            dg_ref[...] = jnp.zeros_like(dg_ref)
            dcw_ref[...] = jnp.zeros_like(dcw_ref)

        rest = jnp.concatenate([dgd_ref[...], dsg_ref[...], dab_ref[...]], axis=1)
        dh = _dg(ds_ref[...], w_ref[:, :q0], 1, 1) + _dg(rest, w_ref[:, q1:], 1, 1)
        first, last = i % n_s == 0, i % n_s == n_s - 1
        dzq, dcw = _dn_prep_vjp(jnp.where(first, 0.0, prev_ref[...]), cur_ref[...], jnp.where(last, 0.0, next_ref[...]),
                                dq_ref[...], jnp.where(last, 0.0, dqn_ref[...]), cw_ref, tb)
        for k in range(4):
            dcw_ref[k:k + 1, :] += dcw[k]
        dzq = dzq.astype(BF16)
        dzq_ref[...] = dzq
        dh = dh + _dg(dzq, w_ref[:, q0:q1], 1, 1)
        _, vjp = jax.vjp(_rms, x_ref[...], g_ref[...])
        dx, dg = vjp(dh)
        dx_ref[...] = dx + dxr_ref[...]
        dg_ref[...] += dg

    n_blk8 = t // 8
    row = lambda i: (i, 0)
    prv = lambda i: (jnp.maximum(i * hb - 1, 0), 0)
    nxt = lambda i: (jnp.minimum((i + 1) * hb, n_blk8 - 1), 0)
    full = lambda i: (0, 0)
    return pl.pallas_call(
        body, grid=(t // tb,),
        in_specs=[pl.BlockSpec((tb, D), row), pl.BlockSpec((1, D), full), pl.BlockSpec((D, ZW), full), pl.BlockSpec((8, w3), full)]
        + [pl.BlockSpec((tb, n), row) for n in (Z_PIECES[0], Z_PIECES[2], Z_PIECES[3], Z_PIECES[4])]
        + [pl.BlockSpec((tb, w3), row), pl.BlockSpec((8, w3), prv), pl.BlockSpec((8, w3), nxt),
           pl.BlockSpec((tb, w3), row), pl.BlockSpec((8, w3), nxt), pl.BlockSpec((tb, D), row)],
        out_specs=[pl.BlockSpec((tb, D), row), pl.BlockSpec((1, D), full), pl.BlockSpec((tb, w3), row), pl.BlockSpec((8, w3), full)],
        out_shape=[jax.ShapeDtypeStruct((t, D), F32), jax.ShapeDtypeStruct((1, D), F32), jax.ShapeDtypeStruct((t, w3), BF16),
                   jax.ShapeDtypeStruct((8, w3), F32)],
        name=name, compiler_params=_cp("arbitrary"))(x, g, w, conv_w8, dz_ssm, dz_gdn, dz_sg, dz_ab, zq, zq, zq, dqkvn, dqkvn, dx_res)


def _in_proj_bwd_dw(x, g, dzs, tb, name):
    t = x.shape[0]

    def body(x_ref, g_ref, d0, d1, d2, d3, d4, *dw_refs):
        @pl.when(pl.program_id(0) == 0)
        def _():
            for r in dw_refs:
                r[...] = jnp.zeros_like(r)

        h = _rms(x_ref[...], g_ref[...]).astype(BF16)
        for d_ref, dw_ref in zip((d0, d1, d2, d3, d4), dw_refs):
            dw_ref[...] += _dg(h, d_ref[...].astype(BF16), 0, 0)

    row = lambda i: (i, 0)
    full = lambda i: (0, 0)
    return pl.pallas_call(
        body, grid=(t // tb,),
        in_specs=[pl.BlockSpec((tb, D), row), pl.BlockSpec((1, D), full)] + [pl.BlockSpec((tb, n), row) for n in Z_PIECES],
        out_specs=[pl.BlockSpec((D, n), full) for n in Z_PIECES],
        out_shape=[jax.ShapeDtypeStruct((D, n), F32) for n in Z_PIECES],
        name=name, compiler_params=_cp("arbitrary"))(x, g, *dzs)


def _lam_pow(a_re, a_im, log_step, k):
    step = jnp.exp(log_step)[:, None]
    mag = jnp.exp(k * a_re * step)
    ang = k * a_im * step
    return mag * jnp.cos(ang), mag * jnp.sin(ang)


def _s5_powers(a_re, a_im, log_step):
    def table(ks):
        re, im = _lam_pow(a_re, a_im, log_step, jnp.asarray(ks, F32)[:, None, None])
        return jnp.concatenate([re.reshape(len(ks), NRE), im.reshape(len(ks), NRE)], axis=-1)

    ld = table(S5_SHIFTS).reshape(len(S5_SHIFTS), 1, 2 * NRE)
    return ld, table(range(1, S5_GROUP + 1)), table(range(S5_GROUP, 0, -1))


def _s5_tables(a_re, a_im, b_re, b_im, c_re, c_im, d_skip, log_step):
    lam_re, lam_im = _lam_pow(a_re, a_im, log_step, 1.0)
    den = a_re * a_re + a_im * a_im
    nr, ni = lam_re - 1.0, lam_im
    f_re = (nr * a_re + ni * a_im) / den
    f_im = (ni * a_re - nr * a_im) / den
    bbar_re = f_re[..., None] * b_re - f_im[..., None] * b_im
    bbar_im = f_re[..., None] * b_im + f_im[..., None] * b_re
    eye = jnp.eye(G, dtype=F32)

    def blk_b(bb):
        return (jnp.transpose(bb, (0, 2, 1))[:, :, None, :] * eye[:, None, :, None]).reshape(D_SSM, NRE)

    def blk_c(cc):
        return (jnp.transpose(cc, (0, 2, 1))[:, :, None, :] * eye[:, None, :, None]).reshape(NRE, D_SSM)

    b_blk = jnp.concatenate([blk_b(bbar_re), blk_b(bbar_im)], axis=1)
    c_blk = jnp.concatenate([blk_c(c_re), -blk_c(c_im)], axis=0)
    lam = jnp.concatenate([lam_re.reshape(1, NRE), lam_im.reshape(1, NRE)], axis=-1)
    return b_blk, c_blk, lam, d_skip.reshape(1, D_SSM)


def _group_shift(x, d, up=False):
    r = lax.broadcasted_iota(jnp.int32, x.shape, 0) & (S5_GROUP - 1)
    if up:
        return jnp.where(r < S5_GROUP - d, pltpu.roll(x, x.shape[0] - d, 0), 0.0)
    return jnp.where(r >= d, pltpu.roll(x, d, 0), 0.0)


def _s5_scan_steps(hr, hi, cr, ci, lds, lp):
    for ld, d in zip(lds, S5_SHIFTS):
        lr, li = ld[:, :NRE], ld[:, NRE:]
        sr, si = _group_shift(hr, d), _group_shift(hi, d)
        hr, hi = hr + lr * sr - li * si, hi + lr * si + li * sr
        yield
    pr, pi = lp[:, :NRE], lp[:, NRE:]
    rows_r, rows_i = [], []
    for r in range(hr.shape[0] // S5_GROUP):
        br, bi = hr[r * S5_GROUP:(r + 1) * S5_GROUP], hi[r * S5_GROUP:(r + 1) * S5_GROUP]
        br, bi = br + pr * cr - pi * ci, bi + pr * ci + pi * cr
        cr, ci = br[S5_GROUP - 1:S5_GROUP], bi[S5_GROUP - 1:S5_GROUP]
        rows_r.append(br)
        rows_i.append(bi)
        if r % 2:
            yield
    return jnp.concatenate(rows_r, axis=0), jnp.concatenate(rows_i, axis=0)


@jax.custom_vjp
def _known_scan(xr, xi, cr, ci, lam, lds, lp_rev, hr, hi):
    return hr, hi


def _known_scan_fwd(xr, xi, cr, ci, lam, lds, lp_rev, hr, hi):
    return (hr, hi), (cr, ci, lam, lds, lp_rev, hr, hi)


def _known_scan_bwd(res, cts):
    cr, ci, lam, lds, lp_rev, hr, hi = res
    ar, ai = cts
    for ld, d in zip(lds, S5_SHIFTS):
        lr, li = ld[:, :NRE], ld[:, NRE:]
        sr, si = _group_shift(ar, d, up=True), _group_shift(ai, d, up=True)
        ar, ai = ar + lr * sr + li * si, ai + lr * si - li * sr
    qr, qi = lp_rev[:, :NRE], lp_rev[:, NRE:]
    nr, ni = jnp.zeros_like(cr), jnp.zeros_like(ci)
    rows_r, rows_i = [], []
    for r in reversed(range(hr.shape[0] // S5_GROUP)):
        br, bi = ar[r * S5_GROUP:(r + 1) * S5_GROUP], ai[r * S5_GROUP:(r + 1) * S5_GROUP]
        br, bi = br + qr * nr + qi * ni, bi + qr * ni - qi * nr
        nr, ni = br[0:1], bi[0:1]
        rows_r.insert(0, br)
        rows_i.insert(0, bi)
    ar, ai = jnp.concatenate(rows_r, axis=0), jnp.concatenate(rows_i, axis=0)
    lr, li = lam[:, :NRE], lam[:, NRE:]
    dcr, dci = lr * nr + li * ni, lr * ni - li * nr
    first = lax.broadcasted_iota(jnp.int32, hr.shape, 0) == 0
    pr = jnp.where(first, cr, pltpu.roll(hr, 1, 0))
    pi = jnp.where(first, ci, pltpu.roll(hi, 1, 0))
    dlam = jnp.concatenate([jnp.sum(ar * pr + ai * pi, axis=0, keepdims=True),
                            jnp.sum(ai * pr - ar * pi, axis=0, keepdims=True)], axis=1)
    return (ar, ai, dcr, dci, dlam, [jnp.zeros_like(ld) for ld in lds], jnp.zeros_like(lp_rev),
            jnp.zeros_like(hr), jnp.zeros_like(hi))


_known_scan.defvjp(_known_scan_fwd, _known_scan_bwd)


def _interleave(short, long, head_start=0):
    gens = list(short) + list(long)
    results = [None] * len(gens)

    def advance(live):
        still = []
        for idx, gen in live:
            try:
                next(gen)
                still.append((idx, gen))
            except StopIteration as done:
                results[idx] = done.value
        return still

    live_short = advance(list(enumerate(gens))[:len(short)])
    live_long = list(enumerate(gens))[len(short):]
    for _ in range(head_start):
        live_long = advance(live_long)
    live = live_short + live_long
    while live:
        live = advance(live)
    return results[:len(short)], results[len(short):]


def _s5_chunk_gen(u, gate, cr, ci, b_blk, c_blk, lam, dv, wglu, bglu, lds, lp, lp_rev, known_h=None):
    bu = _mm(u, b_blk)
    xr, xi = bu[:, :NRE], bu[:, NRE:]
    yield
    if known_h is None:
        hr, hi = yield from _s5_scan_steps(xr, xi, cr, ci, lds, lp)
    else:
        hr, hi = _known_scan(xr, xi, cr, ci, lam, lds, lp_rev, *known_h)
    y = _mm(jnp.concatenate([hr, hi], axis=1), c_blk) + dv * u
    yield
    y = jax.nn.gelu(y)
    y = y * jax.nn.sigmoid(_mm(y, wglu) + bglu)
    return y * _silu(gate), hr, hi


S5_PAR_SHAPES = [(D_SSM, 2 * NRE), (2 * NRE, D_SSM), (1, 2 * NRE), (1, D_SSM), (D_SSM, D_SSM), (1, D_SSM)]
S5_CONST_SHAPES = [(len(S5_SHIFTS), 1, 2 * NRE), (S5_GROUP, 2 * NRE), (S5_GROUP, 2 * NRE)]
DN_PAR_SHAPES = [(1, D_DN), (1, D_DN), (1, DH)]


def _mix_specs(bl, n_c, rev):
    def chunk(i):
        return n_c - 1 - i if rev else i

    def tok(n):
        return pl.BlockSpec((bl, DN_C, n), lambda i: (0, chunk(i), 0))

    def per_chunk(shape):
        return pl.BlockSpec((bl, 1, *shape), lambda i: (0, chunk(i)) + (0,) * len(shape))

    def whole(shape):
        return pl.BlockSpec(shape, lambda i: (0,) * len(shape))

    return tok, per_chunk, whole


def _unit_lower_inverse_steps(ms):
    c_len = ms[0].shape[0]
    eye = lax.broadcasted_iota(jnp.int32, (c_len, c_len), 0) == lax.broadcasted_iota(jnp.int32, (c_len, c_len), 1)
    ident = jnp.where(eye, 1.0, 0.0)
    ps = ms
    tinvs = [ident - m for m in ms]
    for _ in range(c_len.bit_length() - 2):
        ps = [_dg3(p, p, 1, 0) for p in ps]
        yield
        tinvs = [t + _dg3(t, p, 1, 0) for t, p in zip(tinvs, ps)]
        yield
    return tinvs


@jax.custom_vjp
def _known_inverses(ms, tinvs):
    return tinvs


def _known_inverses_fwd(ms, tinvs):
    return tinvs, tinvs


def _known_inverses_bwd(tinvs, gs):
    return [-_dg3(_dg3(t, g, 0, 0), t, 1, 1) for t, g in zip(tinvs, gs)], [jnp.zeros_like(t) for t in tinvs]


_known_inverses.defvjp(_known_inverses_fwd, _known_inverses_bwd)


def _dn_chunk_gen(qkv, zab, zg, states, alog_e, dt_e, ng, known_tinvs=None):
    c_len = DN_C
    r = lax.broadcasted_iota(jnp.int32, (c_len, c_len), 0)
    c = lax.broadcasted_iota(jnp.int32, (c_len, c_len), 1)
    causal, strict = r >= c, r > c
    tril = jnp.where(causal, 1.0, 0.0)
    rr = lax.broadcasted_iota(jnp.int32, (LANES, D_DN), 0)
    cc = lax.broadcasted_iota(jnp.int32, (LANES, D_DN), 1)
    e_a = jnp.where((cc >= rr * DH) & (cc < rr * DH + DH) & (rr < H), 1.0, 0.0)
    e_b = jnp.where((cc >= (rr - H) * DH) & (cc < (rr - H) * DH + DH) & (rr >= H) & (rr < 2 * H), 1.0, 0.0)
    a_e = _sel_r(zab, e_a)
    b_e = _sel_r(zab, e_b)
    beta = jax.nn.sigmoid(b_e)
    g = -jnp.exp(alog_e) * jax.nn.softplus(a_e + dt_e)
    yield
    gc = _sel_l(tril, g)
    glast = jnp.sum(g, axis=0, keepdims=True)
    eg = jnp.exp(gc)
    ekd = jnp.exp(glast - gc)
    dl = jnp.exp(glast)
    yield
    heads = range(H)
    sls = [slice(h * DH, (h + 1) * DH) for h in heads]
    qs = [qkv[:, h * DH:(h + 1) * DH] for h in heads]
    ks = [qkv[:, D_DN + h * DH:D_DN + (h + 1) * DH] for h in heads]
    vs = [qkv[:, 2 * D_DN + h * DH:2 * D_DN + (h + 1) * DH] for h in heads]
    ccols = [gc[:, sl] for sl in sls]
    decs = [jnp.where(causal, jnp.exp(jnp.where(causal, cl - jnp.transpose(cl), 0.0)), 0.0) for cl in ccols]
    kbs = [k * beta[:, sl] for k, sl in zip(ks, sls)]
    ms = [jnp.where(strict, _mm_nt(kb, k) * dec, 0.0) for kb, k, dec in zip(kbs, ks, decs)]
    yield
    if known_tinvs is None:
        tinvs = yield from _unit_lower_inverse_steps(ms)
    else:
        tinvs = _known_inverses(ms, list(known_tinvs))
    sols = [_dot3(t, jnp.concatenate([v * beta[:, sl], kb * eg[:, sl]], axis=1))
            for t, v, kb, sl in zip(tinvs, vs, kbs, sls)]
    yield
    atts = [_mm_nt(q, k) * dec for q, k, dec in zip(qs, ks, decs)]
    vnews = [sol[:, :DH] - _mm(sol[:, DH:], st) for sol, st in zip(sols, states)]
    yield
    os_ = [_mm(q * eg[:, sl], st) + _mm(att, vn) for q, sl, st, att, vn in zip(qs, sls, states, atts, vnews)]
    yield
    new_states = [st * dl[:, sl] + _mm_tn(k * ekd[:, sl], vn) for st, sl, k, vn in zip(states, sls, ks, vnews)]
    yield
    ys = [_rms(o, ng) * _silu(zg[:, sl]) for o, sl in zip(os_, sls)]
    return jnp.concatenate(ys, axis=1), new_states, tinvs


def _mix_fwd(zs, s5_par, s5_const, qkv, zab, zg, dn_par, bl, s, name, gather=()):
    assert S5_L == DN_C
    n_c = s // DN_C
    nd = len(S5_SHIFTS)
    m = len(gather)
    tok, per_chunk, whole = _mix_specs(bl, n_c, False)

    def body(*refs):
        (z_ref, b_ref, c_ref, lam_ref, dv_ref, wg_ref, bg_ref, ld_ref, lp_ref, lpr_ref,
         q_ref, ab_ref, zg_ref, al_ref, dt_ref, ng_ref) = refs[:16]
        ys_ref, car_ref, h_ref, yd_ref, st_ref, ti_ref = refs[16 + m:22 + m]
        cs, ssc = refs[22 + 2 * m:24 + 2 * m]
        gathering = (_gather_protocol(_mesh_place(), refs[16:16 + m], refs[22 + m:22 + 2 * m], *refs[24 + 2 * m:])
                     if m else None)

        @pl.when(pl.program_id(0) == 0)
        def _():
            cs[...] = jnp.zeros_like(cs)
            ssc[...] = jnp.zeros_like(ssc)
            if m:
                next(gathering)

        lds = [ld_ref[k] for k in range(nd)]
        s5_gens, dn_gens = [], []
        for e in range(bl):
            c = cs[e]
            car_ref[e, 0] = c
            sts = [ssc[e, h] for h in range(H)]
            for h in range(H):
                st_ref[e, 0, h] = sts[h]
            z = z_ref[e]
            s5_gens.append(_s5_chunk_gen(z[:, :D_SSM], z[:, D_SSM:], c[:, :NRE], c[:, NRE:], b_ref[...], c_ref[...], lam_ref[...],
                                         dv_ref[...], wg_ref[...], bg_ref[...], lds, lp_ref[...], lpr_ref[...]))
            dn_gens.append(_dn_chunk_gen(q_ref[e], ab_ref[e], zg_ref[e], sts, al_ref[...], dt_ref[...], ng_ref[...]))
        s5_outs, dn_outs = _interleave(s5_gens, dn_gens, head_start=MIX_HEAD_START)
        for e in range(bl):
            y_s, hr, hi = s5_outs[e]
            y_d, new_sts, tinvs = dn_outs[e]
            ys_ref[e] = y_s
            h_ref[e, :, :NRE] = hr
            h_ref[e, :, NRE:] = hi
            cs[e, :, :NRE] = hr[S5_L - 1:S5_L]
            cs[e, :, NRE:] = hi[S5_L - 1:S5_L]
            yd_ref[e] = y_d
            for h in range(H):
                ssc[e, h] = new_sts[h]
                ti_ref[e, 0, h] = tinvs[h]

        if m:
            @pl.when(pl.program_id(0) == n_c - 1)
            def _():
                for _ in gathering:
                    pass

    head_mats = jax.ShapeDtypeStruct((bl, n_c, H, DH, DH), F32)
    outs = pl.pallas_call(
        body, grid=(n_c,),
        in_specs=[tok(2 * D_SSM)] + [whole(sh) for sh in S5_PAR_SHAPES + S5_CONST_SHAPES]
        + [tok(3 * D_DN), tok(LANES), tok(D_DN)] + [whole(sh) for sh in DN_PAR_SHAPES] + _hbm_specs(m),
        out_specs=[tok(D_SSM), per_chunk((1, 2 * NRE)), tok(2 * NRE), tok(D_DN), per_chunk((H, DH, DH)), per_chunk((H, DH, DH))]
        + _hbm_specs(m),
        out_shape=[jax.ShapeDtypeStruct((bl, s, D_SSM), F32), jax.ShapeDtypeStruct((bl, n_c, 1, 2 * NRE), F32),
                   jax.ShapeDtypeStruct((bl, s, 2 * NRE), F32), jax.ShapeDtypeStruct((bl, s, D_DN), F32), head_mats, head_mats]
        + [jax.ShapeDtypeStruct((N_DEV, *b.shape), b.dtype) for b in gather],
        scratch_shapes=[pltpu.VMEM((bl, 1, 2 * NRE), F32), pltpu.VMEM((bl, H, DH, DH), F32)] + (_gather_sems(m) if m else []),
        name=name, compiler_params=_cp("arbitrary"))(zs, *s5_par, *s5_const, qkv, zab, zg, *dn_par, *gather)
    return outs[:6], outs[6:]


def _mix_bwd(zs, carries, h_all, dy_s, s5_par, s5_const, qkv, zab, zg, states, tinvs, dy_d, dn_par, bl, s, name, exchange=()):
    n_c = s // DN_C
    nd = len(S5_SHIFTS)
    k_ex = len(exchange)
    tok, per_chunk, whole = _mix_specs(bl, n_c, True)

    def both(examples, s5_tabs, s5_consts, dn_tabs):
        s5_gens = [_s5_chunk_gen(u, gate, cr, ci, *s5_tabs, *s5_consts, known_h=(hr, hi))
                   for u, gate, cr, ci, hr, hi, _, _, _, _, _ in examples]
        dn_gens = [_dn_chunk_gen(q, ab, zgate, sts, *dn_tabs, known_tinvs=known)
                   for _, _, _, _, _, _, q, ab, zgate, sts, known in examples]
        s5_outs, dn_outs = _interleave(s5_gens, dn_gens, head_start=MIX_HEAD_START)
        return [(y_s, hr[S5_L - 1:S5_L], hi[S5_L - 1:S5_L], y_d, new_sts)
                for (y_s, hr, hi), (y_d, new_sts, _) in zip(s5_outs, dn_outs)]

    def body(*refs):
        (z_ref, car_ref, h_ref, dys_ref, b_ref, c_ref, lam_ref, dv_ref, wg_ref, bg_ref, ld_ref, lp_ref, lpr_ref,
         q_ref, ab_ref, zg_ref, st_ref, ti_ref, dyd_ref, al_ref, dt_ref, ng_ref) = refs[:22]
        (dz_ref, db_ref, dc_ref, dlam_ref, ddv_ref, dwg_ref, dbg_ref,
         dq_ref, dab_ref, dzg_ref, dal_ref, ddt_ref, dng_ref) = refs[22 + k_ex:35 + k_ex]
        dcs, dsc = refs[35 + 2 * k_ex:37 + 2 * k_ex]
        accs = (db_ref, dc_ref, dlam_ref, ddv_ref, dwg_ref, dbg_ref, dal_ref, ddt_ref, dng_ref)
        exchanging = (_chip_protocol(_mesh_place(), refs[22:22 + k_ex], refs[35 + k_ex:35 + 2 * k_ex], *refs[37 + 2 * k_ex:])
                      if k_ex else None)

        @pl.when(pl.program_id(0) == 0)
        def _():
            for r in accs + (dcs, dsc):
                r[...] = jnp.zeros_like(r)
            if k_ex:
                next(exchanging)

        examples = []
        for e in range(bl):
            z = z_ref[e]
            c = car_ref[e, 0]
            examples.append((z[:, :D_SSM], z[:, D_SSM:], c[:, :NRE], c[:, NRE:], h_ref[e, :, :NRE], h_ref[e, :, NRE:],
                             q_ref[e], ab_ref[e], zg_ref[e], [st_ref[e, 0, h] for h in range(H)],
                             [ti_ref[e, 0, h] for h in range(H)]))
        _, vjp = jax.vjp(both, examples,
                         (b_ref[...], c_ref[...], lam_ref[...], dv_ref[...], wg_ref[...], bg_ref[...]),
                         ([ld_ref[k] for k in range(nd)], lp_ref[...], lpr_ref[...]),
                         (al_ref[...], dt_ref[...], ng_ref[...]))
        cts = []
        for e in range(bl):
            dc = dcs[e]
            cts.append((dys_ref[e], dc[:, :NRE], dc[:, NRE:], dyd_ref[e], [dsc[e, h] for h in range(H)]))
        d_examples, d_s5, _, d_dn = vjp(cts)
        for e in range(bl):
            du, dgate, dcr, dci, _, _, dq, dab, dzg, dsts, _ = d_examples[e]
            dz_ref[e] = jnp.concatenate([du, dgate], axis=1).astype(BF16)
            dcs[e, :, :NRE] = dcr
            dcs[e, :, NRE:] = dci
            dq_ref[e] = dq
            dab_ref[e] = dab.astype(BF16)
            dzg_ref[e] = dzg.astype(BF16)
            for h in range(H):
                dsc[e, h] = dsts[h]
        for r, ct in zip(accs, (*d_s5, *d_dn)):
            r[...] += ct

        if k_ex:
            @pl.when(pl.program_id(0) == n_c - 1)
            def _():
                for _ in exchanging:
                    pass

    head_mats = per_chunk((H, DH, DH))
    outs = pl.pallas_call(
        body, grid=(n_c,),
        in_specs=[tok(2 * D_SSM), per_chunk((1, 2 * NRE)), tok(2 * NRE), tok(D_SSM)]
        + [whole(sh) for sh in S5_PAR_SHAPES + S5_CONST_SHAPES]
        + [tok(3 * D_DN), tok(LANES), tok(D_DN), head_mats, head_mats, tok(D_DN)] + [whole(sh) for sh in DN_PAR_SHAPES]
        + _hbm_specs(k_ex),
        out_specs=[tok(2 * D_SSM)] + [whole(sh) for sh in S5_PAR_SHAPES]
        + [tok(3 * D_DN), tok(LANES), tok(D_DN)] + [whole(sh) for sh in DN_PAR_SHAPES] + _hbm_specs(k_ex),
        out_shape=[jax.ShapeDtypeStruct((bl, s, 2 * D_SSM), BF16)] + [jax.ShapeDtypeStruct(sh, F32) for sh in S5_PAR_SHAPES]
        + [jax.ShapeDtypeStruct((bl, s, 3 * D_DN), F32), jax.ShapeDtypeStruct((bl, s, LANES), BF16),
           jax.ShapeDtypeStruct((bl, s, D_DN), BF16)]
        + [jax.ShapeDtypeStruct(sh, F32) for sh in DN_PAR_SHAPES]
        + [jax.ShapeDtypeStruct(q.shape, q.dtype) for q in exchange],
        scratch_shapes=[pltpu.VMEM((bl, 1, 2 * NRE), F32), pltpu.VMEM((bl, H, DH, DH), F32)] + (_chip_sems(k_ex) if k_ex else []),
        name=name, compiler_params=_cp("arbitrary"))(
            zs, carries, h_all, dy_s, *s5_par, *s5_const, qkv, zab, zg, states, tinvs, dy_d, *dn_par, *exchange)
    return outs[:7], outs[7:13], outs[13:]


def _sg_fn(n_chunk):
    def f(z, lng, lnb, w, bsp_t):
        u = jax.nn.gelu(z[:, :D_SG])
        v = jax.nn.gelu(z[:, D_SG:2 * D_SG])
        gate = z[:, 2 * D_SG:]
        xc = v - jnp.mean(v, axis=-1, keepdims=True)
        vn = xc * lax.rsqrt(jnp.mean(xc * xc, axis=-1, keepdims=True) + EPS) * lng + lnb
        r = lax.broadcasted_iota(jnp.int32, (SG_C, SG_C), 0)
        c = lax.broadcasted_iota(jnp.int32, (SG_C, SG_C), 1)
        causal = r >= c
        first_half = c < SG_C // 2
        rr = lax.broadcasted_iota(jnp.int32, (LANES, D_SG), 0)
        cc = lax.broadcasted_iota(jnp.int32, (LANES, D_SG), 1)
        expand = jnp.where((cc >= rr * 64) & (cc < rr * 64 + 64) & (rr < 4), 1.0, 0.0)
        bias = _sel_r(bsp_t, expand)
        wm = [jnp.where(causal, w[h], 0.0) for h in range(4)]
        rows = []
        for ci in range(n_chunk):
            vc = vn[ci * SG_C:(ci + 1) * SG_C]
            pairs = []
            for pr in range(2):
                vp = vc[:, pr * LANES:(pr + 1) * LANES]
                pairs.append(jnp.where(first_half, _mm(wm[2 * pr], vp), _mm(wm[2 * pr + 1], vp)))
            rows.append(jnp.concatenate(pairs, axis=1) + bias)
        sp = jnp.concatenate(rows, axis=0) if n_chunk > 1 else rows[0]
        return u * sp * _silu(gate)

    return f


def _sg_specs():
    full = lambda i: (0, 0)
    full3 = lambda i: (0, 0, 0)
    par = [pl.BlockSpec((1, D_SG), full), pl.BlockSpec((1, D_SG), full), pl.BlockSpec((4, SG_C, SG_C), full3),
           pl.BlockSpec((SG_C, LANES), full)]
    par_shapes = [(1, D_SG), (1, D_SG), (4, SG_C, SG_C), (SG_C, LANES)]
    return par, par_shapes


def _sg_fwd(zsg, params, tb, name):
    t = zsg.shape[0]
    f = _sg_fn(tb // SG_C)
    par, _ = _sg_specs()

    def body(z_ref, g_ref, b_ref, w_ref, bs_ref, y_ref):
        y_ref[...] = f(z_ref[...], g_ref[...], b_ref[...], w_ref[...], bs_ref[...])

    row = lambda i: (i, 0)
    return pl.pallas_call(
        body, grid=(t // tb,), in_specs=[pl.BlockSpec((tb, 3 * D_SG), row)] + par,
        out_specs=pl.BlockSpec((tb, D_SG), row), out_shape=jax.ShapeDtypeStruct((t, D_SG), F32),
        name=name, compiler_params=_cp("parallel"))(zsg, *params)


def _sg_bwd(zsg, dy, params, tb, name):
    t = zsg.shape[0]
    f = _sg_fn(tb // SG_C)
    par, par_shapes = _sg_specs()

    def body(z_ref, dy_ref, g_ref, b_ref, w_ref, bs_ref, dz_ref, dg_ref, db_ref, dw_ref, dbs_ref):
        accs = (dg_ref, db_ref, dw_ref, dbs_ref)

        @pl.when(pl.program_id(0) == 0)
        def _():
            for r in accs:
                r[...] = jnp.zeros_like(r)

        _, vjp = jax.vjp(f, z_ref[...], g_ref[...], b_ref[...], w_ref[...], bs_ref[...])
        cts = vjp(dy_ref[...])
        dz_ref[...] = cts[0].astype(BF16)
        for r, ct in zip(accs, cts[1:]):
            r[...] += ct

    row = lambda i: (i, 0)
    return pl.pallas_call(
        body, grid=(t // tb,), in_specs=[pl.BlockSpec((tb, 3 * D_SG), row), pl.BlockSpec((tb, D_SG), row)] + par,
        out_specs=[pl.BlockSpec((tb, 3 * D_SG), row)] + par,
        out_shape=[jax.ShapeDtypeStruct((t, 3 * D_SG), BF16)] + [jax.ShapeDtypeStruct(sh, F32) for sh in par_shapes],
        name=name, compiler_params=_cp("arbitrary"))(zsg, dy, *params)


def _out_fwd(x, ys, p, w_out, pg, w_gate, w_ple, tb, name):
    t = x.shape[0]

    def body(x_ref, y0, y1, y2, p_ref, wo_ref, pg_ref, wg_ref, wp_ref, o_ref, x1_ref, gate_ref):
        y = jnp.concatenate([y0[...], y1[...], y2[...]], axis=1).astype(BF16)
        x1 = x_ref[...] + jnp.dot(y, wo_ref[...], preferred_element_type=F32)
        hn = _rms(x1, pg_ref[...]).astype(BF16)
        gate = jax.nn.sigmoid(jnp.dot(hn, wg_ref[...], preferred_element_type=F32))
        pp = jnp.dot(p_ref[...].astype(BF16), wp_ref[...], preferred_element_type=F32)
        o_ref[...] = x1 + gate * pp
        x1_ref[...] = x1
        gate_ref[...] = gate

    row = lambda i: (i, 0)
    full = lambda i: (0, 0)
    return pl.pallas_call(
        body, grid=(t // tb,),
        in_specs=[pl.BlockSpec((tb, D), row), pl.BlockSpec((tb, D_SSM), row), pl.BlockSpec((tb, D_DN), row),
                  pl.BlockSpec((tb, D_SG), row), pl.BlockSpec((tb, D_PLE), row), pl.BlockSpec((D, D), full),
                  pl.BlockSpec((1, D), full), pl.BlockSpec((D, D), full), pl.BlockSpec((D_PLE, D), full)],
        out_specs=[pl.BlockSpec((tb, D), row)] * 3, out_shape=[jax.ShapeDtypeStruct((t, D), F32)] * 3,
        name=name, compiler_params=_cp("parallel"))(x, *ys, p, w_out, pg, w_gate, w_ple)


def _out_bwd(x1, gate, ys, p, dx2, w_out, pg, w_gate, w_ple, tb, name):
    t = x1.shape[0]

    def body(x1_ref, gate_ref, y0, y1, y2, p_ref, d_ref, wo_ref, pg_ref, wg_ref, wp_ref,
             dx_ref, dy0, dy1, dy2, dwo_ref, dpg_ref, dwg_ref, dwp_ref):
        accs = (dwo_ref, dpg_ref, dwg_ref, dwp_ref)

        @pl.when(pl.program_id(0) == 0)
        def _():
            for r in accs:
                r[...] = jnp.zeros_like(r)

        y = jnp.concatenate([y0[...], y1[...], y2[...]], axis=1).astype(BF16)
        hn, rms_vjp = jax.vjp(_rms, x1_ref[...], pg_ref[...])
        hb = hn.astype(BF16)
        gate = gate_ref[...]
        pb = p_ref[...].astype(BF16)
        pp = jnp.dot(pb, wp_ref[...], preferred_element_type=F32)
        d2 = d_ref[...]
        dpp = (d2 * gate).astype(BF16)
        dlog = (d2 * pp * gate * (1.0 - gate)).astype(BF16)
        dwp_ref[...] += _dg(pb, dpp, 0, 0)
        dwg_ref[...] += _dg(hb, dlog, 0, 0)
        dx1_n, dpg = rms_vjp(_dg(dlog, wg_ref[...], 1, 1))
        dpg_ref[...] += dpg
        dx1 = d2 + dx1_n
        dx_ref[...] = dx1
        db = dx1.astype(BF16)
        dwo_ref[...] += _dg(y, db, 0, 0)
        dy = _dg(db, wo_ref[...], 1, 1)
        dy0[...] = dy[:, :D_SSM]
        dy1[...] = dy[:, D_SSM:D_SSM + D_DN]
        dy2[...] = dy[:, D_SSM + D_DN:]

    row = lambda i: (i, 0)
    full = lambda i: (0, 0)
    acts = [pl.BlockSpec((tb, D), row), pl.BlockSpec((tb, D_SSM), row), pl.BlockSpec((tb, D_DN), row), pl.BlockSpec((tb, D_SG), row)]
    wts = [pl.BlockSpec((D, D), full), pl.BlockSpec((1, D), full), pl.BlockSpec((D, D), full), pl.BlockSpec((D_PLE, D), full)]
    return pl.pallas_call(
        body, grid=(t // tb,),
        in_specs=[pl.BlockSpec((tb, D), row)] + acts + [pl.BlockSpec((tb, D_PLE), row), pl.BlockSpec((tb, D), row)] + wts,
        out_specs=acts + wts,
        out_shape=[jax.ShapeDtypeStruct((t, n), F32) for n in (D, D_SSM, D_DN, D_SG)]
        + [jax.ShapeDtypeStruct(sh, F32) for sh in ((D, D), (1, D), (D, D), (D_PLE, D))],
        name=name, compiler_params=_cp("arbitrary"))(x1, gate, *ys, p, dx2, w_out, pg, w_gate, w_ple)


def _loss_head(x, fg, target, tb, name):
    t = x.shape[0]

    def body(x_ref, g_ref, t_ref, dx_ref, dg_ref, loss_ref):
        @pl.when(pl.program_id(0) == 0)
        def _():
            dg_ref[...] = jnp.zeros_like(dg_ref)
            loss_ref[...] = jnp.zeros_like(loss_ref)

        y, vjp = jax.vjp(_rms, x_ref[...], g_ref[...])
        err = y - t_ref[...]
        loss_ref[...] += jnp.zeros_like(loss_ref) + 0.5 * jnp.sum(err * err) / D
        dx, dg = vjp(err / D)
        dx_ref[...] = dx
        dg_ref[...] += dg

    row = lambda i: (i, 0)
    full = lambda i: (0, 0)
    return pl.pallas_call(
        body, grid=(t // tb,),
        in_specs=[pl.BlockSpec((tb, D), row), pl.BlockSpec((1, D), full), pl.BlockSpec((tb, D), row)],
        out_specs=[pl.BlockSpec((tb, D), row), pl.BlockSpec((1, D), full), pl.BlockSpec((1, LANES), full)],
        out_shape=[jax.ShapeDtypeStruct((t, D), F32), jax.ShapeDtypeStruct((1, D), F32), jax.ShapeDtypeStruct((1, LANES), F32)],
        name=name, compiler_params=_cp("arbitrary"))(x, fg, target)


def _hbm_specs(n):
    return [pl.BlockSpec(memory_space=pl.ANY)] * n


def _gather_protocol(place, ins, outs, send_sems, recv_sems, local_sems):
    n = len(ins)
    x, y, c, other_x, other_y, other_c = place
    me, sibling = (x, y, c), (x, y, other_c)
    chips = [(other_x, y), (x, other_y), (other_x, other_y)]

    def slot(a, px, py, pc):
        return outs[a].at[4 * px + 2 * py + pc]

    def copy(a, k, blk, to, src=None):
        return pltpu.make_async_remote_copy(
            src_ref=slot(a, *blk) if src is None else src, dst_ref=slot(a, *blk),
            send_sem=send_sems.at[7 * a + k], recv_sem=recv_sems.at[7 * a + k],
            device_id=to, device_id_type=pl.DeviceIdType.MESH)

    def own_copies():
        mines = [pltpu.make_async_copy(ins[a], slot(a, *me), local_sems.at[a]) for a in range(n)]
        first = []
        for a in range(n):
            first.append(copy(a, 0, me, sibling, src=ins[a]))
            first += [copy(a, 1 + j, me, (*chip, c), src=ins[a]) for j, chip in enumerate(chips)]
        return mines, first

    mines, first = own_copies()
    for cp in mines + first:
        cp.start()
    yield
    mines, first = own_copies()
    passed = []
    for j, chip in enumerate(chips):
        for a in range(n):
            copy(a, 1 + j, (*chip, c), me).wait_recv()
            onward = copy(a, 4 + j, (*chip, c), sibling)
            onward.start()
            passed.append(onward)
    for a in range(n):
        copy(a, 0, sibling, me).wait_recv()
    for j, chip in enumerate(chips):
        for a in range(n):
            copy(a, 4 + j, (*chip, other_c), me).wait_recv()
    for cp in first + passed:
        cp.wait_send()
    for cp in mines:
        cp.wait()


def _mesh_place():
    x, y, c = lax.axis_index("x"), lax.axis_index("y"), lax.axis_index("c")
    return x, y, c, 1 - x, 1 - y, 1 - c


def _gather_sems(n):
    return [pltpu.SemaphoreType.DMA((7 * n,)), pltpu.SemaphoreType.DMA((7 * n,)), pltpu.SemaphoreType.DMA((n,))]


def _all_gather(blocks, name):
    n = len(blocks)

    def body(*refs):
        for _ in _gather_protocol(_mesh_place(), refs[:n], refs[n:2 * n], *refs[2 * n:]):
            pass

    return pl.pallas_call(
        body, out_shape=[jax.ShapeDtypeStruct((N_DEV, *b.shape), b.dtype) for b in blocks],
        in_specs=_hbm_specs(n), out_specs=_hbm_specs(n), scratch_shapes=_gather_sems(n), name=name)(*blocks)


def _pair_exchange(gs, name):
    n = len(gs)

    def body(*refs):
        ins, recvs = refs[:n], refs[n:2 * n]
        send_sems, recv_sems = refs[2 * n:]
        x, y, c = lax.axis_index("x"), lax.axis_index("y"), lax.axis_index("c")
        remote = [pltpu.make_async_remote_copy(
            src_ref=ins[a], dst_ref=recvs[a], send_sem=send_sems.at[a], recv_sem=recv_sems.at[a],
            device_id=(x, y, 1 - c), device_id_type=pl.DeviceIdType.MESH) for a in range(n)]
        for cp in remote:
            cp.start()
        for cp in remote:
            cp.wait_send()
            cp.wait_recv()

    return pl.pallas_call(
        body, out_shape=[jax.ShapeDtypeStruct(g.shape, g.dtype) for g in gs], in_specs=_hbm_specs(n), out_specs=_hbm_specs(n),
        scratch_shapes=[pltpu.SemaphoreType.DMA((n,)), pltpu.SemaphoreType.DMA((n,))],
        name=name)(*gs)


def _chip_protocol(place, ins, outs, send_sems, recv_sems, local_sems):
    n = len(ins)
    x, y, c, other_x, other_y, _ = place

    def copies():
        my_chip = 2 * x + y
        local = [pltpu.make_async_copy(ins[a].at[my_chip], outs[a].at[my_chip], local_sems.at[a]) for a in range(n)]
        remote = []
        for j in range(1, 4):
            px = other_x if j & 2 else x
            py = other_y if j & 1 else y
            for a in range(n):
                remote.append(pltpu.make_async_remote_copy(
                    src_ref=ins[a].at[2 * px + py], dst_ref=outs[a].at[my_chip],
                    send_sem=send_sems.at[3 * a + j - 1], recv_sem=recv_sems.at[3 * a + j - 1],
                    device_id=(px, py, c), device_id_type=pl.DeviceIdType.MESH))
        return local, remote

    local, remote = copies()
    for cp in local + remote:
        cp.start()
    yield
    local, remote = copies()
    for cp in remote:
        cp.wait_send()
        cp.wait_recv()
    for cp in local:
        cp.wait()


def _chip_sems(n):
    return [pltpu.SemaphoreType.DMA((3 * n,)), pltpu.SemaphoreType.DMA((3 * n,)), pltpu.SemaphoreType.DMA((n,))]


def _chip_exchange(ps, gather, name):
    n, m = len(ps), len(gather)

    def body(*refs):
        ins, g_ins, outs, g_outs = refs[:n], refs[n:n + m], refs[n + m:2 * n + m], refs[2 * n + m:2 * (n + m)]
        place = _mesh_place()
        exchanging = _chip_protocol(place, ins, outs, *refs[2 * (n + m):2 * (n + m) + 3])
        gathering = _gather_protocol(place, g_ins, g_outs, *refs[2 * (n + m) + 3:])
        next(exchanging)
        for _ in gathering:
            pass
        for _ in exchanging:
            pass

    outs = pl.pallas_call(
        body, out_shape=[jax.ShapeDtypeStruct(q.shape, q.dtype) for q in ps]
        + [jax.ShapeDtypeStruct((N_DEV, *b.shape), b.dtype) for b in gather],
        in_specs=_hbm_specs(n + m), out_specs=_hbm_specs(n + m),
        scratch_shapes=_chip_sems(n) + _gather_sems(m), name=name)(*ps, *gather)
    return outs[:n], outs[n:]


def _row_block(rows, bytes_per_row):
    best = None
    for rb in range(16, rows + 1, 16):
        if rows % rb == 0 and rb * bytes_per_row <= ELEMENTWISE_STEP_BYTES:
            best = rb
    return rows if best is None else best


def _add_pair(own, recv, name):
    shape = own.shape
    last = shape[-1]
    rows = own.size // last
    rb = _row_block(rows, 3 * 4 * (-(-last // LANES) * LANES))

    def body(a_ref, b_ref, o_ref):
        o_ref[...] = (a_ref[...].astype(F32) + b_ref[...].astype(F32)).astype(o_ref.dtype)

    row = lambda i: (i, 0)
    out = pl.pallas_call(
        body, grid=(rows // rb,), in_specs=[pl.BlockSpec((rb, last), row)] * 2, out_specs=pl.BlockSpec((rb, last), row),
        out_shape=jax.ShapeDtypeStruct((rows, last), own.dtype), name=name,
        compiler_params=_cp("parallel"))(own.reshape(rows, last), recv.reshape(rows, last))
    return out.reshape(shape)


def _sum_adamw(gk, w, m, v, name):
    shape = w.shape
    n_part = gk.shape[0]
    last = shape[-1]
    rows = w.size // last
    rb = _row_block(rows, (n_part + 7) * 4 * (-(-last // LANES) * LANES))

    def body(g_ref, w_ref, m_ref, v_ref, go_ref, d_ref, mo_ref, vo_ref):
        g = g_ref[0].astype(F32)
        for k in range(1, n_part):
            g = g + g_ref[k].astype(F32)
        mn = ADAM_B1 * m_ref[...] + (1.0 - ADAM_B1) * g
        vn = ADAM_B2 * v_ref[...] + (1.0 - ADAM_B2) * jnp.square(g)
        m_hat = mn / (1.0 - ADAM_B1 ** ADAM_STEP)
        v_hat = vn / (1.0 - ADAM_B2 ** ADAM_STEP)
        go_ref[...] = g
        d_ref[...] = -ADAM_LR * (m_hat / (jnp.sqrt(v_hat) + ADAM_EPS) + ADAM_WD * w_ref[...])
        mo_ref[...] = mn
        vo_ref[...] = vn

    row = lambda i: (i, 0)
    outs = pl.pallas_call(
        body, grid=(rows // rb,),
        in_specs=[pl.BlockSpec((n_part, rb, last), lambda i: (0, i, 0))] + [pl.BlockSpec((rb, last), row)] * 3,
        out_specs=[pl.BlockSpec((rb, last), row)] * 4,
        out_shape=[jax.ShapeDtypeStruct((rows, last), F32)] * 4,
        name=name, compiler_params=_cp("parallel"))(gk.reshape(n_part, rows, last), *[a.reshape(rows, last) for a in (w, m, v)])
    return [o.reshape(shape) for o in outs]


def _seg_rows(shape):
    n = 1
    for d in shape:
        n *= d
    return -(-n // (8 * LANES)) * 8


def _pack(arrs):
    segs = []
    for a in arrs:
        r = _seg_rows(a.shape)
        segs.append(jnp.pad(a.reshape(-1).astype(F32), (0, r * LANES - a.size)).reshape(r, LANES))
    rows = sum(s.shape[0] for s in segs)
    total = -(-rows // PACK_ROWS) * PACK_ROWS
    if total > rows:
        segs.append(jnp.zeros((total - rows, LANES), F32))
    return jnp.concatenate(segs, axis=0)


def _unpack(pack, shapes):
    out, off = [], 0
    for sh in shapes:
        r = _seg_rows(sh)
        n = 1
        for d in sh:
            n *= d
        out.append(pack[off:off + r].reshape(-1)[:n].reshape(sh))
        off += r
    return out


def _to_dest_blocks(full, axis, dtype):
    sh = list(full.shape)
    sh[axis:axis + 1] = [N_DEV // 2, 2, sh[axis] // N_DEV]
    return jnp.moveaxis(full.reshape(sh), (axis, axis + 1), (1, 0)).astype(dtype)


def _from_gathered(g, axis):
    m = jnp.moveaxis(g, 0, axis)
    sh = list(m.shape)
    sh[axis:axis + 2] = [sh[axis] * sh[axis + 1]]
    return m.reshape(sh)


D_IN = 3336
W_IN_SHARD = D_IN // N_DEV
W_IN_MOVES = ((0, 2048, 0), (2048, 2056, 3328), (2056, D_IN, 2048))


def _w_in_windows(k):
    lo, hi = k * W_IN_SHARD, (k + 1) * W_IN_SHARD
    out = []
    for a, b, mine in W_IN_MOVES:
        a2, b2 = max(a, lo), min(b, hi)
        if b2 > a2:
            out.append((a2 - lo, b2 - a2, mine + a2 - a))
    return out


def _assemble_w_in(gathered, name):
    depth = gathered.shape[1]
    rb = 256

    def body(g_ref, o_ref):
        o_ref[0, :, D_IN:] = jnp.zeros((rb, ZW - D_IN), o_ref.dtype)
        for k in range(N_DEV):
            for off, width, mine in _w_in_windows(k):
                o_ref[0, :, mine:mine + width] = g_ref[k, 0, :, off:off + width]

    return pl.pallas_call(
        body, grid=(depth, D // rb),
        in_specs=[pl.BlockSpec((N_DEV, 1, rb, W_IN_SHARD), lambda l, i: (0, l, i, 0))],
        out_specs=pl.BlockSpec((1, rb, ZW), lambda l, i: (l, i, 0)),
        out_shape=jax.ShapeDtypeStruct((depth, D, ZW), gathered.dtype),
        name=name, compiler_params=_cp("parallel", "parallel"))(gathered)


def _split_dw_in(dws, name):
    depth = len(dws)
    rb = 128

    def body(*refs):
        o_ref = refs[-1]
        for l in range(depth):
            pieces = refs[5 * l:5 * l + 5]
            for k in range(N_DEV):
                for off, width, mine in _w_in_windows(k):
                    for p_ref, start, n in zip(pieces, Z_OFFSETS, Z_PIECES):
                        a, b = max(mine, start), min(mine + width, start + n)
                        if b > a:
                            o_ref[k % 2, k // 2, l, :, off + a - mine:off + b - mine] = (
                                p_ref[:, a - start:b - start].astype(o_ref.dtype))

    row = lambda i: (i, 0)
    flat = [piece for layer in dws for piece in layer]
    return pl.pallas_call(
        body, grid=(D // rb,),
        in_specs=[pl.BlockSpec((rb, n), row) for _ in range(depth) for n in Z_PIECES],
        out_specs=pl.BlockSpec((2, N_DEV // 2, depth, rb, W_IN_SHARD), lambda i: (0, 0, 0, i, 0)),
        out_shape=jax.ShapeDtypeStruct((2, N_DEV // 2, depth, D, W_IN_SHARD), WIRE['w_in']),
        name=name, compiler_params=_cp("parallel"))(*flat)


def _layer_weights(gathered, layer):
    full = {}
    for n, g in zip(SHARDED_ORDER, gathered):
        full[n] = _assemble_w_in(g, f"assemble_w_in_l{layer}") if n == 'w_in' else _from_gathered(g, SHARDED[n])
    full['ssm_w_glu'] = full['ssm_w_glu'].astype(F32)
    return full


def _pair_reduce(layer_grads, layer):
    dest = [_split_dw_in([layer_grads[n]], f"split_dw_in_l{layer}") if n == 'w_in'
            else _to_dest_blocks(layer_grads[n][None], SHARDED[n], WIRE[n]) for n in SHARDED_ORDER]
    c = lax.axis_index("c")
    own = [lax.dynamic_index_in_dim(d, c, 0, keepdims=False) for d in dest]
    for_sibling = [lax.dynamic_index_in_dim(d, 1 - c, 0, keepdims=False) for d in dest]
    from_sibling = _pair_exchange(for_sibling, f"grads_pair_exchange_l{layer}")
    return [_add_pair(a, b, f"grads_pair_sum_{n}_l{layer}") for n, a, b in zip(SHARDED_ORDER, own, from_sibling)]


def _local_step(x, p, wts, sharded, target, shards=None, reduce_early=False):
    sharded = list(sharded)
    bl, s, _ = x.shape
    t = bl * s
    depth = p.shape[0]
    tb, sg_tb = TB, SG_TB

    def by_example(a):
        return a.reshape(bl, s, a.shape[-1])

    def flat(a):
        return a.reshape(t, a.shape[-1])

    xs = [x.reshape(t, D)]
    saved = []
    for i in range(depth):
        li = f"l{i}"
        ng = wts['norm_g'][i].reshape(1, D)
        lw = sharded[i]
        w_in = lw['w_in'][0]
        s5_par_in = (wts['ssm_a_re'][i], wts['ssm_a_im'][i], wts['ssm_b_re'][i], wts['ssm_b_im'][i],
                     wts['ssm_c_re'][i], wts['ssm_c_im'][i], wts['ssm_d'][i], wts['ssm_log_step'][i])
        tabs, tab_vjp = jax.vjp(_s5_tables, *s5_par_in)
        s5_par = (*tabs, lw['ssm_w_glu'][0], wts['ssm_b_glu'][i].reshape(1, D_SSM))
        s5_const = _s5_powers(wts['ssm_a_re'][i], wts['ssm_a_im'][i], wts['ssm_log_step'][i])
        conv8 = jnp.pad(lw['dn_conv_w'][0], ((0, 4), (0, 0)))
        dn_par = (jnp.repeat(wts['dn_a_log'][i], DH).reshape(1, D_DN), jnp.repeat(wts['dn_dt_bias'][i], DH).reshape(1, D_DN),
                  wts['dn_norm_g'][i].reshape(1, DH))
        sg_par = (wts['sg_ln_g'][i].reshape(1, D_SG), wts['sg_ln_b'][i].reshape(1, D_SG), wts['sg_w'][i],
                  jnp.pad(jnp.transpose(wts['sg_b'][i]), ((0, 0), (0, LANES - 4))))
        out_par = (lw['w_out'][0].astype(BF16), wts['ple_norm_g'][i].reshape(1, D), lw['w_ple_gate'][0].astype(BF16),
                   lw['w_ple'][0].astype(BF16))
        pi = p[i].reshape(t, D_PLE)

        z_ssm, z_qkv, z_gdn, z_sg, z_ab, qkvn = _in_proj_fwd(xs[i], ng, w_in, conv8, s, tb, f"in_proj_fwd_{li}")
        wanted = shards[i + 1] if i + 1 < depth and sharded[i + 1] is None else ()
        (y_ssm, carries, h_all, y_dn, states, tinvs), gathered = _mix_fwd(
            by_example(z_ssm), s5_par, s5_const, by_example(qkvn), by_example(z_ab), by_example(z_gdn), dn_par, bl, s,
            f"mix_fwd_{li}", gather=wanted)
        if wanted:
            sharded[i + 1] = _layer_weights(gathered, i + 1)
        y_sg = _sg_fwd(z_sg, sg_par, sg_tb, f"sg_fwd_{li}")
        ys = (flat(y_ssm), flat(y_dn), y_sg)
        x_next, x1, gate = _out_fwd(xs[i], ys, pi, *out_par, tb, f"out_fwd_{li}")
        xs.append(x_next)
        saved.append(dict(ng=ng, w_in=w_in, tab_vjp=tab_vjp, s5_par=s5_par, s5_const=s5_const, conv8=conv8, dn_par=dn_par,
                          sg_par=sg_par, out_par=out_par, pi=pi, z=(z_ssm, z_qkv, z_gdn, z_sg, z_ab), carries=carries,
                          h_all=h_all, qkvn=qkvn, x1=x1, gate=gate,
                          states=states, tinvs=tinvs, ys=ys))

    dx, dfg, loss_vec = _loss_head(xs[depth], wts['final_norm_g'].reshape(1, D), target.reshape(t, D), tb, "loss_head")
    grads = {n: [None] * depth for n in WEIGHTS if n != 'final_norm_g'}
    grads['final_norm_g'] = dfg.reshape(D)
    pair_sums, by_chip = {}, {}
    for i in reversed(range(depth)):
        li = f"l{i}"
        sv = saved[i]
        z_ssm, z_qkv, z_gdn, z_sg, z_ab = sv['z']
        dx_res, dy_ssm, dy_dn, dy_sg, dwo, dpg, dwg, dwp = _out_bwd(sv['x1'], sv['gate'], sv['ys'], sv['pi'], dx, *sv['out_par'], tb,
                                                                    f"out_bwd_{li}")
        dz_sg, dlng, dlnb, dsgw, dbsp = _sg_bwd(z_sg, dy_sg, sv['sg_par'], sg_tb, f"sg_bwd_{li}")
        (dz_ssm, dbb, dcb, dlam, ddv, dwglu, dbglu), (dqkvn, dz_ab, dz_gdn, dal, ddt, dng), exchanged = _mix_bwd(
            by_example(z_ssm), sv['carries'], sv['h_all'], by_example(dy_ssm), sv['s5_par'], sv['s5_const'],
            by_example(sv['qkvn']), by_example(z_ab), by_example(z_gdn), sv['states'], sv['tinvs'], by_example(dy_dn),
            sv['dn_par'], bl, s, f"mix_bwd_{li}", exchange=pair_sums.get(i + 1, ()))
        if exchanged:
            by_chip[i + 1] = exchanged
        dx, dnorm, dz_qkv, dconv = _in_proj_bwd_dx(xs[i], sv['ng'], sv['w_in'], sv['conv8'], flat(dz_ssm), flat(dz_gdn), dz_sg,
                                                   flat(dz_ab), z_qkv, flat(dqkvn), dx_res, s, tb, f"in_proj_bwd_dx_{li}")
        dzs = (flat(dz_ssm), dz_qkv, flat(dz_gdn), dz_sg, flat(dz_ab))
        dws = _in_proj_bwd_dw(xs[i], sv['ng'], dzs, min(TB_DW, t), f"in_proj_bwd_dw_{li}")
        ds5 = sv['tab_vjp']((dbb, dcb, dlam, ddv))
        for n, gval in zip(('ssm_a_re', 'ssm_a_im', 'ssm_b_re', 'ssm_b_im', 'ssm_c_re', 'ssm_c_im', 'ssm_d', 'ssm_log_step'), ds5):
            grads[n][i] = gval
        grads['norm_g'][i] = dnorm.reshape(D)
        grads['w_in'][i] = dws
        grads['ssm_w_glu'][i] = dwglu
        grads['ssm_b_glu'][i] = dbglu.reshape(D_SSM)
        grads['dn_conv_w'][i] = dconv[:4]
        grads['dn_a_log'][i] = dal.reshape(H, DH).sum(axis=1)
        grads['dn_dt_bias'][i] = ddt.reshape(H, DH).sum(axis=1)
        grads['dn_norm_g'][i] = dng.reshape(DH)
        grads['sg_ln_g'][i] = dlng.reshape(D_SG)
        grads['sg_ln_b'][i] = dlnb.reshape(D_SG)
        grads['sg_w'][i] = dsgw
        grads['sg_b'][i] = jnp.transpose(dbsp[:, :4])
        grads['w_out'][i] = dwo
        grads['ple_norm_g'][i] = dpg.reshape(D)
        grads['w_ple_gate'][i] = dwg
        grads['w_ple'][i] = dwp
        if reduce_early:
            pair_sums[i] = _pair_reduce({n: grads[n][i] for n in SHARDED_ORDER}, i)
    grads = {n: (g if n in ('final_norm_g', 'w_in') else jnp.stack(g)) for n, g in grads.items()}
    return loss_vec[0, 0], dx.reshape(bl, s, D), grads, pair_sums.get(0), by_chip


def kernel(x, p, norm_g, w_in, ssm_a_re, ssm_a_im, ssm_b_re, ssm_b_im, ssm_c_re, ssm_c_im, ssm_d, ssm_log_step, ssm_w_glu, ssm_b_glu, dn_conv_w, dn_a_log, dn_dt_bias, dn_norm_g, sg_ln_g, sg_ln_b, sg_w, sg_b, w_out, ple_norm_g, w_ple_gate, w_ple, final_norm_g, loss_target, m_norm_g, m_w_in, m_ssm_a_re, m_ssm_a_im, m_ssm_b_re, m_ssm_b_im, m_ssm_c_re, m_ssm_c_im, m_ssm_d, m_ssm_log_step, m_ssm_w_glu, m_ssm_b_glu, m_dn_conv_w, m_dn_a_log, m_dn_dt_bias, m_dn_norm_g, m_sg_ln_g, m_sg_ln_b, m_sg_w, m_sg_b, m_w_out, m_ple_norm_g, m_w_ple_gate, m_w_ple, m_final_norm_g, v_norm_g, v_w_in, v_ssm_a_re, v_ssm_a_im, v_ssm_b_re, v_ssm_b_im, v_ssm_c_re, v_ssm_c_im, v_ssm_d, v_ssm_log_step, v_ssm_w_glu, v_ssm_b_glu, v_dn_conv_w, v_dn_a_log, v_dn_dt_bias, v_dn_norm_g, v_sg_ln_g, v_sg_ln_b, v_sg_w, v_sg_b, v_w_out, v_ple_norm_g, v_w_ple_gate, v_w_ple, v_final_norm_g):
    w_loc = dict(zip(WEIGHTS, (norm_g, w_in, ssm_a_re, ssm_a_im, ssm_b_re, ssm_b_im, ssm_c_re, ssm_c_im, ssm_d, ssm_log_step,
                               ssm_w_glu, ssm_b_glu, dn_conv_w, dn_a_log, dn_dt_bias, dn_norm_g, sg_ln_g, sg_ln_b, sg_w, sg_b,
                               w_out, ple_norm_g, w_ple_gate, w_ple, final_norm_g)))
    m_loc = dict(zip(WEIGHTS, (m_norm_g, m_w_in, m_ssm_a_re, m_ssm_a_im, m_ssm_b_re, m_ssm_b_im, m_ssm_c_re, m_ssm_c_im, m_ssm_d,
                               m_ssm_log_step, m_ssm_w_glu, m_ssm_b_glu, m_dn_conv_w, m_dn_a_log, m_dn_dt_bias, m_dn_norm_g,
                               m_sg_ln_g, m_sg_ln_b, m_sg_w, m_sg_b, m_w_out, m_ple_norm_g, m_w_ple_gate, m_w_ple, m_final_norm_g)))
    v_loc = dict(zip(WEIGHTS, (v_norm_g, v_w_in, v_ssm_a_re, v_ssm_a_im, v_ssm_b_re, v_ssm_b_im, v_ssm_c_re, v_ssm_c_im, v_ssm_d,
                               v_ssm_log_step, v_ssm_w_glu, v_ssm_b_glu, v_dn_conv_w, v_dn_a_log, v_dn_dt_bias, v_dn_norm_g,
                               v_sg_ln_g, v_sg_ln_b, v_sg_w, v_sg_b, v_w_out, v_ple_norm_g, v_w_ple_gate, v_w_ple, v_final_norm_g)))

    depth = p.shape[0]
    shards = [[w_loc[n][l:l + 1].astype(WIRE[n]) for n in SHARDED_ORDER] for l in range(depth)]
    first_layer = _layer_weights(_all_gather(shards[0], "gather_weights_l0"), 0)

    loss_part, grad_x, grads, first_pair_sums, by_chip = _local_step(
        x, p, w_loc, [first_layer] + [None] * (depth - 1), loss_target, shards, reduce_early=True)

    rep_pack = _pack([grads[n] for n in REPLICATED_ORDER] + [loss_part.reshape(1)])
    by_chip[0], (rep_recv,) = _chip_exchange(first_pair_sums, [rep_pack], "grads_chip_exchange_l0")
    by_chip = [jnp.concatenate([by_chip[l][a] for l in range(depth)], axis=1) for a in range(len(SHARDED_ORDER))]

    outs = {k: {} for k in 'gdmv'}
    for n, gk in zip(SHARDED_ORDER, by_chip):
        for k, o in zip('gdmv', _sum_adamw(gk, w_loc[n], m_loc[n], v_loc[n], f"adamw_{n}")):
            outs[k][n] = o
    one = jnp.zeros((1,), F32)
    rep_out = _sum_adamw(rep_recv, _pack([w_loc[n] for n in REPLICATED_ORDER] + [one]),
                         _pack([m_loc[n] for n in REPLICATED_ORDER] + [one]),
                         _pack([v_loc[n] for n in REPLICATED_ORDER] + [one]), "adamw_replicated")
    rep_shapes = [w_loc[n].shape for n in REPLICATED_ORDER] + [(1,)]
    for k, rep_p in zip('gdmv', rep_out):
        outs[k].update(zip(REPLICATED_ORDER + ['loss'], _unpack(rep_p, rep_shapes)))
    loss = outs['g']['loss'].reshape(())
    return (loss, grad_x, *[outs['g'][n] for n in WEIGHTS], *[outs['d'][n] for n in WEIGHTS],
            *[outs['m'][n] for n in WEIGHTS], *[outs['v'][n] for n in WEIGHTS])
```

```python
import functools

import jax
import jax.numpy as jnp
from jax import lax
from jax.experimental import pallas as pl
from jax.experimental.pallas import tpu as pltpu

F32 = jnp.float32
BF16 = jnp.bfloat16
EPS = 1e-6

D = 1024
D_PLE = 256
D_SSM = 256
D_DN = 512
D_SG = 256
G = 16
CG = 16
NS = 64
NRE = G * NS
H = 4
DH = 128
DN_C = 128
SG_C = 128
ZW = 3456
Z_PIECES = (512, 1536, 512, 768, 128)
N_DEV = 8
LANES = 128
PACK_ROWS = 256
VMEM_LIMIT = 56 * 1024 * 1024
ELEMENTWISE_STEP_BYTES = 4 * 1024 * 1024
TB = 256
SG_TB = 512
TB_DW = 512

ADAM_LR = 0.001
ADAM_B1 = 0.9
ADAM_B2 = 0.999
ADAM_EPS = 1e-08
ADAM_WD = 0.01
ADAM_STEP = 10

MIX_HEAD_START = 3
S5_L = 128
S5_GROUP = 8
S5_SHIFTS = (1, 2, 4)

WEIGHTS = ['norm_g', 'w_in', 'ssm_a_re', 'ssm_a_im', 'ssm_b_re', 'ssm_b_im', 'ssm_c_re', 'ssm_c_im', 'ssm_d',
           'ssm_log_step', 'ssm_w_glu', 'ssm_b_glu', 'dn_conv_w', 'dn_a_log', 'dn_dt_bias', 'dn_norm_g', 'sg_ln_g',
           'sg_ln_b', 'sg_w', 'sg_b', 'w_out', 'ple_norm_g', 'w_ple_gate', 'w_ple', 'final_norm_g']
SHARDED = {'w_in': 2, 'ssm_w_glu': 1, 'dn_conv_w': 2, 'w_out': 1, 'w_ple_gate': 1, 'w_ple': 2}
SHARDED_ORDER = ['w_in', 'ssm_w_glu', 'dn_conv_w', 'w_out', 'w_ple_gate', 'w_ple']
WIRE = {'w_in': BF16, 'ssm_w_glu': BF16, 'dn_conv_w': F32, 'w_out': BF16, 'w_ple_gate': BF16, 'w_ple': BF16}
REPLICATED_ORDER = [n for n in WEIGHTS if n not in SHARDED]


def _cp(*sem):
    return pltpu.CompilerParams(dimension_semantics=sem, vmem_limit_bytes=VMEM_LIMIT)


def _dg(a, b, ca, cb, precision=None):
    return lax.dot_general(a, b, (((ca,), (cb,)), ((), ())), precision=precision, preferred_element_type=F32)


@jax.custom_vjp
def _mm(a, b):
    return _dg(a.astype(BF16), b.astype(BF16), 1, 0)


def _mm_fwd(a, b):
    return _mm(a, b), (a, b)


def _mm_bwd(res, g):
    a, b = res
    gb = g.astype(BF16)
    return _dg(gb, b.astype(BF16), 1, 1), _dg(a.astype(BF16), gb, 0, 0)


_mm.defvjp(_mm_fwd, _mm_bwd)


@jax.custom_vjp
def _mm_nt(a, b):
    return _dg(a.astype(BF16), b.astype(BF16), 1, 1)


def _mm_nt_fwd(a, b):
    return _mm_nt(a, b), (a, b)


def _mm_nt_bwd(res, g):
    a, b = res
    gb = g.astype(BF16)
    return _dg(gb, b.astype(BF16), 1, 0), _dg(gb, a.astype(BF16), 0, 0)


_mm_nt.defvjp(_mm_nt_fwd, _mm_nt_bwd)


@jax.custom_vjp
def _mm_tn(a, b):
    return _dg(a.astype(BF16), b.astype(BF16), 0, 0)


def _mm_tn_fwd(a, b):
    return _mm_tn(a, b), (a, b)


def _mm_tn_bwd(res, g):
    a, b = res
    gb = g.astype(BF16)
    return _dg(b.astype(BF16), gb, 1, 1), _dg(a.astype(BF16), gb, 1, 0)


_mm_tn.defvjp(_mm_tn_fwd, _mm_tn_bwd)


def _split(x, n):
    pieces = []
    for _ in range(n - 1):
        hi = x.astype(BF16)
        pieces.append(hi)
        x = x - hi.astype(F32)
    pieces.append(x.astype(BF16))
    return pieces


def _dg3(a, b, ca, cb):
    a_hi, a_lo = _split(a, 2)
    b_hi, b_lo = _split(b, 2)
    return _dg(a_hi, b_hi, ca, cb) + (_dg(a_hi, b_lo, ca, cb) + _dg(a_lo, b_hi, ca, cb))


@jax.custom_vjp
def _dot3(a, b):
    return _dg3(a, b, 1, 0)


def _dot3_fwd(a, b):
    return _dot3(a, b), (a, b)


def _dot3_bwd(res, g):
    a, b = res
    return _dg3(g, b, 1, 1), _dg3(a, g, 0, 0)


_dot3.defvjp(_dot3_fwd, _dot3_bwd)


def _dg_sel(x, e, cx, ce, x_first):
    eb = e.astype(BF16)
    out = None
    for piece in reversed(_split(x, 3)):
        term = _dg(piece, eb, cx, ce) if x_first else _dg(eb, piece, ce, cx)
        out = term if out is None else out + term
    return out


@jax.custom_vjp
def _sel_r(x, e):
    return _dg_sel(x, e, 1, 0, True)


def _sel_r_fwd(x, e):
    return _sel_r(x, e), e


def _sel_r_bwd(e, g):
    return _dg_sel(g, e, 1, 1, True), jnp.zeros_like(e)


_sel_r.defvjp(_sel_r_fwd, _sel_r_bwd)


@jax.custom_vjp
def _sel_l(e, x):
    return _dg_sel(x, e, 0, 1, False)


def _sel_l_fwd(e, x):
    return _sel_l(e, x), e


def _sel_l_bwd(e, g):
    return jnp.zeros_like(e), _dg_sel(g, e, 0, 0, False)


_sel_l.defvjp(_sel_l_fwd, _sel_l_bwd)


def _rms(x, g):
    return x * lax.rsqrt(jnp.mean(x * x, axis=-1, keepdims=True) + EPS) * g


def _silu(x):
    return x * jax.nn.sigmoid(x)


Z_OFFSETS = (0, 512, 2048, 2560, 3328)


def _dn_post(c):
    s = _silu(c)
    parts = []
    for j in range(12):
        xj = s[:, j * DH:(j + 1) * DH]
        if j < 8:
            xj = xj * lax.rsqrt(jnp.sum(xj * xj, axis=-1, keepdims=True) + EPS)
        if j < 4:
            xj = xj * (DH ** -0.5)
        parts.append(xj)
    return jnp.concatenate(parts, axis=1)


def _dn_conv(ext, cw_ref, rows):
    c = None
    for k in range(4):
        sh = ext if k == 3 else pltpu.roll(ext, 3 - k, 0)
        term = cw_ref[k:k + 1, :] * sh[ext.shape[0] - rows:, :]
        c = term if c is None else c + term
    return c


def _dn_prep_vjp(prev, cur, nxt, d_cur, d_nxt, cw_ref, tb):
    ext = jnp.concatenate([prev, cur, nxt], axis=0)
    shifted = [ext if k == 3 else pltpu.roll(ext, 3 - k, 0) for k in range(4)]
    c2 = None
    for k in range(4):
        term = cw_ref[k:k + 1, :] * shifted[k][8:, :]
        c2 = term if c2 is None else c2 + term
    _, vjp = jax.vjp(_dn_post, c2)
    (dc2,) = vjp(jnp.concatenate([d_cur, d_nxt], axis=0))
    dz, dcw = None, []
    for k in range(4):
        up = dc2 if k == 3 else pltpu.roll(dc2, tb + 8 - (3 - k), 0)
        term = cw_ref[k:k + 1, :] * up[:tb, :]
        dz = term if dz is None else dz + term
        dcw.append(jnp.sum(dc2[:tb, :] * shifted[k][8:8 + tb, :], axis=0, keepdims=True))
    return dz, dcw


def _in_proj_fwd(x, g, w, conv_w8, s, tb, name):
    t = x.shape[0]
    n_s = s // tb
    w3 = 3 * D_DN
    q0, q1 = Z_OFFSETS[1], Z_OFFSETS[2]

    def body(x_ref, g_ref, w_ref, cw_ref, zs_ref, zq_ref, zg_ref, zsg_ref, zab_ref, qkvn_ref, halo):
        h = _rms(x_ref[...], g_ref[...]).astype(BF16)
        zq = jnp.dot(h, w_ref[:, q0:q1], preferred_element_type=F32)
        zq_ref[...] = zq
        prev = jnp.where(pl.program_id(0) % n_s == 0, 0.0, halo[...])
        qkvn_ref[...] = _dn_post(_dn_conv(jnp.concatenate([prev, zq], axis=0), cw_ref, tb))
        halo[...] = zq[tb - 8:, :]
        zs_ref[...] = jnp.dot(h, w_ref[:, :q0], preferred_element_type=F32)
        rest = jnp.dot(h, w_ref[:, q1:], preferred_element_type=F32)
        zg_ref[...] = rest[:, :Z_PIECES[2]]
        zsg_ref[...] = rest[:, Z_PIECES[2]:Z_PIECES[2] + Z_PIECES[3]]
        zab_ref[...] = rest[:, Z_PIECES[2] + Z_PIECES[3]:]

    row = lambda i: (i, 0)
    full = lambda i: (0, 0)
    widths = Z_PIECES + (w3,)
    return pl.pallas_call(
        body, grid=(t // tb,),
        in_specs=[pl.BlockSpec((tb, D), row), pl.BlockSpec((1, D), full), pl.BlockSpec((D, ZW), full), pl.BlockSpec((8, w3), full)],
        out_specs=[pl.BlockSpec((tb, n), row) for n in widths],
        out_shape=[jax.ShapeDtypeStruct((t, n), F32) for n in widths],
        scratch_shapes=[pltpu.VMEM((8, w3), F32)],
        name=name, compiler_params=_cp("arbitrary"))(x, g, w, conv_w8)


def _in_proj_bwd_dx(x, g, w, conv_w8, dz_ssm, dz_gdn, dz_sg, dz_ab, zq, dqkvn, dx_res, s, tb, name):
    t = x.shape[0]
    n_s = s // tb
    hb = tb // 8
    w3 = 3 * D_DN
    q0, q1 = Z_OFFSETS[1], Z_OFFSETS[2]

    def body(x_ref, g_ref, w_ref, cw_ref, ds_ref, dgd_ref, dsg_ref, dab_ref, cur_ref, prev_ref, next_ref, dq_ref, dqn_ref,
             dxr_ref, dx_ref, dg_ref, dzq_ref, dcw_ref):
        i = pl.program_id(0)

        @pl.when(i == 0)
        def _():
            dg_ref[...] = jnp.zeros_like(dg_ref)
            dcw_ref[...] = jnp.zeros_like(dcw_ref)

        rest = jnp.concatenate([dgd_ref[...], dsg_ref[...], dab_ref[...]], axis=1)
        dh = _dg(ds_ref[...], w_ref[:, :q0], 1, 1) + _dg(rest, w_ref[:, q1:], 1, 1)
        first, last = i % n_s == 0, i % n_s == n_s - 1
        dzq, dcw = _dn_prep_vjp(jnp.where(first, 0.0, prev_ref[...]), cur_ref[...], jnp.where(last, 0.0, next_ref[...]),
                                dq_ref[...], jnp.where(last, 0.0, dqn_ref[...]), cw_ref, tb)
        for k in range(4):
            dcw_ref[k:k + 1, :] += dcw[k]
        dzq = dzq.astype(BF16)
        dzq_ref[...] = dzq
        dh = dh + _dg(dzq, w_ref[:, q0:q1], 1, 1)
        _, vjp = jax.vjp(_rms, x_ref[...], g_ref[...])
        dx, dg = vjp(dh)
        dx_ref[...] = dx + dxr_ref[...]
        dg_ref[...] += dg

    n_blk8 = t // 8
    row = lambda i: (i, 0)
    prv = lambda i: (jnp.maximum(i * hb - 1, 0), 0)
    nxt = lambda i: (jnp.minimum((i + 1) * hb, n_blk8 - 1), 0)
    full = lambda i: (0, 0)
    return pl.pallas_call(
        body, grid=(t // tb,),
        in_specs=[pl.BlockSpec((tb, D), row), pl.BlockSpec((1, D), full), pl.BlockSpec((D, ZW), full), pl.BlockSpec((8, w3), full)]
        + [pl.BlockSpec((tb, n), row) for n in (Z_PIECES[0], Z_PIECES[2], Z_PIECES[3], Z_PIECES[4])]
        + [pl.BlockSpec((tb, w3), row), pl.BlockSpec((8, w3), prv), pl.BlockSpec((8, w3), nxt),
           pl.BlockSpec((tb, w3), row), pl.BlockSpec((8, w3), nxt), pl.BlockSpec((tb, D), row)],
        out_specs=[pl.BlockSpec((tb, D), row), pl.BlockSpec((1, D), full), pl.BlockSpec((tb, w3), row), pl.BlockSpec((8, w3), full)],
        out_shape=[jax.ShapeDtypeStruct((t, D), F32), jax.ShapeDtypeStruct((1, D), F32), jax.ShapeDtypeStruct((t, w3), BF16),
                   jax.ShapeDtypeStruct((8, w3), F32)],
        name=name, compiler_params=_cp("arbitrary"))(x, g, w, conv_w8, dz_ssm, dz_gdn, dz_sg, dz_ab, zq, zq, zq, dqkvn, dqkvn, dx_res)


def _in_proj_bwd_dw(x, g, dzs, tb, name):
    t = x.shape[0]

    def body(x_ref, g_ref, d0, d1, d2, d3, d4, *dw_refs):
        @pl.when(pl.program_id(0) == 0)
        def _():
            for r in dw_refs:
                r[...] = jnp.zeros_like(r)

        h = _rms(x_ref[...], g_ref[...]).astype(BF16)
        for d_ref, dw_ref in zip((d0, d1, d2, d3, d4), dw_refs):
            dw_ref[...] += _dg(h, d_ref[...].astype(BF16), 0, 0)

    row = lambda i: (i, 0)
    full = lambda i: (0, 0)
    return pl.pallas_call(
        body, grid=(t // tb,),
        in_specs=[pl.BlockSpec((tb, D), row), pl.BlockSpec((1, D), full)] + [pl.BlockSpec((tb, n), row) for n in Z_PIECES],
        out_specs=[pl.BlockSpec((D, n), full) for n in Z_PIECES],
        out_shape=[jax.ShapeDtypeStruct((D, n), F32) for n in Z_PIECES],
        name=name, compiler_params=_cp("arbitrary"))(x, g, *dzs)


def _lam_pow(a_re, a_im, log_step, k):
    step = jnp.exp(log_step)[:, None]
    mag = jnp.exp(k * a_re * step)
    ang = k * a_im * step
    return mag * jnp.cos(ang), mag * jnp.sin(ang)


def _s5_powers(a_re, a_im, log_step):
    def table(ks):
        re, im = _lam_pow(a_re, a_im, log_step, jnp.asarray(ks, F32)[:, None, None])
        return jnp.concatenate([re.reshape(len(ks), NRE), im.reshape(len(ks), NRE)], axis=-1)

    ld = table(S5_SHIFTS).reshape(len(S5_SHIFTS), 1, 2 * NRE)
    return ld, table(range(1, S5_GROUP + 1)), table(range(S5_GROUP, 0, -1))


def _s5_tables(a_re, a_im, b_re, b_im, c_re, c_im, d_skip, log_step):
    lam_re, lam_im = _lam_pow(a_re, a_im, log_step, 1.0)
    den = a_re * a_re + a_im * a_im
    nr, ni = lam_re - 1.0, lam_im
    f_re = (nr * a_re + ni * a_im) / den
    f_im = (ni * a_re - nr * a_im) / den
    bbar_re = f_re[..., None] * b_re - f_im[..., None] * b_im
    bbar_im = f_re[..., None] * b_im + f_im[..., None] * b_re
    eye = jnp.eye(G, dtype=F32)

    def blk_b(bb):
        return (jnp.transpose(bb, (0, 2, 1))[:, :, None, :] * eye[:, None, :, None]).reshape(D_SSM, NRE)

    def blk_c(cc):
        return (jnp.transpose(cc, (0, 2, 1))[:, :, None, :] * eye[:, None, :, None]).reshape(NRE, D_SSM)

    b_blk = jnp.concatenate([blk_b(bbar_re), blk_b(bbar_im)], axis=1)
    c_blk = jnp.concatenate([blk_c(c_re), -blk_c(c_im)], axis=0)
    lam = jnp.concatenate([lam_re.reshape(1, NRE), lam_im.reshape(1, NRE)], axis=-1)
    return b_blk, c_blk, lam, d_skip.reshape(1, D_SSM)


def _group_shift(x, d, up=False):
    r = lax.broadcasted_iota(jnp.int32, x.shape, 0) & (S5_GROUP - 1)
    if up:
        return jnp.where(r < S5_GROUP - d, pltpu.roll(x, x.shape[0] - d, 0), 0.0)
    return jnp.where(r >= d, pltpu.roll(x, d, 0), 0.0)


def _s5_scan_steps(hr, hi, cr, ci, lds, lp):
    for ld, d in zip(lds, S5_SHIFTS):
        lr, li = ld[:, :NRE], ld[:, NRE:]
        sr, si = _group_shift(hr, d), _group_shift(hi, d)
        hr, hi = hr + lr * sr - li * si, hi + lr * si + li * sr
        yield
    pr, pi = lp[:, :NRE], lp[:, NRE:]
    rows_r, rows_i = [], []
    for r in range(hr.shape[0] // S5_GROUP):
        br, bi = hr[r * S5_GROUP:(r + 1) * S5_GROUP], hi[r * S5_GROUP:(r + 1) * S5_GROUP]
        br, bi = br + pr * cr - pi * ci, bi + pr * ci + pi * cr
        cr, ci = br[S5_GROUP - 1:S5_GROUP], bi[S5_GROUP - 1:S5_GROUP]
        rows_r.append(br)
        rows_i.append(bi)
        if r % 2:
            yield
    return jnp.concatenate(rows_r, axis=0), jnp.concatenate(rows_i, axis=0)


@jax.custom_vjp
def _known_scan(xr, xi, cr, ci, lam, lds, lp_rev, hr, hi):
    return hr, hi


def _known_scan_fwd(xr, xi, cr, ci, lam, lds, lp_rev, hr, hi):
    return (hr, hi), (cr, ci, lam, lds, lp_rev, hr, hi)


def _known_scan_bwd(res, cts):
    cr, ci, lam, lds, lp_rev, hr, hi = res
    ar, ai = cts
    for ld, d in zip(lds, S5_SHIFTS):
        lr, li = ld[:, :NRE], ld[:, NRE:]
        sr, si = _group_shift(ar, d, up=True), _group_shift(ai, d, up=True)
        ar, ai = ar + lr * sr + li * si, ai + lr * si - li * sr
    qr, qi = lp_rev[:, :NRE], lp_rev[:, NRE:]
    nr, ni = jnp.zeros_like(cr), jnp.zeros_like(ci)
    rows_r, rows_i = [], []
    for r in reversed(range(hr.shape[0] // S5_GROUP)):
        br, bi = ar[r * S5_GROUP:(r + 1) * S5_GROUP], ai[r * S5_GROUP:(r + 1) * S5_GROUP]
        br, bi = br + qr * nr + qi * ni, bi + qr * ni - qi * nr
        nr, ni = br[0:1], bi[0:1]
        rows_r.insert(0, br)
        rows_i.insert(0, bi)
    ar, ai = jnp.concatenate(rows_r, axis=0), jnp.concatenate(rows_i, axis=0)
    lr, li = lam[:, :NRE], lam[:, NRE:]
    dcr, dci = lr * nr + li * ni, lr * ni - li * nr
    first = lax.broadcasted_iota(jnp.int32, hr.shape, 0) == 0
    pr = jnp.where(first, cr, pltpu.roll(hr, 1, 0))
    pi = jnp.where(first, ci, pltpu.roll(hi, 1, 0))
    dlam = jnp.concatenate([jnp.sum(ar * pr + ai * pi, axis=0, keepdims=True),
                            jnp.sum(ai * pr - ar * pi, axis=0, keepdims=True)], axis=1)
    return (ar, ai, dcr, dci, dlam, [jnp.zeros_like(ld) for ld in lds], jnp.zeros_like(lp_rev),
            jnp.zeros_like(hr), jnp.zeros_like(hi))


_known_scan.defvjp(_known_scan_fwd, _known_scan_bwd)


def _interleave(short, long, head_start=0):
    gens = list(short) + list(long)
    results = [None] * len(gens)

    def advance(live):
        still = []
        for idx, gen in live:
            try:
                next(gen)
                still.append((idx, gen))
            except StopIteration as done:
                results[idx] = done.value
        return still

    live_short = advance(list(enumerate(gens))[:len(short)])
    live_long = list(enumerate(gens))[len(short):]
    for _ in range(head_start):
        live_long = advance(live_long)
    live = live_short + live_long
    while live:
        live = advance(live)
    return results[:len(short)], results[len(short):]


def _s5_chunk_gen(u, gate, cr, ci, b_blk, c_blk, lam, dv, wglu, bglu, lds, lp, lp_rev, known_h=None):
    bu = _mm(u, b_blk)
    xr, xi = bu[:, :NRE], bu[:, NRE:]
    yield
    if known_h is None:
        hr, hi = yield from _s5_scan_steps(xr, xi, cr, ci, lds, lp)
    else:
        hr, hi = _known_scan(xr, xi, cr, ci, lam, lds, lp_rev, *known_h)
    y = _mm(jnp.concatenate([hr, hi], axis=1), c_blk) + dv * u
    yield
    y = jax.nn.gelu(y)
    y = y * jax.nn.sigmoid(_mm(y, wglu) + bglu)
    return y * _silu(gate), hr, hi


S5_PAR_SHAPES = [(D_SSM, 2 * NRE), (2 * NRE, D_SSM), (1, 2 * NRE), (1, D_SSM), (D_SSM, D_SSM), (1, D_SSM)]
S5_CONST_SHAPES = [(len(S5_SHIFTS), 1, 2 * NRE), (S5_GROUP, 2 * NRE), (S5_GROUP, 2 * NRE)]
DN_PAR_SHAPES = [(1, D_DN), (1, D_DN), (1, DH)]


def _mix_specs(bl, n_c, rev):
    def chunk(i):
        return n_c - 1 - i if rev else i

    def tok(n):
        return pl.BlockSpec((bl, DN_C, n), lambda i: (0, chunk(i), 0))

    def per_chunk(shape):
        return pl.BlockSpec((bl, 1, *shape), lambda i: (0, chunk(i)) + (0,) * len(shape))

    def whole(shape):
        return pl.BlockSpec(shape, lambda i: (0,) * len(shape))

    return tok, per_chunk, whole


def _unit_lower_inverse_steps(ms):
    c_len = ms[0].shape[0]
    eye = lax.broadcasted_iota(jnp.int32, (c_len, c_len), 0) == lax.broadcasted_iota(jnp.int32, (c_len, c_len), 1)
    ident = jnp.where(eye, 1.0, 0.0)
    ps = ms
    tinvs = [ident - m for m in ms]
    for _ in range(c_len.bit_length() - 2):
        ps = [_dg3(p, p, 1, 0) for p in ps]
        yield
        tinvs = [t + _dg3(t, p, 1, 0) for t, p in zip(tinvs, ps)]
        yield
    return tinvs


@jax.custom_vjp
def _known_inverses(ms, tinvs):
    return tinvs


def _known_inverses_fwd(ms, tinvs):
    return tinvs, tinvs


def _known_inverses_bwd(tinvs, gs):
    return [-_dg3(_dg3(t, g, 0, 0), t, 1, 1) for t, g in zip(tinvs, gs)], [jnp.zeros_like(t) for t in tinvs]


_known_inverses.defvjp(_known_inverses_fwd, _known_inverses_bwd)


def _dn_chunk_gen(qkv, zab, zg, states, alog_e, dt_e, ng, known_tinvs=None):
    c_len = DN_C
    r = lax.broadcasted_iota(jnp.int32, (c_len, c_len), 0)
    c = lax.broadcasted_iota(jnp.int32, (c_len, c_len), 1)
    causal, strict = r >= c, r > c
    tril = jnp.where(causal, 1.0, 0.0)
    rr = lax.broadcasted_iota(jnp.int32, (LANES, D_DN), 0)
    cc = lax.broadcasted_iota(jnp.int32, (LANES, D_DN), 1)
    e_a = jnp.where((cc >= rr * DH) & (cc < rr * DH + DH) & (rr < H), 1.0, 0.0)
    e_b = jnp.where((cc >= (rr - H) * DH) & (cc < (rr - H) * DH + DH) & (rr >= H) & (rr < 2 * H), 1.0, 0.0)
    a_e = _sel_r(zab, e_a)
    b_e = _sel_r(zab, e_b)
    beta = jax.nn.sigmoid(b_e)
    g = -jnp.exp(alog_e) * jax.nn.softplus(a_e + dt_e)
    yield
    gc = _sel_l(tril, g)
    glast = jnp.sum(g, axis=0, keepdims=True)
    eg = jnp.exp(gc)
    ekd = jnp.exp(glast - gc)
    dl = jnp.exp(glast)
    yield
    heads = range(H)
    sls = [slice(h * DH, (h + 1) * DH) for h in heads]
    qs = [qkv[:, h * DH:(h + 1) * DH] for h in heads]
    ks = [qkv[:, D_DN + h * DH:D_DN + (h + 1) * DH] for h in heads]
    vs = [qkv[:, 2 * D_DN + h * DH:2 * D_DN + (h + 1) * DH] for h in heads]
    ccols = [gc[:, sl] for sl in sls]
    decs = [jnp.where(causal, jnp.exp(jnp.where(causal, cl - jnp.transpose(cl), 0.0)), 0.0) for cl in ccols]
    kbs = [k * beta[:, sl] for k, sl in zip(ks, sls)]
    ms = [jnp.where(strict, _mm_nt(kb, k) * dec, 0.0) for kb, k, dec in zip(kbs, ks, decs)]
    yield
    if known_tinvs is None:
        tinvs = yield from _unit_lower_inverse_steps(ms)
    else:
        tinvs = _known_inverses(ms, list(known_tinvs))
    sols = [_dot3(t, jnp.concatenate([v * beta[:, sl], kb * eg[:, sl]], axis=1))
            for t, v, kb, sl in zip(tinvs, vs, kbs, sls)]
    yield
    atts = [_mm_nt(q, k) * dec for q, k, dec in zip(qs, ks, decs)]
    vnews = [sol[:, :DH] - _mm(sol[:, DH:], st) for sol, st in zip(sols, states)]
    yield
    os_ = [_mm(q * eg[:, sl], st) + _mm(att, vn) for q, sl, st, att, vn in zip(qs, sls, states, atts, vnews)]
    yield
    new_states = [st * dl[:, sl] + _mm_tn(k * ekd[:, sl], vn) for st, sl, k, vn in zip(states, sls, ks, vnews)]
    yield
    ys = [_rms(o, ng) * _silu(zg[:, sl]) for o, sl in zip(os_, sls)]
    return jnp.concatenate(ys, axis=1), new_states, tinvs


def _mix_fwd(zs, s5_par, s5_const, qkv, zab, zg, dn_par, bl, s, name, gather=()):
    assert S5_L == DN_C
    n_c = s // DN_C
    nd = len(S5_SHIFTS)
    m = len(gather)
    tok, per_chunk, whole = _mix_specs(bl, n_c, False)

    def body(*refs):
        (z_ref, b_ref, c_ref, lam_ref, dv_ref, wg_ref, bg_ref, ld_ref, lp_ref, lpr_ref,
         q_ref, ab_ref, zg_ref, al_ref, dt_ref, ng_ref) = refs[:16]
        ys_ref, car_ref, h_ref, yd_ref, st_ref, ti_ref = refs[16 + m:22 + m]
        cs, ssc = refs[22 + 2 * m:24 + 2 * m]
        gathering = (_gather_protocol(_mesh_place(), refs[16:16 + m], refs[22 + m:22 + 2 * m], *refs[24 + 2 * m:])
                     if m else None)

        @pl.when(pl.program_id(0) == 0)
        def _():
            cs[...] = jnp.zeros_like(cs)
            ssc[...] = jnp.zeros_like(ssc)
            if m:
                next(gathering)

        lds = [ld_ref[k] for k in range(nd)]
        s5_gens, dn_gens = [], []
        for e in range(bl):
            c = cs[e]
            car_ref[e, 0] = c
            sts = [ssc[e, h] for h in range(H)]
            for h in range(H):
                st_ref[e, 0, h] = sts[h]
            z = z_ref[e]
            s5_gens.append(_s5_chunk_gen(z[:, :D_SSM], z[:, D_SSM:], c[:, :NRE], c[:, NRE:], b_ref[...], c_ref[...], lam_ref[...],
                                         dv_ref[...], wg_ref[...], bg_ref[...], lds, lp_ref[...], lpr_ref[...]))
            dn_gens.append(_dn_chunk_gen(q_ref[e], ab_ref[e], zg_ref[e], sts, al_ref[...], dt_ref[...], ng_ref[...]))
        s5_outs, dn_outs = _interleave(s5_gens, dn_gens, head_start=MIX_HEAD_START)
        for e in range(bl):
            y_s, hr, hi = s5_outs[e]
            y_d, new_sts, tinvs = dn_outs[e]
            ys_ref[e] = y_s
            h_ref[e, :, :NRE] = hr
            h_ref[e, :, NRE:] = hi
            cs[e, :, :NRE] = hr[S5_L - 1:S5_L]
            cs[e, :, NRE:] = hi[S5_L - 1:S5_L]
            yd_ref[e] = y_d
            for h in range(H):
                ssc[e, h] = new_sts[h]
                ti_ref[e, 0, h] = tinvs[h]

        if m:
            @pl.when(pl.program_id(0) == n_c - 1)
            def _():
                for _ in gathering:
                    pass

    head_mats = jax.ShapeDtypeStruct((bl, n_c, H, DH, DH), F32)
    outs = pl.pallas_call(
        body, grid=(n_c,),
        in_specs=[tok(2 * D_SSM)] + [whole(sh) for sh in S5_PAR_SHAPES + S5_CONST_SHAPES]
        + [tok(3 * D_DN), tok(LANES), tok(D_DN)] + [whole(sh) for sh in DN_PAR_SHAPES] + _hbm_specs(m),
        out_specs=[tok(D_SSM), per_chunk((1, 2 * NRE)), tok(2 * NRE), tok(D_DN), per_chunk((H, DH, DH)), per_chunk((H, DH, DH))]
        + _hbm_specs(m),
        out_shape=[jax.ShapeDtypeStruct((bl, s, D_SSM), F32), jax.ShapeDtypeStruct((bl, n_c, 1, 2 * NRE), F32),
                   jax.ShapeDtypeStruct((bl, s, 2 * NRE), F32), jax.ShapeDtypeStruct((bl, s, D_DN), F32), head_mats, head_mats]
        + [jax.ShapeDtypeStruct((N_DEV, *b.shape), b.dtype) for b in gather],
        scratch_shapes=[pltpu.VMEM((bl, 1, 2 * NRE), F32), pltpu.VMEM((bl, H, DH, DH), F32)] + (_gather_sems(m) if m else []),
        name=name, compiler_params=_cp("arbitrary"))(zs, *s5_par, *s5_const, qkv, zab, zg, *dn_par, *gather)
    return outs[:6], outs[6:]


def _mix_bwd(zs, carries, h_all, dy_s, s5_par, s5_const, qkv, zab, zg, states, tinvs, dy_d, dn_par, bl, s, name, exchange=()):
    n_c = s // DN_C
    nd = len(S5_SHIFTS)
    k_ex = len(exchange)
    tok, per_chunk, whole = _mix_specs(bl, n_c, True)

    def both(examples, s5_tabs, s5_consts, dn_tabs):
        s5_gens = [_s5_chunk_gen(u, gate, cr, ci, *s5_tabs, *s5_consts, known_h=(hr, hi))
                   for u, gate, cr, ci, hr, hi, _, _, _, _, _ in examples]
        dn_gens = [_dn_chunk_gen(q, ab, zgate, sts, *dn_tabs, known_tinvs=known)
                   for _, _, _, _, _, _, q, ab, zgate, sts, known in examples]
        s5_outs, dn_outs = _interleave(s5_gens, dn_gens, head_start=MIX_HEAD_START)
        return [(y_s, hr[S5_L - 1:S5_L], hi[S5_L - 1:S5_L], y_d, new_sts)
                for (y_s, hr, hi), (y_d, new_sts, _) in zip(s5_outs, dn_outs)]

    def body(*refs):
        (z_ref, car_ref, h_ref, dys_ref, b_ref, c_ref, lam_ref, dv_ref, wg_ref, bg_ref, ld_ref, lp_ref, lpr_ref,
         q_ref, ab_ref, zg_ref, st_ref, ti_ref, dyd_ref, al_ref, dt_ref, ng_ref) = refs[:22]
        (dz_ref, db_ref, dc_ref, dlam_ref, ddv_ref, dwg_ref, dbg_ref,
         dq_ref, dab_ref, dzg_ref, dal_ref, ddt_ref, dng_ref) = refs[22 + k_ex:35 + k_ex]
        dcs, dsc = refs[35 + 2 * k_ex:37 + 2 * k_ex]
        accs = (db_ref, dc_ref, dlam_ref, ddv_ref, dwg_ref, dbg_ref, dal_ref, ddt_ref, dng_ref)
        exchanging = (_chip_protocol(_mesh_place(), refs[22:22 + k_ex], refs[35 + k_ex:35 + 2 * k_ex], *refs[37 + 2 * k_ex:])
                      if k_ex else None)

        @pl.when(pl.program_id(0) == 0)
        def _():
            for r in accs + (dcs, dsc):
                r[...] = jnp.zeros_like(r)
            if k_ex:
                next(exchanging)

        examples = []
        for e in range(bl):
            z = z_ref[e]
            c = car_ref[e, 0]
            examples.append((z[:, :D_SSM], z[:, D_SSM:], c[:, :NRE], c[:, NRE:], h_ref[e, :, :NRE], h_ref[e, :, NRE:],
                             q_ref[e], ab_ref[e], zg_ref[e], [st_ref[e, 0, h] for h in range(H)],
                             [ti_ref[e, 0, h] for h in range(H)]))
        _, vjp = jax.vjp(both, examples,
                         (b_ref[...], c_ref[...], lam_ref[...], dv_ref[...], wg_ref[...], bg_ref[...]),
                         ([ld_ref[k] for k in range(nd)], lp_ref[...], lpr_ref[...]),
                         (al_ref[...], dt_ref[...], ng_ref[...]))
        cts = []
        for e in range(bl):
            dc = dcs[e]
            cts.append((dys_ref[e], dc[:, :NRE], dc[:, NRE:], dyd_ref[e], [dsc[e, h] for h in range(H)]))
        d_examples, d_s5, _, d_dn = vjp(cts)
        for e in range(bl):
            du, dgate, dcr, dci, _, _, dq, dab, dzg, dsts, _ = d_examples[e]
            dz_ref[e] = jnp.concatenate([du, dgate], axis=1).astype(BF16)
            dcs[e, :, :NRE] = dcr
            dcs[e, :, NRE:] = dci
            dq_ref[e] = dq
            dab_ref[e] = dab.astype(BF16)
            dzg_ref[e] = dzg.astype(BF16)
            for h in range(H):
                dsc[e, h] = dsts[h]
        for r, ct in zip(accs, (*d_s5, *d_dn)):
            r[...] += ct

        if k_ex:
            @pl.when(pl.program_id(0) == n_c - 1)
            def _():
                for _ in exchanging:
                    pass

    head_mats = per_chunk((H, DH, DH))
    outs = pl.pallas_call(
        body, grid=(n_c,),
        in_specs=[tok(2 * D_SSM), per_chunk((1, 2 * NRE)), tok(2 * NRE), tok(D_SSM)]
        + [whole(sh) for sh in S5_PAR_SHAPES + S5_CONST_SHAPES]
        + [tok(3 * D_DN), tok(LANES), tok(D_DN), head_mats, head_mats, tok(D_DN)] + [whole(sh) for sh in DN_PAR_SHAPES]
        + _hbm_specs(k_ex),
        out_specs=[tok(2 * D_SSM)] + [whole(sh) for sh in S5_PAR_SHAPES]
        + [tok(3 * D_DN), tok(LANES), tok(D_DN)] + [whole(sh) for sh in DN_PAR_SHAPES] + _hbm_specs(k_ex),
        out_shape=[jax.ShapeDtypeStruct((bl, s, 2 * D_SSM), BF16)] + [jax.ShapeDtypeStruct(sh, F32) for sh in S5_PAR_SHAPES]
        + [jax.ShapeDtypeStruct((bl, s, 3 * D_DN), F32), jax.ShapeDtypeStruct((bl, s, LANES), BF16),
           jax.ShapeDtypeStruct((bl, s, D_DN), BF16)]
        + [jax.ShapeDtypeStruct(sh, F32) for sh in DN_PAR_SHAPES]
        + [jax.ShapeDtypeStruct(q.shape, q.dtype) for q in exchange],
        scratch_shapes=[pltpu.VMEM((bl, 1, 2 * NRE), F32), pltpu.VMEM((bl, H, DH, DH), F32)] + (_chip_sems(k_ex) if k_ex else []),
        name=name, compiler_params=_cp("arbitrary"))(
            zs, carries, h_all, dy_s, *s5_par, *s5_const, qkv, zab, zg, states, tinvs, dy_d, *dn_par, *exchange)
    return outs[:7], outs[7:13], outs[13:]


def _sg_fn(n_chunk):
    def f(z, lng, lnb, w, bsp_t):
        u = jax.nn.gelu(z[:, :D_SG])
        v = jax.nn.gelu(z[:, D_SG:2 * D_SG])
        gate = z[:, 2 * D_SG:]
        xc = v - jnp.mean(v, axis=-1, keepdims=True)
        vn = xc * lax.rsqrt(jnp.mean(xc * xc, axis=-1, keepdims=True) + EPS) * lng + lnb
        r = lax.broadcasted_iota(jnp.int32, (SG_C, SG_C), 0)
        c = lax.broadcasted_iota(jnp.int32, (SG_C, SG_C), 1)
        causal = r >= c
        first_half = c < SG_C // 2
        rr = lax.broadcasted_iota(jnp.int32, (LANES, D_SG), 0)
        cc = lax.broadcasted_iota(jnp.int32, (LANES, D_SG), 1)
        expand = jnp.where((cc >= rr * 64) & (cc < rr * 64 + 64) & (rr < 4), 1.0, 0.0)
        bias = _sel_r(bsp_t, expand)
        wm = [jnp.where(causal, w[h], 0.0) for h in range(4)]
        rows = []
        for ci in range(n_chunk):
            vc = vn[ci * SG_C:(ci + 1) * SG_C]
            pairs = []
            for pr in range(2):
                vp = vc[:, pr * LANES:(pr + 1) * LANES]
                pairs.append(jnp.where(first_half, _mm(wm[2 * pr], vp), _mm(wm[2 * pr + 1], vp)))
            rows.append(jnp.concatenate(pairs, axis=1) + bias)
        sp = jnp.concatenate(rows, axis=0) if n_chunk > 1 else rows[0]
        return u * sp * _silu(gate)

    return f


def _sg_specs():
    full = lambda i: (0, 0)
    full3 = lambda i: (0, 0, 0)
    par = [pl.BlockSpec((1, D_SG), full), pl.BlockSpec((1, D_SG), full), pl.BlockSpec((4, SG_C, SG_C), full3),
           pl.BlockSpec((SG_C, LANES), full)]
    par_shapes = [(1, D_SG), (1, D_SG), (4, SG_C, SG_C), (SG_C, LANES)]
    return par, par_shapes


def _sg_fwd(zsg, params, tb, name):
    t = zsg.shape[0]
    f = _sg_fn(tb // SG_C)
    par, _ = _sg_specs()

    def body(z_ref, g_ref, b_ref, w_ref, bs_ref, y_ref):
        y_ref[...] = f(z_ref[...], g_ref[...], b_ref[...], w_ref[...], bs_ref[...])

    row = lambda i: (i, 0)
    return pl.pallas_call(
        body, grid=(t // tb,), in_specs=[pl.BlockSpec((tb, 3 * D_SG), row)] + par,
        out_specs=pl.BlockSpec((tb, D_SG), row), out_shape=jax.ShapeDtypeStruct((t, D_SG), F32),
        name=name, compiler_params=_cp("parallel"))(zsg, *params)


def _sg_bwd(zsg, dy, params, tb, name):
    t = zsg.shape[0]
    f = _sg_fn(tb // SG_C)
    par, par_shapes = _sg_specs()

    def body(z_ref, dy_ref, g_ref, b_ref, w_ref, bs_ref, dz_ref, dg_ref, db_ref, dw_ref, dbs_ref):
        accs = (dg_ref, db_ref, dw_ref, dbs_ref)

        @pl.when(pl.program_id(0) == 0)
        def _():
            for r in accs:
                r[...] = jnp.zeros_like(r)

        _, vjp = jax.vjp(f, z_ref[...], g_ref[...], b_ref[...], w_ref[...], bs_ref[...])
        cts = vjp(dy_ref[...])
        dz_ref[...] = cts[0].astype(BF16)
        for r, ct in zip(accs, cts[1:]):
            r[...] += ct

    row = lambda i: (i, 0)
    return pl.pallas_call(
        body, grid=(t // tb,), in_specs=[pl.BlockSpec((tb, 3 * D_SG), row), pl.BlockSpec((tb, D_SG), row)] + par,
        out_specs=[pl.BlockSpec((tb, 3 * D_SG), row)] + par,
        out_shape=[jax.ShapeDtypeStruct((t, 3 * D_SG), BF16)] + [jax.ShapeDtypeStruct(sh, F32) for sh in par_shapes],
        name=name, compiler_params=_cp("arbitrary"))(zsg, dy, *params)


def _out_fwd(x, ys, p, w_out, pg, w_gate, w_ple, tb, name):
    t = x.shape[0]

    def body(x_ref, y0, y1, y2, p_ref, wo_ref, pg_ref, wg_ref, wp_ref, o_ref, x1_ref, gate_ref):
        y = jnp.concatenate([y0[...], y1[...], y2[...]], axis=1).astype(BF16)
        x1 = x_ref[...] + jnp.dot(y, wo_ref[...], preferred_element_type=F32)
        hn = _rms(x1, pg_ref[...]).astype(BF16)
        gate = jax.nn.sigmoid(jnp.dot(hn, wg_ref[...], preferred_element_type=F32))
        pp = jnp.dot(p_ref[...].astype(BF16), wp_ref[...], preferred_element_type=F32)
        o_ref[...] = x1 + gate * pp
        x1_ref[...] = x1
        gate_ref[...] = gate

    row = lambda i: (i, 0)
    full = lambda i: (0, 0)
    return pl.pallas_call(
        body, grid=(t // tb,),
        in_specs=[pl.BlockSpec((tb, D), row), pl.BlockSpec((tb, D_SSM), row), pl.BlockSpec((tb, D_DN), row),
                  pl.BlockSpec((tb, D_SG), row), pl.BlockSpec((tb, D_PLE), row), pl.BlockSpec((D, D), full),
                  pl.BlockSpec((1, D), full), pl.BlockSpec((D, D), full), pl.BlockSpec((D_PLE, D), full)],
        out_specs=[pl.BlockSpec((tb, D), row)] * 3, out_shape=[jax.ShapeDtypeStruct((t, D), F32)] * 3,
        name=name, compiler_params=_cp("parallel"))(x, *ys, p, w_out, pg, w_gate, w_ple)


def _out_bwd(x1, gate, ys, p, dx2, w_out, pg, w_gate, w_ple, tb, name):
    t = x1.shape[0]

    def body(x1_ref, gate_ref, y0, y1, y2, p_ref, d_ref, wo_ref, pg_ref, wg_ref, wp_ref,
             dx_ref, dy0, dy1, dy2, dwo_ref, dpg_ref, dwg_ref, dwp_ref):
        accs = (dwo_ref, dpg_ref, dwg_ref, dwp_ref)

        @pl.when(pl.program_id(0) == 0)
        def _():
            for r in accs:
                r[...] = jnp.zeros_like(r)

        y = jnp.concatenate([y0[...], y1[...], y2[...]], axis=1).astype(BF16)
        hn, rms_vjp = jax.vjp(_rms, x1_ref[...], pg_ref[...])
        hb = hn.astype(BF16)
        gate = gate_ref[...]
        pb = p_ref[...].astype(BF16)
        pp = jnp.dot(pb, wp_ref[...], preferred_element_type=F32)
        d2 = d_ref[...]
        dpp = (d2 * gate).astype(BF16)
        dlog = (d2 * pp * gate * (1.0 - gate)).astype(BF16)
        dwp_ref[...] += _dg(pb, dpp, 0, 0)
        dwg_ref[...] += _dg(hb, dlog, 0, 0)
        dx1_n, dpg = rms_vjp(_dg(dlog, wg_ref[...], 1, 1))
        dpg_ref[...] += dpg
        dx1 = d2 + dx1_n
        dx_ref[...] = dx1
        db = dx1.astype(BF16)
        dwo_ref[...] += _dg(y, db, 0, 0)
        dy = _dg(db, wo_ref[...], 1, 1)
        dy0[...] = dy[:, :D_SSM]
        dy1[...] = dy[:, D_SSM:D_SSM + D_DN]
        dy2[...] = dy[:, D_SSM + D_DN:]

    row = lambda i: (i, 0)
    full = lambda i: (0, 0)
    acts = [pl.BlockSpec((tb, D), row), pl.BlockSpec((tb, D_SSM), row), pl.BlockSpec((tb, D_DN), row), pl.BlockSpec((tb, D_SG), row)]
    wts = [pl.BlockSpec((D, D), full), pl.BlockSpec((1, D), full), pl.BlockSpec((D, D), full), pl.BlockSpec((D_PLE, D), full)]
    return pl.pallas_call(
        body, grid=(t // tb,),
        in_specs=[pl.BlockSpec((tb, D), row)] + acts + [pl.BlockSpec((tb, D_PLE), row), pl.BlockSpec((tb, D), row)] + wts,
        out_specs=acts + wts,
        out_shape=[jax.ShapeDtypeStruct((t, n), F32) for n in (D, D_SSM, D_DN, D_SG)]
        + [jax.ShapeDtypeStruct(sh, F32) for sh in ((D, D), (1, D), (D, D), (D_PLE, D))],
        name=name, compiler_params=_cp("arbitrary"))(x1, gate, *ys, p, dx2, w_out, pg, w_gate, w_ple)


def _loss_head(x, fg, target, tb, name):
    t = x.shape[0]

    def body(x_ref, g_ref, t_ref, dx_ref, dg_ref, loss_ref):
        @pl.when(pl.program_id(0) == 0)
        def _():
            dg_ref[...] = jnp.zeros_like(dg_ref)
            loss_ref[...] = jnp.zeros_like(loss_ref)

        y, vjp = jax.vjp(_rms, x_ref[...], g_ref[...])
        err = y - t_ref[...]
        loss_ref[...] += jnp.zeros_like(loss_ref) + 0.5 * jnp.sum(err * err) / D
        dx, dg = vjp(err / D)
        dx_ref[...] = dx
        dg_ref[...] += dg

    row = lambda i: (i, 0)
    full = lambda i: (0, 0)
    return pl.pallas_call(
        body, grid=(t // tb,),
        in_specs=[pl.BlockSpec((tb, D), row), pl.BlockSpec((1, D), full), pl.BlockSpec((tb, D), row)],
        out_specs=[pl.BlockSpec((tb, D), row), pl.BlockSpec((1, D), full), pl.BlockSpec((1, LANES), full)],
        out_shape=[jax.ShapeDtypeStruct((t, D), F32), jax.ShapeDtypeStruct((1, D), F32), jax.ShapeDtypeStruct((1, LANES), F32)],
        name=name, compiler_params=_cp("arbitrary"))(x, fg, target)


def _hbm_specs(n):
    return [pl.BlockSpec(memory_space=pl.ANY)] * n


def _gather_protocol(place, ins, outs, send_sems, recv_sems, local_sems):
    n = len(ins)
    x, y, c, other_x, other_y, other_c = place
    me, sibling = (x, y, c), (x, y, other_c)
    chips = [(other_x, y), (x, other_y), (other_x, other_y)]

    def slot(a, px, py, pc):
        return outs[a].at[4 * px + 2 * py + pc]

    def copy(a, k, blk, to, src=None):
        return pltpu.make_async_remote_copy(
            src_ref=slot(a, *blk) if src is None else src, dst_ref=slot(a, *blk),
            send_sem=send_sems.at[7 * a + k], recv_sem=recv_sems.at[7 * a + k],
            device_id=to, device_id_type=pl.DeviceIdType.MESH)

    def own_copies():
        mines = [pltpu.make_async_copy(ins[a], slot(a, *me), local_sems.at[a]) for a in range(n)]
        first = []
        for a in range(n):
            first.append(copy(a, 0, me, sibling, src=ins[a]))
            first += [copy(a, 1 + j, me, (*chip, c), src=ins[a]) for j, chip in enumerate(chips)]
        return mines, first

    mines, first = own_copies()
    for cp in mines + first:
        cp.start()
    yield
    mines, first = own_copies()
    passed = []
    for j, chip in enumerate(chips):
        for a in range(n):
            copy(a, 1 + j, (*chip, c), me).wait_recv()
            onward = copy(a, 4 + j, (*chip, c), sibling)
            onward.start()
            passed.append(onward)
    for a in range(n):
        copy(a, 0, sibling, me).wait_recv()
    for j, chip in enumerate(chips):
        for a in range(n):
            copy(a, 4 + j, (*chip, other_c), me).wait_recv()
    for cp in first + passed:
        cp.wait_send()
    for cp in mines:
        cp.wait()


def _mesh_place():
    x, y, c = lax.axis_index("x"), lax.axis_index("y"), lax.axis_index("c")
    return x, y, c, 1 - x, 1 - y, 1 - c


def _gather_sems(n):
    return [pltpu.SemaphoreType.DMA((7 * n,)), pltpu.SemaphoreType.DMA((7 * n,)), pltpu.SemaphoreType.DMA((n,))]


def _all_gather(blocks, name):
    n = len(blocks)

    def body(*refs):
        for _ in _gather_protocol(_mesh_place(), refs[:n], refs[n:2 * n], *refs[2 * n:]):
            pass

    return pl.pallas_call(
        body, out_shape=[jax.ShapeDtypeStruct((N_DEV, *b.shape), b.dtype) for b in blocks],
        in_specs=_hbm_specs(n), out_specs=_hbm_specs(n), scratch_shapes=_gather_sems(n), name=name)(*blocks)


def _pair_exchange(gs, name):
    n = len(gs)

    def body(*refs):
        ins, recvs = refs[:n], refs[n:2 * n]
        send_sems, recv_sems = refs[2 * n:]
        x, y, c = lax.axis_index("x"), lax.axis_index("y"), lax.axis_index("c")
        remote = [pltpu.make_async_remote_copy(
            src_ref=ins[a], dst_ref=recvs[a], send_sem=send_sems.at[a], recv_sem=recv_sems.at[a],
            device_id=(x, y, 1 - c), device_id_type=pl.DeviceIdType.MESH) for a in range(n)]
        for cp in remote:
            cp.start()
        for cp in remote:
            cp.wait_send()
            cp.wait_recv()

    return pl.pallas_call(
        body, out_shape=[jax.ShapeDtypeStruct(g.shape, g.dtype) for g in gs], in_specs=_hbm_specs(n), out_specs=_hbm_specs(n),
        scratch_shapes=[pltpu.SemaphoreType.DMA((n,)), pltpu.SemaphoreType.DMA((n,))],
        name=name)(*gs)


def _chip_protocol(place, ins, outs, send_sems, recv_sems, local_sems):
    n = len(ins)
    x, y, c, other_x, other_y, _ = place

    def copies():
        my_chip = 2 * x + y
        local = [pltpu.make_async_copy(ins[a].at[my_chip], outs[a].at[my_chip], local_sems.at[a]) for a in range(n)]
        remote = []
        for j in range(1, 4):
            px = other_x if j & 2 else x
            py = other_y if j & 1 else y
            for a in range(n):
                remote.append(pltpu.make_async_remote_copy(
                    src_ref=ins[a].at[2 * px + py], dst_ref=outs[a].at[my_chip],
                    send_sem=send_sems.at[3 * a + j - 1], recv_sem=recv_sems.at[3 * a + j - 1],
                    device_id=(px, py, c), device_id_type=pl.DeviceIdType.MESH))
        return local, remote

    local, remote = copies()
    for cp in local + remote:
        cp.start()
    yield
    local, remote = copies()
    for cp in remote:
        cp.wait_send()
        cp.wait_recv()
    for cp in local:
        cp.wait()


def _chip_sems(n):
    return [pltpu.SemaphoreType.DMA((3 * n,)), pltpu.SemaphoreType.DMA((3 * n,)), pltpu.SemaphoreType.DMA((n,))]


def _chip_exchange(ps, gather, name):
    n, m = len(ps), len(gather)

    def body(*refs):
        ins, g_ins, outs, g_outs = refs[:n], refs[n:n + m], refs[n + m:2 * n + m], refs[2 * n + m:2 * (n + m)]
        place = _mesh_place()
        exchanging = _chip_protocol(place, ins, outs, *refs[2 * (n + m):2 * (n + m) + 3])
        gathering = _gather_protocol(place, g_ins, g_outs, *refs[2 * (n + m) + 3:])
        next(exchanging)
        for _ in gathering:
            pass
        for _ in exchanging:
            pass

    outs = pl.pallas_call(
        body, out_shape=[jax.ShapeDtypeStruct(q.shape, q.dtype) for q in ps]
        + [jax.ShapeDtypeStruct((N_DEV, *b.shape), b.dtype) for b in gather],
        in_specs=_hbm_specs(n + m), out_specs=_hbm_specs(n + m),
        scratch_shapes=_chip_sems(n) + _gather_sems(m), name=name)(*ps, *gather)
    return outs[:n], outs[n:]


def _row_block(rows, bytes_per_row):
    best = None
    for rb in range(16, rows + 1, 16):
        if rows % rb == 0 and rb * bytes_per_row <= ELEMENTWISE_STEP_BYTES:
            best = rb
    return rows if best is None else best


def _add_pair(own, recv, name):
    shape = own.shape
    last = shape[-1]
    rows = own.size // last
    rb = _row_block(rows, 3 * 4 * (-(-last // LANES) * LANES))

    def body(a_ref, b_ref, o_ref):
        o_ref[...] = (a_ref[...].astype(F32) + b_ref[...].astype(F32)).astype(o_ref.dtype)

    row = lambda i: (i, 0)
    out = pl.pallas_call(
        body, grid=(rows // rb,), in_specs=[pl.BlockSpec((rb, last), row)] * 2, out_specs=pl.BlockSpec((rb, last), row),
        out_shape=jax.ShapeDtypeStruct((rows, last), own.dtype), name=name,
        compiler_params=_cp("parallel"))(own.reshape(rows, last), recv.reshape(rows, last))
    return out.reshape(shape)


def _sum_adamw(gk, w, m, v, name):
    shape = w.shape
    n_part = gk.shape[0]
    last = shape[-1]
    rows = w.size // last
    rb = _row_block(rows, (n_part + 7) * 4 * (-(-last // LANES) * LANES))

    def body(g_ref, w_ref, m_ref, v_ref, go_ref, d_ref, mo_ref, vo_ref):
        g = g_ref[0].astype(F32)
        for k in range(1, n_part):
            g = g + g_ref[k].astype(F32)
        mn = ADAM_B1 * m_ref[...] + (1.0 - ADAM_B1) * g
        vn = ADAM_B2 * v_ref[...] + (1.0 - ADAM_B2) * jnp.square(g)
        m_hat = mn / (1.0 - ADAM_B1 ** ADAM_STEP)
        v_hat = vn / (1.0 - ADAM_B2 ** ADAM_STEP)
        go_ref[...] = g
        d_ref[...] = -ADAM_LR * (m_hat / (jnp.sqrt(v_hat) + ADAM_EPS) + ADAM_WD * w_ref[...])
        mo_ref[...] = mn
        vo_ref[...] = vn

    row = lambda i: (i, 0)
    outs = pl.pallas_call(
        body, grid=(rows // rb,),
        in_specs=[pl.BlockSpec((n_part, rb, last), lambda i: (0, i, 0))] + [pl.BlockSpec((rb, last), row)] * 3,
        out_specs=[pl.BlockSpec((rb, last), row)] * 4,
        out_shape=[jax.ShapeDtypeStruct((rows, last), F32)] * 4,
        name=name, compiler_params=_cp("parallel"))(gk.reshape(n_part, rows, last), *[a.reshape(rows, last) for a in (w, m, v)])
    return [o.reshape(shape) for o in outs]


def _seg_rows(shape):
    n = 1
    for d in shape:
        n *= d
    return -(-n // (8 * LANES)) * 8


def _pack(arrs):
    segs = []
    for a in arrs:
        r = _seg_rows(a.shape)
        segs.append(jnp.pad(a.reshape(-1).astype(F32), (0, r * LANES - a.size)).reshape(r, LANES))
    rows = sum(s.shape[0] for s in segs)
    total = -(-rows // PACK_ROWS) * PACK_ROWS
    if total > rows:
        segs.append(jnp.zeros((total - rows, LANES), F32))
    return jnp.concatenate(segs, axis=0)


def _unpack(pack, shapes):
    out, off = [], 0
    for sh in shapes:
        r = _seg_rows(sh)
        n = 1
        for d in sh:
            n *= d
        out.append(pack[off:off + r].reshape(-1)[:n].reshape(sh))
        off += r
    return out


def _to_dest_blocks(full, axis, dtype):
    sh = list(full.shape)
    sh[axis:axis + 1] = [N_DEV // 2, 2, sh[axis] // N_DEV]
    return jnp.moveaxis(full.reshape(sh), (axis, axis + 1), (1, 0)).astype(dtype)


def _from_gathered(g, axis):
    m = jnp.moveaxis(g, 0, axis)
    sh = list(m.shape)
    sh[axis:axis + 2] = [sh[axis] * sh[axis + 1]]
    return m.reshape(sh)


D_IN = 3336
W_IN_SHARD = D_IN // N_DEV
W_IN_MOVES = ((0, 2048, 0), (2048, 2056, 3328), (2056, D_IN, 2048))


def _w_in_windows(k):
    lo, hi = k * W_IN_SHARD, (k + 1) * W_IN_SHARD
    out = []
    for a, b, mine in W_IN_MOVES:
        a2, b2 = max(a, lo), min(b, hi)
        if b2 > a2:
            out.append((a2 - lo, b2 - a2, mine + a2 - a))
    return out


def _assemble_w_in(gathered, name):
    depth = gathered.shape[1]
    rb = 256

    def body(g_ref, o_ref):
        o_ref[0, :, D_IN:] = jnp.zeros((rb, ZW - D_IN), o_ref.dtype)
        for k in range(N_DEV):
            for off, width, mine in _w_in_windows(k):
                o_ref[0, :, mine:mine + width] = g_ref[k, 0, :, off:off + width]

    return pl.pallas_call(
        body, grid=(depth, D // rb),
        in_specs=[pl.BlockSpec((N_DEV, 1, rb, W_IN_SHARD), lambda l, i: (0, l, i, 0))],
        out_specs=pl.BlockSpec((1, rb, ZW), lambda l, i: (l, i, 0)),
        out_shape=jax.ShapeDtypeStruct((depth, D, ZW), gathered.dtype),
        name=name, compiler_params=_cp("parallel", "parallel"))(gathered)


def _split_dw_in(dws, name):
    depth = len(dws)
    rb = 128

    def body(*refs):
        o_ref = refs[-1]
        for l in range(depth):
            pieces = refs[5 * l:5 * l + 5]
            for k in range(N_DEV):
                for off, width, mine in _w_in_windows(k):
                    for p_ref, start, n in zip(pieces, Z_OFFSETS, Z_PIECES):
                        a, b = max(mine, start), min(mine + width, start + n)
                        if b > a:
                            o_ref[k % 2, k // 2, l, :, off + a - mine:off + b - mine] = (
                                p_ref[:, a - start:b - start].astype(o_ref.dtype))

    row = lambda i: (i, 0)
    flat = [piece for layer in dws for piece in layer]
    return pl.pallas_call(
        body, grid=(D // rb,),
        in_specs=[pl.BlockSpec((rb, n), row) for _ in range(depth) for n in Z_PIECES],
        out_specs=pl.BlockSpec((2, N_DEV // 2, depth, rb, W_IN_SHARD), lambda i: (0, 0, 0, i, 0)),
        out_shape=jax.ShapeDtypeStruct((2, N_DEV // 2, depth, D, W_IN_SHARD), WIRE['w_in']),
        name=name, compiler_params=_cp("parallel"))(*flat)


EARLY = ('w_in', 'ssm_w_glu', 'dn_conv_w')
LATE = ('w_out', 'w_ple_gate', 'w_ple')


def _layer_weights(names, gathered, layer):
    full = {}
    for n, g in zip(names, gathered):
        full[n] = _assemble_w_in(g, f"assemble_w_in_l{layer}") if n == 'w_in' else _from_gathered(g, SHARDED[n])
        if n == 'ssm_w_glu':
            full[n] = full[n].astype(F32)
    return full


def _pair_reduce(names, layer_grads, tag):
    dest = [_split_dw_in([layer_grads[n]], f"split_dw_in_{tag}") if n == 'w_in'
            else _to_dest_blocks(layer_grads[n][None], SHARDED[n], WIRE[n]) for n in names]
    c = lax.axis_index("c")
    own = [lax.dynamic_index_in_dim(d, c, 0, keepdims=False) for d in dest]
    for_sibling = [lax.dynamic_index_in_dim(d, 1 - c, 0, keepdims=False) for d in dest]
    from_sibling = _pair_exchange(for_sibling, f"grads_pair_exchange_{tag}")
    return [_add_pair(a, b, f"grads_pair_sum_{n}_{tag}") for n, a, b in zip(names, own, from_sibling)]


def _local_step(x, p, wts, sharded, target, shards=None, reduce_early=False):
    sharded = [dict(d or {}) for d in sharded]
    bl, s, _ = x.shape
    t = bl * s
    depth = p.shape[0]
    tb, sg_tb = TB, SG_TB

    def by_example(a):
        return a.reshape(bl, s, a.shape[-1])

    def flat(a):
        return a.reshape(t, a.shape[-1])

    xs = [x.reshape(t, D)]
    saved = []
    for i in range(depth):
        li = f"l{i}"
        ng = wts['norm_g'][i].reshape(1, D)
        lw = sharded[i]
        w_in = lw['w_in'][0]
        s5_par_in = (wts['ssm_a_re'][i], wts['ssm_a_im'][i], wts['ssm_b_re'][i], wts['ssm_b_im'][i],
                     wts['ssm_c_re'][i], wts['ssm_c_im'][i], wts['ssm_d'][i], wts['ssm_log_step'][i])
        tabs, tab_vjp = jax.vjp(_s5_tables, *s5_par_in)
        s5_par = (*tabs, lw['ssm_w_glu'][0], wts['ssm_b_glu'][i].reshape(1, D_SSM))
        s5_const = _s5_powers(wts['ssm_a_re'][i], wts['ssm_a_im'][i], wts['ssm_log_step'][i])
        conv8 = jnp.pad(lw['dn_conv_w'][0], ((0, 4), (0, 0)))
        dn_par = (jnp.repeat(wts['dn_a_log'][i], DH).reshape(1, D_DN), jnp.repeat(wts['dn_dt_bias'][i], DH).reshape(1, D_DN),
                  wts['dn_norm_g'][i].reshape(1, DH))
        sg_par = (wts['sg_ln_g'][i].reshape(1, D_SG), wts['sg_ln_b'][i].reshape(1, D_SG), wts['sg_w'][i],
                  jnp.pad(jnp.transpose(wts['sg_b'][i]), ((0, 0), (0, LANES - 4))))
        pi = p[i].reshape(t, D_PLE)

        z_ssm, z_qkv, z_gdn, z_sg, z_ab, qkvn = _in_proj_fwd(xs[i], ng, w_in, conv8, s, tb, f"in_proj_fwd_{li}")
        wanted = [(i, n) for n in LATE if n not in lw]
        if i + 1 < depth:
            wanted += [(i + 1, n) for n in SHARDED_ORDER if n not in sharded[i + 1]]
        (y_ssm, carries, h_all, y_dn, states, tinvs), gathered = _mix_fwd(
            by_example(z_ssm), s5_par, s5_const, by_example(qkvn), by_example(z_ab), by_example(z_gdn), dn_par, bl, s,
            f"mix_fwd_{li}", gather=[shards[l][n] for l, n in wanted])
        for l in sorted({ll for ll, _ in wanted}):
            names = [n for ll, n in wanted if ll == l]
            sharded[l].update(_layer_weights(names, [g for (ll, _), g in zip(wanted, gathered) if ll == l], l))
        out_par = (lw['w_out'][0].astype(BF16), wts['ple_norm_g'][i].reshape(1, D), lw['w_ple_gate'][0].astype(BF16),
                   lw['w_ple'][0].astype(BF16))
        y_sg = _sg_fwd(z_sg, sg_par, sg_tb, f"sg_fwd_{li}")
        ys = (flat(y_ssm), flat(y_dn), y_sg)
        x_next, x1, gate = _out_fwd(xs[i], ys, pi, *out_par, tb, f"out_fwd_{li}")
        xs.append(x_next)
        saved.append(dict(ng=ng, w_in=w_in, tab_vjp=tab_vjp, s5_par=s5_par, s5_const=s5_const, conv8=conv8, dn_par=dn_par,
                          sg_par=sg_par, out_par=out_par, pi=pi, z=(z_ssm, z_qkv, z_gdn, z_sg, z_ab), carries=carries,
                          h_all=h_all, qkvn=qkvn, x1=x1, gate=gate,
                          states=states, tinvs=tinvs, ys=ys))

    dx, dfg, loss_vec = _loss_head(xs[depth], wts['final_norm_g'].reshape(1, D), target.reshape(t, D), tb, "loss_head")
    grads = {n: [None] * depth for n in WEIGHTS if n != 'final_norm_g'}
    grads['final_norm_g'] = dfg.reshape(D)
    pair_sums, by_chip = {}, {}
    for i in reversed(range(depth)):
        li = f"l{i}"
        sv = saved[i]
        z_ssm, z_qkv, z_gdn, z_sg, z_ab = sv['z']
        dx_res, dy_ssm, dy_dn, dy_sg, dwo, dpg, dwg, dwp = _out_bwd(sv['x1'], sv['gate'], sv['ys'], sv['pi'], dx, *sv['out_par'], tb,
                                                                    f"out_bwd_{li}")
        dz_sg, dlng, dlnb, dsgw, dbsp = _sg_bwd(z_sg, dy_sg, sv['sg_par'], sg_tb, f"sg_bwd_{li}")
        if reduce_early:
            for n, gsum in zip(LATE, _pair_reduce(LATE, {'w_out': dwo, 'w_ple_gate': dwg, 'w_ple': dwp}, f"late_{li}")):
                pair_sums[(i, n)] = gsum
        travelling = [key for key in pair_sums if key not in by_chip]
        (dz_ssm, dbb, dcb, dlam, ddv, dwglu, dbglu), (dqkvn, dz_ab, dz_gdn, dal, ddt, dng), exchanged = _mix_bwd(
            by_example(z_ssm), sv['carries'], sv['h_all'], by_example(dy_ssm), sv['s5_par'], sv['s5_const'],
            by_example(sv['qkvn']), by_example(z_ab), by_example(z_gdn), sv['states'], sv['tinvs'], by_example(dy_dn),
            sv['dn_par'], bl, s, f"mix_bwd_{li}", exchange=[pair_sums[key] for key in travelling])
        by_chip.update(zip(travelling, exchanged))
        dx, dnorm, dz_qkv, dconv = _in_proj_bwd_dx(xs[i], sv['ng'], sv['w_in'], sv['conv8'], flat(dz_ssm), flat(dz_gdn), dz_sg,
                                                   flat(dz_ab), z_qkv, flat(dqkvn), dx_res, s, tb, f"in_proj_bwd_dx_{li}")
        dzs = (flat(dz_ssm), dz_qkv, flat(dz_gdn), dz_sg, flat(dz_ab))
        dws = _in_proj_bwd_dw(xs[i], sv['ng'], dzs, min(TB_DW, t), f"in_proj_bwd_dw_{li}")
        ds5 = sv['tab_vjp']((dbb, dcb, dlam, ddv))
        for n, gval in zip(('ssm_a_re', 'ssm_a_im', 'ssm_b_re', 'ssm_b_im', 'ssm_c_re', 'ssm_c_im', 'ssm_d', 'ssm_log_step'), ds5):
            grads[n][i] = gval
        grads['norm_g'][i] = dnorm.reshape(D)
        grads['w_in'][i] = dws
        grads['ssm_w_glu'][i] = dwglu
        grads['ssm_b_glu'][i] = dbglu.reshape(D_SSM)
        grads['dn_conv_w'][i] = dconv[:4]
        grads['dn_a_log'][i] = dal.reshape(H, DH).sum(axis=1)
        grads['dn_dt_bias'][i] = ddt.reshape(H, DH).sum(axis=1)
        grads['dn_norm_g'][i] = dng.reshape(DH)
        grads['sg_ln_g'][i] = dlng.reshape(D_SG)
        grads['sg_ln_b'][i] = dlnb.reshape(D_SG)
        grads['sg_w'][i] = dsgw
        grads['sg_b'][i] = jnp.transpose(dbsp[:, :4])
        grads['w_out'][i] = dwo
        grads['ple_norm_g'][i] = dpg.reshape(D)
        grads['w_ple_gate'][i] = dwg
        grads['w_ple'][i] = dwp
        if reduce_early:
            for n, gsum in zip(EARLY, _pair_reduce(EARLY, {n: grads[n][i] for n in EARLY}, f"early_{li}")):
                pair_sums[(i, n)] = gsum
    grads = {n: (g if n in ('final_norm_g', 'w_in') else jnp.stack(g)) for n, g in grads.items()}
    return loss_vec[0, 0], dx.reshape(bl, s, D), grads, {k: v for k, v in pair_sums.items() if k not in by_chip}, by_chip


def kernel(x, p, norm_g, w_in, ssm_a_re, ssm_a_im, ssm_b_re, ssm_b_im, ssm_c_re, ssm_c_im, ssm_d, ssm_log_step, ssm_w_glu, ssm_b_glu, dn_conv_w, dn_a_log, dn_dt_bias, dn_norm_g, sg_ln_g, sg_ln_b, sg_w, sg_b, w_out, ple_norm_g, w_ple_gate, w_ple, final_norm_g, loss_target, m_norm_g, m_w_in, m_ssm_a_re, m_ssm_a_im, m_ssm_b_re, m_ssm_b_im, m_ssm_c_re, m_ssm_c_im, m_ssm_d, m_ssm_log_step, m_ssm_w_glu, m_ssm_b_glu, m_dn_conv_w, m_dn_a_log, m_dn_dt_bias, m_dn_norm_g, m_sg_ln_g, m_sg_ln_b, m_sg_w, m_sg_b, m_w_out, m_ple_norm_g, m_w_ple_gate, m_w_ple, m_final_norm_g, v_norm_g, v_w_in, v_ssm_a_re, v_ssm_a_im, v_ssm_b_re, v_ssm_b_im, v_ssm_c_re, v_ssm_c_im, v_ssm_d, v_ssm_log_step, v_ssm_w_glu, v_ssm_b_glu, v_dn_conv_w, v_dn_a_log, v_dn_dt_bias, v_dn_norm_g, v_sg_ln_g, v_sg_ln_b, v_sg_w, v_sg_b, v_w_out, v_ple_norm_g, v_w_ple_gate, v_w_ple, v_final_norm_g):
    w_loc = dict(zip(WEIGHTS, (norm_g, w_in, ssm_a_re, ssm_a_im, ssm_b_re, ssm_b_im, ssm_c_re, ssm_c_im, ssm_d, ssm_log_step,
                               ssm_w_glu, ssm_b_glu, dn_conv_w, dn_a_log, dn_dt_bias, dn_norm_g, sg_ln_g, sg_ln_b, sg_w, sg_b,
                               w_out, ple_norm_g, w_ple_gate, w_ple, final_norm_g)))
    m_loc = dict(zip(WEIGHTS, (m_norm_g, m_w_in, m_ssm_a_re, m_ssm_a_im, m_ssm_b_re, m_ssm_b_im, m_ssm_c_re, m_ssm_c_im, m_ssm_d,
                               m_ssm_log_step, m_ssm_w_glu, m_ssm_b_glu, m_dn_conv_w, m_dn_a_log, m_dn_dt_bias, m_dn_norm_g,
                               m_sg_ln_g, m_sg_ln_b, m_sg_w, m_sg_b, m_w_out, m_ple_norm_g, m_w_ple_gate, m_w_ple, m_final_norm_g)))
    v_loc = dict(zip(WEIGHTS, (v_norm_g, v_w_in, v_ssm_a_re, v_ssm_a_im, v_ssm_b_re, v_ssm_b_im, v_ssm_c_re, v_ssm_c_im, v_ssm_d,
                               v_ssm_log_step, v_ssm_w_glu, v_ssm_b_glu, v_dn_conv_w, v_dn_a_log, v_dn_dt_bias, v_dn_norm_g,
                               v_sg_ln_g, v_sg_ln_b, v_sg_w, v_sg_b, v_w_out, v_ple_norm_g, v_w_ple_gate, v_w_ple, v_final_norm_g)))

    depth = p.shape[0]
    shards = [{n: w_loc[n][l:l + 1].astype(WIRE[n]) for n in SHARDED_ORDER} for l in range(depth)]
    first = _layer_weights(EARLY, _all_gather([shards[0][n] for n in EARLY], "gather_weights_l0"), 0)

    loss_part, grad_x, grads, left, by_chip = _local_step(
        x, p, w_loc, [first] + [None] * (depth - 1), loss_target, shards, reduce_early=True)

    rep_pack = _pack([grads[n] for n in REPLICATED_ORDER] + [loss_part.reshape(1)])
    exchanged, (rep_recv,) = _chip_exchange(list(left.values()), [rep_pack], "grads_chip_exchange_l0")
    by_chip.update(zip(left, exchanged))
    by_chip = [jnp.concatenate([by_chip[(l, n)] for l in range(depth)], axis=1) for n in SHARDED_ORDER]

    outs = {k: {} for k in 'gdmv'}
    for n, gk in zip(SHARDED_ORDER, by_chip):
        for k, o in zip('gdmv', _sum_adamw(gk, w_loc[n], m_loc[n], v_loc[n], f"adamw_{n}")):
            outs[k][n] = o
    one = jnp.zeros((1,), F32)
    rep_out = _sum_adamw(rep_recv, _pack([w_loc[n] for n in REPLICATED_ORDER] + [one]),
                         _pack([m_loc[n] for n in REPLICATED_ORDER] + [one]),
                         _pack([v_loc[n] for n in REPLICATED_ORDER] + [one]), "adamw_replicated")
    rep_shapes = [w_loc[n].shape for n in REPLICATED_ORDER] + [(1,)]
    for k, rep_p in zip('gdmv', rep_out):
        outs[k].update(zip(REPLICATED_ORDER + ['loss'], _unpack(rep_p, rep_shapes)))
    loss = outs['g']['loss'].reshape(())
    return (loss, grad_x, *[outs['g'][n] for n in WEIGHTS], *[outs['d'][n] for n in WEIGHTS],
            *[outs['m'][n] for n in WEIGHTS], *[outs['v'][n] for n in WEIGHTS])
```

```python
import functools

import jax
import jax.numpy as jnp
from jax import lax
from jax.experimental import pallas as pl
from jax.experimental.pallas import tpu as pltpu

F32 = jnp.float32
BF16 = jnp.bfloat16
EPS = 1e-6

D = 1024
D_PLE = 256
D_SSM = 256
D_DN = 512
D_SG = 256
G = 16
CG = 16
NS = 64
NRE = G * NS
H = 4
DH = 128
DN_C = 128
SG_C = 128
ZW = 3456
Z_PIECES = (512, 1536, 512, 768, 128)
N_DEV = 8
LANES = 128
PACK_ROWS = 256
VMEM_LIMIT = 56 * 1024 * 1024
ELEMENTWISE_STEP_BYTES = 4 * 1024 * 1024
TB = 256
SG_TB = 512
TB_DW = 512

ADAM_LR = 0.001
ADAM_B1 = 0.9
ADAM_B2 = 0.999
ADAM_EPS = 1e-08
ADAM_WD = 0.01
ADAM_STEP = 10

MIX_HEAD_START = 3
S5_L = 128
S5_GROUP = 8
S5_SHIFTS = (1, 2, 4)

WEIGHTS = ['norm_g', 'w_in', 'ssm_a_re', 'ssm_a_im', 'ssm_b_re', 'ssm_b_im', 'ssm_c_re', 'ssm_c_im', 'ssm_d',
           'ssm_log_step', 'ssm_w_glu', 'ssm_b_glu', 'dn_conv_w', 'dn_a_log', 'dn_dt_bias', 'dn_norm_g', 'sg_ln_g',
           'sg_ln_b', 'sg_w', 'sg_b', 'w_out', 'ple_norm_g', 'w_ple_gate', 'w_ple', 'final_norm_g']
SHARDED = {'w_in': 2, 'ssm_w_glu': 1, 'dn_conv_w': 2, 'w_out': 1, 'w_ple_gate': 1, 'w_ple': 2}
SHARDED_ORDER = ['w_in', 'ssm_w_glu', 'dn_conv_w', 'w_out', 'w_ple_gate', 'w_ple']
WIRE = {'w_in': BF16, 'ssm_w_glu': BF16, 'dn_conv_w': F32, 'w_out': BF16, 'w_ple_gate': BF16, 'w_ple': BF16,
        'replicated': BF16}
REPLICATED_ORDER = [n for n in WEIGHTS if n not in SHARDED]
REP_NARROW = [n for n in REPLICATED_ORDER if n != 'final_norm_g']


def _cp(*sem):
    return pltpu.CompilerParams(dimension_semantics=sem, vmem_limit_bytes=VMEM_LIMIT)


def _dg(a, b, ca, cb, precision=None):
    return lax.dot_general(a, b, (((ca,), (cb,)), ((), ())), precision=precision, preferred_element_type=F32)


@jax.custom_vjp
def _mm(a, b):
    return _dg(a.astype(BF16), b.astype(BF16), 1, 0)


def _mm_fwd(a, b):
    return _mm(a, b), (a, b)


def _mm_bwd(res, g):
    a, b = res
    gb = g.astype(BF16)
    return _dg(gb, b.astype(BF16), 1, 1), _dg(a.astype(BF16), gb, 0, 0)


_mm.defvjp(_mm_fwd, _mm_bwd)


@jax.custom_vjp
def _mm_nt(a, b):
    return _dg(a.astype(BF16), b.astype(BF16), 1, 1)


def _mm_nt_fwd(a, b):
    return _mm_nt(a, b), (a, b)


def _mm_nt_bwd(res, g):
    a, b = res
    gb = g.astype(BF16)
    return _dg(gb, b.astype(BF16), 1, 0), _dg(gb, a.astype(BF16), 0, 0)


_mm_nt.defvjp(_mm_nt_fwd, _mm_nt_bwd)


@jax.custom_vjp
def _mm_tn(a, b):
    return _dg(a.astype(BF16), b.astype(BF16), 0, 0)


def _mm_tn_fwd(a, b):
    return _mm_tn(a, b), (a, b)


def _mm_tn_bwd(res, g):
    a, b = res
    gb = g.astype(BF16)
    return _dg(b.astype(BF16), gb, 1, 1), _dg(a.astype(BF16), gb, 1, 0)


_mm_tn.defvjp(_mm_tn_fwd, _mm_tn_bwd)


def _split(x, n):
    pieces = []
    for _ in range(n - 1):
        hi = x.astype(BF16)
        pieces.append(hi)
        x = x - hi.astype(F32)
    pieces.append(x.astype(BF16))
    return pieces


def _dg3(a, b, ca, cb):
    a_hi, a_lo = _split(a, 2)
    b_hi, b_lo = _split(b, 2)
    return _dg(a_hi, b_hi, ca, cb) + (_dg(a_hi, b_lo, ca, cb) + _dg(a_lo, b_hi, ca, cb))


@jax.custom_vjp
def _dot3(a, b):
    return _dg3(a, b, 1, 0)


def _dot3_fwd(a, b):
    return _dot3(a, b), (a, b)


def _dot3_bwd(res, g):
    a, b = res
    return _dg3(g, b, 1, 1), _dg3(a, g, 0, 0)


_dot3.defvjp(_dot3_fwd, _dot3_bwd)


def _dg_sel(x, e, cx, ce, x_first):
    eb = e.astype(BF16)
    out = None
    for piece in reversed(_split(x, 3)):
        term = _dg(piece, eb, cx, ce) if x_first else _dg(eb, piece, ce, cx)
        out = term if out is None else out + term
    return out


@jax.custom_vjp
def _sel_r(x, e):
    return _dg_sel(x, e, 1, 0, True)


def _sel_r_fwd(x, e):
    return _sel_r(x, e), e


def _sel_r_bwd(e, g):
    return _dg_sel(g, e, 1, 1, True), jnp.zeros_like(e)


_sel_r.defvjp(_sel_r_fwd, _sel_r_bwd)


@jax.custom_vjp
def _sel_l(e, x):
    return _dg_sel(x, e, 0, 1, False)


def _sel_l_fwd(e, x):
    return _sel_l(e, x), e


def _sel_l_bwd(e, g):
    return jnp.zeros_like(e), _dg_sel(g, e, 0, 0, False)


_sel_l.defvjp(_sel_l_fwd, _sel_l_bwd)


def _rms(x, g):
    return x * lax.rsqrt(jnp.mean(x * x, axis=-1, keepdims=True) + EPS) * g


def _silu(x):
    return x * jax.nn.sigmoid(x)


Z_OFFSETS = (0, 512, 2048, 2560, 3328)


def _dn_post(c):
    s = _silu(c)
    parts = []
    for j in range(12):
        xj = s[:, j * DH:(j + 1) * DH]
        if j < 8:
            xj = xj * lax.rsqrt(jnp.sum(xj * xj, axis=-1, keepdims=True) + EPS)
        if j < 4:
            xj = xj * (DH ** -0.5)
        parts.append(xj)
    return jnp.concatenate(parts, axis=1)


def _dn_conv(ext, cw_ref, rows):
    c = None
    for k in range(4):
        sh = ext if k == 3 else pltpu.roll(ext, 3 - k, 0)
        term = cw_ref[k:k + 1, :] * sh[ext.shape[0] - rows:, :]
        c = term if c is None else c + term
    return c


def _dn_prep_vjp(prev, cur, nxt, d_cur, d_nxt, cw_ref, tb):
    ext = jnp.concatenate([prev, cur, nxt], axis=0)
    shifted = [ext if k == 3 else pltpu.roll(ext, 3 - k, 0) for k in range(4)]
    c2 = None
    for k in range(4):
        term = cw_ref[k:k + 1, :] * shifted[k][8:, :]
        c2 = term if c2 is None else c2 + term
    _, vjp = jax.vjp(_dn_post, c2)
    (dc2,) = vjp(jnp.concatenate([d_cur, d_nxt], axis=0))
    dz, dcw = None, []
    for k in range(4):
        up = dc2 if k == 3 else pltpu.roll(dc2, tb + 8 - (3 - k), 0)
        term = cw_ref[k:k + 1, :] * up[:tb, :]
        dz = term if dz is None else dz + term
        dcw.append(jnp.sum(dc2[:tb, :] * shifted[k][8:8 + tb, :], axis=0, keepdims=True))
    return dz, dcw


def _in_proj_fwd(x, g, w, conv_w8, s, tb, name):
    t = x.shape[0]
    n_s = s // tb
    w3 = 3 * D_DN
    q0, q1 = Z_OFFSETS[1], Z_OFFSETS[2]

    def body(x_ref, g_ref, w_ref, cw_ref, zs_ref, zq_ref, zg_ref, zsg_ref, zab_ref, qkvn_ref, halo):
        h = _rms(x_ref[...], g_ref[...]).astype(BF16)
        zq = jnp.dot(h, w_ref[:, q0:q1], preferred_element_type=F32)
        zq_ref[...] = zq
        prev = jnp.where(pl.program_id(0) % n_s == 0, 0.0, halo[...])
        qkvn_ref[...] = _dn_post(_dn_conv(jnp.concatenate([prev, zq], axis=0), cw_ref, tb))
        halo[...] = zq[tb - 8:, :]
        zs_ref[...] = jnp.dot(h, w_ref[:, :q0], preferred_element_type=F32)
        rest = jnp.dot(h, w_ref[:, q1:], preferred_element_type=F32)
        zg_ref[...] = rest[:, :Z_PIECES[2]]
        zsg_ref[...] = rest[:, Z_PIECES[2]:Z_PIECES[2] + Z_PIECES[3]]
        zab_ref[...] = rest[:, Z_PIECES[2] + Z_PIECES[3]:]

    row = lambda i: (i, 0)
    full = lambda i: (0, 0)
    widths = Z_PIECES + (w3,)
    return pl.pallas_call(
        body, grid=(t // tb,),
        in_specs=[pl.BlockSpec((tb, D), row), pl.BlockSpec((1, D), full), pl.BlockSpec((D, ZW), full), pl.BlockSpec((8, w3), full)],
        out_specs=[pl.BlockSpec((tb, n), row) for n in widths],
        out_shape=[jax.ShapeDtypeStruct((t, n), F32) for n in widths],
        scratch_shapes=[pltpu.VMEM((8, w3), F32)],
        name=name, compiler_params=_cp("arbitrary"))(x, g, w, conv_w8)


def _in_proj_bwd_dx(x, g, w, conv_w8, dz_ssm, dz_gdn, dz_sg, dz_ab, zq, dqkvn, dx_res, s, tb, name):
    t = x.shape[0]
    n_s = s // tb
    hb = tb // 8
    w3 = 3 * D_DN
    q0, q1 = Z_OFFSETS[1], Z_OFFSETS[2]

    def body(x_ref, g_ref, w_ref, cw_ref, ds_ref, dgd_ref, dsg_ref, dab_ref, cur_ref, prev_ref, next_ref, dq_ref, dqn_ref,
             dxr_ref, dx_ref, dg_ref, dzq_ref, dcw_ref):
        i = pl.program_id(0)

        @pl.when(i == 0)
        def _():
            dg_ref[...] = jnp.zeros_like(dg_ref)
            dcw_ref[...] = jnp.zeros_like(dcw_ref)

        rest = jnp.concatenate([dgd_ref[...], dsg_ref[...], dab_ref[...]], axis=1)
        dh = _dg(ds_ref[...], w_ref[:, :q0], 1, 1) + _dg(rest, w_ref[:, q1:], 1, 1)
        first, last = i % n_s == 0, i % n_s == n_s - 1
        dzq, dcw = _dn_prep_vjp(jnp.where(first, 0.0, prev_ref[...]), cur_ref[...], jnp.where(last, 0.0, next_ref[...]),
                                dq_ref[...], jnp.where(last, 0.0, dqn_ref[...]), cw_ref, tb)
        for k in range(4):
            dcw_ref[k:k + 1, :] += dcw[k]
        dzq = dzq.astype(BF16)
        dzq_ref[...] = dzq
        dh = dh + _dg(dzq, w_ref[:, q0:q1], 1, 1)
        _, vjp = jax.vjp(_rms, x_ref[...], g_ref[...])
        dx, dg = vjp(dh)
        dx_ref[...] = dx + dxr_ref[...]
        dg_ref[...] += dg

    n_blk8 = t // 8
    row = lambda i: (i, 0)
    prv = lambda i: (jnp.maximum(i * hb - 1, 0), 0)
    nxt = lambda i: (jnp.minimum((i + 1) * hb, n_blk8 - 1), 0)
    full = lambda i: (0, 0)
    return pl.pallas_call(
        body, grid=(t // tb,),
        in_specs=[pl.BlockSpec((tb, D), row), pl.BlockSpec((1, D), full), pl.BlockSpec((D, ZW), full), pl.BlockSpec((8, w3), full)]
        + [pl.BlockSpec((tb, n), row) for n in (Z_PIECES[0], Z_PIECES[2], Z_PIECES[3], Z_PIECES[4])]
        + [pl.BlockSpec((tb, w3), row), pl.BlockSpec((8, w3), prv), pl.BlockSpec((8, w3), nxt),
           pl.BlockSpec((tb, w3), row), pl.BlockSpec((8, w3), nxt), pl.BlockSpec((tb, D), row)],
        out_specs=[pl.BlockSpec((tb, D), row), pl.BlockSpec((1, D), full), pl.BlockSpec((tb, w3), row), pl.BlockSpec((8, w3), full)],
        out_shape=[jax.ShapeDtypeStruct((t, D), F32), jax.ShapeDtypeStruct((1, D), F32), jax.ShapeDtypeStruct((t, w3), BF16),
                   jax.ShapeDtypeStruct((8, w3), F32)],
        name=name, compiler_params=_cp("arbitrary"))(x, g, w, conv_w8, dz_ssm, dz_gdn, dz_sg, dz_ab, zq, zq, zq, dqkvn, dqkvn, dx_res)


def _in_proj_bwd_dw(x, g, dzs, tb, name):
    t = x.shape[0]

    def body(x_ref, g_ref, d0, d1, d2, d3, d4, *dw_refs):
        @pl.when(pl.program_id(0) == 0)
        def _():
            for r in dw_refs:
                r[...] = jnp.zeros_like(r)

        h = _rms(x_ref[...], g_ref[...]).astype(BF16)
        for d_ref, dw_ref in zip((d0, d1, d2, d3, d4), dw_refs):
            dw_ref[...] += _dg(h, d_ref[...].astype(BF16), 0, 0)

    row = lambda i: (i, 0)
    full = lambda i: (0, 0)
    return pl.pallas_call(
        body, grid=(t // tb,),
        in_specs=[pl.BlockSpec((tb, D), row), pl.BlockSpec((1, D), full)] + [pl.BlockSpec((tb, n), row) for n in Z_PIECES],
        out_specs=[pl.BlockSpec((D, n), full) for n in Z_PIECES],
        out_shape=[jax.ShapeDtypeStruct((D, n), F32) for n in Z_PIECES],
        name=name, compiler_params=_cp("arbitrary"))(x, g, *dzs)


def _lam_pow(a_re, a_im, log_step, k):
    step = jnp.exp(log_step)[:, None]
    mag = jnp.exp(k * a_re * step)
    ang = k * a_im * step
    return mag * jnp.cos(ang), mag * jnp.sin(ang)


def _s5_powers(a_re, a_im, log_step):
    def table(ks):
        re, im = _lam_pow(a_re, a_im, log_step, jnp.asarray(ks, F32)[:, None, None])
        return jnp.concatenate([re.reshape(len(ks), NRE), im.reshape(len(ks), NRE)], axis=-1)

    ld = table(S5_SHIFTS).reshape(len(S5_SHIFTS), 1, 2 * NRE)
    return ld, table(range(1, S5_GROUP + 1)), table(range(S5_GROUP, 0, -1))


def _s5_tables(a_re, a_im, b_re, b_im, c_re, c_im, d_skip, log_step):
    lam_re, lam_im = _lam_pow(a_re, a_im, log_step, 1.0)
    den = a_re * a_re + a_im * a_im
    nr, ni = lam_re - 1.0, lam_im
    f_re = (nr * a_re + ni * a_im) / den
    f_im = (ni * a_re - nr * a_im) / den
    bbar_re = f_re[..., None] * b_re - f_im[..., None] * b_im
    bbar_im = f_re[..., None] * b_im + f_im[..., None] * b_re
    eye = jnp.eye(G, dtype=F32)

    def blk_b(bb):
        return (jnp.transpose(bb, (0, 2, 1))[:, :, None, :] * eye[:, None, :, None]).reshape(D_SSM, NRE)

    def blk_c(cc):
        return (jnp.transpose(cc, (0, 2, 1))[:, :, None, :] * eye[:, None, :, None]).reshape(NRE, D_SSM)

    b_blk = jnp.concatenate([blk_b(bbar_re), blk_b(bbar_im)], axis=1)
    c_blk = jnp.concatenate([blk_c(c_re), -blk_c(c_im)], axis=0)
    lam = jnp.concatenate([lam_re.reshape(1, NRE), lam_im.reshape(1, NRE)], axis=-1)
    return b_blk, c_blk, lam, d_skip.reshape(1, D_SSM)


def _group_shift(x, d, up=False):
    r = lax.broadcasted_iota(jnp.int32, x.shape, 0) & (S5_GROUP - 1)
    if up:
        return jnp.where(r < S5_GROUP - d, pltpu.roll(x, x.shape[0] - d, 0), 0.0)
    return jnp.where(r >= d, pltpu.roll(x, d, 0), 0.0)


def _s5_scan_steps(hr, hi, cr, ci, lds, lp):
    for ld, d in zip(lds, S5_SHIFTS):
        lr, li = ld[:, :NRE], ld[:, NRE:]
        sr, si = _group_shift(hr, d), _group_shift(hi, d)
        hr, hi = hr + lr * sr - li * si, hi + lr * si + li * sr
        yield
    pr, pi = lp[:, :NRE], lp[:, NRE:]
    rows_r, rows_i = [], []
    for r in range(hr.shape[0] // S5_GROUP):
        br, bi = hr[r * S5_GROUP:(r + 1) * S5_GROUP], hi[r * S5_GROUP:(r + 1) * S5_GROUP]
        br, bi = br + pr * cr - pi * ci, bi + pr * ci + pi * cr
        cr, ci = br[S5_GROUP - 1:S5_GROUP], bi[S5_GROUP - 1:S5_GROUP]
        rows_r.append(br)
        rows_i.append(bi)
        if r % 2:
            yield
    return jnp.concatenate(rows_r, axis=0), jnp.concatenate(rows_i, axis=0)


@jax.custom_vjp
def _known_scan(xr, xi, cr, ci, lam, lds, lp_rev, hr, hi):
    return hr, hi


def _known_scan_fwd(xr, xi, cr, ci, lam, lds, lp_rev, hr, hi):
    return (hr, hi), (cr, ci, lam, lds, lp_rev, hr, hi)


def _known_scan_bwd(res, cts):
    cr, ci, lam, lds, lp_rev, hr, hi = res
    ar, ai = cts
    for ld, d in zip(lds, S5_SHIFTS):
        lr, li = ld[:, :NRE], ld[:, NRE:]
        sr, si = _group_shift(ar, d, up=True), _group_shift(ai, d, up=True)
        ar, ai = ar + lr * sr + li * si, ai + lr * si - li * sr
    qr, qi = lp_rev[:, :NRE], lp_rev[:, NRE:]
    nr, ni = jnp.zeros_like(cr), jnp.zeros_like(ci)
    rows_r, rows_i = [], []
    for r in reversed(range(hr.shape[0] // S5_GROUP)):
        br, bi = ar[r * S5_GROUP:(r + 1) * S5_GROUP], ai[r * S5_GROUP:(r + 1) * S5_GROUP]
        br, bi = br + qr * nr + qi * ni, bi + qr * ni - qi * nr
        nr, ni = br[0:1], bi[0:1]
        rows_r.insert(0, br)
        rows_i.insert(0, bi)
    ar, ai = jnp.concatenate(rows_r, axis=0), jnp.concatenate(rows_i, axis=0)
    lr, li = lam[:, :NRE], lam[:, NRE:]
    dcr, dci = lr * nr + li * ni, lr * ni - li * nr
    first = lax.broadcasted_iota(jnp.int32, hr.shape, 0) == 0
    pr = jnp.where(first, cr, pltpu.roll(hr, 1, 0))
    pi = jnp.where(first, ci, pltpu.roll(hi, 1, 0))
    dlam = jnp.concatenate([jnp.sum(ar * pr + ai * pi, axis=0, keepdims=True),
                            jnp.sum(ai * pr - ar * pi, axis=0, keepdims=True)], axis=1)
    return (ar, ai, dcr, dci, dlam, [jnp.zeros_like(ld) for ld in lds], jnp.zeros_like(lp_rev),
            jnp.zeros_like(hr), jnp.zeros_like(hi))


_known_scan.defvjp(_known_scan_fwd, _known_scan_bwd)


def _interleave(short, long, head_start=0):
    gens = list(short) + list(long)
    results = [None] * len(gens)

    def advance(live):
        still = []
        for idx, gen in live:
            try:
                next(gen)
                still.append((idx, gen))
            except StopIteration as done:
                results[idx] = done.value
        return still

    live_short = advance(list(enumerate(gens))[:len(short)])
    live_long = list(enumerate(gens))[len(short):]
    for _ in range(head_start):
        live_long = advance(live_long)
    live = live_short + live_long
    while live:
        live = advance(live)
    return results[:len(short)], results[len(short):]


def _s5_chunk_gen(u, gate, cr, ci, b_blk, c_blk, lam, dv, wglu, bglu, lds, lp, lp_rev, known_h=None):
    bu = _mm(u, b_blk)
    xr, xi = bu[:, :NRE], bu[:, NRE:]
    yield
    if known_h is None:
        hr, hi = yield from _s5_scan_steps(xr, xi, cr, ci, lds, lp)
    else:
        hr, hi = _known_scan(xr, xi, cr, ci, lam, lds, lp_rev, *known_h)
    y = _mm(jnp.concatenate([hr, hi], axis=1), c_blk) + dv * u
    yield
    y = jax.nn.gelu(y)
    y = y * jax.nn.sigmoid(_mm(y, wglu) + bglu)
    return y * _silu(gate), hr, hi


S5_PAR_SHAPES = [(D_SSM, 2 * NRE), (2 * NRE, D_SSM), (1, 2 * NRE), (1, D_SSM), (D_SSM, D_SSM), (1, D_SSM)]
S5_CONST_SHAPES = [(len(S5_SHIFTS), 1, 2 * NRE), (S5_GROUP, 2 * NRE), (S5_GROUP, 2 * NRE)]
DN_PAR_SHAPES = [(1, D_DN), (1, D_DN), (1, DH)]


def _mix_specs(bl, n_c, rev):
    def chunk(i):
        return n_c - 1 - i if rev else i

    def tok(n):
        return pl.BlockSpec((bl, DN_C, n), lambda i: (0, chunk(i), 0))

    def per_chunk(shape):
        return pl.BlockSpec((bl, 1, *shape), lambda i: (0, chunk(i)) + (0,) * len(shape))

    def whole(shape):
        return pl.BlockSpec(shape, lambda i: (0,) * len(shape))

    return tok, per_chunk, whole


def _unit_lower_inverse_steps(ms):
    c_len = ms[0].shape[0]
    eye = lax.broadcasted_iota(jnp.int32, (c_len, c_len), 0) == lax.broadcasted_iota(jnp.int32, (c_len, c_len), 1)
    ident = jnp.where(eye, 1.0, 0.0)
    ps = ms
    tinvs = [ident - m for m in ms]
    for _ in range(c_len.bit_length() - 2):
        ps = [_dg3(p, p, 1, 0) for p in ps]
        yield
        tinvs = [t + _dg3(t, p, 1, 0) for t, p in zip(tinvs, ps)]
        yield
    return tinvs


@jax.custom_vjp
def _known_inverses(ms, tinvs):
    return tinvs


def _known_inverses_fwd(ms, tinvs):
    return tinvs, tinvs


def _known_inverses_bwd(tinvs, gs):
    return [-_dg3(_dg3(t, g, 0, 0), t, 1, 1) for t, g in zip(tinvs, gs)], [jnp.zeros_like(t) for t in tinvs]


_known_inverses.defvjp(_known_inverses_fwd, _known_inverses_bwd)


def _dn_chunk_gen(qkv, zab, zg, states, alog_e, dt_e, ng, known_tinvs=None):
    c_len = DN_C
    r = lax.broadcasted_iota(jnp.int32, (c_len, c_len), 0)
    c = lax.broadcasted_iota(jnp.int32, (c_len, c_len), 1)
    causal, strict = r >= c, r > c
    tril = jnp.where(causal, 1.0, 0.0)
    rr = lax.broadcasted_iota(jnp.int32, (LANES, D_DN), 0)
    cc = lax.broadcasted_iota(jnp.int32, (LANES, D_DN), 1)
    e_a = jnp.where((cc >= rr * DH) & (cc < rr * DH + DH) & (rr < H), 1.0, 0.0)
    e_b = jnp.where((cc >= (rr - H) * DH) & (cc < (rr - H) * DH + DH) & (rr >= H) & (rr < 2 * H), 1.0, 0.0)
    a_e = _sel_r(zab, e_a)
    b_e = _sel_r(zab, e_b)
    beta = jax.nn.sigmoid(b_e)
    g = -jnp.exp(alog_e) * jax.nn.softplus(a_e + dt_e)
    yield
    gc = _sel_l(tril, g)
    glast = jnp.sum(g, axis=0, keepdims=True)
    eg = jnp.exp(gc)
    ekd = jnp.exp(glast - gc)
    dl = jnp.exp(glast)
    yield
    heads = range(H)
    sls = [slice(h * DH, (h + 1) * DH) for h in heads]
    qs = [qkv[:, h * DH:(h + 1) * DH] for h in heads]
    ks = [qkv[:, D_DN + h * DH:D_DN + (h + 1) * DH] for h in heads]
    vs = [qkv[:, 2 * D_DN + h * DH:2 * D_DN + (h + 1) * DH] for h in heads]
    ccols = [gc[:, sl] for sl in sls]
    decs = [jnp.where(causal, jnp.exp(jnp.where(causal, cl - jnp.transpose(cl), 0.0)), 0.0) for cl in ccols]
    kbs = [k * beta[:, sl] for k, sl in zip(ks, sls)]
    ms = [jnp.where(strict, _mm_nt(kb, k) * dec, 0.0) for kb, k, dec in zip(kbs, ks, decs)]
    yield
    if known_tinvs is None:
        tinvs = yield from _unit_lower_inverse_steps(ms)
    else:
        tinvs = _known_inverses(ms, list(known_tinvs))
    sols = [_dot3(t, jnp.concatenate([v * beta[:, sl], kb * eg[:, sl]], axis=1))
            for t, v, kb, sl in zip(tinvs, vs, kbs, sls)]
    yield
    atts = [_mm_nt(q, k) * dec for q, k, dec in zip(qs, ks, decs)]
    vnews = [sol[:, :DH] - _mm(sol[:, DH:], st) for sol, st in zip(sols, states)]
    yield
    os_ = [_mm(q * eg[:, sl], st) + _mm(att, vn) for q, sl, st, att, vn in zip(qs, sls, states, atts, vnews)]
    yield
    new_states = [st * dl[:, sl] + _mm_tn(k * ekd[:, sl], vn) for st, sl, k, vn in zip(states, sls, ks, vnews)]
    yield
    ys = [_rms(o, ng) * _silu(zg[:, sl]) for o, sl in zip(os_, sls)]
    return jnp.concatenate(ys, axis=1), new_states, tinvs


def _mix_fwd(zs, s5_par, s5_const, qkv, zab, zg, dn_par, bl, s, name, gather=()):
    assert S5_L == DN_C
    n_c = s // DN_C
    nd = len(S5_SHIFTS)
    m = len(gather)
    tok, per_chunk, whole = _mix_specs(bl, n_c, False)

    def body(*refs):
        (z_ref, b_ref, c_ref, lam_ref, dv_ref, wg_ref, bg_ref, ld_ref, lp_ref, lpr_ref,
         q_ref, ab_ref, zg_ref, al_ref, dt_ref, ng_ref) = refs[:16]
        ys_ref, car_ref, h_ref, yd_ref, st_ref, ti_ref = refs[16 + m:22 + m]
        cs, ssc = refs[22 + 2 * m:24 + 2 * m]
        gathering = (_gather_protocol(_mesh_place(), refs[16:16 + m], refs[22 + m:22 + 2 * m], *refs[24 + 2 * m:])
                     if m else None)

        @pl.when(pl.program_id(0) == 0)
        def _():
            cs[...] = jnp.zeros_like(cs)
            ssc[...] = jnp.zeros_like(ssc)
            if m:
                next(gathering)

        lds = [ld_ref[k] for k in range(nd)]
        s5_gens, dn_gens = [], []
        for e in range(bl):
            c = cs[e]
            car_ref[e, 0] = c
            sts = [ssc[e, h] for h in range(H)]
            for h in range(H):
                st_ref[e, 0, h] = sts[h]
            z = z_ref[e]
            s5_gens.append(_s5_chunk_gen(z[:, :D_SSM], z[:, D_SSM:], c[:, :NRE], c[:, NRE:], b_ref[...], c_ref[...], lam_ref[...],
                                         dv_ref[...], wg_ref[...], bg_ref[...], lds, lp_ref[...], lpr_ref[...]))
            dn_gens.append(_dn_chunk_gen(q_ref[e], ab_ref[e], zg_ref[e], sts, al_ref[...], dt_ref[...], ng_ref[...]))
        s5_outs, dn_outs = _interleave(s5_gens, dn_gens, head_start=MIX_HEAD_START)
        for e in range(bl):
            y_s, hr, hi = s5_outs[e]
            y_d, new_sts, tinvs = dn_outs[e]
            ys_ref[e] = y_s
            h_ref[e, :, :NRE] = hr
            h_ref[e, :, NRE:] = hi
            cs[e, :, :NRE] = hr[S5_L - 1:S5_L]
            cs[e, :, NRE:] = hi[S5_L - 1:S5_L]
            yd_ref[e] = y_d
            for h in range(H):
                ssc[e, h] = new_sts[h]
                ti_ref[e, 0, h] = tinvs[h]

        if m:
            @pl.when(pl.program_id(0) == n_c - 1)
            def _():
                for _ in gathering:
                    pass

    head_mats = jax.ShapeDtypeStruct((bl, n_c, H, DH, DH), F32)
    outs = pl.pallas_call(
        body, grid=(n_c,),
        in_specs=[tok(2 * D_SSM)] + [whole(sh) for sh in S5_PAR_SHAPES + S5_CONST_SHAPES]
        + [tok(3 * D_DN), tok(LANES), tok(D_DN)] + [whole(sh) for sh in DN_PAR_SHAPES] + _hbm_specs(m),
        out_specs=[tok(D_SSM), per_chunk((1, 2 * NRE)), tok(2 * NRE), tok(D_DN), per_chunk((H, DH, DH)), per_chunk((H, DH, DH))]
        + _hbm_specs(m),
        out_shape=[jax.ShapeDtypeStruct((bl, s, D_SSM), F32), jax.ShapeDtypeStruct((bl, n_c, 1, 2 * NRE), F32),
                   jax.ShapeDtypeStruct((bl, s, 2 * NRE), F32), jax.ShapeDtypeStruct((bl, s, D_DN), F32), head_mats, head_mats]
        + [jax.ShapeDtypeStruct((N_DEV, *b.shape), b.dtype) for b in gather],
        scratch_shapes=[pltpu.VMEM((bl, 1, 2 * NRE), F32), pltpu.VMEM((bl, H, DH, DH), F32)] + (_gather_sems(m) if m else []),
        name=name, compiler_params=_cp("arbitrary"))(zs, *s5_par, *s5_const, qkv, zab, zg, *dn_par, *gather)
    return outs[:6], outs[6:]


def _mix_bwd(zs, carries, h_all, dy_s, s5_par, s5_const, qkv, zab, zg, states, tinvs, dy_d, dn_par, bl, s, name, exchange=()):
    n_c = s // DN_C
    nd = len(S5_SHIFTS)
    k_ex = len(exchange)
    tok, per_chunk, whole = _mix_specs(bl, n_c, True)

    def both(examples, s5_tabs, s5_consts, dn_tabs):
        s5_gens = [_s5_chunk_gen(u, gate, cr, ci, *s5_tabs, *s5_consts, known_h=(hr, hi))
                   for u, gate, cr, ci, hr, hi, _, _, _, _, _ in examples]
        dn_gens = [_dn_chunk_gen(q, ab, zgate, sts, *dn_tabs, known_tinvs=known)
                   for _, _, _, _, _, _, q, ab, zgate, sts, known in examples]
        s5_outs, dn_outs = _interleave(s5_gens, dn_gens, head_start=MIX_HEAD_START)
        return [(y_s, hr[S5_L - 1:S5_L], hi[S5_L - 1:S5_L], y_d, new_sts)
                for (y_s, hr, hi), (y_d, new_sts, _) in zip(s5_outs, dn_outs)]

    def body(*refs):
        (z_ref, car_ref, h_ref, dys_ref, b_ref, c_ref, lam_ref, dv_ref, wg_ref, bg_ref, ld_ref, lp_ref, lpr_ref,
         q_ref, ab_ref, zg_ref, st_ref, ti_ref, dyd_ref, al_ref, dt_ref, ng_ref) = refs[:22]
        (dz_ref, db_ref, dc_ref, dlam_ref, ddv_ref, dwg_ref, dbg_ref,
         dq_ref, dab_ref, dzg_ref, dal_ref, ddt_ref, dng_ref) = refs[22 + k_ex:35 + k_ex]
        dcs, dsc = refs[35 + 2 * k_ex:37 + 2 * k_ex]
        accs = (db_ref, dc_ref, dlam_ref, ddv_ref, dwg_ref, dbg_ref, dal_ref, ddt_ref, dng_ref)
        exchanging = (_chip_protocol(_mesh_place(), refs[22:22 + k_ex], refs[35 + k_ex:35 + 2 * k_ex], *refs[37 + 2 * k_ex:])
                      if k_ex else None)

        @pl.when(pl.program_id(0) == 0)
        def _():
            for r in accs + (dcs, dsc):
                r[...] = jnp.zeros_like(r)
            if k_ex:
                next(exchanging)

        examples = []
        for e in range(bl):
            z = z_ref[e]
            c = car_ref[e, 0]
            examples.append((z[:, :D_SSM], z[:, D_SSM:], c[:, :NRE], c[:, NRE:], h_ref[e, :, :NRE], h_ref[e, :, NRE:],
                             q_ref[e], ab_ref[e], zg_ref[e], [st_ref[e, 0, h] for h in range(H)],
                             [ti_ref[e, 0, h] for h in range(H)]))
        _, vjp = jax.vjp(both, examples,
                         (b_ref[...], c_ref[...], lam_ref[...], dv_ref[...], wg_ref[...], bg_ref[...]),
                         ([ld_ref[k] for k in range(nd)], lp_ref[...], lpr_ref[...]),
                         (al_ref[...], dt_ref[...], ng_ref[...]))
        cts = []
        for e in range(bl):
            dc = dcs[e]
            cts.append((dys_ref[e], dc[:, :NRE], dc[:, NRE:], dyd_ref[e], [dsc[e, h] for h in range(H)]))
        d_examples, d_s5, _, d_dn = vjp(cts)
        for e in range(bl):
            du, dgate, dcr, dci, _, _, dq, dab, dzg, dsts, _ = d_examples[e]
            dz_ref[e] = jnp.concatenate([du, dgate], axis=1).astype(BF16)
            dcs[e, :, :NRE] = dcr
            dcs[e, :, NRE:] = dci
            dq_ref[e] = dq
            dab_ref[e] = dab.astype(BF16)
            dzg_ref[e] = dzg.astype(BF16)
            for h in range(H):
                dsc[e, h] = dsts[h]
        for r, ct in zip(accs, (*d_s5, *d_dn)):
            r[...] += ct

        if k_ex:
            @pl.when(pl.program_id(0) == n_c - 1)
            def _():
                for _ in exchanging:
                    pass

    head_mats = per_chunk((H, DH, DH))
    outs = pl.pallas_call(
        body, grid=(n_c,),
        in_specs=[tok(2 * D_SSM), per_chunk((1, 2 * NRE)), tok(2 * NRE), tok(D_SSM)]
        + [whole(sh) for sh in S5_PAR_SHAPES + S5_CONST_SHAPES]
        + [tok(3 * D_DN), tok(LANES), tok(D_DN), head_mats, head_mats, tok(D_DN)] + [whole(sh) for sh in DN_PAR_SHAPES]
        + _hbm_specs(k_ex),
        out_specs=[tok(2 * D_SSM)] + [whole(sh) for sh in S5_PAR_SHAPES]
        + [tok(3 * D_DN), tok(LANES), tok(D_DN)] + [whole(sh) for sh in DN_PAR_SHAPES] + _hbm_specs(k_ex),
        out_shape=[jax.ShapeDtypeStruct((bl, s, 2 * D_SSM), BF16)] + [jax.ShapeDtypeStruct(sh, F32) for sh in S5_PAR_SHAPES]
        + [jax.ShapeDtypeStruct((bl, s, 3 * D_DN), F32), jax.ShapeDtypeStruct((bl, s, LANES), BF16),
           jax.ShapeDtypeStruct((bl, s, D_DN), BF16)]
        + [jax.ShapeDtypeStruct(sh, F32) for sh in DN_PAR_SHAPES]
        + [jax.ShapeDtypeStruct(q.shape, q.dtype) for q in exchange],
        scratch_shapes=[pltpu.VMEM((bl, 1, 2 * NRE), F32), pltpu.VMEM((bl, H, DH, DH), F32)] + (_chip_sems(k_ex) if k_ex else []),
        name=name, compiler_params=_cp("arbitrary"))(
            zs, carries, h_all, dy_s, *s5_par, *s5_const, qkv, zab, zg, states, tinvs, dy_d, *dn_par, *exchange)
    return outs[:7], outs[7:13], outs[13:]


def _sg_fn(n_chunk):
    def f(z, lng, lnb, w, bsp_t):
        u = jax.nn.gelu(z[:, :D_SG])
        v = jax.nn.gelu(z[:, D_SG:2 * D_SG])
        gate = z[:, 2 * D_SG:]
        xc = v - jnp.mean(v, axis=-1, keepdims=True)
        vn = xc * lax.rsqrt(jnp.mean(xc * xc, axis=-1, keepdims=True) + EPS) * lng + lnb
        r = lax.broadcasted_iota(jnp.int32, (SG_C, SG_C), 0)
        c = lax.broadcasted_iota(jnp.int32, (SG_C, SG_C), 1)
        causal = r >= c
        first_half = c < SG_C // 2
        rr = lax.broadcasted_iota(jnp.int32, (LANES, D_SG), 0)
        cc = lax.broadcasted_iota(jnp.int32, (LANES, D_SG), 1)
        expand = jnp.where((cc >= rr * 64) & (cc < rr * 64 + 64) & (rr < 4), 1.0, 0.0)
        bias = _sel_r(bsp_t, expand)
        wm = [jnp.where(causal, w[h], 0.0) for h in range(4)]
        rows = []
        for ci in range(n_chunk):
            vc = vn[ci * SG_C:(ci + 1) * SG_C]
            pairs = []
            for pr in range(2):
                vp = vc[:, pr * LANES:(pr + 1) * LANES]
                pairs.append(jnp.where(first_half, _mm(wm[2 * pr], vp), _mm(wm[2 * pr + 1], vp)))
            rows.append(jnp.concatenate(pairs, axis=1) + bias)
        sp = jnp.concatenate(rows, axis=0) if n_chunk > 1 else rows[0]
        return u * sp * _silu(gate)

    return f


def _sg_specs():
    full = lambda i: (0, 0)
    full3 = lambda i: (0, 0, 0)
    par = [pl.BlockSpec((1, D_SG), full), pl.BlockSpec((1, D_SG), full), pl.BlockSpec((4, SG_C, SG_C), full3),
           pl.BlockSpec((SG_C, LANES), full)]
    par_shapes = [(1, D_SG), (1, D_SG), (4, SG_C, SG_C), (SG_C, LANES)]
    return par, par_shapes


def _sg_fwd(zsg, params, tb, name):
    t = zsg.shape[0]
    f = _sg_fn(tb // SG_C)
    par, _ = _sg_specs()

    def body(z_ref, g_ref, b_ref, w_ref, bs_ref, y_ref):
        y_ref[...] = f(z_ref[...], g_ref[...], b_ref[...], w_ref[...], bs_ref[...])

    row = lambda i: (i, 0)
    return pl.pallas_call(
        body, grid=(t // tb,), in_specs=[pl.BlockSpec((tb, 3 * D_SG), row)] + par,
        out_specs=pl.BlockSpec((tb, D_SG), row), out_shape=jax.ShapeDtypeStruct((t, D_SG), F32),
        name=name, compiler_params=_cp("parallel"))(zsg, *params)


def _sg_bwd(zsg, dy, params, tb, name):
    t = zsg.shape[0]
    f = _sg_fn(tb // SG_C)
    par, par_shapes = _sg_specs()

    def body(z_ref, dy_ref, g_ref, b_ref, w_ref, bs_ref, dz_ref, dg_ref, db_ref, dw_ref, dbs_ref):
        accs = (dg_ref, db_ref, dw_ref, dbs_ref)

        @pl.when(pl.program_id(0) == 0)
        def _():
            for r in accs:
                r[...] = jnp.zeros_like(r)

        _, vjp = jax.vjp(f, z_ref[...], g_ref[...], b_ref[...], w_ref[...], bs_ref[...])
        cts = vjp(dy_ref[...])
        dz_ref[...] = cts[0].astype(BF16)
        for r, ct in zip(accs, cts[1:]):
            r[...] += ct

    row = lambda i: (i, 0)
    return pl.pallas_call(
        body, grid=(t // tb,), in_specs=[pl.BlockSpec((tb, 3 * D_SG), row), pl.BlockSpec((tb, D_SG), row)] + par,
        out_specs=[pl.BlockSpec((tb, 3 * D_SG), row)] + par,
        out_shape=[jax.ShapeDtypeStruct((t, 3 * D_SG), BF16)] + [jax.ShapeDtypeStruct(sh, F32) for sh in par_shapes],
        name=name, compiler_params=_cp("arbitrary"))(zsg, dy, *params)


def _out_fwd(x, ys, p, w_out, pg, w_gate, w_ple, tb, name):
    t = x.shape[0]

    def body(x_ref, y0, y1, y2, p_ref, wo_ref, pg_ref, wg_ref, wp_ref, o_ref, x1_ref, gate_ref):
        y = jnp.concatenate([y0[...], y1[...], y2[...]], axis=1).astype(BF16)
        x1 = x_ref[...] + jnp.dot(y, wo_ref[...], preferred_element_type=F32)
        hn = _rms(x1, pg_ref[...]).astype(BF16)
        gate = jax.nn.sigmoid(jnp.dot(hn, wg_ref[...], preferred_element_type=F32))
        pp = jnp.dot(p_ref[...].astype(BF16), wp_ref[...], preferred_element_type=F32)
        o_ref[...] = x1 + gate * pp
        x1_ref[...] = x1
        gate_ref[...] = gate

    row = lambda i: (i, 0)
    full = lambda i: (0, 0)
    return pl.pallas_call(
        body, grid=(t // tb,),
        in_specs=[pl.BlockSpec((tb, D), row), pl.BlockSpec((tb, D_SSM), row), pl.BlockSpec((tb, D_DN), row),
                  pl.BlockSpec((tb, D_SG), row), pl.BlockSpec((tb, D_PLE), row), pl.BlockSpec((D, D), full),
                  pl.BlockSpec((1, D), full), pl.BlockSpec((D, D), full), pl.BlockSpec((D_PLE, D), full)],
        out_specs=[pl.BlockSpec((tb, D), row)] * 3, out_shape=[jax.ShapeDtypeStruct((t, D), F32)] * 3,
        name=name, compiler_params=_cp("parallel"))(x, *ys, p, w_out, pg, w_gate, w_ple)


def _out_bwd(x1, gate, ys, p, dx2, w_out, pg, w_gate, w_ple, tb, name):
    t = x1.shape[0]

    def body(x1_ref, gate_ref, y0, y1, y2, p_ref, d_ref, wo_ref, pg_ref, wg_ref, wp_ref,
             dx_ref, dy0, dy1, dy2, dwo_ref, dpg_ref, dwg_ref, dwp_ref):
        accs = (dwo_ref, dpg_ref, dwg_ref, dwp_ref)

        @pl.when(pl.program_id(0) == 0)
        def _():
            for r in accs:
                r[...] = jnp.zeros_like(r)

        y = jnp.concatenate([y0[...], y1[...], y2[...]], axis=1).astype(BF16)
        hn, rms_vjp = jax.vjp(_rms, x1_ref[...], pg_ref[...])
        hb = hn.astype(BF16)
        gate = gate_ref[...]
        pb = p_ref[...].astype(BF16)
        pp = jnp.dot(pb, wp_ref[...], preferred_element_type=F32)
        d2 = d_ref[...]
        dpp = (d2 * gate).astype(BF16)
        dlog = (d2 * pp * gate * (1.0 - gate)).astype(BF16)
        dwp_ref[...] += _dg(pb, dpp, 0, 0)
        dwg_ref[...] += _dg(hb, dlog, 0, 0)
        dx1_n, dpg = rms_vjp(_dg(dlog, wg_ref[...], 1, 1))
        dpg_ref[...] += dpg
        dx1 = d2 + dx1_n
        dx_ref[...] = dx1
        db = dx1.astype(BF16)
        dwo_ref[...] += _dg(y, db, 0, 0)
        dy = _dg(db, wo_ref[...], 1, 1)
        dy0[...] = dy[:, :D_SSM]
        dy1[...] = dy[:, D_SSM:D_SSM + D_DN]
        dy2[...] = dy[:, D_SSM + D_DN:]

    row = lambda i: (i, 0)
    full = lambda i: (0, 0)
    acts = [pl.BlockSpec((tb, D), row), pl.BlockSpec((tb, D_SSM), row), pl.BlockSpec((tb, D_DN), row), pl.BlockSpec((tb, D_SG), row)]
    wts = [pl.BlockSpec((D, D), full), pl.BlockSpec((1, D), full), pl.BlockSpec((D, D), full), pl.BlockSpec((D_PLE, D), full)]
    return pl.pallas_call(
        body, grid=(t // tb,),
        in_specs=[pl.BlockSpec((tb, D), row)] + acts + [pl.BlockSpec((tb, D_PLE), row), pl.BlockSpec((tb, D), row)] + wts,
        out_specs=acts + wts,
        out_shape=[jax.ShapeDtypeStruct((t, n), F32) for n in (D, D_SSM, D_DN, D_SG)]
        + [jax.ShapeDtypeStruct(sh, F32) for sh in ((D, D), (1, D), (D, D), (D_PLE, D))],
        name=name, compiler_params=_cp("arbitrary"))(x1, gate, *ys, p, dx2, w_out, pg, w_gate, w_ple)


def _loss_head(x, fg, target, tb, name):
    t = x.shape[0]

    def body(x_ref, g_ref, t_ref, dx_ref, dg_ref, loss_ref):
        @pl.when(pl.program_id(0) == 0)
        def _():
            dg_ref[...] = jnp.zeros_like(dg_ref)
            loss_ref[...] = jnp.zeros_like(loss_ref)

        y, vjp = jax.vjp(_rms, x_ref[...], g_ref[...])
        err = y - t_ref[...]
        loss_ref[...] += jnp.zeros_like(loss_ref) + 0.5 * jnp.sum(err * err) / D
        dx, dg = vjp(err / D)
        dx_ref[...] = dx
        dg_ref[...] += dg

    row = lambda i: (i, 0)
    full = lambda i: (0, 0)
    return pl.pallas_call(
        body, grid=(t // tb,),
        in_specs=[pl.BlockSpec((tb, D), row), pl.BlockSpec((1, D), full), pl.BlockSpec((tb, D), row)],
        out_specs=[pl.BlockSpec((tb, D), row), pl.BlockSpec((1, D), full), pl.BlockSpec((1, LANES), full)],
        out_shape=[jax.ShapeDtypeStruct((t, D), F32), jax.ShapeDtypeStruct((1, D), F32), jax.ShapeDtypeStruct((1, LANES), F32)],
        name=name, compiler_params=_cp("arbitrary"))(x, fg, target)


def _hbm_specs(n):
    return [pl.BlockSpec(memory_space=pl.ANY)] * n


def _gather_protocol(place, ins, outs, send_sems, recv_sems, local_sems):
    n = len(ins)
    x, y, c, other_x, other_y, other_c = place
    me, sibling = (x, y, c), (x, y, other_c)
    chips = [(other_x, y), (x, other_y), (other_x, other_y)]

    def slot(a, px, py, pc):
        return outs[a].at[4 * px + 2 * py + pc]

    def copy(a, k, blk, to, src=None):
        return pltpu.make_async_remote_copy(
            src_ref=slot(a, *blk) if src is None else src, dst_ref=slot(a, *blk),
            send_sem=send_sems.at[7 * a + k], recv_sem=recv_sems.at[7 * a + k],
            device_id=to, device_id_type=pl.DeviceIdType.MESH)

    def own_copies():
        mines = [pltpu.make_async_copy(ins[a], slot(a, *me), local_sems.at[a]) for a in range(n)]
        first = []
        for a in range(n):
            first.append(copy(a, 0, me, sibling, src=ins[a]))
            first += [copy(a, 1 + j, me, (*chip, c), src=ins[a]) for j, chip in enumerate(chips)]
        return mines, first

    mines, first = own_copies()
    for cp in mines + first:
        cp.start()
    yield
    mines, first = own_copies()
    passed = []
    for j, chip in enumerate(chips):
        for a in range(n):
            copy(a, 1 + j, (*chip, c), me).wait_recv()
            onward = copy(a, 4 + j, (*chip, c), sibling)
            onward.start()
            passed.append(onward)
    for a in range(n):
        copy(a, 0, sibling, me).wait_recv()
    for j, chip in enumerate(chips):
        for a in range(n):
            copy(a, 4 + j, (*chip, other_c), me).wait_recv()
    for cp in first + passed:
        cp.wait_send()
    for cp in mines:
        cp.wait()


def _mesh_place():
    x, y, c = lax.axis_index("x"), lax.axis_index("y"), lax.axis_index("c")
    return x, y, c, 1 - x, 1 - y, 1 - c


def _gather_sems(n):
    return [pltpu.SemaphoreType.DMA((7 * n,)), pltpu.SemaphoreType.DMA((7 * n,)), pltpu.SemaphoreType.DMA((n,))]


def _all_gather(blocks, name):
    n = len(blocks)

    def body(*refs):
        for _ in _gather_protocol(_mesh_place(), refs[:n], refs[n:2 * n], *refs[2 * n:]):
            pass

    return pl.pallas_call(
        body, out_shape=[jax.ShapeDtypeStruct((N_DEV, *b.shape), b.dtype) for b in blocks],
        in_specs=_hbm_specs(n), out_specs=_hbm_specs(n), scratch_shapes=_gather_sems(n), name=name)(*blocks)


def _pair_exchange(gs, name):
    n = len(gs)

    def body(*refs):
        ins, recvs = refs[:n], refs[n:2 * n]
        send_sems, recv_sems = refs[2 * n:]
        x, y, c = lax.axis_index("x"), lax.axis_index("y"), lax.axis_index("c")
        remote = [pltpu.make_async_remote_copy(
            src_ref=ins[a], dst_ref=recvs[a], send_sem=send_sems.at[a], recv_sem=recv_sems.at[a],
            device_id=(x, y, 1 - c), device_id_type=pl.DeviceIdType.MESH) for a in range(n)]
        for cp in remote:
            cp.start()
        for cp in remote:
            cp.wait_send()
            cp.wait_recv()

    return pl.pallas_call(
        body, out_shape=[jax.ShapeDtypeStruct(g.shape, g.dtype) for g in gs], in_specs=_hbm_specs(n), out_specs=_hbm_specs(n),
        scratch_shapes=[pltpu.SemaphoreType.DMA((n,)), pltpu.SemaphoreType.DMA((n,))],
        name=name)(*gs)


def _chip_protocol(place, ins, outs, send_sems, recv_sems, local_sems):
    n = len(ins)
    x, y, c, other_x, other_y, _ = place

    def copies():
        my_chip = 2 * x + y
        local = [pltpu.make_async_copy(ins[a].at[my_chip], outs[a].at[my_chip], local_sems.at[a]) for a in range(n)]
        remote = []
        for j in range(1, 4):
            px = other_x if j & 2 else x
            py = other_y if j & 1 else y
            for a in range(n):
                remote.append(pltpu.make_async_remote_copy(
                    src_ref=ins[a].at[2 * px + py], dst_ref=outs[a].at[my_chip],
                    send_sem=send_sems.at[3 * a + j - 1], recv_sem=recv_sems.at[3 * a + j - 1],
                    device_id=(px, py, c), device_id_type=pl.DeviceIdType.MESH))
        return local, remote

    local, remote = copies()
    for cp in local + remote:
        cp.start()
    yield
    local, remote = copies()
    for cp in remote:
        cp.wait_send()
        cp.wait_recv()
    for cp in local:
        cp.wait()


def _chip_sems(n):
    return [pltpu.SemaphoreType.DMA((3 * n,)), pltpu.SemaphoreType.DMA((3 * n,)), pltpu.SemaphoreType.DMA((n,))]


def _chip_exchange(ps, gather, name):
    n, m = len(ps), len(gather)

    def body(*refs):
        ins, g_ins, outs, g_outs = refs[:n], refs[n:n + m], refs[n + m:2 * n + m], refs[2 * n + m:2 * (n + m)]
        place = _mesh_place()
        exchanging = _chip_protocol(place, ins, outs, *refs[2 * (n + m):2 * (n + m) + 3])
        gathering = _gather_protocol(place, g_ins, g_outs, *refs[2 * (n + m) + 3:])
        next(exchanging)
        for _ in gathering:
            pass
        for _ in exchanging:
            pass

    outs = pl.pallas_call(
        body, out_shape=[jax.ShapeDtypeStruct(q.shape, q.dtype) for q in ps]
        + [jax.ShapeDtypeStruct((N_DEV, *b.shape), b.dtype) for b in gather],
        in_specs=_hbm_specs(n + m), out_specs=_hbm_specs(n + m),
        scratch_shapes=_chip_sems(n) + _gather_sems(m), name=name)(*ps, *gather)
    return outs[:n], outs[n:]


def _row_block(rows, bytes_per_row):
    best = None
    for rb in range(16, rows + 1, 16):
        if rows % rb == 0 and rb * bytes_per_row <= ELEMENTWISE_STEP_BYTES:
            best = rb
    return rows if best is None else best


def _add_pair(own, recv, name):
    shape = own.shape
    last = shape[-1]
    rows = own.size // last
    rb = _row_block(rows, 3 * 4 * (-(-last // LANES) * LANES))

    def body(a_ref, b_ref, o_ref):
        o_ref[...] = (a_ref[...].astype(F32) + b_ref[...].astype(F32)).astype(o_ref.dtype)

    row = lambda i: (i, 0)
    out = pl.pallas_call(
        body, grid=(rows // rb,), in_specs=[pl.BlockSpec((rb, last), row)] * 2, out_specs=pl.BlockSpec((rb, last), row),
        out_shape=jax.ShapeDtypeStruct((rows, last), own.dtype), name=name,
        compiler_params=_cp("parallel"))(own.reshape(rows, last), recv.reshape(rows, last))
    return out.reshape(shape)


def _sum_adamw(gk, w, m, v, name):
    shape = w.shape
    n_part = gk.shape[0]
    last = shape[-1]
    rows = w.size // last
    rb = _row_block(rows, (n_part + 7) * 4 * (-(-last // LANES) * LANES))

    def body(g_ref, w_ref, m_ref, v_ref, go_ref, d_ref, mo_ref, vo_ref):
        g = g_ref[0].astype(F32)
        for k in range(1, n_part):
            g = g + g_ref[k].astype(F32)
        mn = ADAM_B1 * m_ref[...] + (1.0 - ADAM_B1) * g
        vn = ADAM_B2 * v_ref[...] + (1.0 - ADAM_B2) * jnp.square(g)
        m_hat = mn / (1.0 - ADAM_B1 ** ADAM_STEP)
        v_hat = vn / (1.0 - ADAM_B2 ** ADAM_STEP)
        go_ref[...] = g
        d_ref[...] = -ADAM_LR * (m_hat / (jnp.sqrt(v_hat) + ADAM_EPS) + ADAM_WD * w_ref[...])
        mo_ref[...] = mn
        vo_ref[...] = vn

    row = lambda i: (i, 0)
    outs = pl.pallas_call(
        body, grid=(rows // rb,),
        in_specs=[pl.BlockSpec((n_part, rb, last), lambda i: (0, i, 0))] + [pl.BlockSpec((rb, last), row)] * 3,
        out_specs=[pl.BlockSpec((rb, last), row)] * 4,
        out_shape=[jax.ShapeDtypeStruct((rows, last), F32)] * 4,
        name=name, compiler_params=_cp("parallel"))(gk.reshape(n_part, rows, last), *[a.reshape(rows, last) for a in (w, m, v)])
    return [o.reshape(shape) for o in outs]


def _seg_rows(shape):
    n = 1
    for d in shape:
        n *= d
    return -(-n // (8 * LANES)) * 8


def _pack(arrs):
    segs = []
    for a in arrs:
        r = _seg_rows(a.shape)
        segs.append(jnp.pad(a.reshape(-1).astype(F32), (0, r * LANES - a.size)).reshape(r, LANES))
    rows = sum(s.shape[0] for s in segs)
    total = -(-rows // PACK_ROWS) * PACK_ROWS
    if total > rows:
        segs.append(jnp.zeros((total - rows, LANES), F32))
    return jnp.concatenate(segs, axis=0)


def _unpack(pack, shapes):
    out, off = [], 0
    for sh in shapes:
        r = _seg_rows(sh)
        n = 1
        for d in sh:
            n *= d
        out.append(pack[off:off + r].reshape(-1)[:n].reshape(sh))
        off += r
    return out


def _to_dest_blocks(full, axis, dtype):
    sh = list(full.shape)
    sh[axis:axis + 1] = [N_DEV // 2, 2, sh[axis] // N_DEV]
    return jnp.moveaxis(full.reshape(sh), (axis, axis + 1), (1, 0)).astype(dtype)


def _from_gathered(g, axis):
    m = jnp.moveaxis(g, 0, axis)
    sh = list(m.shape)
    sh[axis:axis + 2] = [sh[axis] * sh[axis + 1]]
    return m.reshape(sh)


D_IN = 3336
W_IN_SHARD = D_IN // N_DEV
W_IN_MOVES = ((0, 2048, 0), (2048, 2056, 3328), (2056, D_IN, 2048))


def _w_in_windows(k):
    lo, hi = k * W_IN_SHARD, (k + 1) * W_IN_SHARD
    out = []
    for a, b, mine in W_IN_MOVES:
        a2, b2 = max(a, lo), min(b, hi)
        if b2 > a2:
            out.append((a2 - lo, b2 - a2, mine + a2 - a))
    return out


def _assemble_w_in(gathered, name):
    depth = gathered.shape[1]
    rb = 256

    def body(g_ref, o_ref):
        o_ref[0, :, D_IN:] = jnp.zeros((rb, ZW - D_IN), o_ref.dtype)
        for k in range(N_DEV):
            for off, width, mine in _w_in_windows(k):
                o_ref[0, :, mine:mine + width] = g_ref[k, 0, :, off:off + width]

    return pl.pallas_call(
        body, grid=(depth, D // rb),
        in_specs=[pl.BlockSpec((N_DEV, 1, rb, W_IN_SHARD), lambda l, i: (0, l, i, 0))],
        out_specs=pl.BlockSpec((1, rb, ZW), lambda l, i: (l, i, 0)),
        out_shape=jax.ShapeDtypeStruct((depth, D, ZW), gathered.dtype),
        name=name, compiler_params=_cp("parallel", "parallel"))(gathered)


def _split_dw_in(dws, name):
    depth = len(dws)
    rb = 128

    def body(*refs):
        o_ref = refs[-1]
        for l in range(depth):
            pieces = refs[5 * l:5 * l + 5]
            for k in range(N_DEV):
                for off, width, mine in _w_in_windows(k):
                    for p_ref, start, n in zip(pieces, Z_OFFSETS, Z_PIECES):
                        a, b = max(mine, start), min(mine + width, start + n)
                        if b > a:
                            o_ref[k % 2, k // 2, l, :, off + a - mine:off + b - mine] = (
                                p_ref[:, a - start:b - start].astype(o_ref.dtype))

    row = lambda i: (i, 0)
    flat = [piece for layer in dws for piece in layer]
    return pl.pallas_call(
        body, grid=(D // rb,),
        in_specs=[pl.BlockSpec((rb, n), row) for _ in range(depth) for n in Z_PIECES],
        out_specs=pl.BlockSpec((2, N_DEV // 2, depth, rb, W_IN_SHARD), lambda i: (0, 0, 0, i, 0)),
        out_shape=jax.ShapeDtypeStruct((2, N_DEV // 2, depth, D, W_IN_SHARD), WIRE['w_in']),
        name=name, compiler_params=_cp("parallel"))(*flat)


EARLY = ('w_in', 'ssm_w_glu', 'dn_conv_w')
LATE = ('w_out', 'w_ple_gate', 'w_ple')


def _layer_weights(names, gathered, layer):
    full = {}
    for n, g in zip(names, gathered):
        full[n] = _assemble_w_in(g, f"assemble_w_in_l{layer}") if n == 'w_in' else _from_gathered(g, SHARDED[n])
        if n == 'ssm_w_glu':
            full[n] = full[n].astype(F32)
    return full


def _pair_reduce(names, layer_grads, tag):
    dest = [_split_dw_in([layer_grads[n]], f"split_dw_in_{tag}") if n == 'w_in'
            else _to_dest_blocks(layer_grads[n][None], SHARDED[n], WIRE[n]) for n in names]
    c = lax.axis_index("c")
    own = [lax.dynamic_index_in_dim(d, c, 0, keepdims=False) for d in dest]
    for_sibling = [lax.dynamic_index_in_dim(d, 1 - c, 0, keepdims=False) for d in dest]
    from_sibling = _pair_exchange(for_sibling, f"grads_pair_exchange_{tag}")
    return [_add_pair(a, b, f"grads_pair_sum_{n}_{tag}") for n, a, b in zip(names, own, from_sibling)]


def _local_step(x, p, wts, sharded, target, shards=None, reduce_early=False):
    sharded = [dict(d or {}) for d in sharded]
    bl, s, _ = x.shape
    t = bl * s
    depth = p.shape[0]
    tb, sg_tb = TB, SG_TB

    def by_example(a):
        return a.reshape(bl, s, a.shape[-1])

    def flat(a):
        return a.reshape(t, a.shape[-1])

    xs = [x.reshape(t, D)]
    saved = []
    for i in range(depth):
        li = f"l{i}"
        ng = wts['norm_g'][i].reshape(1, D)
        lw = sharded[i]
        w_in = lw['w_in'][0]
        s5_par_in = (wts['ssm_a_re'][i], wts['ssm_a_im'][i], wts['ssm_b_re'][i], wts['ssm_b_im'][i],
                     wts['ssm_c_re'][i], wts['ssm_c_im'][i], wts['ssm_d'][i], wts['ssm_log_step'][i])
        tabs, tab_vjp = jax.vjp(_s5_tables, *s5_par_in)
        s5_par = (*tabs, lw['ssm_w_glu'][0], wts['ssm_b_glu'][i].reshape(1, D_SSM))
        s5_const = _s5_powers(wts['ssm_a_re'][i], wts['ssm_a_im'][i], wts['ssm_log_step'][i])
        conv8 = jnp.pad(lw['dn_conv_w'][0], ((0, 4), (0, 0)))
        dn_par = (jnp.repeat(wts['dn_a_log'][i], DH).reshape(1, D_DN), jnp.repeat(wts['dn_dt_bias'][i], DH).reshape(1, D_DN),
                  wts['dn_norm_g'][i].reshape(1, DH))
        sg_par = (wts['sg_ln_g'][i].reshape(1, D_SG), wts['sg_ln_b'][i].reshape(1, D_SG), wts['sg_w'][i],
                  jnp.pad(jnp.transpose(wts['sg_b'][i]), ((0, 0), (0, LANES - 4))))
        pi = p[i].reshape(t, D_PLE)

        z_ssm, z_qkv, z_gdn, z_sg, z_ab, qkvn = _in_proj_fwd(xs[i], ng, w_in, conv8, s, tb, f"in_proj_fwd_{li}")
        wanted = [(i, n) for n in LATE if n not in lw]
        if i + 1 < depth:
            wanted += [(i + 1, n) for n in SHARDED_ORDER if n not in sharded[i + 1]]
        (y_ssm, carries, h_all, y_dn, states, tinvs), gathered = _mix_fwd(
            by_example(z_ssm), s5_par, s5_const, by_example(qkvn), by_example(z_ab), by_example(z_gdn), dn_par, bl, s,
            f"mix_fwd_{li}", gather=[shards[l][n] for l, n in wanted])
        for l in sorted({ll for ll, _ in wanted}):
            names = [n for ll, n in wanted if ll == l]
            sharded[l].update(_layer_weights(names, [g for (ll, _), g in zip(wanted, gathered) if ll == l], l))
        out_par = (lw['w_out'][0].astype(BF16), wts['ple_norm_g'][i].reshape(1, D), lw['w_ple_gate'][0].astype(BF16),
                   lw['w_ple'][0].astype(BF16))
        y_sg = _sg_fwd(z_sg, sg_par, sg_tb, f"sg_fwd_{li}")
        ys = (flat(y_ssm), flat(y_dn), y_sg)
        x_next, x1, gate = _out_fwd(xs[i], ys, pi, *out_par, tb, f"out_fwd_{li}")
        xs.append(x_next)
        saved.append(dict(ng=ng, w_in=w_in, tab_vjp=tab_vjp, s5_par=s5_par, s5_const=s5_const, conv8=conv8, dn_par=dn_par,
                          sg_par=sg_par, out_par=out_par, pi=pi, z=(z_ssm, z_qkv, z_gdn, z_sg, z_ab), carries=carries,
                          h_all=h_all, qkvn=qkvn, x1=x1, gate=gate,
                          states=states, tinvs=tinvs, ys=ys))

    dx, dfg, loss_vec = _loss_head(xs[depth], wts['final_norm_g'].reshape(1, D), target.reshape(t, D), tb, "loss_head")
    grads = {n: [None] * depth for n in WEIGHTS if n != 'final_norm_g'}
    grads['final_norm_g'] = dfg.reshape(D)
    pair_sums, by_chip = {}, {}
    for i in reversed(range(depth)):
        li = f"l{i}"
        sv = saved[i]
        z_ssm, z_qkv, z_gdn, z_sg, z_ab = sv['z']
        dx_res, dy_ssm, dy_dn, dy_sg, dwo, dpg, dwg, dwp = _out_bwd(sv['x1'], sv['gate'], sv['ys'], sv['pi'], dx, *sv['out_par'], tb,
                                                                    f"out_bwd_{li}")
        dz_sg, dlng, dlnb, dsgw, dbsp = _sg_bwd(z_sg, dy_sg, sv['sg_par'], sg_tb, f"sg_bwd_{li}")
        if reduce_early:
            for n, gsum in zip(LATE, _pair_reduce(LATE, {'w_out': dwo, 'w_ple_gate': dwg, 'w_ple': dwp}, f"late_{li}")):
                pair_sums[(i, n)] = gsum
        travelling = [key for key in pair_sums if key not in by_chip]
        (dz_ssm, dbb, dcb, dlam, ddv, dwglu, dbglu), (dqkvn, dz_ab, dz_gdn, dal, ddt, dng), exchanged = _mix_bwd(
            by_example(z_ssm), sv['carries'], sv['h_all'], by_example(dy_ssm), sv['s5_par'], sv['s5_const'],
            by_example(sv['qkvn']), by_example(z_ab), by_example(z_gdn), sv['states'], sv['tinvs'], by_example(dy_dn),
            sv['dn_par'], bl, s, f"mix_bwd_{li}", exchange=[pair_sums[key] for key in travelling])
        by_chip.update(zip(travelling, exchanged))
        dx, dnorm, dz_qkv, dconv = _in_proj_bwd_dx(xs[i], sv['ng'], sv['w_in'], sv['conv8'], flat(dz_ssm), flat(dz_gdn), dz_sg,
                                                   flat(dz_ab), z_qkv, flat(dqkvn), dx_res, s, tb, f"in_proj_bwd_dx_{li}")
        dzs = (flat(dz_ssm), dz_qkv, flat(dz_gdn), dz_sg, flat(dz_ab))
        dws = _in_proj_bwd_dw(xs[i], sv['ng'], dzs, min(TB_DW, t), f"in_proj_bwd_dw_{li}")
        ds5 = sv['tab_vjp']((dbb, dcb, dlam, ddv))
        for n, gval in zip(('ssm_a_re', 'ssm_a_im', 'ssm_b_re', 'ssm_b_im', 'ssm_c_re', 'ssm_c_im', 'ssm_d', 'ssm_log_step'), ds5):
            grads[n][i] = gval
        grads['norm_g'][i] = dnorm.reshape(D)
        grads['w_in'][i] = dws
        grads['ssm_w_glu'][i] = dwglu
        grads['ssm_b_glu'][i] = dbglu.reshape(D_SSM)
        grads['dn_conv_w'][i] = dconv[:4]
        grads['dn_a_log'][i] = dal.reshape(H, DH).sum(axis=1)
        grads['dn_dt_bias'][i] = ddt.reshape(H, DH).sum(axis=1)
        grads['dn_norm_g'][i] = dng.reshape(DH)
        grads['sg_ln_g'][i] = dlng.reshape(D_SG)
        grads['sg_ln_b'][i] = dlnb.reshape(D_SG)
        grads['sg_w'][i] = dsgw
        grads['sg_b'][i] = jnp.transpose(dbsp[:, :4])
        grads['w_out'][i] = dwo
        grads['ple_norm_g'][i] = dpg.reshape(D)
        grads['w_ple_gate'][i] = dwg
        grads['w_ple'][i] = dwp
        if reduce_early:
            for n, gsum in zip(EARLY, _pair_reduce(EARLY, {n: grads[n][i] for n in EARLY}, f"early_{li}")):
                pair_sums[(i, n)] = gsum
    grads = {n: (g if n in ('final_norm_g', 'w_in') else jnp.stack(g)) for n, g in grads.items()}
    return loss_vec[0, 0], dx.reshape(bl, s, D), grads, {k: v for k, v in pair_sums.items() if k not in by_chip}, by_chip


def kernel(x, p, norm_g, w_in, ssm_a_re, ssm_a_im, ssm_b_re, ssm_b_im, ssm_c_re, ssm_c_im, ssm_d, ssm_log_step, ssm_w_glu, ssm_b_glu, dn_conv_w, dn_a_log, dn_dt_bias, dn_norm_g, sg_ln_g, sg_ln_b, sg_w, sg_b, w_out, ple_norm_g, w_ple_gate, w_ple, final_norm_g, loss_target, m_norm_g, m_w_in, m_ssm_a_re, m_ssm_a_im, m_ssm_b_re, m_ssm_b_im, m_ssm_c_re, m_ssm_c_im, m_ssm_d, m_ssm_log_step, m_ssm_w_glu, m_ssm_b_glu, m_dn_conv_w, m_dn_a_log, m_dn_dt_bias, m_dn_norm_g, m_sg_ln_g, m_sg_ln_b, m_sg_w, m_sg_b, m_w_out, m_ple_norm_g, m_w_ple_gate, m_w_ple, m_final_norm_g, v_norm_g, v_w_in, v_ssm_a_re, v_ssm_a_im, v_ssm_b_re, v_ssm_b_im, v_ssm_c_re, v_ssm_c_im, v_ssm_d, v_ssm_log_step, v_ssm_w_glu, v_ssm_b_glu, v_dn_conv_w, v_dn_a_log, v_dn_dt_bias, v_dn_norm_g, v_sg_ln_g, v_sg_ln_b, v_sg_w, v_sg_b, v_w_out, v_ple_norm_g, v_w_ple_gate, v_w_ple, v_final_norm_g):
    w_loc = dict(zip(WEIGHTS, (norm_g, w_in, ssm_a_re, ssm_a_im, ssm_b_re, ssm_b_im, ssm_c_re, ssm_c_im, ssm_d, ssm_log_step,
                               ssm_w_glu, ssm_b_glu, dn_conv_w, dn_a_log, dn_dt_bias, dn_norm_g, sg_ln_g, sg_ln_b, sg_w, sg_b,
                               w_out, ple_norm_g, w_ple_gate, w_ple, final_norm_g)))
    m_loc = dict(zip(WEIGHTS, (m_norm_g, m_w_in, m_ssm_a_re, m_ssm_a_im, m_ssm_b_re, m_ssm_b_im, m_ssm_c_re, m_ssm_c_im, m_ssm_d,
                               m_ssm_log_step, m_ssm_w_glu, m_ssm_b_glu, m_dn_conv_w, m_dn_a_log, m_dn_dt_bias, m_dn_norm_g,
                               m_sg_ln_g, m_sg_ln_b, m_sg_w, m_sg_b, m_w_out, m_ple_norm_g, m_w_ple_gate, m_w_ple, m_final_norm_g)))
    v_loc = dict(zip(WEIGHTS, (v_norm_g, v_w_in, v_ssm_a_re, v_ssm_a_im, v_ssm_b_re, v_ssm_b_im, v_ssm_c_re, v_ssm_c_im, v_ssm_d,
                               v_ssm_log_step, v_ssm_w_glu, v_ssm_b_glu, v_dn_conv_w, v_dn_a_log, v_dn_dt_bias, v_dn_norm_g,
                               v_sg_ln_g, v_sg_ln_b, v_sg_w, v_sg_b, v_w_out, v_ple_norm_g, v_w_ple_gate, v_w_ple, v_final_norm_g)))

    depth = p.shape[0]
    shards = [{n: w_loc[n][l:l + 1].astype(WIRE[n]) for n in SHARDED_ORDER} for l in range(depth)]
    first = _layer_weights(EARLY, _all_gather([shards[0][n] for n in EARLY], "gather_weights_l0"), 0)

    loss_part, grad_x, grads, left, by_chip = _local_step(
        x, p, w_loc, [first] + [None] * (depth - 1), loss_target, shards, reduce_early=True)

    rep_pack = _pack([grads[n] for n in REP_NARROW]).astype(WIRE['replicated'])
    fine_pack = _pack([grads['final_norm_g'], loss_part.reshape(1)])
    exchanged, (rep_recv, fine_recv) = _chip_exchange(list(left.values()), [rep_pack, fine_pack], "grads_chip_exchange_l0")
    by_chip.update(zip(left, exchanged))
    by_chip = [jnp.concatenate([by_chip[(l, n)] for l in range(depth)], axis=1) for n in SHARDED_ORDER]

    outs = {k: {} for k in 'gdmv'}
    for n, gk in zip(SHARDED_ORDER, by_chip):
        for k, o in zip('gdmv', _sum_adamw(gk, w_loc[n], m_loc[n], v_loc[n], f"adamw_{n}")):
            outs[k][n] = o
    rep_out = _sum_adamw(rep_recv, *[_pack([src[n] for n in REP_NARROW]) for src in (w_loc, m_loc, v_loc)], "adamw_replicated")
    for k, rep_p in zip('gdmv', rep_out):
        outs[k].update(zip(REP_NARROW, _unpack(rep_p, [w_loc[n].shape for n in REP_NARROW])))
    one = jnp.zeros((1,), F32)
    fine_out = _sum_adamw(fine_recv, *[_pack([src['final_norm_g'], one]) for src in (w_loc, m_loc, v_loc)], "adamw_final_norm")
    for k, fine_p in zip('gdmv', fine_out):
        outs[k].update(zip(['final_norm_g', 'loss'], _unpack(fine_p, [w_loc['final_norm_g'].shape, (1,)])))
    loss = outs['g']['loss'].reshape(())
    return (loss, grad_x, *[outs['g'][n] for n in WEIGHTS], *[outs['d'][n] for n in WEIGHTS],
            *[outs['m'][n] for n in WEIGHTS], *[outs['v'][n] for n in WEIGHTS])
```

```python
import functools

import jax
import jax.numpy as jnp
from jax import lax
from jax.experimental import pallas as pl
from jax.experimental.pallas import tpu as pltpu

F32 = jnp.float32
BF16 = jnp.bfloat16
EPS = 1e-6

D = 1024
D_PLE = 256
D_SSM = 256
D_DN = 512
D_SG = 256
G = 16
CG = 16
NS = 64
NRE = G * NS
H = 4
DH = 128
DN_C = 128
SG_C = 128
ZW = 3456
Z_PIECES = (512, 1536, 512, 768, 128)
N_DEV = 8
LANES = 128
PACK_ROWS = 256
VMEM_LIMIT = 56 * 1024 * 1024
ELEMENTWISE_STEP_BYTES = 4 * 1024 * 1024
TB = 256
SG_TB = 512
TB_DW = 512

ADAM_LR = 0.001
ADAM_B1 = 0.9
ADAM_B2 = 0.999
ADAM_EPS = 1e-08
ADAM_WD = 0.01
ADAM_STEP = 10

MIX_HEAD_START = 3
S5_L = 128
S5_GROUP = 8
S5_SHIFTS = (1, 2, 4)

WEIGHTS = ['norm_g', 'w_in', 'ssm_a_re', 'ssm_a_im', 'ssm_b_re', 'ssm_b_im', 'ssm_c_re', 'ssm_c_im', 'ssm_d',
           'ssm_log_step', 'ssm_w_glu', 'ssm_b_glu', 'dn_conv_w', 'dn_a_log', 'dn_dt_bias', 'dn_norm_g', 'sg_ln_g',
           'sg_ln_b', 'sg_w', 'sg_b', 'w_out', 'ple_norm_g', 'w_ple_gate', 'w_ple', 'final_norm_g']
SHARDED = {'w_in': 2, 'ssm_w_glu': 1, 'dn_conv_w': 2, 'w_out': 1, 'w_ple_gate': 1, 'w_ple': 2}
SHARDED_ORDER = ['w_in', 'ssm_w_glu', 'dn_conv_w', 'w_out', 'w_ple_gate', 'w_ple']
WIRE = {'w_in': BF16, 'ssm_w_glu': BF16, 'dn_conv_w': F32, 'w_out': BF16, 'w_ple_gate': BF16, 'w_ple': BF16,
        'replicated': BF16}
REPLICATED_ORDER = [n for n in WEIGHTS if n not in SHARDED]
REP_NARROW = [n for n in REPLICATED_ORDER if n != 'final_norm_g']


def _cp(*sem):
    return pltpu.CompilerParams(dimension_semantics=sem, vmem_limit_bytes=VMEM_LIMIT)


def _dg(a, b, ca, cb, precision=None):
    return lax.dot_general(a, b, (((ca,), (cb,)), ((), ())), precision=precision, preferred_element_type=F32)


@jax.custom_vjp
def _mm(a, b):
    return _dg(a.astype(BF16), b.astype(BF16), 1, 0)


def _mm_fwd(a, b):
    return _mm(a, b), (a, b)


def _mm_bwd(res, g):
    a, b = res
    gb = g.astype(BF16)
    return _dg(gb, b.astype(BF16), 1, 1), _dg(a.astype(BF16), gb, 0, 0)


_mm.defvjp(_mm_fwd, _mm_bwd)


@jax.custom_vjp
def _mm_nt(a, b):
    return _dg(a.astype(BF16), b.astype(BF16), 1, 1)


def _mm_nt_fwd(a, b):
    return _mm_nt(a, b), (a, b)


def _mm_nt_bwd(res, g):
    a, b = res
    gb = g.astype(BF16)
    return _dg(gb, b.astype(BF16), 1, 0), _dg(gb, a.astype(BF16), 0, 0)


_mm_nt.defvjp(_mm_nt_fwd, _mm_nt_bwd)


@jax.custom_vjp
def _mm_tn(a, b):
    return _dg(a.astype(BF16), b.astype(BF16), 0, 0)


def _mm_tn_fwd(a, b):
    return _mm_tn(a, b), (a, b)


def _mm_tn_bwd(res, g):
    a, b = res
    gb = g.astype(BF16)
    return _dg(b.astype(BF16), gb, 1, 1), _dg(a.astype(BF16), gb, 1, 0)


_mm_tn.defvjp(_mm_tn_fwd, _mm_tn_bwd)


def _split(x, n):
    pieces = []
    for _ in range(n - 1):
        hi = x.astype(BF16)
        pieces.append(hi)
        x = x - hi.astype(F32)
    pieces.append(x.astype(BF16))
    return pieces


def _dg3(a, b, ca, cb):
    a_hi, a_lo = _split(a, 2)
    b_hi, b_lo = _split(b, 2)
    return _dg(a_hi, b_hi, ca, cb) + (_dg(a_hi, b_lo, ca, cb) + _dg(a_lo, b_hi, ca, cb))


@jax.custom_vjp
def _dot3(a, b):
    return _dg3(a, b, 1, 0)


def _dot3_fwd(a, b):
    return _dot3(a, b), (a, b)


def _dot3_bwd(res, g):
    a, b = res
    return _dg3(g, b, 1, 1), _dg3(a, g, 0, 0)


_dot3.defvjp(_dot3_fwd, _dot3_bwd)


def _dg_sel(x, e, cx, ce, x_first):
    eb = e.astype(BF16)
    out = None
    for piece in reversed(_split(x, 3)):
        term = _dg(piece, eb, cx, ce) if x_first else _dg(eb, piece, ce, cx)
        out = term if out is None else out + term
    return out


@jax.custom_vjp
def _sel_r(x, e):
    return _dg_sel(x, e, 1, 0, True)


def _sel_r_fwd(x, e):
    return _sel_r(x, e), e


def _sel_r_bwd(e, g):
    return _dg_sel(g, e, 1, 1, True), jnp.zeros_like(e)


_sel_r.defvjp(_sel_r_fwd, _sel_r_bwd)


@jax.custom_vjp
def _sel_l(e, x):
    return _dg_sel(x, e, 0, 1, False)


def _sel_l_fwd(e, x):
    return _sel_l(e, x), e


def _sel_l_bwd(e, g):
    return jnp.zeros_like(e), _dg_sel(g, e, 0, 0, False)


_sel_l.defvjp(_sel_l_fwd, _sel_l_bwd)


def _rms(x, g):
    return x * lax.rsqrt(jnp.mean(x * x, axis=-1, keepdims=True) + EPS) * g


def _silu(x):
    return x * jax.nn.sigmoid(x)


Z_OFFSETS = (0, 512, 2048, 2560, 3328)


def _dn_post(c):
    s = _silu(c)
    parts = []
    for j in range(12):
        xj = s[:, j * DH:(j + 1) * DH]
        if j < 8:
            xj = xj * lax.rsqrt(jnp.sum(xj * xj, axis=-1, keepdims=True) + EPS)
        if j < 4:
            xj = xj * (DH ** -0.5)
        parts.append(xj)
    return jnp.concatenate(parts, axis=1)


def _dn_conv(ext, cw_ref, rows):
    c = None
    for k in range(4):
        sh = ext if k == 3 else pltpu.roll(ext, 3 - k, 0)
        term = cw_ref[k:k + 1, :] * sh[ext.shape[0] - rows:, :]
        c = term if c is None else c + term
    return c


def _dn_prep_vjp(prev, cur, nxt, d_cur, d_nxt, cw_ref, tb):
    ext = jnp.concatenate([prev, cur, nxt], axis=0)
    shifted = [ext if k == 3 else pltpu.roll(ext, 3 - k, 0) for k in range(4)]
    c2 = None
    for k in range(4):
        term = cw_ref[k:k + 1, :] * shifted[k][8:, :]
        c2 = term if c2 is None else c2 + term
    _, vjp = jax.vjp(_dn_post, c2)
    (dc2,) = vjp(jnp.concatenate([d_cur, d_nxt], axis=0))
    dz, dcw = None, []
    for k in range(4):
        up = dc2 if k == 3 else pltpu.roll(dc2, tb + 8 - (3 - k), 0)
        term = cw_ref[k:k + 1, :] * up[:tb, :]
        dz = term if dz is None else dz + term
        dcw.append(jnp.sum(dc2[:tb, :] * shifted[k][8:8 + tb, :], axis=0, keepdims=True))
    return dz, dcw


def _in_proj_fwd(x, g, w, conv_w8, s, tb, name):
    t = x.shape[0]
    n_s = s // tb
    w3 = 3 * D_DN
    q0, q1 = Z_OFFSETS[1], Z_OFFSETS[2]

    def body(x_ref, g_ref, w_ref, cw_ref, zs_ref, zq_ref, zg_ref, zsg_ref, zab_ref, qkvn_ref, halo):
        h = _rms(x_ref[...], g_ref[...]).astype(BF16)
        zq = jnp.dot(h, w_ref[:, q0:q1], preferred_element_type=F32)
        zq_ref[...] = zq
        prev = jnp.where(pl.program_id(0) % n_s == 0, 0.0, halo[...])
        qkvn_ref[...] = _dn_post(_dn_conv(jnp.concatenate([prev, zq], axis=0), cw_ref, tb))
        halo[...] = zq[tb - 8:, :]
        zs_ref[...] = jnp.dot(h, w_ref[:, :q0], preferred_element_type=F32)
        rest = jnp.dot(h, w_ref[:, q1:], preferred_element_type=F32)
        zg_ref[...] = rest[:, :Z_PIECES[2]]
        zsg_ref[...] = rest[:, Z_PIECES[2]:Z_PIECES[2] + Z_PIECES[3]]
        zab_ref[...] = rest[:, Z_PIECES[2] + Z_PIECES[3]:]

    row = lambda i: (i, 0)
    full = lambda i: (0, 0)
    widths = Z_PIECES + (w3,)
    return pl.pallas_call(
        body, grid=(t // tb,),
        in_specs=[pl.BlockSpec((tb, D), row), pl.BlockSpec((1, D), full), pl.BlockSpec((D, ZW), full), pl.BlockSpec((8, w3), full)],
        out_specs=[pl.BlockSpec((tb, n), row) for n in widths],
        out_shape=[jax.ShapeDtypeStruct((t, n), F32) for n in widths],
        scratch_shapes=[pltpu.VMEM((8, w3), F32)],
        name=name, compiler_params=_cp("arbitrary"))(x, g, w, conv_w8)


def _in_proj_bwd_dx(x, g, w, conv_w8, dz_ssm, dz_gdn, dz_sg, dz_ab, zq, dqkvn, dx_res, s, tb, name):
    t = x.shape[0]
    n_s = s // tb
    hb = tb // 8
    w3 = 3 * D_DN
    q0, q1 = Z_OFFSETS[1], Z_OFFSETS[2]

    def body(x_ref, g_ref, w_ref, cw_ref, ds_ref, dgd_ref, dsg_ref, dab_ref, cur_ref, prev_ref, next_ref, dq_ref, dqn_ref,
             dxr_ref, dx_ref, dg_ref, dzq_ref, dcw_ref):
        i = pl.program_id(0)

        @pl.when(i == 0)
        def _():
            dg_ref[...] = jnp.zeros_like(dg_ref)
            dcw_ref[...] = jnp.zeros_like(dcw_ref)

        rest = jnp.concatenate([dgd_ref[...], dsg_ref[...], dab_ref[...]], axis=1)
        dh = _dg(ds_ref[...], w_ref[:, :q0], 1, 1) + _dg(rest, w_ref[:, q1:], 1, 1)
        first, last = i % n_s == 0, i % n_s == n_s - 1
        dzq, dcw = _dn_prep_vjp(jnp.where(first, 0.0, prev_ref[...]), cur_ref[...], jnp.where(last, 0.0, next_ref[...]),
                                dq_ref[...], jnp.where(last, 0.0, dqn_ref[...]), cw_ref, tb)
        for k in range(4):
            dcw_ref[k:k + 1, :] += dcw[k]
        dzq = dzq.astype(BF16)
        dzq_ref[...] = dzq
        dh = dh + _dg(dzq, w_ref[:, q0:q1], 1, 1)
        _, vjp = jax.vjp(_rms, x_ref[...], g_ref[...])
        dx, dg = vjp(dh)
        dx_ref[...] = dx + dxr_ref[...]
        dg_ref[...] += dg

    n_blk8 = t // 8
    row = lambda i: (i, 0)
    prv = lambda i: (jnp.maximum(i * hb - 1, 0), 0)
    nxt = lambda i: (jnp.minimum((i + 1) * hb, n_blk8 - 1), 0)
    full = lambda i: (0, 0)
    return pl.pallas_call(
        body, grid=(t // tb,),
        in_specs=[pl.BlockSpec((tb, D), row), pl.BlockSpec((1, D), full), pl.BlockSpec((D, ZW), full), pl.BlockSpec((8, w3), full)]
        + [pl.BlockSpec((tb, n), row) for n in (Z_PIECES[0], Z_PIECES[2], Z_PIECES[3], Z_PIECES[4])]
        + [pl.BlockSpec((tb, w3), row), pl.BlockSpec((8, w3), prv), pl.BlockSpec((8, w3), nxt),
           pl.BlockSpec((tb, w3), row), pl.BlockSpec((8, w3), nxt), pl.BlockSpec((tb, D), row)],
        out_specs=[pl.BlockSpec((tb, D), row), pl.BlockSpec((1, D), full), pl.BlockSpec((tb, w3), row), pl.BlockSpec((8, w3), full)],
        out_shape=[jax.ShapeDtypeStruct((t, D), F32), jax.ShapeDtypeStruct((1, D), F32), jax.ShapeDtypeStruct((t, w3), BF16),
                   jax.ShapeDtypeStruct((8, w3), F32)],
        name=name, compiler_params=_cp("arbitrary"))(x, g, w, conv_w8, dz_ssm, dz_gdn, dz_sg, dz_ab, zq, zq, zq, dqkvn, dqkvn, dx_res)


def _in_proj_bwd_dw(x, g, dzs, tb, name):
    t = x.shape[0]

    def body(x_ref, g_ref, d0, d1, d2, d3, d4, *dw_refs):
        @pl.when(pl.program_id(0) == 0)
        def _():
            for r in dw_refs:
                r[...] = jnp.zeros_like(r)

        h = _rms(x_ref[...], g_ref[...]).astype(BF16)
        for d_ref, dw_ref in zip((d0, d1, d2, d3, d4), dw_refs):
            dw_ref[...] += _dg(h, d_ref[...].astype(BF16), 0, 0)

    row = lambda i: (i, 0)
    full = lambda i: (0, 0)
    return pl.pallas_call(
        body, grid=(t // tb,),
        in_specs=[pl.BlockSpec((tb, D), row), pl.BlockSpec((1, D), full)] + [pl.BlockSpec((tb, n), row) for n in Z_PIECES],
        out_specs=[pl.BlockSpec((D, n), full) for n in Z_PIECES],
        out_shape=[jax.ShapeDtypeStruct((D, n), F32) for n in Z_PIECES],
        name=name, compiler_params=_cp("arbitrary"))(x, g, *dzs)


def _lam_pow(a_re, a_im, log_step, k):
    step = jnp.exp(log_step)[:, None]
    mag = jnp.exp(k * a_re * step)
    ang = k * a_im * step
    return mag * jnp.cos(ang), mag * jnp.sin(ang)


def _s5_powers(a_re, a_im, log_step):
    def table(ks):
        re, im = _lam_pow(a_re, a_im, log_step, jnp.asarray(ks, F32)[:, None, None])
        return jnp.concatenate([re.reshape(len(ks), NRE), im.reshape(len(ks), NRE)], axis=-1)

    ld = table(S5_SHIFTS).reshape(len(S5_SHIFTS), 1, 2 * NRE)
    return ld, table(range(1, S5_GROUP + 1)), table(range(S5_GROUP, 0, -1))


def _s5_tables(a_re, a_im, b_re, b_im, c_re, c_im, d_skip, log_step):
    lam_re, lam_im = _lam_pow(a_re, a_im, log_step, 1.0)
    den = a_re * a_re + a_im * a_im
    nr, ni = lam_re - 1.0, lam_im
    f_re = (nr * a_re + ni * a_im) / den
    f_im = (ni * a_re - nr * a_im) / den
    bbar_re = f_re[..., None] * b_re - f_im[..., None] * b_im
    bbar_im = f_re[..., None] * b_im + f_im[..., None] * b_re
    eye = jnp.eye(G, dtype=F32)

    def blk_b(bb):
        return (jnp.transpose(bb, (0, 2, 1))[:, :, None, :] * eye[:, None, :, None]).reshape(D_SSM, NRE)

    def blk_c(cc):
        return (jnp.transpose(cc, (0, 2, 1))[:, :, None, :] * eye[:, None, :, None]).reshape(NRE, D_SSM)

    b_blk = jnp.concatenate([blk_b(bbar_re), blk_b(bbar_im)], axis=1)
    c_blk = jnp.concatenate([blk_c(c_re), -blk_c(c_im)], axis=0)
    lam = jnp.concatenate([lam_re.reshape(1, NRE), lam_im.reshape(1, NRE)], axis=-1)
    return b_blk, c_blk, lam, d_skip.reshape(1, D_SSM)


def _group_shift(x, d, up=False):
    r = lax.broadcasted_iota(jnp.int32, x.shape, 0) & (S5_GROUP - 1)
    if up:
        return jnp.where(r < S5_GROUP - d, pltpu.roll(x, x.shape[0] - d, 0), 0.0)
    return jnp.where(r >= d, pltpu.roll(x, d, 0), 0.0)


def _s5_scan_steps(hr, hi, cr, ci, lds, lp):
    for ld, d in zip(lds, S5_SHIFTS):
        lr, li = ld[:, :NRE], ld[:, NRE:]
        sr, si = _group_shift(hr, d), _group_shift(hi, d)
        hr, hi = hr + lr * sr - li * si, hi + lr * si + li * sr
        yield
    pr, pi = lp[:, :NRE], lp[:, NRE:]
    rows_r, rows_i = [], []
    for r in range(hr.shape[0] // S5_GROUP):
        br, bi = hr[r * S5_GROUP:(r + 1) * S5_GROUP], hi[r * S5_GROUP:(r + 1) * S5_GROUP]
        br, bi = br + pr * cr - pi * ci, bi + pr * ci + pi * cr
        cr, ci = br[S5_GROUP - 1:S5_GROUP], bi[S5_GROUP - 1:S5_GROUP]
        rows_r.append(br)
        rows_i.append(bi)
        if r % 2:
            yield
    return jnp.concatenate(rows_r, axis=0), jnp.concatenate(rows_i, axis=0)


@jax.custom_vjp
def _known_scan(xr, xi, cr, ci, lam, lds, lp_rev, hr, hi):
    return hr, hi


def _known_scan_fwd(xr, xi, cr, ci, lam, lds, lp_rev, hr, hi):
    return (hr, hi), (cr, ci, lam, lds, lp_rev, hr, hi)


def _known_scan_bwd(res, cts):
    cr, ci, lam, lds, lp_rev, hr, hi = res
    ar, ai = cts
    for ld, d in zip(lds, S5_SHIFTS):
        lr, li = ld[:, :NRE], ld[:, NRE:]
        sr, si = _group_shift(ar, d, up=True), _group_shift(ai, d, up=True)
        ar, ai = ar + lr * sr + li * si, ai + lr * si - li * sr
    qr, qi = lp_rev[:, :NRE], lp_rev[:, NRE:]
    nr, ni = jnp.zeros_like(cr), jnp.zeros_like(ci)
    rows_r, rows_i = [], []
    for r in reversed(range(hr.shape[0] // S5_GROUP)):
        br, bi = ar[r * S5_GROUP:(r + 1) * S5_GROUP], ai[r * S5_GROUP:(r + 1) * S5_GROUP]
        br, bi = br + qr * nr + qi * ni, bi + qr * ni - qi * nr
        nr, ni = br[0:1], bi[0:1]
        rows_r.insert(0, br)
        rows_i.insert(0, bi)
    ar, ai = jnp.concatenate(rows_r, axis=0), jnp.concatenate(rows_i, axis=0)
    lr, li = lam[:, :NRE], lam[:, NRE:]
    dcr, dci = lr * nr + li * ni, lr * ni - li * nr
    first = lax.broadcasted_iota(jnp.int32, hr.shape, 0) == 0
    pr = jnp.where(first, cr, pltpu.roll(hr, 1, 0))
    pi = jnp.where(first, ci, pltpu.roll(hi, 1, 0))
    dlam = jnp.concatenate([jnp.sum(ar * pr + ai * pi, axis=0, keepdims=True),
                            jnp.sum(ai * pr - ar * pi, axis=0, keepdims=True)], axis=1)
    return (ar, ai, dcr, dci, dlam, [jnp.zeros_like(ld) for ld in lds], jnp.zeros_like(lp_rev),
            jnp.zeros_like(hr), jnp.zeros_like(hi))


_known_scan.defvjp(_known_scan_fwd, _known_scan_bwd)


def _interleave(short, long, head_start=0):
    gens = list(short) + list(long)
    results = [None] * len(gens)

    def advance(live):
        still = []
        for idx, gen in live:
            try:
                next(gen)
                still.append((idx, gen))
            except StopIteration as done:
                results[idx] = done.value
        return still

    live_short = advance(list(enumerate(gens))[:len(short)])
    live_long = list(enumerate(gens))[len(short):]
    for _ in range(head_start):
        live_long = advance(live_long)
    live = live_short + live_long
    while live:
        live = advance(live)
    return results[:len(short)], results[len(short):]


def _s5_chunk_gen(u, gate, cr, ci, b_blk, c_blk, lam, dv, wglu, bglu, lds, lp, lp_rev, known_h=None):
    bu = _mm(u, b_blk)
    xr, xi = bu[:, :NRE], bu[:, NRE:]
    yield
    if known_h is None:
        hr, hi = yield from _s5_scan_steps(xr, xi, cr, ci, lds, lp)
    else:
        hr, hi = _known_scan(xr, xi, cr, ci, lam, lds, lp_rev, *known_h)
    y = _mm(jnp.concatenate([hr, hi], axis=1), c_blk) + dv * u
    yield
    y = jax.nn.gelu(y)
    y = y * jax.nn.sigmoid(_mm(y, wglu) + bglu)
    return y * _silu(gate), hr, hi


S5_PAR_SHAPES = [(D_SSM, 2 * NRE), (2 * NRE, D_SSM), (1, 2 * NRE), (1, D_SSM), (D_SSM, D_SSM), (1, D_SSM)]
S5_CONST_SHAPES = [(len(S5_SHIFTS), 1, 2 * NRE), (S5_GROUP, 2 * NRE), (S5_GROUP, 2 * NRE)]
DN_PAR_SHAPES = [(1, D_DN), (1, D_DN), (1, DH)]


def _mix_specs(bl, n_c, rev):
    def chunk(i):
        return n_c - 1 - i if rev else i

    def tok(n):
        return pl.BlockSpec((bl, DN_C, n), lambda i: (0, chunk(i), 0))

    def per_chunk(shape):
        return pl.BlockSpec((bl, 1, *shape), lambda i: (0, chunk(i)) + (0,) * len(shape))

    def whole(shape):
        return pl.BlockSpec(shape, lambda i: (0,) * len(shape))

    return tok, per_chunk, whole


def _unit_lower_inverse_steps(ms):
    c_len = ms[0].shape[0]
    eye = lax.broadcasted_iota(jnp.int32, (c_len, c_len), 0) == lax.broadcasted_iota(jnp.int32, (c_len, c_len), 1)
    ident = jnp.where(eye, 1.0, 0.0)
    ps = ms
    tinvs = [ident - m for m in ms]
    for _ in range(c_len.bit_length() - 2):
        ps = [_dg3(p, p, 1, 0) for p in ps]
        yield
        tinvs = [t + _dg3(t, p, 1, 0) for t, p in zip(tinvs, ps)]
        yield
    return tinvs


@jax.custom_vjp
def _known_inverses(ms, tinvs):
    return tinvs


def _known_inverses_fwd(ms, tinvs):
    return tinvs, tinvs


def _known_inverses_bwd(tinvs, gs):
    return [-_dg3(_dg3(t, g, 0, 0), t, 1, 1) for t, g in zip(tinvs, gs)], [jnp.zeros_like(t) for t in tinvs]


_known_inverses.defvjp(_known_inverses_fwd, _known_inverses_bwd)


def _dn_chunk_gen(qkv, zab, zg, states, alog_e, dt_e, ng, known_tinvs=None):
    c_len = DN_C
    r = lax.broadcasted_iota(jnp.int32, (c_len, c_len), 0)
    c = lax.broadcasted_iota(jnp.int32, (c_len, c_len), 1)
    causal, strict = r >= c, r > c
    tril = jnp.where(causal, 1.0, 0.0)
    rr = lax.broadcasted_iota(jnp.int32, (LANES, D_DN), 0)
    cc = lax.broadcasted_iota(jnp.int32, (LANES, D_DN), 1)
    e_a = jnp.where((cc >= rr * DH) & (cc < rr * DH + DH) & (rr < H), 1.0, 0.0)
    e_b = jnp.where((cc >= (rr - H) * DH) & (cc < (rr - H) * DH + DH) & (rr >= H) & (rr < 2 * H), 1.0, 0.0)
    a_e = _sel_r(zab, e_a)
    b_e = _sel_r(zab, e_b)
    beta = jax.nn.sigmoid(b_e)
    g = -jnp.exp(alog_e) * jax.nn.softplus(a_e + dt_e)
    yield
    gc = _sel_l(tril, g)
    glast = jnp.sum(g, axis=0, keepdims=True)
    eg = jnp.exp(gc)
    ekd = jnp.exp(glast - gc)
    dl = jnp.exp(glast)
    yield
    heads = range(H)
    sls = [slice(h * DH, (h + 1) * DH) for h in heads]
    qs = [qkv[:, h * DH:(h + 1) * DH] for h in heads]
    ks = [qkv[:, D_DN + h * DH:D_DN + (h + 1) * DH] for h in heads]
    vs = [qkv[:, 2 * D_DN + h * DH:2 * D_DN + (h + 1) * DH] for h in heads]
    ccols = [gc[:, sl] for sl in sls]
    decs = [jnp.where(causal, jnp.exp(jnp.where(causal, cl - jnp.transpose(cl), 0.0)), 0.0) for cl in ccols]
    kbs = [k * beta[:, sl] for k, sl in zip(ks, sls)]
    ms = [jnp.where(strict, _mm_nt(kb, k) * dec, 0.0) for kb, k, dec in zip(kbs, ks, decs)]
    yield
    if known_tinvs is None:
        tinvs = yield from _unit_lower_inverse_steps(ms)
    else:
        tinvs = _known_inverses(ms, list(known_tinvs))
    sols = [_dot3(t, jnp.concatenate([v * beta[:, sl], kb * eg[:, sl]], axis=1))
            for t, v, kb, sl in zip(tinvs, vs, kbs, sls)]
    yield
    atts = [_mm_nt(q, k) * dec for q, k, dec in zip(qs, ks, decs)]
    vnews = [sol[:, :DH] - _mm(sol[:, DH:], st) for sol, st in zip(sols, states)]
    yield
    os_ = [_mm(q * eg[:, sl], st) + _mm(att, vn) for q, sl, st, att, vn in zip(qs, sls, states, atts, vnews)]
    yield
    new_states = [st * dl[:, sl] + _mm_tn(k * ekd[:, sl], vn) for st, sl, k, vn in zip(states, sls, ks, vnews)]
    yield
    ys = [_rms(o, ng) * _silu(zg[:, sl]) for o, sl in zip(os_, sls)]
    return jnp.concatenate(ys, axis=1), new_states, tinvs


def _mix_fwd(zs, s5_par, s5_const, qkv, zab, zg, dn_par, bl, s, name, gather=()):
    assert S5_L == DN_C
    n_c = s // DN_C
    nd = len(S5_SHIFTS)
    m = len(gather)
    tok, per_chunk, whole = _mix_specs(bl, n_c, False)

    def body(*refs):
        (z_ref, b_ref, c_ref, lam_ref, dv_ref, wg_ref, bg_ref, ld_ref, lp_ref, lpr_ref,
         q_ref, ab_ref, zg_ref, al_ref, dt_ref, ng_ref) = refs[:16]
        ys_ref, car_ref, h_ref, yd_ref, st_ref, ti_ref = refs[16 + m:22 + m]
        cs, ssc = refs[22 + 2 * m:24 + 2 * m]
        gathering = (_gather_protocol(_mesh_place(), refs[16:16 + m], refs[22 + m:22 + 2 * m], *refs[24 + 2 * m:])
                     if m else None)

        @pl.when(pl.program_id(0) == 0)
        def _():
            cs[...] = jnp.zeros_like(cs)
            ssc[...] = jnp.zeros_like(ssc)
            if m:
                next(gathering)

        lds = [ld_ref[k] for k in range(nd)]
        s5_gens, dn_gens = [], []
        for e in range(bl):
            c = cs[e]
            car_ref[e, 0] = c
            sts = [ssc[e, h] for h in range(H)]
            for h in range(H):
                st_ref[e, 0, h] = sts[h]
            z = z_ref[e]
            s5_gens.append(_s5_chunk_gen(z[:, :D_SSM], z[:, D_SSM:], c[:, :NRE], c[:, NRE:], b_ref[...], c_ref[...], lam_ref[...],
                                         dv_ref[...], wg_ref[...], bg_ref[...], lds, lp_ref[...], lpr_ref[...]))
            dn_gens.append(_dn_chunk_gen(q_ref[e], ab_ref[e], zg_ref[e], sts, al_ref[...], dt_ref[...], ng_ref[...]))
        s5_outs, dn_outs = _interleave(s5_gens, dn_gens, head_start=MIX_HEAD_START)
        for e in range(bl):
            y_s, hr, hi = s5_outs[e]
            y_d, new_sts, tinvs = dn_outs[e]
            ys_ref[e] = y_s
            h_ref[e, :, :NRE] = hr
            h_ref[e, :, NRE:] = hi
            cs[e, :, :NRE] = hr[S5_L - 1:S5_L]
            cs[e, :, NRE:] = hi[S5_L - 1:S5_L]
            yd_ref[e] = y_d
            for h in range(H):
                ssc[e, h] = new_sts[h]
                ti_ref[e, 0, h] = tinvs[h]

        if m:
            @pl.when(pl.program_id(0) == n_c - 1)
            def _():
                for _ in gathering:
                    pass

    head_mats = jax.ShapeDtypeStruct((bl, n_c, H, DH, DH), F32)
    outs = pl.pallas_call(
        body, grid=(n_c,),
        in_specs=[tok(2 * D_SSM)] + [whole(sh) for sh in S5_PAR_SHAPES + S5_CONST_SHAPES]
        + [tok(3 * D_DN), tok(LANES), tok(D_DN)] + [whole(sh) for sh in DN_PAR_SHAPES] + _hbm_specs(m),
        out_specs=[tok(D_SSM), per_chunk((1, 2 * NRE)), tok(2 * NRE), tok(D_DN), per_chunk((H, DH, DH)), per_chunk((H, DH, DH))]
        + _hbm_specs(m),
        out_shape=[jax.ShapeDtypeStruct((bl, s, D_SSM), F32), jax.ShapeDtypeStruct((bl, n_c, 1, 2 * NRE), F32),
                   jax.ShapeDtypeStruct((bl, s, 2 * NRE), F32), jax.ShapeDtypeStruct((bl, s, D_DN), F32), head_mats, head_mats]
        + [jax.ShapeDtypeStruct((N_DEV, *b.shape), b.dtype) for b in gather],
        scratch_shapes=[pltpu.VMEM((bl, 1, 2 * NRE), F32), pltpu.VMEM((bl, H, DH, DH), F32)] + (_gather_sems(m) if m else []),
        name=name, compiler_params=_cp("arbitrary"))(zs, *s5_par, *s5_const, qkv, zab, zg, *dn_par, *gather)
    return outs[:6], outs[6:]


def _mix_bwd(zs, carries, h_all, dy_s, s5_par, s5_const, qkv, zab, zg, states, tinvs, dy_d, dn_par, bl, s, name, exchange=()):
    n_c = s // DN_C
    nd = len(S5_SHIFTS)
    k_ex = len(exchange)
    tok, per_chunk, whole = _mix_specs(bl, n_c, True)

    def both(examples, s5_tabs, s5_consts, dn_tabs):
        s5_gens = [_s5_chunk_gen(u, gate, cr, ci, *s5_tabs, *s5_consts, known_h=(hr, hi))
                   for u, gate, cr, ci, hr, hi, _, _, _, _, _ in examples]
        dn_gens = [_dn_chunk_gen(q, ab, zgate, sts, *dn_tabs, known_tinvs=known)
                   for _, _, _, _, _, _, q, ab, zgate, sts, known in examples]
        s5_outs, dn_outs = _interleave(s5_gens, dn_gens, head_start=MIX_HEAD_START)
        return [(y_s, hr[S5_L - 1:S5_L], hi[S5_L - 1:S5_L], y_d, new_sts)
                for (y_s, hr, hi), (y_d, new_sts, _) in zip(s5_outs, dn_outs)]

    def body(*refs):
        (z_ref, car_ref, h_ref, dys_ref, b_ref, c_ref, lam_ref, dv_ref, wg_ref, bg_ref, ld_ref, lp_ref, lpr_ref,
         q_ref, ab_ref, zg_ref, st_ref, ti_ref, dyd_ref, al_ref, dt_ref, ng_ref) = refs[:22]
        (dz_ref, db_ref, dc_ref, dlam_ref, ddv_ref, dwg_ref, dbg_ref,
         dq_ref, dab_ref, dzg_ref, dal_ref, ddt_ref, dng_ref) = refs[22 + k_ex:35 + k_ex]
        dcs, dsc = refs[35 + 2 * k_ex:37 + 2 * k_ex]
        accs = (db_ref, dc_ref, dlam_ref, ddv_ref, dwg_ref, dbg_ref, dal_ref, ddt_ref, dng_ref)
        exchanging = (_chip_protocol(_mesh_place(), refs[22:22 + k_ex], refs[35 + k_ex:35 + 2 * k_ex], *refs[37 + 2 * k_ex:])
                      if k_ex else None)

        @pl.when(pl.program_id(0) == 0)
        def _():
            for r in accs + (dcs, dsc):
                r[...] = jnp.zeros_like(r)
            if k_ex:
                next(exchanging)

        examples = []
        for e in range(bl):
            z = z_ref[e]
            c = car_ref[e, 0]
            examples.append((z[:, :D_SSM], z[:, D_SSM:], c[:, :NRE], c[:, NRE:], h_ref[e, :, :NRE], h_ref[e, :, NRE:],
                             q_ref[e], ab_ref[e], zg_ref[e], [st_ref[e, 0, h] for h in range(H)],
                             [ti_ref[e, 0, h] for h in range(H)]))
        _, vjp = jax.vjp(both, examples,
                         (b_ref[...], c_ref[...], lam_ref[...], dv_ref[...], wg_ref[...], bg_ref[...]),
                         ([ld_ref[k] for k in range(nd)], lp_ref[...], lpr_ref[...]),
                         (al_ref[...], dt_ref[...], ng_ref[...]))
        cts = []
        for e in range(bl):
            dc = dcs[e]
            cts.append((dys_ref[e], dc[:, :NRE], dc[:, NRE:], dyd_ref[e], [dsc[e, h] for h in range(H)]))
        d_examples, d_s5, _, d_dn = vjp(cts)
        for e in range(bl):
            du, dgate, dcr, dci, _, _, dq, dab, dzg, dsts, _ = d_examples[e]
            dz_ref[e] = jnp.concatenate([du, dgate], axis=1).astype(BF16)
            dcs[e, :, :NRE] = dcr
            dcs[e, :, NRE:] = dci
            dq_ref[e] = dq
            dab_ref[e] = dab.astype(BF16)
            dzg_ref[e] = dzg.astype(BF16)
            for h in range(H):
                dsc[e, h] = dsts[h]
        for r, ct in zip(accs, (*d_s5, *d_dn)):
            r[...] += ct

        if k_ex:
            @pl.when(pl.program_id(0) == n_c - 1)
            def _():
                for _ in exchanging:
                    pass

    head_mats = per_chunk((H, DH, DH))
    outs = pl.pallas_call(
        body, grid=(n_c,),
        in_specs=[tok(2 * D_SSM), per_chunk((1, 2 * NRE)), tok(2 * NRE), tok(D_SSM)]
        + [whole(sh) for sh in S5_PAR_SHAPES + S5_CONST_SHAPES]
        + [tok(3 * D_DN), tok(LANES), tok(D_DN), head_mats, head_mats, tok(D_DN)] + [whole(sh) for sh in DN_PAR_SHAPES]
        + _hbm_specs(k_ex),
        out_specs=[tok(2 * D_SSM)] + [whole(sh) for sh in S5_PAR_SHAPES]
        + [tok(3 * D_DN), tok(LANES), tok(D_DN)] + [whole(sh) for sh in DN_PAR_SHAPES] + _hbm_specs(k_ex),
        out_shape=[jax.ShapeDtypeStruct((bl, s, 2 * D_SSM), BF16)] + [jax.ShapeDtypeStruct(sh, F32) for sh in S5_PAR_SHAPES]
        + [jax.ShapeDtypeStruct((bl, s, 3 * D_DN), F32), jax.ShapeDtypeStruct((bl, s, LANES), BF16),
           jax.ShapeDtypeStruct((bl, s, D_DN), BF16)]
        + [jax.ShapeDtypeStruct(sh, F32) for sh in DN_PAR_SHAPES]
        + [jax.ShapeDtypeStruct(q.shape, q.dtype) for q in exchange],
        scratch_shapes=[pltpu.VMEM((bl, 1, 2 * NRE), F32), pltpu.VMEM((bl, H, DH, DH), F32)] + (_chip_sems(k_ex) if k_ex else []),
        name=name, compiler_params=_cp("arbitrary"))(
            zs, carries, h_all, dy_s, *s5_par, *s5_const, qkv, zab, zg, states, tinvs, dy_d, *dn_par, *exchange)
    return outs[:7], outs[7:13], outs[13:]


def _sg_fn(n_chunk):
    def f(z, lng, lnb, w, bsp_t):
        u = jax.nn.gelu(z[:, :D_SG])
        v = jax.nn.gelu(z[:, D_SG:2 * D_SG])
        gate = z[:, 2 * D_SG:]
        xc = v - jnp.mean(v, axis=-1, keepdims=True)
        vn = xc * lax.rsqrt(jnp.mean(xc * xc, axis=-1, keepdims=True) + EPS) * lng + lnb
        r = lax.broadcasted_iota(jnp.int32, (SG_C, SG_C), 0)
        c = lax.broadcasted_iota(jnp.int32, (SG_C, SG_C), 1)
        causal = r >= c
        first_half = c < SG_C // 2
        rr = lax.broadcasted_iota(jnp.int32, (LANES, D_SG), 0)
        cc = lax.broadcasted_iota(jnp.int32, (LANES, D_SG), 1)
        expand = jnp.where((cc >= rr * 64) & (cc < rr * 64 + 64) & (rr < 4), 1.0, 0.0)
        bias = _sel_r(bsp_t, expand)
        wm = [jnp.where(causal, w[h], 0.0) for h in range(4)]
        rows = []
        for ci in range(n_chunk):
            vc = vn[ci * SG_C:(ci + 1) * SG_C]
            pairs = []
            for pr in range(2):
                vp = vc[:, pr * LANES:(pr + 1) * LANES]
                pairs.append(jnp.where(first_half, _mm(wm[2 * pr], vp), _mm(wm[2 * pr + 1], vp)))
            rows.append(jnp.concatenate(pairs, axis=1) + bias)
        sp = jnp.concatenate(rows, axis=0) if n_chunk > 1 else rows[0]
        return u * sp * _silu(gate)

    return f


def _sg_specs():
    full = lambda i: (0, 0)
    full3 = lambda i: (0, 0, 0)
    par = [pl.BlockSpec((1, D_SG), full), pl.BlockSpec((1, D_SG), full), pl.BlockSpec((4, SG_C, SG_C), full3),
           pl.BlockSpec((SG_C, LANES), full)]
    par_shapes = [(1, D_SG), (1, D_SG), (4, SG_C, SG_C), (SG_C, LANES)]
    return par, par_shapes


def _sg_fwd(zsg, params, tb, name):
    t = zsg.shape[0]
    f = _sg_fn(tb // SG_C)
    par, _ = _sg_specs()

    def body(z_ref, g_ref, b_ref, w_ref, bs_ref, y_ref):
        y_ref[...] = f(z_ref[...], g_ref[...], b_ref[...], w_ref[...], bs_ref[...])

    row = lambda i: (i, 0)
    return pl.pallas_call(
        body, grid=(t // tb,), in_specs=[pl.BlockSpec((tb, 3 * D_SG), row)] + par,
        out_specs=pl.BlockSpec((tb, D_SG), row), out_shape=jax.ShapeDtypeStruct((t, D_SG), F32),
        name=name, compiler_params=_cp("parallel"))(zsg, *params)


def _sg_bwd(zsg, dy, params, tb, name):
    t = zsg.shape[0]
    f = _sg_fn(tb // SG_C)
    par, par_shapes = _sg_specs()

    def body(z_ref, dy_ref, g_ref, b_ref, w_ref, bs_ref, dz_ref, dg_ref, db_ref, dw_ref, dbs_ref):
        accs = (dg_ref, db_ref, dw_ref, dbs_ref)

        @pl.when(pl.program_id(0) == 0)
        def _():
            for r in accs:
                r[...] = jnp.zeros_like(r)

        _, vjp = jax.vjp(f, z_ref[...], g_ref[...], b_ref[...], w_ref[...], bs_ref[...])
        cts = vjp(dy_ref[...])
        dz_ref[...] = cts[0].astype(BF16)
        for r, ct in zip(accs, cts[1:]):
            r[...] += ct

    row = lambda i: (i, 0)
    return pl.pallas_call(
        body, grid=(t // tb,), in_specs=[pl.BlockSpec((tb, 3 * D_SG), row), pl.BlockSpec((tb, D_SG), row)] + par,
        out_specs=[pl.BlockSpec((tb, 3 * D_SG), row)] + par,
        out_shape=[jax.ShapeDtypeStruct((t, 3 * D_SG), BF16)] + [jax.ShapeDtypeStruct(sh, F32) for sh in par_shapes],
        name=name, compiler_params=_cp("arbitrary"))(zsg, dy, *params)


def _out_fwd(x, ys, p, layer, w_out, pg, w_gate, w_ple, tb, name):
    t = x.shape[0]

    def body(x_ref, y0, y1, y2, p_ref, wo_ref, pg_ref, wg_ref, wp_ref, o_ref, x1_ref, gate_ref):
        y = jnp.concatenate([y0[...], y1[...], y2[...]], axis=1).astype(BF16)
        x1 = x_ref[...] + jnp.dot(y, wo_ref[...], preferred_element_type=F32)
        hn = _rms(x1, pg_ref[...]).astype(BF16)
        gate = jax.nn.sigmoid(jnp.dot(hn, wg_ref[...], preferred_element_type=F32))
        pp = jnp.dot(p_ref[0].astype(BF16), wp_ref[...], preferred_element_type=F32)
        o_ref[...] = x1 + gate * pp
        x1_ref[...] = x1
        gate_ref[...] = gate

    row = lambda i: (i, 0)
    full = lambda i: (0, 0)
    return pl.pallas_call(
        body, grid=(t // tb,),
        in_specs=[pl.BlockSpec((tb, D), row), pl.BlockSpec((tb, D_SSM), row), pl.BlockSpec((tb, D_DN), row),
                  pl.BlockSpec((tb, D_SG), row), pl.BlockSpec((1, tb, D_PLE), lambda i: (layer, i, 0)), pl.BlockSpec((D, D), full),
                  pl.BlockSpec((1, D), full), pl.BlockSpec((D, D), full), pl.BlockSpec((D_PLE, D), full)],
        out_specs=[pl.BlockSpec((tb, D), row)] * 3, out_shape=[jax.ShapeDtypeStruct((t, D), F32)] * 3,
        name=name, compiler_params=_cp("parallel"))(x, *ys, p, w_out, pg, w_gate, w_ple)


def _out_bwd(x1, gate, ys, p, layer, dx2, w_out, pg, w_gate, w_ple, tb, name):
    t = x1.shape[0]

    def body(x1_ref, gate_ref, y0, y1, y2, p_ref, d_ref, wo_ref, pg_ref, wg_ref, wp_ref,
             dx_ref, dy0, dy1, dy2, dwo_ref, dpg_ref, dwg_ref, dwp_ref):
        accs = (dwo_ref, dpg_ref, dwg_ref, dwp_ref)

        @pl.when(pl.program_id(0) == 0)
        def _():
            for r in accs:
                r[...] = jnp.zeros_like(r)

        y = jnp.concatenate([y0[...], y1[...], y2[...]], axis=1).astype(BF16)
        hn, rms_vjp = jax.vjp(_rms, x1_ref[...], pg_ref[...])
        hb = hn.astype(BF16)
        gate = gate_ref[...]
        pb = p_ref[0].astype(BF16)
        pp = jnp.dot(pb, wp_ref[...], preferred_element_type=F32)
        d2 = d_ref[...]
        dpp = (d2 * gate).astype(BF16)
        dlog = (d2 * pp * gate * (1.0 - gate)).astype(BF16)
        dwp_ref[...] += _dg(pb, dpp, 0, 0)
        dwg_ref[...] += _dg(hb, dlog, 0, 0)
        dx1_n, dpg = rms_vjp(_dg(dlog, wg_ref[...], 1, 1))
        dpg_ref[...] += dpg
        dx1 = d2 + dx1_n
        dx_ref[...] = dx1
        db = dx1.astype(BF16)
        dwo_ref[...] += _dg(y, db, 0, 0)
        dy = _dg(db, wo_ref[...], 1, 1)
        dy0[...] = dy[:, :D_SSM]
        dy1[...] = dy[:, D_SSM:D_SSM + D_DN]
        dy2[...] = dy[:, D_SSM + D_DN:]

    row = lambda i: (i, 0)
    full = lambda i: (0, 0)
    acts = [pl.BlockSpec((tb, D), row), pl.BlockSpec((tb, D_SSM), row), pl.BlockSpec((tb, D_DN), row), pl.BlockSpec((tb, D_SG), row)]
    wts = [pl.BlockSpec((D, D), full), pl.BlockSpec((1, D), full), pl.BlockSpec((D, D), full), pl.BlockSpec((D_PLE, D), full)]
    return pl.pallas_call(
        body, grid=(t // tb,),
        in_specs=[pl.BlockSpec((tb, D), row)] + acts
        + [pl.BlockSpec((1, tb, D_PLE), lambda i: (layer, i, 0)), pl.BlockSpec((tb, D), row)] + wts,
        out_specs=acts + wts,
        out_shape=[jax.ShapeDtypeStruct((t, n), F32) for n in (D, D_SSM, D_DN, D_SG)]
        + [jax.ShapeDtypeStruct(sh, F32) for sh in ((D, D), (1, D), (D, D), (D_PLE, D))],
        name=name, compiler_params=_cp("arbitrary"))(x1, gate, *ys, p, dx2, w_out, pg, w_gate, w_ple)


def _loss_head(x, fg, target, tb, name):
    t = x.shape[0]

    def body(x_ref, g_ref, t_ref, dx_ref, dg_ref, loss_ref):
        @pl.when(pl.program_id(0) == 0)
        def _():
            dg_ref[...] = jnp.zeros_like(dg_ref)
            loss_ref[...] = jnp.zeros_like(loss_ref)

        y, vjp = jax.vjp(_rms, x_ref[...], g_ref[...])
        err = y - t_ref[...]
        loss_ref[...] += jnp.zeros_like(loss_ref) + 0.5 * jnp.sum(err * err) / D
        dx, dg = vjp(err / D)
        dx_ref[...] = dx
        dg_ref[...] += dg

    row = lambda i: (i, 0)
    full = lambda i: (0, 0)
    return pl.pallas_call(
        body, grid=(t // tb,),
        in_specs=[pl.BlockSpec((tb, D), row), pl.BlockSpec((1, D), full), pl.BlockSpec((tb, D), row)],
        out_specs=[pl.BlockSpec((tb, D), row), pl.BlockSpec((1, D), full), pl.BlockSpec((1, LANES), full)],
        out_shape=[jax.ShapeDtypeStruct((t, D), F32), jax.ShapeDtypeStruct((1, D), F32), jax.ShapeDtypeStruct((1, LANES), F32)],
        name=name, compiler_params=_cp("arbitrary"))(x, fg, target)


def _hbm_specs(n):
    return [pl.BlockSpec(memory_space=pl.ANY)] * n


def _gather_protocol(place, ins, outs, send_sems, recv_sems, local_sems):
    n = len(ins)
    x, y, c, other_x, other_y, other_c = place
    me, sibling = (x, y, c), (x, y, other_c)
    chips = [(other_x, y), (x, other_y), (other_x, other_y)]

    def slot(a, px, py, pc):
        return outs[a].at[4 * px + 2 * py + pc]

    def copy(a, k, blk, to, src=None):
        return pltpu.make_async_remote_copy(
            src_ref=slot(a, *blk) if src is None else src, dst_ref=slot(a, *blk),
            send_sem=send_sems.at[7 * a + k], recv_sem=recv_sems.at[7 * a + k],
            device_id=to, device_id_type=pl.DeviceIdType.MESH)

    def own_copies():
        mines = [pltpu.make_async_copy(ins[a], slot(a, *me), local_sems.at[a]) for a in range(n)]
        first = []
        for a in range(n):
            first.append(copy(a, 0, me, sibling, src=ins[a]))
            first += [copy(a, 1 + j, me, (*chip, c), src=ins[a]) for j, chip in enumerate(chips)]
        return mines, first

    mines, first = own_copies()
    for cp in mines + first:
        cp.start()
    yield
    mines, first = own_copies()
    passed = []
    for j, chip in enumerate(chips):
        for a in range(n):
            copy(a, 1 + j, (*chip, c), me).wait_recv()
            onward = copy(a, 4 + j, (*chip, c), sibling)
            onward.start()
            passed.append(onward)
    for a in range(n):
        copy(a, 0, sibling, me).wait_recv()
    for j, chip in enumerate(chips):
        for a in range(n):
            copy(a, 4 + j, (*chip, other_c), me).wait_recv()
    for cp in first + passed:
        cp.wait_send()
    for cp in mines:
        cp.wait()


def _mesh_place():
    x, y, c = lax.axis_index("x"), lax.axis_index("y"), lax.axis_index("c")
    return x, y, c, 1 - x, 1 - y, 1 - c


def _gather_sems(n):
    return [pltpu.SemaphoreType.DMA((7 * n,)), pltpu.SemaphoreType.DMA((7 * n,)), pltpu.SemaphoreType.DMA((n,))]


def _all_gather(blocks, name):
    n = len(blocks)

    def body(*refs):
        for _ in _gather_protocol(_mesh_place(), refs[:n], refs[n:2 * n], *refs[2 * n:]):
            pass

    return pl.pallas_call(
        body, out_shape=[jax.ShapeDtypeStruct((N_DEV, *b.shape), b.dtype) for b in blocks],
        in_specs=_hbm_specs(n), out_specs=_hbm_specs(n), scratch_shapes=_gather_sems(n), name=name)(*blocks)


def _pair_exchange(gs, name):
    n = len(gs)

    def body(*refs):
        ins, recvs = refs[:n], refs[n:2 * n]
        send_sems, recv_sems = refs[2 * n:]
        x, y, c = lax.axis_index("x"), lax.axis_index("y"), lax.axis_index("c")
        remote = [pltpu.make_async_remote_copy(
            src_ref=ins[a], dst_ref=recvs[a], send_sem=send_sems.at[a], recv_sem=recv_sems.at[a],
            device_id=(x, y, 1 - c), device_id_type=pl.DeviceIdType.MESH) for a in range(n)]
        for cp in remote:
            cp.start()
        for cp in remote:
            cp.wait_send()
            cp.wait_recv()

    return pl.pallas_call(
        body, out_shape=[jax.ShapeDtypeStruct(g.shape, g.dtype) for g in gs], in_specs=_hbm_specs(n), out_specs=_hbm_specs(n),
        scratch_shapes=[pltpu.SemaphoreType.DMA((n,)), pltpu.SemaphoreType.DMA((n,))],
        name=name)(*gs)


def _chip_protocol(place, ins, outs, send_sems, recv_sems, local_sems):
    n = len(ins)
    x, y, c, other_x, other_y, _ = place

    def copies():
        my_chip = 2 * x + y
        local = [pltpu.make_async_copy(ins[a].at[my_chip], outs[a].at[my_chip], local_sems.at[a]) for a in range(n)]
        remote = []
        for j in range(1, 4):
            px = other_x if j & 2 else x
            py = other_y if j & 1 else y
            for a in range(n):
                remote.append(pltpu.make_async_remote_copy(
                    src_ref=ins[a].at[2 * px + py], dst_ref=outs[a].at[my_chip],
                    send_sem=send_sems.at[3 * a + j - 1], recv_sem=recv_sems.at[3 * a + j - 1],
                    device_id=(px, py, c), device_id_type=pl.DeviceIdType.MESH))
        return local, remote

    local, remote = copies()
    for cp in local + remote:
        cp.start()
    yield
    local, remote = copies()
    for cp in remote:
        cp.wait_send()
        cp.wait_recv()
    for cp in local:
        cp.wait()


def _chip_sems(n):
    return [pltpu.SemaphoreType.DMA((3 * n,)), pltpu.SemaphoreType.DMA((3 * n,)), pltpu.SemaphoreType.DMA((n,))]


def _chip_exchange(ps, gather, name):
    n, m = len(ps), len(gather)

    def body(*refs):
        ins, g_ins, outs, g_outs = refs[:n], refs[n:n + m], refs[n + m:2 * n + m], refs[2 * n + m:2 * (n + m)]
        place = _mesh_place()
        exchanging = _chip_protocol(place, ins, outs, *refs[2 * (n + m):2 * (n + m) + 3])
        gathering = _gather_protocol(place, g_ins, g_outs, *refs[2 * (n + m) + 3:])
        next(exchanging)
        for _ in gathering:
            pass
        for _ in exchanging:
            pass

    outs = pl.pallas_call(
        body, out_shape=[jax.ShapeDtypeStruct(q.shape, q.dtype) for q in ps]
        + [jax.ShapeDtypeStruct((N_DEV, *b.shape), b.dtype) for b in gather],
        in_specs=_hbm_specs(n + m), out_specs=_hbm_specs(n + m),
        scratch_shapes=_chip_sems(n) + _gather_sems(m), name=name)(*ps, *gather)
    return outs[:n], outs[n:]


def _row_block(rows, bytes_per_row):
    best = None
    for rb in range(16, rows + 1, 16):
        if rows % rb == 0 and rb * bytes_per_row <= ELEMENTWISE_STEP_BYTES:
            best = rb
    return rows if best is None else best


def _add_pair(own, recv, name):
    shape = own.shape
    last = shape[-1]
    rows = own.size // last
    rb = _row_block(rows, 3 * 4 * (-(-last // LANES) * LANES))

    def body(a_ref, b_ref, o_ref):
        o_ref[...] = (a_ref[...].astype(F32) + b_ref[...].astype(F32)).astype(o_ref.dtype)

    row = lambda i: (i, 0)
    out = pl.pallas_call(
        body, grid=(rows // rb,), in_specs=[pl.BlockSpec((rb, last), row)] * 2, out_specs=pl.BlockSpec((rb, last), row),
        out_shape=jax.ShapeDtypeStruct((rows, last), own.dtype), name=name,
        compiler_params=_cp("parallel"))(own.reshape(rows, last), recv.reshape(rows, last))
    return out.reshape(shape)


def _sum_adamw(gk, w, m, v, name):
    shape = w.shape
    n_part = gk.shape[0]
    last = shape[-1]
    rows = w.size // last
    rb = _row_block(rows, (n_part + 7) * 4 * (-(-last // LANES) * LANES))

    def body(g_ref, w_ref, m_ref, v_ref, go_ref, d_ref, mo_ref, vo_ref):
        g = g_ref[0].astype(F32)
        for k in range(1, n_part):
            g = g + g_ref[k].astype(F32)
        mn = ADAM_B1 * m_ref[...] + (1.0 - ADAM_B1) * g
        vn = ADAM_B2 * v_ref[...] + (1.0 - ADAM_B2) * jnp.square(g)
        m_hat = mn / (1.0 - ADAM_B1 ** ADAM_STEP)
        v_hat = vn / (1.0 - ADAM_B2 ** ADAM_STEP)
        go_ref[...] = g
        d_ref[...] = -ADAM_LR * (m_hat / (jnp.sqrt(v_hat) + ADAM_EPS) + ADAM_WD * w_ref[...])
        mo_ref[...] = mn
        vo_ref[...] = vn

    row = lambda i: (i, 0)
    outs = pl.pallas_call(
        body, grid=(rows // rb,),
        in_specs=[pl.BlockSpec((n_part, rb, last), lambda i: (0, i, 0))] + [pl.BlockSpec((rb, last), row)] * 3,
        out_specs=[pl.BlockSpec((rb, last), row)] * 4,
        out_shape=[jax.ShapeDtypeStruct((rows, last), F32)] * 4,
        name=name, compiler_params=_cp("parallel"))(gk.reshape(n_part, rows, last), *[a.reshape(rows, last) for a in (w, m, v)])
    return [o.reshape(shape) for o in outs]


def _seg_rows(shape):
    n = 1
    for d in shape:
        n *= d
    return -(-n // (8 * LANES)) * 8


def _pack(arrs):
    segs = []
    for a in arrs:
        r = _seg_rows(a.shape)
        segs.append(jnp.pad(a.reshape(-1).astype(F32), (0, r * LANES - a.size)).reshape(r, LANES))
    rows = sum(s.shape[0] for s in segs)
    total = -(-rows // PACK_ROWS) * PACK_ROWS
    if total > rows:
        segs.append(jnp.zeros((total - rows, LANES), F32))
    return jnp.concatenate(segs, axis=0)


def _unpack(pack, shapes):
    out, off = [], 0
    for sh in shapes:
        r = _seg_rows(sh)
        n = 1
        for d in sh:
            n *= d
        out.append(pack[off:off + r].reshape(-1)[:n].reshape(sh))
        off += r
    return out


def _to_dest_blocks(full, axis, dtype):
    sh = list(full.shape)
    sh[axis:axis + 1] = [N_DEV // 2, 2, sh[axis] // N_DEV]
    return jnp.moveaxis(full.reshape(sh), (axis, axis + 1), (1, 0)).astype(dtype)


def _from_gathered(g, axis):
    m = jnp.moveaxis(g, 0, axis)
    sh = list(m.shape)
    sh[axis:axis + 2] = [sh[axis] * sh[axis + 1]]
    return m.reshape(sh)


D_IN = 3336
W_IN_SHARD = D_IN // N_DEV
W_IN_MOVES = ((0, 2048, 0), (2048, 2056, 3328), (2056, D_IN, 2048))


def _w_in_windows(k):
    lo, hi = k * W_IN_SHARD, (k + 1) * W_IN_SHARD
    out = []
    for a, b, mine in W_IN_MOVES:
        a2, b2 = max(a, lo), min(b, hi)
        if b2 > a2:
            out.append((a2 - lo, b2 - a2, mine + a2 - a))
    return out


def _assemble_w_in(gathered, name):
    depth = gathered.shape[1]
    rb = 256

    def body(g_ref, o_ref):
        o_ref[0, :, D_IN:] = jnp.zeros((rb, ZW - D_IN), o_ref.dtype)
        for k in range(N_DEV):
            for off, width, mine in _w_in_windows(k):
                o_ref[0, :, mine:mine + width] = g_ref[k, 0, :, off:off + width]

    return pl.pallas_call(
        body, grid=(depth, D // rb),
        in_specs=[pl.BlockSpec((N_DEV, 1, rb, W_IN_SHARD), lambda l, i: (0, l, i, 0))],
        out_specs=pl.BlockSpec((1, rb, ZW), lambda l, i: (l, i, 0)),
        out_shape=jax.ShapeDtypeStruct((depth, D, ZW), gathered.dtype),
        name=name, compiler_params=_cp("parallel", "parallel"))(gathered)


def _split_dw_in(dws, name):
    depth = len(dws)
    rb = 128

    def body(*refs):
        o_ref = refs[-1]
        for l in range(depth):
            pieces = refs[5 * l:5 * l + 5]
            for k in range(N_DEV):
                for off, width, mine in _w_in_windows(k):
                    for p_ref, start, n in zip(pieces, Z_OFFSETS, Z_PIECES):
                        a, b = max(mine, start), min(mine + width, start + n)
                        if b > a:
                            o_ref[k % 2, k // 2, l, :, off + a - mine:off + b - mine] = (
                                p_ref[:, a - start:b - start].astype(o_ref.dtype))

    row = lambda i: (i, 0)
    flat = [piece for layer in dws for piece in layer]
    return pl.pallas_call(
        body, grid=(D // rb,),
        in_specs=[pl.BlockSpec((rb, n), row) for _ in range(depth) for n in Z_PIECES],
        out_specs=pl.BlockSpec((2, N_DEV // 2, depth, rb, W_IN_SHARD), lambda i: (0, 0, 0, i, 0)),
        out_shape=jax.ShapeDtypeStruct((2, N_DEV // 2, depth, D, W_IN_SHARD), WIRE['w_in']),
        name=name, compiler_params=_cp("parallel"))(*flat)


EARLY = ('w_in', 'ssm_w_glu', 'dn_conv_w')
LATE = ('w_out', 'w_ple_gate', 'w_ple')


def _layer_weights(names, gathered, layer):
    full = {}
    for n, g in zip(names, gathered):
        full[n] = _assemble_w_in(g, f"assemble_w_in_l{layer}") if n == 'w_in' else _from_gathered(g, SHARDED[n])
        if n == 'ssm_w_glu':
            full[n] = full[n].astype(F32)
    return full


def _pair_reduce(names, layer_grads, tag):
    dest = [_split_dw_in([layer_grads[n]], f"split_dw_in_{tag}") if n == 'w_in'
            else _to_dest_blocks(layer_grads[n][None], SHARDED[n], WIRE[n]) for n in names]
    c = lax.axis_index("c")
    own = [lax.dynamic_index_in_dim(d, c, 0, keepdims=False) for d in dest]
    for_sibling = [lax.dynamic_index_in_dim(d, 1 - c, 0, keepdims=False) for d in dest]
    from_sibling = _pair_exchange(for_sibling, f"grads_pair_exchange_{tag}")
    return [_add_pair(a, b, f"grads_pair_sum_{n}_{tag}") for n, a, b in zip(names, own, from_sibling)]


def _local_step(x, p, wts, sharded, target, shards=None, reduce_early=False):
    sharded = [dict(d or {}) for d in sharded]
    bl, s, _ = x.shape
    t = bl * s
    depth = p.shape[0]
    tb, sg_tb = TB, SG_TB

    def by_example(a):
        return a.reshape(bl, s, a.shape[-1])

    def flat(a):
        return a.reshape(t, a.shape[-1])

    xs = [x.reshape(t, D)]
    p_all = p.reshape(depth, t, D_PLE)
    saved = []
    for i in range(depth):
        li = f"l{i}"
        ng = wts['norm_g'][i].reshape(1, D)
        lw = sharded[i]
        w_in = lw['w_in'][0]
        s5_par_in = (wts['ssm_a_re'][i], wts['ssm_a_im'][i], wts['ssm_b_re'][i], wts['ssm_b_im'][i],
                     wts['ssm_c_re'][i], wts['ssm_c_im'][i], wts['ssm_d'][i], wts['ssm_log_step'][i])
        tabs, tab_vjp = jax.vjp(_s5_tables, *s5_par_in)
        s5_par = (*tabs, lw['ssm_w_glu'][0], wts['ssm_b_glu'][i].reshape(1, D_SSM))
        s5_const = _s5_powers(wts['ssm_a_re'][i], wts['ssm_a_im'][i], wts['ssm_log_step'][i])
        conv8 = jnp.pad(lw['dn_conv_w'][0], ((0, 4), (0, 0)))
        dn_par = (jnp.repeat(wts['dn_a_log'][i], DH).reshape(1, D_DN), jnp.repeat(wts['dn_dt_bias'][i], DH).reshape(1, D_DN),
                  wts['dn_norm_g'][i].reshape(1, DH))
        sg_par = (wts['sg_ln_g'][i].reshape(1, D_SG), wts['sg_ln_b'][i].reshape(1, D_SG), wts['sg_w'][i],
                  jnp.pad(jnp.transpose(wts['sg_b'][i]), ((0, 0), (0, LANES - 4))))

        z_ssm, z_qkv, z_gdn, z_sg, z_ab, qkvn = _in_proj_fwd(xs[i], ng, w_in, conv8, s, tb, f"in_proj_fwd_{li}")
        wanted = [(i, n) for n in LATE if n not in lw]
        if i + 1 < depth:
            wanted += [(i + 1, n) for n in SHARDED_ORDER if n not in sharded[i + 1]]
        (y_ssm, carries, h_all, y_dn, states, tinvs), gathered = _mix_fwd(
            by_example(z_ssm), s5_par, s5_const, by_example(qkvn), by_example(z_ab), by_example(z_gdn), dn_par, bl, s,
            f"mix_fwd_{li}", gather=[shards[l][n] for l, n in wanted])
        for l in sorted({ll for ll, _ in wanted}):
            names = [n for ll, n in wanted if ll == l]
            sharded[l].update(_layer_weights(names, [g for (ll, _), g in zip(wanted, gathered) if ll == l], l))
        out_par = (lw['w_out'][0].astype(BF16), wts['ple_norm_g'][i].reshape(1, D), lw['w_ple_gate'][0].astype(BF16),
                   lw['w_ple'][0].astype(BF16))
        y_sg = _sg_fwd(z_sg, sg_par, sg_tb, f"sg_fwd_{li}")
        ys = (flat(y_ssm), flat(y_dn), y_sg)
        x_next, x1, gate = _out_fwd(xs[i], ys, p_all, i, *out_par, tb, f"out_fwd_{li}")
        xs.append(x_next)
        saved.append(dict(ng=ng, w_in=w_in, tab_vjp=tab_vjp, s5_par=s5_par, s5_const=s5_const, conv8=conv8, dn_par=dn_par,
                          sg_par=sg_par, out_par=out_par, z=(z_ssm, z_qkv, z_gdn, z_sg, z_ab), carries=carries,
                          h_all=h_all, qkvn=qkvn, x1=x1, gate=gate,
                          states=states, tinvs=tinvs, ys=ys))

    dx, dfg, loss_vec = _loss_head(xs[depth], wts['final_norm_g'].reshape(1, D), target.reshape(t, D), tb, "loss_head")
    grads = {n: [None] * depth for n in WEIGHTS if n != 'final_norm_g'}
    grads['final_norm_g'] = dfg.reshape(D)
    pair_sums, by_chip = {}, {}
    for i in reversed(range(depth)):
        li = f"l{i}"
        sv = saved[i]
        z_ssm, z_qkv, z_gdn, z_sg, z_ab = sv['z']
        dx_res, dy_ssm, dy_dn, dy_sg, dwo, dpg, dwg, dwp = _out_bwd(sv['x1'], sv['gate'], sv['ys'], p_all, i, dx, *sv['out_par'], tb,
                                                                    f"out_bwd_{li}")
        dz_sg, dlng, dlnb, dsgw, dbsp = _sg_bwd(z_sg, dy_sg, sv['sg_par'], sg_tb, f"sg_bwd_{li}")
        if reduce_early:
            for n, gsum in zip(LATE, _pair_reduce(LATE, {'w_out': dwo, 'w_ple_gate': dwg, 'w_ple': dwp}, f"late_{li}")):
                pair_sums[(i, n)] = gsum
        travelling = [key for key in pair_sums if key not in by_chip]
        (dz_ssm, dbb, dcb, dlam, ddv, dwglu, dbglu), (dqkvn, dz_ab, dz_gdn, dal, ddt, dng), exchanged = _mix_bwd(
            by_example(z_ssm), sv['carries'], sv['h_all'], by_example(dy_ssm), sv['s5_par'], sv['s5_const'],
            by_example(sv['qkvn']), by_example(z_ab), by_example(z_gdn), sv['states'], sv['tinvs'], by_example(dy_dn),
            sv['dn_par'], bl, s, f"mix_bwd_{li}", exchange=[pair_sums[key] for key in travelling])
        by_chip.update(zip(travelling, exchanged))
        dx, dnorm, dz_qkv, dconv = _in_proj_bwd_dx(xs[i], sv['ng'], sv['w_in'], sv['conv8'], flat(dz_ssm), flat(dz_gdn), dz_sg,
                                                   flat(dz_ab), z_qkv, flat(dqkvn), dx_res, s, tb, f"in_proj_bwd_dx_{li}")
        dzs = (flat(dz_ssm), dz_qkv, flat(dz_gdn), dz_sg, flat(dz_ab))
        dws = _in_proj_bwd_dw(xs[i], sv['ng'], dzs, min(TB_DW, t), f"in_proj_bwd_dw_{li}")
        ds5 = sv['tab_vjp']((dbb, dcb, dlam, ddv))
        for n, gval in zip(('ssm_a_re', 'ssm_a_im', 'ssm_b_re', 'ssm_b_im', 'ssm_c_re', 'ssm_c_im', 'ssm_d', 'ssm_log_step'), ds5):
            grads[n][i] = gval
        grads['norm_g'][i] = dnorm.reshape(D)
        grads['w_in'][i] = dws
        grads['ssm_w_glu'][i] = dwglu
        grads['ssm_b_glu'][i] = dbglu.reshape(D_SSM)
        grads['dn_conv_w'][i] = dconv[:4]
        grads['dn_a_log'][i] = dal.reshape(H, DH).sum(axis=1)
        grads['dn_dt_bias'][i] = ddt.reshape(H, DH).sum(axis=1)
        grads['dn_norm_g'][i] = dng.reshape(DH)
        grads['sg_ln_g'][i] = dlng.reshape(D_SG)
        grads['sg_ln_b'][i] = dlnb.reshape(D_SG)
        grads['sg_w'][i] = dsgw
        grads['sg_b'][i] = jnp.transpose(dbsp[:, :4])
        grads['w_out'][i] = dwo
        grads['ple_norm_g'][i] = dpg.reshape(D)
        grads['w_ple_gate'][i] = dwg
        grads['w_ple'][i] = dwp
        if reduce_early:
            for n, gsum in zip(EARLY, _pair_reduce(EARLY, {n: grads[n][i] for n in EARLY}, f"early_{li}")):
                pair_sums[(i, n)] = gsum
    grads = {n: (g if n in ('final_norm_g', 'w_in') else jnp.stack(g)) for n, g in grads.items()}
    return loss_vec[0, 0], dx.reshape(bl, s, D), grads, {k: v for k, v in pair_sums.items() if k not in by_chip}, by_chip


def kernel(x, p, norm_g, w_in, ssm_a_re, ssm_a_im, ssm_b_re, ssm_b_im, ssm_c_re, ssm_c_im, ssm_d, ssm_log_step, ssm_w_glu, ssm_b_glu, dn_conv_w, dn_a_log, dn_dt_bias, dn_norm_g, sg_ln_g, sg_ln_b, sg_w, sg_b, w_out, ple_norm_g, w_ple_gate, w_ple, final_norm_g, loss_target, m_norm_g, m_w_in, m_ssm_a_re, m_ssm_a_im, m_ssm_b_re, m_ssm_b_im, m_ssm_c_re, m_ssm_c_im, m_ssm_d, m_ssm_log_step, m_ssm_w_glu, m_ssm_b_glu, m_dn_conv_w, m_dn_a_log, m_dn_dt_bias, m_dn_norm_g, m_sg_ln_g, m_sg_ln_b, m_sg_w, m_sg_b, m_w_out, m_ple_norm_g, m_w_ple_gate, m_w_ple, m_final_norm_g, v_norm_g, v_w_in, v_ssm_a_re, v_ssm_a_im, v_ssm_b_re, v_ssm_b_im, v_ssm_c_re, v_ssm_c_im, v_ssm_d, v_ssm_log_step, v_ssm_w_glu, v_ssm_b_glu, v_dn_conv_w, v_dn_a_log, v_dn_dt_bias, v_dn_norm_g, v_sg_ln_g, v_sg_ln_b, v_sg_w, v_sg_b, v_w_out, v_ple_norm_g, v_w_ple_gate, v_w_ple, v_final_norm_g):
    w_loc = dict(zip(WEIGHTS, (norm_g, w_in, ssm_a_re, ssm_a_im, ssm_b_re, ssm_b_im, ssm_c_re, ssm_c_im, ssm_d, ssm_log_step,
                               ssm_w_glu, ssm_b_glu, dn_conv_w, dn_a_log, dn_dt_bias, dn_norm_g, sg_ln_g, sg_ln_b, sg_w, sg_b,
                               w_out, ple_norm_g, w_ple_gate, w_ple, final_norm_g)))
    m_loc = dict(zip(WEIGHTS, (m_norm_g, m_w_in, m_ssm_a_re, m_ssm_a_im, m_ssm_b_re, m_ssm_b_im, m_ssm_c_re, m_ssm_c_im, m_ssm_d,
                               m_ssm_log_step, m_ssm_w_glu, m_ssm_b_glu, m_dn_conv_w, m_dn_a_log, m_dn_dt_bias, m_dn_norm_g,
                               m_sg_ln_g, m_sg_ln_b, m_sg_w, m_sg_b, m_w_out, m_ple_norm_g, m_w_ple_gate, m_w_ple, m_final_norm_g)))
    v_loc = dict(zip(WEIGHTS, (v_norm_g, v_w_in, v_ssm_a_re, v_ssm_a_im, v_ssm_b_re, v_ssm_b_im, v_ssm_c_re, v_ssm_c_im, v_ssm_d,
                               v_ssm_log_step, v_ssm_w_glu, v_ssm_b_glu, v_dn_conv_w, v_dn_a_log, v_dn_dt_bias, v_dn_norm_g,
                               v_sg_ln_g, v_sg_ln_b, v_sg_w, v_sg_b, v_w_out, v_ple_norm_g, v_w_ple_gate, v_w_ple, v_final_norm_g)))

    depth = p.shape[0]
    shards = [{n: w_loc[n][l:l + 1].astype(WIRE[n]) for n in SHARDED_ORDER} for l in range(depth)]
    first = _layer_weights(EARLY, _all_gather([shards[0][n] for n in EARLY], "gather_weights_l0"), 0)

    loss_part, grad_x, grads, left, by_chip = _local_step(
        x, p, w_loc, [first] + [None] * (depth - 1), loss_target, shards, reduce_early=True)

    rep_pack = _pack([grads[n] for n in REP_NARROW]).astype(WIRE['replicated'])
    fine_pack = _pack([grads['final_norm_g'], loss_part.reshape(1)])
    exchanged, (rep_recv, fine_recv) = _chip_exchange(list(left.values()), [rep_pack, fine_pack], "grads_chip_exchange_l0")
    by_chip.update(zip(left, exchanged))
    by_chip = [jnp.concatenate([by_chip[(l, n)] for l in range(depth)], axis=1) for n in SHARDED_ORDER]

    outs = {k: {} for k in 'gdmv'}
    for n, gk in zip(SHARDED_ORDER, by_chip):
        for k, o in zip('gdmv', _sum_adamw(gk, w_loc[n], m_loc[n], v_loc[n], f"adamw_{n}")):
            outs[k][n] = o
    rep_out = _sum_adamw(rep_recv, *[_pack([src[n] for n in REP_NARROW]) for src in (w_loc, m_loc, v_loc)], "adamw_replicated")
    for k, rep_p in zip('gdmv', rep_out):
        outs[k].update(zip(REP_NARROW, _unpack(rep_p, [w_loc[n].shape for n in REP_NARROW])))
    one = jnp.zeros((1,), F32)
    fine_out = _sum_adamw(fine_recv, *[_pack([src['final_norm_g'], one]) for src in (w_loc, m_loc, v_loc)], "adamw_final_norm")
    for k, fine_p in zip('gdmv', fine_out):
        outs[k].update(zip(['final_norm_g', 'loss'], _unpack(fine_p, [w_loc['final_norm_g'].shape, (1,)])))
    loss = outs['g']['loss'].reshape(())
    return (loss, grad_x, *[outs['g'][n] for n in WEIGHTS], *[outs['d'][n] for n in WEIGHTS],
            *[outs['m'][n] for n in WEIGHTS], *[outs['v'][n] for n in WEIGHTS])
```

```python
import functools

import jax
import jax.numpy as jnp
from jax import lax
from jax.experimental import pallas as pl
from jax.experimental.pallas import tpu as pltpu

F32 = jnp.float32
BF16 = jnp.bfloat16
EPS = 1e-6

D = 1024
D_PLE = 256
D_SSM = 256
D_DN = 512
D_SG = 256
G = 16
CG = 16
NS = 64
NRE = G * NS
H = 4
DH = 128
DN_C = 128
SG_C = 128
ZW = 3456
Z_PIECES = (512, 1536, 512, 768, 128)
N_DEV = 8
LANES = 128
PACK_ROWS = 256
VMEM_LIMIT = 56 * 1024 * 1024
ELEMENTWISE_STEP_BYTES = 4 * 1024 * 1024
TB = 256
SG_TB = 512
TB_DW = 512

ADAM_LR = 0.001
ADAM_B1 = 0.9
ADAM_B2 = 0.999
ADAM_EPS = 1e-08
ADAM_WD = 0.01
ADAM_STEP = 10

MIX_HEAD_START = 3
S5_L = 128
S5_GROUP = 8
S5_SHIFTS = (1, 2, 4)

WEIGHTS = ['norm_g', 'w_in', 'ssm_a_re', 'ssm_a_im', 'ssm_b_re', 'ssm_b_im', 'ssm_c_re', 'ssm_c_im', 'ssm_d',
           'ssm_log_step', 'ssm_w_glu', 'ssm_b_glu', 'dn_conv_w', 'dn_a_log', 'dn_dt_bias', 'dn_norm_g', 'sg_ln_g',
           'sg_ln_b', 'sg_w', 'sg_b', 'w_out', 'ple_norm_g', 'w_ple_gate', 'w_ple', 'final_norm_g']
SHARDED = {'w_in': 2, 'ssm_w_glu': 1, 'dn_conv_w': 2, 'w_out': 1, 'w_ple_gate': 1, 'w_ple': 2}
SHARDED_ORDER = ['w_in', 'ssm_w_glu', 'dn_conv_w', 'w_out', 'w_ple_gate', 'w_ple']
WIRE = {'w_in': BF16, 'ssm_w_glu': BF16, 'dn_conv_w': F32, 'w_out': BF16, 'w_ple_gate': BF16, 'w_ple': BF16,
        'replicated': BF16}
REPLICATED_ORDER = [n for n in WEIGHTS if n not in SHARDED]
REP_NARROW = [n for n in REPLICATED_ORDER if n != 'final_norm_g']


def _cp(*sem):
    return pltpu.CompilerParams(dimension_semantics=sem, vmem_limit_bytes=VMEM_LIMIT)


def _dg(a, b, ca, cb, precision=None):
    return lax.dot_general(a, b, (((ca,), (cb,)), ((), ())), precision=precision, preferred_element_type=F32)


@jax.custom_vjp
def _mm(a, b):
    return _dg(a.astype(BF16), b.astype(BF16), 1, 0)


def _mm_fwd(a, b):
    return _mm(a, b), (a, b)


def _mm_bwd(res, g):
    a, b = res
    gb = g.astype(BF16)
    return _dg(gb, b.astype(BF16), 1, 1), _dg(a.astype(BF16), gb, 0, 0)


_mm.defvjp(_mm_fwd, _mm_bwd)


@jax.custom_vjp
def _mm_nt(a, b):
    return _dg(a.astype(BF16), b.astype(BF16), 1, 1)


def _mm_nt_fwd(a, b):
    return _mm_nt(a, b), (a, b)


def _mm_nt_bwd(res, g):
    a, b = res
    gb = g.astype(BF16)
    return _dg(gb, b.astype(BF16), 1, 0), _dg(gb, a.astype(BF16), 0, 0)


_mm_nt.defvjp(_mm_nt_fwd, _mm_nt_bwd)


@jax.custom_vjp
def _mm_tn(a, b):
    return _dg(a.astype(BF16), b.astype(BF16), 0, 0)


def _mm_tn_fwd(a, b):
    return _mm_tn(a, b), (a, b)


def _mm_tn_bwd(res, g):
    a, b = res
    gb = g.astype(BF16)
    return _dg(b.astype(BF16), gb, 1, 1), _dg(a.astype(BF16), gb, 1, 0)


_mm_tn.defvjp(_mm_tn_fwd, _mm_tn_bwd)


def _split(x, n):
    pieces = []
    for _ in range(n - 1):
        hi = x.astype(BF16)
        pieces.append(hi)
        x = x - hi.astype(F32)
    pieces.append(x.astype(BF16))
    return pieces


def _dg3(a, b, ca, cb):
    a_hi, a_lo = _split(a, 2)
    b_hi, b_lo = _split(b, 2)
    return _dg(a_hi, b_hi, ca, cb) + (_dg(a_hi, b_lo, ca, cb) + _dg(a_lo, b_hi, ca, cb))


@jax.custom_vjp
def _dot3(a, b):
    return _dg3(a, b, 1, 0)


def _dot3_fwd(a, b):
    return _dot3(a, b), (a, b)


def _dot3_bwd(res, g):
    a, b = res
    return _dg3(g, b, 1, 1), _dg3(a, g, 0, 0)


_dot3.defvjp(_dot3_fwd, _dot3_bwd)


def _dg_sel(x, e, cx, ce, x_first):
    eb = e.astype(BF16)
    out = None
    for piece in reversed(_split(x, 3)):
        term = _dg(piece, eb, cx, ce) if x_first else _dg(eb, piece, ce, cx)
        out = term if out is None else out + term
    return out


@jax.custom_vjp
def _sel_r(x, e):
    return _dg_sel(x, e, 1, 0, True)


def _sel_r_fwd(x, e):
    return _sel_r(x, e), e


def _sel_r_bwd(e, g):
    return _dg_sel(g, e, 1, 1, True), jnp.zeros_like(e)


_sel_r.defvjp(_sel_r_fwd, _sel_r_bwd)


@jax.custom_vjp
def _sel_l(e, x):
    return _dg_sel(x, e, 0, 1, False)


def _sel_l_fwd(e, x):
    return _sel_l(e, x), e


def _sel_l_bwd(e, g):
    return jnp.zeros_like(e), _dg_sel(g, e, 0, 0, False)


_sel_l.defvjp(_sel_l_fwd, _sel_l_bwd)


def _rms(x, g):
    return x * lax.rsqrt(jnp.mean(x * x, axis=-1, keepdims=True) + EPS) * g


def _silu(x):
    return x * jax.nn.sigmoid(x)


Z_OFFSETS = (0, 512, 2048, 2560, 3328)


def _dn_post(c):
    s = _silu(c)
    parts = []
    for j in range(12):
        xj = s[:, j * DH:(j + 1) * DH]
        if j < 8:
            xj = xj * lax.rsqrt(jnp.sum(xj * xj, axis=-1, keepdims=True) + EPS)
        if j < 4:
            xj = xj * (DH ** -0.5)
        parts.append(xj)
    return jnp.concatenate(parts, axis=1)


def _dn_conv(ext, cw_ref, rows):
    c = None
    for k in range(4):
        sh = ext if k == 3 else pltpu.roll(ext, 3 - k, 0)
        term = cw_ref[k:k + 1, :] * sh[ext.shape[0] - rows:, :]
        c = term if c is None else c + term
    return c


def _dn_prep_vjp(prev, cur, nxt, d_cur, d_nxt, cw_ref, tb):
    ext = jnp.concatenate([prev, cur, nxt], axis=0)
    shifted = [ext if k == 3 else pltpu.roll(ext, 3 - k, 0) for k in range(4)]
    c2 = None
    for k in range(4):
        term = cw_ref[k:k + 1, :] * shifted[k][8:, :]
        c2 = term if c2 is None else c2 + term
    _, vjp = jax.vjp(_dn_post, c2)
    (dc2,) = vjp(jnp.concatenate([d_cur, d_nxt], axis=0))
    dz, dcw = None, []
    for k in range(4):
        up = dc2 if k == 3 else pltpu.roll(dc2, tb + 8 - (3 - k), 0)
        term = cw_ref[k:k + 1, :] * up[:tb, :]
        dz = term if dz is None else dz + term
        dcw.append(jnp.sum(dc2[:tb, :] * shifted[k][8:8 + tb, :], axis=0, keepdims=True))
    return dz, dcw


def _in_proj_fwd(x, g, w, conv_w8, s, tb, name):
    t = x.shape[0]
    n_s = s // tb
    w3 = 3 * D_DN
    q0, q1 = Z_OFFSETS[1], Z_OFFSETS[2]

    def body(x_ref, g_ref, w_ref, cw_ref, zs_ref, zq_ref, zg_ref, zsg_ref, zab_ref, qkvn_ref, halo):
        h = _rms(x_ref[...], g_ref[...]).astype(BF16)
        zq = jnp.dot(h, w_ref[:, q0:q1], preferred_element_type=F32)
        zq_ref[...] = zq
        prev = jnp.where(pl.program_id(0) % n_s == 0, 0.0, halo[...])
        qkvn_ref[...] = _dn_post(_dn_conv(jnp.concatenate([prev, zq], axis=0), cw_ref, tb))
        halo[...] = zq[tb - 8:, :]
        zs_ref[...] = jnp.dot(h, w_ref[:, :q0], preferred_element_type=F32)
        rest = jnp.dot(h, w_ref[:, q1:], preferred_element_type=F32)
        zg_ref[...] = rest[:, :Z_PIECES[2]]
        zsg_ref[...] = rest[:, Z_PIECES[2]:Z_PIECES[2] + Z_PIECES[3]]
        zab_ref[...] = rest[:, Z_PIECES[2] + Z_PIECES[3]:]

    row = lambda i: (i, 0)
    full = lambda i: (0, 0)
    widths = Z_PIECES + (w3,)
    return pl.pallas_call(
        body, grid=(t // tb,),
        in_specs=[pl.BlockSpec((tb, D), row), pl.BlockSpec((1, D), full), pl.BlockSpec((D, ZW), full), pl.BlockSpec((8, w3), full)],
        out_specs=[pl.BlockSpec((tb, n), row) for n in widths],
        out_shape=[jax.ShapeDtypeStruct((t, n), F32) for n in widths],
        scratch_shapes=[pltpu.VMEM((8, w3), F32)],
        name=name, compiler_params=_cp("arbitrary"))(x, g, w, conv_w8)


def _in_proj_bwd_dx(x, g, w, conv_w8, dz_ssm, dz_gdn, dz_sg, dz_ab, zq, dqkvn, dx_res, s, tb, name):
    t = x.shape[0]
    n_s = s // tb
    hb = tb // 8
    w3 = 3 * D_DN
    q0, q1 = Z_OFFSETS[1], Z_OFFSETS[2]

    def body(x_ref, g_ref, w_ref, cw_ref, ds_ref, dgd_ref, dsg_ref, dab_ref, cur_ref, prev_ref, next_ref, dq_ref, dqn_ref,
             dxr_ref, dx_ref, dg_ref, dzq_ref, dcw_ref):
        i = pl.program_id(0)

        @pl.when(i == 0)
        def _():
            dg_ref[...] = jnp.zeros_like(dg_ref)
            dcw_ref[...] = jnp.zeros_like(dcw_ref)

        rest = jnp.concatenate([dgd_ref[...], dsg_ref[...], dab_ref[...]], axis=1)
        dh = _dg(ds_ref[...], w_ref[:, :q0], 1, 1) + _dg(rest, w_ref[:, q1:], 1, 1)
        first, last = i % n_s == 0, i % n_s == n_s - 1
        dzq, dcw = _dn_prep_vjp(jnp.where(first, 0.0, prev_ref[...]), cur_ref[...], jnp.where(last, 0.0, next_ref[...]),
                                dq_ref[...], jnp.where(last, 0.0, dqn_ref[...]), cw_ref, tb)
        for k in range(4):
            dcw_ref[k:k + 1, :] += dcw[k]
        dzq = dzq.astype(BF16)
        dzq_ref[...] = dzq
        dh = dh + _dg(dzq, w_ref[:, q0:q1], 1, 1)
        _, vjp = jax.vjp(_rms, x_ref[...], g_ref[...])
        dx, dg = vjp(dh)
        dx_ref[...] = dx + dxr_ref[...]
        dg_ref[...] += dg

    n_blk8 = t // 8
    row = lambda i: (i, 0)
    prv = lambda i: (jnp.maximum(i * hb - 1, 0), 0)
    nxt = lambda i: (jnp.minimum((i + 1) * hb, n_blk8 - 1), 0)
    full = lambda i: (0, 0)
    return pl.pallas_call(
        body, grid=(t // tb,),
        in_specs=[pl.BlockSpec((tb, D), row), pl.BlockSpec((1, D), full), pl.BlockSpec((D, ZW), full), pl.BlockSpec((8, w3), full)]
        + [pl.BlockSpec((tb, n), row) for n in (Z_PIECES[0], Z_PIECES[2], Z_PIECES[3], Z_PIECES[4])]
        + [pl.BlockSpec((tb, w3), row), pl.BlockSpec((8, w3), prv), pl.BlockSpec((8, w3), nxt),
           pl.BlockSpec((tb, w3), row), pl.BlockSpec((8, w3), nxt), pl.BlockSpec((tb, D), row)],
        out_specs=[pl.BlockSpec((tb, D), row), pl.BlockSpec((1, D), full), pl.BlockSpec((tb, w3), row), pl.BlockSpec((8, w3), full)],
        out_shape=[jax.ShapeDtypeStruct((t, D), F32), jax.ShapeDtypeStruct((1, D), F32), jax.ShapeDtypeStruct((t, w3), BF16),
                   jax.ShapeDtypeStruct((8, w3), F32)],
        name=name, compiler_params=_cp("arbitrary"))(x, g, w, conv_w8, dz_ssm, dz_gdn, dz_sg, dz_ab, zq, zq, zq, dqkvn, dqkvn, dx_res)


def _in_proj_bwd_dw(x, g, dzs, tb, name):
    t = x.shape[0]

    def body(x_ref, g_ref, d0, d1, d2, d3, d4, *dw_refs):
        @pl.when(pl.program_id(0) == 0)
        def _():
            for r in dw_refs:
                r[...] = jnp.zeros_like(r)

        h = _rms(x_ref[...], g_ref[...]).astype(BF16)
        for d_ref, dw_ref in zip((d0, d1, d2, d3, d4), dw_refs):
            dw_ref[...] += _dg(h, d_ref[...].astype(BF16), 0, 0)

    row = lambda i: (i, 0)
    full = lambda i: (0, 0)
    return pl.pallas_call(
        body, grid=(t // tb,),
        in_specs=[pl.BlockSpec((tb, D), row), pl.BlockSpec((1, D), full)] + [pl.BlockSpec((tb, n), row) for n in Z_PIECES],
        out_specs=[pl.BlockSpec((D, n), full) for n in Z_PIECES],
        out_shape=[jax.ShapeDtypeStruct((D, n), F32) for n in Z_PIECES],
        name=name, compiler_params=_cp("arbitrary"))(x, g, *dzs)


def _lam_pow(a_re, a_im, log_step, k):
    step = jnp.exp(log_step)[:, None]
    mag = jnp.exp(k * a_re * step)
    ang = k * a_im * step
    return mag * jnp.cos(ang), mag * jnp.sin(ang)


def _s5_powers(a_re, a_im, log_step):
    def table(ks):
        re, im = _lam_pow(a_re, a_im, log_step, jnp.asarray(ks, F32)[:, None, None])
        return jnp.concatenate([re.reshape(len(ks), NRE), im.reshape(len(ks), NRE)], axis=-1)

    ld = table(S5_SHIFTS).reshape(len(S5_SHIFTS), 1, 2 * NRE)
    return ld, table(range(1, S5_GROUP + 1)), table(range(S5_GROUP, 0, -1))


def _s5_tables(a_re, a_im, b_re, b_im, c_re, c_im, d_skip, log_step):
    lam_re, lam_im = _lam_pow(a_re, a_im, log_step, 1.0)
    den = a_re * a_re + a_im * a_im
    nr, ni = lam_re - 1.0, lam_im
    f_re = (nr * a_re + ni * a_im) / den
    f_im = (ni * a_re - nr * a_im) / den
    bbar_re = f_re[..., None] * b_re - f_im[..., None] * b_im
    bbar_im = f_re[..., None] * b_im + f_im[..., None] * b_re
    eye = jnp.eye(G, dtype=F32)

    def blk_b(bb):
        return (jnp.transpose(bb, (0, 2, 1))[:, :, None, :] * eye[:, None, :, None]).reshape(D_SSM, NRE)

    def blk_c(cc):
        return (jnp.transpose(cc, (0, 2, 1))[:, :, None, :] * eye[:, None, :, None]).reshape(NRE, D_SSM)

    b_blk = jnp.concatenate([blk_b(bbar_re), blk_b(bbar_im)], axis=1)
    c_blk = jnp.concatenate([blk_c(c_re), -blk_c(c_im)], axis=0)
    lam = jnp.concatenate([lam_re.reshape(1, NRE), lam_im.reshape(1, NRE)], axis=-1)
    return b_blk, c_blk, lam, d_skip.reshape(1, D_SSM)


def _group_shift(x, d, up=False):
    r = lax.broadcasted_iota(jnp.int32, x.shape, 0) & (S5_GROUP - 1)
    if up:
        return jnp.where(r < S5_GROUP - d, pltpu.roll(x, x.shape[0] - d, 0), 0.0)
    return jnp.where(r >= d, pltpu.roll(x, d, 0), 0.0)


def _s5_scan_steps(hr, hi, cr, ci, lds, lp):
    for ld, d in zip(lds, S5_SHIFTS):
        lr, li = ld[:, :NRE], ld[:, NRE:]
        sr, si = _group_shift(hr, d), _group_shift(hi, d)
        hr, hi = hr + lr * sr - li * si, hi + lr * si + li * sr
        yield
    pr, pi = lp[:, :NRE], lp[:, NRE:]
    rows_r, rows_i = [], []
    for r in range(hr.shape[0] // S5_GROUP):
        br, bi = hr[r * S5_GROUP:(r + 1) * S5_GROUP], hi[r * S5_GROUP:(r + 1) * S5_GROUP]
        br, bi = br + pr * cr - pi * ci, bi + pr * ci + pi * cr
        cr, ci = br[S5_GROUP - 1:S5_GROUP], bi[S5_GROUP - 1:S5_GROUP]
        rows_r.append(br)
        rows_i.append(bi)
        if r % 2:
            yield
    return jnp.concatenate(rows_r, axis=0), jnp.concatenate(rows_i, axis=0)


@jax.custom_vjp
def _known_scan(xr, xi, cr, ci, lam, lds, lp_rev, hr, hi):
    return hr, hi


def _known_scan_fwd(xr, xi, cr, ci, lam, lds, lp_rev, hr, hi):
    return (hr, hi), (cr, ci, lam, lds, lp_rev, hr, hi)


def _known_scan_bwd(res, cts):
    cr, ci, lam, lds, lp_rev, hr, hi = res
    ar, ai = cts
    for ld, d in zip(lds, S5_SHIFTS):
        lr, li = ld[:, :NRE], ld[:, NRE:]
        sr, si = _group_shift(ar, d, up=True), _group_shift(ai, d, up=True)
        ar, ai = ar + lr * sr + li * si, ai + lr * si - li * sr
    qr, qi = lp_rev[:, :NRE], lp_rev[:, NRE:]
    nr, ni = jnp.zeros_like(cr), jnp.zeros_like(ci)
    rows_r, rows_i = [], []
    for r in reversed(range(hr.shape[0] // S5_GROUP)):
        br, bi = ar[r * S5_GROUP:(r + 1) * S5_GROUP], ai[r * S5_GROUP:(r + 1) * S5_GROUP]
        br, bi = br + qr * nr + qi * ni, bi + qr * ni - qi * nr
        nr, ni = br[0:1], bi[0:1]
        rows_r.insert(0, br)
        rows_i.insert(0, bi)
    ar, ai = jnp.concatenate(rows_r, axis=0), jnp.concatenate(rows_i, axis=0)
    lr, li = lam[:, :NRE], lam[:, NRE:]
    dcr, dci = lr * nr + li * ni, lr * ni - li * nr
    first = lax.broadcasted_iota(jnp.int32, hr.shape, 0) == 0
    pr = jnp.where(first, cr, pltpu.roll(hr, 1, 0))
    pi = jnp.where(first, ci, pltpu.roll(hi, 1, 0))
    dlam = jnp.concatenate([jnp.sum(ar * pr + ai * pi, axis=0, keepdims=True),
                            jnp.sum(ai * pr - ar * pi, axis=0, keepdims=True)], axis=1)
    return (ar, ai, dcr, dci, dlam, [jnp.zeros_like(ld) for ld in lds], jnp.zeros_like(lp_rev),
            jnp.zeros_like(hr), jnp.zeros_like(hi))


_known_scan.defvjp(_known_scan_fwd, _known_scan_bwd)


def _interleave(short, long, head_start=0):
    gens = list(short) + list(long)
    results = [None] * len(gens)

    def advance(live):
        still = []
        for idx, gen in live:
            try:
                next(gen)
                still.append((idx, gen))
            except StopIteration as done:
                results[idx] = done.value
        return still

    live_short = advance(list(enumerate(gens))[:len(short)])
    live_long = list(enumerate(gens))[len(short):]
    for _ in range(head_start):
        live_long = advance(live_long)
    live = live_short + live_long
    while live:
        live = advance(live)
    return results[:len(short)], results[len(short):]


def _s5_chunk_gen(u, gate, cr, ci, b_blk, c_blk, lam, dv, wglu, bglu, lds, lp, lp_rev, known_h=None):
    bu = _mm(u, b_blk)
    xr, xi = bu[:, :NRE], bu[:, NRE:]
    yield
    if known_h is None:
        hr, hi = yield from _s5_scan_steps(xr, xi, cr, ci, lds, lp)
    else:
        hr, hi = _known_scan(xr, xi, cr, ci, lam, lds, lp_rev, *known_h)
    y = _mm(jnp.concatenate([hr, hi], axis=1), c_blk) + dv * u
    yield
    y = jax.nn.gelu(y)
    y = y * jax.nn.sigmoid(_mm(y, wglu) + bglu)
    return y * _silu(gate), hr, hi


S5_PAR_SHAPES = [(D_SSM, 2 * NRE), (2 * NRE, D_SSM), (1, 2 * NRE), (1, D_SSM), (D_SSM, D_SSM), (1, D_SSM)]
S5_CONST_SHAPES = [(len(S5_SHIFTS), 1, 2 * NRE), (S5_GROUP, 2 * NRE), (S5_GROUP, 2 * NRE)]
DN_PAR_SHAPES = [(1, D_DN), (1, D_DN), (1, DH)]


def _mix_specs(bl, n_c, rev):
    def chunk(i):
        return n_c - 1 - i if rev else i

    def tok(n):
        return pl.BlockSpec((bl, DN_C, n), lambda i: (0, chunk(i), 0))

    def per_chunk(shape):
        return pl.BlockSpec((bl, 1, *shape), lambda i: (0, chunk(i)) + (0,) * len(shape))

    def whole(shape):
        return pl.BlockSpec(shape, lambda i: (0,) * len(shape))

    return tok, per_chunk, whole


def _unit_lower_inverse_steps(ms):
    c_len = ms[0].shape[0]
    eye = lax.broadcasted_iota(jnp.int32, (c_len, c_len), 0) == lax.broadcasted_iota(jnp.int32, (c_len, c_len), 1)
    ident = jnp.where(eye, 1.0, 0.0)
    ps = ms
    tinvs = [ident - m for m in ms]
    for _ in range(c_len.bit_length() - 2):
        ps = [_dg3(p, p, 1, 0) for p in ps]
        yield
        tinvs = [t + _dg3(t, p, 1, 0) for t, p in zip(tinvs, ps)]
        yield
    return tinvs


@jax.custom_vjp
def _known_inverses(ms, tinvs):
    return tinvs


def _known_inverses_fwd(ms, tinvs):
    return tinvs, tinvs


def _known_inverses_bwd(tinvs, gs):
    return [-_dg3(_dg3(t, g, 0, 0), t, 1, 1) for t, g in zip(tinvs, gs)], [jnp.zeros_like(t) for t in tinvs]


_known_inverses.defvjp(_known_inverses_fwd, _known_inverses_bwd)


def _dn_chunk_gen(qkv, zab, zg, states, alog_e, dt_e, ng, known_tinvs=None):
    c_len = DN_C
    r = lax.broadcasted_iota(jnp.int32, (c_len, c_len), 0)
    c = lax.broadcasted_iota(jnp.int32, (c_len, c_len), 1)
    causal, strict = r >= c, r > c
    tril = jnp.where(causal, 1.0, 0.0)
    rr = lax.broadcasted_iota(jnp.int32, (LANES, D_DN), 0)
    cc = lax.broadcasted_iota(jnp.int32, (LANES, D_DN), 1)
    e_a = jnp.where((cc >= rr * DH) & (cc < rr * DH + DH) & (rr < H), 1.0, 0.0)
    e_b = jnp.where((cc >= (rr - H) * DH) & (cc < (rr - H) * DH + DH) & (rr >= H) & (rr < 2 * H), 1.0, 0.0)
    a_e = _sel_r(zab, e_a)
    b_e = _sel_r(zab, e_b)
    beta = jax.nn.sigmoid(b_e)
    g = -jnp.exp(alog_e) * jax.nn.softplus(a_e + dt_e)
    yield
    gc = _sel_l(tril, g)
    glast = jnp.sum(g, axis=0, keepdims=True)
    eg = jnp.exp(gc)
    ekd = jnp.exp(glast - gc)
    dl = jnp.exp(glast)
    yield
    heads = range(H)
    sls = [slice(h * DH, (h + 1) * DH) for h in heads]
    qs = [qkv[:, h * DH:(h + 1) * DH] for h in heads]
    ks = [qkv[:, D_DN + h * DH:D_DN + (h + 1) * DH] for h in heads]
    vs = [qkv[:, 2 * D_DN + h * DH:2 * D_DN + (h + 1) * DH] for h in heads]
    ccols = [gc[:, sl] for sl in sls]
    decs = [jnp.where(causal, jnp.exp(jnp.where(causal, cl - jnp.transpose(cl), 0.0)), 0.0) for cl in ccols]
    kbs = [k * beta[:, sl] for k, sl in zip(ks, sls)]
    ms = [jnp.where(strict, _mm_nt(kb, k) * dec, 0.0) for kb, k, dec in zip(kbs, ks, decs)]
    yield
    if known_tinvs is None:
        tinvs = yield from _unit_lower_inverse_steps(ms)
    else:
        tinvs = _known_inverses(ms, list(known_tinvs))
    sols = [_dot3(t, jnp.concatenate([v * beta[:, sl], kb * eg[:, sl]], axis=1))
            for t, v, kb, sl in zip(tinvs, vs, kbs, sls)]
    yield
    atts = [_mm_nt(q, k) * dec for q, k, dec in zip(qs, ks, decs)]
    vnews = [sol[:, :DH] - _mm(sol[:, DH:], st) for sol, st in zip(sols, states)]
    yield
    os_ = [_mm(q * eg[:, sl], st) + _mm(att, vn) for q, sl, st, att, vn in zip(qs, sls, states, atts, vnews)]
    yield
    new_states = [st * dl[:, sl] + _mm_tn(k * ekd[:, sl], vn) for st, sl, k, vn in zip(states, sls, ks, vnews)]
    yield
    ys = [_rms(o, ng) * _silu(zg[:, sl]) for o, sl in zip(os_, sls)]
    return jnp.concatenate(ys, axis=1), new_states, tinvs


def _mix_fwd(zs, s5_par, s5_const, qkv, zab, zg, dn_par, bl, s, name, gather=()):
    assert S5_L == DN_C
    n_c = s // DN_C
    nd = len(S5_SHIFTS)
    m = len(gather)
    tok, per_chunk, whole = _mix_specs(bl, n_c, False)

    def body(*refs):
        (z_ref, b_ref, c_ref, lam_ref, dv_ref, wg_ref, bg_ref, ld_ref, lp_ref, lpr_ref,
         q_ref, ab_ref, zg_ref, al_ref, dt_ref, ng_ref) = refs[:16]
        ys_ref, car_ref, h_ref, yd_ref, st_ref, ti_ref = refs[16 + m:22 + m]
        cs, ssc = refs[22 + 2 * m:24 + 2 * m]
        gathering = (_gather_protocol(_mesh_place(), refs[16:16 + m], refs[22 + m:22 + 2 * m], *refs[24 + 2 * m:])
                     if m else None)

        @pl.when(pl.program_id(0) == 0)
        def _():
            cs[...] = jnp.zeros_like(cs)
            ssc[...] = jnp.zeros_like(ssc)
            if m:
                next(gathering)

        lds = [ld_ref[k] for k in range(nd)]
        s5_gens, dn_gens = [], []
        for e in range(bl):
            c = cs[e]
            car_ref[e, 0] = c
            sts = [ssc[e, h] for h in range(H)]
            for h in range(H):
                st_ref[e, 0, h] = sts[h]
            z = z_ref[e]
            s5_gens.append(_s5_chunk_gen(z[:, :D_SSM], z[:, D_SSM:], c[:, :NRE], c[:, NRE:], b_ref[...], c_ref[...], lam_ref[...],
                                         dv_ref[...], wg_ref[...], bg_ref[...], lds, lp_ref[...], lpr_ref[...]))
            dn_gens.append(_dn_chunk_gen(q_ref[e], ab_ref[e], zg_ref[e], sts, al_ref[...], dt_ref[...], ng_ref[...]))
        s5_outs, dn_outs = _interleave(s5_gens, dn_gens, head_start=MIX_HEAD_START)
        for e in range(bl):
            y_s, hr, hi = s5_outs[e]
            y_d, new_sts, tinvs = dn_outs[e]
            ys_ref[e] = y_s
            h_ref[e, :, :NRE] = hr
            h_ref[e, :, NRE:] = hi
            cs[e, :, :NRE] = hr[S5_L - 1:S5_L]
            cs[e, :, NRE:] = hi[S5_L - 1:S5_L]
            yd_ref[e] = y_d
            for h in range(H):
                ssc[e, h] = new_sts[h]
                ti_ref[e, 0, h] = tinvs[h]

        if m:
            @pl.when(pl.program_id(0) == n_c - 1)
            def _():
                for _ in gathering:
                    pass

    head_mats = jax.ShapeDtypeStruct((bl, n_c, H, DH, DH), F32)
    outs = pl.pallas_call(
        body, grid=(n_c,),
        in_specs=[tok(2 * D_SSM)] + [whole(sh) for sh in S5_PAR_SHAPES + S5_CONST_SHAPES]
        + [tok(3 * D_DN), tok(LANES), tok(D_DN)] + [whole(sh) for sh in DN_PAR_SHAPES] + _hbm_specs(m),
        out_specs=[tok(D_SSM), per_chunk((1, 2 * NRE)), tok(2 * NRE), tok(D_DN), per_chunk((H, DH, DH)), per_chunk((H, DH, DH))]
        + _hbm_specs(m),
        out_shape=[jax.ShapeDtypeStruct((bl, s, D_SSM), F32), jax.ShapeDtypeStruct((bl, n_c, 1, 2 * NRE), F32),
                   jax.ShapeDtypeStruct((bl, s, 2 * NRE), F32), jax.ShapeDtypeStruct((bl, s, D_DN), F32), head_mats, head_mats]
        + [jax.ShapeDtypeStruct((N_DEV, *b.shape), b.dtype) for b in gather],
        scratch_shapes=[pltpu.VMEM((bl, 1, 2 * NRE), F32), pltpu.VMEM((bl, H, DH, DH), F32)] + (_gather_sems(m) if m else []),
        name=name, compiler_params=_cp("arbitrary"))(zs, *s5_par, *s5_const, qkv, zab, zg, *dn_par, *gather)
    return outs[:6], outs[6:]


def _mix_bwd(zs, carries, h_all, dy_s, s5_par, s5_const, qkv, zab, zg, states, tinvs, dy_d, dn_par, bl, s, name, exchange=()):
    n_c = s // DN_C
    nd = len(S5_SHIFTS)
    k_ex = len(exchange)
    tok, per_chunk, whole = _mix_specs(bl, n_c, True)

    def both(examples, s5_tabs, s5_consts, dn_tabs):
        s5_gens = [_s5_chunk_gen(u, gate, cr, ci, *s5_tabs, *s5_consts, known_h=(hr, hi))
                   for u, gate, cr, ci, hr, hi, _, _, _, _, _ in examples]
        dn_gens = [_dn_chunk_gen(q, ab, zgate, sts, *dn_tabs, known_tinvs=known)
                   for _, _, _, _, _, _, q, ab, zgate, sts, known in examples]
        s5_outs, dn_outs = _interleave(s5_gens, dn_gens, head_start=MIX_HEAD_START)
        return [(y_s, hr[S5_L - 1:S5_L], hi[S5_L - 1:S5_L], y_d, new_sts)
                for (y_s, hr, hi), (y_d, new_sts, _) in zip(s5_outs, dn_outs)]

    def body(*refs):
        (z_ref, car_ref, h_ref, dys_ref, b_ref, c_ref, lam_ref, dv_ref, wg_ref, bg_ref, ld_ref, lp_ref, lpr_ref,
         q_ref, ab_ref, zg_ref, st_ref, ti_ref, dyd_ref, al_ref, dt_ref, ng_ref) = refs[:22]
        (dz_ref, db_ref, dc_ref, dlam_ref, ddv_ref, dwg_ref, dbg_ref,
         dq_ref, dab_ref, dzg_ref, dal_ref, ddt_ref, dng_ref) = refs[22 + k_ex:35 + k_ex]
        dcs, dsc = refs[35 + 2 * k_ex:37 + 2 * k_ex]
        accs = (db_ref, dc_ref, dlam_ref, ddv_ref, dwg_ref, dbg_ref, dal_ref, ddt_ref, dng_ref)
        exchanging = (_chip_protocol(_mesh_place(), refs[22:22 + k_ex], refs[35 + k_ex:35 + 2 * k_ex], *refs[37 + 2 * k_ex:])
                      if k_ex else None)

        @pl.when(pl.program_id(0) == 0)
        def _():
            for r in accs + (dcs, dsc):
                r[...] = jnp.zeros_like(r)
            if k_ex:
                next(exchanging)

        examples = []
        for e in range(bl):
            z = z_ref[e]
            c = car_ref[e, 0]
            examples.append((z[:, :D_SSM], z[:, D_SSM:], c[:, :NRE], c[:, NRE:], h_ref[e, :, :NRE], h_ref[e, :, NRE:],
                             q_ref[e], ab_ref[e], zg_ref[e], [st_ref[e, 0, h] for h in range(H)],
                             [ti_ref[e, 0, h] for h in range(H)]))
        _, vjp = jax.vjp(both, examples,
                         (b_ref[...], c_ref[...], lam_ref[...], dv_ref[...], wg_ref[...], bg_ref[...]),
                         ([ld_ref[k] for k in range(nd)], lp_ref[...], lpr_ref[...]),
                         (al_ref[...], dt_ref[...], ng_ref[...]))
        cts = []
        for e in range(bl):
            dc = dcs[e]
            cts.append((dys_ref[e], dc[:, :NRE], dc[:, NRE:], dyd_ref[e], [dsc[e, h] for h in range(H)]))
        d_examples, d_s5, _, d_dn = vjp(cts)
        for e in range(bl):
            du, dgate, dcr, dci, _, _, dq, dab, dzg, dsts, _ = d_examples[e]
            dz_ref[e] = jnp.concatenate([du, dgate], axis=1).astype(BF16)
            dcs[e, :, :NRE] = dcr
            dcs[e, :, NRE:] = dci
            dq_ref[e] = dq
            dab_ref[e] = dab.astype(BF16)
            dzg_ref[e] = dzg.astype(BF16)
            for h in range(H):
                dsc[e, h] = dsts[h]
        for r, ct in zip(accs, (*d_s5, *d_dn)):
            r[...] += ct

        if k_ex:
            @pl.when(pl.program_id(0) == n_c - 1)
            def _():
                for _ in exchanging:
                    pass

    head_mats = per_chunk((H, DH, DH))
    outs = pl.pallas_call(
        body, grid=(n_c,),
        in_specs=[tok(2 * D_SSM), per_chunk((1, 2 * NRE)), tok(2 * NRE), tok(D_SSM)]
        + [whole(sh) for sh in S5_PAR_SHAPES + S5_CONST_SHAPES]
        + [tok(3 * D_DN), tok(LANES), tok(D_DN), head_mats, head_mats, tok(D_DN)] + [whole(sh) for sh in DN_PAR_SHAPES]
        + _hbm_specs(k_ex),
        out_specs=[tok(2 * D_SSM)] + [whole(sh) for sh in S5_PAR_SHAPES]
        + [tok(3 * D_DN), tok(LANES), tok(D_DN)] + [whole(sh) for sh in DN_PAR_SHAPES] + _hbm_specs(k_ex),
        out_shape=[jax.ShapeDtypeStruct((bl, s, 2 * D_SSM), BF16)] + [jax.ShapeDtypeStruct(sh, F32) for sh in S5_PAR_SHAPES]
        + [jax.ShapeDtypeStruct((bl, s, 3 * D_DN), F32), jax.ShapeDtypeStruct((bl, s, LANES), BF16),
           jax.ShapeDtypeStruct((bl, s, D_DN), BF16)]
        + [jax.ShapeDtypeStruct(sh, F32) for sh in DN_PAR_SHAPES]
        + [jax.ShapeDtypeStruct(q.shape, q.dtype) for q in exchange],
        scratch_shapes=[pltpu.VMEM((bl, 1, 2 * NRE), F32), pltpu.VMEM((bl, H, DH, DH), F32)] + (_chip_sems(k_ex) if k_ex else []),
        name=name, compiler_params=_cp("arbitrary"))(
            zs, carries, h_all, dy_s, *s5_par, *s5_const, qkv, zab, zg, states, tinvs, dy_d, *dn_par, *exchange)
    return outs[:7], outs[7:13], outs[13:]


def _sg_fn(n_chunk):
    def f(z, lng, lnb, w, bsp_t):
        u = jax.nn.gelu(z[:, :D_SG])
        v = jax.nn.gelu(z[:, D_SG:2 * D_SG])
        gate = z[:, 2 * D_SG:]
        xc = v - jnp.mean(v, axis=-1, keepdims=True)
        vn = xc * lax.rsqrt(jnp.mean(xc * xc, axis=-1, keepdims=True) + EPS) * lng + lnb
        r = lax.broadcasted_iota(jnp.int32, (SG_C, SG_C), 0)
        c = lax.broadcasted_iota(jnp.int32, (SG_C, SG_C), 1)
        causal = r >= c
        first_half = c < SG_C // 2
        rr = lax.broadcasted_iota(jnp.int32, (LANES, D_SG), 0)
        cc = lax.broadcasted_iota(jnp.int32, (LANES, D_SG), 1)
        expand = jnp.where((cc >= rr * 64) & (cc < rr * 64 + 64) & (rr < 4), 1.0, 0.0)
        bias = _sel_r(bsp_t, expand)
        wm = [jnp.where(causal, w[h], 0.0) for h in range(4)]
        rows = []
        for ci in range(n_chunk):
            vc = vn[ci * SG_C:(ci + 1) * SG_C]
            pairs = []
            for pr in range(2):
                vp = vc[:, pr * LANES:(pr + 1) * LANES]
                pairs.append(jnp.where(first_half, _mm(wm[2 * pr], vp), _mm(wm[2 * pr + 1], vp)))
            rows.append(jnp.concatenate(pairs, axis=1) + bias)
        sp = jnp.concatenate(rows, axis=0) if n_chunk > 1 else rows[0]
        return u * sp * _silu(gate)

    return f


def _sg_specs():
    full = lambda i: (0, 0)
    full3 = lambda i: (0, 0, 0)
    par = [pl.BlockSpec((1, D_SG), full), pl.BlockSpec((1, D_SG), full), pl.BlockSpec((4, SG_C, SG_C), full3),
           pl.BlockSpec((SG_C, LANES), full)]
    par_shapes = [(1, D_SG), (1, D_SG), (4, SG_C, SG_C), (SG_C, LANES)]
    return par, par_shapes


def _sg_fwd(zsg, params, tb, name):
    t = zsg.shape[0]
    f = _sg_fn(tb // SG_C)
    par, _ = _sg_specs()

    def body(z_ref, g_ref, b_ref, w_ref, bs_ref, y_ref):
        y_ref[...] = f(z_ref[...], g_ref[...], b_ref[...], w_ref[...], bs_ref[...])

    row = lambda i: (i, 0)
    return pl.pallas_call(
        body, grid=(t // tb,), in_specs=[pl.BlockSpec((tb, 3 * D_SG), row)] + par,
        out_specs=pl.BlockSpec((tb, D_SG), row), out_shape=jax.ShapeDtypeStruct((t, D_SG), F32),
        name=name, compiler_params=_cp("parallel"))(zsg, *params)


def _sg_bwd(zsg, dy, params, tb, name):
    t = zsg.shape[0]
    f = _sg_fn(tb // SG_C)
    par, par_shapes = _sg_specs()

    def body(z_ref, dy_ref, g_ref, b_ref, w_ref, bs_ref, dz_ref, dg_ref, db_ref, dw_ref, dbs_ref):
        accs = (dg_ref, db_ref, dw_ref, dbs_ref)

        @pl.when(pl.program_id(0) == 0)
        def _():
            for r in accs:
                r[...] = jnp.zeros_like(r)

        _, vjp = jax.vjp(f, z_ref[...], g_ref[...], b_ref[...], w_ref[...], bs_ref[...])
        cts = vjp(dy_ref[...])
        dz_ref[...] = cts[0].astype(BF16)
        for r, ct in zip(accs, cts[1:]):
            r[...] += ct

    row = lambda i: (i, 0)
    return pl.pallas_call(
        body, grid=(t // tb,), in_specs=[pl.BlockSpec((tb, 3 * D_SG), row), pl.BlockSpec((tb, D_SG), row)] + par,
        out_specs=[pl.BlockSpec((tb, 3 * D_SG), row)] + par,
        out_shape=[jax.ShapeDtypeStruct((t, 3 * D_SG), BF16)] + [jax.ShapeDtypeStruct(sh, F32) for sh in par_shapes],
        name=name, compiler_params=_cp("arbitrary"))(zsg, dy, *params)


def _out_fwd(x, ys, p, layer, w_out, pg, w_gate, w_ple, tb, name, head=None):
    t = x.shape[0]

    def body(*refs):
        x_ref, y0, y1, y2, p_ref, wo_ref, pg_ref, wg_ref, wp_ref = refs[:9]
        o_ref, x1_ref, gate_ref = refs[-5:-2] if head else refs[-3:]
        y = jnp.concatenate([y0[...], y1[...], y2[...]], axis=1).astype(BF16)
        x1 = x_ref[...] + jnp.dot(y, wo_ref[...], preferred_element_type=F32)
        hn = _rms(x1, pg_ref[...]).astype(BF16)
        gate = jax.nn.sigmoid(jnp.dot(hn, wg_ref[...], preferred_element_type=F32))
        pp = jnp.dot(p_ref[0].astype(BF16), wp_ref[...], preferred_element_type=F32)
        x2 = x1 + gate * pp
        x1_ref[...] = x1
        gate_ref[...] = gate
        if not head:
            o_ref[...] = x2
            return
        fg_ref, t_ref = refs[9:11]
        dfg_ref, loss_ref = refs[-2:]

        @pl.when(pl.program_id(0) == 0)
        def _():
            dfg_ref[...] = jnp.zeros_like(dfg_ref)
            loss_ref[...] = jnp.zeros_like(loss_ref)

        yf, vjp = jax.vjp(_rms, x2, fg_ref[...])
        err = yf - t_ref[...]
        loss_ref[...] += jnp.zeros_like(loss_ref) + 0.5 * jnp.sum(err * err) / D
        dx2, dfg = vjp(err / D)
        o_ref[...] = dx2
        dfg_ref[...] += dfg

    row = lambda i: (i, 0)
    full = lambda i: (0, 0)
    acts = [pl.BlockSpec((tb, D), row)] * 3
    return pl.pallas_call(
        body, grid=(t // tb,),
        in_specs=[pl.BlockSpec((tb, D), row), pl.BlockSpec((tb, D_SSM), row), pl.BlockSpec((tb, D_DN), row),
                  pl.BlockSpec((tb, D_SG), row), pl.BlockSpec((1, tb, D_PLE), lambda i: (layer, i, 0)), pl.BlockSpec((D, D), full),
                  pl.BlockSpec((1, D), full), pl.BlockSpec((D, D), full), pl.BlockSpec((D_PLE, D), full)]
        + ([pl.BlockSpec((1, D), full), pl.BlockSpec((tb, D), row)] if head else []),
        out_specs=acts + ([pl.BlockSpec((1, D), full), pl.BlockSpec((1, LANES), full)] if head else []),
        out_shape=[jax.ShapeDtypeStruct((t, D), F32)] * 3
        + ([jax.ShapeDtypeStruct((1, D), F32), jax.ShapeDtypeStruct((1, LANES), F32)] if head else []),
        name=name, compiler_params=_cp("arbitrary" if head else "parallel"))(x, *ys, p, w_out, pg, w_gate, w_ple, *(head or ()))


def _out_bwd(x1, gate, ys, p, layer, dx2, w_out, pg, w_gate, w_ple, tb, name):
    t = x1.shape[0]

    def body(x1_ref, gate_ref, y0, y1, y2, p_ref, d_ref, wo_ref, pg_ref, wg_ref, wp_ref,
             dx_ref, dy0, dy1, dy2, dwo_ref, dpg_ref, dwg_ref, dwp_ref):
        accs = (dwo_ref, dpg_ref, dwg_ref, dwp_ref)

        @pl.when(pl.program_id(0) == 0)
        def _():
            for r in accs:
                r[...] = jnp.zeros_like(r)

        y = jnp.concatenate([y0[...], y1[...], y2[...]], axis=1).astype(BF16)
        hn, rms_vjp = jax.vjp(_rms, x1_ref[...], pg_ref[...])
        hb = hn.astype(BF16)
        gate = gate_ref[...]
        pb = p_ref[0].astype(BF16)
        pp = jnp.dot(pb, wp_ref[...], preferred_element_type=F32)
        d2 = d_ref[...]
        dpp = (d2 * gate).astype(BF16)
        dlog = (d2 * pp * gate * (1.0 - gate)).astype(BF16)
        dwp_ref[...] += _dg(pb, dpp, 0, 0)
        dwg_ref[...] += _dg(hb, dlog, 0, 0)
        dx1_n, dpg = rms_vjp(_dg(dlog, wg_ref[...], 1, 1))
        dpg_ref[...] += dpg
        dx1 = d2 + dx1_n
        dx_ref[...] = dx1
        db = dx1.astype(BF16)
        dwo_ref[...] += _dg(y, db, 0, 0)
        dy = _dg(db, wo_ref[...], 1, 1)
        dy0[...] = dy[:, :D_SSM]
        dy1[...] = dy[:, D_SSM:D_SSM + D_DN]
        dy2[...] = dy[:, D_SSM + D_DN:]

    row = lambda i: (i, 0)
    full = lambda i: (0, 0)
    acts = [pl.BlockSpec((tb, D), row), pl.BlockSpec((tb, D_SSM), row), pl.BlockSpec((tb, D_DN), row), pl.BlockSpec((tb, D_SG), row)]
    wts = [pl.BlockSpec((D, D), full), pl.BlockSpec((1, D), full), pl.BlockSpec((D, D), full), pl.BlockSpec((D_PLE, D), full)]
    return pl.pallas_call(
        body, grid=(t // tb,),
        in_specs=[pl.BlockSpec((tb, D), row)] + acts
        + [pl.BlockSpec((1, tb, D_PLE), lambda i: (layer, i, 0)), pl.BlockSpec((tb, D), row)] + wts,
        out_specs=acts + wts,
        out_shape=[jax.ShapeDtypeStruct((t, n), F32) for n in (D, D_SSM, D_DN, D_SG)]
        + [jax.ShapeDtypeStruct(sh, F32) for sh in ((D, D), (1, D), (D, D), (D_PLE, D))],
        name=name, compiler_params=_cp("arbitrary"))(x1, gate, *ys, p, dx2, w_out, pg, w_gate, w_ple)


def _hbm_specs(n):
    return [pl.BlockSpec(memory_space=pl.ANY)] * n


def _gather_protocol(place, ins, outs, send_sems, recv_sems, local_sems):
    n = len(ins)
    x, y, c, other_x, other_y, other_c = place
    me, sibling = (x, y, c), (x, y, other_c)
    chips = [(other_x, y), (x, other_y), (other_x, other_y)]

    def slot(a, px, py, pc):
        return outs[a].at[4 * px + 2 * py + pc]

    def copy(a, k, blk, to, src=None):
        return pltpu.make_async_remote_copy(
            src_ref=slot(a, *blk) if src is None else src, dst_ref=slot(a, *blk),
            send_sem=send_sems.at[7 * a + k], recv_sem=recv_sems.at[7 * a + k],
            device_id=to, device_id_type=pl.DeviceIdType.MESH)

    def own_copies():
        mines = [pltpu.make_async_copy(ins[a], slot(a, *me), local_sems.at[a]) for a in range(n)]
        first = []
        for a in range(n):
            first.append(copy(a, 0, me, sibling, src=ins[a]))
            first += [copy(a, 1 + j, me, (*chip, c), src=ins[a]) for j, chip in enumerate(chips)]
        return mines, first

    mines, first = own_copies()
    for cp in mines + first:
        cp.start()
    yield
    mines, first = own_copies()
    passed = []
    for j, chip in enumerate(chips):
        for a in range(n):
            copy(a, 1 + j, (*chip, c), me).wait_recv()
            onward = copy(a, 4 + j, (*chip, c), sibling)
            onward.start()
            passed.append(onward)
    for a in range(n):
        copy(a, 0, sibling, me).wait_recv()
    for j, chip in enumerate(chips):
        for a in range(n):
            copy(a, 4 + j, (*chip, other_c), me).wait_recv()
    for cp in first + passed:
        cp.wait_send()
    for cp in mines:
        cp.wait()


def _mesh_place():
    x, y, c = lax.axis_index("x"), lax.axis_index("y"), lax.axis_index("c")
    return x, y, c, 1 - x, 1 - y, 1 - c


def _gather_sems(n):
    return [pltpu.SemaphoreType.DMA((7 * n,)), pltpu.SemaphoreType.DMA((7 * n,)), pltpu.SemaphoreType.DMA((n,))]


def _all_gather(blocks, name):
    n = len(blocks)

    def body(*refs):
        for _ in _gather_protocol(_mesh_place(), refs[:n], refs[n:2 * n], *refs[2 * n:]):
            pass

    return pl.pallas_call(
        body, out_shape=[jax.ShapeDtypeStruct((N_DEV, *b.shape), b.dtype) for b in blocks],
        in_specs=_hbm_specs(n), out_specs=_hbm_specs(n), scratch_shapes=_gather_sems(n), name=name)(*blocks)


def _pair_exchange(gs, name):
    n = len(gs)

    def body(*refs):
        ins, recvs = refs[:n], refs[n:2 * n]
        send_sems, recv_sems = refs[2 * n:]
        x, y, c = lax.axis_index("x"), lax.axis_index("y"), lax.axis_index("c")
        remote = [pltpu.make_async_remote_copy(
            src_ref=ins[a], dst_ref=recvs[a], send_sem=send_sems.at[a], recv_sem=recv_sems.at[a],
            device_id=(x, y, 1 - c), device_id_type=pl.DeviceIdType.MESH) for a in range(n)]
        for cp in remote:
            cp.start()
        for cp in remote:
            cp.wait_send()
            cp.wait_recv()

    return pl.pallas_call(
        body, out_shape=[jax.ShapeDtypeStruct(g.shape, g.dtype) for g in gs], in_specs=_hbm_specs(n), out_specs=_hbm_specs(n),
        scratch_shapes=[pltpu.SemaphoreType.DMA((n,)), pltpu.SemaphoreType.DMA((n,))],
        name=name)(*gs)


def _chip_protocol(place, ins, outs, send_sems, recv_sems, local_sems):
    n = len(ins)
    x, y, c, other_x, other_y, _ = place

    def copies():
        my_chip = 2 * x + y
        local = [pltpu.make_async_copy(ins[a].at[my_chip], outs[a].at[my_chip], local_sems.at[a]) for a in range(n)]
        remote = []
        for j in range(1, 4):
            px = other_x if j & 2 else x
            py = other_y if j & 1 else y
            for a in range(n):
                remote.append(pltpu.make_async_remote_copy(
                    src_ref=ins[a].at[2 * px + py], dst_ref=outs[a].at[my_chip],
                    send_sem=send_sems.at[3 * a + j - 1], recv_sem=recv_sems.at[3 * a + j - 1],
                    device_id=(px, py, c), device_id_type=pl.DeviceIdType.MESH))
        return local, remote

    local, remote = copies()
    for cp in local + remote:
        cp.start()
    yield
    local, remote = copies()
    for cp in remote:
        cp.wait_send()
        cp.wait_recv()
    for cp in local:
        cp.wait()


def _chip_sems(n):
    return [pltpu.SemaphoreType.DMA((3 * n,)), pltpu.SemaphoreType.DMA((3 * n,)), pltpu.SemaphoreType.DMA((n,))]


def _chip_exchange(ps, gather, name):
    n, m = len(ps), len(gather)

    def body(*refs):
        ins, g_ins, outs, g_outs = refs[:n], refs[n:n + m], refs[n + m:2 * n + m], refs[2 * n + m:2 * (n + m)]
        place = _mesh_place()
        exchanging = _chip_protocol(place, ins, outs, *refs[2 * (n + m):2 * (n + m) + 3])
        gathering = _gather_protocol(place, g_ins, g_outs, *refs[2 * (n + m) + 3:])
        next(exchanging)
        for _ in gathering:
            pass
        for _ in exchanging:
            pass

    outs = pl.pallas_call(
        body, out_shape=[jax.ShapeDtypeStruct(q.shape, q.dtype) for q in ps]
        + [jax.ShapeDtypeStruct((N_DEV, *b.shape), b.dtype) for b in gather],
        in_specs=_hbm_specs(n + m), out_specs=_hbm_specs(n + m),
        scratch_shapes=_chip_sems(n) + _gather_sems(m), name=name)(*ps, *gather)
    return outs[:n], outs[n:]


def _row_block(rows, bytes_per_row):
    best = None
    for rb in range(16, rows + 1, 16):
        if rows % rb == 0 and rb * bytes_per_row <= ELEMENTWISE_STEP_BYTES:
            best = rb
    return rows if best is None else best


def _add_pair(own, recv, name):
    shape = own.shape
    last = shape[-1]
    rows = own.size // last
    rb = _row_block(rows, 3 * 4 * (-(-last // LANES) * LANES))

    def body(a_ref, b_ref, o_ref):
        o_ref[...] = (a_ref[...].astype(F32) + b_ref[...].astype(F32)).astype(o_ref.dtype)

    row = lambda i: (i, 0)
    out = pl.pallas_call(
        body, grid=(rows // rb,), in_specs=[pl.BlockSpec((rb, last), row)] * 2, out_specs=pl.BlockSpec((rb, last), row),
        out_shape=jax.ShapeDtypeStruct((rows, last), own.dtype), name=name,
        compiler_params=_cp("parallel"))(own.reshape(rows, last), recv.reshape(rows, last))
    return out.reshape(shape)


def _sum_adamw(gk, w, m, v, name):
    shape = w.shape
    n_part = gk.shape[0]
    last = shape[-1]
    rows = w.size // last
    rb = _row_block(rows, (n_part + 7) * 4 * (-(-last // LANES) * LANES))

    def body(g_ref, w_ref, m_ref, v_ref, go_ref, d_ref, mo_ref, vo_ref):
        g = g_ref[0].astype(F32)
        for k in range(1, n_part):
            g = g + g_ref[k].astype(F32)
        mn = ADAM_B1 * m_ref[...] + (1.0 - ADAM_B1) * g
        vn = ADAM_B2 * v_ref[...] + (1.0 - ADAM_B2) * jnp.square(g)
        m_hat = mn / (1.0 - ADAM_B1 ** ADAM_STEP)
        v_hat = vn / (1.0 - ADAM_B2 ** ADAM_STEP)
        go_ref[...] = g
        d_ref[...] = -ADAM_LR * (m_hat / (jnp.sqrt(v_hat) + ADAM_EPS) + ADAM_WD * w_ref[...])
        mo_ref[...] = mn
        vo_ref[...] = vn

    row = lambda i: (i, 0)
    outs = pl.pallas_call(
        body, grid=(rows // rb,),
        in_specs=[pl.BlockSpec((n_part, rb, last), lambda i: (0, i, 0))] + [pl.BlockSpec((rb, last), row)] * 3,
        out_specs=[pl.BlockSpec((rb, last), row)] * 4,
        out_shape=[jax.ShapeDtypeStruct((rows, last), F32)] * 4,
        name=name, compiler_params=_cp("parallel"))(gk.reshape(n_part, rows, last), *[a.reshape(rows, last) for a in (w, m, v)])
    return [o.reshape(shape) for o in outs]


def _seg_rows(shape):
    n = 1
    for d in shape:
        n *= d
    return -(-n // (8 * LANES)) * 8


def _pack(arrs):
    segs = []
    for a in arrs:
        r = _seg_rows(a.shape)
        segs.append(jnp.pad(a.reshape(-1).astype(F32), (0, r * LANES - a.size)).reshape(r, LANES))
    rows = sum(s.shape[0] for s in segs)
    total = -(-rows // PACK_ROWS) * PACK_ROWS
    if total > rows:
        segs.append(jnp.zeros((total - rows, LANES), F32))
    return jnp.concatenate(segs, axis=0)


def _unpack(pack, shapes):
    out, off = [], 0
    for sh in shapes:
        r = _seg_rows(sh)
        n = 1
        for d in sh:
            n *= d
        out.append(pack[off:off + r].reshape(-1)[:n].reshape(sh))
        off += r
    return out


def _to_dest_blocks(full, axis, dtype):
    sh = list(full.shape)
    sh[axis:axis + 1] = [N_DEV // 2, 2, sh[axis] // N_DEV]
    return jnp.moveaxis(full.reshape(sh), (axis, axis + 1), (1, 0)).astype(dtype)


def _from_gathered(g, axis):
    m = jnp.moveaxis(g, 0, axis)
    sh = list(m.shape)
    sh[axis:axis + 2] = [sh[axis] * sh[axis + 1]]
    return m.reshape(sh)


D_IN = 3336
W_IN_SHARD = D_IN // N_DEV
W_IN_MOVES = ((0, 2048, 0), (2048, 2056, 3328), (2056, D_IN, 2048))


def _w_in_windows(k):
    lo, hi = k * W_IN_SHARD, (k + 1) * W_IN_SHARD
    out = []
    for a, b, mine in W_IN_MOVES:
        a2, b2 = max(a, lo), min(b, hi)
        if b2 > a2:
            out.append((a2 - lo, b2 - a2, mine + a2 - a))
    return out


def _assemble_w_in(gathered, name):
    depth = gathered.shape[1]
    rb = 256

    def body(g_ref, o_ref):
        o_ref[0, :, D_IN:] = jnp.zeros((rb, ZW - D_IN), o_ref.dtype)
        for k in range(N_DEV):
            for off, width, mine in _w_in_windows(k):
                o_ref[0, :, mine:mine + width] = g_ref[k, 0, :, off:off + width]

    return pl.pallas_call(
        body, grid=(depth, D // rb),
        in_specs=[pl.BlockSpec((N_DEV, 1, rb, W_IN_SHARD), lambda l, i: (0, l, i, 0))],
        out_specs=pl.BlockSpec((1, rb, ZW), lambda l, i: (l, i, 0)),
        out_shape=jax.ShapeDtypeStruct((depth, D, ZW), gathered.dtype),
        name=name, compiler_params=_cp("parallel", "parallel"))(gathered)


def _split_dw_in(dws, name):
    depth = len(dws)
    rb = 128

    def body(*refs):
        o_ref = refs[-1]
        for l in range(depth):
            pieces = refs[5 * l:5 * l + 5]
            for k in range(N_DEV):
                for off, width, mine in _w_in_windows(k):
                    for p_ref, start, n in zip(pieces, Z_OFFSETS, Z_PIECES):
                        a, b = max(mine, start), min(mine + width, start + n)
                        if b > a:
                            o_ref[k % 2, k // 2, l, :, off + a - mine:off + b - mine] = (
                                p_ref[:, a - start:b - start].astype(o_ref.dtype))

    row = lambda i: (i, 0)
    flat = [piece for layer in dws for piece in layer]
    return pl.pallas_call(
        body, grid=(D // rb,),
        in_specs=[pl.BlockSpec((rb, n), row) for _ in range(depth) for n in Z_PIECES],
        out_specs=pl.BlockSpec((2, N_DEV // 2, depth, rb, W_IN_SHARD), lambda i: (0, 0, 0, i, 0)),
        out_shape=jax.ShapeDtypeStruct((2, N_DEV // 2, depth, D, W_IN_SHARD), WIRE['w_in']),
        name=name, compiler_params=_cp("parallel"))(*flat)


EARLY = ('w_in', 'ssm_w_glu', 'dn_conv_w')
LATE = ('w_out', 'w_ple_gate', 'w_ple')


def _layer_weights(names, gathered, layer):
    full = {}
    for n, g in zip(names, gathered):
        full[n] = _assemble_w_in(g, f"assemble_w_in_l{layer}") if n == 'w_in' else _from_gathered(g, SHARDED[n])
        if n == 'ssm_w_glu':
            full[n] = full[n].astype(F32)
    return full


def _pair_reduce(names, layer_grads, tag):
    dest = [_split_dw_in([layer_grads[n]], f"split_dw_in_{tag}") if n == 'w_in'
            else _to_dest_blocks(layer_grads[n][None], SHARDED[n], WIRE[n]) for n in names]
    c = lax.axis_index("c")
    own = [lax.dynamic_index_in_dim(d, c, 0, keepdims=False) for d in dest]
    for_sibling = [lax.dynamic_index_in_dim(d, 1 - c, 0, keepdims=False) for d in dest]
    from_sibling = _pair_exchange(for_sibling, f"grads_pair_exchange_{tag}")
    return [_add_pair(a, b, f"grads_pair_sum_{n}_{tag}") for n, a, b in zip(names, own, from_sibling)]


def _local_step(x, p, wts, sharded, target, shards=None, reduce_early=False):
    sharded = [dict(d or {}) for d in sharded]
    bl, s, _ = x.shape
    t = bl * s
    depth = p.shape[0]
    tb, sg_tb = TB, SG_TB

    def by_example(a):
        return a.reshape(bl, s, a.shape[-1])

    def flat(a):
        return a.reshape(t, a.shape[-1])

    xs = [x.reshape(t, D)]
    p_all = p.reshape(depth, t, D_PLE)
    saved = []
    for i in range(depth):
        li = f"l{i}"
        ng = wts['norm_g'][i].reshape(1, D)
        lw = sharded[i]
        w_in = lw['w_in'][0]
        s5_par_in = (wts['ssm_a_re'][i], wts['ssm_a_im'][i], wts['ssm_b_re'][i], wts['ssm_b_im'][i],
                     wts['ssm_c_re'][i], wts['ssm_c_im'][i], wts['ssm_d'][i], wts['ssm_log_step'][i])
        tabs, tab_vjp = jax.vjp(_s5_tables, *s5_par_in)
        s5_par = (*tabs, lw['ssm_w_glu'][0], wts['ssm_b_glu'][i].reshape(1, D_SSM))
        s5_const = _s5_powers(wts['ssm_a_re'][i], wts['ssm_a_im'][i], wts['ssm_log_step'][i])
        conv8 = jnp.pad(lw['dn_conv_w'][0], ((0, 4), (0, 0)))
        dn_par = (jnp.repeat(wts['dn_a_log'][i], DH).reshape(1, D_DN), jnp.repeat(wts['dn_dt_bias'][i], DH).reshape(1, D_DN),
                  wts['dn_norm_g'][i].reshape(1, DH))
        sg_par = (wts['sg_ln_g'][i].reshape(1, D_SG), wts['sg_ln_b'][i].reshape(1, D_SG), wts['sg_w'][i],
                  jnp.pad(jnp.transpose(wts['sg_b'][i]), ((0, 0), (0, LANES - 4))))

        z_ssm, z_qkv, z_gdn, z_sg, z_ab, qkvn = _in_proj_fwd(xs[i], ng, w_in, conv8, s, tb, f"in_proj_fwd_{li}")
        wanted = [(i, n) for n in LATE if n not in lw]
        if i + 1 < depth:
            wanted += [(i + 1, n) for n in SHARDED_ORDER if n not in sharded[i + 1]]
        (y_ssm, carries, h_all, y_dn, states, tinvs), gathered = _mix_fwd(
            by_example(z_ssm), s5_par, s5_const, by_example(qkvn), by_example(z_ab), by_example(z_gdn), dn_par, bl, s,
            f"mix_fwd_{li}", gather=[shards[l][n] for l, n in wanted])
        for l in sorted({ll for ll, _ in wanted}):
            names = [n for ll, n in wanted if ll == l]
            sharded[l].update(_layer_weights(names, [g for (ll, _), g in zip(wanted, gathered) if ll == l], l))
        out_par = (lw['w_out'][0].astype(BF16), wts['ple_norm_g'][i].reshape(1, D), lw['w_ple_gate'][0].astype(BF16),
                   lw['w_ple'][0].astype(BF16))
        y_sg = _sg_fwd(z_sg, sg_par, sg_tb, f"sg_fwd_{li}")
        ys = (flat(y_ssm), flat(y_dn), y_sg)
        if i + 1 < depth:
            x_next, x1, gate = _out_fwd(xs[i], ys, p_all, i, *out_par, tb, f"out_fwd_{li}")
            xs.append(x_next)
        else:
            dx, x1, gate, dfg, loss_vec = _out_fwd(xs[i], ys, p_all, i, *out_par, tb, f"out_fwd_{li}",
                                                   head=(wts['final_norm_g'].reshape(1, D), target.reshape(t, D)))
        saved.append(dict(ng=ng, w_in=w_in, tab_vjp=tab_vjp, s5_par=s5_par, s5_const=s5_const, conv8=conv8, dn_par=dn_par,
                          sg_par=sg_par, out_par=out_par, z=(z_ssm, z_qkv, z_gdn, z_sg, z_ab), carries=carries,
                          h_all=h_all, qkvn=qkvn, x1=x1, gate=gate,
                          states=states, tinvs=tinvs, ys=ys))

    grads = {n: [None] * depth for n in WEIGHTS if n != 'final_norm_g'}
    grads['final_norm_g'] = dfg.reshape(D)
    pair_sums, by_chip = {}, {}
    for i in reversed(range(depth)):
        li = f"l{i}"
        sv = saved[i]
        z_ssm, z_qkv, z_gdn, z_sg, z_ab = sv['z']
        dx_res, dy_ssm, dy_dn, dy_sg, dwo, dpg, dwg, dwp = _out_bwd(sv['x1'], sv['gate'], sv['ys'], p_all, i, dx, *sv['out_par'], tb,
                                                                    f"out_bwd_{li}")
        dz_sg, dlng, dlnb, dsgw, dbsp = _sg_bwd(z_sg, dy_sg, sv['sg_par'], sg_tb, f"sg_bwd_{li}")
        if reduce_early:
            for n, gsum in zip(LATE, _pair_reduce(LATE, {'w_out': dwo, 'w_ple_gate': dwg, 'w_ple': dwp}, f"late_{li}")):
                pair_sums[(i, n)] = gsum
        travelling = [key for key in pair_sums if key not in by_chip]
        (dz_ssm, dbb, dcb, dlam, ddv, dwglu, dbglu), (dqkvn, dz_ab, dz_gdn, dal, ddt, dng), exchanged = _mix_bwd(
            by_example(z_ssm), sv['carries'], sv['h_all'], by_example(dy_ssm), sv['s5_par'], sv['s5_const'],
            by_example(sv['qkvn']), by_example(z_ab), by_example(z_gdn), sv['states'], sv['tinvs'], by_example(dy_dn),
            sv['dn_par'], bl, s, f"mix_bwd_{li}", exchange=[pair_sums[key] for key in travelling])
        by_chip.update(zip(travelling, exchanged))
        dx, dnorm, dz_qkv, dconv = _in_proj_bwd_dx(xs[i], sv['ng'], sv['w_in'], sv['conv8'], flat(dz_ssm), flat(dz_gdn), dz_sg,
                                                   flat(dz_ab), z_qkv, flat(dqkvn), dx_res, s, tb, f"in_proj_bwd_dx_{li}")
        dzs = (flat(dz_ssm), dz_qkv, flat(dz_gdn), dz_sg, flat(dz_ab))
        dws = _in_proj_bwd_dw(xs[i], sv['ng'], dzs, min(TB_DW, t), f"in_proj_bwd_dw_{li}")
        ds5 = sv['tab_vjp']((dbb, dcb, dlam, ddv))
        for n, gval in zip(('ssm_a_re', 'ssm_a_im', 'ssm_b_re', 'ssm_b_im', 'ssm_c_re', 'ssm_c_im', 'ssm_d', 'ssm_log_step'), ds5):
            grads[n][i] = gval
        grads['norm_g'][i] = dnorm.reshape(D)
        grads['w_in'][i] = dws
        grads['ssm_w_glu'][i] = dwglu
        grads['ssm_b_glu'][i] = dbglu.reshape(D_SSM)
        grads['dn_conv_w'][i] = dconv[:4]
        grads['dn_a_log'][i] = dal.reshape(H, DH).sum(axis=1)
        grads['dn_dt_bias'][i] = ddt.reshape(H, DH).sum(axis=1)
        grads['dn_norm_g'][i] = dng.reshape(DH)
        grads['sg_ln_g'][i] = dlng.reshape(D_SG)
        grads['sg_ln_b'][i] = dlnb.reshape(D_SG)
        grads['sg_w'][i] = dsgw
        grads['sg_b'][i] = jnp.transpose(dbsp[:, :4])
        grads['w_out'][i] = dwo
        grads['ple_norm_g'][i] = dpg.reshape(D)
        grads['w_ple_gate'][i] = dwg
        grads['w_ple'][i] = dwp
        if reduce_early:
            for n, gsum in zip(EARLY, _pair_reduce(EARLY, {n: grads[n][i] for n in EARLY}, f"early_{li}")):
                pair_sums[(i, n)] = gsum
    grads = {n: (g if n in ('final_norm_g', 'w_in') else jnp.stack(g)) for n, g in grads.items()}
    return loss_vec[0, 0], dx.reshape(bl, s, D), grads, {k: v for k, v in pair_sums.items() if k not in by_chip}, by_chip


def kernel(x, p, norm_g, w_in, ssm_a_re, ssm_a_im, ssm_b_re, ssm_b_im, ssm_c_re, ssm_c_im, ssm_d, ssm_log_step, ssm_w_glu, ssm_b_glu, dn_conv_w, dn_a_log, dn_dt_bias, dn_norm_g, sg_ln_g, sg_ln_b, sg_w, sg_b, w_out, ple_norm_g, w_ple_gate, w_ple, final_norm_g, loss_target, m_norm_g, m_w_in, m_ssm_a_re, m_ssm_a_im, m_ssm_b_re, m_ssm_b_im, m_ssm_c_re, m_ssm_c_im, m_ssm_d, m_ssm_log_step, m_ssm_w_glu, m_ssm_b_glu, m_dn_conv_w, m_dn_a_log, m_dn_dt_bias, m_dn_norm_g, m_sg_ln_g, m_sg_ln_b, m_sg_w, m_sg_b, m_w_out, m_ple_norm_g, m_w_ple_gate, m_w_ple, m_final_norm_g, v_norm_g, v_w_in, v_ssm_a_re, v_ssm_a_im, v_ssm_b_re, v_ssm_b_im, v_ssm_c_re, v_ssm_c_im, v_ssm_d, v_ssm_log_step, v_ssm_w_glu, v_ssm_b_glu, v_dn_conv_w, v_dn_a_log, v_dn_dt_bias, v_dn_norm_g, v_sg_ln_g, v_sg_ln_b, v_sg_w, v_sg_b, v_w_out, v_ple_norm_g, v_w_ple_gate, v_w_ple, v_final_norm_g):
    w_loc = dict(zip(WEIGHTS, (norm_g, w_in, ssm_a_re, ssm_a_im, ssm_b_re, ssm_b_im, ssm_c_re, ssm_c_im, ssm_d, ssm_log_step,
                               ssm_w_glu, ssm_b_glu, dn_conv_w, dn_a_log, dn_dt_bias, dn_norm_g, sg_ln_g, sg_ln_b, sg_w, sg_b,
                               w_out, ple_norm_g, w_ple_gate, w_ple, final_norm_g)))
    m_loc = dict(zip(WEIGHTS, (m_norm_g, m_w_in, m_ssm_a_re, m_ssm_a_im, m_ssm_b_re, m_ssm_b_im, m_ssm_c_re, m_ssm_c_im, m_ssm_d,
                               m_ssm_log_step, m_ssm_w_glu, m_ssm_b_glu, m_dn_conv_w, m_dn_a_log, m_dn_dt_bias, m_dn_norm_g,
                               m_sg_ln_g, m_sg_ln_b, m_sg_w, m_sg_b, m_w_out, m_ple_norm_g, m_w_ple_gate, m_w_ple, m_final_norm_g)))
    v_loc = dict(zip(WEIGHTS, (v_norm_g, v_w_in, v_ssm_a_re, v_ssm_a_im, v_ssm_b_re, v_ssm_b_im, v_ssm_c_re, v_ssm_c_im, v_ssm_d,
                               v_ssm_log_step, v_ssm_w_glu, v_ssm_b_glu, v_dn_conv_w, v_dn_a_log, v_dn_dt_bias, v_dn_norm_g,
                               v_sg_ln_g, v_sg_ln_b, v_sg_w, v_sg_b, v_w_out, v_ple_norm_g, v_w_ple_gate, v_w_ple, v_final_norm_g)))

    depth = p.shape[0]
    shards = [{n: w_loc[n][l:l + 1].astype(WIRE[n]) for n in SHARDED_ORDER} for l in range(depth)]
    first = _layer_weights(EARLY, _all_gather([shards[0][n] for n in EARLY], "gather_weights_l0"), 0)

    loss_part, grad_x, grads, left, by_chip = _local_step(
        x, p, w_loc, [first] + [None] * (depth - 1), loss_target, shards, reduce_early=True)

    rep_pack = _pack([grads[n] for n in REP_NARROW]).astype(WIRE['replicated'])
    fine_pack = _pack([grads['final_norm_g'], loss_part.reshape(1)])
    exchanged, (rep_recv, fine_recv) = _chip_exchange(list(left.values()), [rep_pack, fine_pack], "grads_chip_exchange_l0")
    by_chip.update(zip(left, exchanged))
    by_chip = [jnp.concatenate([by_chip[(l, n)] for l in range(depth)], axis=1) for n in SHARDED_ORDER]

    outs = {k: {} for k in 'gdmv'}
    for n, gk in zip(SHARDED_ORDER, by_chip):
        for k, o in zip('gdmv', _sum_adamw(gk, w_loc[n], m_loc[n], v_loc[n], f"adamw_{n}")):
            outs[k][n] = o
    rep_out = _sum_adamw(rep_recv, *[_pack([src[n] for n in REP_NARROW]) for src in (w_loc, m_loc, v_loc)], "adamw_replicated")
    for k, rep_p in zip('gdmv', rep_out):
        outs[k].update(zip(REP_NARROW, _unpack(rep_p, [w_loc[n].shape for n in REP_NARROW])))
    one = jnp.zeros((1,), F32)
    fine_out = _sum_adamw(fine_recv, *[_pack([src['final_norm_g'], one]) for src in (w_loc, m_loc, v_loc)], "adamw_final_norm")
    for k, fine_p in zip('gdmv', fine_out):
        outs[k].update(zip(['final_norm_g', 'loss'], _unpack(fine_p, [w_loc['final_norm_g'].shape, (1,)])))
    loss = outs['g']['loss'].reshape(())
    return (loss, grad_x, *[outs['g'][n] for n in WEIGHTS], *[outs['d'][n] for n in WEIGHTS],
            *[outs['m'][n] for n in WEIGHTS], *[outs['v'][n] for n in WEIGHTS])
```

```python
import jax
import jax.numpy as jnp
from jax import lax
from jax.experimental import pallas as pl
from jax.experimental.pallas import tpu as pltpu

F32 = jnp.float32
BF16 = jnp.bfloat16
EPS = 1e-6

D = 1024
D_PLE = 256
D_SSM = 256
D_DN = 512
D_SG = 256
G = 16
CG = 16
NS = 64
NRE = G * NS
H = 4
DH = 128
DN_C = 128
SG_C = 128
ZW = 3456
Z_PIECES = (512, 1536, 512, 768, 128)
N_DEV = 8
LANES = 128
PACK_ROWS = 256
VMEM_LIMIT = 56 * 1024 * 1024
ELEMENTWISE_STEP_BYTES = 4 * 1024 * 1024
TB = 256
SG_TB = 512
TB_DW = 1024

ADAM_LR = 0.001
ADAM_B1 = 0.9
ADAM_B2 = 0.999
ADAM_EPS = 1e-08
ADAM_WD = 0.01
ADAM_STEP = 10

MIX_HEAD_START = 3
S5_L = 128
S5_GROUP = 8
S5_SHIFTS = (1, 2, 4)

WEIGHTS = ['norm_g', 'w_in', 'ssm_a_re', 'ssm_a_im', 'ssm_b_re', 'ssm_b_im', 'ssm_c_re', 'ssm_c_im', 'ssm_d',
           'ssm_log_step', 'ssm_w_glu', 'ssm_b_glu', 'dn_conv_w', 'dn_a_log', 'dn_dt_bias', 'dn_norm_g', 'sg_ln_g',
           'sg_ln_b', 'sg_w', 'sg_b', 'w_out', 'ple_norm_g', 'w_ple_gate', 'w_ple', 'final_norm_g']
SHARDED = {'w_in': 2, 'ssm_w_glu': 1, 'dn_conv_w': 2, 'w_out': 1, 'w_ple_gate': 1, 'w_ple': 2}
SHARDED_ORDER = ['w_in', 'ssm_w_glu', 'dn_conv_w', 'w_out', 'w_ple_gate', 'w_ple']
WIRE = {'w_in': BF16, 'ssm_w_glu': BF16, 'dn_conv_w': F32, 'w_out': BF16, 'w_ple_gate': BF16, 'w_ple': BF16,
        'replicated': BF16}
REPLICATED_ORDER = [n for n in WEIGHTS if n not in SHARDED]
REP_NARROW = [n for n in REPLICATED_ORDER if n != 'final_norm_g']


def _cp(*sem):
    return pltpu.CompilerParams(dimension_semantics=sem, vmem_limit_bytes=VMEM_LIMIT)


def _dg(a, b, ca, cb, precision=None):
    return lax.dot_general(a, b, (((ca,), (cb,)), ((), ())), precision=precision, preferred_element_type=F32)


@jax.custom_vjp
def _mm(a, b):
    return _dg(a.astype(BF16), b.astype(BF16), 1, 0)


def _mm_fwd(a, b):
    return _mm(a, b), (a, b)


def _mm_bwd(res, g):
    a, b = res
    gb = g.astype(BF16)
    return _dg(gb, b.astype(BF16), 1, 1), _dg(a.astype(BF16), gb, 0, 0)


_mm.defvjp(_mm_fwd, _mm_bwd)


@jax.custom_vjp
def _mm_nt(a, b):
    return _dg(a.astype(BF16), b.astype(BF16), 1, 1)


def _mm_nt_fwd(a, b):
    return _mm_nt(a, b), (a, b)


def _mm_nt_bwd(res, g):
    a, b = res
    gb = g.astype(BF16)
    return _dg(gb, b.astype(BF16), 1, 0), _dg(gb, a.astype(BF16), 0, 0)


_mm_nt.defvjp(_mm_nt_fwd, _mm_nt_bwd)


@jax.custom_vjp
def _mm_tn(a, b):
    return _dg(a.astype(BF16), b.astype(BF16), 0, 0)


def _mm_tn_fwd(a, b):
    return _mm_tn(a, b), (a, b)


def _mm_tn_bwd(res, g):
    a, b = res
    gb = g.astype(BF16)
    return _dg(b.astype(BF16), gb, 1, 1), _dg(a.astype(BF16), gb, 1, 0)


_mm_tn.defvjp(_mm_tn_fwd, _mm_tn_bwd)


def _split(x, n):
    pieces = []
    for _ in range(n - 1):
        hi = x.astype(BF16)
        pieces.append(hi)
        x = x - hi.astype(F32)
    pieces.append(x.astype(BF16))
    return pieces


def _dg3(a, b, ca, cb):
    a_hi, a_lo = _split(a, 2)
    b_hi, b_lo = _split(b, 2)
    return _dg(a_hi, b_hi, ca, cb) + (_dg(a_hi, b_lo, ca, cb) + _dg(a_lo, b_hi, ca, cb))


@jax.custom_vjp
def _dot3(a, b):
    return _dg3(a, b, 1, 0)


def _dot3_fwd(a, b):
    return _dot3(a, b), (a, b)


def _dot3_bwd(res, g):
    a, b = res
    return _dg3(g, b, 1, 1), _dg3(a, g, 0, 0)


_dot3.defvjp(_dot3_fwd, _dot3_bwd)


def _dg_sel(x, e, cx, ce, x_first):
    eb = e.astype(BF16)
    out = None
    for piece in reversed(_split(x, 3)):
        term = _dg(piece, eb, cx, ce) if x_first else _dg(eb, piece, ce, cx)
        out = term if out is None else out + term
    return out


@jax.custom_vjp
def _sel_r(x, e):
    return _dg_sel(x, e, 1, 0, True)


def _sel_r_fwd(x, e):
    return _sel_r(x, e), e


def _sel_r_bwd(e, g):
    return _dg_sel(g, e, 1, 1, True), jnp.zeros_like(e)


_sel_r.defvjp(_sel_r_fwd, _sel_r_bwd)


@jax.custom_vjp
def _sel_l(e, x):
    return _dg_sel(x, e, 0, 1, False)


def _sel_l_fwd(e, x):
    return _sel_l(e, x), e


def _sel_l_bwd(e, g):
    return jnp.zeros_like(e), _dg_sel(g, e, 0, 0, False)


_sel_l.defvjp(_sel_l_fwd, _sel_l_bwd)


def _rms(x, g):
    return x * lax.rsqrt(jnp.mean(x * x, axis=-1, keepdims=True) + EPS) * g


def _silu(x):
    return x * jax.nn.sigmoid(x)


Z_OFFSETS = (0, 512, 2048, 2560, 3328)


def _dn_post(c):
    s = _silu(c)
    parts = []
    for j in range(12):
        xj = s[:, j * DH:(j + 1) * DH]
        if j < 8:
            xj = xj * lax.rsqrt(jnp.sum(xj * xj, axis=-1, keepdims=True) + EPS)
        if j < 4:
            xj = xj * (DH ** -0.5)
        parts.append(xj)
    return jnp.concatenate(parts, axis=1)


def _dn_conv(ext, cw_ref, rows):
    c = None
    for k in range(4):
        sh = ext if k == 3 else pltpu.roll(ext, 3 - k, 0)
        term = cw_ref[k:k + 1, :] * sh[ext.shape[0] - rows:, :]
        c = term if c is None else c + term
    return c


def _dn_prep_vjp(prev, cur, nxt, d_cur, d_nxt, cw_ref, tb):
    ext = jnp.concatenate([prev, cur, nxt], axis=0)
    shifted = [ext if k == 3 else pltpu.roll(ext, 3 - k, 0) for k in range(4)]
    c2 = None
    for k in range(4):
        term = cw_ref[k:k + 1, :] * shifted[k][8:, :]
        c2 = term if c2 is None else c2 + term
    _, vjp = jax.vjp(_dn_post, c2)
    (dc2,) = vjp(jnp.concatenate([d_cur, d_nxt], axis=0))
    dz, dcw = None, []
    for k in range(4):
        up = dc2 if k == 3 else pltpu.roll(dc2, tb + 8 - (3 - k), 0)
        term = cw_ref[k:k + 1, :] * up[:tb, :]
        dz = term if dz is None else dz + term
        dcw.append(jnp.sum(dc2[:tb, :] * shifted[k][8:8 + tb, :], axis=0, keepdims=True))
    return dz, dcw


def _in_proj_fwd(x, g, w, conv_w8, s, tb, name):
    t = x.shape[0]
    n_s = s // tb
    w3 = 3 * D_DN
    q0, q1 = Z_OFFSETS[1], Z_OFFSETS[2]

    def body(x_ref, g_ref, w_ref, cw_ref, zs_ref, zq_ref, zg_ref, zsg_ref, zab_ref, qkvn_ref, halo):
        h = _rms(x_ref[...], g_ref[...]).astype(BF16)
        zq = jnp.dot(h, w_ref[:, q0:q1], preferred_element_type=F32)
        zq_ref[...] = zq
        prev = jnp.where(pl.program_id(0) % n_s == 0, 0.0, halo[...])
        qkvn_ref[...] = _dn_post(_dn_conv(jnp.concatenate([prev, zq], axis=0), cw_ref, tb))
        halo[...] = zq[tb - 8:, :]
        zs_ref[...] = jnp.dot(h, w_ref[:, :q0], preferred_element_type=F32)
        rest = jnp.dot(h, w_ref[:, q1:], preferred_element_type=F32)
        zg_ref[...] = rest[:, :Z_PIECES[2]]
        zsg_ref[...] = rest[:, Z_PIECES[2]:Z_PIECES[2] + Z_PIECES[3]]
        zab_ref[...] = rest[:, Z_PIECES[2] + Z_PIECES[3]:]

    row = lambda i: (i, 0)
    full = lambda i: (0, 0)
    widths = Z_PIECES + (w3,)
    return pl.pallas_call(
        body, grid=(t // tb,),
        in_specs=[pl.BlockSpec((tb, D), row), pl.BlockSpec((1, D), full), pl.BlockSpec((D, ZW), full), pl.BlockSpec((8, w3), full)],
        out_specs=[pl.BlockSpec((tb, n), row) for n in widths],
        out_shape=[jax.ShapeDtypeStruct((t, n), F32) for n in widths],
        scratch_shapes=[pltpu.VMEM((8, w3), F32)],
        name=name, compiler_params=_cp("arbitrary"))(x, g, w, conv_w8)


def _in_proj_bwd_dx(x, g, w, conv_w8, dz_ssm, dz_gdn, dz_sg, dz_ab, zq, dqkvn, dx_res, s, tb, name):
    t = x.shape[0]
    n_s = s // tb
    hb = tb // 8
    w3 = 3 * D_DN
    q0, q1 = Z_OFFSETS[1], Z_OFFSETS[2]

    def body(x_ref, g_ref, w_ref, cw_ref, ds_ref, dgd_ref, dsg_ref, dab_ref, cur_ref, prev_ref, next_ref, dq_ref, dqn_ref,
             dxr_ref, dx_ref, dg_ref, dzq_ref, dcw_ref):
        i = pl.program_id(0)

        @pl.when(i == 0)
        def _():
            dg_ref[...] = jnp.zeros_like(dg_ref)
            dcw_ref[...] = jnp.zeros_like(dcw_ref)

        rest = jnp.concatenate([dgd_ref[...], dsg_ref[...], dab_ref[...]], axis=1)
        dh = _dg(ds_ref[...], w_ref[:, :q0], 1, 1) + _dg(rest, w_ref[:, q1:], 1, 1)
        first, last = i % n_s == 0, i % n_s == n_s - 1
        dzq, dcw = _dn_prep_vjp(jnp.where(first, 0.0, prev_ref[...]), cur_ref[...], jnp.where(last, 0.0, next_ref[...]),
                                dq_ref[...], jnp.where(last, 0.0, dqn_ref[...]), cw_ref, tb)
        for k in range(4):
            dcw_ref[k:k + 1, :] += dcw[k]
        dzq = dzq.astype(BF16)
        dzq_ref[...] = dzq
        dh = dh + _dg(dzq, w_ref[:, q0:q1], 1, 1)
        _, vjp = jax.vjp(_rms, x_ref[...], g_ref[...])
        dx, dg = vjp(dh)
        dx_ref[...] = dx + dxr_ref[...]
        dg_ref[...] += dg

    n_blk8 = t // 8
    row = lambda i: (i, 0)
    prv = lambda i: (jnp.maximum(i * hb - 1, 0), 0)
    nxt = lambda i: (jnp.minimum((i + 1) * hb, n_blk8 - 1), 0)
    full = lambda i: (0, 0)
    return pl.pallas_call(
        body, grid=(t // tb,),
        in_specs=[pl.BlockSpec((tb, D), row), pl.BlockSpec((1, D), full), pl.BlockSpec((D, ZW), full), pl.BlockSpec((8, w3), full)]
        + [pl.BlockSpec((tb, n), row) for n in (Z_PIECES[0], Z_PIECES[2], Z_PIECES[3], Z_PIECES[4])]
        + [pl.BlockSpec((tb, w3), row), pl.BlockSpec((8, w3), prv), pl.BlockSpec((8, w3), nxt),
           pl.BlockSpec((tb, w3), row), pl.BlockSpec((8, w3), nxt), pl.BlockSpec((tb, D), row)],
        out_specs=[pl.BlockSpec((tb, D), row), pl.BlockSpec((1, D), full), pl.BlockSpec((tb, w3), row), pl.BlockSpec((8, w3), full)],
        out_shape=[jax.ShapeDtypeStruct((t, D), F32), jax.ShapeDtypeStruct((1, D), F32), jax.ShapeDtypeStruct((t, w3), BF16),
                   jax.ShapeDtypeStruct((8, w3), F32)],
        name=name, compiler_params=_cp("arbitrary"))(x, g, w, conv_w8, dz_ssm, dz_gdn, dz_sg, dz_ab, zq, zq, zq, dqkvn, dqkvn, dx_res)


def _in_proj_bwd_dw(x, g, dzs, tb, name):
    t = x.shape[0]

    def body(x_ref, g_ref, d0, d1, d2, d3, d4, *dw_refs):
        @pl.when(pl.program_id(0) == 0)
        def _():
            for r in dw_refs:
                r[...] = jnp.zeros_like(r)

        h = _rms(x_ref[...], g_ref[...]).astype(BF16)
        for d_ref, dw_ref in zip((d0, d1, d2, d3, d4), dw_refs):
            dw_ref[...] += _dg(h, d_ref[...].astype(BF16), 0, 0)

    row = lambda i: (i, 0)
    full = lambda i: (0, 0)
    return pl.pallas_call(
        body, grid=(t // tb,),
        in_specs=[pl.BlockSpec((tb, D), row), pl.BlockSpec((1, D), full)] + [pl.BlockSpec((tb, n), row) for n in Z_PIECES],
        out_specs=[pl.BlockSpec((D, n), full) for n in Z_PIECES],
        out_shape=[jax.ShapeDtypeStruct((D, n), F32) for n in Z_PIECES],
        name=name, compiler_params=_cp("arbitrary"))(x, g, *dzs)


def _lam_pow(a_re, a_im, log_step, k):
    step = jnp.exp(log_step)[:, None]
    mag = jnp.exp(k * a_re * step)
    ang = k * a_im * step
    return mag * jnp.cos(ang), mag * jnp.sin(ang)


def _s5_powers(a_re, a_im, log_step):
    def table(ks):
        re, im = _lam_pow(a_re, a_im, log_step, jnp.asarray(ks, F32)[:, None, None])
        return jnp.concatenate([re.reshape(len(ks), NRE), im.reshape(len(ks), NRE)], axis=-1)

    ld = table(S5_SHIFTS).reshape(len(S5_SHIFTS), 1, 2 * NRE)
    return ld, table(range(1, S5_GROUP + 1)), table(range(S5_GROUP, 0, -1))


def _s5_tables(a_re, a_im, b_re, b_im, c_re, c_im, d_skip, log_step):
    lam_re, lam_im = _lam_pow(a_re, a_im, log_step, 1.0)
    den = a_re * a_re + a_im * a_im
    nr, ni = lam_re - 1.0, lam_im
    f_re = (nr * a_re + ni * a_im) / den
    f_im = (ni * a_re - nr * a_im) / den
    bbar_re = f_re[..., None] * b_re - f_im[..., None] * b_im
    bbar_im = f_re[..., None] * b_im + f_im[..., None] * b_re
    eye = jnp.eye(G, dtype=F32)

    def blk_b(bb):
        return (jnp.transpose(bb, (0, 2, 1))[:, :, None, :] * eye[:, None, :, None]).reshape(D_SSM, NRE)

    def blk_c(cc):
        return (jnp.transpose(cc, (0, 2, 1))[:, :, None, :] * eye[:, None, :, None]).reshape(NRE, D_SSM)

    b_blk = jnp.concatenate([blk_b(bbar_re), blk_b(bbar_im)], axis=1)
    c_blk = jnp.concatenate([blk_c(c_re), -blk_c(c_im)], axis=0)
    lam = jnp.concatenate([lam_re.reshape(1, NRE), lam_im.reshape(1, NRE)], axis=-1)
    return b_blk, c_blk, lam, d_skip.reshape(1, D_SSM)


def _group_shift(x, d, up=False):
    r = lax.broadcasted_iota(jnp.int32, x.shape, 0) & (S5_GROUP - 1)
    if up:
        return jnp.where(r < S5_GROUP - d, pltpu.roll(x, x.shape[0] - d, 0), 0.0)
    return jnp.where(r >= d, pltpu.roll(x, d, 0), 0.0)


def _s5_scan_steps(hr, hi, cr, ci, lds, lp):
    for ld, d in zip(lds, S5_SHIFTS):
        lr, li = ld[:, :NRE], ld[:, NRE:]
        sr, si = _group_shift(hr, d), _group_shift(hi, d)
        hr, hi = hr + lr * sr - li * si, hi + lr * si + li * sr
        yield
    pr, pi = lp[:, :NRE], lp[:, NRE:]
    rows_r, rows_i = [], []
    for r in range(hr.shape[0] // S5_GROUP):
        br, bi = hr[r * S5_GROUP:(r + 1) * S5_GROUP], hi[r * S5_GROUP:(r + 1) * S5_GROUP]
        br, bi = br + pr * cr - pi * ci, bi + pr * ci + pi * cr
        cr, ci = br[S5_GROUP - 1:S5_GROUP], bi[S5_GROUP - 1:S5_GROUP]
        rows_r.append(br)
        rows_i.append(bi)
        if r % 2:
            yield
    return jnp.concatenate(rows_r, axis=0), jnp.concatenate(rows_i, axis=0)


@jax.custom_vjp
def _known_scan(xr, xi, cr, ci, lam, lds, lp_rev, hr, hi):
    return hr, hi


def _known_scan_fwd(xr, xi, cr, ci, lam, lds, lp_rev, hr, hi):
    return (hr, hi), (cr, ci, lam, lds, lp_rev, hr, hi)


def _known_scan_bwd(res, cts):
    cr, ci, lam, lds, lp_rev, hr, hi = res
    ar, ai = cts
    for ld, d in zip(lds, S5_SHIFTS):
        lr, li = ld[:, :NRE], ld[:, NRE:]
        sr, si = _group_shift(ar, d, up=True), _group_shift(ai, d, up=True)
        ar, ai = ar + lr * sr + li * si, ai + lr * si - li * sr
    qr, qi = lp_rev[:, :NRE], lp_rev[:, NRE:]
    nr, ni = jnp.zeros_like(cr), jnp.zeros_like(ci)
    rows_r, rows_i = [], []
    for r in reversed(range(hr.shape[0] // S5_GROUP)):
        br, bi = ar[r * S5_GROUP:(r + 1) * S5_GROUP], ai[r * S5_GROUP:(r + 1) * S5_GROUP]
        br, bi = br + qr * nr + qi * ni, bi + qr * ni - qi * nr
        nr, ni = br[0:1], bi[0:1]
        rows_r.insert(0, br)
        rows_i.insert(0, bi)
    ar, ai = jnp.concatenate(rows_r, axis=0), jnp.concatenate(rows_i, axis=0)
    lr, li = lam[:, :NRE], lam[:, NRE:]
    dcr, dci = lr * nr + li * ni, lr * ni - li * nr
    first = lax.broadcasted_iota(jnp.int32, hr.shape, 0) == 0
    pr = jnp.where(first, cr, pltpu.roll(hr, 1, 0))
    pi = jnp.where(first, ci, pltpu.roll(hi, 1, 0))
    dlam = jnp.concatenate([jnp.sum(ar * pr + ai * pi, axis=0, keepdims=True),
                            jnp.sum(ai * pr - ar * pi, axis=0, keepdims=True)], axis=1)
    return (ar, ai, dcr, dci, dlam, [jnp.zeros_like(ld) for ld in lds], jnp.zeros_like(lp_rev),
            jnp.zeros_like(hr), jnp.zeros_like(hi))


_known_scan.defvjp(_known_scan_fwd, _known_scan_bwd)


def _interleave(short, long, head_start=0):
    gens = list(short) + list(long)
    results = [None] * len(gens)

    def advance(live):
        still = []
        for idx, gen in live:
            try:
                next(gen)
                still.append((idx, gen))
            except StopIteration as done:
                results[idx] = done.value
        return still

    live_short = advance(list(enumerate(gens))[:len(short)])
    live_long = list(enumerate(gens))[len(short):]
    for _ in range(head_start):
        live_long = advance(live_long)
    live = live_short + live_long
    while live:
        live = advance(live)
    return results[:len(short)], results[len(short):]


def _s5_chunk_gen(u, gate, cr, ci, b_blk, c_blk, lam, dv, wglu, bglu, lds, lp, lp_rev, known_h=None):
    bu = _mm(u, b_blk)
    xr, xi = bu[:, :NRE], bu[:, NRE:]
    yield
    if known_h is None:
        hr, hi = yield from _s5_scan_steps(xr, xi, cr, ci, lds, lp)
    else:
        hr, hi = _known_scan(xr, xi, cr, ci, lam, lds, lp_rev, *known_h)
    y = _mm(jnp.concatenate([hr, hi], axis=1), c_blk) + dv * u
    yield
    y = jax.nn.gelu(y)
    y = y * jax.nn.sigmoid(_mm(y, wglu) + bglu)
    return y * _silu(gate), hr, hi


S5_PAR_SHAPES = [(D_SSM, 2 * NRE), (2 * NRE, D_SSM), (1, 2 * NRE), (1, D_SSM), (D_SSM, D_SSM), (1, D_SSM)]
S5_CONST_SHAPES = [(len(S5_SHIFTS), 1, 2 * NRE), (S5_GROUP, 2 * NRE), (S5_GROUP, 2 * NRE)]
DN_PAR_SHAPES = [(1, D_DN), (1, D_DN), (1, DH)]


def _mix_specs(bl, n_c, rev):
    def chunk(i):
        return n_c - 1 - i if rev else i

    def tok(n):
        return pl.BlockSpec((bl, DN_C, n), lambda i: (0, chunk(i), 0))

    def per_chunk(shape):
        return pl.BlockSpec((bl, 1, *shape), lambda i: (0, chunk(i)) + (0,) * len(shape))

    def whole(shape):
        return pl.BlockSpec(shape, lambda i: (0,) * len(shape))

    return tok, per_chunk, whole


def _unit_lower_inverse_steps(ms):
    c_len = ms[0].shape[0]
    eye = lax.broadcasted_iota(jnp.int32, (c_len, c_len), 0) == lax.broadcasted_iota(jnp.int32, (c_len, c_len), 1)
    ident = jnp.where(eye, 1.0, 0.0)
    ps = ms
    tinvs = [ident - m for m in ms]
    for _ in range(c_len.bit_length() - 2):
        ps = [_dg3(p, p, 1, 0) for p in ps]
        yield
        tinvs = [t + _dg3(t, p, 1, 0) for t, p in zip(tinvs, ps)]
        yield
    return tinvs


@jax.custom_vjp
def _known_inverses(ms, tinvs):
    return tinvs


def _known_inverses_fwd(ms, tinvs):
    return tinvs, tinvs


def _known_inverses_bwd(tinvs, gs):
    return [-_dg3(_dg3(t, g, 0, 0), t, 1, 1) for t, g in zip(tinvs, gs)], [jnp.zeros_like(t) for t in tinvs]


_known_inverses.defvjp(_known_inverses_fwd, _known_inverses_bwd)


def _dn_chunk_gen(qkv, zab, zg, states, alog_e, dt_e, ng, known_tinvs=None):
    c_len = DN_C
    r = lax.broadcasted_iota(jnp.int32, (c_len, c_len), 0)
    c = lax.broadcasted_iota(jnp.int32, (c_len, c_len), 1)
    causal, strict = r >= c, r > c
    tril = jnp.where(causal, 1.0, 0.0)
    rr = lax.broadcasted_iota(jnp.int32, (LANES, D_DN), 0)
    cc = lax.broadcasted_iota(jnp.int32, (LANES, D_DN), 1)
    e_a = jnp.where((cc >= rr * DH) & (cc < rr * DH + DH) & (rr < H), 1.0, 0.0)
    e_b = jnp.where((cc >= (rr - H) * DH) & (cc < (rr - H) * DH + DH) & (rr >= H) & (rr < 2 * H), 1.0, 0.0)
    a_e = _sel_r(zab, e_a)
    b_e = _sel_r(zab, e_b)
    beta = jax.nn.sigmoid(b_e)
    g = -jnp.exp(alog_e) * jax.nn.softplus(a_e + dt_e)
    yield
    gc = _sel_l(tril, g)
    glast = jnp.sum(g, axis=0, keepdims=True)
    eg = jnp.exp(gc)
    ekd = jnp.exp(glast - gc)
    dl = jnp.exp(glast)
    yield
    heads = range(H)
    sls = [slice(h * DH, (h + 1) * DH) for h in heads]
    qs = [qkv[:, h * DH:(h + 1) * DH] for h in heads]
    ks = [qkv[:, D_DN + h * DH:D_DN + (h + 1) * DH] for h in heads]
    vs = [qkv[:, 2 * D_DN + h * DH:2 * D_DN + (h + 1) * DH] for h in heads]
    ccols = [gc[:, sl] for sl in sls]
    decs = [jnp.where(causal, jnp.exp(jnp.where(causal, cl - jnp.transpose(cl), 0.0)), 0.0) for cl in ccols]
    kbs = [k * beta[:, sl] for k, sl in zip(ks, sls)]
    ms = [jnp.where(strict, _mm_nt(kb, k) * dec, 0.0) for kb, k, dec in zip(kbs, ks, decs)]
    yield
    if known_tinvs is None:
        tinvs = yield from _unit_lower_inverse_steps(ms)
    else:
        tinvs = _known_inverses(ms, list(known_tinvs))
    sols = [_dot3(t, jnp.concatenate([v * beta[:, sl], kb * eg[:, sl]], axis=1))
            for t, v, kb, sl in zip(tinvs, vs, kbs, sls)]
    yield
    atts = [_mm_nt(q, k) * dec for q, k, dec in zip(qs, ks, decs)]
    vnews = [sol[:, :DH] - _mm(sol[:, DH:], st) for sol, st in zip(sols, states)]
    yield
    os_ = [_mm(q * eg[:, sl], st) + _mm(att, vn) for q, sl, st, att, vn in zip(qs, sls, states, atts, vnews)]
    yield
    new_states = [st * dl[:, sl] + _mm_tn(k * ekd[:, sl], vn) for st, sl, k, vn in zip(states, sls, ks, vnews)]
    yield
    ys = [_rms(o, ng) * _silu(zg[:, sl]) for o, sl in zip(os_, sls)]
    return jnp.concatenate(ys, axis=1), new_states, tinvs


def _mix_fwd(zs, s5_par, s5_const, qkv, zab, zg, dn_par, bl, s, name, gather=()):
    assert S5_L == DN_C
    n_c = s // DN_C
    nd = len(S5_SHIFTS)
    m = len(gather)
    tok, per_chunk, whole = _mix_specs(bl, n_c, False)

    def body(*refs):
        (z_ref, b_ref, c_ref, lam_ref, dv_ref, wg_ref, bg_ref, ld_ref, lp_ref, lpr_ref,
         q_ref, ab_ref, zg_ref, al_ref, dt_ref, ng_ref) = refs[:16]
        ys_ref, car_ref, h_ref, yd_ref, st_ref, ti_ref = refs[16 + m:22 + m]
        cs, ssc = refs[22 + 2 * m:24 + 2 * m]
        gathering = (_gather_protocol(_mesh_place(), refs[16:16 + m], refs[22 + m:22 + 2 * m], *refs[24 + 2 * m:])
                     if m else None)

        @pl.when(pl.program_id(0) == 0)
        def _():
            cs[...] = jnp.zeros_like(cs)
            ssc[...] = jnp.zeros_like(ssc)
            if m:
                next(gathering)

        lds = [ld_ref[k] for k in range(nd)]
        s5_gens, dn_gens = [], []
        for e in range(bl):
            c = cs[e]
            car_ref[e, 0] = c
            sts = [ssc[e, h] for h in range(H)]
            for h in range(H):
                st_ref[e, 0, h] = sts[h]
            z = z_ref[e]
            s5_gens.append(_s5_chunk_gen(z[:, :D_SSM], z[:, D_SSM:], c[:, :NRE], c[:, NRE:], b_ref[...], c_ref[...], lam_ref[...],
                                         dv_ref[...], wg_ref[...], bg_ref[...], lds, lp_ref[...], lpr_ref[...]))
            dn_gens.append(_dn_chunk_gen(q_ref[e], ab_ref[e], zg_ref[e], sts, al_ref[...], dt_ref[...], ng_ref[...]))
        s5_outs, dn_outs = _interleave(s5_gens, dn_gens, head_start=MIX_HEAD_START)
        for e in range(bl):
            y_s, hr, hi = s5_outs[e]
            y_d, new_sts, tinvs = dn_outs[e]
            ys_ref[e] = y_s
            h_ref[e, :, :NRE] = hr
            h_ref[e, :, NRE:] = hi
            cs[e, :, :NRE] = hr[S5_L - 1:S5_L]
            cs[e, :, NRE:] = hi[S5_L - 1:S5_L]
            yd_ref[e] = y_d
            for h in range(H):
                ssc[e, h] = new_sts[h]
                ti_ref[e, 0, h] = tinvs[h]

        if m:
            @pl.when(pl.program_id(0) == n_c - 1)
            def _():
                for _ in gathering:
                    pass

    head_mats = jax.ShapeDtypeStruct((bl, n_c, H, DH, DH), F32)
    outs = pl.pallas_call(
        body, grid=(n_c,),
        in_specs=[tok(2 * D_SSM)] + [whole(sh) for sh in S5_PAR_SHAPES + S5_CONST_SHAPES]
        + [tok(3 * D_DN), tok(LANES), tok(D_DN)] + [whole(sh) for sh in DN_PAR_SHAPES] + _hbm_specs(m),
        out_specs=[tok(D_SSM), per_chunk((1, 2 * NRE)), tok(2 * NRE), tok(D_DN), per_chunk((H, DH, DH)), per_chunk((H, DH, DH))]
        + _hbm_specs(m),
        out_shape=[jax.ShapeDtypeStruct((bl, s, D_SSM), F32), jax.ShapeDtypeStruct((bl, n_c, 1, 2 * NRE), F32),
                   jax.ShapeDtypeStruct((bl, s, 2 * NRE), F32), jax.ShapeDtypeStruct((bl, s, D_DN), F32), head_mats, head_mats]
        + [jax.ShapeDtypeStruct((N_DEV, *b.shape), b.dtype) for b in gather],
        scratch_shapes=[pltpu.VMEM((bl, 1, 2 * NRE), F32), pltpu.VMEM((bl, H, DH, DH), F32)] + (_gather_sems(m) if m else []),
        name=name, compiler_params=_cp("arbitrary"))(zs, *s5_par, *s5_const, qkv, zab, zg, *dn_par, *gather)
    return outs[:6], outs[6:]


def _mix_bwd(zs, carries, h_all, dy_s, s5_par, s5_const, qkv, zab, zg, states, tinvs, dy_d, dn_par, bl, s, name, exchange=()):
    n_c = s // DN_C
    nd = len(S5_SHIFTS)
    k_ex = len(exchange)
    tok, per_chunk, whole = _mix_specs(bl, n_c, True)

    def both(examples, s5_tabs, s5_consts, dn_tabs):
        s5_gens = [_s5_chunk_gen(u, gate, cr, ci, *s5_tabs, *s5_consts, known_h=(hr, hi))
                   for u, gate, cr, ci, hr, hi, _, _, _, _, _ in examples]
        dn_gens = [_dn_chunk_gen(q, ab, zgate, sts, *dn_tabs, known_tinvs=known)
                   for _, _, _, _, _, _, q, ab, zgate, sts, known in examples]
        s5_outs, dn_outs = _interleave(s5_gens, dn_gens, head_start=MIX_HEAD_START)
        return [(y_s, hr[S5_L - 1:S5_L], hi[S5_L - 1:S5_L], y_d, new_sts)
                for (y_s, hr, hi), (y_d, new_sts, _) in zip(s5_outs, dn_outs)]

    def body(*refs):
        (z_ref, car_ref, h_ref, dys_ref, b_ref, c_ref, lam_ref, dv_ref, wg_ref, bg_ref, ld_ref, lp_ref, lpr_ref,
         q_ref, ab_ref, zg_ref, st_ref, ti_ref, dyd_ref, al_ref, dt_ref, ng_ref) = refs[:22]
        (dz_ref, db_ref, dc_ref, dlam_ref, ddv_ref, dwg_ref, dbg_ref,
         dq_ref, dab_ref, dzg_ref, dal_ref, ddt_ref, dng_ref) = refs[22 + k_ex:35 + k_ex]
        dcs, dsc = refs[35 + 2 * k_ex:37 + 2 * k_ex]
        accs = (db_ref, dc_ref, dlam_ref, ddv_ref, dwg_ref, dbg_ref, dal_ref, ddt_ref, dng_ref)
        exchanging = (_chip_protocol(_mesh_place(), refs[22:22 + k_ex], refs[35 + k_ex:35 + 2 * k_ex], *refs[37 + 2 * k_ex:])
                      if k_ex else None)

        @pl.when(pl.program_id(0) == 0)
        def _():
            for r in accs + (dcs, dsc):
                r[...] = jnp.zeros_like(r)
            if k_ex:
                next(exchanging)

        examples = []
        for e in range(bl):
            z = z_ref[e]
            c = car_ref[e, 0]
            examples.append((z[:, :D_SSM], z[:, D_SSM:], c[:, :NRE], c[:, NRE:], h_ref[e, :, :NRE], h_ref[e, :, NRE:],
                             q_ref[e], ab_ref[e], zg_ref[e], [st_ref[e, 0, h] for h in range(H)],
                             [ti_ref[e, 0, h] for h in range(H)]))
        _, vjp = jax.vjp(both, examples,
                         (b_ref[...], c_ref[...], lam_ref[...], dv_ref[...], wg_ref[...], bg_ref[...]),
                         ([ld_ref[k] for k in range(nd)], lp_ref[...], lpr_ref[...]),
                         (al_ref[...], dt_ref[...], ng_ref[...]))
        cts = []
        for e in range(bl):
            dc = dcs[e]
            cts.append((dys_ref[e], dc[:, :NRE], dc[:, NRE:], dyd_ref[e], [dsc[e, h] for h in range(H)]))
        d_examples, d_s5, _, d_dn = vjp(cts)
        for e in range(bl):
            du, dgate, dcr, dci, _, _, dq, dab, dzg, dsts, _ = d_examples[e]
            dz_ref[e] = jnp.concatenate([du, dgate], axis=1).astype(BF16)
            dcs[e, :, :NRE] = dcr
            dcs[e, :, NRE:] = dci
            dq_ref[e] = dq
            dab_ref[e] = dab.astype(BF16)
            dzg_ref[e] = dzg.astype(BF16)
            for h in range(H):
                dsc[e, h] = dsts[h]
        for r, ct in zip(accs, (*d_s5, *d_dn)):
            r[...] += ct

        if k_ex:
            @pl.when(pl.program_id(0) == n_c - 1)
            def _():
                for _ in exchanging:
                    pass

    head_mats = per_chunk((H, DH, DH))
    outs = pl.pallas_call(
        body, grid=(n_c,),
        in_specs=[tok(2 * D_SSM), per_chunk((1, 2 * NRE)), tok(2 * NRE), tok(D_SSM)]
        + [whole(sh) for sh in S5_PAR_SHAPES + S5_CONST_SHAPES]
        + [tok(3 * D_DN), tok(LANES), tok(D_DN), head_mats, head_mats, tok(D_DN)] + [whole(sh) for sh in DN_PAR_SHAPES]
        + _hbm_specs(k_ex),
        out_specs=[tok(2 * D_SSM)] + [whole(sh) for sh in S5_PAR_SHAPES]
        + [tok(3 * D_DN), tok(LANES), tok(D_DN)] + [whole(sh) for sh in DN_PAR_SHAPES] + _hbm_specs(k_ex),
        out_shape=[jax.ShapeDtypeStruct((bl, s, 2 * D_SSM), BF16)] + [jax.ShapeDtypeStruct(sh, F32) for sh in S5_PAR_SHAPES]
        + [jax.ShapeDtypeStruct((bl, s, 3 * D_DN), F32), jax.ShapeDtypeStruct((bl, s, LANES), BF16),
           jax.ShapeDtypeStruct((bl, s, D_DN), BF16)]
        + [jax.ShapeDtypeStruct(sh, F32) for sh in DN_PAR_SHAPES]
        + [jax.ShapeDtypeStruct(q.shape, q.dtype) for q in exchange],
        scratch_shapes=[pltpu.VMEM((bl, 1, 2 * NRE), F32), pltpu.VMEM((bl, H, DH, DH), F32)] + (_chip_sems(k_ex) if k_ex else []),
        name=name, compiler_params=_cp("arbitrary"))(
            zs, carries, h_all, dy_s, *s5_par, *s5_const, qkv, zab, zg, states, tinvs, dy_d, *dn_par, *exchange)
    return outs[:7], outs[7:13], outs[13:]


def _sg_fn(n_chunk):
    def f(z, lng, lnb, w, bsp_t):
        u = jax.nn.gelu(z[:, :D_SG])
        v = jax.nn.gelu(z[:, D_SG:2 * D_SG])
        gate = z[:, 2 * D_SG:]
        xc = v - jnp.mean(v, axis=-1, keepdims=True)
        vn = xc * lax.rsqrt(jnp.mean(xc * xc, axis=-1, keepdims=True) + EPS) * lng + lnb
        r = lax.broadcasted_iota(jnp.int32, (SG_C, SG_C), 0)
        c = lax.broadcasted_iota(jnp.int32, (SG_C, SG_C), 1)
        causal = r >= c
        first_half = c < SG_C // 2
        rr = lax.broadcasted_iota(jnp.int32, (LANES, D_SG), 0)
        cc = lax.broadcasted_iota(jnp.int32, (LANES, D_SG), 1)
        expand = jnp.where((cc >= rr * 64) & (cc < rr * 64 + 64) & (rr < 4), 1.0, 0.0)
        bias = _sel_r(bsp_t, expand)
        wm = [jnp.where(causal, w[h], 0.0) for h in range(4)]
        rows = []
        for ci in range(n_chunk):
            vc = vn[ci * SG_C:(ci + 1) * SG_C]
            pairs = []
            for pr in range(2):
                vp = vc[:, pr * LANES:(pr + 1) * LANES]
                pairs.append(jnp.where(first_half, _mm(wm[2 * pr], vp), _mm(wm[2 * pr + 1], vp)))
            rows.append(jnp.concatenate(pairs, axis=1) + bias)
        sp = jnp.concatenate(rows, axis=0) if n_chunk > 1 else rows[0]
        return u * sp * _silu(gate)

    return f


def _sg_specs():
    full = lambda i: (0, 0)
    full3 = lambda i: (0, 0, 0)
    par = [pl.BlockSpec((1, D_SG), full), pl.BlockSpec((1, D_SG), full), pl.BlockSpec((4, SG_C, SG_C), full3),
           pl.BlockSpec((SG_C, LANES), full)]
    par_shapes = [(1, D_SG), (1, D_SG), (4, SG_C, SG_C), (SG_C, LANES)]
    return par, par_shapes


def _sg_fwd(zsg, params, tb, name):
    t = zsg.shape[0]
    f = _sg_fn(tb // SG_C)
    par, _ = _sg_specs()

    def body(z_ref, g_ref, b_ref, w_ref, bs_ref, y_ref):
        y_ref[...] = f(z_ref[...], g_ref[...], b_ref[...], w_ref[...], bs_ref[...])

    row = lambda i: (i, 0)
    return pl.pallas_call(
        body, grid=(t // tb,), in_specs=[pl.BlockSpec((tb, 3 * D_SG), row)] + par,
        out_specs=pl.BlockSpec((tb, D_SG), row), out_shape=jax.ShapeDtypeStruct((t, D_SG), F32),
        name=name, compiler_params=_cp("parallel"))(zsg, *params)


def _sg_bwd(zsg, dy, params, tb, name):
    t = zsg.shape[0]
    f = _sg_fn(tb // SG_C)
    par, par_shapes = _sg_specs()

    def body(z_ref, dy_ref, g_ref, b_ref, w_ref, bs_ref, dz_ref, dg_ref, db_ref, dw_ref, dbs_ref):
        accs = (dg_ref, db_ref, dw_ref, dbs_ref)

        @pl.when(pl.program_id(0) == 0)
        def _():
            for r in accs:
                r[...] = jnp.zeros_like(r)

        _, vjp = jax.vjp(f, z_ref[...], g_ref[...], b_ref[...], w_ref[...], bs_ref[...])
        cts = vjp(dy_ref[...])
        dz_ref[...] = cts[0].astype(BF16)
        for r, ct in zip(accs, cts[1:]):
            r[...] += ct

    row = lambda i: (i, 0)
    return pl.pallas_call(
        body, grid=(t // tb,), in_specs=[pl.BlockSpec((tb, 3 * D_SG), row), pl.BlockSpec((tb, D_SG), row)] + par,
        out_specs=[pl.BlockSpec((tb, 3 * D_SG), row)] + par,
        out_shape=[jax.ShapeDtypeStruct((t, 3 * D_SG), BF16)] + [jax.ShapeDtypeStruct(sh, F32) for sh in par_shapes],
        name=name, compiler_params=_cp("arbitrary"))(zsg, dy, *params)


def _out_fwd(x, ys, p, layer, w_out, pg, w_gate, w_ple, tb, name, head=None):
    t = x.shape[0]

    def body(*refs):
        x_ref, y0, y1, y2, p_ref, wo_ref, pg_ref, wg_ref, wp_ref = refs[:9]
        o_ref, x1_ref, gate_ref = refs[-5:-2] if head else refs[-3:]
        y = jnp.concatenate([y0[...], y1[...], y2[...]], axis=1).astype(BF16)
        x1 = x_ref[...] + jnp.dot(y, wo_ref[...], preferred_element_type=F32)
        hn = _rms(x1, pg_ref[...]).astype(BF16)
        gate = jax.nn.sigmoid(jnp.dot(hn, wg_ref[...], preferred_element_type=F32))
        pp = jnp.dot(p_ref[0].astype(BF16), wp_ref[...], preferred_element_type=F32)
        x2 = x1 + gate * pp
        x1_ref[...] = x1
        gate_ref[...] = gate
        if not head:
            o_ref[...] = x2
            return
        fg_ref, t_ref = refs[9:11]
        dfg_ref, loss_ref = refs[-2:]

        @pl.when(pl.program_id(0) == 0)
        def _():
            dfg_ref[...] = jnp.zeros_like(dfg_ref)
            loss_ref[...] = jnp.zeros_like(loss_ref)

        yf, vjp = jax.vjp(_rms, x2, fg_ref[...])
        err = yf - t_ref[...]
        loss_ref[...] += jnp.zeros_like(loss_ref) + 0.5 * jnp.sum(err * err) / D
        dx2, dfg = vjp(err / D)
        o_ref[...] = dx2
        dfg_ref[...] += dfg

    row = lambda i: (i, 0)
    full = lambda i: (0, 0)
    acts = [pl.BlockSpec((tb, D), row)] * 3
    return pl.pallas_call(
        body, grid=(t // tb,),
        in_specs=[pl.BlockSpec((tb, D), row), pl.BlockSpec((tb, D_SSM), row), pl.BlockSpec((tb, D_DN), row),
                  pl.BlockSpec((tb, D_SG), row), pl.BlockSpec((1, tb, D_PLE), lambda i: (layer, i, 0)), pl.BlockSpec((D, D), full),
                  pl.BlockSpec((1, D), full), pl.BlockSpec((D, D), full), pl.BlockSpec((D_PLE, D), full)]
        + ([pl.BlockSpec((1, D), full), pl.BlockSpec((tb, D), row)] if head else []),
        out_specs=acts + ([pl.BlockSpec((1, D), full), pl.BlockSpec((1, LANES), full)] if head else []),
        out_shape=[jax.ShapeDtypeStruct((t, D), F32)] * 3
        + ([jax.ShapeDtypeStruct((1, D), F32), jax.ShapeDtypeStruct((1, LANES), F32)] if head else []),
        name=name, compiler_params=_cp("arbitrary" if head else "parallel"))(x, *ys, p, w_out, pg, w_gate, w_ple, *(head or ()))


def _out_bwd(x1, gate, ys, p, layer, dx2, w_out, pg, w_gate, w_ple, tb, name):
    t = x1.shape[0]

    def body(x1_ref, gate_ref, y0, y1, y2, p_ref, d_ref, wo_ref, pg_ref, wg_ref, wp_ref,
             dx_ref, dy0, dy1, dy2, dwo_ref, dpg_ref, dwg_ref, dwp_ref):
        accs = (dwo_ref, dpg_ref, dwg_ref, dwp_ref)

        @pl.when(pl.program_id(0) == 0)
        def _():
            for r in accs:
                r[...] = jnp.zeros_like(r)

        y = jnp.concatenate([y0[...], y1[...], y2[...]], axis=1).astype(BF16)
        hn, rms_vjp = jax.vjp(_rms, x1_ref[...], pg_ref[...])
        hb = hn.astype(BF16)
        gate = gate_ref[...]
        pb = p_ref[0].astype(BF16)
        pp = jnp.dot(pb, wp_ref[...], preferred_element_type=F32)
        d2 = d_ref[...]
        dpp = (d2 * gate).astype(BF16)
        dlog = (d2 * pp * gate * (1.0 - gate)).astype(BF16)
        dwp_ref[...] += _dg(pb, dpp, 0, 0)
        dwg_ref[...] += _dg(hb, dlog, 0, 0)
        dx1_n, dpg = rms_vjp(_dg(dlog, wg_ref[...], 1, 1))
        dpg_ref[...] += dpg
        dx1 = d2 + dx1_n
        dx_ref[...] = dx1
        db = dx1.astype(BF16)
        dwo_ref[...] += _dg(y, db, 0, 0)
        dy = _dg(db, wo_ref[...], 1, 1)
        dy0[...] = dy[:, :D_SSM]
        dy1[...] = dy[:, D_SSM:D_SSM + D_DN]
        dy2[...] = dy[:, D_SSM + D_DN:]

    row = lambda i: (i, 0)
    full = lambda i: (0, 0)
    acts = [pl.BlockSpec((tb, D), row), pl.BlockSpec((tb, D_SSM), row), pl.BlockSpec((tb, D_DN), row), pl.BlockSpec((tb, D_SG), row)]
    wts = [pl.BlockSpec((D, D), full), pl.BlockSpec((1, D), full), pl.BlockSpec((D, D), full), pl.BlockSpec((D_PLE, D), full)]
    return pl.pallas_call(
        body, grid=(t // tb,),
        in_specs=[pl.BlockSpec((tb, D), row)] + acts
        + [pl.BlockSpec((1, tb, D_PLE), lambda i: (layer, i, 0)), pl.BlockSpec((tb, D), row)] + wts,
        out_specs=acts + wts,
        out_shape=[jax.ShapeDtypeStruct((t, n), F32) for n in (D, D_SSM, D_DN, D_SG)]
        + [jax.ShapeDtypeStruct(sh, F32) for sh in ((D, D), (1, D), (D, D), (D_PLE, D))],
        name=name, compiler_params=_cp("arbitrary"))(x1, gate, *ys, p, dx2, w_out, pg, w_gate, w_ple)


def _hbm_specs(n):
    return [pl.BlockSpec(memory_space=pl.ANY)] * n


def _gather_protocol(place, ins, outs, send_sems, recv_sems, local_sems):
    n = len(ins)
    x, y, c, other_x, other_y, other_c = place
    me, sibling = (x, y, c), (x, y, other_c)
    chips = [(other_x, y), (x, other_y), (other_x, other_y)]

    def slot(a, px, py, pc):
        return outs[a].at[4 * px + 2 * py + pc]

    def copy(a, k, blk, to, src=None):
        return pltpu.make_async_remote_copy(
            src_ref=slot(a, *blk) if src is None else src, dst_ref=slot(a, *blk),
            send_sem=send_sems.at[7 * a + k], recv_sem=recv_sems.at[7 * a + k],
            device_id=to, device_id_type=pl.DeviceIdType.MESH)

    def own_copies():
        mines = [pltpu.make_async_copy(ins[a], slot(a, *me), local_sems.at[a]) for a in range(n)]
        first = []
        for a in range(n):
            first.append(copy(a, 0, me, sibling, src=ins[a]))
            first += [copy(a, 1 + j, me, (*chip, c), src=ins[a]) for j, chip in enumerate(chips)]
        return mines, first

    mines, first = own_copies()
    for cp in mines + first:
        cp.start()
    yield
    mines, first = own_copies()
    passed = []
    for j, chip in enumerate(chips):
        for a in range(n):
            copy(a, 1 + j, (*chip, c), me).wait_recv()
            onward = copy(a, 4 + j, (*chip, c), sibling)
            onward.start()
            passed.append(onward)
    for a in range(n):
        copy(a, 0, sibling, me).wait_recv()
    for j, chip in enumerate(chips):
        for a in range(n):
            copy(a, 4 + j, (*chip, other_c), me).wait_recv()
    for cp in first + passed:
        cp.wait_send()
    for cp in mines:
        cp.wait()


def _mesh_place():
    x, y, c = lax.axis_index("x"), lax.axis_index("y"), lax.axis_index("c")
    return x, y, c, 1 - x, 1 - y, 1 - c


def _gather_sems(n):
    return [pltpu.SemaphoreType.DMA((7 * n,)), pltpu.SemaphoreType.DMA((7 * n,)), pltpu.SemaphoreType.DMA((n,))]


def _all_gather(blocks, name):
    n = len(blocks)

    def body(*refs):
        for _ in _gather_protocol(_mesh_place(), refs[:n], refs[n:2 * n], *refs[2 * n:]):
            pass

    return pl.pallas_call(
        body, out_shape=[jax.ShapeDtypeStruct((N_DEV, *b.shape), b.dtype) for b in blocks],
        in_specs=_hbm_specs(n), out_specs=_hbm_specs(n), scratch_shapes=_gather_sems(n), name=name)(*blocks)


def _pair_exchange(gs, name):
    n = len(gs)

    def body(*refs):
        ins, recvs = refs[:n], refs[n:2 * n]
        send_sems, recv_sems = refs[2 * n:]
        x, y, c = lax.axis_index("x"), lax.axis_index("y"), lax.axis_index("c")
        remote = [pltpu.make_async_remote_copy(
            src_ref=ins[a], dst_ref=recvs[a], send_sem=send_sems.at[a], recv_sem=recv_sems.at[a],
            device_id=(x, y, 1 - c), device_id_type=pl.DeviceIdType.MESH) for a in range(n)]
        for cp in remote:
            cp.start()
        for cp in remote:
            cp.wait_send()
            cp.wait_recv()

    return pl.pallas_call(
        body, out_shape=[jax.ShapeDtypeStruct(g.shape, g.dtype) for g in gs], in_specs=_hbm_specs(n), out_specs=_hbm_specs(n),
        scratch_shapes=[pltpu.SemaphoreType.DMA((n,)), pltpu.SemaphoreType.DMA((n,))],
        name=name)(*gs)


def _chip_protocol(place, ins, outs, send_sems, recv_sems, local_sems):
    n = len(ins)
    x, y, c, other_x, other_y, _ = place

    def copies():
        my_chip = 2 * x + y
        local = [pltpu.make_async_copy(ins[a].at[my_chip], outs[a].at[my_chip], local_sems.at[a]) for a in range(n)]
        remote = []
        for j in range(1, 4):
            px = other_x if j & 2 else x
            py = other_y if j & 1 else y
            for a in range(n):
                remote.append(pltpu.make_async_remote_copy(
                    src_ref=ins[a].at[2 * px + py], dst_ref=outs[a].at[my_chip],
                    send_sem=send_sems.at[3 * a + j - 1], recv_sem=recv_sems.at[3 * a + j - 1],
                    device_id=(px, py, c), device_id_type=pl.DeviceIdType.MESH))
        return local, remote

    local, remote = copies()
    for cp in local + remote:
        cp.start()
    yield
    local, remote = copies()
    for cp in remote:
        cp.wait_send()
        cp.wait_recv()
    for cp in local:
        cp.wait()


def _chip_sems(n):
    return [pltpu.SemaphoreType.DMA((3 * n,)), pltpu.SemaphoreType.DMA((3 * n,)), pltpu.SemaphoreType.DMA((n,))]


def _chip_exchange(ps, gather, name):
    n, m = len(ps), len(gather)

    def body(*refs):
        ins, g_ins, outs, g_outs = refs[:n], refs[n:n + m], refs[n + m:2 * n + m], refs[2 * n + m:2 * (n + m)]
        place = _mesh_place()
        exchanging = _chip_protocol(place, ins, outs, *refs[2 * (n + m):2 * (n + m) + 3])
        gathering = _gather_protocol(place, g_ins, g_outs, *refs[2 * (n + m) + 3:])
        next(exchanging)
        for _ in gathering:
            pass
        for _ in exchanging:
            pass

    outs = pl.pallas_call(
        body, out_shape=[jax.ShapeDtypeStruct(q.shape, q.dtype) for q in ps]
        + [jax.ShapeDtypeStruct((N_DEV, *b.shape), b.dtype) for b in gather],
        in_specs=_hbm_specs(n + m), out_specs=_hbm_specs(n + m),
        scratch_shapes=_chip_sems(n) + _gather_sems(m), name=name)(*ps, *gather)
    return outs[:n], outs[n:]


def _row_block(rows, bytes_per_row):
    best = None
    for rb in range(16, rows + 1, 16):
        if rows % rb == 0 and rb * bytes_per_row <= ELEMENTWISE_STEP_BYTES:
            best = rb
    return rows if best is None else best


def _add_pair(own, recv, name):
    shape = own.shape
    last = shape[-1]
    rows = own.size // last
    rb = _row_block(rows, 3 * 4 * (-(-last // LANES) * LANES))

    def body(a_ref, b_ref, o_ref):
        o_ref[...] = (a_ref[...].astype(F32) + b_ref[...].astype(F32)).astype(o_ref.dtype)

    row = lambda i: (i, 0)
    out = pl.pallas_call(
        body, grid=(rows // rb,), in_specs=[pl.BlockSpec((rb, last), row)] * 2, out_specs=pl.BlockSpec((rb, last), row),
        out_shape=jax.ShapeDtypeStruct((rows, last), own.dtype), name=name,
        compiler_params=_cp("parallel"))(own.reshape(rows, last), recv.reshape(rows, last))
    return out.reshape(shape)


def _sum_adamw(gk, w, m, v, name):
    shape = w.shape
    n_part = gk.shape[0]
    last = shape[-1]
    rows = w.size // last
    rb = _row_block(rows, (n_part + 7) * 4 * (-(-last // LANES) * LANES))

    def body(g_ref, w_ref, m_ref, v_ref, go_ref, d_ref, mo_ref, vo_ref):
        g = g_ref[0].astype(F32)
        for k in range(1, n_part):
            g = g + g_ref[k].astype(F32)
        mn = ADAM_B1 * m_ref[...] + (1.0 - ADAM_B1) * g
        vn = ADAM_B2 * v_ref[...] + (1.0 - ADAM_B2) * jnp.square(g)
        m_hat = mn / (1.0 - ADAM_B1 ** ADAM_STEP)
        v_hat = vn / (1.0 - ADAM_B2 ** ADAM_STEP)
        go_ref[...] = g
        d_ref[...] = -ADAM_LR * (m_hat / (jnp.sqrt(v_hat) + ADAM_EPS) + ADAM_WD * w_ref[...])
        mo_ref[...] = mn
        vo_ref[...] = vn

    row = lambda i: (i, 0)
    outs = pl.pallas_call(
        body, grid=(rows // rb,),
        in_specs=[pl.BlockSpec((n_part, rb, last), lambda i: (0, i, 0))] + [pl.BlockSpec((rb, last), row)] * 3,
        out_specs=[pl.BlockSpec((rb, last), row)] * 4,
        out_shape=[jax.ShapeDtypeStruct((rows, last), F32)] * 4,
        name=name, compiler_params=_cp("parallel"))(gk.reshape(n_part, rows, last), *[a.reshape(rows, last) for a in (w, m, v)])
    return [o.reshape(shape) for o in outs]


def _seg_rows(shape):
    n = 1
    for d in shape:
        n *= d
    return -(-n // (8 * LANES)) * 8


def _pack(arrs):
    segs = []
    for a in arrs:
        r = _seg_rows(a.shape)
        segs.append(jnp.pad(a.reshape(-1).astype(F32), (0, r * LANES - a.size)).reshape(r, LANES))
    rows = sum(s.shape[0] for s in segs)
    total = -(-rows // PACK_ROWS) * PACK_ROWS
    if total > rows:
        segs.append(jnp.zeros((total - rows, LANES), F32))
    return jnp.concatenate(segs, axis=0)


def _unpack(pack, shapes):
    out, off = [], 0
    for sh in shapes:
        r = _seg_rows(sh)
        n = 1
        for d in sh:
            n *= d
        out.append(pack[off:off + r].reshape(-1)[:n].reshape(sh))
        off += r
    return out


def _to_dest_blocks(full, axis, dtype):
    sh = list(full.shape)
    sh[axis:axis + 1] = [N_DEV // 2, 2, sh[axis] // N_DEV]
    return jnp.moveaxis(full.reshape(sh), (axis, axis + 1), (1, 0)).astype(dtype)


def _from_gathered(g, axis):
    m = jnp.moveaxis(g, 0, axis)
    sh = list(m.shape)
    sh[axis:axis + 2] = [sh[axis] * sh[axis + 1]]
    return m.reshape(sh)


D_IN = 3336
W_IN_SHARD = D_IN // N_DEV
W_IN_MOVES = ((0, 2048, 0), (2048, 2056, 3328), (2056, D_IN, 2048))


def _w_in_windows(k):
    lo, hi = k * W_IN_SHARD, (k + 1) * W_IN_SHARD
    out = []
    for a, b, mine in W_IN_MOVES:
        a2, b2 = max(a, lo), min(b, hi)
        if b2 > a2:
            out.append((a2 - lo, b2 - a2, mine + a2 - a))
    return out


def _assemble_w_in(gathered, name):
    depth = gathered.shape[1]
    rb = 256

    def body(g_ref, o_ref):
        o_ref[0, :, D_IN:] = jnp.zeros((rb, ZW - D_IN), o_ref.dtype)
        for k in range(N_DEV):
            for off, width, mine in _w_in_windows(k):
                o_ref[0, :, mine:mine + width] = g_ref[k, 0, :, off:off + width]

    return pl.pallas_call(
        body, grid=(depth, D // rb),
        in_specs=[pl.BlockSpec((N_DEV, 1, rb, W_IN_SHARD), lambda l, i: (0, l, i, 0))],
        out_specs=pl.BlockSpec((1, rb, ZW), lambda l, i: (l, i, 0)),
        out_shape=jax.ShapeDtypeStruct((depth, D, ZW), gathered.dtype),
        name=name, compiler_params=_cp("parallel", "parallel"))(gathered)


def _split_dw_in(dws, name):
    depth = len(dws)
    rb = 128

    def body(*refs):
        o_ref = refs[-1]
        for l in range(depth):
            pieces = refs[5 * l:5 * l + 5]
            for k in range(N_DEV):
                for off, width, mine in _w_in_windows(k):
                    for p_ref, start, n in zip(pieces, Z_OFFSETS, Z_PIECES):
                        a, b = max(mine, start), min(mine + width, start + n)
                        if b > a:
                            o_ref[k % 2, k // 2, l, :, off + a - mine:off + b - mine] = (
                                p_ref[:, a - start:b - start].astype(o_ref.dtype))

    row = lambda i: (i, 0)
    flat = [piece for layer in dws for piece in layer]
    return pl.pallas_call(
        body, grid=(D // rb,),
        in_specs=[pl.BlockSpec((rb, n), row) for _ in range(depth) for n in Z_PIECES],
        out_specs=pl.BlockSpec((2, N_DEV // 2, depth, rb, W_IN_SHARD), lambda i: (0, 0, 0, i, 0)),
        out_shape=jax.ShapeDtypeStruct((2, N_DEV // 2, depth, D, W_IN_SHARD), WIRE['w_in']),
        name=name, compiler_params=_cp("parallel"))(*flat)


EARLY = ('w_in', 'ssm_w_glu', 'dn_conv_w')
LATE = ('w_out', 'w_ple_gate', 'w_ple')


def _layer_weights(names, gathered, layer):
    full = {}
    for n, g in zip(names, gathered):
        full[n] = _assemble_w_in(g, f"assemble_w_in_l{layer}") if n == 'w_in' else _from_gathered(g, SHARDED[n])
        if n == 'ssm_w_glu':
            full[n] = full[n].astype(F32)
    return full


def _pair_reduce(names, layer_grads, tag):
    dest = [_split_dw_in([layer_grads[n]], f"split_dw_in_{tag}") if n == 'w_in'
            else _to_dest_blocks(layer_grads[n][None], SHARDED[n], WIRE[n]) for n in names]
    c = lax.axis_index("c")
    own = [lax.dynamic_index_in_dim(d, c, 0, keepdims=False) for d in dest]
    for_sibling = [lax.dynamic_index_in_dim(d, 1 - c, 0, keepdims=False) for d in dest]
    from_sibling = _pair_exchange(for_sibling, f"grads_pair_exchange_{tag}")
    return [_add_pair(a, b, f"grads_pair_sum_{n}_{tag}") for n, a, b in zip(names, own, from_sibling)]


def _local_step(x, p, wts, sharded, target, shards=None, reduce_early=False):
    sharded = [dict(d or {}) for d in sharded]
    bl, s, _ = x.shape
    t = bl * s
    depth = p.shape[0]
    tb, sg_tb = TB, SG_TB

    def by_example(a):
        return a.reshape(bl, s, a.shape[-1])

    def flat(a):
        return a.reshape(t, a.shape[-1])

    xs = [x.reshape(t, D)]
    p_all = p.reshape(depth, t, D_PLE)
    saved = []
    for i in range(depth):
        li = f"l{i}"
        ng = wts['norm_g'][i].reshape(1, D)
        lw = sharded[i]
        w_in = lw['w_in'][0]
        s5_par_in = (wts['ssm_a_re'][i], wts['ssm_a_im'][i], wts['ssm_b_re'][i], wts['ssm_b_im'][i],
                     wts['ssm_c_re'][i], wts['ssm_c_im'][i], wts['ssm_d'][i], wts['ssm_log_step'][i])
        tabs, tab_vjp = jax.vjp(_s5_tables, *s5_par_in)
        s5_par = (*tabs, lw['ssm_w_glu'][0], wts['ssm_b_glu'][i].reshape(1, D_SSM))
        s5_const = _s5_powers(wts['ssm_a_re'][i], wts['ssm_a_im'][i], wts['ssm_log_step'][i])
        conv8 = jnp.pad(lw['dn_conv_w'][0], ((0, 4), (0, 0)))
        dn_par = (jnp.repeat(wts['dn_a_log'][i], DH).reshape(1, D_DN), jnp.repeat(wts['dn_dt_bias'][i], DH).reshape(1, D_DN),
                  wts['dn_norm_g'][i].reshape(1, DH))
        sg_par = (wts['sg_ln_g'][i].reshape(1, D_SG), wts['sg_ln_b'][i].reshape(1, D_SG), wts['sg_w'][i],
                  jnp.pad(jnp.transpose(wts['sg_b'][i]), ((0, 0), (0, LANES - 4))))

        z_ssm, z_qkv, z_gdn, z_sg, z_ab, qkvn = _in_proj_fwd(xs[i], ng, w_in, conv8, s, tb, f"in_proj_fwd_{li}")
        wanted = [(i, n) for n in LATE if n not in lw]
        if i + 1 < depth:
            wanted += [(i + 1, n) for n in SHARDED_ORDER if n not in sharded[i + 1]]
        (y_ssm, carries, h_all, y_dn, states, tinvs), gathered = _mix_fwd(
            by_example(z_ssm), s5_par, s5_const, by_example(qkvn), by_example(z_ab), by_example(z_gdn), dn_par, bl, s,
            f"mix_fwd_{li}", gather=[shards[l][n] for l, n in wanted])
        for l in sorted({ll for ll, _ in wanted}):
            names = [n for ll, n in wanted if ll == l]
            sharded[l].update(_layer_weights(names, [g for (ll, _), g in zip(wanted, gathered) if ll == l], l))
        out_par = (lw['w_out'][0].astype(BF16), wts['ple_norm_g'][i].reshape(1, D), lw['w_ple_gate'][0].astype(BF16),
                   lw['w_ple'][0].astype(BF16))
        y_sg = _sg_fwd(z_sg, sg_par, sg_tb, f"sg_fwd_{li}")
        ys = (flat(y_ssm), flat(y_dn), y_sg)
        if i + 1 < depth:
            x_next, x1, gate = _out_fwd(xs[i], ys, p_all, i, *out_par, tb, f"out_fwd_{li}")
            xs.append(x_next)
        else:
            dx, x1, gate, dfg, loss_vec = _out_fwd(xs[i], ys, p_all, i, *out_par, tb, f"out_fwd_{li}",
                                                   head=(wts['final_norm_g'].reshape(1, D), target.reshape(t, D)))
        saved.append(dict(ng=ng, w_in=w_in, tab_vjp=tab_vjp, s5_par=s5_par, s5_const=s5_const, conv8=conv8, dn_par=dn_par,
                          sg_par=sg_par, out_par=out_par, z=(z_ssm, z_qkv, z_gdn, z_sg, z_ab), carries=carries,
                          h_all=h_all, qkvn=qkvn, x1=x1, gate=gate,
                          states=states, tinvs=tinvs, ys=ys))

    grads = {n: [None] * depth for n in WEIGHTS if n != 'final_norm_g'}
    grads['final_norm_g'] = dfg.reshape(D)
    pair_sums, by_chip = {}, {}
    for i in reversed(range(depth)):
        li = f"l{i}"
        sv = saved[i]
        z_ssm, z_qkv, z_gdn, z_sg, z_ab = sv['z']
        dx_res, dy_ssm, dy_dn, dy_sg, dwo, dpg, dwg, dwp = _out_bwd(sv['x1'], sv['gate'], sv['ys'], p_all, i, dx, *sv['out_par'], tb,
                                                                    f"out_bwd_{li}")
        dz_sg, dlng, dlnb, dsgw, dbsp = _sg_bwd(z_sg, dy_sg, sv['sg_par'], sg_tb, f"sg_bwd_{li}")
        if reduce_early:
            for n, gsum in zip(LATE, _pair_reduce(LATE, {'w_out': dwo, 'w_ple_gate': dwg, 'w_ple': dwp}, f"late_{li}")):
                pair_sums[(i, n)] = gsum
        travelling = [key for key in pair_sums if key not in by_chip]
        (dz_ssm, dbb, dcb, dlam, ddv, dwglu, dbglu), (dqkvn, dz_ab, dz_gdn, dal, ddt, dng), exchanged = _mix_bwd(
            by_example(z_ssm), sv['carries'], sv['h_all'], by_example(dy_ssm), sv['s5_par'], sv['s5_const'],
            by_example(sv['qkvn']), by_example(z_ab), by_example(z_gdn), sv['states'], sv['tinvs'], by_example(dy_dn),
            sv['dn_par'], bl, s, f"mix_bwd_{li}", exchange=[pair_sums[key] for key in travelling])
        by_chip.update(zip(travelling, exchanged))
        dx, dnorm, dz_qkv, dconv = _in_proj_bwd_dx(xs[i], sv['ng'], sv['w_in'], sv['conv8'], flat(dz_ssm), flat(dz_gdn), dz_sg,
                                                   flat(dz_ab), z_qkv, flat(dqkvn), dx_res, s, tb, f"in_proj_bwd_dx_{li}")
        dzs = (flat(dz_ssm), dz_qkv, flat(dz_gdn), dz_sg, flat(dz_ab))
        dws = _in_proj_bwd_dw(xs[i], sv['ng'], dzs, min(TB_DW, t), f"in_proj_bwd_dw_{li}")
        ds5 = sv['tab_vjp']((dbb, dcb, dlam, ddv))
        for n, gval in zip(('ssm_a_re', 'ssm_a_im', 'ssm_b_re', 'ssm_b_im', 'ssm_c_re', 'ssm_c_im', 'ssm_d', 'ssm_log_step'), ds5):
            grads[n][i] = gval
        grads['norm_g'][i] = dnorm.reshape(D)
        grads['w_in'][i] = dws
        grads['ssm_w_glu'][i] = dwglu
        grads['ssm_b_glu'][i] = dbglu.reshape(D_SSM)
        grads['dn_conv_w'][i] = dconv[:4]
        grads['dn_a_log'][i] = dal.reshape(H, DH).sum(axis=1)
        grads['dn_dt_bias'][i] = ddt.reshape(H, DH).sum(axis=1)
        grads['dn_norm_g'][i] = dng.reshape(DH)
        grads['sg_ln_g'][i] = dlng.reshape(D_SG)
        grads['sg_ln_b'][i] = dlnb.reshape(D_SG)
        grads['sg_w'][i] = dsgw
        grads['sg_b'][i] = jnp.transpose(dbsp[:, :4])
        grads['w_out'][i] = dwo
        grads['ple_norm_g'][i] = dpg.reshape(D)
        grads['w_ple_gate'][i] = dwg
        grads['w_ple'][i] = dwp
        if reduce_early:
            for n, gsum in zip(EARLY, _pair_reduce(EARLY, {n: grads[n][i] for n in EARLY}, f"early_{li}")):
                pair_sums[(i, n)] = gsum
    grads = {n: (g if n in ('final_norm_g', 'w_in') else jnp.stack(g)) for n, g in grads.items()}
    return loss_vec[0, 0], dx.reshape(bl, s, D), grads, {k: v for k, v in pair_sums.items() if k not in by_chip}, by_chip


def kernel(x, p, norm_g, w_in, ssm_a_re, ssm_a_im, ssm_b_re, ssm_b_im, ssm_c_re, ssm_c_im, ssm_d, ssm_log_step, ssm_w_glu, ssm_b_glu, dn_conv_w, dn_a_log, dn_dt_bias, dn_norm_g, sg_ln_g, sg_ln_b, sg_w, sg_b, w_out, ple_norm_g, w_ple_gate, w_ple, final_norm_g, loss_target, m_norm_g, m_w_in, m_ssm_a_re, m_ssm_a_im, m_ssm_b_re, m_ssm_b_im, m_ssm_c_re, m_ssm_c_im, m_ssm_d, m_ssm_log_step, m_ssm_w_glu, m_ssm_b_glu, m_dn_conv_w, m_dn_a_log, m_dn_dt_bias, m_dn_norm_g, m_sg_ln_g, m_sg_ln_b, m_sg_w, m_sg_b, m_w_out, m_ple_norm_g, m_w_ple_gate, m_w_ple, m_final_norm_g, v_norm_g, v_w_in, v_ssm_a_re, v_ssm_a_im, v_ssm_b_re, v_ssm_b_im, v_ssm_c_re, v_ssm_c_im, v_ssm_d, v_ssm_log_step, v_ssm_w_glu, v_ssm_b_glu, v_dn_conv_w, v_dn_a_log, v_dn_dt_bias, v_dn_norm_g, v_sg_ln_g, v_sg_ln_b, v_sg_w, v_sg_b, v_w_out, v_ple_norm_g, v_w_ple_gate, v_w_ple, v_final_norm_g):
    w_loc = dict(zip(WEIGHTS, (norm_g, w_in, ssm_a_re, ssm_a_im, ssm_b_re, ssm_b_im, ssm_c_re, ssm_c_im, ssm_d, ssm_log_step,
                               ssm_w_glu, ssm_b_glu, dn_conv_w, dn_a_log, dn_dt_bias, dn_norm_g, sg_ln_g, sg_ln_b, sg_w, sg_b,
                               w_out, ple_norm_g, w_ple_gate, w_ple, final_norm_g)))
    m_loc = dict(zip(WEIGHTS, (m_norm_g, m_w_in, m_ssm_a_re, m_ssm_a_im, m_ssm_b_re, m_ssm_b_im, m_ssm_c_re, m_ssm_c_im, m_ssm_d,
                               m_ssm_log_step, m_ssm_w_glu, m_ssm_b_glu, m_dn_conv_w, m_dn_a_log, m_dn_dt_bias, m_dn_norm_g,
                               m_sg_ln_g, m_sg_ln_b, m_sg_w, m_sg_b, m_w_out, m_ple_norm_g, m_w_ple_gate, m_w_ple, m_final_norm_g)))
    v_loc = dict(zip(WEIGHTS, (v_norm_g, v_w_in, v_ssm_a_re, v_ssm_a_im, v_ssm_b_re, v_ssm_b_im, v_ssm_c_re, v_ssm_c_im, v_ssm_d,
                               v_ssm_log_step, v_ssm_w_glu, v_ssm_b_glu, v_dn_conv_w, v_dn_a_log, v_dn_dt_bias, v_dn_norm_g,
                               v_sg_ln_g, v_sg_ln_b, v_sg_w, v_sg_b, v_w_out, v_ple_norm_g, v_w_ple_gate, v_w_ple, v_final_norm_g)))

    depth = p.shape[0]
    shards = [{n: w_loc[n][l:l + 1].astype(WIRE[n]) for n in SHARDED_ORDER} for l in range(depth)]
    first = _layer_weights(EARLY, _all_gather([shards[0][n] for n in EARLY], "gather_weights_l0"), 0)

    loss_part, grad_x, grads, left, by_chip = _local_step(
        x, p, w_loc, [first] + [None] * (depth - 1), loss_target, shards, reduce_early=True)

    rep_pack = _pack([grads[n] for n in REP_NARROW]).astype(WIRE['replicated'])
    fine_pack = _pack([grads['final_norm_g'], loss_part.reshape(1)])
    exchanged, (rep_recv, fine_recv) = _chip_exchange(list(left.values()), [rep_pack, fine_pack], "grads_chip_exchange_l0")
    by_chip.update(zip(left, exchanged))
    by_chip = [jnp.concatenate([by_chip[(l, n)] for l in range(depth)], axis=1) for n in SHARDED_ORDER]

    outs = {k: {} for k in 'gdmv'}
    for n, gk in zip(SHARDED_ORDER, by_chip):
        for k, o in zip('gdmv', _sum_adamw(gk, w_loc[n], m_loc[n], v_loc[n], f"adamw_{n}")):
            outs[k][n] = o
    rep_out = _sum_adamw(rep_recv, *[_pack([src[n] for n in REP_NARROW]) for src in (w_loc, m_loc, v_loc)], "adamw_replicated")
    for k, rep_p in zip('gdmv', rep_out):
        outs[k].update(zip(REP_NARROW, _unpack(rep_p, [w_loc[n].shape for n in REP_NARROW])))
    one = jnp.zeros((1,), F32)
    fine_out = _sum_adamw(fine_recv, *[_pack([src['final_norm_g'], one]) for src in (w_loc, m_loc, v_loc)], "adamw_final_norm")
    for k, fine_p in zip('gdmv', fine_out):
        outs[k].update(zip(['final_norm_g', 'loss'], _unpack(fine_p, [w_loc['final_norm_g'].shape, (1,)])))
    loss = outs['g']['loss'].reshape(())
    return (loss, grad_x, *[outs['g'][n] for n in WEIGHTS], *[outs['d'][n] for n in WEIGHTS],
            *[outs['m'][n] for n in WEIGHTS], *[outs['v'][n] for n in WEIGHTS])
```

```python
import jax
import jax.numpy as jnp
from jax import lax
from jax.experimental import pallas as pl
from jax.experimental.pallas import tpu as pltpu

F32 = jnp.float32
BF16 = jnp.bfloat16
EPS = 1e-6

D = 1024
D_PLE = 256
D_SSM = 256
D_DN = 512
D_SG = 256
G = 16
CG = 16
NS = 64
NRE = G * NS
H = 4
DH = 128
DN_C = 128
SG_C = 128
ZW = 3456
Z_PIECES = (512, 1536, 512, 768, 128)
N_DEV = 8
LANES = 128
PACK_ROWS = 256
VMEM_LIMIT = 56 * 1024 * 1024
ELEMENTWISE_STEP_BYTES = 4 * 1024 * 1024
TB = 512
SG_TB = 512
TB_DW = 1024

ADAM_LR = 0.001
ADAM_B1 = 0.9
ADAM_B2 = 0.999
ADAM_EPS = 1e-08
ADAM_WD = 0.01
ADAM_STEP = 10

MIX_HEAD_START = 3
S5_L = 128
S5_GROUP = 8
S5_SHIFTS = (1, 2, 4)

WEIGHTS = ['norm_g', 'w_in', 'ssm_a_re', 'ssm_a_im', 'ssm_b_re', 'ssm_b_im', 'ssm_c_re', 'ssm_c_im', 'ssm_d',
           'ssm_log_step', 'ssm_w_glu', 'ssm_b_glu', 'dn_conv_w', 'dn_a_log', 'dn_dt_bias', 'dn_norm_g', 'sg_ln_g',
           'sg_ln_b', 'sg_w', 'sg_b', 'w_out', 'ple_norm_g', 'w_ple_gate', 'w_ple', 'final_norm_g']
SHARDED = {'w_in': 2, 'ssm_w_glu': 1, 'dn_conv_w': 2, 'w_out': 1, 'w_ple_gate': 1, 'w_ple': 2}
SHARDED_ORDER = ['w_in', 'ssm_w_glu', 'dn_conv_w', 'w_out', 'w_ple_gate', 'w_ple']
WIRE = {'w_in': BF16, 'ssm_w_glu': BF16, 'dn_conv_w': F32, 'w_out': BF16, 'w_ple_gate': BF16, 'w_ple': BF16,
        'replicated': BF16}
REPLICATED_ORDER = [n for n in WEIGHTS if n not in SHARDED]
REP_NARROW = [n for n in REPLICATED_ORDER if n != 'final_norm_g']


def _cp(*sem):
    return pltpu.CompilerParams(dimension_semantics=sem, vmem_limit_bytes=VMEM_LIMIT)


def _dg(a, b, ca, cb, precision=None):
    return lax.dot_general(a, b, (((ca,), (cb,)), ((), ())), precision=precision, preferred_element_type=F32)


@jax.custom_vjp
def _mm(a, b):
    return _dg(a.astype(BF16), b.astype(BF16), 1, 0)


def _mm_fwd(a, b):
    return _mm(a, b), (a, b)


def _mm_bwd(res, g):
    a, b = res
    gb = g.astype(BF16)
    return _dg(gb, b.astype(BF16), 1, 1), _dg(a.astype(BF16), gb, 0, 0)


_mm.defvjp(_mm_fwd, _mm_bwd)


@jax.custom_vjp
def _mm_nt(a, b):
    return _dg(a.astype(BF16), b.astype(BF16), 1, 1)


def _mm_nt_fwd(a, b):
    return _mm_nt(a, b), (a, b)


def _mm_nt_bwd(res, g):
    a, b = res
    gb = g.astype(BF16)
    return _dg(gb, b.astype(BF16), 1, 0), _dg(gb, a.astype(BF16), 0, 0)


_mm_nt.defvjp(_mm_nt_fwd, _mm_nt_bwd)


@jax.custom_vjp
def _mm_tn(a, b):
    return _dg(a.astype(BF16), b.astype(BF16), 0, 0)


def _mm_tn_fwd(a, b):
    return _mm_tn(a, b), (a, b)


def _mm_tn_bwd(res, g):
    a, b = res
    gb = g.astype(BF16)
    return _dg(b.astype(BF16), gb, 1, 1), _dg(a.astype(BF16), gb, 1, 0)


_mm_tn.defvjp(_mm_tn_fwd, _mm_tn_bwd)


def _split(x, n):
    pieces = []
    for _ in range(n - 1):
        hi = x.astype(BF16)
        pieces.append(hi)
        x = x - hi.astype(F32)
    pieces.append(x.astype(BF16))
    return pieces


def _dg3(a, b, ca, cb):
    a_hi, a_lo = _split(a, 2)
    b_hi, b_lo = _split(b, 2)
    return _dg(a_hi, b_hi, ca, cb) + (_dg(a_hi, b_lo, ca, cb) + _dg(a_lo, b_hi, ca, cb))


@jax.custom_vjp
def _dot3(a, b):
    return _dg3(a, b, 1, 0)


def _dot3_fwd(a, b):
    return _dot3(a, b), (a, b)


def _dot3_bwd(res, g):
    a, b = res
    return _dg3(g, b, 1, 1), _dg3(a, g, 0, 0)


_dot3.defvjp(_dot3_fwd, _dot3_bwd)


def _dg_sel(x, e, cx, ce, x_first):
    eb = e.astype(BF16)
    out = None
    for piece in reversed(_split(x, 3)):
        term = _dg(piece, eb, cx, ce) if x_first else _dg(eb, piece, ce, cx)
        out = term if out is None else out + term
    return out


@jax.custom_vjp
def _sel_r(x, e):
    return _dg_sel(x, e, 1, 0, True)


def _sel_r_fwd(x, e):
    return _sel_r(x, e), e


def _sel_r_bwd(e, g):
    return _dg_sel(g, e, 1, 1, True), jnp.zeros_like(e)


_sel_r.defvjp(_sel_r_fwd, _sel_r_bwd)


@jax.custom_vjp
def _sel_l(e, x):
    return _dg_sel(x, e, 0, 1, False)


def _sel_l_fwd(e, x):
    return _sel_l(e, x), e


def _sel_l_bwd(e, g):
    return jnp.zeros_like(e), _dg_sel(g, e, 0, 0, False)


_sel_l.defvjp(_sel_l_fwd, _sel_l_bwd)


def _rms(x, g):
    return x * lax.rsqrt(jnp.mean(x * x, axis=-1, keepdims=True) + EPS) * g


def _silu(x):
    return x * jax.nn.sigmoid(x)


Z_OFFSETS = (0, 512, 2048, 2560, 3328)


def _dn_post(c):
    s = _silu(c)
    parts = []
    for j in range(12):
        xj = s[:, j * DH:(j + 1) * DH]
        if j < 8:
            xj = xj * lax.rsqrt(jnp.sum(xj * xj, axis=-1, keepdims=True) + EPS)
        if j < 4:
            xj = xj * (DH ** -0.5)
        parts.append(xj)
    return jnp.concatenate(parts, axis=1)


def _dn_conv(ext, cw_ref, rows):
    c = None
    for k in range(4):
        sh = ext if k == 3 else pltpu.roll(ext, 3 - k, 0)
        term = cw_ref[k:k + 1, :] * sh[ext.shape[0] - rows:, :]
        c = term if c is None else c + term
    return c


def _dn_prep_vjp(prev, cur, nxt, d_cur, d_nxt, cw_ref, tb):
    ext = jnp.concatenate([prev, cur, nxt], axis=0)
    shifted = [ext if k == 3 else pltpu.roll(ext, 3 - k, 0) for k in range(4)]
    c2 = None
    for k in range(4):
        term = cw_ref[k:k + 1, :] * shifted[k][8:, :]
        c2 = term if c2 is None else c2 + term
    _, vjp = jax.vjp(_dn_post, c2)
    (dc2,) = vjp(jnp.concatenate([d_cur, d_nxt], axis=0))
    dz, dcw = None, []
    for k in range(4):
        up = dc2 if k == 3 else pltpu.roll(dc2, tb + 8 - (3 - k), 0)
        term = cw_ref[k:k + 1, :] * up[:tb, :]
        dz = term if dz is None else dz + term
        dcw.append(jnp.sum(dc2[:tb, :] * shifted[k][8:8 + tb, :], axis=0, keepdims=True))
    return dz, dcw


def _in_proj_fwd(x, g, w, conv_w8, s, tb, name):
    t = x.shape[0]
    n_s = s // tb
    w3 = 3 * D_DN
    q0, q1 = Z_OFFSETS[1], Z_OFFSETS[2]

    def body(x_ref, g_ref, w_ref, cw_ref, zs_ref, zq_ref, zg_ref, zsg_ref, zab_ref, qkvn_ref, halo):
        h = _rms(x_ref[...], g_ref[...]).astype(BF16)
        zq = jnp.dot(h, w_ref[:, q0:q1], preferred_element_type=F32)
        zq_ref[...] = zq
        prev = jnp.where(pl.program_id(0) % n_s == 0, 0.0, halo[...])
        qkvn_ref[...] = _dn_post(_dn_conv(jnp.concatenate([prev, zq], axis=0), cw_ref, tb))
        halo[...] = zq[tb - 8:, :]
        zs_ref[...] = jnp.dot(h, w_ref[:, :q0], preferred_element_type=F32)
        rest = jnp.dot(h, w_ref[:, q1:], preferred_element_type=F32)
        zg_ref[...] = rest[:, :Z_PIECES[2]]
        zsg_ref[...] = rest[:, Z_PIECES[2]:Z_PIECES[2] + Z_PIECES[3]]
        zab_ref[...] = rest[:, Z_PIECES[2] + Z_PIECES[3]:]

    row = lambda i: (i, 0)
    full = lambda i: (0, 0)
    widths = Z_PIECES + (w3,)
    return pl.pallas_call(
        body, grid=(t // tb,),
        in_specs=[pl.BlockSpec((tb, D), row), pl.BlockSpec((1, D), full), pl.BlockSpec((D, ZW), full), pl.BlockSpec((8, w3), full)],
        out_specs=[pl.BlockSpec((tb, n), row) for n in widths],
        out_shape=[jax.ShapeDtypeStruct((t, n), F32) for n in widths],
        scratch_shapes=[pltpu.VMEM((8, w3), F32)],
        name=name, compiler_params=_cp("arbitrary"))(x, g, w, conv_w8)


def _in_proj_bwd_dx(x, g, w, conv_w8, dz_ssm, dz_gdn, dz_sg, dz_ab, zq, dqkvn, dx_res, s, tb, name):
    t = x.shape[0]
    n_s = s // tb
    hb = tb // 8
    w3 = 3 * D_DN
    q0, q1 = Z_OFFSETS[1], Z_OFFSETS[2]

    def body(x_ref, g_ref, w_ref, cw_ref, ds_ref, dgd_ref, dsg_ref, dab_ref, cur_ref, prev_ref, next_ref, dq_ref, dqn_ref,
             dxr_ref, dx_ref, dg_ref, dzq_ref, dcw_ref):
        i = pl.program_id(0)

        @pl.when(i == 0)
        def _():
            dg_ref[...] = jnp.zeros_like(dg_ref)
            dcw_ref[...] = jnp.zeros_like(dcw_ref)

        rest = jnp.concatenate([dgd_ref[...], dsg_ref[...], dab_ref[...]], axis=1)
        dh = _dg(ds_ref[...], w_ref[:, :q0], 1, 1) + _dg(rest, w_ref[:, q1:], 1, 1)
        first, last = i % n_s == 0, i % n_s == n_s - 1
        dzq, dcw = _dn_prep_vjp(jnp.where(first, 0.0, prev_ref[...]), cur_ref[...], jnp.where(last, 0.0, next_ref[...]),
                                dq_ref[...], jnp.where(last, 0.0, dqn_ref[...]), cw_ref, tb)
        for k in range(4):
            dcw_ref[k:k + 1, :] += dcw[k]
        dzq = dzq.astype(BF16)
        dzq_ref[...] = dzq
        dh = dh + _dg(dzq, w_ref[:, q0:q1], 1, 1)
        _, vjp = jax.vjp(_rms, x_ref[...], g_ref[...])
        dx, dg = vjp(dh)
        dx_ref[...] = dx + dxr_ref[...]
        dg_ref[...] += dg

    n_blk8 = t // 8
    row = lambda i: (i, 0)
    prv = lambda i: (jnp.maximum(i * hb - 1, 0), 0)
    nxt = lambda i: (jnp.minimum((i + 1) * hb, n_blk8 - 1), 0)
    full = lambda i: (0, 0)
    return pl.pallas_call(
        body, grid=(t // tb,),
        in_specs=[pl.BlockSpec((tb, D), row), pl.BlockSpec((1, D), full), pl.BlockSpec((D, ZW), full), pl.BlockSpec((8, w3), full)]
        + [pl.BlockSpec((tb, n), row) for n in (Z_PIECES[0], Z_PIECES[2], Z_PIECES[3], Z_PIECES[4])]
        + [pl.BlockSpec((tb, w3), row), pl.BlockSpec((8, w3), prv), pl.BlockSpec((8, w3), nxt),
           pl.BlockSpec((tb, w3), row), pl.BlockSpec((8, w3), nxt), pl.BlockSpec((tb, D), row)],
        out_specs=[pl.BlockSpec((tb, D), row), pl.BlockSpec((1, D), full), pl.BlockSpec((tb, w3), row), pl.BlockSpec((8, w3), full)],
        out_shape=[jax.ShapeDtypeStruct((t, D), F32), jax.ShapeDtypeStruct((1, D), F32), jax.ShapeDtypeStruct((t, w3), BF16),
                   jax.ShapeDtypeStruct((8, w3), F32)],
        name=name, compiler_params=_cp("arbitrary"))(x, g, w, conv_w8, dz_ssm, dz_gdn, dz_sg, dz_ab, zq, zq, zq, dqkvn, dqkvn, dx_res)


def _in_proj_bwd_dw(x, g, dzs, tb, name):
    t = x.shape[0]

    def body(x_ref, g_ref, d0, d1, d2, d3, d4, *dw_refs):
        @pl.when(pl.program_id(0) == 0)
        def _():
            for r in dw_refs:
                r[...] = jnp.zeros_like(r)

        h = _rms(x_ref[...], g_ref[...]).astype(BF16)
        for d_ref, dw_ref in zip((d0, d1, d2, d3, d4), dw_refs):
            dw_ref[...] += _dg(h, d_ref[...].astype(BF16), 0, 0)

    row = lambda i: (i, 0)
    full = lambda i: (0, 0)
    return pl.pallas_call(
        body, grid=(t // tb,),
        in_specs=[pl.BlockSpec((tb, D), row), pl.BlockSpec((1, D), full)] + [pl.BlockSpec((tb, n), row) for n in Z_PIECES],
        out_specs=[pl.BlockSpec((D, n), full) for n in Z_PIECES],
        out_shape=[jax.ShapeDtypeStruct((D, n), F32) for n in Z_PIECES],
        name=name, compiler_params=_cp("arbitrary"))(x, g, *dzs)


def _lam_pow(a_re, a_im, log_step, k):
    step = jnp.exp(log_step)[:, None]
    mag = jnp.exp(k * a_re * step)
    ang = k * a_im * step
    return mag * jnp.cos(ang), mag * jnp.sin(ang)


def _s5_powers(a_re, a_im, log_step):
    def table(ks):
        re, im = _lam_pow(a_re, a_im, log_step, jnp.asarray(ks, F32)[:, None, None])
        return jnp.concatenate([re.reshape(len(ks), NRE), im.reshape(len(ks), NRE)], axis=-1)

    ld = table(S5_SHIFTS).reshape(len(S5_SHIFTS), 1, 2 * NRE)
    return ld, table(range(1, S5_GROUP + 1)), table(range(S5_GROUP, 0, -1))


def _s5_tables(a_re, a_im, b_re, b_im, c_re, c_im, d_skip, log_step):
    lam_re, lam_im = _lam_pow(a_re, a_im, log_step, 1.0)
    den = a_re * a_re + a_im * a_im
    nr, ni = lam_re - 1.0, lam_im
    f_re = (nr * a_re + ni * a_im) / den
    f_im = (ni * a_re - nr * a_im) / den
    bbar_re = f_re[..., None] * b_re - f_im[..., None] * b_im
    bbar_im = f_re[..., None] * b_im + f_im[..., None] * b_re
    eye = jnp.eye(G, dtype=F32)

    def blk_b(bb):
        return (jnp.transpose(bb, (0, 2, 1))[:, :, None, :] * eye[:, None, :, None]).reshape(D_SSM, NRE)

    def blk_c(cc):
        return (jnp.transpose(cc, (0, 2, 1))[:, :, None, :] * eye[:, None, :, None]).reshape(NRE, D_SSM)

    b_blk = jnp.concatenate([blk_b(bbar_re), blk_b(bbar_im)], axis=1)
    c_blk = jnp.concatenate([blk_c(c_re), -blk_c(c_im)], axis=0)
    lam = jnp.concatenate([lam_re.reshape(1, NRE), lam_im.reshape(1, NRE)], axis=-1)
    return b_blk, c_blk, lam, d_skip.reshape(1, D_SSM)


def _group_shift(x, d, up=False):
    r = lax.broadcasted_iota(jnp.int32, x.shape, 0) & (S5_GROUP - 1)
    if up:
        return jnp.where(r < S5_GROUP - d, pltpu.roll(x, x.shape[0] - d, 0), 0.0)
    return jnp.where(r >= d, pltpu.roll(x, d, 0), 0.0)


def _s5_scan_steps(hr, hi, cr, ci, lds, lp):
    for ld, d in zip(lds, S5_SHIFTS):
        lr, li = ld[:, :NRE], ld[:, NRE:]
        sr, si = _group_shift(hr, d), _group_shift(hi, d)
        hr, hi = hr + lr * sr - li * si, hi + lr * si + li * sr
        yield
    pr, pi = lp[:, :NRE], lp[:, NRE:]
    rows_r, rows_i = [], []
    for r in range(hr.shape[0] // S5_GROUP):
        br, bi = hr[r * S5_GROUP:(r + 1) * S5_GROUP], hi[r * S5_GROUP:(r + 1) * S5_GROUP]
        br, bi = br + pr * cr - pi * ci, bi + pr * ci + pi * cr
        cr, ci = br[S5_GROUP - 1:S5_GROUP], bi[S5_GROUP - 1:S5_GROUP]
        rows_r.append(br)
        rows_i.append(bi)
        if r % 2:
            yield
    return jnp.concatenate(rows_r, axis=0), jnp.concatenate(rows_i, axis=0)


@jax.custom_vjp
def _known_scan(xr, xi, cr, ci, lam, lds, lp_rev, hr, hi):
    return hr, hi


def _known_scan_fwd(xr, xi, cr, ci, lam, lds, lp_rev, hr, hi):
    return (hr, hi), (cr, ci, lam, lds, lp_rev, hr, hi)


def _known_scan_bwd(res, cts):
    cr, ci, lam, lds, lp_rev, hr, hi = res
    ar, ai = cts
    for ld, d in zip(lds, S5_SHIFTS):
        lr, li = ld[:, :NRE], ld[:, NRE:]
        sr, si = _group_shift(ar, d, up=True), _group_shift(ai, d, up=True)
        ar, ai = ar + lr * sr + li * si, ai + lr * si - li * sr
    qr, qi = lp_rev[:, :NRE], lp_rev[:, NRE:]
    nr, ni = jnp.zeros_like(cr), jnp.zeros_like(ci)
    rows_r, rows_i = [], []
    for r in reversed(range(hr.shape[0] // S5_GROUP)):
        br, bi = ar[r * S5_GROUP:(r + 1) * S5_GROUP], ai[r * S5_GROUP:(r + 1) * S5_GROUP]
        br, bi = br + qr * nr + qi * ni, bi + qr * ni - qi * nr
        nr, ni = br[0:1], bi[0:1]
        rows_r.insert(0, br)
        rows_i.insert(0, bi)
    ar, ai = jnp.concatenate(rows_r, axis=0), jnp.concatenate(rows_i, axis=0)
    lr, li = lam[:, :NRE], lam[:, NRE:]
    dcr, dci = lr * nr + li * ni, lr * ni - li * nr
    first = lax.broadcasted_iota(jnp.int32, hr.shape, 0) == 0
    pr = jnp.where(first, cr, pltpu.roll(hr, 1, 0))
    pi = jnp.where(first, ci, pltpu.roll(hi, 1, 0))
    dlam = jnp.concatenate([jnp.sum(ar * pr + ai * pi, axis=0, keepdims=True),
                            jnp.sum(ai * pr - ar * pi, axis=0, keepdims=True)], axis=1)
    return (ar, ai, dcr, dci, dlam, [jnp.zeros_like(ld) for ld in lds], jnp.zeros_like(lp_rev),
            jnp.zeros_like(hr), jnp.zeros_like(hi))


_known_scan.defvjp(_known_scan_fwd, _known_scan_bwd)


def _interleave(short, long, head_start=0):
    gens = list(short) + list(long)
    results = [None] * len(gens)

    def advance(live):
        still = []
        for idx, gen in live:
            try:
                next(gen)
                still.append((idx, gen))
            except StopIteration as done:
                results[idx] = done.value
        return still

    live_short = advance(list(enumerate(gens))[:len(short)])
    live_long = list(enumerate(gens))[len(short):]
    for _ in range(head_start):
        live_long = advance(live_long)
    live = live_short + live_long
    while live:
        live = advance(live)
    return results[:len(short)], results[len(short):]


def _s5_chunk_gen(u, gate, cr, ci, b_blk, c_blk, lam, dv, wglu, bglu, lds, lp, lp_rev, known_h=None):
    bu = _mm(u, b_blk)
    xr, xi = bu[:, :NRE], bu[:, NRE:]
    yield
    if known_h is None:
        hr, hi = yield from _s5_scan_steps(xr, xi, cr, ci, lds, lp)
    else:
        hr, hi = _known_scan(xr, xi, cr, ci, lam, lds, lp_rev, *known_h)
    y = _mm(jnp.concatenate([hr, hi], axis=1), c_blk) + dv * u
    yield
    y = jax.nn.gelu(y)
    y = y * jax.nn.sigmoid(_mm(y, wglu) + bglu)
    return y * _silu(gate), hr, hi


S5_PAR_SHAPES = [(D_SSM, 2 * NRE), (2 * NRE, D_SSM), (1, 2 * NRE), (1, D_SSM), (D_SSM, D_SSM), (1, D_SSM)]
S5_CONST_SHAPES = [(len(S5_SHIFTS), 1, 2 * NRE), (S5_GROUP, 2 * NRE), (S5_GROUP, 2 * NRE)]
DN_PAR_SHAPES = [(1, D_DN), (1, D_DN), (1, DH)]


def _mix_specs(bl, n_c, rev):
    def chunk(i):
        return n_c - 1 - i if rev else i

    def tok(n):
        return pl.BlockSpec((bl, DN_C, n), lambda i: (0, chunk(i), 0))

    def per_chunk(shape):
        return pl.BlockSpec((bl, 1, *shape), lambda i: (0, chunk(i)) + (0,) * len(shape))

    def whole(shape):
        return pl.BlockSpec(shape, lambda i: (0,) * len(shape))

    return tok, per_chunk, whole


def _unit_lower_inverse_steps(ms):
    c_len = ms[0].shape[0]
    eye = lax.broadcasted_iota(jnp.int32, (c_len, c_len), 0) == lax.broadcasted_iota(jnp.int32, (c_len, c_len), 1)
    ident = jnp.where(eye, 1.0, 0.0)
    ps = ms
    tinvs = [ident - m for m in ms]
    for _ in range(c_len.bit_length() - 2):
        ps = [_dg3(p, p, 1, 0) for p in ps]
        yield
        tinvs = [t + _dg3(t, p, 1, 0) for t, p in zip(tinvs, ps)]
        yield
    return tinvs


@jax.custom_vjp
def _known_inverses(ms, tinvs):
    return tinvs


def _known_inverses_fwd(ms, tinvs):
    return tinvs, tinvs


def _known_inverses_bwd(tinvs, gs):
    return [-_dg3(_dg3(t, g, 0, 0), t, 1, 1) for t, g in zip(tinvs, gs)], [jnp.zeros_like(t) for t in tinvs]


_known_inverses.defvjp(_known_inverses_fwd, _known_inverses_bwd)


def _dn_chunk_gen(qkv, zab, zg, states, alog_e, dt_e, ng, known_tinvs=None):
    c_len = DN_C
    r = lax.broadcasted_iota(jnp.int32, (c_len, c_len), 0)
    c = lax.broadcasted_iota(jnp.int32, (c_len, c_len), 1)
    causal, strict = r >= c, r > c
    tril = jnp.where(causal, 1.0, 0.0)
    rr = lax.broadcasted_iota(jnp.int32, (LANES, D_DN), 0)
    cc = lax.broadcasted_iota(jnp.int32, (LANES, D_DN), 1)
    e_a = jnp.where((cc >= rr * DH) & (cc < rr * DH + DH) & (rr < H), 1.0, 0.0)
    e_b = jnp.where((cc >= (rr - H) * DH) & (cc < (rr - H) * DH + DH) & (rr >= H) & (rr < 2 * H), 1.0, 0.0)
    a_e = _sel_r(zab, e_a)
    b_e = _sel_r(zab, e_b)
    beta = jax.nn.sigmoid(b_e)
    g = -jnp.exp(alog_e) * jax.nn.softplus(a_e + dt_e)
    yield
    gc = _sel_l(tril, g)
    glast = jnp.sum(g, axis=0, keepdims=True)
    eg = jnp.exp(gc)
    ekd = jnp.exp(glast - gc)
    dl = jnp.exp(glast)
    yield
    heads = range(H)
    sls = [slice(h * DH, (h + 1) * DH) for h in heads]
    qs = [qkv[:, h * DH:(h + 1) * DH] for h in heads]
    ks = [qkv[:, D_DN + h * DH:D_DN + (h + 1) * DH] for h in heads]
    vs = [qkv[:, 2 * D_DN + h * DH:2 * D_DN + (h + 1) * DH] for h in heads]
    ccols = [gc[:, sl] for sl in sls]
    decs = [jnp.where(causal, jnp.exp(jnp.where(causal, cl - jnp.transpose(cl), 0.0)), 0.0) for cl in ccols]
    kbs = [k * beta[:, sl] for k, sl in zip(ks, sls)]
    ms = [jnp.where(strict, _mm_nt(kb, k) * dec, 0.0) for kb, k, dec in zip(kbs, ks, decs)]
    yield
    if known_tinvs is None:
        tinvs = yield from _unit_lower_inverse_steps(ms)
    else:
        tinvs = _known_inverses(ms, list(known_tinvs))
    sols = [_dot3(t, jnp.concatenate([v * beta[:, sl], kb * eg[:, sl]], axis=1))
            for t, v, kb, sl in zip(tinvs, vs, kbs, sls)]
    yield
    atts = [_mm_nt(q, k) * dec for q, k, dec in zip(qs, ks, decs)]
    vnews = [sol[:, :DH] - _mm(sol[:, DH:], st) for sol, st in zip(sols, states)]
    yield
    os_ = [_mm(q * eg[:, sl], st) + _mm(att, vn) for q, sl, st, att, vn in zip(qs, sls, states, atts, vnews)]
    yield
    new_states = [st * dl[:, sl] + _mm_tn(k * ekd[:, sl], vn) for st, sl, k, vn in zip(states, sls, ks, vnews)]
    yield
    ys = [_rms(o, ng) * _silu(zg[:, sl]) for o, sl in zip(os_, sls)]
    return jnp.concatenate(ys, axis=1), new_states, tinvs


def _mix_fwd(zs, s5_par, s5_const, qkv, zab, zg, dn_par, bl, s, name, gather=()):
    assert S5_L == DN_C
    n_c = s // DN_C
    nd = len(S5_SHIFTS)
    m = len(gather)
    tok, per_chunk, whole = _mix_specs(bl, n_c, False)

    def body(*refs):
        (z_ref, b_ref, c_ref, lam_ref, dv_ref, wg_ref, bg_ref, ld_ref, lp_ref, lpr_ref,
         q_ref, ab_ref, zg_ref, al_ref, dt_ref, ng_ref) = refs[:16]
        ys_ref, car_ref, h_ref, yd_ref, st_ref, ti_ref = refs[16 + m:22 + m]
        cs, ssc = refs[22 + 2 * m:24 + 2 * m]
        gathering = (_gather_protocol(_mesh_place(), refs[16:16 + m], refs[22 + m:22 + 2 * m], *refs[24 + 2 * m:])
                     if m else None)

        @pl.when(pl.program_id(0) == 0)
        def _():
            cs[...] = jnp.zeros_like(cs)
            ssc[...] = jnp.zeros_like(ssc)
            if m:
                next(gathering)

        lds = [ld_ref[k] for k in range(nd)]
        s5_gens, dn_gens = [], []
        for e in range(bl):
            c = cs[e]
            car_ref[e, 0] = c
            sts = [ssc[e, h] for h in range(H)]
            for h in range(H):
                st_ref[e, 0, h] = sts[h]
            z = z_ref[e]
            s5_gens.append(_s5_chunk_gen(z[:, :D_SSM], z[:, D_SSM:], c[:, :NRE], c[:, NRE:], b_ref[...], c_ref[...], lam_ref[...],
                                         dv_ref[...], wg_ref[...], bg_ref[...], lds, lp_ref[...], lpr_ref[...]))
            dn_gens.append(_dn_chunk_gen(q_ref[e], ab_ref[e], zg_ref[e], sts, al_ref[...], dt_ref[...], ng_ref[...]))
        s5_outs, dn_outs = _interleave(s5_gens, dn_gens, head_start=MIX_HEAD_START)
        for e in range(bl):
            y_s, hr, hi = s5_outs[e]
            y_d, new_sts, tinvs = dn_outs[e]
            ys_ref[e] = y_s
            h_ref[e, :, :NRE] = hr
            h_ref[e, :, NRE:] = hi
            cs[e, :, :NRE] = hr[S5_L - 1:S5_L]
            cs[e, :, NRE:] = hi[S5_L - 1:S5_L]
            yd_ref[e] = y_d
            for h in range(H):
                ssc[e, h] = new_sts[h]
                ti_ref[e, 0, h] = tinvs[h]

        if m:
            @pl.when(pl.program_id(0) == n_c - 1)
            def _():
                for _ in gathering:
                    pass

    head_mats = jax.ShapeDtypeStruct((bl, n_c, H, DH, DH), F32)
    outs = pl.pallas_call(
        body, grid=(n_c,),
        in_specs=[tok(2 * D_SSM)] + [whole(sh) for sh in S5_PAR_SHAPES + S5_CONST_SHAPES]
        + [tok(3 * D_DN), tok(LANES), tok(D_DN)] + [whole(sh) for sh in DN_PAR_SHAPES] + _hbm_specs(m),
        out_specs=[tok(D_SSM), per_chunk((1, 2 * NRE)), tok(2 * NRE), tok(D_DN), per_chunk((H, DH, DH)), per_chunk((H, DH, DH))]
        + _hbm_specs(m),
        out_shape=[jax.ShapeDtypeStruct((bl, s, D_SSM), F32), jax.ShapeDtypeStruct((bl, n_c, 1, 2 * NRE), F32),
                   jax.ShapeDtypeStruct((bl, s, 2 * NRE), F32), jax.ShapeDtypeStruct((bl, s, D_DN), F32), head_mats, head_mats]
        + [jax.ShapeDtypeStruct((N_DEV, *b.shape), b.dtype) for b in gather],
        scratch_shapes=[pltpu.VMEM((bl, 1, 2 * NRE), F32), pltpu.VMEM((bl, H, DH, DH), F32)] + (_gather_sems(m) if m else []),
        name=name, compiler_params=_cp("arbitrary"))(zs, *s5_par, *s5_const, qkv, zab, zg, *dn_par, *gather)
    return outs[:6], outs[6:]


def _mix_bwd(zs, carries, h_all, dy_s, s5_par, s5_const, qkv, zab, zg, states, tinvs, dy_d, dn_par, bl, s, name, exchange=()):
    n_c = s // DN_C
    nd = len(S5_SHIFTS)
    k_ex = len(exchange)
    tok, per_chunk, whole = _mix_specs(bl, n_c, True)

    def both(examples, s5_tabs, s5_consts, dn_tabs):
        s5_gens = [_s5_chunk_gen(u, gate, cr, ci, *s5_tabs, *s5_consts, known_h=(hr, hi))
                   for u, gate, cr, ci, hr, hi, _, _, _, _, _ in examples]
        dn_gens = [_dn_chunk_gen(q, ab, zgate, sts, *dn_tabs, known_tinvs=known)
                   for _, _, _, _, _, _, q, ab, zgate, sts, known in examples]
        s5_outs, dn_outs = _interleave(s5_gens, dn_gens, head_start=MIX_HEAD_START)
        return [(y_s, hr[S5_L - 1:S5_L], hi[S5_L - 1:S5_L], y_d, new_sts)
                for (y_s, hr, hi), (y_d, new_sts, _) in zip(s5_outs, dn_outs)]

    def body(*refs):
        (z_ref, car_ref, h_ref, dys_ref, b_ref, c_ref, lam_ref, dv_ref, wg_ref, bg_ref, ld_ref, lp_ref, lpr_ref,
         q_ref, ab_ref, zg_ref, st_ref, ti_ref, dyd_ref, al_ref, dt_ref, ng_ref) = refs[:22]
        (dz_ref, db_ref, dc_ref, dlam_ref, ddv_ref, dwg_ref, dbg_ref,
         dq_ref, dab_ref, dzg_ref, dal_ref, ddt_ref, dng_ref) = refs[22 + k_ex:35 + k_ex]
        dcs, dsc = refs[35 + 2 * k_ex:37 + 2 * k_ex]
        accs = (db_ref, dc_ref, dlam_ref, ddv_ref, dwg_ref, dbg_ref, dal_ref, ddt_ref, dng_ref)
        exchanging = (_chip_protocol(_mesh_place(), refs[22:22 + k_ex], refs[35 + k_ex:35 + 2 * k_ex], *refs[37 + 2 * k_ex:])
                      if k_ex else None)

        @pl.when(pl.program_id(0) == 0)
        def _():
            for r in accs + (dcs, dsc):
                r[...] = jnp.zeros_like(r)
            if k_ex:
                next(exchanging)

        examples = []
        for e in range(bl):
            z = z_ref[e]
            c = car_ref[e, 0]
            examples.append((z[:, :D_SSM], z[:, D_SSM:], c[:, :NRE], c[:, NRE:], h_ref[e, :, :NRE], h_ref[e, :, NRE:],
                             q_ref[e], ab_ref[e], zg_ref[e], [st_ref[e, 0, h] for h in range(H)],
                             [ti_ref[e, 0, h] for h in range(H)]))
        _, vjp = jax.vjp(both, examples,
                         (b_ref[...], c_ref[...], lam_ref[...], dv_ref[...], wg_ref[...], bg_ref[...]),
                         ([ld_ref[k] for k in range(nd)], lp_ref[...], lpr_ref[...]),
                         (al_ref[...], dt_ref[...], ng_ref[...]))
        cts = []
        for e in range(bl):
            dc = dcs[e]
            cts.append((dys_ref[e], dc[:, :NRE], dc[:, NRE:], dyd_ref[e], [dsc[e, h] for h in range(H)]))
        d_examples, d_s5, _, d_dn = vjp(cts)
        for e in range(bl):
            du, dgate, dcr, dci, _, _, dq, dab, dzg, dsts, _ = d_examples[e]
            dz_ref[e] = jnp.concatenate([du, dgate], axis=1).astype(BF16)
            dcs[e, :, :NRE] = dcr
            dcs[e, :, NRE:] = dci
            dq_ref[e] = dq
            dab_ref[e] = dab.astype(BF16)
            dzg_ref[e] = dzg.astype(BF16)
            for h in range(H):
                dsc[e, h] = dsts[h]
        for r, ct in zip(accs, (*d_s5, *d_dn)):
            r[...] += ct

        if k_ex:
            @pl.when(pl.program_id(0) == n_c - 1)
            def _():
                for _ in exchanging:
                    pass

    head_mats = per_chunk((H, DH, DH))
    outs = pl.pallas_call(
        body, grid=(n_c,),
        in_specs=[tok(2 * D_SSM), per_chunk((1, 2 * NRE)), tok(2 * NRE), tok(D_SSM)]
        + [whole(sh) for sh in S5_PAR_SHAPES + S5_CONST_SHAPES]
        + [tok(3 * D_DN), tok(LANES), tok(D_DN), head_mats, head_mats, tok(D_DN)] + [whole(sh) for sh in DN_PAR_SHAPES]
        + _hbm_specs(k_ex),
        out_specs=[tok(2 * D_SSM)] + [whole(sh) for sh in S5_PAR_SHAPES]
        + [tok(3 * D_DN), tok(LANES), tok(D_DN)] + [whole(sh) for sh in DN_PAR_SHAPES] + _hbm_specs(k_ex),
        out_shape=[jax.ShapeDtypeStruct((bl, s, 2 * D_SSM), BF16)] + [jax.ShapeDtypeStruct(sh, F32) for sh in S5_PAR_SHAPES]
        + [jax.ShapeDtypeStruct((bl, s, 3 * D_DN), F32), jax.ShapeDtypeStruct((bl, s, LANES), BF16),
           jax.ShapeDtypeStruct((bl, s, D_DN), BF16)]
        + [jax.ShapeDtypeStruct(sh, F32) for sh in DN_PAR_SHAPES]
        + [jax.ShapeDtypeStruct(q.shape, q.dtype) for q in exchange],
        scratch_shapes=[pltpu.VMEM((bl, 1, 2 * NRE), F32), pltpu.VMEM((bl, H, DH, DH), F32)] + (_chip_sems(k_ex) if k_ex else []),
        name=name, compiler_params=_cp("arbitrary"))(
            zs, carries, h_all, dy_s, *s5_par, *s5_const, qkv, zab, zg, states, tinvs, dy_d, *dn_par, *exchange)
    return outs[:7], outs[7:13], outs[13:]


def _sg_fn(n_chunk):
    def f(z, lng, lnb, w, bsp_t):
        u = jax.nn.gelu(z[:, :D_SG])
        v = jax.nn.gelu(z[:, D_SG:2 * D_SG])
        gate = z[:, 2 * D_SG:]
        xc = v - jnp.mean(v, axis=-1, keepdims=True)
        vn = xc * lax.rsqrt(jnp.mean(xc * xc, axis=-1, keepdims=True) + EPS) * lng + lnb
        r = lax.broadcasted_iota(jnp.int32, (SG_C, SG_C), 0)
        c = lax.broadcasted_iota(jnp.int32, (SG_C, SG_C), 1)
        causal = r >= c
        first_half = c < SG_C // 2
        rr = lax.broadcasted_iota(jnp.int32, (LANES, D_SG), 0)
        cc = lax.broadcasted_iota(jnp.int32, (LANES, D_SG), 1)
        expand = jnp.where((cc >= rr * 64) & (cc < rr * 64 + 64) & (rr < 4), 1.0, 0.0)
        bias = _sel_r(bsp_t, expand)
        wm = [jnp.where(causal, w[h], 0.0) for h in range(4)]
        rows = []
        for ci in range(n_chunk):
            vc = vn[ci * SG_C:(ci + 1) * SG_C]
            pairs = []
            for pr in range(2):
                vp = vc[:, pr * LANES:(pr + 1) * LANES]
                pairs.append(jnp.where(first_half, _mm(wm[2 * pr], vp), _mm(wm[2 * pr + 1], vp)))
            rows.append(jnp.concatenate(pairs, axis=1) + bias)
        sp = jnp.concatenate(rows, axis=0) if n_chunk > 1 else rows[0]
        return u * sp * _silu(gate)

    return f


def _sg_specs():
    full = lambda i: (0, 0)
    full3 = lambda i: (0, 0, 0)
    par = [pl.BlockSpec((1, D_SG), full), pl.BlockSpec((1, D_SG), full), pl.BlockSpec((4, SG_C, SG_C), full3),
           pl.BlockSpec((SG_C, LANES), full)]
    par_shapes = [(1, D_SG), (1, D_SG), (4, SG_C, SG_C), (SG_C, LANES)]
    return par, par_shapes


def _sg_fwd(zsg, params, tb, name):
    t = zsg.shape[0]
    f = _sg_fn(tb // SG_C)
    par, _ = _sg_specs()

    def body(z_ref, g_ref, b_ref, w_ref, bs_ref, y_ref):
        y_ref[...] = f(z_ref[...], g_ref[...], b_ref[...], w_ref[...], bs_ref[...])

    row = lambda i: (i, 0)
    return pl.pallas_call(
        body, grid=(t // tb,), in_specs=[pl.BlockSpec((tb, 3 * D_SG), row)] + par,
        out_specs=pl.BlockSpec((tb, D_SG), row), out_shape=jax.ShapeDtypeStruct((t, D_SG), F32),
        name=name, compiler_params=_cp("parallel"))(zsg, *params)


def _sg_bwd(zsg, dy, params, tb, name):
    t = zsg.shape[0]
    f = _sg_fn(tb // SG_C)
    par, par_shapes = _sg_specs()

    def body(z_ref, dy_ref, g_ref, b_ref, w_ref, bs_ref, dz_ref, dg_ref, db_ref, dw_ref, dbs_ref):
        accs = (dg_ref, db_ref, dw_ref, dbs_ref)

        @pl.when(pl.program_id(0) == 0)
        def _():
            for r in accs:
                r[...] = jnp.zeros_like(r)

        _, vjp = jax.vjp(f, z_ref[...], g_ref[...], b_ref[...], w_ref[...], bs_ref[...])
        cts = vjp(dy_ref[...])
        dz_ref[...] = cts[0].astype(BF16)
        for r, ct in zip(accs, cts[1:]):
            r[...] += ct

    row = lambda i: (i, 0)
    return pl.pallas_call(
        body, grid=(t // tb,), in_specs=[pl.BlockSpec((tb, 3 * D_SG), row), pl.BlockSpec((tb, D_SG), row)] + par,
        out_specs=[pl.BlockSpec((tb, 3 * D_SG), row)] + par,
        out_shape=[jax.ShapeDtypeStruct((t, 3 * D_SG), BF16)] + [jax.ShapeDtypeStruct(sh, F32) for sh in par_shapes],
        name=name, compiler_params=_cp("arbitrary"))(zsg, dy, *params)


def _out_fwd(x, ys, p, layer, w_out, pg, w_gate, w_ple, tb, name, head=None):
    t = x.shape[0]

    def body(*refs):
        x_ref, y0, y1, y2, p_ref, wo_ref, pg_ref, wg_ref, wp_ref = refs[:9]
        o_ref, x1_ref, gate_ref = refs[-5:-2] if head else refs[-3:]
        y = jnp.concatenate([y0[...], y1[...], y2[...]], axis=1).astype(BF16)
        x1 = x_ref[...] + jnp.dot(y, wo_ref[...], preferred_element_type=F32)
        hn = _rms(x1, pg_ref[...]).astype(BF16)
        gate = jax.nn.sigmoid(jnp.dot(hn, wg_ref[...], preferred_element_type=F32))
        pp = jnp.dot(p_ref[0].astype(BF16), wp_ref[...], preferred_element_type=F32)
        x2 = x1 + gate * pp
        x1_ref[...] = x1
        gate_ref[...] = gate
        if not head:
            o_ref[...] = x2
            return
        fg_ref, t_ref = refs[9:11]
        dfg_ref, loss_ref = refs[-2:]

        @pl.when(pl.program_id(0) == 0)
        def _():
            dfg_ref[...] = jnp.zeros_like(dfg_ref)
            loss_ref[...] = jnp.zeros_like(loss_ref)

        yf, vjp = jax.vjp(_rms, x2, fg_ref[...])
        err = yf - t_ref[...]
        loss_ref[...] += jnp.zeros_like(loss_ref) + 0.5 * jnp.sum(err * err) / D
        dx2, dfg = vjp(err / D)
        o_ref[...] = dx2
        dfg_ref[...] += dfg

    row = lambda i: (i, 0)
    full = lambda i: (0, 0)
    acts = [pl.BlockSpec((tb, D), row)] * 3
    return pl.pallas_call(
        body, grid=(t // tb,),
        in_specs=[pl.BlockSpec((tb, D), row), pl.BlockSpec((tb, D_SSM), row), pl.BlockSpec((tb, D_DN), row),
                  pl.BlockSpec((tb, D_SG), row), pl.BlockSpec((1, tb, D_PLE), lambda i: (layer, i, 0)), pl.BlockSpec((D, D), full),
                  pl.BlockSpec((1, D), full), pl.BlockSpec((D, D), full), pl.BlockSpec((D_PLE, D), full)]
        + ([pl.BlockSpec((1, D), full), pl.BlockSpec((tb, D), row)] if head else []),
        out_specs=acts + ([pl.BlockSpec((1, D), full), pl.BlockSpec((1, LANES), full)] if head else []),
        out_shape=[jax.ShapeDtypeStruct((t, D), F32)] * 3
        + ([jax.ShapeDtypeStruct((1, D), F32), jax.ShapeDtypeStruct((1, LANES), F32)] if head else []),
        name=name, compiler_params=_cp("arbitrary" if head else "parallel"))(x, *ys, p, w_out, pg, w_gate, w_ple, *(head or ()))


def _out_bwd(x1, gate, ys, p, layer, dx2, w_out, pg, w_gate, w_ple, tb, name):
    t = x1.shape[0]

    def body(x1_ref, gate_ref, y0, y1, y2, p_ref, d_ref, wo_ref, pg_ref, wg_ref, wp_ref,
             dx_ref, dy0, dy1, dy2, dwo_ref, dpg_ref, dwg_ref, dwp_ref):
        accs = (dwo_ref, dpg_ref, dwg_ref, dwp_ref)

        @pl.when(pl.program_id(0) == 0)
        def _():
            for r in accs:
                r[...] = jnp.zeros_like(r)

        y = jnp.concatenate([y0[...], y1[...], y2[...]], axis=1).astype(BF16)
        hn, rms_vjp = jax.vjp(_rms, x1_ref[...], pg_ref[...])
        hb = hn.astype(BF16)
        gate = gate_ref[...]
        pb = p_ref[0].astype(BF16)
        pp = jnp.dot(pb, wp_ref[...], preferred_element_type=F32)
        d2 = d_ref[...]
        dpp = (d2 * gate).astype(BF16)
        dlog = (d2 * pp * gate * (1.0 - gate)).astype(BF16)
        dwp_ref[...] += _dg(pb, dpp, 0, 0)
        dwg_ref[...] += _dg(hb, dlog, 0, 0)
        dx1_n, dpg = rms_vjp(_dg(dlog, wg_ref[...], 1, 1))
        dpg_ref[...] += dpg
        dx1 = d2 + dx1_n
        dx_ref[...] = dx1
        db = dx1.astype(BF16)
        dwo_ref[...] += _dg(y, db, 0, 0)
        dy = _dg(db, wo_ref[...], 1, 1)
        dy0[...] = dy[:, :D_SSM]
        dy1[...] = dy[:, D_SSM:D_SSM + D_DN]
        dy2[...] = dy[:, D_SSM + D_DN:]

    row = lambda i: (i, 0)
    full = lambda i: (0, 0)
    acts = [pl.BlockSpec((tb, D), row), pl.BlockSpec((tb, D_SSM), row), pl.BlockSpec((tb, D_DN), row), pl.BlockSpec((tb, D_SG), row)]
    wts = [pl.BlockSpec((D, D), full), pl.BlockSpec((1, D), full), pl.BlockSpec((D, D), full), pl.BlockSpec((D_PLE, D), full)]
    return pl.pallas_call(
        body, grid=(t // tb,),
        in_specs=[pl.BlockSpec((tb, D), row)] + acts
        + [pl.BlockSpec((1, tb, D_PLE), lambda i: (layer, i, 0)), pl.BlockSpec((tb, D), row)] + wts,
        out_specs=acts + wts,
        out_shape=[jax.ShapeDtypeStruct((t, n), F32) for n in (D, D_SSM, D_DN, D_SG)]
        + [jax.ShapeDtypeStruct(sh, F32) for sh in ((D, D), (1, D), (D, D), (D_PLE, D))],
        name=name, compiler_params=_cp("arbitrary"))(x1, gate, *ys, p, dx2, w_out, pg, w_gate, w_ple)


def _hbm_specs(n):
    return [pl.BlockSpec(memory_space=pl.ANY)] * n


def _gather_protocol(place, ins, outs, send_sems, recv_sems, local_sems):
    n = len(ins)
    x, y, c, other_x, other_y, other_c = place
    me, sibling = (x, y, c), (x, y, other_c)
    chips = [(other_x, y), (x, other_y), (other_x, other_y)]

    def slot(a, px, py, pc):
        return outs[a].at[4 * px + 2 * py + pc]

    def copy(a, k, blk, to, src=None):
        return pltpu.make_async_remote_copy(
            src_ref=slot(a, *blk) if src is None else src, dst_ref=slot(a, *blk),
            send_sem=send_sems.at[7 * a + k], recv_sem=recv_sems.at[7 * a + k],
            device_id=to, device_id_type=pl.DeviceIdType.MESH)

    def own_copies():
        mines = [pltpu.make_async_copy(ins[a], slot(a, *me), local_sems.at[a]) for a in range(n)]
        first = []
        for a in range(n):
            first.append(copy(a, 0, me, sibling, src=ins[a]))
            first += [copy(a, 1 + j, me, (*chip, c), src=ins[a]) for j, chip in enumerate(chips)]
        return mines, first

    mines, first = own_copies()
    for cp in mines + first:
        cp.start()
    yield
    mines, first = own_copies()
    passed = []
    for j, chip in enumerate(chips):
        for a in range(n):
            copy(a, 1 + j, (*chip, c), me).wait_recv()
            onward = copy(a, 4 + j, (*chip, c), sibling)
            onward.start()
            passed.append(onward)
    for a in range(n):
        copy(a, 0, sibling, me).wait_recv()
    for j, chip in enumerate(chips):
        for a in range(n):
            copy(a, 4 + j, (*chip, other_c), me).wait_recv()
    for cp in first + passed:
        cp.wait_send()
    for cp in mines:
        cp.wait()


def _mesh_place():
    x, y, c = lax.axis_index("x"), lax.axis_index("y"), lax.axis_index("c")
    return x, y, c, 1 - x, 1 - y, 1 - c


def _gather_sems(n):
    return [pltpu.SemaphoreType.DMA((7 * n,)), pltpu.SemaphoreType.DMA((7 * n,)), pltpu.SemaphoreType.DMA((n,))]


def _all_gather(blocks, name):
    n = len(blocks)

    def body(*refs):
        for _ in _gather_protocol(_mesh_place(), refs[:n], refs[n:2 * n], *refs[2 * n:]):
            pass

    return pl.pallas_call(
        body, out_shape=[jax.ShapeDtypeStruct((N_DEV, *b.shape), b.dtype) for b in blocks],
        in_specs=_hbm_specs(n), out_specs=_hbm_specs(n), scratch_shapes=_gather_sems(n), name=name)(*blocks)


def _pair_exchange(gs, name):
    n = len(gs)

    def body(*refs):
        ins, recvs = refs[:n], refs[n:2 * n]
        send_sems, recv_sems = refs[2 * n:]
        x, y, c = lax.axis_index("x"), lax.axis_index("y"), lax.axis_index("c")
        remote = [pltpu.make_async_remote_copy(
            src_ref=ins[a], dst_ref=recvs[a], send_sem=send_sems.at[a], recv_sem=recv_sems.at[a],
            device_id=(x, y, 1 - c), device_id_type=pl.DeviceIdType.MESH) for a in range(n)]
        for cp in remote:
            cp.start()
        for cp in remote:
            cp.wait_send()
            cp.wait_recv()

    return pl.pallas_call(
        body, out_shape=[jax.ShapeDtypeStruct(g.shape, g.dtype) for g in gs], in_specs=_hbm_specs(n), out_specs=_hbm_specs(n),
        scratch_shapes=[pltpu.SemaphoreType.DMA((n,)), pltpu.SemaphoreType.DMA((n,))],
        name=name)(*gs)


def _chip_protocol(place, ins, outs, send_sems, recv_sems, local_sems):
    n = len(ins)
    x, y, c, other_x, other_y, _ = place

    def copies():
        my_chip = 2 * x + y
        local = [pltpu.make_async_copy(ins[a].at[my_chip], outs[a].at[my_chip], local_sems.at[a]) for a in range(n)]
        remote = []
        for j in range(1, 4):
            px = other_x if j & 2 else x
            py = other_y if j & 1 else y
            for a in range(n):
                remote.append(pltpu.make_async_remote_copy(
                    src_ref=ins[a].at[2 * px + py], dst_ref=outs[a].at[my_chip],
                    send_sem=send_sems.at[3 * a + j - 1], recv_sem=recv_sems.at[3 * a + j - 1],
                    device_id=(px, py, c), device_id_type=pl.DeviceIdType.MESH))
        return local, remote

    local, remote = copies()
    for cp in local + remote:
        cp.start()
    yield
    local, remote = copies()
    for cp in remote:
        cp.wait_send()
        cp.wait_recv()
    for cp in local:
        cp.wait()


def _chip_sems(n):
    return [pltpu.SemaphoreType.DMA((3 * n,)), pltpu.SemaphoreType.DMA((3 * n,)), pltpu.SemaphoreType.DMA((n,))]


def _chip_exchange(ps, gather, name):
    n, m = len(ps), len(gather)

    def body(*refs):
        ins, g_ins, outs, g_outs = refs[:n], refs[n:n + m], refs[n + m:2 * n + m], refs[2 * n + m:2 * (n + m)]
        place = _mesh_place()
        exchanging = _chip_protocol(place, ins, outs, *refs[2 * (n + m):2 * (n + m) + 3])
        gathering = _gather_protocol(place, g_ins, g_outs, *refs[2 * (n + m) + 3:])
        next(exchanging)
        for _ in gathering:
            pass
        for _ in exchanging:
            pass

    outs = pl.pallas_call(
        body, out_shape=[jax.ShapeDtypeStruct(q.shape, q.dtype) for q in ps]
        + [jax.ShapeDtypeStruct((N_DEV, *b.shape), b.dtype) for b in gather],
        in_specs=_hbm_specs(n + m), out_specs=_hbm_specs(n + m),
        scratch_shapes=_chip_sems(n) + _gather_sems(m), name=name)(*ps, *gather)
    return outs[:n], outs[n:]


def _row_block(rows, bytes_per_row):
    best = None
    for rb in range(16, rows + 1, 16):
        if rows % rb == 0 and rb * bytes_per_row <= ELEMENTWISE_STEP_BYTES:
            best = rb
    return rows if best is None else best


def _add_pair(own, recv, name):
    shape = own.shape
    last = shape[-1]
    rows = own.size // last
    rb = _row_block(rows, 3 * 4 * (-(-last // LANES) * LANES))

    def body(a_ref, b_ref, o_ref):
        o_ref[...] = (a_ref[...].astype(F32) + b_ref[...].astype(F32)).astype(o_ref.dtype)

    row = lambda i: (i, 0)
    out = pl.pallas_call(
        body, grid=(rows // rb,), in_specs=[pl.BlockSpec((rb, last), row)] * 2, out_specs=pl.BlockSpec((rb, last), row),
        out_shape=jax.ShapeDtypeStruct((rows, last), own.dtype), name=name,
        compiler_params=_cp("parallel"))(own.reshape(rows, last), recv.reshape(rows, last))
    return out.reshape(shape)


def _sum_adamw(gk, w, m, v, name):
    shape = w.shape
    n_part = gk.shape[0]
    last = shape[-1]
    rows = w.size // last
    rb = _row_block(rows, (n_part + 7) * 4 * (-(-last // LANES) * LANES))

    def body(g_ref, w_ref, m_ref, v_ref, go_ref, d_ref, mo_ref, vo_ref):
        g = g_ref[0].astype(F32)
        for k in range(1, n_part):
            g = g + g_ref[k].astype(F32)
        mn = ADAM_B1 * m_ref[...] + (1.0 - ADAM_B1) * g
        vn = ADAM_B2 * v_ref[...] + (1.0 - ADAM_B2) * jnp.square(g)
        m_hat = mn / (1.0 - ADAM_B1 ** ADAM_STEP)
        v_hat = vn / (1.0 - ADAM_B2 ** ADAM_STEP)
        go_ref[...] = g
        d_ref[...] = -ADAM_LR * (m_hat / (jnp.sqrt(v_hat) + ADAM_EPS) + ADAM_WD * w_ref[...])
        mo_ref[...] = mn
        vo_ref[...] = vn

    row = lambda i: (i, 0)
    outs = pl.pallas_call(
        body, grid=(rows // rb,),
        in_specs=[pl.BlockSpec((n_part, rb, last), lambda i: (0, i, 0))] + [pl.BlockSpec((rb, last), row)] * 3,
        out_specs=[pl.BlockSpec((rb, last), row)] * 4,
        out_shape=[jax.ShapeDtypeStruct((rows, last), F32)] * 4,
        name=name, compiler_params=_cp("parallel"))(gk.reshape(n_part, rows, last), *[a.reshape(rows, last) for a in (w, m, v)])
    return [o.reshape(shape) for o in outs]


def _seg_rows(shape):
    n = 1
    for d in shape:
        n *= d
    return -(-n // (8 * LANES)) * 8


def _pack(arrs):
    segs = []
    for a in arrs:
        r = _seg_rows(a.shape)
        segs.append(jnp.pad(a.reshape(-1).astype(F32), (0, r * LANES - a.size)).reshape(r, LANES))
    rows = sum(s.shape[0] for s in segs)
    total = -(-rows // PACK_ROWS) * PACK_ROWS
    if total > rows:
        segs.append(jnp.zeros((total - rows, LANES), F32))
    return jnp.concatenate(segs, axis=0)


def _unpack(pack, shapes):
    out, off = [], 0
    for sh in shapes:
        r = _seg_rows(sh)
        n = 1
        for d in sh:
            n *= d
        out.append(pack[off:off + r].reshape(-1)[:n].reshape(sh))
        off += r
    return out


def _to_dest_blocks(full, axis, dtype):
    sh = list(full.shape)
    sh[axis:axis + 1] = [N_DEV // 2, 2, sh[axis] // N_DEV]
    return jnp.moveaxis(full.reshape(sh), (axis, axis + 1), (1, 0)).astype(dtype)


def _from_gathered(g, axis):
    m = jnp.moveaxis(g, 0, axis)
    sh = list(m.shape)
    sh[axis:axis + 2] = [sh[axis] * sh[axis + 1]]
    return m.reshape(sh)


D_IN = 3336
W_IN_SHARD = D_IN // N_DEV
W_IN_MOVES = ((0, 2048, 0), (2048, 2056, 3328), (2056, D_IN, 2048))


def _w_in_windows(k):
    lo, hi = k * W_IN_SHARD, (k + 1) * W_IN_SHARD
    out = []
    for a, b, mine in W_IN_MOVES:
        a2, b2 = max(a, lo), min(b, hi)
        if b2 > a2:
            out.append((a2 - lo, b2 - a2, mine + a2 - a))
    return out


def _assemble_w_in(gathered, name):
    depth = gathered.shape[1]
    rb = 256

    def body(g_ref, o_ref):
        o_ref[0, :, D_IN:] = jnp.zeros((rb, ZW - D_IN), o_ref.dtype)
        for k in range(N_DEV):
            for off, width, mine in _w_in_windows(k):
                o_ref[0, :, mine:mine + width] = g_ref[k, 0, :, off:off + width]

    return pl.pallas_call(
        body, grid=(depth, D // rb),
        in_specs=[pl.BlockSpec((N_DEV, 1, rb, W_IN_SHARD), lambda l, i: (0, l, i, 0))],
        out_specs=pl.BlockSpec((1, rb, ZW), lambda l, i: (l, i, 0)),
        out_shape=jax.ShapeDtypeStruct((depth, D, ZW), gathered.dtype),
        name=name, compiler_params=_cp("parallel", "parallel"))(gathered)


def _split_dw_in(dws, name):
    depth = len(dws)
    rb = 128

    def body(*refs):
        o_ref = refs[-1]
        for l in range(depth):
            pieces = refs[5 * l:5 * l + 5]
            for k in range(N_DEV):
                for off, width, mine in _w_in_windows(k):
                    for p_ref, start, n in zip(pieces, Z_OFFSETS, Z_PIECES):
                        a, b = max(mine, start), min(mine + width, start + n)
                        if b > a:
                            o_ref[k % 2, k // 2, l, :, off + a - mine:off + b - mine] = (
                                p_ref[:, a - start:b - start].astype(o_ref.dtype))

    row = lambda i: (i, 0)
    flat = [piece for layer in dws for piece in layer]
    return pl.pallas_call(
        body, grid=(D // rb,),
        in_specs=[pl.BlockSpec((rb, n), row) for _ in range(depth) for n in Z_PIECES],
        out_specs=pl.BlockSpec((2, N_DEV // 2, depth, rb, W_IN_SHARD), lambda i: (0, 0, 0, i, 0)),
        out_shape=jax.ShapeDtypeStruct((2, N_DEV // 2, depth, D, W_IN_SHARD), WIRE['w_in']),
        name=name, compiler_params=_cp("parallel"))(*flat)


EARLY = ('w_in', 'ssm_w_glu', 'dn_conv_w')
LATE = ('w_out', 'w_ple_gate', 'w_ple')


def _layer_weights(names, gathered, layer):
    full = {}
    for n, g in zip(names, gathered):
        full[n] = _assemble_w_in(g, f"assemble_w_in_l{layer}") if n == 'w_in' else _from_gathered(g, SHARDED[n])
        if n == 'ssm_w_glu':
            full[n] = full[n].astype(F32)
    return full


def _pair_reduce(names, layer_grads, tag):
    dest = [_split_dw_in([layer_grads[n]], f"split_dw_in_{tag}") if n == 'w_in'
            else _to_dest_blocks(layer_grads[n][None], SHARDED[n], WIRE[n]) for n in names]
    c = lax.axis_index("c")
    own = [lax.dynamic_index_in_dim(d, c, 0, keepdims=False) for d in dest]
    for_sibling = [lax.dynamic_index_in_dim(d, 1 - c, 0, keepdims=False) for d in dest]
    from_sibling = _pair_exchange(for_sibling, f"grads_pair_exchange_{tag}")
    return [_add_pair(a, b, f"grads_pair_sum_{n}_{tag}") for n, a, b in zip(names, own, from_sibling)]


def _local_step(x, p, wts, sharded, target, shards=None, reduce_early=False):
    sharded = [dict(d or {}) for d in sharded]
    bl, s, _ = x.shape
    t = bl * s
    depth = p.shape[0]
    tb, sg_tb = TB, SG_TB

    def by_example(a):
        return a.reshape(bl, s, a.shape[-1])

    def flat(a):
        return a.reshape(t, a.shape[-1])

    xs = [x.reshape(t, D)]
    p_all = p.reshape(depth, t, D_PLE)
    saved = []
    for i in range(depth):
        li = f"l{i}"
        ng = wts['norm_g'][i].reshape(1, D)
        lw = sharded[i]
        w_in = lw['w_in'][0]
        s5_par_in = (wts['ssm_a_re'][i], wts['ssm_a_im'][i], wts['ssm_b_re'][i], wts['ssm_b_im'][i],
                     wts['ssm_c_re'][i], wts['ssm_c_im'][i], wts['ssm_d'][i], wts['ssm_log_step'][i])
        tabs, tab_vjp = jax.vjp(_s5_tables, *s5_par_in)
        s5_par = (*tabs, lw['ssm_w_glu'][0], wts['ssm_b_glu'][i].reshape(1, D_SSM))
        s5_const = _s5_powers(wts['ssm_a_re'][i], wts['ssm_a_im'][i], wts['ssm_log_step'][i])
        conv8 = jnp.pad(lw['dn_conv_w'][0], ((0, 4), (0, 0)))
        dn_par = (jnp.repeat(wts['dn_a_log'][i], DH).reshape(1, D_DN), jnp.repeat(wts['dn_dt_bias'][i], DH).reshape(1, D_DN),
                  wts['dn_norm_g'][i].reshape(1, DH))
        sg_par = (wts['sg_ln_g'][i].reshape(1, D_SG), wts['sg_ln_b'][i].reshape(1, D_SG), wts['sg_w'][i],
                  jnp.pad(jnp.transpose(wts['sg_b'][i]), ((0, 0), (0, LANES - 4))))

        z_ssm, z_qkv, z_gdn, z_sg, z_ab, qkvn = _in_proj_fwd(xs[i], ng, w_in, conv8, s, tb, f"in_proj_fwd_{li}")
        wanted = [(i, n) for n in LATE if n not in lw]
        if i + 1 < depth:
            wanted += [(i + 1, n) for n in SHARDED_ORDER if n not in sharded[i + 1]]
        (y_ssm, carries, h_all, y_dn, states, tinvs), gathered = _mix_fwd(
            by_example(z_ssm), s5_par, s5_const, by_example(qkvn), by_example(z_ab), by_example(z_gdn), dn_par, bl, s,
            f"mix_fwd_{li}", gather=[shards[l][n] for l, n in wanted])
        for l in sorted({ll for ll, _ in wanted}):
            names = [n for ll, n in wanted if ll == l]
            sharded[l].update(_layer_weights(names, [g for (ll, _), g in zip(wanted, gathered) if ll == l], l))
        out_par = (lw['w_out'][0].astype(BF16), wts['ple_norm_g'][i].reshape(1, D), lw['w_ple_gate'][0].astype(BF16),
                   lw['w_ple'][0].astype(BF16))
        y_sg = _sg_fwd(z_sg, sg_par, sg_tb, f"sg_fwd_{li}")
        ys = (flat(y_ssm), flat(y_dn), y_sg)
        if i + 1 < depth:
            x_next, x1, gate = _out_fwd(xs[i], ys, p_all, i, *out_par, tb, f"out_fwd_{li}")
            xs.append(x_next)
        else:
            dx, x1, gate, dfg, loss_vec = _out_fwd(xs[i], ys, p_all, i, *out_par, tb, f"out_fwd_{li}",
                                                   head=(wts['final_norm_g'].reshape(1, D), target.reshape(t, D)))
        saved.append(dict(ng=ng, w_in=w_in, tab_vjp=tab_vjp, s5_par=s5_par, s5_const=s5_const, conv8=conv8, dn_par=dn_par,
                          sg_par=sg_par, out_par=out_par, z=(z_ssm, z_qkv, z_gdn, z_sg, z_ab), carries=carries,
                          h_all=h_all, qkvn=qkvn, x1=x1, gate=gate,
                          states=states, tinvs=tinvs, ys=ys))

    grads = {n: [None] * depth for n in WEIGHTS if n != 'final_norm_g'}
    grads['final_norm_g'] = dfg.reshape(D)
    pair_sums, by_chip = {}, {}
    for i in reversed(range(depth)):
        li = f"l{i}"
        sv = saved[i]
        z_ssm, z_qkv, z_gdn, z_sg, z_ab = sv['z']
        dx_res, dy_ssm, dy_dn, dy_sg, dwo, dpg, dwg, dwp = _out_bwd(sv['x1'], sv['gate'], sv['ys'], p_all, i, dx, *sv['out_par'], tb,
                                                                    f"out_bwd_{li}")
        dz_sg, dlng, dlnb, dsgw, dbsp = _sg_bwd(z_sg, dy_sg, sv['sg_par'], sg_tb, f"sg_bwd_{li}")
        if reduce_early:
            for n, gsum in zip(LATE, _pair_reduce(LATE, {'w_out': dwo, 'w_ple_gate': dwg, 'w_ple': dwp}, f"late_{li}")):
                pair_sums[(i, n)] = gsum
        travelling = [key for key in pair_sums if key not in by_chip]
        (dz_ssm, dbb, dcb, dlam, ddv, dwglu, dbglu), (dqkvn, dz_ab, dz_gdn, dal, ddt, dng), exchanged = _mix_bwd(
            by_example(z_ssm), sv['carries'], sv['h_all'], by_example(dy_ssm), sv['s5_par'], sv['s5_const'],
            by_example(sv['qkvn']), by_example(z_ab), by_example(z_gdn), sv['states'], sv['tinvs'], by_example(dy_dn),
            sv['dn_par'], bl, s, f"mix_bwd_{li}", exchange=[pair_sums[key] for key in travelling])
        by_chip.update(zip(travelling, exchanged))
        dx, dnorm, dz_qkv, dconv = _in_proj_bwd_dx(xs[i], sv['ng'], sv['w_in'], sv['conv8'], flat(dz_ssm), flat(dz_gdn), dz_sg,
                                                   flat(dz_ab), z_qkv, flat(dqkvn), dx_res, s, tb, f"in_proj_bwd_dx_{li}")
        dzs = (flat(dz_ssm), dz_qkv, flat(dz_gdn), dz_sg, flat(dz_ab))
        dws = _in_proj_bwd_dw(xs[i], sv['ng'], dzs, min(TB_DW, t), f"in_proj_bwd_dw_{li}")
        ds5 = sv['tab_vjp']((dbb, dcb, dlam, ddv))
        for n, gval in zip(('ssm_a_re', 'ssm_a_im', 'ssm_b_re', 'ssm_b_im', 'ssm_c_re', 'ssm_c_im', 'ssm_d', 'ssm_log_step'), ds5):
            grads[n][i] = gval
        grads['norm_g'][i] = dnorm.reshape(D)
        grads['w_in'][i] = dws
        grads['ssm_w_glu'][i] = dwglu
        grads['ssm_b_glu'][i] = dbglu.reshape(D_SSM)
        grads['dn_conv_w'][i] = dconv[:4]
        grads['dn_a_log'][i] = dal.reshape(H, DH).sum(axis=1)
        grads['dn_dt_bias'][i] = ddt.reshape(H, DH).sum(axis=1)
        grads['dn_norm_g'][i] = dng.reshape(DH)
        grads['sg_ln_g'][i] = dlng.reshape(D_SG)
        grads['sg_ln_b'][i] = dlnb.reshape(D_SG)
        grads['sg_w'][i] = dsgw
        grads['sg_b'][i] = jnp.transpose(dbsp[:, :4])
        grads['w_out'][i] = dwo
        grads['ple_norm_g'][i] = dpg.reshape(D)
        grads['w_ple_gate'][i] = dwg
        grads['w_ple'][i] = dwp
        if reduce_early:
            for n, gsum in zip(EARLY, _pair_reduce(EARLY, {n: grads[n][i] for n in EARLY}, f"early_{li}")):
                pair_sums[(i, n)] = gsum
    grads = {n: (g if n in ('final_norm_g', 'w_in') else jnp.stack(g)) for n, g in grads.items()}
    return loss_vec[0, 0], dx.reshape(bl, s, D), grads, {k: v for k, v in pair_sums.items() if k not in by_chip}, by_chip


def kernel(x, p, norm_g, w_in, ssm_a_re, ssm_a_im, ssm_b_re, ssm_b_im, ssm_c_re, ssm_c_im, ssm_d, ssm_log_step, ssm_w_glu, ssm_b_glu, dn_conv_w, dn_a_log, dn_dt_bias, dn_norm_g, sg_ln_g, sg_ln_b, sg_w, sg_b, w_out, ple_norm_g, w_ple_gate, w_ple, final_norm_g, loss_target, m_norm_g, m_w_in, m_ssm_a_re, m_ssm_a_im, m_ssm_b_re, m_ssm_b_im, m_ssm_c_re, m_ssm_c_im, m_ssm_d, m_ssm_log_step, m_ssm_w_glu, m_ssm_b_glu, m_dn_conv_w, m_dn_a_log, m_dn_dt_bias, m_dn_norm_g, m_sg_ln_g, m_sg_ln_b, m_sg_w, m_sg_b, m_w_out, m_ple_norm_g, m_w_ple_gate, m_w_ple, m_final_norm_g, v_norm_g, v_w_in, v_ssm_a_re, v_ssm_a_im, v_ssm_b_re, v_ssm_b_im, v_ssm_c_re, v_ssm_c_im, v_ssm_d, v_ssm_log_step, v_ssm_w_glu, v_ssm_b_glu, v_dn_conv_w, v_dn_a_log, v_dn_dt_bias, v_dn_norm_g, v_sg_ln_g, v_sg_ln_b, v_sg_w, v_sg_b, v_w_out, v_ple_norm_g, v_w_ple_gate, v_w_ple, v_final_norm_g):
    w_loc = dict(zip(WEIGHTS, (norm_g, w_in, ssm_a_re, ssm_a_im, ssm_b_re, ssm_b_im, ssm_c_re, ssm_c_im, ssm_d, ssm_log_step,
                               ssm_w_glu, ssm_b_glu, dn_conv_w, dn_a_log, dn_dt_bias, dn_norm_g, sg_ln_g, sg_ln_b, sg_w, sg_b,
                               w_out, ple_norm_g, w_ple_gate, w_ple, final_norm_g)))
    m_loc = dict(zip(WEIGHTS, (m_norm_g, m_w_in, m_ssm_a_re, m_ssm_a_im, m_ssm_b_re, m_ssm_b_im, m_ssm_c_re, m_ssm_c_im, m_ssm_d,
                               m_ssm_log_step, m_ssm_w_glu, m_ssm_b_glu, m_dn_conv_w, m_dn_a_log, m_dn_dt_bias, m_dn_norm_g,
                               m_sg_ln_g, m_sg_ln_b, m_sg_w, m_sg_b, m_w_out, m_ple_norm_g, m_w_ple_gate, m_w_ple, m_final_norm_g)))
    v_loc = dict(zip(WEIGHTS, (v_norm_g, v_w_in, v_ssm_a_re, v_ssm_a_im, v_ssm_b_re, v_ssm_b_im, v_ssm_c_re, v_ssm_c_im, v_ssm_d,
                               v_ssm_log_step, v_ssm_w_glu, v_ssm_b_glu, v_dn_conv_w, v_dn_a_log, v_dn_dt_bias, v_dn_norm_g,
                               v_sg_ln_g, v_sg_ln_b, v_sg_w, v_sg_b, v_w_out, v_ple_norm_g, v_w_ple_gate, v_w_ple, v_final_norm_g)))

    depth = p.shape[0]
    shards = [{n: w_loc[n][l:l + 1].astype(WIRE[n]) for n in SHARDED_ORDER} for l in range(depth)]
    first = _layer_weights(EARLY, _all_gather([shards[0][n] for n in EARLY], "gather_weights_l0"), 0)

    loss_part, grad_x, grads, left, by_chip = _local_step(
        x, p, w_loc, [first] + [None] * (depth - 1), loss_target, shards, reduce_early=True)

    rep_pack = _pack([grads[n] for n in REP_NARROW]).astype(WIRE['replicated'])
    fine_pack = _pack([grads['final_norm_g'], loss_part.reshape(1)])
    exchanged, (rep_recv, fine_recv) = _chip_exchange(list(left.values()), [rep_pack, fine_pack], "grads_chip_exchange_l0")
    by_chip.update(zip(left, exchanged))
    by_chip = [jnp.concatenate([by_chip[(l, n)] for l in range(depth)], axis=1) for n in SHARDED_ORDER]

    outs = {k: {} for k in 'gdmv'}
    for n, gk in zip(SHARDED_ORDER, by_chip):
        for k, o in zip('gdmv', _sum_adamw(gk, w_loc[n], m_loc[n], v_loc[n], f"adamw_{n}")):
            outs[k][n] = o
    rep_out = _sum_adamw(rep_recv, *[_pack([src[n] for n in REP_NARROW]) for src in (w_loc, m_loc, v_loc)], "adamw_replicated")
    for k, rep_p in zip('gdmv', rep_out):
        outs[k].update(zip(REP_NARROW, _unpack(rep_p, [w_loc[n].shape for n in REP_NARROW])))
    one = jnp.zeros((1,), F32)
    fine_out = _sum_adamw(fine_recv, *[_pack([src['final_norm_g'], one]) for src in (w_loc, m_loc, v_loc)], "adamw_final_norm")
    for k, fine_p in zip('gdmv', fine_out):
        outs[k].update(zip(['final_norm_g', 'loss'], _unpack(fine_p, [w_loc['final_norm_g'].shape, (1,)])))
    loss = outs['g']['loss'].reshape(())
    return (loss, grad_x, *[outs['g'][n] for n in WEIGHTS], *[outs['d'][n] for n in WEIGHTS],
            *[outs['m'][n] for n in WEIGHTS], *[outs['v'][n] for n in WEIGHTS])
```
